```python
import jax, jax.numpy as jnp
from jax import lax
import numpy as np

D_MODEL = 1024
BATCH = 8
SEQ = 8192
DEPTH = 1

D_MIX = D_MODEL
D_SGU = D_MIX // 2
D_CONV = D_MIX - D_SGU
HEAD_DIM = 64
N_SGU_HEADS = D_SGU // HEAD_DIM
N_CONV_GROUPS = D_CONV // HEAD_DIM
CHUNK = 128
CONV_WIDTH = 31
D_FF = -(-8 * D_MODEL // (3 * 256)) * 256
ALPHA = (2.0 * DEPTH) ** 0.25
BETA = (8.0 * DEPTH) ** -0.25
LN_EPS = 1e-5

kernel_name = "hybrid_sgu_conformer_deepnorm"


def layer_norm(x, g, b):
    xf = x.astype(jnp.float32)
    mu = jnp.mean(xf, axis=-1, keepdims=True)
    var = jnp.mean(jnp.square(xf - mu), axis=-1, keepdims=True)
    y = (xf - mu) * lax.rsqrt(var + LN_EPS)
    return (y * g.astype(jnp.float32) + b.astype(jnp.float32)).astype(x.dtype)


def spatial_gating(z, ln_g, ln_b, w_s, b_s):
    u, v = jnp.split(z, 2, axis=-1)
    v = layer_norm(v, ln_g, ln_b)
    bsz, seq, _ = v.shape
    n_chunks = seq // CHUNK
    v = v.reshape(bsz, n_chunks, CHUNK, N_SGU_HEADS, HEAD_DIM)
    causal = jnp.tril(jnp.ones((CHUNK, CHUNK), dtype=bool))
    w = jnp.where(causal, w_s, 0).astype(v.dtype)
    mixed = jnp.einsum('hts,bcshd->bcthd', w, v) + b_s.T[None, None, :, :, None]
    return u * mixed.reshape(bsz, seq, D_SGU)


def conv_module(a, g, conv_w, conv_b, ln_g, ln_b):
    h = a * jax.nn.sigmoid(g)
    y = lax.conv_general_dilated(
        h, conv_w[:, None, :].astype(h.dtype),
        window_strides=(1,), padding=[(CONV_WIDTH - 1, 0)],
        dimension_numbers=('NWC', 'WIO', 'NWC'),
        feature_group_count=D_CONV) + conv_b
    y = layer_norm(y, ln_g, ln_b)
    return jax.nn.silu(y)


def hybrid_layer(x, w_in, sgu_ln_g, sgu_ln_b, w_s, b_s, conv_w, conv_b, conv_ln_g, conv_ln_b,
                 w_out, ln1_g, ln1_b, w_gate, w_up, w_down, ln2_g, ln2_b):
    proj = jnp.einsum('bsd,de->bse', x, w_in)
    z_sgu = jax.nn.gelu(proj[..., :2 * D_SGU], approximate=False)
    a_conv, g_conv = jnp.split(proj[..., 2 * D_SGU:], 2, axis=-1)
    y = jnp.concatenate([
        spatial_gating(z_sgu, sgu_ln_g, sgu_ln_b, w_s, b_s),
        conv_module(a_conv, g_conv, conv_w, conv_b, conv_ln_g, conv_ln_b),
    ], axis=-1)
    x = layer_norm(ALPHA * x + jnp.einsum('bse,ed->bsd', y, w_out), ln1_g, ln1_b)
    h = jax.nn.silu(jnp.einsum('bsd,df->bsf', x, w_gate)) * jnp.einsum('bsd,df->bsf', x, w_up)
    x = layer_norm(ALPHA * x + jnp.einsum('bsf,fd->bsd', h, w_down), ln2_g, ln2_b)
    return x


def _fwd_setup_inputs(seed: int = 0) -> dict:
    key = jax.random.key(seed)
    ks = jax.random.split(key, 20)
    f32 = jnp.float32

    def nrm(k, shape, scale):
        return jax.random.normal(k, shape, f32) * scale

    L = DEPTH
    return {
        "x": nrm(ks[0], (BATCH, SEQ, D_MODEL), 1.0),
        "w_in": nrm(ks[1], (L, D_MODEL, 2 * D_SGU + 2 * D_CONV), D_MODEL ** -0.5),
        "sgu_ln_g": 1.0 + nrm(ks[2], (L, D_SGU), 0.02),
        "sgu_ln_b": nrm(ks[3], (L, D_SGU), 0.02),
        "w_s": nrm(ks[4], (L, N_SGU_HEADS, CHUNK, CHUNK), CHUNK ** -0.5),
        "b_s": 1.0 + nrm(ks[5], (L, N_SGU_HEADS, CHUNK), 0.02),
        "conv_w": nrm(ks[6], (L, CONV_WIDTH, D_CONV), CONV_WIDTH ** -0.5),
        "conv_b": nrm(ks[7], (L, D_CONV), 0.02),
        "conv_ln_g": 1.0 + nrm(ks[8], (L, D_CONV), 0.02),
        "conv_ln_b": nrm(ks[9], (L, D_CONV), 0.02),
        "w_out": nrm(ks[10], (L, D_MIX, D_MODEL), BETA * D_MIX ** -0.5),
        "ln1_g": 1.0 + nrm(ks[11], (L, D_MODEL), 0.02),
        "ln1_b": nrm(ks[12], (L, D_MODEL), 0.02),
        "w_gate": nrm(ks[13], (L, D_MODEL, D_FF), D_MODEL ** -0.5),
        "w_up": nrm(ks[14], (L, D_MODEL, D_FF), D_MODEL ** -0.5),
        "w_down": nrm(ks[15], (L, D_FF, D_MODEL), BETA * D_FF ** -0.5),
        "ln2_g": 1.0 + nrm(ks[16], (L, D_MODEL), 0.02),
        "ln2_b": nrm(ks[17], (L, D_MODEL), 0.02),
    }


def _fwd_reference(x, w_in, sgu_ln_g, sgu_ln_b, w_s, b_s, conv_w, conv_b, conv_ln_g, conv_ln_b,
              w_out, ln1_g, ln1_b, w_gate, w_up, w_down, ln2_g, ln2_b):
    for l in range(DEPTH):
        x = hybrid_layer(x, w_in[l], sgu_ln_g[l], sgu_ln_b[l], w_s[l], b_s[l], conv_w[l], conv_b[l],
                         conv_ln_g[l], conv_ln_b[l], w_out[l], ln1_g[l], ln1_b[l],
                         w_gate[l], w_up[l], w_down[l], ln2_g[l], ln2_b[l])
    return x


import jax as _jax
import jax.numpy as _jnp

TWIN_FORMAT = 'train_step'
FWD_PARAMS = ['x', 'w_in', 'sgu_ln_g', 'sgu_ln_b', 'w_s', 'b_s', 'conv_w', 'conv_b', 'conv_ln_g', 'conv_ln_b', 'w_out', 'ln1_g', 'ln1_b', 'w_gate', 'w_up', 'w_down', 'ln2_g', 'ln2_b']
TWIN_WEIGHTS = ['w_in', 'sgu_ln_g', 'sgu_ln_b', 'w_s', 'b_s', 'conv_w', 'conv_b', 'conv_ln_g', 'conv_ln_b', 'w_out', 'ln1_g', 'ln1_b', 'w_gate', 'w_up', 'w_down', 'ln2_g', 'ln2_b']
TWIN_DIFF_INPUT = 'x'
TWIN_INPUTS = ['x', 'w_in', 'sgu_ln_g', 'sgu_ln_b', 'w_s', 'b_s', 'conv_w', 'conv_b', 'conv_ln_g', 'conv_ln_b', 'w_out', 'ln1_g', 'ln1_b', 'w_gate', 'w_up', 'w_down', 'ln2_g', 'ln2_b', 'loss_target', 'm_w_in', 'm_sgu_ln_g', 'm_sgu_ln_b', 'm_w_s', 'm_b_s', 'm_conv_w', 'm_conv_b', 'm_conv_ln_g', 'm_conv_ln_b', 'm_w_out', 'm_ln1_g', 'm_ln1_b', 'm_w_gate', 'm_w_up', 'm_w_down', 'm_ln2_g', 'm_ln2_b', 'v_w_in', 'v_sgu_ln_g', 'v_sgu_ln_b', 'v_w_s', 'v_b_s', 'v_conv_w', 'v_conv_b', 'v_conv_ln_g', 'v_conv_ln_b', 'v_w_out', 'v_ln1_g', 'v_ln1_b', 'v_w_gate', 'v_w_up', 'v_w_down', 'v_ln2_g', 'v_ln2_b']
TWIN_OUTPUTS = ['loss', 'grad_x', 'grad_w_in', 'grad_sgu_ln_g', 'grad_sgu_ln_b', 'grad_w_s', 'grad_b_s', 'grad_conv_w', 'grad_conv_b', 'grad_conv_ln_g', 'grad_conv_ln_b', 'grad_w_out', 'grad_ln1_g', 'grad_ln1_b', 'grad_w_gate', 'grad_w_up', 'grad_w_down', 'grad_ln2_g', 'grad_ln2_b', 'delta_w_in', 'delta_sgu_ln_g', 'delta_sgu_ln_b', 'delta_w_s', 'delta_b_s', 'delta_conv_w', 'delta_conv_b', 'delta_conv_ln_g', 'delta_conv_ln_b', 'delta_w_out', 'delta_ln1_g', 'delta_ln1_b', 'delta_w_gate', 'delta_w_up', 'delta_w_down', 'delta_ln2_g', 'delta_ln2_b', 'new_m_w_in', 'new_m_sgu_ln_g', 'new_m_sgu_ln_b', 'new_m_w_s', 'new_m_b_s', 'new_m_conv_w', 'new_m_conv_b', 'new_m_conv_ln_g', 'new_m_conv_ln_b', 'new_m_w_out', 'new_m_ln1_g', 'new_m_ln1_b', 'new_m_w_gate', 'new_m_w_up', 'new_m_w_down', 'new_m_ln2_g', 'new_m_ln2_b', 'new_v_w_in', 'new_v_sgu_ln_g', 'new_v_sgu_ln_b', 'new_v_w_s', 'new_v_b_s', 'new_v_conv_w', 'new_v_conv_b', 'new_v_conv_ln_g', 'new_v_conv_ln_b', 'new_v_w_out', 'new_v_ln1_g', 'new_v_ln1_b', 'new_v_w_gate', 'new_v_w_up', 'new_v_w_down', 'new_v_ln2_g', 'new_v_ln2_b']
TWIN_LEAF_KINDS = {'loss': 'loss', 'grad_x': 'grad_x', 'grad_w_in': 'grad_w', 'grad_sgu_ln_g': 'grad_w', 'grad_sgu_ln_b': 'grad_w', 'grad_w_s': 'grad_w', 'grad_b_s': 'grad_w', 'grad_conv_w': 'grad_w', 'grad_conv_b': 'grad_w', 'grad_conv_ln_g': 'grad_w', 'grad_conv_ln_b': 'grad_w', 'grad_w_out': 'grad_w', 'grad_ln1_g': 'grad_w', 'grad_ln1_b': 'grad_w', 'grad_w_gate': 'grad_w', 'grad_w_up': 'grad_w', 'grad_w_down': 'grad_w', 'grad_ln2_g': 'grad_w', 'grad_ln2_b': 'grad_w', 'delta_w_in': 'delta_w', 'delta_sgu_ln_g': 'delta_w', 'delta_sgu_ln_b': 'delta_w', 'delta_w_s': 'delta_w', 'delta_b_s': 'delta_w', 'delta_conv_w': 'delta_w', 'delta_conv_b': 'delta_w', 'delta_conv_ln_g': 'delta_w', 'delta_conv_ln_b': 'delta_w', 'delta_w_out': 'delta_w', 'delta_ln1_g': 'delta_w', 'delta_ln1_b': 'delta_w', 'delta_w_gate': 'delta_w', 'delta_w_up': 'delta_w', 'delta_w_down': 'delta_w', 'delta_ln2_g': 'delta_w', 'delta_ln2_b': 'delta_w', 'new_m_w_in': 'new_m', 'new_m_sgu_ln_g': 'new_m', 'new_m_sgu_ln_b': 'new_m', 'new_m_w_s': 'new_m', 'new_m_b_s': 'new_m', 'new_m_conv_w': 'new_m', 'new_m_conv_b': 'new_m', 'new_m_conv_ln_g': 'new_m', 'new_m_conv_ln_b': 'new_m', 'new_m_w_out': 'new_m', 'new_m_ln1_g': 'new_m', 'new_m_ln1_b': 'new_m', 'new_m_w_gate': 'new_m', 'new_m_w_up': 'new_m', 'new_m_w_down': 'new_m', 'new_m_ln2_g': 'new_m', 'new_m_ln2_b': 'new_m', 'new_v_w_in': 'new_v', 'new_v_sgu_ln_g': 'new_v', 'new_v_sgu_ln_b': 'new_v', 'new_v_w_s': 'new_v', 'new_v_b_s': 'new_v', 'new_v_conv_w': 'new_v', 'new_v_conv_b': 'new_v', 'new_v_conv_ln_g': 'new_v', 'new_v_conv_ln_b': 'new_v', 'new_v_w_out': 'new_v', 'new_v_ln1_g': 'new_v', 'new_v_ln1_b': 'new_v', 'new_v_w_gate': 'new_v', 'new_v_w_up': 'new_v', 'new_v_w_down': 'new_v', 'new_v_ln2_g': 'new_v', 'new_v_ln2_b': 'new_v'}


def _forward(args):
    return _fwd_reference(*[args[k] for k in FWD_PARAMS])


def _output_shape():
    def fwd():
        inp = _fwd_setup_inputs(0)
        return _fwd_reference(*[inp[k] for k in FWD_PARAMS])
    out = _jax.eval_shape(fwd)
    return out.shape, out.dtype

N_MICROBATCH = 1
ADAM_LR = 0.001
ADAM_B1 = 0.9
ADAM_B2 = 0.999
ADAM_EPS = 1e-08
ADAM_WD = 0.01
ADAM_STEP = 10
PER_EXAMPLE_BATCH_AXIS = {'x': 0, 'loss_target': 0}
SHARED_INPUTS = []
_WEIGHT_DTYPES = {'w_in': _jnp.float32, 'sgu_ln_g': _jnp.float32, 'sgu_ln_b': _jnp.float32, 'w_s': _jnp.float32, 'b_s': _jnp.float32, 'conv_w': _jnp.float32, 'conv_b': _jnp.float32, 'conv_ln_g': _jnp.float32, 'conv_ln_b': _jnp.float32, 'w_out': _jnp.float32, 'ln1_g': _jnp.float32, 'ln1_b': _jnp.float32, 'w_gate': _jnp.float32, 'w_up': _jnp.float32, 'w_down': _jnp.float32, 'ln2_g': _jnp.float32, 'ln2_b': _jnp.float32}
MOMENT_SCALE = {'w_in': 7.328822e-02, 'sgu_ln_g': 5.906526e-02, 'sgu_ln_b': 6.077783e-02, 'w_s': 4.054482e-02, 'b_s': 5.669426e-02, 'conv_w': 8.224324e-02, 'conv_b': 3.709161e-01, 'conv_ln_g': 1.772962e-01, 'conv_ln_b': 2.295701e-01, 'w_out': 2.044147e-01, 'ln1_g': 2.020363e+00, 'ln1_b': 1.002080e+00, 'w_gate': 4.476860e-02, 'w_up': 4.361988e-02, 'w_down': 1.214731e-01, 'ln2_g': 6.409269e+01, 'ln2_b': 7.116559e+00}


def _to_microbatches(a, axis):
    t = _jnp.moveaxis(a, axis, 0)
    t = t.reshape((N_MICROBATCH, t.shape[0] // N_MICROBATCH) + t.shape[1:])
    return _jnp.moveaxis(t, 1, axis + 1)


def setup_inputs(seed: int = 0) -> dict:
    inp = _fwd_setup_inputs(seed)
    key = _jax.random.fold_in(_jax.random.key(seed), 7919)
    shape, _ = _output_shape()
    out = dict(inp)
    out["loss_target"] = _jax.random.normal(_jax.random.fold_in(key, 0), shape, _jnp.float32)
    for i, name in enumerate(TWIN_WEIGHTS):
        w = inp[name].astype(_jnp.float32)
        if MOMENT_SCALE is None:
            s = _jnp.sqrt(_jnp.mean(_jnp.square(w)) + 1e-30)
        else:
            s = MOMENT_SCALE[name]
        km, kv = _jax.random.split(_jax.random.fold_in(key, i + 1))
        out[name] = w
        out["m_" + name] = s * _jax.random.normal(km, w.shape, _jnp.float32)
        out["v_" + name] = (s * s) * _jax.random.uniform(kv, w.shape, _jnp.float32, 0.5, 1.5)
    if N_MICROBATCH > 1:
        for name, axis in PER_EXAMPLE_BATCH_AXIS.items():
            out[name] = _to_microbatches(out[name], axis)
    return {'x': out['x'], 'w_in': out['w_in'], 'sgu_ln_g': out['sgu_ln_g'], 'sgu_ln_b': out['sgu_ln_b'], 'w_s': out['w_s'], 'b_s': out['b_s'], 'conv_w': out['conv_w'], 'conv_b': out['conv_b'], 'conv_ln_g': out['conv_ln_g'], 'conv_ln_b': out['conv_ln_b'], 'w_out': out['w_out'], 'ln1_g': out['ln1_g'], 'ln1_b': out['ln1_b'], 'w_gate': out['w_gate'], 'w_up': out['w_up'], 'w_down': out['w_down'], 'ln2_g': out['ln2_g'], 'ln2_b': out['ln2_b'], 'loss_target': out['loss_target'], 'm_w_in': out['m_w_in'], 'm_sgu_ln_g': out['m_sgu_ln_g'], 'm_sgu_ln_b': out['m_sgu_ln_b'], 'm_w_s': out['m_w_s'], 'm_b_s': out['m_b_s'], 'm_conv_w': out['m_conv_w'], 'm_conv_b': out['m_conv_b'], 'm_conv_ln_g': out['m_conv_ln_g'], 'm_conv_ln_b': out['m_conv_ln_b'], 'm_w_out': out['m_w_out'], 'm_ln1_g': out['m_ln1_g'], 'm_ln1_b': out['m_ln1_b'], 'm_w_gate': out['m_w_gate'], 'm_w_up': out['m_w_up'], 'm_w_down': out['m_w_down'], 'm_ln2_g': out['m_ln2_g'], 'm_ln2_b': out['m_ln2_b'], 'v_w_in': out['v_w_in'], 'v_sgu_ln_g': out['v_sgu_ln_g'], 'v_sgu_ln_b': out['v_sgu_ln_b'], 'v_w_s': out['v_w_s'], 'v_b_s': out['v_b_s'], 'v_conv_w': out['v_conv_w'], 'v_conv_b': out['v_conv_b'], 'v_conv_ln_g': out['v_conv_ln_g'], 'v_conv_ln_b': out['v_conv_ln_b'], 'v_w_out': out['v_w_out'], 'v_ln1_g': out['v_ln1_g'], 'v_ln1_b': out['v_ln1_b'], 'v_w_gate': out['v_w_gate'], 'v_w_up': out['v_w_up'], 'v_w_down': out['v_w_down'], 'v_ln2_g': out['v_ln2_g'], 'v_ln2_b': out['v_ln2_b']}


def _loss(weights, diff, rest, loss_target):
    with _jax.named_scope("forward"):
        args = {**rest, TWIN_DIFF_INPUT: diff, **{k: w.astype(_WEIGHT_DTYPES[k]) for k, w in weights.items()}}
        y = _forward(args)
    with _jax.named_scope("loss_head"):
        err = _jnp.square(y.astype(_jnp.float32) - loss_target)
        return 0.5 * _jnp.sum(_jnp.mean(err, axis=-1)) if err.ndim else 0.5 * err


def _adamw(w, g, m, v):
    m = ADAM_B1 * m + (1.0 - ADAM_B1) * g
    v = ADAM_B2 * v + (1.0 - ADAM_B2) * _jnp.square(g)
    m_hat = m / (1.0 - ADAM_B1 ** ADAM_STEP)
    v_hat = v / (1.0 - ADAM_B2 ** ADAM_STEP)
    delta = -ADAM_LR * (m_hat / (_jnp.sqrt(v_hat) + ADAM_EPS) + ADAM_WD * w)
    return delta, m, v


def reference(x, w_in, sgu_ln_g, sgu_ln_b, w_s, b_s, conv_w, conv_b, conv_ln_g, conv_ln_b, w_out, ln1_g, ln1_b, w_gate, w_up, w_down, ln2_g, ln2_b, loss_target, m_w_in, m_sgu_ln_g, m_sgu_ln_b, m_w_s, m_b_s, m_conv_w, m_conv_b, m_conv_ln_g, m_conv_ln_b, m_w_out, m_ln1_g, m_ln1_b, m_w_gate, m_w_up, m_w_down, m_ln2_g, m_ln2_b, v_w_in, v_sgu_ln_g, v_sgu_ln_b, v_w_s, v_b_s, v_conv_w, v_conv_b, v_conv_ln_g, v_conv_ln_b, v_w_out, v_ln1_g, v_ln1_b, v_w_gate, v_w_up, v_w_down, v_ln2_g, v_ln2_b):
    given = dict(x=x, w_in=w_in, sgu_ln_g=sgu_ln_g, sgu_ln_b=sgu_ln_b, w_s=w_s, b_s=b_s, conv_w=conv_w, conv_b=conv_b, conv_ln_g=conv_ln_g, conv_ln_b=conv_ln_b, w_out=w_out, ln1_g=ln1_g, ln1_b=ln1_b, w_gate=w_gate, w_up=w_up, w_down=w_down, ln2_g=ln2_g, ln2_b=ln2_b, loss_target=loss_target, m_w_in=m_w_in, m_sgu_ln_g=m_sgu_ln_g, m_sgu_ln_b=m_sgu_ln_b, m_w_s=m_w_s, m_b_s=m_b_s, m_conv_w=m_conv_w, m_conv_b=m_conv_b, m_conv_ln_g=m_conv_ln_g, m_conv_ln_b=m_conv_ln_b, m_w_out=m_w_out, m_ln1_g=m_ln1_g, m_ln1_b=m_ln1_b, m_w_gate=m_w_gate, m_w_up=m_w_up, m_w_down=m_w_down, m_ln2_g=m_ln2_g, m_ln2_b=m_ln2_b, v_w_in=v_w_in, v_sgu_ln_g=v_sgu_ln_g, v_sgu_ln_b=v_sgu_ln_b, v_w_s=v_w_s, v_b_s=v_b_s, v_conv_w=v_conv_w, v_conv_b=v_conv_b, v_conv_ln_g=v_conv_ln_g, v_conv_ln_b=v_conv_ln_b, v_w_out=v_w_out, v_ln1_g=v_ln1_g, v_ln1_b=v_ln1_b, v_w_gate=v_w_gate, v_w_up=v_w_up, v_w_down=v_w_down, v_ln2_g=v_ln2_g, v_ln2_b=v_ln2_b)
    weights = {n: given[n] for n in TWIN_WEIGHTS}
    shared = {n: given[n] for n in SHARED_INPUTS}
    per_example = {n: given[n] for n in ['x']}
    grad_fn = _jax.value_and_grad(_loss, argnums=(0, 1))

    def one_microbatch(ex, loss_target):
        ex = dict(ex)
        diff = ex.pop(TWIN_DIFF_INPUT)
        return grad_fn(weights, diff, {**shared, **ex}, loss_target)

    if N_MICROBATCH == 1:
        loss, (grad_w, grad_x) = one_microbatch(per_example, given["loss_target"])
    else:
        def body(carry, xs):
            loss_sum, grad_sum = carry
            l_k, (gw_k, gx_k) = one_microbatch(xs[0], xs[1])
            with _jax.named_scope("update"):
                return (loss_sum + l_k, _jax.tree.map(_jnp.add, grad_sum, gw_k)), gx_k

        init = (_jnp.zeros((), _jnp.float32), _jax.tree.map(_jnp.zeros_like, weights))
        (loss, grad_w), grad_x = _jax.lax.scan(body, init, (per_example, given["loss_target"]))
    with _jax.named_scope("update"):
        delta_w, new_m, new_v = {}, {}, {}
        for n in TWIN_WEIGHTS:
            delta_w[n], new_m[n], new_v[n] = _adamw(weights[n], grad_w[n], given["m_" + n], given["v_" + n])
    return (loss, grad_x, *[grad_w[n] for n in TWIN_WEIGHTS], *[delta_w[n] for n in TWIN_WEIGHTS],
            *[new_m[n] for n in TWIN_WEIGHTS], *[new_v[n] for n in TWIN_WEIGHTS])
```

```python
import functools
import math

import jax
import jax.numpy as jnp
from jax import lax
from jax.experimental import pallas as pl
from jax.experimental.pallas import tpu as pltpu

F32 = jnp.float32
BF16 = jnp.bfloat16

LN_EPS = 1e-5
HEAD_DIM = 64
CHUNK = 128
HALO = 32
LANES = 128
MXU_N = 256
ADAM_LR, ADAM_B1, ADAM_B2, ADAM_EPS, ADAM_WD, ADAM_STEP = 0.001, 0.9, 0.999, 1e-08, 0.01, 10
VMEM_LIMIT = 60 * 1024 * 1024
MESH_AXES = ("x", "y", "c")
MESH_ID = pl.DeviceIdType.MESH


def _dot(a, b):
    return jnp.dot(a, b, preferred_element_type=F32)


def _dot_nt(a, b):
    return lax.dot_general(a, b, (((1,), (1,)), ((), ())), preferred_element_type=F32)


def _dot_tn(a, b):
    return lax.dot_general(a, b, (((0,), (0,)), ((), ())), preferred_element_type=F32)


def _sigmoid(v):
    return 1.0 / (1.0 + jnp.exp(-v))


def _gelu(v):
    cdf = 0.5 * (1.0 + lax.erf(v * (1.0 / math.sqrt(2.0))))
    pdf = jnp.exp(-0.5 * v * v) * (1.0 / math.sqrt(2.0 * math.pi))
    return v * cdf, cdf + v * pdf


def _ln_stats(v):
    mu = jnp.mean(v, axis=-1, keepdims=True)
    d = v - mu
    rstd = lax.rsqrt(jnp.mean(d * d, axis=-1, keepdims=True) + LN_EPS)
    return d * rstd, rstd


def _ln_bwd(dxhat, xhat, rstd):
    m1 = jnp.mean(dxhat, axis=-1, keepdims=True)
    m2 = jnp.mean(dxhat * xhat, axis=-1, keepdims=True)
    return rstd * (dxhat - m1 - xhat * m2)


def _colsum(v):
    return jnp.sum(v, axis=0, keepdims=True)


def _pair_lanes(v, nc, p):
    return jnp.concatenate([v[c * CHUNK:(c + 1) * CHUNK, p * LANES:(p + 1) * LANES] for c in range(nc)], axis=1)


def _unpair(parts, nc):
    rows = [jnp.concatenate([part[:, c * LANES:(c + 1) * LANES] for part in parts], axis=1) for c in range(nc)]
    return jnp.concatenate(rows, axis=0)


def _low_head(nc):
    lane = lax.broadcasted_iota(jnp.int32, (CHUNK, nc * LANES), 1)
    return (lane & (LANES - 1)) < HEAD_DIM


def _mix(wst_ref, v, nc, n_pairs):
    vb = v.astype(BF16)
    low = _low_head(nc)
    parts = []
    for p in range(n_pairs):
        r = _dot(wst_ref[p], _pair_lanes(vb, nc, p))
        parts.append(jnp.where(low, r[:CHUNK], r[CHUNK:]))
    return _unpair(parts, nc)


def _mix_wgrad(dm, vn, nc, n_pairs):
    low = _low_head(nc)
    vb = vn.astype(BF16)
    out = []
    for p in range(n_pairs):
        a = _pair_lanes(dm, nc, p)
        lhs = jnp.concatenate([jnp.where(low, a, 0.0), jnp.where(low, 0.0, a)], axis=0).astype(BF16)
        out.append(_dot_nt(lhs, _pair_lanes(vb, nc, p)))
    return out


def _conv_taps(buf_ref, w_ref, tm, q, off, kw):
    rb = min(tm, 128)
    cols = []
    for l0 in range(0, q, LANES):
        rows = []
        for r0 in range(0, tm, rb):
            acc = None
            for k in range(kw):
                t = w_ref[k:k + 1, l0:l0 + LANES] * buf_ref[off + k + r0:off + k + r0 + rb, l0:l0 + LANES]
                acc = t if acc is None else acc + t
            rows.append(acc)
        cols.append(jnp.concatenate(rows, axis=0))
    return jnp.concatenate(cols, axis=1)


def _cparams():
    return pltpu.CompilerParams(dimension_semantics=("arbitrary",), vmem_limit_bytes=VMEM_LIMIT)


def _full(shape):
    return pl.BlockSpec(shape, lambda i: (0,) * len(shape))


ANY = pl.BlockSpec(memory_space=pl.ANY)

VQ_SGU_G, VQ_SGU_B, VQ_CONV_B, VQ_CLN_G, VQ_CLN_B = range(5)
VD_LN1_G, VD_LN1_B, VD_LN2_G, VD_LN2_B = range(4)


def _fwd_mix(x, wi, wo, wst, bmat, cw, vq, vd, alpha, kw, tm):
    t, d = x.shape
    q = wi.shape[2]
    nc, n_pairs = tm // CHUNK, q // LANES
    n = t // tm

    def body(x_ref, wi_hbm, wo_hbm, wst_ref, bmat_ref, cw_ref, vq_ref, vd_ref,
             proj_ref, y_ref, xh_ref, rstd_ref, wi_v, wo_v, hb_ref):
        @pl.when(pl.program_id(0) == 0)
        def _():
            pltpu.sync_copy(wi_hbm, wi_v)
            pltpu.sync_copy(wo_hbm, wo_v)
            hb_ref[0:HALO, :] = jnp.zeros((HALO, q), F32)

        xv = x_ref[...]
        xb = xv.astype(BF16)
        pu, pv, pa, pg = (_dot(xb, wi_v[j]) for j in range(4))
        for j, pj in enumerate((pu, pv, pa, pg)):
            proj_ref[:, j * q:(j + 1) * q] = pj.astype(BF16)
        zu, _ = _gelu(pu)
        zv, _ = _gelu(pv)
        vhat, _ = _ln_stats(zv)
        vn = vhat * vq_ref[VQ_SGU_G:VQ_SGU_G + 1, :] + vq_ref[VQ_SGU_B:VQ_SGU_B + 1, :]
        mixed = _mix(wst_ref, vn, nc, n_pairs) + jnp.concatenate([bmat_ref[...]] * nc, axis=0)
        y_ref[:, 0:q] = (zu * mixed).astype(BF16)

        hb_ref[HALO:HALO + tm, :] = pa * _sigmoid(pg)
        yc = _conv_taps(hb_ref, cw_ref, tm, q, HALO - (kw - 1), kw) + vq_ref[VQ_CONV_B:VQ_CONV_B + 1, :]
        hb_ref[0:HALO, :] = hb_ref[tm:tm + HALO, :]
        yhat, _ = _ln_stats(yc)
        yn = yhat * vq_ref[VQ_CLN_G:VQ_CLN_G + 1, :] + vq_ref[VQ_CLN_B:VQ_CLN_B + 1, :]
        y_ref[:, q:2 * q] = (yn * _sigmoid(yn)).astype(BF16)

        r1 = alpha * xv + _dot(y_ref[...], wo_v[...])
        xhat, rstd = _ln_stats(r1)
        xh_ref[...] = xhat
        rstd_ref[...] = rstd

    row = lambda w: pl.BlockSpec((tm, w), lambda i: (i, 0))
    return pl.pallas_call(
        body, name="fwd_mix", grid=(n,),
        in_specs=[row(d), ANY, ANY, _full(wst.shape), _full(bmat.shape), _full(cw.shape), _full(vq.shape), _full(vd.shape)],
        out_specs=[row(4 * q), row(d), row(d), row(1)],
        out_shape=[jax.ShapeDtypeStruct((t, 4 * q), BF16), jax.ShapeDtypeStruct((t, d), BF16),
                   jax.ShapeDtypeStruct((t, d), F32), jax.ShapeDtypeStruct((t, 1), F32)],
        scratch_shapes=[pltpu.VMEM(wi.shape, BF16), pltpu.VMEM(wo.shape, BF16), pltpu.VMEM((HALO + tm, q), F32)],
        compiler_params=_cparams(),
    )(x, wi, wo, wst, bmat, cw, vq, vd)


def _fwd_mlp(xh1, tgt, wg, wu, wd, vd, alpha, tm):
    t, d = xh1.shape
    ns, _, fp = wg.shape
    n = t // tm

    def body(xh_ref, tgt_ref, wg_hbm, wu_hbm, wd_hbm, vd_ref,
             gp_ref, up_ref, x1b_ref, dr2_ref, loss_ref, dg2_ref, db2_ref, wg_v, wu_v, wd_v):
        @pl.when(pl.program_id(0) == 0)
        def _():
            pltpu.sync_copy(wg_hbm, wg_v)
            pltpu.sync_copy(wu_hbm, wu_v)
            pltpu.sync_copy(wd_hbm, wd_v)
            loss_ref[...] = jnp.zeros_like(loss_ref)
            dg2_ref[...] = jnp.zeros_like(dg2_ref)
            db2_ref[...] = jnp.zeros_like(db2_ref)

        x1 = xh_ref[...] * vd_ref[VD_LN1_G:VD_LN1_G + 1, :] + vd_ref[VD_LN1_B:VD_LN1_B + 1, :]
        x1b = x1.astype(BF16)
        x1b_ref[...] = x1b
        acc = alpha * x1
        for k in range(ns):
            gp = _dot(x1b, wg_v[k])
            up = _dot(x1b, wu_v[k])
            gp_ref[:, k * fp:(k + 1) * fp] = gp.astype(BF16)
            up_ref[:, k * fp:(k + 1) * fp] = up.astype(BF16)
            acc = acc + _dot((gp * _sigmoid(gp) * up).astype(BF16), wd_v[k])
        xh2, rstd2 = _ln_stats(acc)
        g2 = vd_ref[VD_LN2_G:VD_LN2_G + 1, :]
        err = xh2 * g2 + vd_ref[VD_LN2_B:VD_LN2_B + 1, :] - tgt_ref[...]
        loss_ref[...] += _colsum(jnp.sum(err * err, axis=1, keepdims=True)) * (0.5 / d)
        dy = err * (1.0 / d)
        dg2_ref[...] += _colsum(dy * xh2)
        db2_ref[...] += _colsum(dy)
        dr2_ref[...] = _ln_bwd(dy * g2, xh2, rstd2)

    row = lambda w: pl.BlockSpec((tm, w), lambda i: (i, 0))
    return pl.pallas_call(
        body, name="fwd_mlp", grid=(n,),
        in_specs=[row(d), row(d), ANY, ANY, ANY, _full(vd.shape)],
        out_specs=[row(ns * fp), row(ns * fp), row(d), row(d), _full((8, LANES)), _full((1, d)), _full((1, d))],
        out_shape=[jax.ShapeDtypeStruct((t, ns * fp), BF16), jax.ShapeDtypeStruct((t, ns * fp), BF16),
                   jax.ShapeDtypeStruct((t, d), BF16), jax.ShapeDtypeStruct((t, d), F32),
                   jax.ShapeDtypeStruct((8, LANES), F32), jax.ShapeDtypeStruct((1, d), F32), jax.ShapeDtypeStruct((1, d), F32)],
        scratch_shapes=[pltpu.VMEM(wg.shape, BF16), pltpu.VMEM(wu.shape, BF16), pltpu.VMEM(wd.shape, BF16)],
        compiler_params=_cparams(),
    )(xh1, tgt, wg, wu, wd, vd)


def _bwd_mlp_slab(k, dr2, dx_prev, x1b, gp, up, wg, wu, wd, alpha, tm):
    t, d = dr2.shape
    fp = wg.shape[2]
    n = t // tm
    first = dx_prev is None

    def body(*refs):
        if first:
            dr_ref, x1b_ref, gp_ref, up_ref, wg_ref, wu_ref, wd_ref, dx_ref, dwg_hbm, dwu_hbm, dwd_hbm, ag, au, ad = refs
        else:
            dr_ref, dxp_ref, x1b_ref, gp_ref, up_ref, wg_ref, wu_ref, wd_ref, dx_ref, dwg_hbm, dwu_hbm, dwd_hbm, ag, au, ad = refs

        @pl.when(pl.program_id(0) == 0)
        def _():
            ag[...] = jnp.zeros_like(ag)
            au[...] = jnp.zeros_like(au)
            ad[...] = jnp.zeros_like(ad)

        dr = dr_ref[...]
        drb = dr.astype(BF16)
        x1b = x1b_ref[...]
        gpv = gp_ref[...].astype(F32)
        upv = up_ref[...].astype(F32)
        dh = _dot_nt(drb, wd_ref[0])
        sg = _sigmoid(gpv)
        silu = gpv * sg
        ad[...] += _dot_tn((silu * upv).astype(BF16), drb)
        dgp = (dh * upv * (sg * (1.0 + gpv * (1.0 - sg)))).astype(BF16)
        dup = (dh * silu).astype(BF16)
        ag[...] += _dot_tn(x1b, dgp)
        au[...] += _dot_tn(x1b, dup)
        base = alpha * dr if first else dxp_ref[...]
        dx_ref[...] = base + _dot_nt(dgp, wg_ref[0]) + _dot_nt(dup, wu_ref[0])

        @pl.when(pl.program_id(0) == n - 1)
        def _():
            pltpu.sync_copy(ag, dwg_hbm)
            pltpu.sync_copy(au, dwu_hbm)
            pltpu.sync_copy(ad, dwd_hbm)

    row = lambda w: pl.BlockSpec((tm, w), lambda i: (i, 0))
    slab = pl.BlockSpec((tm, fp), lambda i: (i, k))
    wcol = pl.BlockSpec((1, d, fp), lambda i: (k, 0, 0))
    wrow = pl.BlockSpec((1, fp, d), lambda i: (k, 0, 0))
    ins = [dr2] + ([] if first else [dx_prev]) + [x1b, gp, up, wg, wu, wd]
    in_specs = [row(d)] + ([] if first else [row(d)]) + [row(d), slab, slab, wcol, wcol, wrow]
    return pl.pallas_call(
        body, name=f"bwd_mlp_{k}", grid=(n,),
        in_specs=in_specs,
        out_specs=[row(d), ANY, ANY, ANY],
        out_shape=[jax.ShapeDtypeStruct((t, d), F32), jax.ShapeDtypeStruct((d, fp), F32),
                   jax.ShapeDtypeStruct((d, fp), F32), jax.ShapeDtypeStruct((fp, d), F32)],
        scratch_shapes=[pltpu.VMEM((d, fp), F32), pltpu.VMEM((d, fp), F32), pltpu.VMEM((fp, d), F32)],
        compiler_params=_cparams(),
    )(*ins)


def _bwd_mix(dx1, xh1, rstd1, x, proj, y, wi, wo, wst, wstt, bmat, cw, cwf, vq, vd, alpha, kw, tm):
    t, d = x.shape
    q = wi.shape[2]
    nc, n_pairs = tm // CHUNK, q // LANES
    n = t // tm
    hb_per_tile = tm // HALO

    def body(dx1_ref, xh_ref, rstd_ref, x_ref, proj_ref, halo_ref, y_ref, wi_hbm, wo_hbm, wst_ref, wstt_ref, bmat_ref,
             cw_ref, cwf_ref, vq_ref, vd_ref,
             gx_ref, dwi_hbm, dwo_hbm, dws_ref, dbs_ref, dcw_ref, dvq_ref, dvd_ref,
             wi_v, wo_v, awi, awo, hb_ref, dyb_ref, dbm_ref):
        i = pl.program_id(0)

        @pl.when(i == 0)
        def _():
            pltpu.sync_copy(wi_hbm, wi_v)
            pltpu.sync_copy(wo_hbm, wo_v)
            awi[...] = jnp.zeros_like(awi)
            awo[...] = jnp.zeros_like(awo)
            for r in (dws_ref, dbm_ref, dcw_ref, dvq_ref, dvd_ref):
                r[...] = jnp.zeros_like(r)
            dyb_ref[tm:tm + HALO, :] = jnp.zeros((HALO, q), F32)

        dx1v = dx1_ref[...]
        xh = xh_ref[...]
        dvd_ref[VD_LN1_G:VD_LN1_G + 1, :] += _colsum(dx1v * xh)
        dvd_ref[VD_LN1_B:VD_LN1_B + 1, :] += _colsum(dx1v)
        dr1 = _ln_bwd(dx1v * vd_ref[VD_LN1_G:VD_LN1_G + 1, :], xh, rstd_ref[...])
        dr1b = dr1.astype(BF16)
        awo[...] += _dot_tn(y_ref[...], dr1b)
        dy = _dot_nt(dr1b, wo_v[...])

        pu = proj_ref[:, 0:q].astype(F32)
        pv = proj_ref[:, q:2 * q].astype(F32)
        zu, gu = _gelu(pu)
        zv, gv = _gelu(pv)
        vhat, rstd_v = _ln_stats(zv)
        sgu_g = vq_ref[VQ_SGU_G:VQ_SGU_G + 1, :]
        vn = vhat * sgu_g + vq_ref[VQ_SGU_B:VQ_SGU_B + 1, :]
        mixed = _mix(wst_ref, vn, nc, n_pairs) + jnp.concatenate([bmat_ref[...]] * nc, axis=0)
        doa = dy[:, 0:q]
        dm = doa * zu
        dpu = (doa * mixed * gu).astype(BF16)
        acc = dm[0:CHUNK]
        for c in range(1, nc):
            acc = acc + dm[c * CHUNK:(c + 1) * CHUNK]
        dbm_ref[...] += acc
        for p, g in enumerate(_mix_wgrad(dm, vn, nc, n_pairs)):
            dws_ref[p] += g
        dvn = _mix(wstt_ref, dm, nc, n_pairs)
        dvq_ref[VQ_SGU_G:VQ_SGU_G + 1, :] += _colsum(dvn * vhat)
        dvq_ref[VQ_SGU_B:VQ_SGU_B + 1, :] += _colsum(dvn)
        dpv = (_ln_bwd(dvn * sgu_g, vhat, rstd_v) * gv).astype(BF16)

        pa = proj_ref[:, 2 * q:3 * q].astype(F32)
        pg = proj_ref[:, 3 * q:4 * q].astype(F32)
        sg = _sigmoid(pg)
        hb_ref[HALO:HALO + tm, :] = pa * sg
        ha = halo_ref[:, 0:q].astype(F32)
        hg = halo_ref[:, q:2 * q].astype(F32)
        hb_ref[0:HALO, :] = jnp.where(i == n - 1, 0.0, ha * _sigmoid(hg))
        yc = _conv_taps(hb_ref, cw_ref, tm, q, HALO - (kw - 1), kw) + vq_ref[VQ_CONV_B:VQ_CONV_B + 1, :]
        yhat, rstd_c = _ln_stats(yc)
        cln_g = vq_ref[VQ_CLN_G:VQ_CLN_G + 1, :]
        yn = yhat * cln_g + vq_ref[VQ_CLN_B:VQ_CLN_B + 1, :]
        sy = _sigmoid(yn)
        dyn = dy[:, q:2 * q] * (sy * (1.0 + yn * (1.0 - sy)))
        dvq_ref[VQ_CLN_G:VQ_CLN_G + 1, :] += _colsum(dyn * yhat)
        dvq_ref[VQ_CLN_B:VQ_CLN_B + 1, :] += _colsum(dyn)
        dyc = _ln_bwd(dyn * cln_g, yhat, rstd_c)
        dvq_ref[VQ_CONV_B:VQ_CONV_B + 1, :] += _colsum(dyc)
        dyb_ref[0:tm, :] = dyc
        off = HALO - (kw - 1)
        for k in range(kw):
            dcw_ref[k:k + 1, :] += _colsum(dyb_ref[0:tm, :] * hb_ref[off + k:off + k + tm, :])
        dh = _conv_taps(dyb_ref, cwf_ref, tm, q, 0, kw)
        dyb_ref[tm:tm + HALO, :] = dyb_ref[0:HALO, :]
        da = (dh * sg).astype(BF16)
        dg = (dh * pa * (sg * (1.0 - sg))).astype(BF16)

        xb = x_ref[...].astype(BF16)
        gx = alpha * dr1
        for j, dpj in enumerate((dpu, dpv, da, dg)):
            awi[j] += _dot_tn(xb, dpj)
            gx = gx + _dot_nt(dpj, wi_v[j])
        gx_ref[...] = gx

        @pl.when(i == n - 1)
        def _():
            pltpu.sync_copy(awi, dwi_hbm)
            pltpu.sync_copy(awo, dwo_hbm)
            lane = lax.broadcasted_iota(jnp.int32, (CHUNK, LANES), 1)
            low = lane < HEAD_DIM
            dbs = jnp.zeros((CHUNK, LANES), F32)
            for p in range(n_pairs):
                grp = dbm_ref[:, p * LANES:(p + 1) * LANES]
                dbs = jnp.where(lane == 2 * p, jnp.sum(jnp.where(low, grp, 0.0), axis=1, keepdims=True), dbs)
                dbs = jnp.where(lane == 2 * p + 1, jnp.sum(jnp.where(low, 0.0, grp), axis=1, keepdims=True), dbs)
            dbs_ref[...] = dbs

    rev =lambda w: pl.BlockSpec((tm, w), lambda i: (n - 1 - i, 0))
    halo = pl.BlockSpec((HALO, 2 * q), lambda i: (jnp.maximum((n - 1 - i) * hb_per_tile - 1, 0), 1))
    small = [jax.ShapeDtypeStruct((n_pairs, 2 * CHUNK, CHUNK), F32), jax.ShapeDtypeStruct((CHUNK, LANES), F32),
             jax.ShapeDtypeStruct(cw.shape, F32), jax.ShapeDtypeStruct(vq.shape, F32), jax.ShapeDtypeStruct(vd.shape, F32)]
    return pl.pallas_call(
        body, name="bwd_mix", grid=(n,),
        in_specs=[rev(d), rev(d), rev(1), rev(d), rev(4 * q), halo, rev(d), ANY, ANY, _full(wst.shape), _full(wstt.shape),
                  _full(bmat.shape), _full(cw.shape), _full(cwf.shape), _full(vq.shape), _full(vd.shape)],
        out_specs=[rev(d), ANY, ANY] + [_full(s.shape) for s in small],
        out_shape=[jax.ShapeDtypeStruct((t, d), F32), jax.ShapeDtypeStruct(wi.shape, F32), jax.ShapeDtypeStruct(wo.shape, F32)] + small,
        scratch_shapes=[pltpu.VMEM(wi.shape, BF16), pltpu.VMEM(wo.shape, BF16), pltpu.VMEM(wi.shape, F32), pltpu.VMEM(wo.shape, F32),
                        pltpu.VMEM((HALO + tm, q), F32), pltpu.VMEM((tm + HALO, q), F32), pltpu.VMEM((CHUNK, q), F32)],
        compiler_params=_cparams(),
    )(dx1, xh1, rstd1, x, proj, proj, y, wi, wo, wst, wstt, bmat, cw, cwf, vq, vd)


def _prep(w_in, w_out, w_gate, w_up, w_down, fp):
    d, fs = w_gate.shape

    def body(wi_ref, wo_ref, wg_ref, wu_ref, wd_ref, oi, oo, og, ou, od):
        oi[...] = wi_ref[...].astype(BF16)
        oo[...] = wo_ref[...].astype(BF16)
        for src, dst in ((wg_ref, og), (wu_ref, ou)):
            dst[:, 0:fs] = src[...].astype(BF16)
            if fp > fs:
                dst[:, fs:fp] = jnp.zeros((d, fp - fs), BF16)
        od[0:fs, :] = wd_ref[...].astype(BF16)
        if fp > fs:
            od[fs:fp, :] = jnp.zeros((fp - fs, d), BF16)

    return pl.pallas_call(
        body, name="wprep",
        out_shape=[jax.ShapeDtypeStruct(w_in.shape, BF16), jax.ShapeDtypeStruct(w_out.shape, BF16),
                   jax.ShapeDtypeStruct((d, fp), BF16), jax.ShapeDtypeStruct((d, fp), BF16), jax.ShapeDtypeStruct((fp, d), BF16)],
        compiler_params=pltpu.CompilerParams(vmem_limit_bytes=VMEM_LIMIT),
    )(w_in, w_out, w_gate, w_up, w_down)


def _coords():
    return tuple(lax.axis_index(a) for a in MESH_AXES)


def _other_chips(x, y):
    return [(1 - x, y), (x, 1 - y), (1 - x, 1 - y)]


def _remote(src, dst, send_sem, recv_sem, to):
    return pltpu.make_async_remote_copy(src_ref=src, dst_ref=dst, send_sem=send_sem, recv_sem=recv_sem,
                                        device_id=to, device_id_type=MESH_ID)


def _hbm_call(body, name, ins, out_shape, scratch_shapes):
    return pl.pallas_call(
        body, name=name, in_specs=[ANY] * len(ins), out_specs=[ANY] * len(out_shape), out_shape=out_shape,
        scratch_shapes=scratch_shapes,
    )(*ins)


def _gather_shards(shards):
    n = len(shards)

    def body(*refs):
        src, dst = refs[:n], refs[n:2 * n]
        send_sems, recv_sems, local_sems = refs[2 * n:]
        x, y, c = _coords()
        me, sib = 2 * x + y, (x, y, 1 - c)
        chips = _other_chips(x, y)

        def half(a, slot, hc):
            hr = shards[a].shape[0] // 2
            return dst[a].at[slot, pl.ds(hc * hr, hr)]

        local, sends = [], []
        for a in range(n):
            hr = shards[a].shape[0] // 2
            local.append(pltpu.make_async_copy(src[a], dst[a].at[me], local_sems.at[a]))
            local[-1].start()
            for j, chip in enumerate(chips):
                sends.append(_remote(src[a].at[pl.ds(c * hr, hr)], half(a, me, c), send_sems.at[a, j], recv_sems.at[a, j], (*chip, c)))
                sends[-1].start()
        for a in range(n):
            for j, chip in enumerate(chips):
                landed = half(a, 2 * chip[0] + chip[1], c)
                _remote(landed, landed, send_sems.at[a, j], recv_sems.at[a, j], (*chip, c)).wait_recv()
                sends.append(_remote(landed, landed, send_sems.at[a, 3 + j], recv_sems.at[a, 3 + j], sib))
                sends[-1].start()
        for a in range(n):
            for j, chip in enumerate(chips):
                other = half(a, 2 * chip[0] + chip[1], 1 - c)
                _remote(other, other, send_sems.at[a, 3 + j], recv_sems.at[a, 3 + j], sib).wait_recv()
        for cp in sends:
            cp.wait_send()
        for cp in local:
            cp.wait()

    return _hbm_call(body, "gather_shards", shards, [jax.ShapeDtypeStruct((4,) + s.shape, s.dtype) for s in shards],
                     [pltpu.SemaphoreType.DMA((n, 6)), pltpu.SemaphoreType.DMA((n, 6)), pltpu.SemaphoreType.DMA((n,))])


def _pair_swap(arrs):
    n = len(arrs)

    def body(*refs):
        src, land = refs[:n], refs[n:2 * n]
        send_sems, recv_sems = refs[2 * n:]
        x, y, c = _coords()
        copies = []
        for a in range(n):
            s = src[a].at[pl.ds(0, arrs[a].shape[0]), 1 - c] if arrs[a].ndim == 4 else src[a].at[1 - c]
            copies.append(_remote(s, land[a], send_sems.at[a], recv_sems.at[a], (x, y, 1 - c)))
            copies[-1].start()
        for cp in copies:
            cp.wait()

    outs = [jax.ShapeDtypeStruct(s.shape[:-3] + s.shape[-2:], s.dtype) for s in arrs]
    return _hbm_call(body, "pair_swap", arrs, outs, [pltpu.SemaphoreType.DMA((n,)), pltpu.SemaphoreType.DMA((n,))])


def _chip_exchange(arrs):
    n = len(arrs)

    def body(*refs):
        src, dst = refs[:n], refs[n:2 * n]
        send_sems, recv_sems, local_sems = refs[2 * n:]
        x, y, c = _coords()
        me = 2 * x + y
        chips = _other_chips(x, y)
        part = lambda a, slot: src[a].at[slot] if arrs[a].ndim == 3 else src[a]
        local, sends = [], []
        for a in range(n):
            local.append(pltpu.make_async_copy(part(a, me), dst[a].at[me], local_sems.at[a]))
            local[-1].start()
            for j, chip in enumerate(chips):
                sends.append(_remote(part(a, 2 * chip[0] + chip[1]), dst[a].at[me], send_sems.at[a, j], recv_sems.at[a, j], (*chip, c)))
                sends[-1].start()
        for a in range(n):
            for j, chip in enumerate(chips):
                landed = dst[a].at[2 * chip[0] + chip[1]]
                _remote(landed, landed, send_sems.at[a, j], recv_sems.at[a, j], (*chip, c)).wait_recv()
        for cp in sends:
            cp.wait_send()
        for cp in local:
            cp.wait()

    outs = [jax.ShapeDtypeStruct((4,) + s.shape[-2:], s.dtype) for s in arrs]
    return _hbm_call(body, "chip_exchange", arrs, outs,
                     [pltpu.SemaphoreType.DMA((n, 3)), pltpu.SemaphoreType.DMA((n, 3)), pltpu.SemaphoreType.DMA((n,))])


def _pair_gather(halves):
    n = len(halves)

    def body(*refs):
        src, dst = refs[:n], refs[n:2 * n]
        send_sems, recv_sems, local_sems = refs[2 * n:]
        x, y, c = _coords()
        local, sends = [], []
        for a in range(n):
            local.append(pltpu.make_async_copy(src[a], dst[a].at[c], local_sems.at[a]))
            local[-1].start()
            sends.append(_remote(src[a], dst[a].at[c], send_sems.at[a], recv_sems.at[a], (x, y, 1 - c)))
            sends[-1].start()
        for a in range(n):
            _remote(src[a], dst[a].at[1 - c], send_sems.at[a], recv_sems.at[a], (x, y, 1 - c)).wait_recv()
        for cp in sends:
            cp.wait_send()
        for cp in local:
            cp.wait()

    outs = [jax.ShapeDtypeStruct((2,) + s.shape, s.dtype) for s in halves]
    return _hbm_call(body, "pair_gather", halves, outs,
                     [pltpu.SemaphoreType.DMA((n,)), pltpu.SemaphoreType.DMA((n,)), pltpu.SemaphoreType.DMA((n,))])


def _pair_sum(a, g, land, c_arr, out_dtype):
    nq, _, hr, cc = g.shape

    def body(c_ref, g_ref, l_ref, o_ref):
        o_ref[...] = (g_ref[...] + l_ref[...]).astype(out_dtype)

    spec = pl.BlockSpec((None, hr, cc), lambda qi, cr: (qi, 0, 0))
    grid_spec = pltpu.PrefetchScalarGridSpec(
        num_scalar_prefetch=1, grid=(nq,),
        in_specs=[pl.BlockSpec((None, None, hr, cc), lambda qi, cr: (qi, cr[0], 0, 0)), spec], out_specs=spec)
    return pl.pallas_call(body, name=f"pair_sum_{a}", grid_spec=grid_spec, out_shape=jax.ShapeDtypeStruct((nq, hr, cc), out_dtype),
                          compiler_params=_cparams())(c_arr, g, land)


def _chip_sum(a, parts):
    _, hr, cc = parts.shape

    def body(p_ref, o_ref):
        s = p_ref[0].astype(F32) + p_ref[1].astype(F32)
        o_ref[...] = (s + p_ref[2].astype(F32)) + p_ref[3].astype(F32)

    return pl.pallas_call(body, name=f"chip_sum_{a}", out_shape=jax.ShapeDtypeStruct((hr, cc), F32),
                          compiler_params=pltpu.CompilerParams(vmem_limit_bytes=VMEM_LIMIT))(parts)


def _row_block(rows, cols, limit=1 << 20):
    best = 8
    for tr in range(8, rows + 1, 8):
        if rows % tr == 0 and tr * cols * 4 <= limit:
            best = tr
    return best


def _adamw(name, w, g, m, v):
    r, c = w.shape
    cg = g.shape[1]
    tr = _row_block(r, cg)
    bc1 = 1.0 - ADAM_B1 ** ADAM_STEP
    bc2 = 1.0 - ADAM_B2 ** ADAM_STEP

    def body(w_ref, g_ref, m_ref, v_ref, go, do, mo, vo):
        gv = g_ref[:, 0:c]
        mn = ADAM_B1 * m_ref[...] + (1.0 - ADAM_B1) * gv
        vn = ADAM_B2 * v_ref[...] + (1.0 - ADAM_B2) * (gv * gv)
        go[...] = gv
        mo[...] = mn
        vo[...] = vn
        do[...] = -ADAM_LR * ((mn / bc1) / (jnp.sqrt(vn / bc2) + ADAM_EPS) + ADAM_WD * w_ref[...])

    blk = pl.BlockSpec((tr, c), lambda i: (i, 0))
    return pl.pallas_call(
        body, name=f"adamw_{name}", grid=(r // tr,),
        in_specs=[blk, pl.BlockSpec((tr, cg), lambda i: (i, 0)), blk, blk], out_specs=[blk] * 4,
        out_shape=[jax.ShapeDtypeStruct((r, c), F32)] * 4, compiler_params=_cparams(),
    )(w, g, m, v)


def _rows128(a):
    return a.reshape(-1, LANES)


def _pad_rows(a, rows):
    return jnp.pad(a, ((0, rows - a.shape[0]), (0, 0)))


def kernel(x, w_in, sgu_ln_g, sgu_ln_b, w_s, b_s, conv_w, conv_b, conv_ln_g, conv_ln_b, w_out, ln1_g, ln1_b, w_gate, w_up, w_down, ln2_g, ln2_b, loss_target, m_w_in, m_sgu_ln_g, m_sgu_ln_b, m_w_s, m_b_s, m_conv_w, m_conv_b, m_conv_ln_g, m_conv_ln_b, m_w_out, m_ln1_g, m_ln1_b, m_w_gate, m_w_up, m_w_down, m_ln2_g, m_ln2_b, v_w_in, v_sgu_ln_g, v_sgu_ln_b, v_w_s, v_b_s, v_conv_w, v_conv_b, v_conv_ln_g, v_conv_ln_b, v_w_out, v_ln1_g, v_ln1_b, v_w_gate, v_w_up, v_w_down, v_ln2_g, v_ln2_b):
    depth, d, q = w_in.shape
    assert depth == 1 and x.shape[0] == 1
    t = x.shape[1]
    heads = w_s.shape[1]
    kw, cshard = conv_w.shape[1], conv_w.shape[2]
    fs = w_gate.shape[2]
    fp = -(-fs // MXU_N) * MXU_N
    n_pairs = q // LANES
    assert heads * HEAD_DIM == q and q % LANES == 0 and w_s.shape[2] == CHUNK and 4 * cshard == q and kw - 1 <= HALO
    alpha = (2.0 * depth) ** 0.25
    tm = min(512, t)
    assert t % tm == 0 and tm % CHUNK == 0
    x2, tgt = x[0], loss_target[0]
    mx, my, mc = _coords()
    me = 2 * mx + my
    c_arr = jnp.reshape(mc, (1,)).astype(jnp.int32)

    kwp = -(-kw // 16) * 16
    cw_shard = _pad_rows(conv_w[0], kwp)
    wi_b, wo_b, wg_b, wu_b, wd_b = _prep(w_in[0], w_out[0], w_gate[0], w_up[0], w_down[0], fp)
    wi, wo, wg, wu, wd, cw4 = _gather_shards([wi_b, wo_b, wg_b, wu_b, wd_b, cw_shard])
    wo = wo.reshape(d, d)
    cw = jnp.transpose(cw4, (1, 0, 2)).reshape(kwp, q)
    cwf = _pad_rows(cw[:kw][::-1], kwp)

    wm = jnp.where(jnp.tril(jnp.ones((CHUNK, CHUNK), bool)), w_s[0], 0.0)
    wst = wm.reshape(n_pairs, 2 * CHUNK, CHUNK).astype(BF16)
    wstt = jnp.transpose(wm, (0, 2, 1)).reshape(n_pairs, 2 * CHUNK, CHUNK).astype(BF16)
    bmat = jnp.repeat(b_s[0].T, HEAD_DIM, axis=1)
    vq = _pad_rows(jnp.concatenate([sgu_ln_g, sgu_ln_b, conv_b, conv_ln_g, conv_ln_b], axis=0), 8)
    vd = _pad_rows(jnp.concatenate([ln1_g, ln1_b, ln2_g, ln2_b], axis=0), 8)

    proj, y, xh1, rstd1 = _fwd_mix(x2, wi, wo, wst, bmat, cw, vq, vd, alpha, kw, tm)
    gp, up, x1b, dr2, loss_part, dg2, db2 = _fwd_mlp(xh1, tgt, wg, wu, wd, vd, alpha, tm)
    dx1, dwg, dwu, dwd = None, [], [], []
    for k in range(4):
        dx1, g_k, u_k, d_k = _bwd_mlp_slab(k, dr2, dx1, x1b, gp, up, wg, wu, wd, alpha, tm)
        dwg.append(g_k)
        dwu.append(u_k)
        dwd.append(d_k)
    grad_x, dwi, dwo, dws, dbs, dcw, dvq, dvd = _bwd_mix(dx1, xh1, rstd1, x2, proj, y, wi, wo, wst, wstt, bmat, cw, cwf, vq, vd, alpha, kw, tm)
    loss = lax.psum(loss_part[0, 0], MESH_AXES)

    dws = jnp.where(jnp.tril(jnp.ones((CHUNK, CHUNK), bool)), dws.reshape(heads, CHUNK, CHUNK), 0.0)
    dvd = dvd.at[VD_LN2_G].set(dg2[0]).at[VD_LN2_B].set(db2[0])
    pieces = [_rows128(dws), dbs[:, :heads].T, _rows128(dcw), _rows128(dvq), _rows128(dvd)]
    sizes = [p.shape[0] for p in pieces]
    rows = -(-sum(sizes) // 16) * 16
    small = _pad_rows(jnp.concatenate(pieces, axis=0), rows)

    big = [dwi, dwo.reshape(4, d // 4, d), jnp.stack(dwg), jnp.stack(dwu), jnp.stack(dwd)]
    halves = [b.reshape(4, 2, b.shape[1] // 2, b.shape[2]) for b in big] + [small.reshape(2, rows // 2, LANES)]
    landed = _pair_swap(halves)
    sums = [_pair_sum(a, h, l, c_arr, BF16) for a, (h, l) in enumerate(zip(halves[:-1], landed[:-1]))]
    sums.append(_pair_sum(len(big), halves[-1][None], landed[-1][None], c_arr, F32)[0])
    parts = _chip_exchange(sums)
    reduced = _pair_gather([_chip_sum(a, p) for a, p in enumerate(parts)])
    g_wi, g_wo, g_wg, g_wu, g_wd, g_small = [r.reshape(2 * r.shape[1], r.shape[2]) for r in reduced]

    out = {}
    out["w_in"] = _adamw("w_in", w_in[0], g_wi, m_w_in[0], v_w_in[0])
    out["w_out"] = _adamw("w_out", w_out[0], g_wo, m_w_out[0], v_w_out[0])
    out["w_gate"] = _adamw("w_gate", w_gate[0], g_wg, m_w_gate[0], v_w_gate[0])
    out["w_up"] = _adamw("w_up", w_up[0], g_wu, m_w_up[0], v_w_up[0])
    out["w_down"] = _adamw("w_down", w_down[0], g_wd, m_w_down[0], v_w_down[0])

    offs = [sum(sizes[:i]) for i in range(len(sizes))]
    g_cw_full = g_small[offs[2]:offs[2] + sizes[2]].reshape(kwp, q)
    g_cw = lax.dynamic_slice(g_cw_full, (0, me * cshard), (kwp, cshard))[:kw]
    out["conv_w"] = _adamw("conv_w", _pad_rows(conv_w[0], kwp), _pad_rows(g_cw, kwp), _pad_rows(m_conv_w[0], kwp), _pad_rows(v_conv_w[0], kwp))
    out["conv_w"] = [o[:kw] for o in out["conv_w"]]

    def pack(ws, bs, vqs, vds):
        ps = [_rows128(ws[0]), bs[0], jnp.zeros((sizes[2], LANES), F32),
              _rows128(_pad_rows(jnp.concatenate(vqs, axis=0), 8)), _rows128(_pad_rows(jnp.concatenate(vds, axis=0), 8))]
        return _pad_rows(jnp.concatenate(ps, axis=0), rows)

    packed = _adamw(
        "small",
        pack(w_s, b_s, [sgu_ln_g, sgu_ln_b, conv_b, conv_ln_g, conv_ln_b], [ln1_g, ln1_b, ln2_g, ln2_b]), g_small,
        pack(m_w_s, m_b_s, [m_sgu_ln_g, m_sgu_ln_b, m_conv_b, m_conv_ln_g, m_conv_ln_b], [m_ln1_g, m_ln1_b, m_ln2_g, m_ln2_b]),
        pack(v_w_s, v_b_s, [v_sgu_ln_g, v_sgu_ln_b, v_conv_b, v_conv_ln_g, v_conv_ln_b], [v_ln1_g, v_ln1_b, v_ln2_g, v_ln2_b]))

    def unpack(p):
        vq_o = p[offs[3]:offs[3] + sizes[3]].reshape(8, q)
        vd_o = p[offs[4]:offs[4] + sizes[4]].reshape(8, d)
        return {"w_s": p[offs[0]:offs[0] + sizes[0]].reshape(heads, CHUNK, CHUNK), "b_s": p[offs[1]:offs[1] + sizes[1]],
                "sgu_ln_g": vq_o[VQ_SGU_G], "sgu_ln_b": vq_o[VQ_SGU_B], "conv_b": vq_o[VQ_CONV_B],
                "conv_ln_g": vq_o[VQ_CLN_G], "conv_ln_b": vq_o[VQ_CLN_B],
                "ln1_g": vd_o[VD_LN1_G], "ln1_b": vd_o[VD_LN1_B], "ln2_g": vd_o[VD_LN2_G], "ln2_b": vd_o[VD_LN2_B]}

    small_out = [unpack(p) for p in packed]
    names = ["w_in", "sgu_ln_g", "sgu_ln_b", "w_s", "b_s", "conv_w", "conv_b", "conv_ln_g", "conv_ln_b", "w_out",
             "ln1_g", "ln1_b", "w_gate", "w_up", "w_down", "ln2_g", "ln2_b"]
    result = [loss, grad_x[None]]
    for kind in range(4):
        for nm in names:
            val = out[nm][kind] if nm in out else small_out[kind][nm]
            result.append(val[None])
    return tuple(result)
```

```python
import functools
import math

import jax
import jax.numpy as jnp
from jax import lax
from jax.experimental import pallas as pl
from jax.experimental.pallas import tpu as pltpu

F32 = jnp.float32
BF16 = jnp.bfloat16

LN_EPS = 1e-5
HEAD_DIM = 64
CHUNK = 128
HALO = 32
LANES = 128
MXU_N = 256
ADAM_LR, ADAM_B1, ADAM_B2, ADAM_EPS, ADAM_WD, ADAM_STEP = 0.001, 0.9, 0.999, 1e-08, 0.01, 10
VMEM_LIMIT = 60 * 1024 * 1024
MESH_AXES = ("x", "y", "c")
MESH_ID = pl.DeviceIdType.MESH


def _dot(a, b):
    return jnp.dot(a, b, preferred_element_type=F32)


def _dot_nt(a, b):
    return lax.dot_general(a, b, (((1,), (1,)), ((), ())), preferred_element_type=F32)


def _dot_tn(a, b):
    return lax.dot_general(a, b, (((0,), (0,)), ((), ())), preferred_element_type=F32)


def _sigmoid(v):
    return 1.0 / (1.0 + jnp.exp(-v))


def _gelu(v):
    cdf = 0.5 * (1.0 + lax.erf(v * (1.0 / math.sqrt(2.0))))
    pdf = jnp.exp(-0.5 * v * v) * (1.0 / math.sqrt(2.0 * math.pi))
    return v * cdf, cdf + v * pdf


def _ln_stats(v):
    mu = jnp.mean(v, axis=-1, keepdims=True)
    d = v - mu
    rstd = lax.rsqrt(jnp.mean(d * d, axis=-1, keepdims=True) + LN_EPS)
    return d * rstd, rstd


def _ln_bwd(dxhat, xhat, rstd):
    m1 = jnp.mean(dxhat, axis=-1, keepdims=True)
    m2 = jnp.mean(dxhat * xhat, axis=-1, keepdims=True)
    return rstd * (dxhat - m1 - xhat * m2)


def _colsum(v):
    return jnp.sum(v, axis=0, keepdims=True)


def _pair_lanes(v, nc, p):
    return jnp.concatenate([v[c * CHUNK:(c + 1) * CHUNK, p * LANES:(p + 1) * LANES] for c in range(nc)], axis=1)


def _unpair(parts, nc):
    rows = [jnp.concatenate([part[:, c * LANES:(c + 1) * LANES] for part in parts], axis=1) for c in range(nc)]
    return jnp.concatenate(rows, axis=0)


def _low_head(nc):
    lane = lax.broadcasted_iota(jnp.int32, (CHUNK, nc * LANES), 1)
    return (lane & (LANES - 1)) < HEAD_DIM


def _mix(wst_ref, v, nc, n_pairs):
    vb = v.astype(BF16)
    low = _low_head(nc)
    parts = []
    for p in range(n_pairs):
        r = _dot(wst_ref[p], _pair_lanes(vb, nc, p))
        parts.append(jnp.where(low, r[:CHUNK], r[CHUNK:]))
    return _unpair(parts, nc)


def _mix_wgrad(dm, vn, nc, n_pairs):
    low = _low_head(nc)
    vb = vn.astype(BF16)
    out = []
    for p in range(n_pairs):
        a = _pair_lanes(dm, nc, p)
        lhs = jnp.concatenate([jnp.where(low, a, 0.0), jnp.where(low, 0.0, a)], axis=0).astype(BF16)
        out.append(_dot_nt(lhs, _pair_lanes(vb, nc, p)))
    return out


def _conv_taps(buf_ref, w_ref, tm, q, off, kw):
    rb = min(tm, 128)
    cols = []
    for l0 in range(0, q, LANES):
        rows = []
        for r0 in range(0, tm, rb):
            acc = None
            for k in range(kw):
                t = w_ref[k:k + 1, l0:l0 + LANES] * buf_ref[off + k + r0:off + k + r0 + rb, l0:l0 + LANES]
                acc = t if acc is None else acc + t
            rows.append(acc)
        cols.append(jnp.concatenate(rows, axis=0))
    return jnp.concatenate(cols, axis=1)


def _cparams():
    return pltpu.CompilerParams(dimension_semantics=("arbitrary",), vmem_limit_bytes=VMEM_LIMIT)


def _full(shape):
    return pl.BlockSpec(shape, lambda i: (0,) * len(shape))


ANY = pl.BlockSpec(memory_space=pl.ANY)

VQ_SGU_G, VQ_SGU_B, VQ_CONV_B, VQ_CLN_G, VQ_CLN_B = range(5)
VD_LN1_G, VD_LN1_B, VD_LN2_G, VD_LN2_B = range(4)


def _fwd_mix(x, wi, wo, wst, bmat, cw, vq, vd, alpha, kw, tm):
    t, d = x.shape
    q = wi.shape[2]
    nc, n_pairs = tm // CHUNK, q // LANES
    n = t // tm

    def body(x_ref, wi_hbm, wo_hbm, wst_ref, bmat_ref, cw_ref, vq_ref, vd_ref,
             proj_ref, y_ref, xh_ref, rstd_ref, wi_v, wo_v, hb_ref):
        @pl.when(pl.program_id(0) == 0)
        def _():
            pltpu.sync_copy(wi_hbm, wi_v)
            pltpu.sync_copy(wo_hbm, wo_v)
            hb_ref[0:HALO, :] = jnp.zeros((HALO, q), F32)

        xv = x_ref[...]
        xb = xv.astype(BF16)
        pu, pv, pa, pg = (_dot(xb, wi_v[j]) for j in range(4))
        for j, pj in enumerate((pu, pv, pa, pg)):
            proj_ref[:, j * q:(j + 1) * q] = pj.astype(BF16)
        zu, _ = _gelu(pu)
        zv, _ = _gelu(pv)
        vhat, _ = _ln_stats(zv)
        vn = vhat * vq_ref[VQ_SGU_G:VQ_SGU_G + 1, :] + vq_ref[VQ_SGU_B:VQ_SGU_B + 1, :]
        mixed = _mix(wst_ref, vn, nc, n_pairs) + jnp.concatenate([bmat_ref[...]] * nc, axis=0)
        y_ref[:, 0:q] = (zu * mixed).astype(BF16)

        hb_ref[HALO:HALO + tm, :] = pa * _sigmoid(pg)
        yc = _conv_taps(hb_ref, cw_ref, tm, q, HALO - (kw - 1), kw) + vq_ref[VQ_CONV_B:VQ_CONV_B + 1, :]
        hb_ref[0:HALO, :] = hb_ref[tm:tm + HALO, :]
        yhat, _ = _ln_stats(yc)
        yn = yhat * vq_ref[VQ_CLN_G:VQ_CLN_G + 1, :] + vq_ref[VQ_CLN_B:VQ_CLN_B + 1, :]
        y_ref[:, q:2 * q] = (yn * _sigmoid(yn)).astype(BF16)

        r1 = alpha * xv + _dot(y_ref[...], wo_v[...])
        xhat, rstd = _ln_stats(r1)
        xh_ref[...] = xhat
        rstd_ref[...] = rstd

    row = lambda w: pl.BlockSpec((tm, w), lambda i: (i, 0))
    return pl.pallas_call(
        body, name="fwd_mix", grid=(n,),
        in_specs=[row(d), ANY, ANY, _full(wst.shape), _full(bmat.shape), _full(cw.shape), _full(vq.shape), _full(vd.shape)],
        out_specs=[row(4 * q), row(d), row(d), row(1)],
        out_shape=[jax.ShapeDtypeStruct((t, 4 * q), BF16), jax.ShapeDtypeStruct((t, d), BF16),
                   jax.ShapeDtypeStruct((t, d), F32), jax.ShapeDtypeStruct((t, 1), F32)],
        scratch_shapes=[pltpu.VMEM(wi.shape, BF16), pltpu.VMEM(wo.shape, BF16), pltpu.VMEM((HALO + tm, q), F32)],
        compiler_params=_cparams(),
    )(x, wi, wo, wst, bmat, cw, vq, vd)


def _fwd_mlp(xh1, tgt, wg, wu, wd, vd, alpha, tm):
    t, d = xh1.shape
    ns, _, fp = wg.shape
    n = t // tm

    def body(xh_ref, tgt_ref, wg_hbm, wu_hbm, wd_hbm, vd_ref,
             gp_ref, up_ref, x1b_ref, dr2_ref, loss_ref, dg2_ref, db2_ref, wg_v, wu_v, wd_v):
        @pl.when(pl.program_id(0) == 0)
        def _():
            pltpu.sync_copy(wg_hbm, wg_v)
            pltpu.sync_copy(wu_hbm, wu_v)
            pltpu.sync_copy(wd_hbm, wd_v)
            loss_ref[...] = jnp.zeros_like(loss_ref)
            dg2_ref[...] = jnp.zeros_like(dg2_ref)
            db2_ref[...] = jnp.zeros_like(db2_ref)

        x1 = xh_ref[...] * vd_ref[VD_LN1_G:VD_LN1_G + 1, :] + vd_ref[VD_LN1_B:VD_LN1_B + 1, :]
        x1b = x1.astype(BF16)
        x1b_ref[...] = x1b
        acc = alpha * x1
        for k in range(ns):
            gp = _dot(x1b, wg_v[k])
            up = _dot(x1b, wu_v[k])
            gp_ref[:, k * fp:(k + 1) * fp] = gp.astype(BF16)
            up_ref[:, k * fp:(k + 1) * fp] = up.astype(BF16)
            acc = acc + _dot((gp * _sigmoid(gp) * up).astype(BF16), wd_v[k])
        xh2, rstd2 = _ln_stats(acc)
        g2 = vd_ref[VD_LN2_G:VD_LN2_G + 1, :]
        err = xh2 * g2 + vd_ref[VD_LN2_B:VD_LN2_B + 1, :] - tgt_ref[...]
        loss_ref[...] += _colsum(jnp.sum(err * err, axis=1, keepdims=True)) * (0.5 / d)
        dy = err * (1.0 / d)
        dg2_ref[...] += _colsum(dy * xh2)
        db2_ref[...] += _colsum(dy)
        dr2_ref[...] = _ln_bwd(dy * g2, xh2, rstd2)

    row = lambda w: pl.BlockSpec((tm, w), lambda i: (i, 0))
    return pl.pallas_call(
        body, name="fwd_mlp", grid=(n,),
        in_specs=[row(d), row(d), ANY, ANY, ANY, _full(vd.shape)],
        out_specs=[row(ns * fp), row(ns * fp), row(d), row(d), _full((8, LANES)), _full((1, d)), _full((1, d))],
        out_shape=[jax.ShapeDtypeStruct((t, ns * fp), BF16), jax.ShapeDtypeStruct((t, ns * fp), BF16),
                   jax.ShapeDtypeStruct((t, d), BF16), jax.ShapeDtypeStruct((t, d), F32),
                   jax.ShapeDtypeStruct((8, LANES), F32), jax.ShapeDtypeStruct((1, d), F32), jax.ShapeDtypeStruct((1, d), F32)],
        scratch_shapes=[pltpu.VMEM(wg.shape, BF16), pltpu.VMEM(wu.shape, BF16), pltpu.VMEM(wd.shape, BF16)],
        compiler_params=_cparams(),
    )(xh1, tgt, wg, wu, wd, vd)


def _bwd_mlp_slab(k, dr2, prev, x1b, gp, up, wg, wu, wd, alpha, tm):
    t, d = dr2.shape
    ns, _, fp = wg.shape
    n = t // tm
    first = prev is None

    def body(*refs):
        if first:
            dr_ref, x1b_ref, gp_ref, up_ref, wg_ref, wu_ref, wd_ref, dx_ref, dwg_hbm, dwu_hbm, dwd_hbm, ag, au, ad = refs
        else:
            (dr_ref, dxp_ref, x1b_ref, gp_ref, up_ref, wg_ref, wu_ref, wd_ref, _, _, _,
             dx_ref, dwg_hbm, dwu_hbm, dwd_hbm, ag, au, ad) = refs

        @pl.when(pl.program_id(0) == 0)
        def _():
            ag[...] = jnp.zeros_like(ag)
            au[...] = jnp.zeros_like(au)
            ad[...] = jnp.zeros_like(ad)

        dr = dr_ref[...]
        drb = dr.astype(BF16)
        x1b = x1b_ref[...]
        gpv = gp_ref[...].astype(F32)
        upv = up_ref[...].astype(F32)
        dh = _dot_nt(drb, wd_ref[0])
        sg = _sigmoid(gpv)
        silu = gpv * sg
        ad[...] += _dot_tn((silu * upv).astype(BF16), drb)
        dgp = (dh * upv * (sg * (1.0 + gpv * (1.0 - sg)))).astype(BF16)
        dup = (dh * silu).astype(BF16)
        ag[...] += _dot_tn(x1b, dgp)
        au[...] += _dot_tn(x1b, dup)
        base = alpha * dr if first else dxp_ref[...]
        dx_ref[...] = base + _dot_nt(dgp, wg_ref[0]) + _dot_nt(dup, wu_ref[0])

        @pl.when(pl.program_id(0) == n - 1)
        def _():
            pltpu.sync_copy(ag, dwg_hbm.at[k])
            pltpu.sync_copy(au, dwu_hbm.at[k])
            pltpu.sync_copy(ad, dwd_hbm.at[k])

    row = lambda w: pl.BlockSpec((tm, w), lambda i: (i, 0))
    slab = pl.BlockSpec((tm, fp), lambda i: (i, k))
    wcol = pl.BlockSpec((1, d, fp), lambda i: (k, 0, 0))
    wrow = pl.BlockSpec((1, fp, d), lambda i: (k, 0, 0))
    ins = [dr2] + ([] if first else [prev[0]]) + [x1b, gp, up, wg, wu, wd] + ([] if first else list(prev[1:]))
    in_specs = [row(d)] + ([] if first else [row(d)]) + [row(d), slab, slab, wcol, wcol, wrow] + ([] if first else [ANY] * 3)
    return pl.pallas_call(
        body, name=f"bwd_mlp_{k}", grid=(n,),
        in_specs=in_specs,
        out_specs=[row(d), ANY, ANY, ANY],
        out_shape=[jax.ShapeDtypeStruct((t, d), F32), jax.ShapeDtypeStruct((ns, d, fp), F32),
                   jax.ShapeDtypeStruct((ns, d, fp), F32), jax.ShapeDtypeStruct((ns, fp, d), F32)],
        scratch_shapes=[pltpu.VMEM((d, fp), F32), pltpu.VMEM((d, fp), F32), pltpu.VMEM((fp, d), F32)],
        input_output_aliases={} if first else {8: 1, 9: 2, 10: 3},
        compiler_params=_cparams(),
    )(*ins)


def _bwd_mix(dx1, xh1, rstd1, x, proj, y, wi, wo, wst, wstt, bmat, cw, cwf, vq, vd, alpha, kw, tm):
    t, d = x.shape
    q = wi.shape[2]
    nc, n_pairs = tm // CHUNK, q // LANES
    n = t // tm
    hb_per_tile = tm // HALO

    def body(dx1_ref, xh_ref, rstd_ref, x_ref, proj_ref, halo_ref, y_ref, wi_hbm, wo_hbm, wst_ref, wstt_ref, bmat_ref,
             cw_ref, cwf_ref, vq_ref, vd_ref,
             gx_ref, dwi_hbm, dwo_hbm, dws_ref, dbs_ref, dcw_ref, dvq_ref, dvd_ref,
             wi_v, wo_v, awi, awo, hb_ref, dyb_ref, dbm_ref):
        i = pl.program_id(0)

        @pl.when(i == 0)
        def _():
            pltpu.sync_copy(wi_hbm, wi_v)
            pltpu.sync_copy(wo_hbm, wo_v)
            awi[...] = jnp.zeros_like(awi)
            awo[...] = jnp.zeros_like(awo)
            for r in (dws_ref, dbm_ref, dcw_ref, dvq_ref, dvd_ref):
                r[...] = jnp.zeros_like(r)
            dyb_ref[tm:tm + HALO, :] = jnp.zeros((HALO, q), F32)

        dx1v = dx1_ref[...]
        xh = xh_ref[...]
        dvd_ref[VD_LN1_G:VD_LN1_G + 1, :] += _colsum(dx1v * xh)
        dvd_ref[VD_LN1_B:VD_LN1_B + 1, :] += _colsum(dx1v)
        dr1 = _ln_bwd(dx1v * vd_ref[VD_LN1_G:VD_LN1_G + 1, :], xh, rstd_ref[...])
        dr1b = dr1.astype(BF16)
        awo[...] += _dot_tn(y_ref[...], dr1b)
        dy = _dot_nt(dr1b, wo_v[...])

        pu = proj_ref[:, 0:q].astype(F32)
        pv = proj_ref[:, q:2 * q].astype(F32)
        zu, gu = _gelu(pu)
        zv, gv = _gelu(pv)
        vhat, rstd_v = _ln_stats(zv)
        sgu_g = vq_ref[VQ_SGU_G:VQ_SGU_G + 1, :]
        vn = vhat * sgu_g + vq_ref[VQ_SGU_B:VQ_SGU_B + 1, :]
        mixed = _mix(wst_ref, vn, nc, n_pairs) + jnp.concatenate([bmat_ref[...]] * nc, axis=0)
        doa = dy[:, 0:q]
        dm = doa * zu
        dpu = (doa * mixed * gu).astype(BF16)
        acc = dm[0:CHUNK]
        for c in range(1, nc):
            acc = acc + dm[c * CHUNK:(c + 1) * CHUNK]
        dbm_ref[...] += acc
        for p, g in enumerate(_mix_wgrad(dm, vn, nc, n_pairs)):
            dws_ref[p] += g
        dvn = _mix(wstt_ref, dm, nc, n_pairs)
        dvq_ref[VQ_SGU_G:VQ_SGU_G + 1, :] += _colsum(dvn * vhat)
        dvq_ref[VQ_SGU_B:VQ_SGU_B + 1, :] += _colsum(dvn)
        dpv = (_ln_bwd(dvn * sgu_g, vhat, rstd_v) * gv).astype(BF16)

        pa = proj_ref[:, 2 * q:3 * q].astype(F32)
        pg = proj_ref[:, 3 * q:4 * q].astype(F32)
        sg = _sigmoid(pg)
        hb_ref[HALO:HALO + tm, :] = pa * sg
        ha = halo_ref[:, 0:q].astype(F32)
        hg = halo_ref[:, q:2 * q].astype(F32)
        hb_ref[0:HALO, :] = jnp.where(i == n - 1, 0.0, ha * _sigmoid(hg))
        yc = _conv_taps(hb_ref, cw_ref, tm, q, HALO - (kw - 1), kw) + vq_ref[VQ_CONV_B:VQ_CONV_B + 1, :]
        yhat, rstd_c = _ln_stats(yc)
        cln_g = vq_ref[VQ_CLN_G:VQ_CLN_G + 1, :]
        yn = yhat * cln_g + vq_ref[VQ_CLN_B:VQ_CLN_B + 1, :]
        sy = _sigmoid(yn)
        dyn = dy[:, q:2 * q] * (sy * (1.0 + yn * (1.0 - sy)))
        dvq_ref[VQ_CLN_G:VQ_CLN_G + 1, :] += _colsum(dyn * yhat)
        dvq_ref[VQ_CLN_B:VQ_CLN_B + 1, :] += _colsum(dyn)
        dyc = _ln_bwd(dyn * cln_g, yhat, rstd_c)
        dvq_ref[VQ_CONV_B:VQ_CONV_B + 1, :] += _colsum(dyc)
        dyb_ref[0:tm, :] = dyc
        off = HALO - (kw - 1)
        for k in range(kw):
            dcw_ref[k:k + 1, :] += _colsum(dyb_ref[0:tm, :] * hb_ref[off + k:off + k + tm, :])
        dh = _conv_taps(dyb_ref, cwf_ref, tm, q, 0, kw)
        dyb_ref[tm:tm + HALO, :] = dyb_ref[0:HALO, :]
        da = (dh * sg).astype(BF16)
        dg = (dh * pa * (sg * (1.0 - sg))).astype(BF16)

        xb = x_ref[...].astype(BF16)
        gx = alpha * dr1
        for j, dpj in enumerate((dpu, dpv, da, dg)):
            awi[j] += _dot_tn(xb, dpj)
            gx = gx + _dot_nt(dpj, wi_v[j])
        gx_ref[...] = gx

        @pl.when(i == n - 1)
        def _():
            pltpu.sync_copy(awi, dwi_hbm)
            pltpu.sync_copy(awo, dwo_hbm)
            lane = lax.broadcasted_iota(jnp.int32, (CHUNK, LANES), 1)
            low = lane < HEAD_DIM
            dbs = jnp.zeros((CHUNK, LANES), F32)
            for p in range(n_pairs):
                grp = dbm_ref[:, p * LANES:(p + 1) * LANES]
                dbs = jnp.where(lane == 2 * p, jnp.sum(jnp.where(low, grp, 0.0), axis=1, keepdims=True), dbs)
                dbs = jnp.where(lane == 2 * p + 1, jnp.sum(jnp.where(low, 0.0, grp), axis=1, keepdims=True), dbs)
            dbs_ref[...] = dbs

    rev =lambda w: pl.BlockSpec((tm, w), lambda i: (n - 1 - i, 0))
    halo = pl.BlockSpec((HALO, 2 * q), lambda i: (jnp.maximum((n - 1 - i) * hb_per_tile - 1, 0), 1))
    small = [jax.ShapeDtypeStruct((n_pairs, 2 * CHUNK, CHUNK), F32), jax.ShapeDtypeStruct((CHUNK, LANES), F32),
             jax.ShapeDtypeStruct(cw.shape, F32), jax.ShapeDtypeStruct(vq.shape, F32), jax.ShapeDtypeStruct(vd.shape, F32)]
    return pl.pallas_call(
        body, name="bwd_mix", grid=(n,),
        in_specs=[rev(d), rev(d), rev(1), rev(d), rev(4 * q), halo, rev(d), ANY, ANY, _full(wst.shape), _full(wstt.shape),
                  _full(bmat.shape), _full(cw.shape), _full(cwf.shape), _full(vq.shape), _full(vd.shape)],
        out_specs=[rev(d), ANY, ANY] + [_full(s.shape) for s in small],
        out_shape=[jax.ShapeDtypeStruct((t, d), F32), jax.ShapeDtypeStruct(wi.shape, F32), jax.ShapeDtypeStruct(wo.shape, F32)] + small,
        scratch_shapes=[pltpu.VMEM(wi.shape, BF16), pltpu.VMEM(wo.shape, BF16), pltpu.VMEM(wi.shape, F32), pltpu.VMEM(wo.shape, F32),
                        pltpu.VMEM((HALO + tm, q), F32), pltpu.VMEM((tm + HALO, q), F32), pltpu.VMEM((CHUNK, q), F32)],
        compiler_params=_cparams(),
    )(dx1, xh1, rstd1, x, proj, proj, y, wi, wo, wst, wstt, bmat, cw, cwf, vq, vd)


def _prep(me_arr, w_in, w_out, w_gate, w_up, w_down, conv_w, fp, kwp):
    d, fs = w_gate.shape
    kw, cshard = conv_w.shape

    def body(me_ref, wi_ref, wo_ref, wg_ref, wu_ref, wd_ref, cw_ref, oi, oo, og, ou, od, oc):
        oi[...] = wi_ref[...].astype(BF16)
        oo[...] = wo_ref[...].astype(BF16)
        for src, dst in ((wg_ref, og), (wu_ref, ou)):
            dst[:, 0:fs] = src[...].astype(BF16)
            if fp > fs:
                dst[:, fs:fp] = jnp.zeros((d, fp - fs), BF16)
        od[0:fs, :] = wd_ref[...].astype(BF16)
        if fp > fs:
            od[fs:fp, :] = jnp.zeros((fp - fs, d), BF16)
        oc[0:kw, :] = cw_ref[...]
        oc[kw:kwp, :] = jnp.zeros((kwp - kw, cshard), F32)

    ins = (w_in, w_out, w_gate, w_up, w_down, conv_w)
    outs = [jax.ShapeDtypeStruct((4,) + w_in.shape, BF16), jax.ShapeDtypeStruct((4,) + w_out.shape, BF16),
            jax.ShapeDtypeStruct((4, d, fp), BF16), jax.ShapeDtypeStruct((4, d, fp), BF16), jax.ShapeDtypeStruct((4, fp, d), BF16),
            jax.ShapeDtypeStruct((4, kwp, cshard), F32)]
    grid_spec = pltpu.PrefetchScalarGridSpec(
        num_scalar_prefetch=1, grid=(1,),
        in_specs=[pl.BlockSpec(a.shape, lambda i, me: (0, 0)) for a in ins],
        out_specs=[pl.BlockSpec((None,) + o.shape[1:], lambda i, me: (me[0], 0, 0)) for o in outs])
    return pl.pallas_call(body, name="wprep", grid_spec=grid_spec, out_shape=outs, compiler_params=_cparams())(me_arr, *ins)


def _coords():
    return tuple(lax.axis_index(a) for a in MESH_AXES)


def _other_chips(x, y):
    return [(1 - x, y), (x, 1 - y), (1 - x, 1 - y)]


def _remote(src, dst, send_sem, recv_sem, to):
    return pltpu.make_async_remote_copy(src_ref=src, dst_ref=dst, send_sem=send_sem, recv_sem=recv_sem,
                                        device_id=to, device_id_type=MESH_ID)


def _hbm_call(body, name, ins, out_shape, scratch_shapes, aliases=None):
    return pl.pallas_call(
        body, name=name, in_specs=[ANY] * len(ins), out_specs=[ANY] * len(out_shape), out_shape=out_shape,
        scratch_shapes=scratch_shapes, input_output_aliases=aliases or {},
    )(*ins)


def _gather_shards(bufs):
    n = len(bufs)

    def body(*refs):
        dst = refs[n:2 * n]
        send_sems, recv_sems = refs[2 * n:]
        x, y, c = _coords()
        me, sib = 2 * x + y, (x, y, 1 - c)
        chips = _other_chips(x, y)

        def half(a, slot, hc):
            hr = bufs[a].shape[1] // 2
            return dst[a].at[slot, pl.ds(hc * hr, hr)]

        sends = []
        for a in range(n):
            for j, chip in enumerate(chips):
                sends.append(_remote(half(a, me, c), half(a, me, c), send_sems.at[a, j], recv_sems.at[a, j], (*chip, c)))
                sends[-1].start()
        for a in range(n):
            for j, chip in enumerate(chips):
                landed = half(a, 2 * chip[0] + chip[1], c)
                _remote(landed, landed, send_sems.at[a, j], recv_sems.at[a, j], (*chip, c)).wait_recv()
                sends.append(_remote(landed, landed, send_sems.at[a, 3 + j], recv_sems.at[a, 3 + j], sib))
                sends[-1].start()
        for a in range(n):
            for j, chip in enumerate(chips):
                other = half(a, 2 * chip[0] + chip[1], 1 - c)
                _remote(other, other, send_sems.at[a, 3 + j], recv_sems.at[a, 3 + j], sib).wait_recv()
        for cp in sends:
            cp.wait_send()

    return _hbm_call(body, "gather_shards", bufs, [jax.ShapeDtypeStruct(s.shape, s.dtype) for s in bufs],
                     [pltpu.SemaphoreType.DMA((n, 6)), pltpu.SemaphoreType.DMA((n, 6))], aliases={a: a for a in range(n)})


def _pair_swap(arrs):
    n = len(arrs)

    def body(*refs):
        src, land = refs[:n], refs[n:2 * n]
        send_sems, recv_sems = refs[2 * n:]
        x, y, c = _coords()
        copies = []
        for a in range(n):
            s = src[a].at[pl.ds(0, arrs[a].shape[0]), 1 - c] if arrs[a].ndim == 4 else src[a].at[1 - c]
            copies.append(_remote(s, land[a], send_sems.at[a], recv_sems.at[a], (x, y, 1 - c)))
            copies[-1].start()
        for cp in copies:
            cp.wait()

    outs = [jax.ShapeDtypeStruct(s.shape[:-3] + s.shape[-2:], s.dtype) for s in arrs]
    return _hbm_call(body, "pair_swap", arrs, outs, [pltpu.SemaphoreType.DMA((n,)), pltpu.SemaphoreType.DMA((n,))])


def _chip_exchange(arrs):
    n = len(arrs)

    def body(*refs):
        src, dst = refs[:n], refs[n:2 * n]
        send_sems, recv_sems = refs[2 * n:]
        x, y, c = _coords()
        me = 2 * x + y
        chips = _other_chips(x, y)
        part = lambda a, slot: src[a].at[slot] if arrs[a].ndim == 3 else src[a]
        sends = []
        for a in range(n):
            for j, chip in enumerate(chips):
                sends.append(_remote(part(a, 2 * chip[0] + chip[1]), dst[a].at[me], send_sems.at[a, j], recv_sems.at[a, j], (*chip, c)))
                sends[-1].start()
        for a in range(n):
            for j, chip in enumerate(chips):
                landed = dst[a].at[2 * chip[0] + chip[1]]
                _remote(landed, landed, send_sems.at[a, j], recv_sems.at[a, j], (*chip, c)).wait_recv()
        for cp in sends:
            cp.wait_send()

    outs = [jax.ShapeDtypeStruct((4,) + s.shape[-2:], s.dtype) for s in arrs]
    return _hbm_call(body, "chip_exchange", arrs, outs, [pltpu.SemaphoreType.DMA((n, 3)), pltpu.SemaphoreType.DMA((n, 3))])


def _pair_gather(halves):
    n = len(halves)

    def body(*refs):
        src, dst = refs[:n], refs[n:2 * n]
        send_sems, recv_sems = refs[2 * n:]
        x, y, c = _coords()
        copies = [_remote(src[a], dst[a], send_sems.at[a], recv_sems.at[a], (x, y, 1 - c)) for a in range(n)]
        for cp in copies:
            cp.start()
        for cp in copies:
            cp.wait()

    outs = [jax.ShapeDtypeStruct(s.shape, s.dtype) for s in halves]
    return _hbm_call(body, "pair_gather", halves, outs, [pltpu.SemaphoreType.DMA((n,)), pltpu.SemaphoreType.DMA((n,))])


def _pair_sum(a, g, land, c_arr, out_dtype):
    nq, _, hr, cc = g.shape

    def body(c_ref, g_ref, l_ref, o_ref):
        o_ref[...] = (g_ref[...] + l_ref[...]).astype(out_dtype)

    spec = pl.BlockSpec((None, hr, cc), lambda qi, cr: (qi, 0, 0))
    grid_spec = pltpu.PrefetchScalarGridSpec(
        num_scalar_prefetch=1, grid=(nq,),
        in_specs=[pl.BlockSpec((None, None, hr, cc), lambda qi, cr: (qi, cr[0], 0, 0)), spec], out_specs=spec)
    return pl.pallas_call(body, name=f"pair_sum_{a}", grid_spec=grid_spec, out_shape=jax.ShapeDtypeStruct((nq, hr, cc), out_dtype),
                          compiler_params=_cparams())(c_arr, g, land)


def _chip_sum(a, parts, own, me_arr):
    _, hr, cc = parts.shape

    def body(me_ref, p_ref, own_ref, o_ref):
        for mine in range(4):
            @pl.when(me_ref[0] == mine)
            def _():
                term = lambda j: (own_ref if j == mine else p_ref.at[j])[...].astype(F32)
                o_ref[...] = ((term(0) + term(1)) + term(2)) + term(3)

    own_spec = (pl.BlockSpec((None, hr, cc), lambda i, me: (me[0], 0, 0)) if own.ndim == 3
                else pl.BlockSpec((hr, cc), lambda i, me: (0, 0)))
    grid_spec = pltpu.PrefetchScalarGridSpec(
        num_scalar_prefetch=1, grid=(1,),
        in_specs=[pl.BlockSpec((4, hr, cc), lambda i, me: (0, 0, 0)), own_spec],
        out_specs=pl.BlockSpec((hr, cc), lambda i, me: (0, 0)))
    return pl.pallas_call(body, name=f"chip_sum_{a}", grid_spec=grid_spec, out_shape=jax.ShapeDtypeStruct((hr, cc), F32),
                          compiler_params=_cparams())(me_arr, parts, own)


def _row_block(rows, cols, limit=1 << 20):
    best = 8
    for tr in range(8, rows + 1, 8):
        if rows % tr == 0 and tr * cols * 4 <= limit:
            best = tr
    return best


def _adamw(name, w, g_mine, g_other, m, v, c_arr):
    r, c = w.shape
    hr, cg = g_mine.shape
    tr = math.gcd(_row_block(hr, cg), r)
    per_half = hr // tr
    bc1 = 1.0 - ADAM_B1 ** ADAM_STEP
    bc2 = 1.0 - ADAM_B2 ** ADAM_STEP

    def body(c_ref, w_ref, gm_ref, go_ref, m_ref, v_ref, go, do, mo, vo):
        gv = jnp.where(pl.program_id(0) // per_half == c_ref[0], gm_ref[:, 0:c], go_ref[:, 0:c])
        mn = ADAM_B1 * m_ref[...] + (1.0 - ADAM_B1) * gv
        vn = ADAM_B2 * v_ref[...] + (1.0 - ADAM_B2) * (gv * gv)
        go[...] = gv
        mo[...] = mn
        vo[...] = vn
        do[...] = -ADAM_LR * ((mn / bc1) / (jnp.sqrt(vn / bc2) + ADAM_EPS) + ADAM_WD * w_ref[...])

    blk = pl.BlockSpec((tr, c), lambda i, cr: (i, 0))
    gblk = pl.BlockSpec((tr, cg), lambda i, cr: (i % per_half, 0))
    grid_spec = pltpu.PrefetchScalarGridSpec(num_scalar_prefetch=1, grid=(r // tr,), in_specs=[blk, gblk, gblk, blk, blk],
                                             out_specs=[blk] * 4)
    return pl.pallas_call(body, name=f"adamw_{name}", grid_spec=grid_spec, out_shape=[jax.ShapeDtypeStruct((r, c), F32)] * 4,
                          compiler_params=_cparams())(c_arr, w, g_mine, g_other, m, v)


def _rows128(a):
    return a.reshape(-1, LANES)


def _pad_rows(a, rows):
    return jnp.pad(a, ((0, rows - a.shape[0]), (0, 0)))


def kernel(x, w_in, sgu_ln_g, sgu_ln_b, w_s, b_s, conv_w, conv_b, conv_ln_g, conv_ln_b, w_out, ln1_g, ln1_b, w_gate, w_up, w_down, ln2_g, ln2_b, loss_target, m_w_in, m_sgu_ln_g, m_sgu_ln_b, m_w_s, m_b_s, m_conv_w, m_conv_b, m_conv_ln_g, m_conv_ln_b, m_w_out, m_ln1_g, m_ln1_b, m_w_gate, m_w_up, m_w_down, m_ln2_g, m_ln2_b, v_w_in, v_sgu_ln_g, v_sgu_ln_b, v_w_s, v_b_s, v_conv_w, v_conv_b, v_conv_ln_g, v_conv_ln_b, v_w_out, v_ln1_g, v_ln1_b, v_w_gate, v_w_up, v_w_down, v_ln2_g, v_ln2_b):
    depth, d, q = w_in.shape
    assert depth == 1 and x.shape[0] == 1
    t = x.shape[1]
    heads = w_s.shape[1]
    kw, cshard = conv_w.shape[1], conv_w.shape[2]
    fs = w_gate.shape[2]
    fp = -(-fs // MXU_N) * MXU_N
    n_pairs = q // LANES
    assert heads * HEAD_DIM == q and q % LANES == 0 and w_s.shape[2] == CHUNK and 4 * cshard == q and kw - 1 <= HALO
    alpha = (2.0 * depth) ** 0.25
    tm = min(512, t)
    assert t % tm == 0 and tm % CHUNK == 0
    x2, tgt = x[0], loss_target[0]
    mx, my, mc = _coords()
    me = 2 * mx + my
    c_arr = jnp.reshape(mc, (1,)).astype(jnp.int32)

    kwp = -(-kw // 16) * 16
    me_arr = jnp.reshape(me, (1,)).astype(jnp.int32)
    wi, wo, wg, wu, wd, cw4 = _gather_shards(_prep(me_arr, w_in[0], w_out[0], w_gate[0], w_up[0], w_down[0], conv_w[0], fp, kwp))
    wo = wo.reshape(d, d)
    cw = jnp.transpose(cw4, (1, 0, 2)).reshape(kwp, q)
    cwf = _pad_rows(cw[:kw][::-1], kwp)

    wm = jnp.where(jnp.tril(jnp.ones((CHUNK, CHUNK), bool)), w_s[0], 0.0)
    wst = wm.reshape(n_pairs, 2 * CHUNK, CHUNK).astype(BF16)
    wstt = jnp.transpose(wm, (0, 2, 1)).reshape(n_pairs, 2 * CHUNK, CHUNK).astype(BF16)
    bmat = jnp.repeat(b_s[0].T, HEAD_DIM, axis=1)
    vq = _pad_rows(jnp.concatenate([sgu_ln_g, sgu_ln_b, conv_b, conv_ln_g, conv_ln_b], axis=0), 8)
    vd = _pad_rows(jnp.concatenate([ln1_g, ln1_b, ln2_g, ln2_b], axis=0), 8)

    proj, y, xh1, rstd1 = _fwd_mix(x2, wi, wo, wst, bmat, cw, vq, vd, alpha, kw, tm)
    gp, up, x1b, dr2, loss_part, dg2, db2 = _fwd_mlp(xh1, tgt, wg, wu, wd, vd, alpha, tm)
    mlp_grads = None
    for k in range(4):
        mlp_grads = _bwd_mlp_slab(k, dr2, mlp_grads, x1b, gp, up, wg, wu, wd, alpha, tm)
    dx1, dwg, dwu, dwd = mlp_grads
    grad_x, dwi, dwo, dws, dbs, dcw, dvq, dvd = _bwd_mix(dx1, xh1, rstd1, x2, proj, y, wi, wo, wst, wstt, bmat, cw, cwf, vq, vd, alpha, kw, tm)
    loss = lax.psum(loss_part[0, 0], MESH_AXES)

    dws = jnp.where(jnp.tril(jnp.ones((CHUNK, CHUNK), bool)), dws.reshape(heads, CHUNK, CHUNK), 0.0)
    dvd = dvd.at[VD_LN2_G].set(dg2[0]).at[VD_LN2_B].set(db2[0])
    pieces = [_rows128(dws), dbs[:, :heads].T, _rows128(dcw), _rows128(dvq), _rows128(dvd)]
    sizes = [p.shape[0] for p in pieces]
    rows = -(-sum(sizes) // 16) * 16
    small = _pad_rows(jnp.concatenate(pieces, axis=0), rows)

    big = [dwi, dwo.reshape(4, d // 4, d), dwg, dwu, dwd]
    halves = [b.reshape(4, 2, b.shape[1] // 2, b.shape[2]) for b in big] + [small.reshape(2, rows // 2, LANES)]
    landed = _pair_swap(halves)
    sums = [_pair_sum(a, h, l, c_arr, BF16) for a, (h, l) in enumerate(zip(halves[:-1], landed[:-1]))]
    sums.append(_pair_sum(len(big), halves[-1][None], landed[-1][None], c_arr, F32)[0])
    parts = _chip_exchange(sums)
    mine = [_chip_sum(a, p, s, me_arr) for a, (p, s) in enumerate(zip(parts, sums))]
    other = _pair_gather(mine)

    def pack(ws, bs, vqs, vds):
        ps = [_rows128(ws[0]), bs[0], jnp.zeros((sizes[2], LANES), F32),
              _rows128(_pad_rows(jnp.concatenate(vqs, axis=0), 8)), _rows128(_pad_rows(jnp.concatenate(vds, axis=0), 8))]
        return _pad_rows(jnp.concatenate(ps, axis=0), rows)

    out = {}
    for a, (nm, w_, m_, v_) in enumerate((("w_in", w_in, m_w_in, v_w_in), ("w_out", w_out, m_w_out, v_w_out),
                                          ("w_gate", w_gate, m_w_gate, v_w_gate), ("w_up", w_up, m_w_up, v_w_up),
                                          ("w_down", w_down, m_w_down, v_w_down))):
        out[nm] = _adamw(nm, w_[0], mine[a], other[a], m_[0], v_[0], c_arr)
    packed = _adamw(
        "small",
        pack(w_s, b_s, [sgu_ln_g, sgu_ln_b, conv_b, conv_ln_g, conv_ln_b], [ln1_g, ln1_b, ln2_g, ln2_b]), mine[-1], other[-1],
        pack(m_w_s, m_b_s, [m_sgu_ln_g, m_sgu_ln_b, m_conv_b, m_conv_ln_g, m_conv_ln_b], [m_ln1_g, m_ln1_b, m_ln2_g, m_ln2_b]),
        pack(v_w_s, v_b_s, [v_sgu_ln_g, v_sgu_ln_b, v_conv_b, v_conv_ln_g, v_conv_ln_b], [v_ln1_g, v_ln1_b, v_ln2_g, v_ln2_b]),
        c_arr)

    offs = [sum(sizes[:i]) for i in range(len(sizes))]
    g_cw_full = packed[0][offs[2]:offs[2] + sizes[2]].reshape(kwp, q)
    g_cw = lax.dynamic_slice(g_cw_full, (0, me * cshard), (kwp, cshard))
    out["conv_w"] = _adamw("conv_w", _pad_rows(conv_w[0], kwp), g_cw, g_cw, _pad_rows(m_conv_w[0], kwp), _pad_rows(v_conv_w[0], kwp), c_arr)
    out["conv_w"] = [o[:kw] for o in out["conv_w"]]

    def unpack(p):
        vq_o = p[offs[3]:offs[3] + sizes[3]].reshape(8, q)
        vd_o = p[offs[4]:offs[4] + sizes[4]].reshape(8, d)
        return {"w_s": p[offs[0]:offs[0] + sizes[0]].reshape(heads, CHUNK, CHUNK), "b_s": p[offs[1]:offs[1] + sizes[1]],
                "sgu_ln_g": vq_o[VQ_SGU_G], "sgu_ln_b": vq_o[VQ_SGU_B], "conv_b": vq_o[VQ_CONV_B],
                "conv_ln_g": vq_o[VQ_CLN_G], "conv_ln_b": vq_o[VQ_CLN_B],
                "ln1_g": vd_o[VD_LN1_G], "ln1_b": vd_o[VD_LN1_B], "ln2_g": vd_o[VD_LN2_G], "ln2_b": vd_o[VD_LN2_B]}

    small_out = [unpack(p) for p in packed]
    names = ["w_in", "sgu_ln_g", "sgu_ln_b", "w_s", "b_s", "conv_w", "conv_b", "conv_ln_g", "conv_ln_b", "w_out",
             "ln1_g", "ln1_b", "w_gate", "w_up", "w_down", "ln2_g", "ln2_b"]
    result = [loss, grad_x[None]]
    for kind in range(4):
        for nm in names:
            val = out[nm][kind] if nm in out else small_out[kind][nm]
            result.append(val[None])
    return tuple(result)
```

```python
import functools
import math

import jax
import jax.numpy as jnp
from jax import lax
from jax.experimental import pallas as pl
from jax.experimental.pallas import tpu as pltpu

F32 = jnp.float32
BF16 = jnp.bfloat16

LN_EPS = 1e-5
HEAD_DIM = 64
CHUNK = 128
HALO = 32
LANES = 128
MXU_N = 256
ADAM_LR, ADAM_B1, ADAM_B2, ADAM_EPS, ADAM_WD, ADAM_STEP = 0.001, 0.9, 0.999, 1e-08, 0.01, 10
VMEM_LIMIT = 60 * 1024 * 1024
MESH_AXES = ("x", "y", "c")
MESH_ID = pl.DeviceIdType.MESH


def _dot(a, b):
    return jnp.dot(a, b, preferred_element_type=F32)


def _dot_nt(a, b):
    return lax.dot_general(a, b, (((1,), (1,)), ((), ())), preferred_element_type=F32)


def _dot_tn(a, b):
    return lax.dot_general(a, b, (((0,), (0,)), ((), ())), preferred_element_type=F32)


def _sigmoid(v):
    return 1.0 / (1.0 + jnp.exp(-v))


def _gelu(v):
    cdf = 0.5 * (1.0 + lax.erf(v * (1.0 / math.sqrt(2.0))))
    pdf = jnp.exp(-0.5 * v * v) * (1.0 / math.sqrt(2.0 * math.pi))
    return v * cdf, cdf + v * pdf


def _ln_stats(v):
    mu = jnp.mean(v, axis=-1, keepdims=True)
    d = v - mu
    rstd = lax.rsqrt(jnp.mean(d * d, axis=-1, keepdims=True) + LN_EPS)
    return d * rstd, rstd


def _ln_bwd(dxhat, xhat, rstd):
    m1 = jnp.mean(dxhat, axis=-1, keepdims=True)
    m2 = jnp.mean(dxhat * xhat, axis=-1, keepdims=True)
    return rstd * (dxhat - m1 - xhat * m2)


def _colsum(v):
    return jnp.sum(v, axis=0, keepdims=True)


def _pair_lanes(v, nc, p):
    return jnp.concatenate([v[c * CHUNK:(c + 1) * CHUNK, p * LANES:(p + 1) * LANES] for c in range(nc)], axis=1)


def _unpair(parts, nc):
    rows = [jnp.concatenate([part[:, c * LANES:(c + 1) * LANES] for part in parts], axis=1) for c in range(nc)]
    return jnp.concatenate(rows, axis=0)


def _low_head(nc):
    lane = lax.broadcasted_iota(jnp.int32, (CHUNK, nc * LANES), 1)
    return (lane & (LANES - 1)) < HEAD_DIM


def _mix(wst_ref, v, nc, n_pairs):
    vb = v.astype(BF16)
    low = _low_head(nc)
    parts = []
    for p in range(n_pairs):
        r = _dot(wst_ref[p], _pair_lanes(vb, nc, p))
        parts.append(jnp.where(low, r[:CHUNK], r[CHUNK:]))
    return _unpair(parts, nc)


def _mix_wgrad(dm, vn, nc, n_pairs):
    low = _low_head(nc)
    vb = vn.astype(BF16)
    out = []
    for p in range(n_pairs):
        a = _pair_lanes(dm, nc, p)
        lhs = jnp.concatenate([jnp.where(low, a, 0.0), jnp.where(low, 0.0, a)], axis=0).astype(BF16)
        out.append(_dot_nt(lhs, _pair_lanes(vb, nc, p)))
    return out


SUBLANES = 8


def _shift_rows(sh_ref, buf_ref, rows):
    for b in range(SUBLANES):
        sh_ref[b, 0:rows, :] = buf_ref[b:b + rows, :]


def _conv_taps(sh_ref, w_ref, tm, q, off, kw):
    rb = min(tm, 64)
    cols = []
    for l0 in range(0, q, LANES):
        rows = []
        for r0 in range(0, tm, rb):
            acc = None
            for k in range(kw):
                a, b = divmod(off + k, SUBLANES)
                t = w_ref[k:k + 1, l0:l0 + LANES] * sh_ref[b, a * SUBLANES + r0:a * SUBLANES + r0 + rb, l0:l0 + LANES]
                acc = t if acc is None else acc + t
            rows.append(acc)
        cols.append(jnp.concatenate(rows, axis=0))
    return jnp.concatenate(cols, axis=1)


def _conv_wgrad(acc_ref, dy_ref, sh_ref, tm, q, off, kw):
    rb = min(tm, 128)
    for l0 in range(0, q, LANES):
        for k in range(kw):
            a, b = divmod(off + k, SUBLANES)
            part = None
            for r0 in range(0, tm, rb):
                prod = dy_ref[r0:r0 + rb, l0:l0 + LANES] * sh_ref[b, a * SUBLANES + r0:a * SUBLANES + r0 + rb, l0:l0 + LANES]
                prod = jnp.sum(prod.reshape(rb // SUBLANES, SUBLANES, LANES), axis=0)
                part = prod if part is None else part + prod
            acc_ref[k, :, l0:l0 + LANES] += part


def _cparams():
    return pltpu.CompilerParams(dimension_semantics=("arbitrary",), vmem_limit_bytes=VMEM_LIMIT)


def _full(shape):
    return pl.BlockSpec(shape, lambda i: (0,) * len(shape))


ANY = pl.BlockSpec(memory_space=pl.ANY)

VQ_SGU_G, VQ_SGU_B, VQ_CONV_B, VQ_CLN_G, VQ_CLN_B = range(5)
VD_LN1_G, VD_LN1_B, VD_LN2_G, VD_LN2_B = range(4)
RS_LN1, RS_SGU, RS_CONV = range(3)
RS_COLS = 8


def _fwd_mix(x, wi, wo, wst, bmat, cw, vq, vd, alpha, kw, tm):
    t, d = x.shape
    q = wi.shape[2]
    nc, n_pairs = tm // CHUNK, q // LANES
    n = t // tm

    def body(x_ref, wi_hbm, wo_hbm, wst_ref, bmat_ref, cw_ref, vq_ref, vd_ref,
             xb_ref, pag_ref, y_ref, xh_ref, rs_ref, zu_ref, mg_ref, vhat_ref, gv_ref, vnb_ref, yhat_ref,
             wi_v, wo_v, hb_ref, sh_ref):
        @pl.when(pl.program_id(0) == 0)
        def _():
            pltpu.sync_copy(wi_hbm, wi_v)
            pltpu.sync_copy(wo_hbm, wo_v)
            hb_ref[...] = jnp.zeros_like(hb_ref)

        xv = x_ref[...]
        xb = xv.astype(BF16)
        xb_ref[...] = xb
        pu, pv, pa, pg = (_dot(xb, wi_v[j]) for j in range(4))
        pag_ref[:, 0:q] = pa.astype(BF16)
        pag_ref[:, q:2 * q] = pg.astype(BF16)
        zu, gu = _gelu(pu)
        zv, gv = _gelu(pv)
        vhat, rstd_v = _ln_stats(zv)
        vnb = (vhat * vq_ref[VQ_SGU_G:VQ_SGU_G + 1, :] + vq_ref[VQ_SGU_B:VQ_SGU_B + 1, :]).astype(BF16)
        mixed = _mix(wst_ref, vnb, nc, n_pairs) + jnp.concatenate([bmat_ref[...]] * nc, axis=0)
        y_ref[:, 0:q] = (zu * mixed).astype(BF16)
        zu_ref[...] = zu
        mg_ref[...] = mixed * gu
        vhat_ref[...] = vhat
        gv_ref[...] = gv
        vnb_ref[...] = vnb

        hb_ref[HALO:HALO + tm, :] = pa * _sigmoid(pg)
        _shift_rows(sh_ref, hb_ref, tm + HALO)
        yc = _conv_taps(sh_ref, cw_ref, tm, q, HALO - (kw - 1), kw) + vq_ref[VQ_CONV_B:VQ_CONV_B + 1, :]
        hb_ref[0:HALO, :] = hb_ref[tm:tm + HALO, :]
        yhat, rstd_c = _ln_stats(yc)
        yhat_ref[...] = yhat
        yn = yhat * vq_ref[VQ_CLN_G:VQ_CLN_G + 1, :] + vq_ref[VQ_CLN_B:VQ_CLN_B + 1, :]
        y_ref[:, q:2 * q] = (yn * _sigmoid(yn)).astype(BF16)

        r1 = alpha * xv + _dot(y_ref[...], wo_v[...])
        xhat, rstd1 = _ln_stats(r1)
        xh_ref[...] = xhat
        col = lax.broadcasted_iota(jnp.int32, (tm, RS_COLS), 1)
        rs_ref[...] = jnp.where(col == RS_LN1, rstd1, jnp.where(col == RS_SGU, rstd_v, jnp.where(col == RS_CONV, rstd_c, 0.0)))

    row = lambda w: pl.BlockSpec((tm, w), lambda i: (i, 0))
    widths = [(d, BF16), (2 * q, BF16), (d, BF16), (d, F32), (RS_COLS, F32), (q, F32), (q, F32), (q, F32), (q, F32), (q, BF16), (q, F32)]
    return pl.pallas_call(
        body, name="fwd_mix", grid=(n,),
        in_specs=[row(d), ANY, ANY, _full(wst.shape), _full(bmat.shape), _full(cw.shape), _full(vq.shape), _full(vd.shape)],
        out_specs=[row(w) for w, _ in widths],
        out_shape=[jax.ShapeDtypeStruct((t, w), dt) for w, dt in widths],
        scratch_shapes=[pltpu.VMEM(wi.shape, BF16), pltpu.VMEM(wo.shape, BF16), pltpu.VMEM((HALO + tm + SUBLANES, q), F32),
                        pltpu.VMEM((SUBLANES, tm + HALO, q), F32)],
        compiler_params=_cparams(),
    )(x, wi, wo, wst, bmat, cw, vq, vd)


def _fwd_mlp(xh1, tgt, wg, wu, wd, vd, alpha, tm):
    t, d = xh1.shape
    ns, _, fp = wg.shape
    n = t // tm

    def body(xh_ref, tgt_ref, wg_hbm, wu_hbm, wd_hbm, vd_ref,
             gp_ref, up_ref, x1b_ref, dr2_ref, loss_ref, dg2_ref, db2_ref, wg_v, wu_v, wd_v):
        @pl.when(pl.program_id(0) == 0)
        def _():
            pltpu.sync_copy(wg_hbm, wg_v)
            pltpu.sync_copy(wu_hbm, wu_v)
            pltpu.sync_copy(wd_hbm, wd_v)
            loss_ref[...] = jnp.zeros_like(loss_ref)
            dg2_ref[...] = jnp.zeros_like(dg2_ref)
            db2_ref[...] = jnp.zeros_like(db2_ref)

        x1 = xh_ref[...] * vd_ref[VD_LN1_G:VD_LN1_G + 1, :] + vd_ref[VD_LN1_B:VD_LN1_B + 1, :]
        x1b = x1.astype(BF16)
        x1b_ref[...] = x1b
        acc = alpha * x1
        for k in range(ns):
            gp = _dot(x1b, wg_v[k])
            up = _dot(x1b, wu_v[k])
            gp_ref[:, k * fp:(k + 1) * fp] = gp.astype(BF16)
            up_ref[:, k * fp:(k + 1) * fp] = up.astype(BF16)
            acc = acc + _dot((gp * _sigmoid(gp) * up).astype(BF16), wd_v[k])
        xh2, rstd2 = _ln_stats(acc)
        g2 = vd_ref[VD_LN2_G:VD_LN2_G + 1, :]
        err = xh2 * g2 + vd_ref[VD_LN2_B:VD_LN2_B + 1, :] - tgt_ref[...]
        loss_ref[...] += _colsum(jnp.sum(err * err, axis=1, keepdims=True)) * (0.5 / d)
        dy = err * (1.0 / d)
        dg2_ref[...] += _colsum(dy * xh2)
        db2_ref[...] += _colsum(dy)
        dr2_ref[...] = _ln_bwd(dy * g2, xh2, rstd2)

    row = lambda w: pl.BlockSpec((tm, w), lambda i: (i, 0))
    return pl.pallas_call(
        body, name="fwd_mlp", grid=(n,),
        in_specs=[row(d), row(d), ANY, ANY, ANY, _full(vd.shape)],
        out_specs=[row(ns * fp), row(ns * fp), row(d), row(d), _full((8, LANES)), _full((1, d)), _full((1, d))],
        out_shape=[jax.ShapeDtypeStruct((t, ns * fp), BF16), jax.ShapeDtypeStruct((t, ns * fp), BF16),
                   jax.ShapeDtypeStruct((t, d), BF16), jax.ShapeDtypeStruct((t, d), F32),
                   jax.ShapeDtypeStruct((8, LANES), F32), jax.ShapeDtypeStruct((1, d), F32), jax.ShapeDtypeStruct((1, d), F32)],
        scratch_shapes=[pltpu.VMEM(wg.shape, BF16), pltpu.VMEM(wu.shape, BF16), pltpu.VMEM(wd.shape, BF16)],
        compiler_params=_cparams(),
    )(xh1, tgt, wg, wu, wd, vd)


def _bwd_mlp_slab(k, dr2, prev, x1b, gp, up, wg, wu, wd, alpha, tm):
    t, d = dr2.shape
    ns, _, fp = wg.shape
    n = t // tm
    first = prev is None

    def body(*refs):
        if first:
            dr_ref, x1b_ref, gp_ref, up_ref, wg_ref, wu_ref, wd_ref, dx_ref, dwg_hbm, dwu_hbm, dwd_hbm, ag, au, ad = refs
        else:
            (dr_ref, dxp_ref, x1b_ref, gp_ref, up_ref, wg_ref, wu_ref, wd_ref, _, _, _,
             dx_ref, dwg_hbm, dwu_hbm, dwd_hbm, ag, au, ad) = refs

        @pl.when(pl.program_id(0) == 0)
        def _():
            ag[...] = jnp.zeros_like(ag)
            au[...] = jnp.zeros_like(au)
            ad[...] = jnp.zeros_like(ad)

        dr = dr_ref[...]
        drb = dr.astype(BF16)
        x1b = x1b_ref[...]
        gpv = gp_ref[...].astype(F32)
        upv = up_ref[...].astype(F32)
        dh = _dot_nt(drb, wd_ref[0])
        sg = _sigmoid(gpv)
        silu = gpv * sg
        ad[...] += _dot_tn((silu * upv).astype(BF16), drb)
        dgp = (dh * upv * (sg * (1.0 + gpv * (1.0 - sg)))).astype(BF16)
        dup = (dh * silu).astype(BF16)
        ag[...] += _dot_tn(x1b, dgp)
        au[...] += _dot_tn(x1b, dup)
        base = alpha * dr if first else dxp_ref[...]
        dx_ref[...] = base + _dot_nt(dgp, wg_ref[0]) + _dot_nt(dup, wu_ref[0])

        @pl.when(pl.program_id(0) == n - 1)
        def _():
            pltpu.sync_copy(ag, dwg_hbm.at[k])
            pltpu.sync_copy(au, dwu_hbm.at[k])
            pltpu.sync_copy(ad, dwd_hbm.at[k])

    row = lambda w: pl.BlockSpec((tm, w), lambda i: (i, 0))
    slab = pl.BlockSpec((tm, fp), lambda i: (i, k))
    wcol = pl.BlockSpec((1, d, fp), lambda i: (k, 0, 0))
    wrow = pl.BlockSpec((1, fp, d), lambda i: (k, 0, 0))
    ins = [dr2] + ([] if first else [prev[0]]) + [x1b, gp, up, wg, wu, wd] + ([] if first else list(prev[1:]))
    in_specs = [row(d)] + ([] if first else [row(d)]) + [row(d), slab, slab, wcol, wcol, wrow] + ([] if first else [ANY] * 3)
    return pl.pallas_call(
        body, name=f"bwd_mlp_{k}", grid=(n,),
        in_specs=in_specs,
        out_specs=[row(d), ANY, ANY, ANY],
        out_shape=[jax.ShapeDtypeStruct((t, d), F32), jax.ShapeDtypeStruct((ns, d, fp), F32),
                   jax.ShapeDtypeStruct((ns, d, fp), F32), jax.ShapeDtypeStruct((ns, fp, d), F32)],
        scratch_shapes=[pltpu.VMEM((d, fp), F32), pltpu.VMEM((d, fp), F32), pltpu.VMEM((fp, d), F32)],
        input_output_aliases={} if first else {8: 1, 9: 2, 10: 3},
        compiler_params=_cparams(),
    )(*ins)


def _bwd_mix(dx1, saved, wi, wo, wstt, cw, cwf, vq, vd, alpha, kw, tm):
    xb, pag, y, xh1, rs, zu_s, mg_s, vhat_s, gv_s, vnb_s, yhat_s = saved
    t, d = xh1.shape
    q = wi.shape[2]
    nc, n_pairs = tm // CHUNK, q // LANES
    n = t // tm
    hb_per_tile = tm // HALO
    off = HALO - (kw - 1)

    def body(dx1_ref, xb_ref, pag_ref, halo_ref, y_ref, xh_ref, rs_ref, zu_ref, mg_ref, vhat_ref, gv_ref, vnb_ref, yhat_ref,
             wi_hbm, wo_hbm, wstt_ref, cw_ref, cwf_ref, vq_ref, vd_ref,
             gx_ref, dwi_hbm, dwo_hbm, dws_ref, dbs_ref, dcw_ref, dvq_ref, dvd_ref,
             wi_v, wo_v, awi, awo, hb_ref, dyb_ref, sh_ref, dbm_ref, dcw8_ref):
        i = pl.program_id(0)

        @pl.when(i == 0)
        def _():
            pltpu.sync_copy(wi_hbm, wi_v)
            pltpu.sync_copy(wo_hbm, wo_v)
            for r in (awi, awo, dws_ref, dbm_ref, dcw8_ref, dvq_ref, dvd_ref, hb_ref, dyb_ref):
                r[...] = jnp.zeros_like(r)

        dx1v = dx1_ref[...]
        xh = xh_ref[...]
        rsv = rs_ref[...]
        dvd_ref[VD_LN1_G:VD_LN1_G + 1, :] += _colsum(dx1v * xh)
        dvd_ref[VD_LN1_B:VD_LN1_B + 1, :] += _colsum(dx1v)
        dr1 = _ln_bwd(dx1v * vd_ref[VD_LN1_G:VD_LN1_G + 1, :], xh, rsv[:, RS_LN1:RS_LN1 + 1])
        dr1b = dr1.astype(BF16)
        awo[...] += _dot_tn(y_ref[...], dr1b)
        dy = _dot_nt(dr1b, wo_v[...])

        vhat = vhat_ref[...]
        sgu_g = vq_ref[VQ_SGU_G:VQ_SGU_G + 1, :]
        doa = dy[:, 0:q]
        dm = doa * zu_ref[...]
        dpu = (doa * mg_ref[...]).astype(BF16)
        acc = dm[0:CHUNK]
        for c in range(1, nc):
            acc = acc + dm[c * CHUNK:(c + 1) * CHUNK]
        dbm_ref[...] += acc
        for p, g in enumerate(_mix_wgrad(dm, vnb_ref[...], nc, n_pairs)):
            dws_ref[p] += g
        dvn = _mix(wstt_ref, dm, nc, n_pairs)
        dvq_ref[VQ_SGU_G:VQ_SGU_G + 1, :] += _colsum(dvn * vhat)
        dvq_ref[VQ_SGU_B:VQ_SGU_B + 1, :] += _colsum(dvn)
        dpv = (_ln_bwd(dvn * sgu_g, vhat, rsv[:, RS_SGU:RS_SGU + 1]) * gv_ref[...]).astype(BF16)

        pa = pag_ref[:, 0:q].astype(F32)
        sg = _sigmoid(pag_ref[:, q:2 * q].astype(F32))
        hb_ref[HALO:HALO + tm, :] = pa * sg
        ha = halo_ref[:, 0:q].astype(F32)
        hg = halo_ref[:, q:2 * q].astype(F32)
        hb_ref[0:HALO, :] = jnp.where(i == n - 1, 0.0, ha * _sigmoid(hg))
        yhat = yhat_ref[...]
        cln_g = vq_ref[VQ_CLN_G:VQ_CLN_G + 1, :]
        yn = yhat * cln_g + vq_ref[VQ_CLN_B:VQ_CLN_B + 1, :]
        sy = _sigmoid(yn)
        dyn = dy[:, q:2 * q] * (sy * (1.0 + yn * (1.0 - sy)))
        dvq_ref[VQ_CLN_G:VQ_CLN_G + 1, :] += _colsum(dyn * yhat)
        dvq_ref[VQ_CLN_B:VQ_CLN_B + 1, :] += _colsum(dyn)
        dyc = _ln_bwd(dyn * cln_g, yhat, rsv[:, RS_CONV:RS_CONV + 1])
        dvq_ref[VQ_CONV_B:VQ_CONV_B + 1, :] += _colsum(dyc)
        dyb_ref[0:tm, :] = dyc
        _shift_rows(sh_ref, hb_ref, tm + HALO)
        _conv_wgrad(dcw8_ref, dyb_ref, sh_ref, tm, q, off, kw)
        _shift_rows(sh_ref, dyb_ref, tm + HALO)
        dh = _conv_taps(sh_ref, cwf_ref, tm, q, 0, kw)
        dyb_ref[tm:tm + HALO, :] = dyb_ref[0:HALO, :]
        da = (dh * sg).astype(BF16)
        dg = (dh * pa * (sg * (1.0 - sg))).astype(BF16)

        xb = xb_ref[...]
        gx = alpha * dr1
        for j, dpj in enumerate((dpu, dpv, da, dg)):
            awi[j] += _dot_tn(xb, dpj)
            gx = gx + _dot_nt(dpj, wi_v[j])
        gx_ref[...] = gx

        @pl.when(i == n - 1)
        def _():
            pltpu.sync_copy(awi, dwi_hbm)
            pltpu.sync_copy(awo, dwo_hbm)
            lane = lax.broadcasted_iota(jnp.int32, (CHUNK, LANES), 1)
            low = lane < HEAD_DIM
            dbs = jnp.zeros((CHUNK, LANES), F32)
            for p in range(n_pairs):
                grp = dbm_ref[:, p * LANES:(p + 1) * LANES]
                dbs = jnp.where(lane == 2 * p, jnp.sum(jnp.where(low, grp, 0.0), axis=1, keepdims=True), dbs)
                dbs = jnp.where(lane == 2 * p + 1, jnp.sum(jnp.where(low, 0.0, grp), axis=1, keepdims=True), dbs)
            dbs_ref[...] = dbs
            for k in range(cw.shape[0]):
                dcw_ref[k:k + 1, :] = _colsum(dcw8_ref[k])

    rev = lambda w: pl.BlockSpec((tm, w), lambda i: (n - 1 - i, 0))
    halo = pl.BlockSpec((HALO, 2 * q), lambda i: (jnp.maximum((n - 1 - i) * hb_per_tile - 1, 0), 0))
    small = [jax.ShapeDtypeStruct((n_pairs, 2 * CHUNK, CHUNK), F32), jax.ShapeDtypeStruct((CHUNK, LANES), F32),
             jax.ShapeDtypeStruct(cw.shape, F32), jax.ShapeDtypeStruct(vq.shape, F32), jax.ShapeDtypeStruct(vd.shape, F32)]
    return pl.pallas_call(
        body, name="bwd_mix", grid=(n,),
        in_specs=[rev(d), rev(d), rev(2 * q), halo, rev(d), rev(d), rev(RS_COLS), rev(q), rev(q), rev(q), rev(q), rev(q), rev(q),
                  ANY, ANY, _full(wstt.shape), _full(cw.shape), _full(cwf.shape), _full(vq.shape), _full(vd.shape)],
        out_specs=[rev(d), ANY, ANY] + [_full(s.shape) for s in small],
        out_shape=[jax.ShapeDtypeStruct((t, d), F32), jax.ShapeDtypeStruct(wi.shape, F32), jax.ShapeDtypeStruct(wo.shape, F32)] + small,
        scratch_shapes=[pltpu.VMEM(wi.shape, BF16), pltpu.VMEM(wo.shape, BF16), pltpu.VMEM(wi.shape, F32), pltpu.VMEM(wo.shape, F32),
                        pltpu.VMEM((HALO + tm + SUBLANES, q), F32), pltpu.VMEM((tm + HALO + SUBLANES, q), F32),
                        pltpu.VMEM((SUBLANES, tm + HALO, q), F32), pltpu.VMEM((CHUNK, q), F32),
                        pltpu.VMEM((cw.shape[0], SUBLANES, q), F32)],
        compiler_params=_cparams(),
    )(dx1, xb, pag, pag, y, xh1, rs, zu_s, mg_s, vhat_s, gv_s, vnb_s, yhat_s, wi, wo, wstt, cw, cwf, vq, vd)


def _prep(me_arr, w_in, w_out, w_gate, w_up, w_down, conv_w, fp, kwp):
    d, fs = w_gate.shape
    kw, cshard = conv_w.shape

    def body(me_ref, wi_ref, wo_ref, wg_ref, wu_ref, wd_ref, cw_ref, oi, oo, og, ou, od, oc):
        oi[...] = wi_ref[...].astype(BF16)
        oo[...] = wo_ref[...].astype(BF16)
        for src, dst in ((wg_ref, og), (wu_ref, ou)):
            dst[:, 0:fs] = src[...].astype(BF16)
            if fp > fs:
                dst[:, fs:fp] = jnp.zeros((d, fp - fs), BF16)
        od[0:fs, :] = wd_ref[...].astype(BF16)
        if fp > fs:
            od[fs:fp, :] = jnp.zeros((fp - fs, d), BF16)
        oc[0:kw, :] = cw_ref[...]
        oc[kw:kwp, :] = jnp.zeros((kwp - kw, cshard), F32)

    ins = (w_in, w_out, w_gate, w_up, w_down, conv_w)
    outs = [jax.ShapeDtypeStruct((4,) + w_in.shape, BF16), jax.ShapeDtypeStruct((4,) + w_out.shape, BF16),
            jax.ShapeDtypeStruct((4, d, fp), BF16), jax.ShapeDtypeStruct((4, d, fp), BF16), jax.ShapeDtypeStruct((4, fp, d), BF16),
            jax.ShapeDtypeStruct((4, kwp, cshard), F32)]
    grid_spec = pltpu.PrefetchScalarGridSpec(
        num_scalar_prefetch=1, grid=(1,),
        in_specs=[pl.BlockSpec(a.shape, lambda i, me: (0, 0)) for a in ins],
        out_specs=[pl.BlockSpec((None,) + o.shape[1:], lambda i, me: (me[0], 0, 0)) for o in outs])
    return pl.pallas_call(body, name="wprep", grid_spec=grid_spec, out_shape=outs, compiler_params=_cparams())(me_arr, *ins)


def _coords():
    return tuple(lax.axis_index(a) for a in MESH_AXES)


def _other_chips(x, y):
    return [(1 - x, y), (x, 1 - y), (1 - x, 1 - y)]


def _remote(src, dst, send_sem, recv_sem, to):
    return pltpu.make_async_remote_copy(src_ref=src, dst_ref=dst, send_sem=send_sem, recv_sem=recv_sem,
                                        device_id=to, device_id_type=MESH_ID)


def _hbm_call(body, name, ins, out_shape, scratch_shapes, aliases=None):
    return pl.pallas_call(
        body, name=name, in_specs=[ANY] * len(ins), out_specs=[ANY] * len(out_shape), out_shape=out_shape,
        scratch_shapes=scratch_shapes, input_output_aliases=aliases or {},
    )(*ins)


def _gather_shards(bufs):
    n = len(bufs)

    def body(*refs):
        dst = refs[n:2 * n]
        send_sems, recv_sems = refs[2 * n:]
        x, y, c = _coords()
        me, sib = 2 * x + y, (x, y, 1 - c)
        chips = _other_chips(x, y)

        def half(a, slot, hc):
            hr = bufs[a].shape[1] // 2
            return dst[a].at[slot, pl.ds(hc * hr, hr)]

        sends = []
        for a in range(n):
            for j, chip in enumerate(chips):
                sends.append(_remote(half(a, me, c), half(a, me, c), send_sems.at[a, j], recv_sems.at[a, j], (*chip, c)))
                sends[-1].start()
        for a in range(n):
            for j, chip in enumerate(chips):
                landed = half(a, 2 * chip[0] + chip[1], c)
                _remote(landed, landed, send_sems.at[a, j], recv_sems.at[a, j], (*chip, c)).wait_recv()
                sends.append(_remote(landed, landed, send_sems.at[a, 3 + j], recv_sems.at[a, 3 + j], sib))
                sends[-1].start()
        for a in range(n):
            for j, chip in enumerate(chips):
                other = half(a, 2 * chip[0] + chip[1], 1 - c)
                _remote(other, other, send_sems.at[a, 3 + j], recv_sems.at[a, 3 + j], sib).wait_recv()
        for cp in sends:
            cp.wait_send()

    return _hbm_call(body, "gather_shards", bufs, [jax.ShapeDtypeStruct(s.shape, s.dtype) for s in bufs],
                     [pltpu.SemaphoreType.DMA((n, 6)), pltpu.SemaphoreType.DMA((n, 6))], aliases={a: a for a in range(n)})


def _pair_swap(arrs):
    n = len(arrs)

    def body(*refs):
        src, land = refs[:n], refs[n:2 * n]
        send_sems, recv_sems = refs[2 * n:]
        x, y, c = _coords()
        copies = []
        for a in range(n):
            s = src[a].at[pl.ds(0, arrs[a].shape[0]), 1 - c] if arrs[a].ndim == 4 else src[a].at[1 - c]
            copies.append(_remote(s, land[a], send_sems.at[a], recv_sems.at[a], (x, y, 1 - c)))
            copies[-1].start()
        for cp in copies:
            cp.wait()

    outs = [jax.ShapeDtypeStruct(s.shape[:-3] + s.shape[-2:], s.dtype) for s in arrs]
    return _hbm_call(body, "pair_swap", arrs, outs, [pltpu.SemaphoreType.DMA((n,)), pltpu.SemaphoreType.DMA((n,))])


def _chip_exchange(arrs):
    n = len(arrs)

    def body(*refs):
        src, dst = refs[:n], refs[n:2 * n]
        send_sems, recv_sems = refs[2 * n:]
        x, y, c = _coords()
        me = 2 * x + y
        chips = _other_chips(x, y)
        part = lambda a, slot: src[a].at[slot] if arrs[a].ndim == 3 else src[a]
        sends = []
        for a in range(n):
            for j, chip in enumerate(chips):
                sends.append(_remote(part(a, 2 * chip[0] + chip[1]), dst[a].at[me], send_sems.at[a, j], recv_sems.at[a, j], (*chip, c)))
                sends[-1].start()
        for a in range(n):
            for j, chip in enumerate(chips):
                landed = dst[a].at[2 * chip[0] + chip[1]]
                _remote(landed, landed, send_sems.at[a, j], recv_sems.at[a, j], (*chip, c)).wait_recv()
        for cp in sends:
            cp.wait_send()

    outs = [jax.ShapeDtypeStruct((4,) + s.shape[-2:], s.dtype) for s in arrs]
    return _hbm_call(body, "chip_exchange", arrs, outs, [pltpu.SemaphoreType.DMA((n, 3)), pltpu.SemaphoreType.DMA((n, 3))])


def _pair_gather(halves):
    n = len(halves)

    def body(*refs):
        src, dst = refs[:n], refs[n:2 * n]
        send_sems, recv_sems = refs[2 * n:]
        x, y, c = _coords()
        copies = [_remote(src[a], dst[a], send_sems.at[a], recv_sems.at[a], (x, y, 1 - c)) for a in range(n)]
        for cp in copies:
            cp.start()
        for cp in copies:
            cp.wait()

    outs = [jax.ShapeDtypeStruct(s.shape, s.dtype) for s in halves]
    return _hbm_call(body, "pair_gather", halves, outs, [pltpu.SemaphoreType.DMA((n,)), pltpu.SemaphoreType.DMA((n,))])


def _pair_sum(a, g, land, c_arr, out_dtype):
    nq, _, hr, cc = g.shape

    def body(c_ref, g_ref, l_ref, o_ref):
        o_ref[...] = (g_ref[...] + l_ref[...]).astype(out_dtype)

    spec = pl.BlockSpec((None, hr, cc), lambda qi, cr: (qi, 0, 0))
    grid_spec = pltpu.PrefetchScalarGridSpec(
        num_scalar_prefetch=1, grid=(nq,),
        in_specs=[pl.BlockSpec((None, None, hr, cc), lambda qi, cr: (qi, cr[0], 0, 0)), spec], out_specs=spec)
    return pl.pallas_call(body, name=f"pair_sum_{a}", grid_spec=grid_spec, out_shape=jax.ShapeDtypeStruct((nq, hr, cc), out_dtype),
                          compiler_params=_cparams())(c_arr, g, land)


def _chip_sum(a, parts, own, me_arr):
    _, hr, cc = parts.shape

    def body(me_ref, p_ref, own_ref, o_ref):
        for mine in range(4):
            @pl.when(me_ref[0] == mine)
            def _():
                term = lambda j: (own_ref if j == mine else p_ref.at[j])[...].astype(F32)
                o_ref[...] = ((term(0) + term(1)) + term(2)) + term(3)

    own_spec = (pl.BlockSpec((None, hr, cc), lambda i, me: (me[0], 0, 0)) if own.ndim == 3
                else pl.BlockSpec((hr, cc), lambda i, me: (0, 0)))
    grid_spec = pltpu.PrefetchScalarGridSpec(
        num_scalar_prefetch=1, grid=(1,),
        in_specs=[pl.BlockSpec((4, hr, cc), lambda i, me: (0, 0, 0)), own_spec],
        out_specs=pl.BlockSpec((hr, cc), lambda i, me: (0, 0)))
    return pl.pallas_call(body, name=f"chip_sum_{a}", grid_spec=grid_spec, out_shape=jax.ShapeDtypeStruct((hr, cc), F32),
                          compiler_params=_cparams())(me_arr, parts, own)


def _row_block(rows, cols, limit=1 << 20):
    best = 8
    for tr in range(8, rows + 1, 8):
        if rows % tr == 0 and tr * cols * 4 <= limit:
            best = tr
    return best


def _adamw(name, w, g_mine, g_other, m, v, c_arr):
    r, c = w.shape
    hr, cg = g_mine.shape
    tr = math.gcd(_row_block(hr, cg), r)
    per_half = hr // tr
    bc1 = 1.0 - ADAM_B1 ** ADAM_STEP
    bc2 = 1.0 - ADAM_B2 ** ADAM_STEP

    def body(c_ref, w_ref, gm_ref, go_ref, m_ref, v_ref, go, do, mo, vo):
        gv = jnp.where(pl.program_id(0) // per_half == c_ref[0], gm_ref[:, 0:c], go_ref[:, 0:c])
        mn = ADAM_B1 * m_ref[...] + (1.0 - ADAM_B1) * gv
        vn = ADAM_B2 * v_ref[...] + (1.0 - ADAM_B2) * (gv * gv)
        go[...] = gv
        mo[...] = mn
        vo[...] = vn
        do[...] = -ADAM_LR * ((mn / bc1) / (jnp.sqrt(vn / bc2) + ADAM_EPS) + ADAM_WD * w_ref[...])

    blk = pl.BlockSpec((tr, c), lambda i, cr: (i, 0))
    gblk = pl.BlockSpec((tr, cg), lambda i, cr: (i % per_half, 0))
    grid_spec = pltpu.PrefetchScalarGridSpec(num_scalar_prefetch=1, grid=(r // tr,), in_specs=[blk, gblk, gblk, blk, blk],
                                             out_specs=[blk] * 4)
    return pl.pallas_call(body, name=f"adamw_{name}", grid_spec=grid_spec, out_shape=[jax.ShapeDtypeStruct((r, c), F32)] * 4,
                          compiler_params=_cparams())(c_arr, w, g_mine, g_other, m, v)


def _rows128(a):
    return a.reshape(-1, LANES)


def _pad_rows(a, rows):
    return jnp.pad(a, ((0, rows - a.shape[0]), (0, 0)))


def kernel(x, w_in, sgu_ln_g, sgu_ln_b, w_s, b_s, conv_w, conv_b, conv_ln_g, conv_ln_b, w_out, ln1_g, ln1_b, w_gate, w_up, w_down, ln2_g, ln2_b, loss_target, m_w_in, m_sgu_ln_g, m_sgu_ln_b, m_w_s, m_b_s, m_conv_w, m_conv_b, m_conv_ln_g, m_conv_ln_b, m_w_out, m_ln1_g, m_ln1_b, m_w_gate, m_w_up, m_w_down, m_ln2_g, m_ln2_b, v_w_in, v_sgu_ln_g, v_sgu_ln_b, v_w_s, v_b_s, v_conv_w, v_conv_b, v_conv_ln_g, v_conv_ln_b, v_w_out, v_ln1_g, v_ln1_b, v_w_gate, v_w_up, v_w_down, v_ln2_g, v_ln2_b):
    depth, d, q = w_in.shape
    assert depth == 1 and x.shape[0] == 1
    t = x.shape[1]
    heads = w_s.shape[1]
    kw, cshard = conv_w.shape[1], conv_w.shape[2]
    fs = w_gate.shape[2]
    fp = -(-fs // MXU_N) * MXU_N
    n_pairs = q // LANES
    assert heads * HEAD_DIM == q and q % LANES == 0 and w_s.shape[2] == CHUNK and 4 * cshard == q and kw - 1 <= HALO
    alpha = (2.0 * depth) ** 0.25
    tm = min(512, t)
    assert t % tm == 0 and tm % CHUNK == 0
    x2, tgt = x[0], loss_target[0]
    mx, my, mc = _coords()
    me = 2 * mx + my
    c_arr = jnp.reshape(mc, (1,)).astype(jnp.int32)

    kwp = -(-kw // 16) * 16
    me_arr = jnp.reshape(me, (1,)).astype(jnp.int32)
    wi, wo, wg, wu, wd, cw4 = _gather_shards(_prep(me_arr, w_in[0], w_out[0], w_gate[0], w_up[0], w_down[0], conv_w[0], fp, kwp))
    wo = wo.reshape(d, d)
    cw = jnp.transpose(cw4, (1, 0, 2)).reshape(kwp, q)
    cwf = _pad_rows(cw[:kw][::-1], kwp)

    wm = jnp.where(jnp.tril(jnp.ones((CHUNK, CHUNK), bool)), w_s[0], 0.0)
    wst = wm.reshape(n_pairs, 2 * CHUNK, CHUNK).astype(BF16)
    wstt = jnp.transpose(wm, (0, 2, 1)).reshape(n_pairs, 2 * CHUNK, CHUNK).astype(BF16)
    bmat = jnp.repeat(b_s[0].T, HEAD_DIM, axis=1)
    vq = _pad_rows(jnp.concatenate([sgu_ln_g, sgu_ln_b, conv_b, conv_ln_g, conv_ln_b], axis=0), 8)
    vd = _pad_rows(jnp.concatenate([ln1_g, ln1_b, ln2_g, ln2_b], axis=0), 8)

    saved = _fwd_mix(x2, wi, wo, wst, bmat, cw, vq, vd, alpha, kw, tm)
    gp, up, x1b, dr2, loss_part, dg2, db2 = _fwd_mlp(saved[3], tgt, wg, wu, wd, vd, alpha, tm)
    mlp_grads = None
    for k in range(4):
        mlp_grads = _bwd_mlp_slab(k, dr2, mlp_grads, x1b, gp, up, wg, wu, wd, alpha, tm)
    dx1, dwg, dwu, dwd = mlp_grads
    grad_x, dwi, dwo, dws, dbs, dcw, dvq, dvd = _bwd_mix(dx1, saved, wi, wo, wstt, cw, cwf, vq, vd, alpha, kw, min(tm, 256))
    loss = lax.psum(loss_part[0, 0], MESH_AXES)

    dws = jnp.where(jnp.tril(jnp.ones((CHUNK, CHUNK), bool)), dws.reshape(heads, CHUNK, CHUNK), 0.0)
    dvd = dvd.at[VD_LN2_G].set(dg2[0]).at[VD_LN2_B].set(db2[0])
    pieces = [_rows128(dws), dbs[:, :heads].T, _rows128(dcw), _rows128(dvq), _rows128(dvd)]
    sizes = [p.shape[0] for p in pieces]
    rows = -(-sum(sizes) // 16) * 16
    small = _pad_rows(jnp.concatenate(pieces, axis=0), rows)

    big = [dwi, dwo.reshape(4, d // 4, d), dwg, dwu, dwd]
    halves = [b.reshape(4, 2, b.shape[1] // 2, b.shape[2]) for b in big] + [small.reshape(2, rows // 2, LANES)]
    landed = _pair_swap(halves)
    sums = [_pair_sum(a, h, l, c_arr, BF16) for a, (h, l) in enumerate(zip(halves[:-1], landed[:-1]))]
    sums.append(_pair_sum(len(big), halves[-1][None], landed[-1][None], c_arr, F32)[0])
    parts = _chip_exchange(sums)
    mine = [_chip_sum(a, p, s, me_arr) for a, (p, s) in enumerate(zip(parts, sums))]
    other = _pair_gather(mine)

    def pack(ws, bs, vqs, vds):
        ps = [_rows128(ws[0]), bs[0], jnp.zeros((sizes[2], LANES), F32),
              _rows128(_pad_rows(jnp.concatenate(vqs, axis=0), 8)), _rows128(_pad_rows(jnp.concatenate(vds, axis=0), 8))]
        return _pad_rows(jnp.concatenate(ps, axis=0), rows)

    out = {}
    for a, (nm, w_, m_, v_) in enumerate((("w_in", w_in, m_w_in, v_w_in), ("w_out", w_out, m_w_out, v_w_out),
                                          ("w_gate", w_gate, m_w_gate, v_w_gate), ("w_up", w_up, m_w_up, v_w_up),
                                          ("w_down", w_down, m_w_down, v_w_down))):
        out[nm] = _adamw(nm, w_[0], mine[a], other[a], m_[0], v_[0], c_arr)
    packed = _adamw(
        "small",
        pack(w_s, b_s, [sgu_ln_g, sgu_ln_b, conv_b, conv_ln_g, conv_ln_b], [ln1_g, ln1_b, ln2_g, ln2_b]), mine[-1], other[-1],
        pack(m_w_s, m_b_s, [m_sgu_ln_g, m_sgu_ln_b, m_conv_b, m_conv_ln_g, m_conv_ln_b], [m_ln1_g, m_ln1_b, m_ln2_g, m_ln2_b]),
        pack(v_w_s, v_b_s, [v_sgu_ln_g, v_sgu_ln_b, v_conv_b, v_conv_ln_g, v_conv_ln_b], [v_ln1_g, v_ln1_b, v_ln2_g, v_ln2_b]),
        c_arr)

    offs = [sum(sizes[:i]) for i in range(len(sizes))]
    g_cw_full = packed[0][offs[2]:offs[2] + sizes[2]].reshape(kwp, q)
    g_cw = lax.dynamic_slice(g_cw_full, (0, me * cshard), (kwp, cshard))
    out["conv_w"] = _adamw("conv_w", _pad_rows(conv_w[0], kwp), g_cw, g_cw, _pad_rows(m_conv_w[0], kwp), _pad_rows(v_conv_w[0], kwp), c_arr)
    out["conv_w"] = [o[:kw] for o in out["conv_w"]]

    def unpack(p):
        vq_o = p[offs[3]:offs[3] + sizes[3]].reshape(8, q)
        vd_o = p[offs[4]:offs[4] + sizes[4]].reshape(8, d)
        return {"w_s": p[offs[0]:offs[0] + sizes[0]].reshape(heads, CHUNK, CHUNK), "b_s": p[offs[1]:offs[1] + sizes[1]],
                "sgu_ln_g": vq_o[VQ_SGU_G], "sgu_ln_b": vq_o[VQ_SGU_B], "conv_b": vq_o[VQ_CONV_B],
                "conv_ln_g": vq_o[VQ_CLN_G], "conv_ln_b": vq_o[VQ_CLN_B],
                "ln1_g": vd_o[VD_LN1_G], "ln1_b": vd_o[VD_LN1_B], "ln2_g": vd_o[VD_LN2_G], "ln2_b": vd_o[VD_LN2_B]}

    small_out = [unpack(p) for p in packed]
    names = ["w_in", "sgu_ln_g", "sgu_ln_b", "w_s", "b_s", "conv_w", "conv_b", "conv_ln_g", "conv_ln_b", "w_out",
             "ln1_g", "ln1_b", "w_gate", "w_up", "w_down", "ln2_g", "ln2_b"]
    result = [loss, grad_x[None]]
    for kind in range(4):
        for nm in names:
            val = out[nm][kind] if nm in out else small_out[kind][nm]
            result.append(val[None])
    return tuple(result)
```

```python
import functools
import math

import jax
import jax.numpy as jnp
from jax import lax
from jax.experimental import pallas as pl
from jax.experimental.pallas import tpu as pltpu

F32 = jnp.float32
BF16 = jnp.bfloat16

LN_EPS = 1e-5
HEAD_DIM = 64
CHUNK = 128
HALO = 32
LANES = 128
MXU_N = 256
ADAM_LR, ADAM_B1, ADAM_B2, ADAM_EPS, ADAM_WD, ADAM_STEP = 0.001, 0.9, 0.999, 1e-08, 0.01, 10
VMEM_LIMIT = 60 * 1024 * 1024
MESH_AXES = ("x", "y", "c")
MESH_ID = pl.DeviceIdType.MESH


def _dot(a, b):
    return jnp.dot(a, b, preferred_element_type=F32)


def _dot_nt(a, b):
    return lax.dot_general(a, b, (((1,), (1,)), ((), ())), preferred_element_type=F32)


def _dot_tn(a, b):
    return lax.dot_general(a, b, (((0,), (0,)), ((), ())), preferred_element_type=F32)


def _sigmoid(v):
    return 1.0 / (1.0 + jnp.exp(-v))


def _gelu(v):
    cdf = 0.5 * (1.0 + lax.erf(v * (1.0 / math.sqrt(2.0))))
    pdf = jnp.exp(-0.5 * v * v) * (1.0 / math.sqrt(2.0 * math.pi))
    return v * cdf, cdf + v * pdf


def _ln_stats(v):
    mu = jnp.mean(v, axis=-1, keepdims=True)
    d = v - mu
    rstd = lax.rsqrt(jnp.mean(d * d, axis=-1, keepdims=True) + LN_EPS)
    return d * rstd, rstd


def _ln_bwd(dxhat, xhat, rstd):
    m1 = jnp.mean(dxhat, axis=-1, keepdims=True)
    m2 = jnp.mean(dxhat * xhat, axis=-1, keepdims=True)
    return rstd * (dxhat - m1 - xhat * m2)


def _colsum(v):
    return jnp.sum(v, axis=0, keepdims=True)


def _pair_lanes(v, nc, p):
    return jnp.concatenate([v[c * CHUNK:(c + 1) * CHUNK, p * LANES:(p + 1) * LANES] for c in range(nc)], axis=1)


def _unpair(parts, nc):
    rows = [jnp.concatenate([part[:, c * LANES:(c + 1) * LANES] for part in parts], axis=1) for c in range(nc)]
    return jnp.concatenate(rows, axis=0)


def _low_head(nc):
    lane = lax.broadcasted_iota(jnp.int32, (CHUNK, nc * LANES), 1)
    return (lane & (LANES - 1)) < HEAD_DIM


def _mix(wst_ref, v, nc, n_pairs):
    vb = v.astype(BF16)
    low = _low_head(nc)
    parts = []
    for p in range(n_pairs):
        r = _dot(wst_ref[p], _pair_lanes(vb, nc, p))
        parts.append(jnp.where(low, r[:CHUNK], r[CHUNK:]))
    return _unpair(parts, nc)


def _mix_wgrad(dm, vn, nc, n_pairs):
    low = _low_head(nc)
    vb = vn.astype(BF16)
    out = []
    for p in range(n_pairs):
        a = _pair_lanes(dm, nc, p)
        lhs = jnp.concatenate([jnp.where(low, a, 0.0), jnp.where(low, 0.0, a)], axis=0).astype(BF16)
        out.append(_dot_nt(lhs, _pair_lanes(vb, nc, p)))
    return out


SUBLANES = 8


def _shift_rows(sh_ref, buf_ref, rows):
    for b in range(SUBLANES):
        sh_ref[b, 0:rows, :] = buf_ref[b:b + rows, :]


def _conv_taps(sh_ref, w_ref, tm, q, off, kw):
    rb = min(tm, 64)
    cols = []
    for l0 in range(0, q, LANES):
        rows = []
        for r0 in range(0, tm, rb):
            acc = None
            for k in range(kw):
                a, b = divmod(off + k, SUBLANES)
                t = w_ref[k:k + 1, l0:l0 + LANES] * sh_ref[b, a * SUBLANES + r0:a * SUBLANES + r0 + rb, l0:l0 + LANES]
                acc = t if acc is None else acc + t
            rows.append(acc)
        cols.append(jnp.concatenate(rows, axis=0))
    return jnp.concatenate(cols, axis=1)


def _conv_wgrad(acc_ref, dy_ref, sh_ref, tm, q, off, kw):
    rb = min(tm, 128)
    for l0 in range(0, q, LANES):
        for k in range(kw):
            a, b = divmod(off + k, SUBLANES)
            part = None
            for r0 in range(0, tm, rb):
                prod = dy_ref[r0:r0 + rb, l0:l0 + LANES] * sh_ref[b, a * SUBLANES + r0:a * SUBLANES + r0 + rb, l0:l0 + LANES]
                prod = jnp.sum(prod.reshape(rb // SUBLANES, SUBLANES, LANES), axis=0)
                part = prod if part is None else part + prod
            acc_ref[k, :, l0:l0 + LANES] += part


def _cparams():
    return pltpu.CompilerParams(dimension_semantics=("arbitrary",), vmem_limit_bytes=VMEM_LIMIT)


def _full(shape):
    return pl.BlockSpec(shape, lambda i: (0,) * len(shape))


ANY = pl.BlockSpec(memory_space=pl.ANY)

VQ_SGU_G, VQ_SGU_B, VQ_CONV_B, VQ_CLN_G, VQ_CLN_B = range(5)
VD_LN1_G, VD_LN1_B, VD_LN2_G, VD_LN2_B = range(4)
RS_LN1, RS_SGU, RS_CONV = range(3)
RS_COLS = 8


def _fwd_mix(x, wi, wo, wst, bmat, cw, vq, vd, mlp_w, alpha, kw, tm):
    t, d = x.shape
    q = wi.shape[2]
    nc, n_pairs = tm // CHUNK, q // LANES
    n = t // tm
    n_in, n_saved = 8, 11

    def body(x_ref, wi_hbm, wo_hbm, wst_ref, bmat_ref, cw_ref, vq_ref, vd_ref, *rest):
        (xb_ref, pag_ref, y_ref, xh_ref, rs_ref, zu_ref, mg_ref, vhat_ref, gv_ref, vnb_ref, yhat_ref) = rest[3:3 + n_saved]
        gathered = rest[3 + n_saved:6 + n_saved]
        wi_v, wo_v, hb_ref, sh_ref, send_sems, recv_sems = rest[6 + n_saved:]
        step = pl.program_id(0)

        @pl.when(step == 0)
        def _():
            _Gather(gathered, send_sems, recv_sems).start()
            pltpu.sync_copy(wi_hbm, wi_v)
            pltpu.sync_copy(wo_hbm, wo_v)
            hb_ref[...] = jnp.zeros_like(hb_ref)

        @pl.when(step == (3 * n) // 4)
        def _():
            _Gather(gathered, send_sems, recv_sems).forward()

        xv = x_ref[...]
        xb = xv.astype(BF16)
        xb_ref[...] = xb
        pu, pv, pa, pg = (_dot(xb, wi_v[j]) for j in range(4))
        pag_ref[:, 0:q] = pa.astype(BF16)
        pag_ref[:, q:2 * q] = pg.astype(BF16)
        zu, gu = _gelu(pu)
        zv, gv = _gelu(pv)
        vhat, rstd_v = _ln_stats(zv)
        vnb = (vhat * vq_ref[VQ_SGU_G:VQ_SGU_G + 1, :] + vq_ref[VQ_SGU_B:VQ_SGU_B + 1, :]).astype(BF16)
        mixed = _mix(wst_ref, vnb, nc, n_pairs) + jnp.concatenate([bmat_ref[...]] * nc, axis=0)
        y_ref[:, 0:q] = (zu * mixed).astype(BF16)
        zu_ref[...] = zu
        mg_ref[...] = mixed * gu
        vhat_ref[...] = vhat
        gv_ref[...] = gv
        vnb_ref[...] = vnb

        hb_ref[HALO:HALO + tm, :] = pa * _sigmoid(pg)
        _shift_rows(sh_ref, hb_ref, tm + HALO)
        yc = _conv_taps(sh_ref, cw_ref, tm, q, HALO - (kw - 1), kw) + vq_ref[VQ_CONV_B:VQ_CONV_B + 1, :]
        hb_ref[0:HALO, :] = hb_ref[tm:tm + HALO, :]
        yhat, rstd_c = _ln_stats(yc)
        yhat_ref[...] = yhat
        yn = yhat * vq_ref[VQ_CLN_G:VQ_CLN_G + 1, :] + vq_ref[VQ_CLN_B:VQ_CLN_B + 1, :]
        y_ref[:, q:2 * q] = (yn * _sigmoid(yn)).astype(BF16)

        r1 = alpha * xv + _dot(y_ref[...], wo_v[...])
        xhat, rstd1 = _ln_stats(r1)
        xh_ref[...] = xhat
        col = lax.broadcasted_iota(jnp.int32, (tm, RS_COLS), 1)
        rs_ref[...] = jnp.where(col == RS_LN1, rstd1, jnp.where(col == RS_SGU, rstd_v, jnp.where(col == RS_CONV, rstd_c, 0.0)))

        @pl.when(step == n - 1)
        def _():
            _Gather(gathered, send_sems, recv_sems).finish()

    row = lambda w: pl.BlockSpec((tm, w), lambda i: (i, 0))
    widths = [(d, BF16), (2 * q, BF16), (d, BF16), (d, F32), (RS_COLS, F32), (q, F32), (q, F32), (q, F32), (q, F32), (q, BF16), (q, F32)]
    assert len(widths) == n_saved
    return pl.pallas_call(
        body, name="fwd_mix", grid=(n,),
        in_specs=[row(d), ANY, ANY, _full(wst.shape), _full(bmat.shape), _full(cw.shape), _full(vq.shape), _full(vd.shape)] + [ANY] * 3,
        out_specs=[row(w) for w, _ in widths] + [ANY] * 3,
        out_shape=[jax.ShapeDtypeStruct((t, w), dt) for w, dt in widths] + [jax.ShapeDtypeStruct(b.shape, b.dtype) for b in mlp_w],
        scratch_shapes=[pltpu.VMEM(wi.shape, BF16), pltpu.VMEM(wo.shape, BF16), pltpu.VMEM((HALO + tm + SUBLANES, q), F32),
                        pltpu.VMEM((SUBLANES, tm + HALO, q), F32)] + _gather_sems(3),
        input_output_aliases={n_in + a: n_saved + a for a in range(3)},
        compiler_params=_cparams(),
    )(x, wi, wo, wst, bmat, cw, vq, vd, *mlp_w)


def _fwd_mlp(xh1, tgt, wg, wu, wd, vd, alpha, tm):
    t, d = xh1.shape
    ns, _, fp = wg.shape
    n = t // tm

    def body(xh_ref, tgt_ref, wg_hbm, wu_hbm, wd_hbm, vd_ref,
             gp_ref, up_ref, x1b_ref, dr2_ref, loss_ref, dg2_ref, db2_ref, wg_v, wu_v, wd_v):
        @pl.when(pl.program_id(0) == 0)
        def _():
            pltpu.sync_copy(wg_hbm, wg_v)
            pltpu.sync_copy(wu_hbm, wu_v)
            pltpu.sync_copy(wd_hbm, wd_v)
            loss_ref[...] = jnp.zeros_like(loss_ref)
            dg2_ref[...] = jnp.zeros_like(dg2_ref)
            db2_ref[...] = jnp.zeros_like(db2_ref)

        x1 = xh_ref[...] * vd_ref[VD_LN1_G:VD_LN1_G + 1, :] + vd_ref[VD_LN1_B:VD_LN1_B + 1, :]
        x1b = x1.astype(BF16)
        x1b_ref[...] = x1b
        acc = alpha * x1
        for k in range(ns):
            gp = _dot(x1b, wg_v[k])
            up = _dot(x1b, wu_v[k])
            gp_ref[:, k * fp:(k + 1) * fp] = gp.astype(BF16)
            up_ref[:, k * fp:(k + 1) * fp] = up.astype(BF16)
            acc = acc + _dot((gp * _sigmoid(gp) * up).astype(BF16), wd_v[k])
        xh2, rstd2 = _ln_stats(acc)
        g2 = vd_ref[VD_LN2_G:VD_LN2_G + 1, :]
        err = xh2 * g2 + vd_ref[VD_LN2_B:VD_LN2_B + 1, :] - tgt_ref[...]
        loss_ref[...] += _colsum(jnp.sum(err * err, axis=1, keepdims=True)) * (0.5 / d)
        dy = err * (1.0 / d)
        dg2_ref[...] += _colsum(dy * xh2)
        db2_ref[...] += _colsum(dy)
        dr2_ref[...] = _ln_bwd(dy * g2, xh2, rstd2)

    row = lambda w: pl.BlockSpec((tm, w), lambda i: (i, 0))
    return pl.pallas_call(
        body, name="fwd_mlp", grid=(n,),
        in_specs=[row(d), row(d), ANY, ANY, ANY, _full(vd.shape)],
        out_specs=[row(ns * fp), row(ns * fp), row(d), row(d), _full((8, LANES)), _full((1, d)), _full((1, d))],
        out_shape=[jax.ShapeDtypeStruct((t, ns * fp), BF16), jax.ShapeDtypeStruct((t, ns * fp), BF16),
                   jax.ShapeDtypeStruct((t, d), BF16), jax.ShapeDtypeStruct((t, d), F32),
                   jax.ShapeDtypeStruct((8, LANES), F32), jax.ShapeDtypeStruct((1, d), F32), jax.ShapeDtypeStruct((1, d), F32)],
        scratch_shapes=[pltpu.VMEM(wg.shape, BF16), pltpu.VMEM(wu.shape, BF16), pltpu.VMEM(wd.shape, BF16)],
        compiler_params=_cparams(),
    )(xh1, tgt, wg, wu, wd, vd)


def _bwd_mlp_slab(k, dr2, prev, x1b, gp, up, wg, wu, wd, alpha, tm):
    t, d = dr2.shape
    ns, _, fp = wg.shape
    n = t // tm
    first = prev is None

    def body(*refs):
        if first:
            dr_ref, x1b_ref, gp_ref, up_ref, wg_ref, wu_ref, wd_ref, dx_ref, dwg_hbm, dwu_hbm, dwd_hbm, ag, au, ad = refs
        else:
            (dr_ref, dxp_ref, x1b_ref, gp_ref, up_ref, wg_ref, wu_ref, wd_ref, _, _, _,
             dx_ref, dwg_hbm, dwu_hbm, dwd_hbm, ag, au, ad) = refs

        @pl.when(pl.program_id(0) == 0)
        def _():
            ag[...] = jnp.zeros_like(ag)
            au[...] = jnp.zeros_like(au)
            ad[...] = jnp.zeros_like(ad)

        dr = dr_ref[...]
        drb = dr.astype(BF16)
        x1b = x1b_ref[...]
        gpv = gp_ref[...].astype(F32)
        upv = up_ref[...].astype(F32)
        dh = _dot_nt(drb, wd_ref[0])
        sg = _sigmoid(gpv)
        silu = gpv * sg
        ad[...] += _dot_tn((silu * upv).astype(BF16), drb)
        dgp = (dh * upv * (sg * (1.0 + gpv * (1.0 - sg)))).astype(BF16)
        dup = (dh * silu).astype(BF16)
        ag[...] += _dot_tn(x1b, dgp)
        au[...] += _dot_tn(x1b, dup)
        base = alpha * dr if first else dxp_ref[...]
        dx_ref[...] = base + _dot_nt(dgp, wg_ref[0]) + _dot_nt(dup, wu_ref[0])

        @pl.when(pl.program_id(0) == n - 1)
        def _():
            pltpu.sync_copy(ag, dwg_hbm.at[k])
            pltpu.sync_copy(au, dwu_hbm.at[k])
            pltpu.sync_copy(ad, dwd_hbm.at[k])

    row = lambda w: pl.BlockSpec((tm, w), lambda i: (i, 0))
    slab = pl.BlockSpec((tm, fp), lambda i: (i, k))
    wcol = pl.BlockSpec((1, d, fp), lambda i: (k, 0, 0))
    wrow = pl.BlockSpec((1, fp, d), lambda i: (k, 0, 0))
    ins = [dr2] + ([] if first else [prev[0]]) + [x1b, gp, up, wg, wu, wd] + ([] if first else list(prev[1:]))
    in_specs = [row(d)] + ([] if first else [row(d)]) + [row(d), slab, slab, wcol, wcol, wrow] + ([] if first else [ANY] * 3)
    return pl.pallas_call(
        body, name=f"bwd_mlp_{k}", grid=(n,),
        in_specs=in_specs,
        out_specs=[row(d), ANY, ANY, ANY],
        out_shape=[jax.ShapeDtypeStruct((t, d), F32), jax.ShapeDtypeStruct((ns, d, fp), F32),
                   jax.ShapeDtypeStruct((ns, d, fp), F32), jax.ShapeDtypeStruct((ns, fp, d), F32)],
        scratch_shapes=[pltpu.VMEM((d, fp), F32), pltpu.VMEM((d, fp), F32), pltpu.VMEM((fp, d), F32)],
        input_output_aliases={} if first else {8: 1, 9: 2, 10: 3},
        compiler_params=_cparams(),
    )(*ins)


def _bwd_mix(dx1, saved, wi, wo, wstt, cw, cwf, vq, vd, mlp_sums, alpha, kw, tm):
    xb, pag, y, xh1, rs, zu_s, mg_s, vhat_s, gv_s, vnb_s, yhat_s = saved
    t, d = xh1.shape
    q = wi.shape[2]
    nc, n_pairs = tm // CHUNK, q // LANES
    n = t // tm
    hb_per_tile = tm // HALO
    off = HALO - (kw - 1)

    def body(dx1_ref, xb_ref, pag_ref, halo_ref, y_ref, xh_ref, rs_ref, zu_ref, mg_ref, vhat_ref, gv_ref, vnb_ref, yhat_ref,
             wi_hbm, wo_hbm, wstt_ref, cw_ref, cwf_ref, vq_ref, vd_ref, sum_g, sum_u, sum_d,
             gx_ref, dwi_hbm, dwo_hbm, dws_ref, dbs_ref, dcw_ref, dvq_ref, dvd_ref, got_g, got_u, got_d,
             wi_v, wo_v, awi, awo, hb_ref, dyb_ref, sh_ref, dbm_ref, dcw8_ref, send_sems, recv_sems):
        i = pl.program_id(0)
        exchange = lambda: _Exchange((sum_g, sum_u, sum_d), (got_g, got_u, got_d), send_sems, recv_sems)

        @pl.when(i == 0)
        def _():
            exchange().start()
            pltpu.sync_copy(wi_hbm, wi_v)
            pltpu.sync_copy(wo_hbm, wo_v)
            for r in (awi, awo, dws_ref, dbm_ref, dcw8_ref, dvq_ref, dvd_ref, hb_ref, dyb_ref):
                r[...] = jnp.zeros_like(r)

        dx1v = dx1_ref[...]
        xh = xh_ref[...]
        rsv = rs_ref[...]
        dvd_ref[VD_LN1_G:VD_LN1_G + 1, :] += _colsum(dx1v * xh)
        dvd_ref[VD_LN1_B:VD_LN1_B + 1, :] += _colsum(dx1v)
        dr1 = _ln_bwd(dx1v * vd_ref[VD_LN1_G:VD_LN1_G + 1, :], xh, rsv[:, RS_LN1:RS_LN1 + 1])
        dr1b = dr1.astype(BF16)
        awo[...] += _dot_tn(y_ref[...], dr1b)
        dy = _dot_nt(dr1b, wo_v[...])

        vhat = vhat_ref[...]
        sgu_g = vq_ref[VQ_SGU_G:VQ_SGU_G + 1, :]
        doa = dy[:, 0:q]
        dm = doa * zu_ref[...]
        dpu = (doa * mg_ref[...]).astype(BF16)
        acc = dm[0:CHUNK]
        for c in range(1, nc):
            acc = acc + dm[c * CHUNK:(c + 1) * CHUNK]
        dbm_ref[...] += acc
        for p, g in enumerate(_mix_wgrad(dm, vnb_ref[...], nc, n_pairs)):
            dws_ref[p] += g
        dvn = _mix(wstt_ref, dm, nc, n_pairs)
        dvq_ref[VQ_SGU_G:VQ_SGU_G + 1, :] += _colsum(dvn * vhat)
        dvq_ref[VQ_SGU_B:VQ_SGU_B + 1, :] += _colsum(dvn)
        dpv = (_ln_bwd(dvn * sgu_g, vhat, rsv[:, RS_SGU:RS_SGU + 1]) * gv_ref[...]).astype(BF16)

        pa = pag_ref[:, 0:q].astype(F32)
        sg = _sigmoid(pag_ref[:, q:2 * q].astype(F32))
        hb_ref[HALO:HALO + tm, :] = pa * sg
        ha = halo_ref[:, 0:q].astype(F32)
        hg = halo_ref[:, q:2 * q].astype(F32)
        hb_ref[0:HALO, :] = jnp.where(i == n - 1, 0.0, ha * _sigmoid(hg))
        yhat = yhat_ref[...]
        cln_g = vq_ref[VQ_CLN_G:VQ_CLN_G + 1, :]
        yn = yhat * cln_g + vq_ref[VQ_CLN_B:VQ_CLN_B + 1, :]
        sy = _sigmoid(yn)
        dyn = dy[:, q:2 * q] * (sy * (1.0 + yn * (1.0 - sy)))
        dvq_ref[VQ_CLN_G:VQ_CLN_G + 1, :] += _colsum(dyn * yhat)
        dvq_ref[VQ_CLN_B:VQ_CLN_B + 1, :] += _colsum(dyn)
        dyc = _ln_bwd(dyn * cln_g, yhat, rsv[:, RS_CONV:RS_CONV + 1])
        dvq_ref[VQ_CONV_B:VQ_CONV_B + 1, :] += _colsum(dyc)
        dyb_ref[0:tm, :] = dyc
        _shift_rows(sh_ref, hb_ref, tm + HALO)
        _conv_wgrad(dcw8_ref, dyb_ref, sh_ref, tm, q, off, kw)
        _shift_rows(sh_ref, dyb_ref, tm + HALO)
        dh = _conv_taps(sh_ref, cwf_ref, tm, q, 0, kw)
        dyb_ref[tm:tm + HALO, :] = dyb_ref[0:HALO, :]
        da = (dh * sg).astype(BF16)
        dg = (dh * pa * (sg * (1.0 - sg))).astype(BF16)

        xb = xb_ref[...]
        gx = alpha * dr1
        for j, dpj in enumerate((dpu, dpv, da, dg)):
            awi[j] += _dot_tn(xb, dpj)
            gx = gx + _dot_nt(dpj, wi_v[j])
        gx_ref[...] = gx

        @pl.when(i == n - 1)
        def _():
            pltpu.sync_copy(awi, dwi_hbm)
            pltpu.sync_copy(awo, dwo_hbm)
            lane = lax.broadcasted_iota(jnp.int32, (CHUNK, LANES), 1)
            low = lane < HEAD_DIM
            dbs = jnp.zeros((CHUNK, LANES), F32)
            for p in range(n_pairs):
                grp = dbm_ref[:, p * LANES:(p + 1) * LANES]
                dbs = jnp.where(lane == 2 * p, jnp.sum(jnp.where(low, grp, 0.0), axis=1, keepdims=True), dbs)
                dbs = jnp.where(lane == 2 * p + 1, jnp.sum(jnp.where(low, 0.0, grp), axis=1, keepdims=True), dbs)
            dbs_ref[...] = dbs
            for k in range(cw.shape[0]):
                dcw_ref[k:k + 1, :] = _colsum(dcw8_ref[k])
            exchange().finish()

    rev = lambda w: pl.BlockSpec((tm, w), lambda i: (n - 1 - i, 0))
    halo = pl.BlockSpec((HALO, 2 * q), lambda i: (jnp.maximum((n - 1 - i) * hb_per_tile - 1, 0), 0))
    small = [jax.ShapeDtypeStruct((n_pairs, 2 * CHUNK, CHUNK), F32), jax.ShapeDtypeStruct((CHUNK, LANES), F32),
             jax.ShapeDtypeStruct(cw.shape, F32), jax.ShapeDtypeStruct(vq.shape, F32), jax.ShapeDtypeStruct(vd.shape, F32)]
    return pl.pallas_call(
        body, name="bwd_mix", grid=(n,),
        in_specs=[rev(d), rev(d), rev(2 * q), halo, rev(d), rev(d), rev(RS_COLS), rev(q), rev(q), rev(q), rev(q), rev(q), rev(q),
                  ANY, ANY, _full(wstt.shape), _full(cw.shape), _full(cwf.shape), _full(vq.shape), _full(vd.shape)] + [ANY] * 3,
        out_specs=[rev(d), ANY, ANY] + [_full(s.shape) for s in small] + [ANY] * 3,
        out_shape=[jax.ShapeDtypeStruct((t, d), F32), jax.ShapeDtypeStruct(wi.shape, F32), jax.ShapeDtypeStruct(wo.shape, F32)] + small
        + _exchange_shapes(mlp_sums),
        scratch_shapes=[pltpu.VMEM(wi.shape, BF16), pltpu.VMEM(wo.shape, BF16), pltpu.VMEM(wi.shape, F32), pltpu.VMEM(wo.shape, F32),
                        pltpu.VMEM((HALO + tm + SUBLANES, q), F32), pltpu.VMEM((tm + HALO + SUBLANES, q), F32),
                        pltpu.VMEM((SUBLANES, tm + HALO, q), F32), pltpu.VMEM((CHUNK, q), F32),
                        pltpu.VMEM((cw.shape[0], SUBLANES, q), F32)] + _exchange_sems(3),
        compiler_params=_cparams(),
    )(dx1, xb, pag, pag, y, xh1, rs, zu_s, mg_s, vhat_s, gv_s, vnb_s, yhat_s, wi, wo, wstt, cw, cwf, vq, vd, *mlp_sums)


def _prep(me_arr, w_in, w_out, w_gate, w_up, w_down, conv_w, fp, kwp):
    d, fs = w_gate.shape
    kw, cshard = conv_w.shape

    def body(me_ref, wi_ref, wo_ref, wg_ref, wu_ref, wd_ref, cw_ref, oi, oo, og, ou, od, oc):
        oi[...] = wi_ref[...].astype(BF16)
        oo[...] = wo_ref[...].astype(BF16)
        for src, dst in ((wg_ref, og), (wu_ref, ou)):
            dst[:, 0:fs] = src[...].astype(BF16)
            if fp > fs:
                dst[:, fs:fp] = jnp.zeros((d, fp - fs), BF16)
        od[0:fs, :] = wd_ref[...].astype(BF16)
        if fp > fs:
            od[fs:fp, :] = jnp.zeros((fp - fs, d), BF16)
        oc[0:kw, :] = cw_ref[...]
        oc[kw:kwp, :] = jnp.zeros((kwp - kw, cshard), F32)

    ins = (w_in, w_out, w_gate, w_up, w_down, conv_w)
    outs = [jax.ShapeDtypeStruct((4,) + w_in.shape, BF16), jax.ShapeDtypeStruct((4,) + w_out.shape, BF16),
            jax.ShapeDtypeStruct((4, d, fp), BF16), jax.ShapeDtypeStruct((4, d, fp), BF16), jax.ShapeDtypeStruct((4, fp, d), BF16),
            jax.ShapeDtypeStruct((4, kwp, cshard), F32)]
    grid_spec = pltpu.PrefetchScalarGridSpec(
        num_scalar_prefetch=1, grid=(1,),
        in_specs=[pl.BlockSpec(a.shape, lambda i, me: (0, 0)) for a in ins],
        out_specs=[pl.BlockSpec((None,) + o.shape[1:], lambda i, me: (me[0], 0, 0)) for o in outs])
    return pl.pallas_call(body, name="wprep", grid_spec=grid_spec, out_shape=outs, compiler_params=_cparams())(me_arr, *ins)


def _coords():
    return tuple(lax.axis_index(a) for a in MESH_AXES)


def _other_chips(x, y):
    return [(1 - x, y), (x, 1 - y), (1 - x, 1 - y)]


def _remote(src, dst, send_sem, recv_sem, to):
    return pltpu.make_async_remote_copy(src_ref=src, dst_ref=dst, send_sem=send_sem, recv_sem=recv_sem,
                                        device_id=to, device_id_type=MESH_ID)


def _hbm_call(body, name, ins, out_shape, scratch_shapes, aliases=None):
    return pl.pallas_call(
        body, name=name, in_specs=[ANY] * len(ins), out_specs=[ANY] * len(out_shape), out_shape=out_shape,
        scratch_shapes=scratch_shapes, input_output_aliases=aliases or {},
    )(*ins)


class _Gather:
    def __init__(self, bufs, send_sems, recv_sems):
        self.bufs, self.send_sems, self.recv_sems = bufs, send_sems, recv_sems
        self.x, self.y, self.c = _coords()

    def _copies(self, stage):
        x, y, c = self.x, self.y, self.c
        for a, buf in enumerate(self.bufs):
            hr = buf.shape[1] // 2
            for j, chip in enumerate(_other_chips(x, y)):
                if stage == "ici_out":
                    ref, k, to = buf.at[2 * x + y, pl.ds(c * hr, hr)], j, (*chip, c)
                elif stage == "ici_in":
                    ref, k, to = buf.at[2 * chip[0] + chip[1], pl.ds(c * hr, hr)], j, (*chip, c)
                elif stage == "d2d_out":
                    ref, k, to = buf.at[2 * chip[0] + chip[1], pl.ds(c * hr, hr)], 3 + j, (x, y, 1 - c)
                else:
                    ref, k, to = buf.at[2 * chip[0] + chip[1], pl.ds((1 - c) * hr, hr)], 3 + j, (x, y, 1 - c)
                yield _remote(ref, ref, self.send_sems.at[a, k], self.recv_sems.at[a, k], to)

    def start(self):
        for cp in self._copies("ici_out"):
            cp.start()

    def forward(self):
        for landed, onward in zip(self._copies("ici_in"), self._copies("d2d_out")):
            landed.wait_recv()
            onward.start()

    def finish(self):
        for cp in self._copies("d2d_in"):
            cp.wait_recv()
        for stage in ("ici_out", "d2d_out"):
            for cp in self._copies(stage):
                cp.wait_send()


def _gather_sems(n):
    return [pltpu.SemaphoreType.DMA((n, 6)), pltpu.SemaphoreType.DMA((n, 6))]


def _gather_shards(bufs):
    n = len(bufs)

    def body(*refs):
        g = _Gather(refs[n:2 * n], *refs[2 * n:])
        g.start()
        g.forward()
        g.finish()

    return _hbm_call(body, "gather_shards", bufs, [jax.ShapeDtypeStruct(s.shape, s.dtype) for s in bufs],
                     _gather_sems(n), aliases={a: a for a in range(n)})


def _pair_swap(name, arrs):
    n = len(arrs)

    def body(*refs):
        src, land = refs[:n], refs[n:2 * n]
        send_sems, recv_sems = refs[2 * n:]
        x, y, c = _coords()
        copies = []
        for a in range(n):
            s = src[a].at[pl.ds(0, arrs[a].shape[0]), 1 - c] if arrs[a].ndim == 4 else src[a].at[1 - c]
            copies.append(_remote(s, land[a], send_sems.at[a], recv_sems.at[a], (x, y, 1 - c)))
            copies[-1].start()
        for cp in copies:
            cp.wait()

    outs = [jax.ShapeDtypeStruct(s.shape[:-3] + s.shape[-2:], s.dtype) for s in arrs]
    return _hbm_call(body, name, arrs, outs, [pltpu.SemaphoreType.DMA((n,)), pltpu.SemaphoreType.DMA((n,))])


class _Exchange:
    def __init__(self, src, dst, send_sems, recv_sems):
        self.src, self.dst, self.send_sems, self.recv_sems = src, dst, send_sems, recv_sems
        self.x, self.y, self.c = _coords()

    def _copies(self, incoming):
        x, y, c = self.x, self.y, self.c
        for a, (s, d) in enumerate(zip(self.src, self.dst)):
            for j, chip in enumerate(_other_chips(x, y)):
                slot = 2 * chip[0] + chip[1]
                if incoming:
                    out, into = d.at[slot], d.at[slot]
                else:
                    out, into = (s.at[slot] if len(s.shape) == 3 else s), d.at[2 * x + y]
                yield _remote(out, into, self.send_sems.at[a, j], self.recv_sems.at[a, j], (*chip, c))

    def start(self):
        for cp in self._copies(False):
            cp.start()

    def finish(self):
        for cp in self._copies(True):
            cp.wait_recv()
        for cp in self._copies(False):
            cp.wait_send()


def _exchange_sems(n):
    return [pltpu.SemaphoreType.DMA((n, 3)), pltpu.SemaphoreType.DMA((n, 3))]


def _exchange_shapes(arrs):
    return [jax.ShapeDtypeStruct((4,) + s.shape[-2:], s.dtype) for s in arrs]


def _chip_exchange(arrs):
    n = len(arrs)

    def body(*refs):
        ex = _Exchange(refs[:n], refs[n:2 * n], *refs[2 * n:])
        ex.start()
        ex.finish()

    return _hbm_call(body, "chip_exchange", arrs, _exchange_shapes(arrs), _exchange_sems(n))


def _pair_gather(halves):
    n = len(halves)

    def body(*refs):
        src, dst = refs[:n], refs[n:2 * n]
        send_sems, recv_sems = refs[2 * n:]
        x, y, c = _coords()
        copies = [_remote(src[a], dst[a], send_sems.at[a], recv_sems.at[a], (x, y, 1 - c)) for a in range(n)]
        for cp in copies:
            cp.start()
        for cp in copies:
            cp.wait()

    outs = [jax.ShapeDtypeStruct(s.shape, s.dtype) for s in halves]
    return _hbm_call(body, "pair_gather", halves, outs, [pltpu.SemaphoreType.DMA((n,)), pltpu.SemaphoreType.DMA((n,))])


def _pair_sum(a, g, land, c_arr, out_dtype):
    nq, _, hr, cc = g.shape

    def body(c_ref, g_ref, l_ref, o_ref):
        o_ref[...] = (g_ref[...] + l_ref[...]).astype(out_dtype)

    spec = pl.BlockSpec((None, hr, cc), lambda qi, cr: (qi, 0, 0))
    grid_spec = pltpu.PrefetchScalarGridSpec(
        num_scalar_prefetch=1, grid=(nq,),
        in_specs=[pl.BlockSpec((None, None, hr, cc), lambda qi, cr: (qi, cr[0], 0, 0)), spec], out_specs=spec)
    return pl.pallas_call(body, name=f"pair_sum_{a}", grid_spec=grid_spec, out_shape=jax.ShapeDtypeStruct((nq, hr, cc), out_dtype),
                          compiler_params=_cparams())(c_arr, g, land)


def _chip_sum(a, parts, own, me_arr):
    _, hr, cc = parts.shape

    def body(me_ref, p_ref, own_ref, o_ref):
        for mine in range(4):
            @pl.when(me_ref[0] == mine)
            def _():
                term = lambda j: (own_ref if j == mine else p_ref.at[j])[...].astype(F32)
                o_ref[...] = ((term(0) + term(1)) + term(2)) + term(3)

    own_spec = (pl.BlockSpec((None, hr, cc), lambda i, me: (me[0], 0, 0)) if own.ndim == 3
                else pl.BlockSpec((hr, cc), lambda i, me: (0, 0)))
    grid_spec = pltpu.PrefetchScalarGridSpec(
        num_scalar_prefetch=1, grid=(1,),
        in_specs=[pl.BlockSpec((4, hr, cc), lambda i, me: (0, 0, 0)), own_spec],
        out_specs=pl.BlockSpec((hr, cc), lambda i, me: (0, 0)))
    return pl.pallas_call(body, name=f"chip_sum_{a}", grid_spec=grid_spec, out_shape=jax.ShapeDtypeStruct((hr, cc), F32),
                          compiler_params=_cparams())(me_arr, parts, own)


def _row_block(rows, cols, limit=1 << 20):
    best = 8
    for tr in range(8, rows + 1, 8):
        if rows % tr == 0 and tr * cols * 4 <= limit:
            best = tr
    return best


def _adamw(name, w, g_mine, g_other, m, v, c_arr):
    r, c = w.shape
    hr, cg = g_mine.shape
    tr = math.gcd(_row_block(hr, cg), r)
    per_half = hr // tr
    bc1 = 1.0 - ADAM_B1 ** ADAM_STEP
    bc2 = 1.0 - ADAM_B2 ** ADAM_STEP

    def body(c_ref, w_ref, gm_ref, go_ref, m_ref, v_ref, go, do, mo, vo):
        gv = jnp.where(pl.program_id(0) // per_half == c_ref[0], gm_ref[:, 0:c], go_ref[:, 0:c])
        mn = ADAM_B1 * m_ref[...] + (1.0 - ADAM_B1) * gv
        vn = ADAM_B2 * v_ref[...] + (1.0 - ADAM_B2) * (gv * gv)
        go[...] = gv
        mo[...] = mn
        vo[...] = vn
        do[...] = -ADAM_LR * ((mn / bc1) / (jnp.sqrt(vn / bc2) + ADAM_EPS) + ADAM_WD * w_ref[...])

    blk = pl.BlockSpec((tr, c), lambda i, cr: (i, 0))
    gblk = pl.BlockSpec((tr, cg), lambda i, cr: (i % per_half, 0))
    grid_spec = pltpu.PrefetchScalarGridSpec(num_scalar_prefetch=1, grid=(r // tr,), in_specs=[blk, gblk, gblk, blk, blk],
                                             out_specs=[blk] * 4)
    return pl.pallas_call(body, name=f"adamw_{name}", grid_spec=grid_spec, out_shape=[jax.ShapeDtypeStruct((r, c), F32)] * 4,
                          compiler_params=_cparams())(c_arr, w, g_mine, g_other, m, v)


def _rows128(a):
    return a.reshape(-1, LANES)


def _pad_rows(a, rows):
    return jnp.pad(a, ((0, rows - a.shape[0]), (0, 0)))


def kernel(x, w_in, sgu_ln_g, sgu_ln_b, w_s, b_s, conv_w, conv_b, conv_ln_g, conv_ln_b, w_out, ln1_g, ln1_b, w_gate, w_up, w_down, ln2_g, ln2_b, loss_target, m_w_in, m_sgu_ln_g, m_sgu_ln_b, m_w_s, m_b_s, m_conv_w, m_conv_b, m_conv_ln_g, m_conv_ln_b, m_w_out, m_ln1_g, m_ln1_b, m_w_gate, m_w_up, m_w_down, m_ln2_g, m_ln2_b, v_w_in, v_sgu_ln_g, v_sgu_ln_b, v_w_s, v_b_s, v_conv_w, v_conv_b, v_conv_ln_g, v_conv_ln_b, v_w_out, v_ln1_g, v_ln1_b, v_w_gate, v_w_up, v_w_down, v_ln2_g, v_ln2_b):
    depth, d, q = w_in.shape
    assert depth == 1 and x.shape[0] == 1
    t = x.shape[1]
    heads = w_s.shape[1]
    kw, cshard = conv_w.shape[1], conv_w.shape[2]
    fs = w_gate.shape[2]
    fp = -(-fs // MXU_N) * MXU_N
    n_pairs = q // LANES
    assert heads * HEAD_DIM == q and q % LANES == 0 and w_s.shape[2] == CHUNK and 4 * cshard == q and kw - 1 <= HALO
    alpha = (2.0 * depth) ** 0.25
    tm = min(512, t)
    assert t % tm == 0 and tm % CHUNK == 0
    x2, tgt = x[0], loss_target[0]
    mx, my, mc = _coords()
    me = 2 * mx + my
    c_arr = jnp.reshape(mc, (1,)).astype(jnp.int32)

    kwp = -(-kw // 16) * 16
    me_arr = jnp.reshape(me, (1,)).astype(jnp.int32)
    wi, wo, wg, wu, wd, cw4 = _prep(me_arr, w_in[0], w_out[0], w_gate[0], w_up[0], w_down[0], conv_w[0], fp, kwp)
    wi, wo, cw4 = _gather_shards([wi, wo, cw4])
    wo = wo.reshape(d, d)
    cw = jnp.transpose(cw4, (1, 0, 2)).reshape(kwp, q)
    cwf = _pad_rows(cw[:kw][::-1], kwp)

    wm = jnp.where(jnp.tril(jnp.ones((CHUNK, CHUNK), bool)), w_s[0], 0.0)
    wst = wm.reshape(n_pairs, 2 * CHUNK, CHUNK).astype(BF16)
    wstt = jnp.transpose(wm, (0, 2, 1)).reshape(n_pairs, 2 * CHUNK, CHUNK).astype(BF16)
    bmat = jnp.repeat(b_s[0].T, HEAD_DIM, axis=1)
    vq = _pad_rows(jnp.concatenate([sgu_ln_g, sgu_ln_b, conv_b, conv_ln_g, conv_ln_b], axis=0), 8)
    vd = _pad_rows(jnp.concatenate([ln1_g, ln1_b, ln2_g, ln2_b], axis=0), 8)

    *saved, wg, wu, wd = _fwd_mix(x2, wi, wo, wst, bmat, cw, vq, vd, [wg, wu, wd], alpha, kw, tm)
    gp, up, x1b, dr2, loss_part, dg2, db2 = _fwd_mlp(saved[3], tgt, wg, wu, wd, vd, alpha, tm)
    mlp_grads = None
    for k in range(4):
        mlp_grads = _bwd_mlp_slab(k, dr2, mlp_grads, x1b, gp, up, wg, wu, wd, alpha, tm)
    dx1 = mlp_grads[0]
    mlp_halves = [b.reshape(4, 2, b.shape[1] // 2, b.shape[2]) for b in mlp_grads[1:]]
    mlp_sums = [_pair_sum(f"mlp{a}", h, l, c_arr, BF16) for a, (h, l) in enumerate(zip(mlp_halves, _pair_swap("pair_swap_mlp", mlp_halves)))]
    grad_x, dwi, dwo, dws, dbs, dcw, dvq, dvd, *mlp_parts = _bwd_mix(dx1, saved, wi, wo, wstt, cw, cwf, vq, vd, mlp_sums, alpha, kw, min(tm, 256))
    loss = lax.psum(loss_part[0, 0], MESH_AXES)

    dws = jnp.where(jnp.tril(jnp.ones((CHUNK, CHUNK), bool)), dws.reshape(heads, CHUNK, CHUNK), 0.0)
    dvd = dvd.at[VD_LN2_G].set(dg2[0]).at[VD_LN2_B].set(db2[0])
    pieces = [_rows128(dws), dbs[:, :heads].T, _rows128(dcw), _rows128(dvq), _rows128(dvd)]
    sizes = [p.shape[0] for p in pieces]
    rows = -(-sum(sizes) // 16) * 16
    small = _pad_rows(jnp.concatenate(pieces, axis=0), rows)

    big = [dwi, dwo.reshape(4, d // 4, d)]
    halves = [b.reshape(4, 2, b.shape[1] // 2, b.shape[2]) for b in big] + [small.reshape(2, rows // 2, LANES)]
    landed = _pair_swap("pair_swap_mix", halves)
    sums = [_pair_sum(f"mix{a}", h, l, c_arr, BF16) for a, (h, l) in enumerate(zip(halves[:-1], landed[:-1]))]
    sums.append(_pair_sum("small", halves[-1][None], landed[-1][None], c_arr, F32)[0])
    parts = _chip_exchange(sums)
    parts = list(parts)
    parts, sums = parts[:2] + list(mlp_parts) + parts[2:], sums[:2] + list(mlp_sums) + sums[2:]
    mine = [_chip_sum(a, p, s, me_arr) for a, (p, s) in enumerate(zip(parts, sums))]
    other = _pair_gather(mine)

    def pack(ws, bs, vqs, vds):
        ps = [_rows128(ws[0]), bs[0], jnp.zeros((sizes[2], LANES), F32),
              _rows128(_pad_rows(jnp.concatenate(vqs, axis=0), 8)), _rows128(_pad_rows(jnp.concatenate(vds, axis=0), 8))]
        return _pad_rows(jnp.concatenate(ps, axis=0), rows)

    out = {}
    for a, (nm, w_, m_, v_) in enumerate((("w_in", w_in, m_w_in, v_w_in), ("w_out", w_out, m_w_out, v_w_out),
                                          ("w_gate", w_gate, m_w_gate, v_w_gate), ("w_up", w_up, m_w_up, v_w_up),
                                          ("w_down", w_down, m_w_down, v_w_down))):
        out[nm] = _adamw(nm, w_[0], mine[a], other[a], m_[0], v_[0], c_arr)
    packed = _adamw(
        "small",
        pack(w_s, b_s, [sgu_ln_g, sgu_ln_b, conv_b, conv_ln_g, conv_ln_b], [ln1_g, ln1_b, ln2_g, ln2_b]), mine[-1], other[-1],
        pack(m_w_s, m_b_s, [m_sgu_ln_g, m_sgu_ln_b, m_conv_b, m_conv_ln_g, m_conv_ln_b], [m_ln1_g, m_ln1_b, m_ln2_g, m_ln2_b]),
        pack(v_w_s, v_b_s, [v_sgu_ln_g, v_sgu_ln_b, v_conv_b, v_conv_ln_g, v_conv_ln_b], [v_ln1_g, v_ln1_b, v_ln2_g, v_ln2_b]),
        c_arr)

    offs = [sum(sizes[:i]) for i in range(len(sizes))]
    g_cw_full = packed[0][offs[2]:offs[2] + sizes[2]].reshape(kwp, q)
    g_cw = lax.dynamic_slice(g_cw_full, (0, me * cshard), (kwp, cshard))
    out["conv_w"] = _adamw("conv_w", _pad_rows(conv_w[0], kwp), g_cw, g_cw, _pad_rows(m_conv_w[0], kwp), _pad_rows(v_conv_w[0], kwp), c_arr)
    out["conv_w"] = [o[:kw] for o in out["conv_w"]]

    def unpack(p):
        vq_o = p[offs[3]:offs[3] + sizes[3]].reshape(8, q)
        vd_o = p[offs[4]:offs[4] + sizes[4]].reshape(8, d)
        return {"w_s": p[offs[0]:offs[0] + sizes[0]].reshape(heads, CHUNK, CHUNK), "b_s": p[offs[1]:offs[1] + sizes[1]],
                "sgu_ln_g": vq_o[VQ_SGU_G], "sgu_ln_b": vq_o[VQ_SGU_B], "conv_b": vq_o[VQ_CONV_B],
                "conv_ln_g": vq_o[VQ_CLN_G], "conv_ln_b": vq_o[VQ_CLN_B],
                "ln1_g": vd_o[VD_LN1_G], "ln1_b": vd_o[VD_LN1_B], "ln2_g": vd_o[VD_LN2_G], "ln2_b": vd_o[VD_LN2_B]}

    small_out = [unpack(p) for p in packed]
    names = ["w_in", "sgu_ln_g", "sgu_ln_b", "w_s", "b_s", "conv_w", "conv_b", "conv_ln_g", "conv_ln_b", "w_out",
             "ln1_g", "ln1_b", "w_gate", "w_up", "w_down", "ln2_g", "ln2_b"]
    result = [loss, grad_x[None]]
    for kind in range(4):
        for nm in names:
            val = out[nm][kind] if nm in out else small_out[kind][nm]
            result.append(val[None])
    return tuple(result)
```

```python
import functools
import math

import jax
import jax.numpy as jnp
from jax import lax
from jax.experimental import pallas as pl
from jax.experimental.pallas import tpu as pltpu

F32 = jnp.float32
BF16 = jnp.bfloat16

LN_EPS = 1e-5
HEAD_DIM = 64
CHUNK = 128
HALO = 32
LANES = 128
MXU_N = 256
ADAM_LR, ADAM_B1, ADAM_B2, ADAM_EPS, ADAM_WD, ADAM_STEP = 0.001, 0.9, 0.999, 1e-08, 0.01, 10
VMEM_LIMIT = 60 * 1024 * 1024
MESH_AXES = ("x", "y", "c")
MESH_ID = pl.DeviceIdType.MESH


def _dot(a, b):
    return jnp.dot(a, b, preferred_element_type=F32)


def _dot_nt(a, b):
    return lax.dot_general(a, b, (((1,), (1,)), ((), ())), preferred_element_type=F32)


def _dot_tn(a, b):
    return lax.dot_general(a, b, (((0,), (0,)), ((), ())), preferred_element_type=F32)


def _sigmoid(v):
    return 1.0 / (1.0 + jnp.exp(-v))


def _gelu(v):
    cdf = 0.5 * (1.0 + lax.erf(v * (1.0 / math.sqrt(2.0))))
    pdf = jnp.exp(-0.5 * v * v) * (1.0 / math.sqrt(2.0 * math.pi))
    return v * cdf, cdf + v * pdf


def _ln_stats(v):
    mu = jnp.mean(v, axis=-1, keepdims=True)
    d = v - mu
    rstd = lax.rsqrt(jnp.mean(d * d, axis=-1, keepdims=True) + LN_EPS)
    return d * rstd, rstd


def _ln_bwd(dxhat, xhat, rstd):
    m1 = jnp.mean(dxhat, axis=-1, keepdims=True)
    m2 = jnp.mean(dxhat * xhat, axis=-1, keepdims=True)
    return rstd * (dxhat - m1 - xhat * m2)


def _colsum(v):
    return jnp.sum(v, axis=0, keepdims=True)


def _pair_lanes(v, nc, p):
    return jnp.concatenate([v[c * CHUNK:(c + 1) * CHUNK, p * LANES:(p + 1) * LANES] for c in range(nc)], axis=1)


def _unpair(parts, nc):
    rows = [jnp.concatenate([part[:, c * LANES:(c + 1) * LANES] for part in parts], axis=1) for c in range(nc)]
    return jnp.concatenate(rows, axis=0)


def _low_head(nc):
    lane = lax.broadcasted_iota(jnp.int32, (CHUNK, nc * LANES), 1)
    return (lane & (LANES - 1)) < HEAD_DIM


def _mix(wst_ref, v, nc, n_pairs):
    vb = v.astype(BF16)
    low = _low_head(nc)
    parts = []
    for p in range(n_pairs):
        r = _dot(wst_ref[p], _pair_lanes(vb, nc, p))
        parts.append(jnp.where(low, r[:CHUNK], r[CHUNK:]))
    return _unpair(parts, nc)


def _mix_wgrad(dm, vn, nc, n_pairs):
    low = _low_head(nc)
    vb = vn.astype(BF16)
    out = []
    for p in range(n_pairs):
        a = _pair_lanes(dm, nc, p)
        lhs = jnp.concatenate([jnp.where(low, a, 0.0), jnp.where(low, 0.0, a)], axis=0).astype(BF16)
        out.append(_dot_nt(lhs, _pair_lanes(vb, nc, p)))
    return out


SUBLANES = 8


def _shift_rows(sh_ref, buf_ref, rows):
    for b in range(SUBLANES):
        sh_ref[b, 0:rows, :] = buf_ref[b:b + rows, :]


def _conv_taps(sh_ref, w_ref, tm, q, off, kw):
    rb = min(tm, 64)
    cols = []
    for l0 in range(0, q, LANES):
        rows = []
        for r0 in range(0, tm, rb):
            acc = None
            for k in range(kw):
                a, b = divmod(off + k, SUBLANES)
                t = w_ref[k:k + 1, l0:l0 + LANES] * sh_ref[b, a * SUBLANES + r0:a * SUBLANES + r0 + rb, l0:l0 + LANES]
                acc = t if acc is None else acc + t
            rows.append(acc)
        cols.append(jnp.concatenate(rows, axis=0))
    return jnp.concatenate(cols, axis=1)


def _conv_wgrad(acc_ref, dy_ref, sh_ref, tm, q, off, kw):
    rb = min(tm, 128)
    for l0 in range(0, q, LANES):
        for k in range(kw):
            a, b = divmod(off + k, SUBLANES)
            part = None
            for r0 in range(0, tm, rb):
                prod = dy_ref[r0:r0 + rb, l0:l0 + LANES] * sh_ref[b, a * SUBLANES + r0:a * SUBLANES + r0 + rb, l0:l0 + LANES]
                prod = jnp.sum(prod.reshape(rb // SUBLANES, SUBLANES, LANES), axis=0)
                part = prod if part is None else part + prod
            acc_ref[k, :, l0:l0 + LANES] += part


def _cparams():
    return pltpu.CompilerParams(dimension_semantics=("arbitrary",), vmem_limit_bytes=VMEM_LIMIT)


def _full(shape):
    return pl.BlockSpec(shape, lambda i: (0,) * len(shape))


ANY = pl.BlockSpec(memory_space=pl.ANY)

VQ_SGU_G, VQ_SGU_B, VQ_CONV_B, VQ_CLN_G, VQ_CLN_B = range(5)
VD_LN1_G, VD_LN1_B, VD_LN2_G, VD_LN2_B = range(4)
RS_LN1, RS_SGU, RS_CONV = range(3)
RS_COLS = 8


def _fwd_mix(x, wi, wo, wst, bmat, cw, vq, vd, mlp_w, alpha, kw, tm):
    t, d = x.shape
    q = wi.shape[2]
    nc, n_pairs = tm // CHUNK, q // LANES
    n = t // tm
    n_in, n_saved = 8, 11

    def body(x_ref, wi_hbm, wo_hbm, wst_ref, bmat_ref, cw_ref, vq_ref, vd_ref, *rest):
        (xb_ref, pag_ref, y_ref, xh_ref, rs_ref, zu_ref, mg_ref, vhat_ref, gv_ref, vnb_ref, yhat_ref) = rest[3:3 + n_saved]
        gathered = rest[3 + n_saved:6 + n_saved]
        wi_v, wo_v, hb_ref, sh_ref, send_sems, recv_sems = rest[6 + n_saved:]
        step = pl.program_id(0)

        @pl.when(step == 0)
        def _():
            _Gather(gathered, send_sems, recv_sems).start()
            pltpu.sync_copy(wi_hbm, wi_v)
            pltpu.sync_copy(wo_hbm, wo_v)
            hb_ref[...] = jnp.zeros_like(hb_ref)

        @pl.when(step == (3 * n) // 4)
        def _():
            _Gather(gathered, send_sems, recv_sems).forward()

        xv = x_ref[...]
        xb = xv.astype(BF16)
        xb_ref[...] = xb
        pu, pv, pa, pg = (_dot(xb, wi_v[j]) for j in range(4))
        pag_ref[:, 0:q] = pa.astype(BF16)
        pag_ref[:, q:2 * q] = pg.astype(BF16)
        zu, gu = _gelu(pu)
        zv, gv = _gelu(pv)
        vhat, rstd_v = _ln_stats(zv)
        vnb = (vhat * vq_ref[VQ_SGU_G:VQ_SGU_G + 1, :] + vq_ref[VQ_SGU_B:VQ_SGU_B + 1, :]).astype(BF16)
        mixed = _mix(wst_ref, vnb, nc, n_pairs) + jnp.concatenate([bmat_ref[...]] * nc, axis=0)
        y_ref[:, 0:q] = (zu * mixed).astype(BF16)
        zu_ref[...] = zu
        mg_ref[...] = mixed * gu
        vhat_ref[...] = vhat
        gv_ref[...] = gv
        vnb_ref[...] = vnb

        hb_ref[HALO:HALO + tm, :] = pa * _sigmoid(pg)
        _shift_rows(sh_ref, hb_ref, tm + HALO)
        yc = _conv_taps(sh_ref, cw_ref, tm, q, HALO - (kw - 1), kw) + vq_ref[VQ_CONV_B:VQ_CONV_B + 1, :]
        hb_ref[0:HALO, :] = hb_ref[tm:tm + HALO, :]
        yhat, rstd_c = _ln_stats(yc)
        yhat_ref[...] = yhat
        yn = yhat * vq_ref[VQ_CLN_G:VQ_CLN_G + 1, :] + vq_ref[VQ_CLN_B:VQ_CLN_B + 1, :]
        y_ref[:, q:2 * q] = (yn * _sigmoid(yn)).astype(BF16)

        r1 = alpha * xv + _dot(y_ref[...], wo_v[...])
        xhat, rstd1 = _ln_stats(r1)
        xh_ref[...] = xhat
        col = lax.broadcasted_iota(jnp.int32, (tm, RS_COLS), 1)
        rs_ref[...] = jnp.where(col == RS_LN1, rstd1, jnp.where(col == RS_SGU, rstd_v, jnp.where(col == RS_CONV, rstd_c, 0.0)))

        @pl.when(step == n - 1)
        def _():
            _Gather(gathered, send_sems, recv_sems).finish()

    row = lambda w: pl.BlockSpec((tm, w), lambda i: (i, 0))
    widths = [(d, BF16), (2 * q, BF16), (d, BF16), (d, F32), (RS_COLS, F32), (q, F32), (q, F32), (q, F32), (q, F32), (q, BF16), (q, F32)]
    assert len(widths) == n_saved
    return pl.pallas_call(
        body, name="fwd_mix", grid=(n,),
        in_specs=[row(d), ANY, ANY, _full(wst.shape), _full(bmat.shape), _full(cw.shape), _full(vq.shape), _full(vd.shape)] + [ANY] * 3,
        out_specs=[row(w) for w, _ in widths] + [ANY] * 3,
        out_shape=[jax.ShapeDtypeStruct((t, w), dt) for w, dt in widths] + [jax.ShapeDtypeStruct(b.shape, b.dtype) for b in mlp_w],
        scratch_shapes=[pltpu.VMEM(wi.shape, BF16), pltpu.VMEM(wo.shape, BF16), pltpu.VMEM((HALO + tm + SUBLANES, q), F32),
                        pltpu.VMEM((SUBLANES, tm + HALO, q), F32)] + _gather_sems(3),
        input_output_aliases={n_in + a: n_saved + a for a in range(3)},
        compiler_params=_cparams(),
    )(x, wi, wo, wst, bmat, cw, vq, vd, *mlp_w)


def _fwd_mlp(xh1, tgt, wg, wu, wd, vd, alpha, tm):
    t, d = xh1.shape
    ns, fp, _ = wg.shape
    n = t // tm

    def body(xh_ref, tgt_ref, wg_hbm, wu_hbm, wd_hbm, vd_ref,
             gp_ref, up_ref, x1b_ref, dr2_ref, loss_ref, dg2_ref, db2_ref, wg_v, wu_v, wd_v):
        @pl.when(pl.program_id(0) == 0)
        def _():
            pltpu.sync_copy(wg_hbm, wg_v)
            pltpu.sync_copy(wu_hbm, wu_v)
            pltpu.sync_copy(wd_hbm, wd_v)
            loss_ref[...] = jnp.zeros_like(loss_ref)
            dg2_ref[...] = jnp.zeros_like(dg2_ref)
            db2_ref[...] = jnp.zeros_like(db2_ref)

        x1 = xh_ref[...] * vd_ref[VD_LN1_G:VD_LN1_G + 1, :] + vd_ref[VD_LN1_B:VD_LN1_B + 1, :]
        x1b = x1.astype(BF16)
        x1b_ref[...] = x1b
        acc = alpha * x1
        for k in range(ns):
            gp = _dot_nt(x1b, wg_v[k])
            up = _dot_nt(x1b, wu_v[k])
            gp_ref[:, k * fp:(k + 1) * fp] = gp.astype(BF16)
            up_ref[:, k * fp:(k + 1) * fp] = up.astype(BF16)
            acc = acc + _dot((gp * _sigmoid(gp) * up).astype(BF16), wd_v[k])
        xh2, rstd2 = _ln_stats(acc)
        g2 = vd_ref[VD_LN2_G:VD_LN2_G + 1, :]
        err = xh2 * g2 + vd_ref[VD_LN2_B:VD_LN2_B + 1, :] - tgt_ref[...]
        loss_ref[...] += _colsum(jnp.sum(err * err, axis=1, keepdims=True)) * (0.5 / d)
        dy = err * (1.0 / d)
        dg2_ref[...] += _colsum(dy * xh2)
        db2_ref[...] += _colsum(dy)
        dr2_ref[...] = _ln_bwd(dy * g2, xh2, rstd2)

    row = lambda w: pl.BlockSpec((tm, w), lambda i: (i, 0))
    return pl.pallas_call(
        body, name="fwd_mlp", grid=(n,),
        in_specs=[row(d), row(d), ANY, ANY, ANY, _full(vd.shape)],
        out_specs=[row(ns * fp), row(ns * fp), row(d), row(d), _full((8, LANES)), _full((1, d)), _full((1, d))],
        out_shape=[jax.ShapeDtypeStruct((t, ns * fp), BF16), jax.ShapeDtypeStruct((t, ns * fp), BF16),
                   jax.ShapeDtypeStruct((t, d), BF16), jax.ShapeDtypeStruct((t, d), F32),
                   jax.ShapeDtypeStruct((8, LANES), F32), jax.ShapeDtypeStruct((1, d), F32), jax.ShapeDtypeStruct((1, d), F32)],
        scratch_shapes=[pltpu.VMEM(wg.shape, BF16), pltpu.VMEM(wu.shape, BF16), pltpu.VMEM(wd.shape, BF16)],
        compiler_params=_cparams(),
    )(xh1, tgt, wg, wu, wd, vd)


def _bwd_mlp_slab(k, dr2, prev, x1b, gp, up, wg, wu, wd, alpha, tm):
    t, d = dr2.shape
    ns, fp, _ = wg.shape
    n = t // tm
    first = prev is None

    def body(*refs):
        if first:
            dr_ref, x1b_ref, gp_ref, up_ref, wg_ref, wu_ref, wd_ref, dx_ref, dwg_hbm, dwu_hbm, dwd_hbm, ag, au, ad = refs
        else:
            (dr_ref, dxp_ref, x1b_ref, gp_ref, up_ref, wg_ref, wu_ref, wd_ref, _, _, _,
             dx_ref, dwg_hbm, dwu_hbm, dwd_hbm, ag, au, ad) = refs

        @pl.when(pl.program_id(0) == 0)
        def _():
            ag[...] = jnp.zeros_like(ag)
            au[...] = jnp.zeros_like(au)
            ad[...] = jnp.zeros_like(ad)

        dr = dr_ref[...]
        drb = dr.astype(BF16)
        x1b = x1b_ref[...]
        gpv = gp_ref[...].astype(F32)
        upv = up_ref[...].astype(F32)
        dh = _dot_nt(drb, wd_ref[0])
        sg = _sigmoid(gpv)
        silu = gpv * sg
        ad[...] += _dot_tn((silu * upv).astype(BF16), drb)
        dgp = (dh * upv * (sg * (1.0 + gpv * (1.0 - sg)))).astype(BF16)
        dup = (dh * silu).astype(BF16)
        ag[...] += _dot_tn(dgp, x1b)
        au[...] += _dot_tn(dup, x1b)
        base = alpha * dr if first else dxp_ref[...]
        dx_ref[...] = base + _dot(dgp, wg_ref[0]) + _dot(dup, wu_ref[0])

        @pl.when(pl.program_id(0) == n - 1)
        def _():
            pltpu.sync_copy(ag, dwg_hbm.at[k])
            pltpu.sync_copy(au, dwu_hbm.at[k])
            pltpu.sync_copy(ad, dwd_hbm.at[k])

    row = lambda w: pl.BlockSpec((tm, w), lambda i: (i, 0))
    slab = pl.BlockSpec((tm, fp), lambda i: (i, k))
    wrow = pl.BlockSpec((1, fp, d), lambda i: (k, 0, 0))
    ins = [dr2] + ([] if first else [prev[0]]) + [x1b, gp, up, wg, wu, wd] + ([] if first else list(prev[1:]))
    in_specs = [row(d)] + ([] if first else [row(d)]) + [row(d), slab, slab, wrow, wrow, wrow] + ([] if first else [ANY] * 3)
    return pl.pallas_call(
        body, name=f"bwd_mlp_{k}", grid=(n,),
        in_specs=in_specs,
        out_specs=[row(d), ANY, ANY, ANY],
        out_shape=[jax.ShapeDtypeStruct((t, d), F32)] + [jax.ShapeDtypeStruct((ns, fp, d), F32)] * 3,
        scratch_shapes=[pltpu.VMEM((fp, d), F32)] * 3,
        input_output_aliases={} if first else {8: 1, 9: 2, 10: 3},
        compiler_params=_cparams(),
    )(*ins)


def _bwd_mix(dx1, saved, wi, wo, wstt, cw, cwf, vq, vd, mlp_sums, alpha, kw, tm):
    xb, pag, y, xh1, rs, zu_s, mg_s, vhat_s, gv_s, vnb_s, yhat_s = saved
    t, d = xh1.shape
    q = wi.shape[2]
    nc, n_pairs = tm // CHUNK, q // LANES
    n = t // tm
    hb_per_tile = tm // HALO
    off = HALO - (kw - 1)

    def body(dx1_ref, xb_ref, pag_ref, halo_ref, y_ref, xh_ref, rs_ref, zu_ref, mg_ref, vhat_ref, gv_ref, vnb_ref, yhat_ref,
             wi_hbm, wo_hbm, wstt_ref, cw_ref, cwf_ref, vq_ref, vd_ref, sum_g, sum_u, sum_d,
             gx_ref, dwi_hbm, dwo_hbm, dws_ref, dbs_ref, dcw_ref, dvq_ref, dvd_ref, got_g, got_u, got_d,
             wi_v, wo_v, awi, awo, hb_ref, dyb_ref, sh_ref, dbm_ref, dcw8_ref, send_sems, recv_sems):
        i = pl.program_id(0)
        exchange = lambda: _Exchange((sum_g, sum_u, sum_d), (got_g, got_u, got_d), send_sems, recv_sems)

        @pl.when(i == 0)
        def _():
            exchange().start()
            pltpu.sync_copy(wi_hbm, wi_v)
            pltpu.sync_copy(wo_hbm, wo_v)
            for r in (awi, awo, dws_ref, dbm_ref, dcw8_ref, dvq_ref, dvd_ref, hb_ref, dyb_ref):
                r[...] = jnp.zeros_like(r)

        dx1v = dx1_ref[...]
        xh = xh_ref[...]
        rsv = rs_ref[...]
        dvd_ref[VD_LN1_G:VD_LN1_G + 1, :] += _colsum(dx1v * xh)
        dvd_ref[VD_LN1_B:VD_LN1_B + 1, :] += _colsum(dx1v)
        dr1 = _ln_bwd(dx1v * vd_ref[VD_LN1_G:VD_LN1_G + 1, :], xh, rsv[:, RS_LN1:RS_LN1 + 1])
        dr1b = dr1.astype(BF16)
        awo[...] += _dot_tn(y_ref[...], dr1b)
        dy = _dot_nt(dr1b, wo_v[...])

        vhat = vhat_ref[...]
        sgu_g = vq_ref[VQ_SGU_G:VQ_SGU_G + 1, :]
        doa = dy[:, 0:q]
        dm = doa * zu_ref[...]
        dpu = (doa * mg_ref[...]).astype(BF16)
        acc = dm[0:CHUNK]
        for c in range(1, nc):
            acc = acc + dm[c * CHUNK:(c + 1) * CHUNK]
        dbm_ref[...] += acc
        for p, g in enumerate(_mix_wgrad(dm, vnb_ref[...], nc, n_pairs)):
            dws_ref[p] += g
        dvn = _mix(wstt_ref, dm, nc, n_pairs)
        dvq_ref[VQ_SGU_G:VQ_SGU_G + 1, :] += _colsum(dvn * vhat)
        dvq_ref[VQ_SGU_B:VQ_SGU_B + 1, :] += _colsum(dvn)
        dpv = (_ln_bwd(dvn * sgu_g, vhat, rsv[:, RS_SGU:RS_SGU + 1]) * gv_ref[...]).astype(BF16)

        pa = pag_ref[:, 0:q].astype(F32)
        sg = _sigmoid(pag_ref[:, q:2 * q].astype(F32))
        hb_ref[HALO:HALO + tm, :] = pa * sg
        ha = halo_ref[:, 0:q].astype(F32)
        hg = halo_ref[:, q:2 * q].astype(F32)
        hb_ref[0:HALO, :] = jnp.where(i == n - 1, 0.0, ha * _sigmoid(hg))
        yhat = yhat_ref[...]
        cln_g = vq_ref[VQ_CLN_G:VQ_CLN_G + 1, :]
        yn = yhat * cln_g + vq_ref[VQ_CLN_B:VQ_CLN_B + 1, :]
        sy = _sigmoid(yn)
        dyn = dy[:, q:2 * q] * (sy * (1.0 + yn * (1.0 - sy)))
        dvq_ref[VQ_CLN_G:VQ_CLN_G + 1, :] += _colsum(dyn * yhat)
        dvq_ref[VQ_CLN_B:VQ_CLN_B + 1, :] += _colsum(dyn)
        dyc = _ln_bwd(dyn * cln_g, yhat, rsv[:, RS_CONV:RS_CONV + 1])
        dvq_ref[VQ_CONV_B:VQ_CONV_B + 1, :] += _colsum(dyc)
        dyb_ref[0:tm, :] = dyc
        _shift_rows(sh_ref, hb_ref, tm + HALO)
        _conv_wgrad(dcw8_ref, dyb_ref, sh_ref, tm, q, off, kw)
        _shift_rows(sh_ref, dyb_ref, tm + HALO)
        dh = _conv_taps(sh_ref, cwf_ref, tm, q, 0, kw)
        dyb_ref[tm:tm + HALO, :] = dyb_ref[0:HALO, :]
        da = (dh * sg).astype(BF16)
        dg = (dh * pa * (sg * (1.0 - sg))).astype(BF16)

        xb = xb_ref[...]
        gx = alpha * dr1
        for j, dpj in enumerate((dpu, dpv, da, dg)):
            awi[j] += _dot_tn(xb, dpj)
            gx = gx + _dot_nt(dpj, wi_v[j])
        gx_ref[...] = gx

        @pl.when(i == n - 1)
        def _():
            pltpu.sync_copy(awi, dwi_hbm)
            pltpu.sync_copy(awo, dwo_hbm)
            lane = lax.broadcasted_iota(jnp.int32, (CHUNK, LANES), 1)
            low = lane < HEAD_DIM
            dbs = jnp.zeros((CHUNK, LANES), F32)
            for p in range(n_pairs):
                grp = dbm_ref[:, p * LANES:(p + 1) * LANES]
                dbs = jnp.where(lane == 2 * p, jnp.sum(jnp.where(low, grp, 0.0), axis=1, keepdims=True), dbs)
                dbs = jnp.where(lane == 2 * p + 1, jnp.sum(jnp.where(low, 0.0, grp), axis=1, keepdims=True), dbs)
            dbs_ref[...] = dbs
            for k in range(cw.shape[0]):
                dcw_ref[k:k + 1, :] = _colsum(dcw8_ref[k])
            exchange().finish()

    rev = lambda w: pl.BlockSpec((tm, w), lambda i: (n - 1 - i, 0))
    halo = pl.BlockSpec((HALO, 2 * q), lambda i: (jnp.maximum((n - 1 - i) * hb_per_tile - 1, 0), 0))
    small = [jax.ShapeDtypeStruct((n_pairs, 2 * CHUNK, CHUNK), F32), jax.ShapeDtypeStruct((CHUNK, LANES), F32),
             jax.ShapeDtypeStruct(cw.shape, F32), jax.ShapeDtypeStruct(vq.shape, F32), jax.ShapeDtypeStruct(vd.shape, F32)]
    return pl.pallas_call(
        body, name="bwd_mix", grid=(n,),
        in_specs=[rev(d), rev(d), rev(2 * q), halo, rev(d), rev(d), rev(RS_COLS), rev(q), rev(q), rev(q), rev(q), rev(q), rev(q),
                  ANY, ANY, _full(wstt.shape), _full(cw.shape), _full(cwf.shape), _full(vq.shape), _full(vd.shape)] + [ANY] * 3,
        out_specs=[rev(d), ANY, ANY] + [_full(s.shape) for s in small] + [ANY] * 3,
        out_shape=[jax.ShapeDtypeStruct((t, d), F32), jax.ShapeDtypeStruct(wi.shape, F32), jax.ShapeDtypeStruct(wo.shape, F32)] + small
        + _exchange_shapes(mlp_sums),
        scratch_shapes=[pltpu.VMEM(wi.shape, BF16), pltpu.VMEM(wo.shape, BF16), pltpu.VMEM(wi.shape, F32), pltpu.VMEM(wo.shape, F32),
                        pltpu.VMEM((HALO + tm + SUBLANES, q), F32), pltpu.VMEM((tm + HALO + SUBLANES, q), F32),
                        pltpu.VMEM((SUBLANES, tm + HALO, q), F32), pltpu.VMEM((CHUNK, q), F32),
                        pltpu.VMEM((cw.shape[0], SUBLANES, q), F32)] + _exchange_sems(3),
        compiler_params=_cparams(),
    )(dx1, xb, pag, pag, y, xh1, rs, zu_s, mg_s, vhat_s, gv_s, vnb_s, yhat_s, wi, wo, wstt, cw, cwf, vq, vd, *mlp_sums)


def _prep(me_arr, w_in, w_out, w_gate_t, w_up_t, w_down, conv_w, fp, kwp):
    fs, d = w_down.shape
    kw, cshard = conv_w.shape

    def body(me_ref, wi_ref, wo_ref, wg_ref, wu_ref, wd_ref, cw_ref, oi, oo, og, ou, od, oc):
        oi[...] = wi_ref[...].astype(BF16)
        oo[...] = wo_ref[...].astype(BF16)
        for src, dst in ((wg_ref, og), (wu_ref, ou), (wd_ref, od)):
            dst[0:fs, :] = src[...].astype(BF16)
            if fp > fs:
                dst[fs:fp, :] = jnp.zeros((fp - fs, d), BF16)
        oc[0:kw, :] = cw_ref[...]
        oc[kw:kwp, :] = jnp.zeros((kwp - kw, cshard), F32)

    ins = (w_in, w_out, w_gate_t, w_up_t, w_down, conv_w)
    outs = [jax.ShapeDtypeStruct((4,) + w_in.shape, BF16), jax.ShapeDtypeStruct((4,) + w_out.shape, BF16),
            jax.ShapeDtypeStruct((4, fp, d), BF16), jax.ShapeDtypeStruct((4, fp, d), BF16), jax.ShapeDtypeStruct((4, fp, d), BF16),
            jax.ShapeDtypeStruct((4, kwp, cshard), F32)]
    grid_spec = pltpu.PrefetchScalarGridSpec(
        num_scalar_prefetch=1, grid=(1,),
        in_specs=[pl.BlockSpec(a.shape, lambda i, me: (0, 0)) for a in ins],
        out_specs=[pl.BlockSpec((None,) + o.shape[1:], lambda i, me: (me[0], 0, 0)) for o in outs])
    return pl.pallas_call(body, name="wprep", grid_spec=grid_spec, out_shape=outs, compiler_params=_cparams())(me_arr, *ins)


def _coords():
    return tuple(lax.axis_index(a) for a in MESH_AXES)


def _other_chips(x, y):
    return [(1 - x, y), (x, 1 - y), (1 - x, 1 - y)]


def _remote(src, dst, send_sem, recv_sem, to):
    return pltpu.make_async_remote_copy(src_ref=src, dst_ref=dst, send_sem=send_sem, recv_sem=recv_sem,
                                        device_id=to, device_id_type=MESH_ID)


def _hbm_call(body, name, ins, out_shape, scratch_shapes, aliases=None):
    return pl.pallas_call(
        body, name=name, in_specs=[ANY] * len(ins), out_specs=[ANY] * len(out_shape), out_shape=out_shape,
        scratch_shapes=scratch_shapes, input_output_aliases=aliases or {},
    )(*ins)


class _Gather:
    def __init__(self, bufs, send_sems, recv_sems):
        self.bufs, self.send_sems, self.recv_sems = bufs, send_sems, recv_sems
        self.x, self.y, self.c = _coords()

    def _copies(self, stage):
        x, y, c = self.x, self.y, self.c
        for a, buf in enumerate(self.bufs):
            hr = buf.shape[1] // 2
            for j, chip in enumerate(_other_chips(x, y)):
                if stage == "ici_out":
                    ref, k, to = buf.at[2 * x + y, pl.ds(c * hr, hr)], j, (*chip, c)
                elif stage == "ici_in":
                    ref, k, to = buf.at[2 * chip[0] + chip[1], pl.ds(c * hr, hr)], j, (*chip, c)
                elif stage == "d2d_out":
                    ref, k, to = buf.at[2 * chip[0] + chip[1], pl.ds(c * hr, hr)], 3 + j, (x, y, 1 - c)
                else:
                    ref, k, to = buf.at[2 * chip[0] + chip[1], pl.ds((1 - c) * hr, hr)], 3 + j, (x, y, 1 - c)
                yield _remote(ref, ref, self.send_sems.at[a, k], self.recv_sems.at[a, k], to)

    def start(self):
        for cp in self._copies("ici_out"):
            cp.start()

    def forward(self):
        for landed, onward in zip(self._copies("ici_in"), self._copies("d2d_out")):
            landed.wait_recv()
            onward.start()

    def finish(self):
        for cp in self._copies("d2d_in"):
            cp.wait_recv()
        for stage in ("ici_out", "d2d_out"):
            for cp in self._copies(stage):
                cp.wait_send()


def _gather_sems(n):
    return [pltpu.SemaphoreType.DMA((n, 6)), pltpu.SemaphoreType.DMA((n, 6))]


def _gather_shards(bufs):
    n = len(bufs)

    def body(*refs):
        g = _Gather(refs[n:2 * n], *refs[2 * n:])
        g.start()
        g.forward()
        g.finish()

    return _hbm_call(body, "gather_shards", bufs, [jax.ShapeDtypeStruct(s.shape, s.dtype) for s in bufs],
                     _gather_sems(n), aliases={a: a for a in range(n)})


def _pair_swap(name, arrs):
    n = len(arrs)

    def body(*refs):
        src, land = refs[:n], refs[n:2 * n]
        send_sems, recv_sems = refs[2 * n:]
        x, y, c = _coords()
        copies = []
        for a in range(n):
            s = src[a].at[pl.ds(0, arrs[a].shape[0]), 1 - c] if arrs[a].ndim == 4 else src[a].at[1 - c]
            copies.append(_remote(s, land[a], send_sems.at[a], recv_sems.at[a], (x, y, 1 - c)))
            copies[-1].start()
        for cp in copies:
            cp.wait()

    outs = [jax.ShapeDtypeStruct(s.shape[:-3] + s.shape[-2:], s.dtype) for s in arrs]
    return _hbm_call(body, name, arrs, outs, [pltpu.SemaphoreType.DMA((n,)), pltpu.SemaphoreType.DMA((n,))])


class _Exchange:
    def __init__(self, src, dst, send_sems, recv_sems):
        self.src, self.dst, self.send_sems, self.recv_sems = src, dst, send_sems, recv_sems
        self.x, self.y, self.c = _coords()

    def _copies(self, incoming):
        x, y, c = self.x, self.y, self.c
        for a, (s, d) in enumerate(zip(self.src, self.dst)):
            for j, chip in enumerate(_other_chips(x, y)):
                slot = 2 * chip[0] + chip[1]
                if incoming:
                    out, into = d.at[slot], d.at[slot]
                else:
                    out, into = (s.at[slot] if len(s.shape) == 3 else s), d.at[2 * x + y]
                yield _remote(out, into, self.send_sems.at[a, j], self.recv_sems.at[a, j], (*chip, c))

    def start(self):
        for cp in self._copies(False):
            cp.start()

    def finish(self):
        for cp in self._copies(True):
            cp.wait_recv()
        for cp in self._copies(False):
            cp.wait_send()


def _exchange_sems(n):
    return [pltpu.SemaphoreType.DMA((n, 3)), pltpu.SemaphoreType.DMA((n, 3))]


def _exchange_shapes(arrs):
    return [jax.ShapeDtypeStruct((4,) + s.shape[-2:], s.dtype) for s in arrs]


def _chip_exchange(arrs):
    n = len(arrs)

    def body(*refs):
        ex = _Exchange(refs[:n], refs[n:2 * n], *refs[2 * n:])
        ex.start()
        ex.finish()

    return _hbm_call(body, "chip_exchange", arrs, _exchange_shapes(arrs), _exchange_sems(n))


def _pair_gather(halves):
    n = len(halves)

    def body(*refs):
        src, dst = refs[:n], refs[n:2 * n]
        send_sems, recv_sems = refs[2 * n:]
        x, y, c = _coords()
        copies = [_remote(src[a], dst[a], send_sems.at[a], recv_sems.at[a], (x, y, 1 - c)) for a in range(n)]
        for cp in copies:
            cp.start()
        for cp in copies:
            cp.wait()

    outs = [jax.ShapeDtypeStruct(s.shape, s.dtype) for s in halves]
    return _hbm_call(body, "pair_gather", halves, outs, [pltpu.SemaphoreType.DMA((n,)), pltpu.SemaphoreType.DMA((n,))])


def _pair_sum(a, g, land, c_arr, out_dtype):
    nq, _, hr, cc = g.shape

    def body(c_ref, g_ref, l_ref, o_ref):
        o_ref[...] = (g_ref[...] + l_ref[...]).astype(out_dtype)

    spec = pl.BlockSpec((None, hr, cc), lambda qi, cr: (qi, 0, 0))
    grid_spec = pltpu.PrefetchScalarGridSpec(
        num_scalar_prefetch=1, grid=(nq,),
        in_specs=[pl.BlockSpec((None, None, hr, cc), lambda qi, cr: (qi, cr[0], 0, 0)), spec], out_specs=spec)
    return pl.pallas_call(body, name=f"pair_sum_{a}", grid_spec=grid_spec, out_shape=jax.ShapeDtypeStruct((nq, hr, cc), out_dtype),
                          compiler_params=_cparams())(c_arr, g, land)


def _chip_sum(a, parts, own, me_arr):
    _, hr, cc = parts.shape

    def body(me_ref, p_ref, own_ref, o_ref):
        for mine in range(4):
            @pl.when(me_ref[0] == mine)
            def _():
                term = lambda j: (own_ref if j == mine else p_ref.at[j])[...].astype(F32)
                o_ref[...] = ((term(0) + term(1)) + term(2)) + term(3)

    own_spec = (pl.BlockSpec((None, hr, cc), lambda i, me: (me[0], 0, 0)) if own.ndim == 3
                else pl.BlockSpec((hr, cc), lambda i, me: (0, 0)))
    grid_spec = pltpu.PrefetchScalarGridSpec(
        num_scalar_prefetch=1, grid=(1,),
        in_specs=[pl.BlockSpec((4, hr, cc), lambda i, me: (0, 0, 0)), own_spec],
        out_specs=pl.BlockSpec((hr, cc), lambda i, me: (0, 0)))
    return pl.pallas_call(body, name=f"chip_sum_{a}", grid_spec=grid_spec, out_shape=jax.ShapeDtypeStruct((hr, cc), F32),
                          compiler_params=_cparams())(me_arr, parts, own)


def _row_block(rows, cols, limit=1 << 20):
    best = 8
    for tr in range(8, rows + 1, 8):
        if rows % tr == 0 and tr * cols * 4 <= limit:
            best = tr
    return best


def _adamw(name, w, g_mine, g_other, m, v, c_arr):
    r, c = w.shape
    hr, cg = g_mine.shape
    tr = math.gcd(_row_block(hr, cg), r)
    per_half = hr // tr
    bc1 = 1.0 - ADAM_B1 ** ADAM_STEP
    bc2 = 1.0 - ADAM_B2 ** ADAM_STEP

    def body(c_ref, w_ref, gm_ref, go_ref, m_ref, v_ref, go, do, mo, vo):
        gv = jnp.where(pl.program_id(0) // per_half == c_ref[0], gm_ref[:, 0:c], go_ref[:, 0:c])
        mn = ADAM_B1 * m_ref[...] + (1.0 - ADAM_B1) * gv
        vn = ADAM_B2 * v_ref[...] + (1.0 - ADAM_B2) * (gv * gv)
        go[...] = gv
        mo[...] = mn
        vo[...] = vn
        do[...] = -ADAM_LR * ((mn / bc1) / (jnp.sqrt(vn / bc2) + ADAM_EPS) + ADAM_WD * w_ref[...])

    blk = pl.BlockSpec((tr, c), lambda i, cr: (i, 0))
    gblk = pl.BlockSpec((tr, cg), lambda i, cr: (i % per_half, 0))
    grid_spec = pltpu.PrefetchScalarGridSpec(num_scalar_prefetch=1, grid=(r // tr,), in_specs=[blk, gblk, gblk, blk, blk],
                                             out_specs=[blk] * 4)
    return pl.pallas_call(body, name=f"adamw_{name}", grid_spec=grid_spec, out_shape=[jax.ShapeDtypeStruct((r, c), F32)] * 4,
                          compiler_params=_cparams())(c_arr, w, g_mine, g_other, m, v)


def _rows128(a):
    return a.reshape(-1, LANES)


def _pad_rows(a, rows):
    return jnp.pad(a, ((0, rows - a.shape[0]), (0, 0)))


def kernel(x, w_in, sgu_ln_g, sgu_ln_b, w_s, b_s, conv_w, conv_b, conv_ln_g, conv_ln_b, w_out, ln1_g, ln1_b, w_gate, w_up, w_down, ln2_g, ln2_b, loss_target, m_w_in, m_sgu_ln_g, m_sgu_ln_b, m_w_s, m_b_s, m_conv_w, m_conv_b, m_conv_ln_g, m_conv_ln_b, m_w_out, m_ln1_g, m_ln1_b, m_w_gate, m_w_up, m_w_down, m_ln2_g, m_ln2_b, v_w_in, v_sgu_ln_g, v_sgu_ln_b, v_w_s, v_b_s, v_conv_w, v_conv_b, v_conv_ln_g, v_conv_ln_b, v_w_out, v_ln1_g, v_ln1_b, v_w_gate, v_w_up, v_w_down, v_ln2_g, v_ln2_b):
    depth, d, q = w_in.shape
    assert depth == 1 and x.shape[0] == 1
    t = x.shape[1]
    heads = w_s.shape[1]
    kw, cshard = conv_w.shape[1], conv_w.shape[2]
    fs = w_gate.shape[2]
    fp = -(-fs // MXU_N) * MXU_N
    n_pairs = q // LANES
    assert heads * HEAD_DIM == q and q % LANES == 0 and w_s.shape[2] == CHUNK and 4 * cshard == q and kw - 1 <= HALO
    alpha = (2.0 * depth) ** 0.25
    tm = min(512, t)
    assert t % tm == 0 and tm % CHUNK == 0
    x2, tgt = x[0], loss_target[0]
    mx, my, mc = _coords()
    me = 2 * mx + my
    c_arr = jnp.reshape(mc, (1,)).astype(jnp.int32)

    kwp = -(-kw // 16) * 16
    me_arr = jnp.reshape(me, (1,)).astype(jnp.int32)
    wi, wo, wg, wu, wd, cw4 = _prep(me_arr, w_in[0], w_out[0], w_gate[0].T, w_up[0].T, w_down[0], conv_w[0], fp, kwp)
    wi, wo, cw4 = _gather_shards([wi, wo, cw4])
    wo = wo.reshape(d, d)
    cw = jnp.transpose(cw4, (1, 0, 2)).reshape(kwp, q)
    cwf = _pad_rows(cw[:kw][::-1], kwp)

    wm = jnp.where(jnp.tril(jnp.ones((CHUNK, CHUNK), bool)), w_s[0], 0.0)
    wst = wm.reshape(n_pairs, 2 * CHUNK, CHUNK).astype(BF16)
    wstt = jnp.transpose(wm, (0, 2, 1)).reshape(n_pairs, 2 * CHUNK, CHUNK).astype(BF16)
    bmat = jnp.repeat(b_s[0].T, HEAD_DIM, axis=1)
    vq = _pad_rows(jnp.concatenate([sgu_ln_g, sgu_ln_b, conv_b, conv_ln_g, conv_ln_b], axis=0), 8)
    vd = _pad_rows(jnp.concatenate([ln1_g, ln1_b, ln2_g, ln2_b], axis=0), 8)

    *saved, wg, wu, wd = _fwd_mix(x2, wi, wo, wst, bmat, cw, vq, vd, [wg, wu, wd], alpha, kw, tm)
    gp, up, x1b, dr2, loss_part, dg2, db2 = _fwd_mlp(saved[3], tgt, wg, wu, wd, vd, alpha, tm)
    mlp_grads = None
    for k in range(4):
        mlp_grads = _bwd_mlp_slab(k, dr2, mlp_grads, x1b, gp, up, wg, wu, wd, alpha, tm)
    dx1 = mlp_grads[0]
    mlp_halves = [b.reshape(4, 2, b.shape[1] // 2, b.shape[2]) for b in mlp_grads[1:]]
    mlp_sums = [_pair_sum(f"mlp{a}", h, l, c_arr, BF16) for a, (h, l) in enumerate(zip(mlp_halves, _pair_swap("pair_swap_mlp", mlp_halves)))]
    grad_x, dwi, dwo, dws, dbs, dcw, dvq, dvd, *mlp_parts = _bwd_mix(dx1, saved, wi, wo, wstt, cw, cwf, vq, vd, mlp_sums, alpha, kw, min(tm, 256))

    dws = jnp.where(jnp.tril(jnp.ones((CHUNK, CHUNK), bool)), dws.reshape(heads, CHUNK, CHUNK), 0.0)
    dvd = dvd.at[VD_LN2_G].set(dg2[0]).at[VD_LN2_B].set(db2[0])
    pieces = [_rows128(dws), dbs[:, :heads].T, _rows128(dcw), _rows128(dvq), _rows128(dvd), loss_part]
    sizes = [p.shape[0] for p in pieces]
    rows = -(-sum(sizes) // 16) * 16
    small = _pad_rows(jnp.concatenate(pieces, axis=0), rows)

    big = [dwi, dwo.reshape(4, d // 4, d)]
    halves = [b.reshape(4, 2, b.shape[1] // 2, b.shape[2]) for b in big] + [small.reshape(2, rows // 2, LANES)]
    landed = _pair_swap("pair_swap_mix", halves)
    sums = [_pair_sum(f"mix{a}", h, l, c_arr, BF16) for a, (h, l) in enumerate(zip(halves[:-1], landed[:-1]))]
    sums.append(_pair_sum("small", halves[-1][None], landed[-1][None], c_arr, F32)[0])
    parts = _chip_exchange(sums)
    parts = list(parts)
    parts, sums = parts[:2] + list(mlp_parts) + parts[2:], sums[:2] + list(mlp_sums) + sums[2:]
    mine = [_chip_sum(a, p, s, me_arr) for a, (p, s) in enumerate(zip(parts, sums))]
    other = _pair_gather(mine)

    def pack(ws, bs, vqs, vds):
        ps = [_rows128(ws[0]), bs[0], jnp.zeros((sizes[2], LANES), F32),
              _rows128(_pad_rows(jnp.concatenate(vqs, axis=0), 8)), _rows128(_pad_rows(jnp.concatenate(vds, axis=0), 8)),
              jnp.zeros((sizes[5], LANES), F32)]
        return _pad_rows(jnp.concatenate(ps, axis=0), rows)

    out = {}
    for a, (nm, w_, m_, v_) in enumerate((("w_in", w_in, m_w_in, v_w_in), ("w_out", w_out, m_w_out, v_w_out),
                                          ("w_gate", w_gate, m_w_gate, v_w_gate), ("w_up", w_up, m_w_up, v_w_up),
                                          ("w_down", w_down, m_w_down, v_w_down))):
        if nm in ("w_gate", "w_up"):
            out[nm] = [o.T for o in _adamw(nm, w_[0].T, mine[a], other[a], m_[0].T, v_[0].T, c_arr)]
        else:
            out[nm] = _adamw(nm, w_[0], mine[a], other[a], m_[0], v_[0], c_arr)
    packed = _adamw(
        "small",
        pack(w_s, b_s, [sgu_ln_g, sgu_ln_b, conv_b, conv_ln_g, conv_ln_b], [ln1_g, ln1_b, ln2_g, ln2_b]), mine[-1], other[-1],
        pack(m_w_s, m_b_s, [m_sgu_ln_g, m_sgu_ln_b, m_conv_b, m_conv_ln_g, m_conv_ln_b], [m_ln1_g, m_ln1_b, m_ln2_g, m_ln2_b]),
        pack(v_w_s, v_b_s, [v_sgu_ln_g, v_sgu_ln_b, v_conv_b, v_conv_ln_g, v_conv_ln_b], [v_ln1_g, v_ln1_b, v_ln2_g, v_ln2_b]),
        c_arr)

    offs = [sum(sizes[:i]) for i in range(len(sizes))]
    g_cw_full = packed[0][offs[2]:offs[2] + sizes[2]].reshape(kwp, q)
    g_cw = lax.dynamic_slice(g_cw_full, (0, me * cshard), (kwp, cshard))
    out["conv_w"] = _adamw("conv_w", _pad_rows(conv_w[0], kwp), g_cw, g_cw, _pad_rows(m_conv_w[0], kwp), _pad_rows(v_conv_w[0], kwp), c_arr)
    out["conv_w"] = [o[:kw] for o in out["conv_w"]]

    def unpack(p):
        vq_o = p[offs[3]:offs[3] + sizes[3]].reshape(8, q)
        vd_o = p[offs[4]:offs[4] + sizes[4]].reshape(8, d)
        return {"w_s": p[offs[0]:offs[0] + sizes[0]].reshape(heads, CHUNK, CHUNK), "b_s": p[offs[1]:offs[1] + sizes[1]],
                "sgu_ln_g": vq_o[VQ_SGU_G], "sgu_ln_b": vq_o[VQ_SGU_B], "conv_b": vq_o[VQ_CONV_B],
                "conv_ln_g": vq_o[VQ_CLN_G], "conv_ln_b": vq_o[VQ_CLN_B],
                "ln1_g": vd_o[VD_LN1_G], "ln1_b": vd_o[VD_LN1_B], "ln2_g": vd_o[VD_LN2_G], "ln2_b": vd_o[VD_LN2_B]}

    small_out = [unpack(p) for p in packed]
    loss = packed[0][offs[5], 0]
    names = ["w_in", "sgu_ln_g", "sgu_ln_b", "w_s", "b_s", "conv_w", "conv_b", "conv_ln_g", "conv_ln_b", "w_out",
             "ln1_g", "ln1_b", "w_gate", "w_up", "w_down", "ln2_g", "ln2_b"]
    result = [loss, grad_x[None]]
    for kind in range(4):
        for nm in names:
            val = out[nm][kind] if nm in out else small_out[kind][nm]
            result.append(val[None])
    return tuple(result)
```

```python
import functools
import math

import jax
import jax.numpy as jnp
from jax import lax
from jax.experimental import pallas as pl
from jax.experimental.pallas import tpu as pltpu

F32 = jnp.float32
BF16 = jnp.bfloat16

LN_EPS = 1e-5
HEAD_DIM = 64
CHUNK = 128
HALO = 32
LANES = 128
MXU_N = 256
ADAM_LR, ADAM_B1, ADAM_B2, ADAM_EPS, ADAM_WD, ADAM_STEP = 0.001, 0.9, 0.999, 1e-08, 0.01, 10
VMEM_LIMIT = 60 * 1024 * 1024
MESH_AXES = ("x", "y", "c")
MESH_ID = pl.DeviceIdType.MESH


def _dot(a, b):
    return jnp.dot(a, b, preferred_element_type=F32)


def _dot_nt(a, b):
    return lax.dot_general(a, b, (((1,), (1,)), ((), ())), preferred_element_type=F32)


def _dot_tn(a, b):
    return lax.dot_general(a, b, (((0,), (0,)), ((), ())), preferred_element_type=F32)


def _sigmoid(v):
    return 1.0 / (1.0 + jnp.exp(-v))


def _gelu(v):
    cdf = 0.5 * (1.0 + lax.erf(v * (1.0 / math.sqrt(2.0))))
    pdf = jnp.exp(-0.5 * v * v) * (1.0 / math.sqrt(2.0 * math.pi))
    return v * cdf, cdf + v * pdf


def _ln_stats(v):
    mu = jnp.mean(v, axis=-1, keepdims=True)
    d = v - mu
    rstd = lax.rsqrt(jnp.mean(d * d, axis=-1, keepdims=True) + LN_EPS)
    return d * rstd, rstd


def _ln_bwd(dxhat, xhat, rstd):
    m1 = jnp.mean(dxhat, axis=-1, keepdims=True)
    m2 = jnp.mean(dxhat * xhat, axis=-1, keepdims=True)
    return rstd * (dxhat - m1 - xhat * m2)


def _colsum(v):
    return jnp.sum(v, axis=0, keepdims=True)


def _pair_lanes(v, nc, p):
    return jnp.concatenate([v[c * CHUNK:(c + 1) * CHUNK, p * LANES:(p + 1) * LANES] for c in range(nc)], axis=1)


def _unpair(parts, nc):
    rows = [jnp.concatenate([part[:, c * LANES:(c + 1) * LANES] for part in parts], axis=1) for c in range(nc)]
    return jnp.concatenate(rows, axis=0)


def _low_head(nc):
    lane = lax.broadcasted_iota(jnp.int32, (CHUNK, nc * LANES), 1)
    return (lane & (LANES - 1)) < HEAD_DIM


def _mix(wst_ref, v, nc, n_pairs):
    vb = v.astype(BF16)
    low = _low_head(nc)
    parts = []
    for p in range(n_pairs):
        r = _dot(wst_ref[p], _pair_lanes(vb, nc, p))
        parts.append(jnp.where(low, r[:CHUNK], r[CHUNK:]))
    return _unpair(parts, nc)


def _mix_wgrad(dm, vn, nc, n_pairs):
    low = _low_head(nc)
    vb = vn.astype(BF16)
    out = []
    for p in range(n_pairs):
        a = _pair_lanes(dm, nc, p)
        lhs = jnp.concatenate([jnp.where(low, a, 0.0), jnp.where(low, 0.0, a)], axis=0).astype(BF16)
        out.append(_dot_nt(lhs, _pair_lanes(vb, nc, p)))
    return out


SUBLANES = 8


def _shift_rows(sh_ref, buf_ref, rows):
    for b in range(SUBLANES):
        sh_ref[b, 0:rows, :] = buf_ref[b:b + rows, :]


def _conv_taps(sh_ref, w_ref, tm, q, off, kw):
    rb = min(tm, 64)
    cols = []
    for l0 in range(0, q, LANES):
        rows = []
        for r0 in range(0, tm, rb):
            acc = None
            for k in range(kw):
                a, b = divmod(off + k, SUBLANES)
                t = w_ref[k:k + 1, l0:l0 + LANES] * sh_ref[b, a * SUBLANES + r0:a * SUBLANES + r0 + rb, l0:l0 + LANES]
                acc = t if acc is None else acc + t
            rows.append(acc)
        cols.append(jnp.concatenate(rows, axis=0))
    return jnp.concatenate(cols, axis=1)


def _conv_wgrad(acc_ref, dy_ref, sh_ref, tm, q, off, kw):
    rb = min(tm, 128)
    for l0 in range(0, q, LANES):
        for k in range(kw):
            a, b = divmod(off + k, SUBLANES)
            part = None
            for r0 in range(0, tm, rb):
                prod = dy_ref[r0:r0 + rb, l0:l0 + LANES] * sh_ref[b, a * SUBLANES + r0:a * SUBLANES + r0 + rb, l0:l0 + LANES]
                prod = jnp.sum(prod.reshape(rb // SUBLANES, SUBLANES, LANES), axis=0)
                part = prod if part is None else part + prod
            acc_ref[k, :, l0:l0 + LANES] += part


def _cparams():
    return pltpu.CompilerParams(dimension_semantics=("arbitrary",), vmem_limit_bytes=VMEM_LIMIT)


def _full(shape):
    return pl.BlockSpec(shape, lambda i: (0,) * len(shape))


ANY = pl.BlockSpec(memory_space=pl.ANY)

VQ_SGU_G, VQ_SGU_B, VQ_CONV_B, VQ_CLN_G, VQ_CLN_B = range(5)
VD_LN1_G, VD_LN1_B, VD_LN2_G, VD_LN2_B = range(4)
RS_LN1, RS_SGU, RS_CONV = range(3)
RS_COLS = 8


def _fwd_mix(x, wi, wo, wst, bmat, cw, vq, vd, mlp_w, alpha, kw, tm):
    t, d = x.shape
    q = wi.shape[2]
    nc, n_pairs = tm // CHUNK, q // LANES
    n = t // tm
    n_in, n_saved = 8, 11

    def body(x_ref, wi_hbm, wo_hbm, wst_ref, bmat_ref, cw_ref, vq_ref, vd_ref, *rest):
        (xb_ref, pag_ref, y_ref, xh_ref, rs_ref, zu_ref, mg_ref, vhat_ref, gv_ref, vnb_ref, yhat_ref) = rest[3:3 + n_saved]
        gathered = rest[3 + n_saved:6 + n_saved]
        wi_v, wo_v, hb_ref, sh_ref, send_sems, recv_sems = rest[6 + n_saved:]
        step = pl.program_id(0)

        @pl.when(step == 0)
        def _():
            _Gather(gathered, send_sems, recv_sems).start()
            pltpu.sync_copy(wi_hbm, wi_v)
            pltpu.sync_copy(wo_hbm, wo_v)
            hb_ref[...] = jnp.zeros_like(hb_ref)

        @pl.when(step == (3 * n) // 4)
        def _():
            _Gather(gathered, send_sems, recv_sems).forward()

        xv = x_ref[...]
        xb = xv.astype(BF16)
        xb_ref[...] = xb
        pu, pv, pa, pg = (_dot(xb, wi_v[j]) for j in range(4))
        pag_ref[:, 0:q] = pa.astype(BF16)
        pag_ref[:, q:2 * q] = pg.astype(BF16)
        zu, gu = _gelu(pu)
        zv, gv = _gelu(pv)
        vhat, rstd_v = _ln_stats(zv)
        vnb = (vhat * vq_ref[VQ_SGU_G:VQ_SGU_G + 1, :] + vq_ref[VQ_SGU_B:VQ_SGU_B + 1, :]).astype(BF16)
        mixed = _mix(wst_ref, vnb, nc, n_pairs) + jnp.concatenate([bmat_ref[...]] * nc, axis=0)
        y_ref[:, 0:q] = (zu * mixed).astype(BF16)
        zu_ref[...] = zu
        mg_ref[...] = mixed * gu
        vhat_ref[...] = vhat
        gv_ref[...] = gv
        vnb_ref[...] = vnb

        hb_ref[HALO:HALO + tm, :] = pa * _sigmoid(pg)
        _shift_rows(sh_ref, hb_ref, tm + HALO)
        yc = _conv_taps(sh_ref, cw_ref, tm, q, HALO - (kw - 1), kw) + vq_ref[VQ_CONV_B:VQ_CONV_B + 1, :]
        hb_ref[0:HALO, :] = hb_ref[tm:tm + HALO, :]
        yhat, rstd_c = _ln_stats(yc)
        yhat_ref[...] = yhat
        yn = yhat * vq_ref[VQ_CLN_G:VQ_CLN_G + 1, :] + vq_ref[VQ_CLN_B:VQ_CLN_B + 1, :]
        y_ref[:, q:2 * q] = (yn * _sigmoid(yn)).astype(BF16)

        r1 = alpha * xv + _dot(y_ref[...], wo_v[...])
        xhat, rstd1 = _ln_stats(r1)
        xh_ref[...] = xhat
        col = lax.broadcasted_iota(jnp.int32, (tm, RS_COLS), 1)
        rs_ref[...] = jnp.where(col == RS_LN1, rstd1, jnp.where(col == RS_SGU, rstd_v, jnp.where(col == RS_CONV, rstd_c, 0.0)))

        @pl.when(step == n - 1)
        def _():
            _Gather(gathered, send_sems, recv_sems).finish()

    row = lambda w: pl.BlockSpec((tm, w), lambda i: (i, 0))
    widths = [(d, BF16), (2 * q, BF16), (d, BF16), (d, F32), (RS_COLS, F32), (q, F32), (q, F32), (q, F32), (q, F32), (q, BF16), (q, F32)]
    assert len(widths) == n_saved
    return pl.pallas_call(
        body, name="fwd_mix", grid=(n,),
        in_specs=[row(d), ANY, ANY, _full(wst.shape), _full(bmat.shape), _full(cw.shape), _full(vq.shape), _full(vd.shape)] + [ANY] * 3,
        out_specs=[row(w) for w, _ in widths] + [ANY] * 3,
        out_shape=[jax.ShapeDtypeStruct((t, w), dt) for w, dt in widths] + [jax.ShapeDtypeStruct(b.shape, b.dtype) for b in mlp_w],
        scratch_shapes=[pltpu.VMEM(wi.shape, BF16), pltpu.VMEM(wo.shape, BF16), pltpu.VMEM((HALO + tm + SUBLANES, q), F32),
                        pltpu.VMEM((SUBLANES, tm + HALO, q), F32)] + _gather_sems(3),
        input_output_aliases={n_in + a: n_saved + a for a in range(3)},
        compiler_params=_cparams(),
    )(x, wi, wo, wst, bmat, cw, vq, vd, *mlp_w)


def _fwd_mlp(xh1, tgt, wg, wu, wd, vd, alpha, tm):
    t, d = xh1.shape
    ns, fp, _ = wg.shape
    n = t // tm

    def body(xh_ref, tgt_ref, wg_hbm, wu_hbm, wd_hbm, vd_ref,
             gp_ref, up_ref, x1b_ref, dr2_ref, loss_ref, dg2_ref, db2_ref, wg_v, wu_v, wd_v):
        @pl.when(pl.program_id(0) == 0)
        def _():
            pltpu.sync_copy(wg_hbm, wg_v)
            pltpu.sync_copy(wu_hbm, wu_v)
            pltpu.sync_copy(wd_hbm, wd_v)
            loss_ref[...] = jnp.zeros_like(loss_ref)
            dg2_ref[...] = jnp.zeros_like(dg2_ref)
            db2_ref[...] = jnp.zeros_like(db2_ref)

        x1 = xh_ref[...] * vd_ref[VD_LN1_G:VD_LN1_G + 1, :] + vd_ref[VD_LN1_B:VD_LN1_B + 1, :]
        x1b = x1.astype(BF16)
        x1b_ref[...] = x1b
        acc = alpha * x1
        for k in range(ns):
            gp = _dot_nt(x1b, wg_v[k])
            up = _dot_nt(x1b, wu_v[k])
            gp_ref[:, k * fp:(k + 1) * fp] = gp.astype(BF16)
            up_ref[:, k * fp:(k + 1) * fp] = up.astype(BF16)
            acc = acc + _dot((gp * _sigmoid(gp) * up).astype(BF16), wd_v[k])
        xh2, rstd2 = _ln_stats(acc)
        g2 = vd_ref[VD_LN2_G:VD_LN2_G + 1, :]
        err = xh2 * g2 + vd_ref[VD_LN2_B:VD_LN2_B + 1, :] - tgt_ref[...]
        loss_ref[...] += _colsum(jnp.sum(err * err, axis=1, keepdims=True)) * (0.5 / d)
        dy = err * (1.0 / d)
        dg2_ref[...] += _colsum(dy * xh2)
        db2_ref[...] += _colsum(dy)
        dr2_ref[...] = _ln_bwd(dy * g2, xh2, rstd2)

    row = lambda w: pl.BlockSpec((tm, w), lambda i: (i, 0))
    return pl.pallas_call(
        body, name="fwd_mlp", grid=(n,),
        in_specs=[row(d), row(d), ANY, ANY, ANY, _full(vd.shape)],
        out_specs=[row(ns * fp), row(ns * fp), row(d), row(d), _full((8, LANES)), _full((1, d)), _full((1, d))],
        out_shape=[jax.ShapeDtypeStruct((t, ns * fp), BF16), jax.ShapeDtypeStruct((t, ns * fp), BF16),
                   jax.ShapeDtypeStruct((t, d), BF16), jax.ShapeDtypeStruct((t, d), F32),
                   jax.ShapeDtypeStruct((8, LANES), F32), jax.ShapeDtypeStruct((1, d), F32), jax.ShapeDtypeStruct((1, d), F32)],
        scratch_shapes=[pltpu.VMEM(wg.shape, BF16), pltpu.VMEM(wu.shape, BF16), pltpu.VMEM(wd.shape, BF16)],
        compiler_params=_cparams(),
    )(xh1, tgt, wg, wu, wd, vd)


def _bwd_mlp_slab(k, dr2, prev, x1b, gp, up, wg, wu, wd, alpha, fs, tm):
    t, d = dr2.shape
    ns, fp, _ = wg.shape
    n = t // tm
    first = prev is None

    def body(*refs):
        if first:
            dr_ref, x1b_ref, gp_ref, up_ref, wg_ref, wu_ref, wd_ref, dx_ref, dwg_hbm, dwu_hbm, dwd_hbm, ag, au, ad = refs
        else:
            (dr_ref, dxp_ref, x1b_ref, gp_ref, up_ref, wg_ref, wu_ref, wd_ref, _, _, _,
             dx_ref, dwg_hbm, dwu_hbm, dwd_hbm, ag, au, ad) = refs

        @pl.when(pl.program_id(0) == 0)
        def _():
            ag[...] = jnp.zeros_like(ag)
            au[...] = jnp.zeros_like(au)
            ad[...] = jnp.zeros_like(ad)

        dr = dr_ref[...]
        drb = dr.astype(BF16)
        x1b = x1b_ref[...]
        gpv = gp_ref[...].astype(F32)
        upv = up_ref[...].astype(F32)
        dh = _dot_nt(drb, wd_ref[0])
        sg = _sigmoid(gpv)
        silu = gpv * sg
        ad[...] += _dot_tn((silu * upv).astype(BF16), drb)
        dgp = (dh * upv * (sg * (1.0 + gpv * (1.0 - sg)))).astype(BF16)
        dup = (dh * silu).astype(BF16)
        ag[...] += _dot_tn(dgp, x1b)
        au[...] += _dot_tn(dup, x1b)
        base = alpha * dr if first else dxp_ref[...]
        dx_ref[...] = base + _dot(dgp, wg_ref[0]) + _dot(dup, wu_ref[0])

        @pl.when(pl.program_id(0) == n - 1)
        def _():
            pltpu.sync_copy(ag.at[0:fs], dwg_hbm.at[k])
            pltpu.sync_copy(au.at[0:fs], dwu_hbm.at[k])
            pltpu.sync_copy(ad.at[0:fs], dwd_hbm.at[k])

    row = lambda w: pl.BlockSpec((tm, w), lambda i: (i, 0))
    slab = pl.BlockSpec((tm, fp), lambda i: (i, k))
    wrow = pl.BlockSpec((1, fp, d), lambda i: (k, 0, 0))
    ins = [dr2] + ([] if first else [prev[0]]) + [x1b, gp, up, wg, wu, wd] + ([] if first else list(prev[1:]))
    in_specs = [row(d)] + ([] if first else [row(d)]) + [row(d), slab, slab, wrow, wrow, wrow] + ([] if first else [ANY] * 3)
    return pl.pallas_call(
        body, name=f"bwd_mlp_{k}", grid=(n,),
        in_specs=in_specs,
        out_specs=[row(d), ANY, ANY, ANY],
        out_shape=[jax.ShapeDtypeStruct((t, d), F32)] + [jax.ShapeDtypeStruct((ns, fs, d), F32)] * 3,
        scratch_shapes=[pltpu.VMEM((fp, d), F32)] * 3,
        input_output_aliases={} if first else {8: 1, 9: 2, 10: 3},
        compiler_params=_cparams(),
    )(*ins)


def _bwd_mix(dx1, saved, wi, wo, wstt, cw, cwf, vq, vd, mlp_sums, alpha, kw, tm):
    xb, pag, y, xh1, rs, zu_s, mg_s, vhat_s, gv_s, vnb_s, yhat_s = saved
    t, d = xh1.shape
    q = wi.shape[2]
    nc, n_pairs = tm // CHUNK, q // LANES
    n = t // tm
    hb_per_tile = tm // HALO
    off = HALO - (kw - 1)

    def body(dx1_ref, xb_ref, pag_ref, halo_ref, y_ref, xh_ref, rs_ref, zu_ref, mg_ref, vhat_ref, gv_ref, vnb_ref, yhat_ref,
             wi_hbm, wo_hbm, wstt_ref, cw_ref, cwf_ref, vq_ref, vd_ref, sum_g, sum_u, sum_d,
             gx_ref, dwi_hbm, dwo_hbm, dws_ref, dbs_ref, dcw_ref, dvq_ref, dvd_ref, got_g, got_u, got_d,
             wi_v, wo_v, awi, awo, hb_ref, dyb_ref, sh_ref, dbm_ref, dcw8_ref, send_sems, recv_sems):
        i = pl.program_id(0)
        exchange = lambda: _Exchange((sum_g, sum_u, sum_d), (got_g, got_u, got_d), send_sems, recv_sems)

        @pl.when(i == 0)
        def _():
            exchange().start()
            pltpu.sync_copy(wi_hbm, wi_v)
            pltpu.sync_copy(wo_hbm, wo_v)
            for r in (awi, awo, dws_ref, dbm_ref, dcw8_ref, dvq_ref, dvd_ref, hb_ref, dyb_ref):
                r[...] = jnp.zeros_like(r)

        dx1v = dx1_ref[...]
        xh = xh_ref[...]
        rsv = rs_ref[...]
        dvd_ref[VD_LN1_G:VD_LN1_G + 1, :] += _colsum(dx1v * xh)
        dvd_ref[VD_LN1_B:VD_LN1_B + 1, :] += _colsum(dx1v)
        dr1 = _ln_bwd(dx1v * vd_ref[VD_LN1_G:VD_LN1_G + 1, :], xh, rsv[:, RS_LN1:RS_LN1 + 1])
        dr1b = dr1.astype(BF16)
        awo[...] += _dot_tn(y_ref[...], dr1b)
        dy = _dot_nt(dr1b, wo_v[...])

        vhat = vhat_ref[...]
        sgu_g = vq_ref[VQ_SGU_G:VQ_SGU_G + 1, :]
        doa = dy[:, 0:q]
        dm = doa * zu_ref[...]
        dpu = (doa * mg_ref[...]).astype(BF16)
        acc = dm[0:CHUNK]
        for c in range(1, nc):
            acc = acc + dm[c * CHUNK:(c + 1) * CHUNK]
        dbm_ref[...] += acc
        for p, g in enumerate(_mix_wgrad(dm, vnb_ref[...], nc, n_pairs)):
            dws_ref[p] += g
        dvn = _mix(wstt_ref, dm, nc, n_pairs)
        dvq_ref[VQ_SGU_G:VQ_SGU_G + 1, :] += _colsum(dvn * vhat)
        dvq_ref[VQ_SGU_B:VQ_SGU_B + 1, :] += _colsum(dvn)
        dpv = (_ln_bwd(dvn * sgu_g, vhat, rsv[:, RS_SGU:RS_SGU + 1]) * gv_ref[...]).astype(BF16)

        pa = pag_ref[:, 0:q].astype(F32)
        sg = _sigmoid(pag_ref[:, q:2 * q].astype(F32))
        hb_ref[HALO:HALO + tm, :] = pa * sg
        ha = halo_ref[:, 0:q].astype(F32)
        hg = halo_ref[:, q:2 * q].astype(F32)
        hb_ref[0:HALO, :] = jnp.where(i == n - 1, 0.0, ha * _sigmoid(hg))
        yhat = yhat_ref[...]
        cln_g = vq_ref[VQ_CLN_G:VQ_CLN_G + 1, :]
        yn = yhat * cln_g + vq_ref[VQ_CLN_B:VQ_CLN_B + 1, :]
        sy = _sigmoid(yn)
        dyn = dy[:, q:2 * q] * (sy * (1.0 + yn * (1.0 - sy)))
        dvq_ref[VQ_CLN_G:VQ_CLN_G + 1, :] += _colsum(dyn * yhat)
        dvq_ref[VQ_CLN_B:VQ_CLN_B + 1, :] += _colsum(dyn)
        dyc = _ln_bwd(dyn * cln_g, yhat, rsv[:, RS_CONV:RS_CONV + 1])
        dvq_ref[VQ_CONV_B:VQ_CONV_B + 1, :] += _colsum(dyc)
        dyb_ref[0:tm, :] = dyc
        _shift_rows(sh_ref, hb_ref, tm + HALO)
        _conv_wgrad(dcw8_ref, dyb_ref, sh_ref, tm, q, off, kw)
        _shift_rows(sh_ref, dyb_ref, tm + HALO)
        dh = _conv_taps(sh_ref, cwf_ref, tm, q, 0, kw)
        dyb_ref[tm:tm + HALO, :] = dyb_ref[0:HALO, :]
        da = (dh * sg).astype(BF16)
        dg = (dh * pa * (sg * (1.0 - sg))).astype(BF16)

        xb = xb_ref[...]
        gx = alpha * dr1
        for j, dpj in enumerate((dpu, dpv, da, dg)):
            awi[j] += _dot_tn(xb, dpj)
            gx = gx + _dot_nt(dpj, wi_v[j])
        gx_ref[...] = gx

        @pl.when(i == n - 1)
        def _():
            pltpu.sync_copy(awi, dwi_hbm)
            pltpu.sync_copy(awo, dwo_hbm)
            lane = lax.broadcasted_iota(jnp.int32, (CHUNK, LANES), 1)
            low = lane < HEAD_DIM
            dbs = jnp.zeros((CHUNK, LANES), F32)
            for p in range(n_pairs):
                grp = dbm_ref[:, p * LANES:(p + 1) * LANES]
                dbs = jnp.where(lane == 2 * p, jnp.sum(jnp.where(low, grp, 0.0), axis=1, keepdims=True), dbs)
                dbs = jnp.where(lane == 2 * p + 1, jnp.sum(jnp.where(low, 0.0, grp), axis=1, keepdims=True), dbs)
            dbs_ref[...] = dbs
            for k in range(cw.shape[0]):
                dcw_ref[k:k + 1, :] = _colsum(dcw8_ref[k])
            exchange().finish()

    rev = lambda w: pl.BlockSpec((tm, w), lambda i: (n - 1 - i, 0))
    halo = pl.BlockSpec((HALO, 2 * q), lambda i: (jnp.maximum((n - 1 - i) * hb_per_tile - 1, 0), 0))
    small = [jax.ShapeDtypeStruct((n_pairs, 2 * CHUNK, CHUNK), F32), jax.ShapeDtypeStruct((CHUNK, LANES), F32),
             jax.ShapeDtypeStruct(cw.shape, F32), jax.ShapeDtypeStruct(vq.shape, F32), jax.ShapeDtypeStruct(vd.shape, F32)]
    return pl.pallas_call(
        body, name="bwd_mix", grid=(n,),
        in_specs=[rev(d), rev(d), rev(2 * q), halo, rev(d), rev(d), rev(RS_COLS), rev(q), rev(q), rev(q), rev(q), rev(q), rev(q),
                  ANY, ANY, _full(wstt.shape), _full(cw.shape), _full(cwf.shape), _full(vq.shape), _full(vd.shape)] + [ANY] * 3,
        out_specs=[rev(d), ANY, ANY] + [_full(s.shape) for s in small] + [ANY] * 3,
        out_shape=[jax.ShapeDtypeStruct((t, d), F32), jax.ShapeDtypeStruct(wi.shape, F32), jax.ShapeDtypeStruct(wo.shape, F32)] + small
        + _exchange_shapes(mlp_sums),
        scratch_shapes=[pltpu.VMEM(wi.shape, BF16), pltpu.VMEM(wo.shape, BF16), pltpu.VMEM(wi.shape, F32), pltpu.VMEM(wo.shape, F32),
                        pltpu.VMEM((HALO + tm + SUBLANES, q), F32), pltpu.VMEM((tm + HALO + SUBLANES, q), F32),
                        pltpu.VMEM((SUBLANES, tm + HALO, q), F32), pltpu.VMEM((CHUNK, q), F32),
                        pltpu.VMEM((cw.shape[0], SUBLANES, q), F32)] + _exchange_sems(3),
        compiler_params=_cparams(),
    )(dx1, xb, pag, pag, y, xh1, rs, zu_s, mg_s, vhat_s, gv_s, vnb_s, yhat_s, wi, wo, wstt, cw, cwf, vq, vd, *mlp_sums)


def _prep(me_arr, w_in, w_out, w_gate_t, w_up_t, w_down, conv_w, fp, kwp):
    fs, d = w_down.shape
    kw, cshard = conv_w.shape

    def body(me_ref, wi_ref, wo_ref, wg_ref, wu_ref, wd_ref, cw_ref, oi, oo, og, ou, od, oc):
        oi[...] = wi_ref[...].astype(BF16)
        oo[...] = wo_ref[...].astype(BF16)
        for src, dst in ((wg_ref, og), (wu_ref, ou), (wd_ref, od)):
            dst[0:fs, :] = src[...].astype(BF16)
            if fp > fs:
                dst[fs:fp, :] = jnp.zeros((fp - fs, d), BF16)
        oc[0:kw, :] = cw_ref[...]
        oc[kw:kwp, :] = jnp.zeros((kwp - kw, cshard), F32)

    ins = (w_in, w_out, w_gate_t, w_up_t, w_down, conv_w)
    outs = [jax.ShapeDtypeStruct((4,) + w_in.shape, BF16), jax.ShapeDtypeStruct((4,) + w_out.shape, BF16),
            jax.ShapeDtypeStruct((4, fp, d), BF16), jax.ShapeDtypeStruct((4, fp, d), BF16), jax.ShapeDtypeStruct((4, fp, d), BF16),
            jax.ShapeDtypeStruct((4, kwp, cshard), F32)]
    grid_spec = pltpu.PrefetchScalarGridSpec(
        num_scalar_prefetch=1, grid=(1,),
        in_specs=[pl.BlockSpec(a.shape, lambda i, me: (0, 0)) for a in ins],
        out_specs=[pl.BlockSpec((None,) + o.shape[1:], lambda i, me: (me[0], 0, 0)) for o in outs])
    return pl.pallas_call(body, name="wprep", grid_spec=grid_spec, out_shape=outs, compiler_params=_cparams())(me_arr, *ins)


def _coords():
    return tuple(lax.axis_index(a) for a in MESH_AXES)


def _other_chips(x, y):
    return [(1 - x, y), (x, 1 - y), (1 - x, 1 - y)]


def _remote(src, dst, send_sem, recv_sem, to):
    return pltpu.make_async_remote_copy(src_ref=src, dst_ref=dst, send_sem=send_sem, recv_sem=recv_sem,
                                        device_id=to, device_id_type=MESH_ID)


def _hbm_call(body, name, ins, out_shape, scratch_shapes, aliases=None):
    return pl.pallas_call(
        body, name=name, in_specs=[ANY] * len(ins), out_specs=[ANY] * len(out_shape), out_shape=out_shape,
        scratch_shapes=scratch_shapes, input_output_aliases=aliases or {},
    )(*ins)


class _Gather:
    def __init__(self, bufs, send_sems, recv_sems):
        self.bufs, self.send_sems, self.recv_sems = bufs, send_sems, recv_sems
        self.x, self.y, self.c = _coords()

    def _copies(self, stage):
        x, y, c = self.x, self.y, self.c
        for a, buf in enumerate(self.bufs):
            hr = buf.shape[1] // 2
            for j, chip in enumerate(_other_chips(x, y)):
                if stage == "ici_out":
                    ref, k, to = buf.at[2 * x + y, pl.ds(c * hr, hr)], j, (*chip, c)
                elif stage == "ici_in":
                    ref, k, to = buf.at[2 * chip[0] + chip[1], pl.ds(c * hr, hr)], j, (*chip, c)
                elif stage == "d2d_out":
                    ref, k, to = buf.at[2 * chip[0] + chip[1], pl.ds(c * hr, hr)], 3 + j, (x, y, 1 - c)
                else:
                    ref, k, to = buf.at[2 * chip[0] + chip[1], pl.ds((1 - c) * hr, hr)], 3 + j, (x, y, 1 - c)
                yield _remote(ref, ref, self.send_sems.at[a, k], self.recv_sems.at[a, k], to)

    def start(self):
        for cp in self._copies("ici_out"):
            cp.start()

    def forward(self):
        for landed, onward in zip(self._copies("ici_in"), self._copies("d2d_out")):
            landed.wait_recv()
            onward.start()

    def finish(self):
        for cp in self._copies("d2d_in"):
            cp.wait_recv()
        for stage in ("ici_out", "d2d_out"):
            for cp in self._copies(stage):
                cp.wait_send()


def _gather_sems(n):
    return [pltpu.SemaphoreType.DMA((n, 6)), pltpu.SemaphoreType.DMA((n, 6))]


def _gather_shards(bufs):
    n = len(bufs)

    def body(*refs):
        g = _Gather(refs[n:2 * n], *refs[2 * n:])
        g.start()
        g.forward()
        g.finish()

    return _hbm_call(body, "gather_shards", bufs, [jax.ShapeDtypeStruct(s.shape, s.dtype) for s in bufs],
                     _gather_sems(n), aliases={a: a for a in range(n)})


def _pair_swap(name, arrs):
    n = len(arrs)

    def body(*refs):
        src, land = refs[:n], refs[n:2 * n]
        send_sems, recv_sems = refs[2 * n:]
        x, y, c = _coords()
        copies = []
        for a in range(n):
            s = src[a].at[pl.ds(0, arrs[a].shape[0]), 1 - c] if arrs[a].ndim == 4 else src[a].at[1 - c]
            copies.append(_remote(s, land[a], send_sems.at[a], recv_sems.at[a], (x, y, 1 - c)))
            copies[-1].start()
        for cp in copies:
            cp.wait()

    outs = [jax.ShapeDtypeStruct(s.shape[:-3] + s.shape[-2:], s.dtype) for s in arrs]
    return _hbm_call(body, name, arrs, outs, [pltpu.SemaphoreType.DMA((n,)), pltpu.SemaphoreType.DMA((n,))])


class _Exchange:
    def __init__(self, src, dst, send_sems, recv_sems):
        self.src, self.dst, self.send_sems, self.recv_sems = src, dst, send_sems, recv_sems
        self.x, self.y, self.c = _coords()

    def _copies(self, incoming):
        x, y, c = self.x, self.y, self.c
        for a, (s, d) in enumerate(zip(self.src, self.dst)):
            for j, chip in enumerate(_other_chips(x, y)):
                slot = 2 * chip[0] + chip[1]
                if incoming:
                    out, into = d.at[slot], d.at[slot]
                else:
                    out, into = (s.at[slot] if len(s.shape) == 3 else s), d.at[2 * x + y]
                yield _remote(out, into, self.send_sems.at[a, j], self.recv_sems.at[a, j], (*chip, c))

    def start(self):
        for cp in self._copies(False):
            cp.start()

    def finish(self):
        for cp in self._copies(True):
            cp.wait_recv()
        for cp in self._copies(False):
            cp.wait_send()


def _exchange_sems(n):
    return [pltpu.SemaphoreType.DMA((n, 3)), pltpu.SemaphoreType.DMA((n, 3))]


def _exchange_shapes(arrs):
    return [jax.ShapeDtypeStruct((4,) + s.shape[-2:], s.dtype) for s in arrs]


def _chip_exchange(arrs):
    n = len(arrs)

    def body(*refs):
        ex = _Exchange(refs[:n], refs[n:2 * n], *refs[2 * n:])
        ex.start()
        ex.finish()

    return _hbm_call(body, "chip_exchange", arrs, _exchange_shapes(arrs), _exchange_sems(n))


def _pair_gather(halves):
    n = len(halves)

    def body(*refs):
        src, dst = refs[:n], refs[n:2 * n]
        send_sems, recv_sems = refs[2 * n:]
        x, y, c = _coords()
        copies = [_remote(src[a], dst[a], send_sems.at[a], recv_sems.at[a], (x, y, 1 - c)) for a in range(n)]
        for cp in copies:
            cp.start()
        for cp in copies:
            cp.wait()

    outs = [jax.ShapeDtypeStruct(s.shape, s.dtype) for s in halves]
    return _hbm_call(body, "pair_gather", halves, outs, [pltpu.SemaphoreType.DMA((n,)), pltpu.SemaphoreType.DMA((n,))])


def _pair_sum(a, g, land, c_arr, out_dtype):
    nq, _, hr, cc = g.shape

    def body(c_ref, g_ref, l_ref, o_ref):
        o_ref[...] = (g_ref[...] + l_ref[...]).astype(out_dtype)

    spec = pl.BlockSpec((None, hr, cc), lambda qi, cr: (qi, 0, 0))
    grid_spec = pltpu.PrefetchScalarGridSpec(
        num_scalar_prefetch=1, grid=(nq,),
        in_specs=[pl.BlockSpec((None, None, hr, cc), lambda qi, cr: (qi, cr[0], 0, 0)), spec], out_specs=spec)
    return pl.pallas_call(body, name=f"pair_sum_{a}", grid_spec=grid_spec, out_shape=jax.ShapeDtypeStruct((nq, hr, cc), out_dtype),
                          compiler_params=_cparams())(c_arr, g, land)


def _chip_sum(a, parts, own, me_arr):
    _, hr, cc = parts.shape

    def body(me_ref, p_ref, own_ref, o_ref):
        for mine in range(4):
            @pl.when(me_ref[0] == mine)
            def _():
                term = lambda j: (own_ref if j == mine else p_ref.at[j])[...].astype(F32)
                o_ref[...] = ((term(0) + term(1)) + term(2)) + term(3)

    own_spec = (pl.BlockSpec((None, hr, cc), lambda i, me: (me[0], 0, 0)) if own.ndim == 3
                else pl.BlockSpec((hr, cc), lambda i, me: (0, 0)))
    grid_spec = pltpu.PrefetchScalarGridSpec(
        num_scalar_prefetch=1, grid=(1,),
        in_specs=[pl.BlockSpec((4, hr, cc), lambda i, me: (0, 0, 0)), own_spec],
        out_specs=pl.BlockSpec((hr, cc), lambda i, me: (0, 0)))
    return pl.pallas_call(body, name=f"chip_sum_{a}", grid_spec=grid_spec, out_shape=jax.ShapeDtypeStruct((hr, cc), F32),
                          compiler_params=_cparams())(me_arr, parts, own)


def _row_block(rows, cols, limit=1 << 20):
    best = 8
    for tr in range(8, rows + 1, 8):
        if rows % tr == 0 and tr * cols * 4 <= limit:
            best = tr
    return best


def _adamw(name, w, g_mine, g_other, m, v, c_arr):
    r, c = w.shape
    hr, cg = g_mine.shape
    tr = hr if r % hr == 0 and hr * cg * 4 <= (3 << 19) else math.gcd(_row_block(hr, cg), r)
    per_half = hr // tr
    bc1 = 1.0 - ADAM_B1 ** ADAM_STEP
    bc2 = 1.0 - ADAM_B2 ** ADAM_STEP

    def body(c_ref, w_ref, gm_ref, go_ref, m_ref, v_ref, go, do, mo, vo):
        gv = jnp.where(pl.program_id(0) // per_half == c_ref[0], gm_ref[:, 0:c], go_ref[:, 0:c])
        mn = ADAM_B1 * m_ref[...] + (1.0 - ADAM_B1) * gv
        vn = ADAM_B2 * v_ref[...] + (1.0 - ADAM_B2) * (gv * gv)
        go[...] = gv
        mo[...] = mn
        vo[...] = vn
        do[...] = -ADAM_LR * ((mn / bc1) / (jnp.sqrt(vn / bc2) + ADAM_EPS) + ADAM_WD * w_ref[...])

    blk = pl.BlockSpec((tr, c), lambda i, cr: (i, 0))
    gblk = pl.BlockSpec((tr, cg), lambda i, cr: (i % per_half, 0))
    grid_spec = pltpu.PrefetchScalarGridSpec(num_scalar_prefetch=1, grid=(r // tr,), in_specs=[blk, gblk, gblk, blk, blk],
                                             out_specs=[blk] * 4)
    return pl.pallas_call(body, name=f"adamw_{name}", grid_spec=grid_spec, out_shape=[jax.ShapeDtypeStruct((r, c), F32)] * 4,
                          compiler_params=_cparams())(c_arr, w, g_mine, g_other, m, v)


def _rows128(a):
    return a.reshape(-1, LANES)


def _pad_rows(a, rows):
    return jnp.pad(a, ((0, rows - a.shape[0]), (0, 0)))


def kernel(x, w_in, sgu_ln_g, sgu_ln_b, w_s, b_s, conv_w, conv_b, conv_ln_g, conv_ln_b, w_out, ln1_g, ln1_b, w_gate, w_up, w_down, ln2_g, ln2_b, loss_target, m_w_in, m_sgu_ln_g, m_sgu_ln_b, m_w_s, m_b_s, m_conv_w, m_conv_b, m_conv_ln_g, m_conv_ln_b, m_w_out, m_ln1_g, m_ln1_b, m_w_gate, m_w_up, m_w_down, m_ln2_g, m_ln2_b, v_w_in, v_sgu_ln_g, v_sgu_ln_b, v_w_s, v_b_s, v_conv_w, v_conv_b, v_conv_ln_g, v_conv_ln_b, v_w_out, v_ln1_g, v_ln1_b, v_w_gate, v_w_up, v_w_down, v_ln2_g, v_ln2_b):
    depth, d, q = w_in.shape
    assert depth == 1 and x.shape[0] == 1
    t = x.shape[1]
    heads = w_s.shape[1]
    kw, cshard = conv_w.shape[1], conv_w.shape[2]
    fs = w_gate.shape[2]
    fp = -(-fs // MXU_N) * MXU_N
    n_pairs = q // LANES
    assert heads * HEAD_DIM == q and q % LANES == 0 and w_s.shape[2] == CHUNK and 4 * cshard == q and kw - 1 <= HALO
    alpha = (2.0 * depth) ** 0.25
    tm = min(512, t)
    assert t % tm == 0 and tm % CHUNK == 0
    x2, tgt = x[0], loss_target[0]
    mx, my, mc = _coords()
    me = 2 * mx + my
    c_arr = jnp.reshape(mc, (1,)).astype(jnp.int32)

    kwp = -(-kw // 16) * 16
    me_arr = jnp.reshape(me, (1,)).astype(jnp.int32)
    wi, wo, wg, wu, wd, cw4 = _prep(me_arr, w_in[0], w_out[0], w_gate[0].T, w_up[0].T, w_down[0], conv_w[0], fp, kwp)
    wi, wo, cw4 = _gather_shards([wi, wo, cw4])
    wo = wo.reshape(d, d)
    cw = jnp.transpose(cw4, (1, 0, 2)).reshape(kwp, q)
    cwf = _pad_rows(cw[:kw][::-1], kwp)

    wm = jnp.where(jnp.tril(jnp.ones((CHUNK, CHUNK), bool)), w_s[0], 0.0)
    wst = wm.reshape(n_pairs, 2 * CHUNK, CHUNK).astype(BF16)
    wstt = jnp.transpose(wm, (0, 2, 1)).reshape(n_pairs, 2 * CHUNK, CHUNK).astype(BF16)
    bmat = jnp.repeat(b_s[0].T, HEAD_DIM, axis=1)
    vq = _pad_rows(jnp.concatenate([sgu_ln_g, sgu_ln_b, conv_b, conv_ln_g, conv_ln_b], axis=0), 8)
    vd = _pad_rows(jnp.concatenate([ln1_g, ln1_b, ln2_g, ln2_b], axis=0), 8)

    *saved, wg, wu, wd = _fwd_mix(x2, wi, wo, wst, bmat, cw, vq, vd, [wg, wu, wd], alpha, kw, tm)
    gp, up, x1b, dr2, loss_part, dg2, db2 = _fwd_mlp(saved[3], tgt, wg, wu, wd, vd, alpha, tm)
    mlp_grads = None
    for k in range(4):
        mlp_grads = _bwd_mlp_slab(k, dr2, mlp_grads, x1b, gp, up, wg, wu, wd, alpha, fs, tm)
    dx1 = mlp_grads[0]
    mlp_halves = [b.reshape(4, 2, b.shape[1] // 2, b.shape[2]) for b in mlp_grads[1:]]
    mlp_sums = [_pair_sum(f"mlp{a}", h, l, c_arr, BF16) for a, (h, l) in enumerate(zip(mlp_halves, _pair_swap("pair_swap_mlp", mlp_halves)))]
    grad_x, dwi, dwo, dws, dbs, dcw, dvq, dvd, *mlp_parts = _bwd_mix(dx1, saved, wi, wo, wstt, cw, cwf, vq, vd, mlp_sums, alpha, kw, min(tm, 256))

    dws = jnp.where(jnp.tril(jnp.ones((CHUNK, CHUNK), bool)), dws.reshape(heads, CHUNK, CHUNK), 0.0)
    dvd = dvd.at[VD_LN2_G].set(dg2[0]).at[VD_LN2_B].set(db2[0])
    pieces = [_rows128(dws), dbs[:, :heads].T, _rows128(dcw), _rows128(dvq), _rows128(dvd), loss_part]
    sizes = [p.shape[0] for p in pieces]
    rows = -(-sum(sizes) // 16) * 16
    small = _pad_rows(jnp.concatenate(pieces, axis=0), rows)

    big = [dwi, dwo.reshape(4, d // 4, d)]
    halves = [b.reshape(4, 2, b.shape[1] // 2, b.shape[2]) for b in big] + [small.reshape(2, rows // 2, LANES)]
    landed = _pair_swap("pair_swap_mix", halves)
    sums = [_pair_sum(f"mix{a}", h, l, c_arr, BF16) for a, (h, l) in enumerate(zip(halves[:-1], landed[:-1]))]
    sums.append(_pair_sum("small", halves[-1][None], landed[-1][None], c_arr, F32)[0])
    parts = _chip_exchange(sums)
    parts = list(parts)
    parts, sums = parts[:2] + list(mlp_parts) + parts[2:], sums[:2] + list(mlp_sums) + sums[2:]
    mine = [_chip_sum(a, p, s, me_arr) for a, (p, s) in enumerate(zip(parts, sums))]
    other = _pair_gather(mine)

    def pack(ws, bs, vqs, vds):
        ps = [_rows128(ws[0]), bs[0], jnp.zeros((sizes[2], LANES), F32),
              _rows128(_pad_rows(jnp.concatenate(vqs, axis=0), 8)), _rows128(_pad_rows(jnp.concatenate(vds, axis=0), 8)),
              jnp.zeros((sizes[5], LANES), F32)]
        return _pad_rows(jnp.concatenate(ps, axis=0), rows)

    out = {}
    for a, (nm, w_, m_, v_) in enumerate((("w_in", w_in, m_w_in, v_w_in), ("w_out", w_out, m_w_out, v_w_out),
                                          ("w_gate", w_gate, m_w_gate, v_w_gate), ("w_up", w_up, m_w_up, v_w_up),
                                          ("w_down", w_down, m_w_down, v_w_down))):
        if nm in ("w_gate", "w_up"):
            out[nm] = [o.T for o in _adamw(nm, w_[0].T, mine[a], other[a], m_[0].T, v_[0].T, c_arr)]
        else:
            out[nm] = _adamw(nm, w_[0], mine[a], other[a], m_[0], v_[0], c_arr)
    packed = _adamw(
        "small",
        pack(w_s, b_s, [sgu_ln_g, sgu_ln_b, conv_b, conv_ln_g, conv_ln_b], [ln1_g, ln1_b, ln2_g, ln2_b]), mine[-1], other[-1],
        pack(m_w_s, m_b_s, [m_sgu_ln_g, m_sgu_ln_b, m_conv_b, m_conv_ln_g, m_conv_ln_b], [m_ln1_g, m_ln1_b, m_ln2_g, m_ln2_b]),
        pack(v_w_s, v_b_s, [v_sgu_ln_g, v_sgu_ln_b, v_conv_b, v_conv_ln_g, v_conv_ln_b], [v_ln1_g, v_ln1_b, v_ln2_g, v_ln2_b]),
        c_arr)

    offs = [sum(sizes[:i]) for i in range(len(sizes))]
    g_cw_full = packed[0][offs[2]:offs[2] + sizes[2]].reshape(kwp, q)
    g_cw = lax.dynamic_slice(g_cw_full, (0, me * cshard), (kwp, cshard))
    out["conv_w"] = _adamw("conv_w", _pad_rows(conv_w[0], kwp), g_cw, g_cw, _pad_rows(m_conv_w[0], kwp), _pad_rows(v_conv_w[0], kwp), c_arr)
    out["conv_w"] = [o[:kw] for o in out["conv_w"]]

    def unpack(p):
        vq_o = p[offs[3]:offs[3] + sizes[3]].reshape(8, q)
        vd_o = p[offs[4]:offs[4] + sizes[4]].reshape(8, d)
        return {"w_s": p[offs[0]:offs[0] + sizes[0]].reshape(heads, CHUNK, CHUNK), "b_s": p[offs[1]:offs[1] + sizes[1]],
                "sgu_ln_g": vq_o[VQ_SGU_G], "sgu_ln_b": vq_o[VQ_SGU_B], "conv_b": vq_o[VQ_CONV_B],
                "conv_ln_g": vq_o[VQ_CLN_G], "conv_ln_b": vq_o[VQ_CLN_B],
                "ln1_g": vd_o[VD_LN1_G], "ln1_b": vd_o[VD_LN1_B], "ln2_g": vd_o[VD_LN2_G], "ln2_b": vd_o[VD_LN2_B]}

    small_out = [unpack(p) for p in packed]
    loss = packed[0][offs[5], 0]
    names = ["w_in", "sgu_ln_g", "sgu_ln_b", "w_s", "b_s", "conv_w", "conv_b", "conv_ln_g", "conv_ln_b", "w_out",
             "ln1_g", "ln1_b", "w_gate", "w_up", "w_down", "ln2_g", "ln2_b"]
    result = [loss, grad_x[None]]
    for kind in range(4):
        for nm in names:
            val = out[nm][kind] if nm in out else small_out[kind][nm]
            result.append(val[None])
    return tuple(result)
```

```python
import functools
import math

import jax
import numpy as np
import jax.numpy as jnp
from jax import lax
from jax.experimental import pallas as pl
from jax.experimental.pallas import tpu as pltpu

F32 = jnp.float32
BF16 = jnp.bfloat16

LN_EPS = 1e-5
HEAD_DIM = 64
CHUNK = 128
HALO = 32
LANES = 128
MXU_N = 256
ADAM_LR, ADAM_B1, ADAM_B2, ADAM_EPS, ADAM_WD, ADAM_STEP = 0.001, 0.9, 0.999, 1e-08, 0.01, 10
VMEM_LIMIT = 60 * 1024 * 1024
MESH_AXES = ("x", "y", "c")
MESH_ID = pl.DeviceIdType.MESH


def _dot(a, b):
    return jnp.dot(a, b, preferred_element_type=F32)


def _dot_nt(a, b):
    return lax.dot_general(a, b, (((1,), (1,)), ((), ())), preferred_element_type=F32)


def _dot_tn(a, b):
    return lax.dot_general(a, b, (((0,), (0,)), ((), ())), preferred_element_type=F32)


def _sigmoid(v):
    return 1.0 / (1.0 + jnp.exp(-v))


def _gelu(v):
    cdf = 0.5 * (1.0 + lax.erf(v * (1.0 / math.sqrt(2.0))))
    pdf = jnp.exp(-0.5 * v * v) * (1.0 / math.sqrt(2.0 * math.pi))
    return v * cdf, cdf + v * pdf


def _ln_stats(v):
    mu = jnp.mean(v, axis=-1, keepdims=True)
    d = v - mu
    rstd = lax.rsqrt(jnp.mean(d * d, axis=-1, keepdims=True) + LN_EPS)
    return d * rstd, rstd


def _ln_bwd(dxhat, xhat, rstd):
    m1 = jnp.mean(dxhat, axis=-1, keepdims=True)
    m2 = jnp.mean(dxhat * xhat, axis=-1, keepdims=True)
    return rstd * (dxhat - m1 - xhat * m2)


def _colsum(v):
    return jnp.sum(v, axis=0, keepdims=True)


def _pair_lanes(v, nc, p):
    return jnp.concatenate([v[c * CHUNK:(c + 1) * CHUNK, p * LANES:(p + 1) * LANES] for c in range(nc)], axis=1)


def _unpair(parts, nc):
    rows = [jnp.concatenate([part[:, c * LANES:(c + 1) * LANES] for part in parts], axis=1) for c in range(nc)]
    return jnp.concatenate(rows, axis=0)


def _low_head(nc):
    lane = lax.broadcasted_iota(jnp.int32, (CHUNK, nc * LANES), 1)
    return (lane & (LANES - 1)) < HEAD_DIM


def _mix(wst_ref, v, nc, n_pairs):
    vb = v.astype(BF16)
    low = _low_head(nc)
    parts = []
    for p in range(n_pairs):
        r = _dot(wst_ref[p], _pair_lanes(vb, nc, p))
        parts.append(jnp.where(low, r[:CHUNK], r[CHUNK:]))
    return _unpair(parts, nc)


def _mix_wgrad(dm, vn, nc, n_pairs):
    low = _low_head(nc)
    vb = vn.astype(BF16)
    out = []
    for p in range(n_pairs):
        a = _pair_lanes(dm, nc, p)
        lhs = jnp.concatenate([jnp.where(low, a, 0.0), jnp.where(low, 0.0, a)], axis=0).astype(BF16)
        out.append(_dot_nt(lhs, _pair_lanes(vb, nc, p)))
    return out


SUBLANES = 8


CONV_BLOCK = 256
DFT_N = CONV_BLOCK + HALO
DFT_F = -(-(DFT_N // 2 + 1) // SUBLANES) * SUBLANES


def _three_term(m):
    hi = m.astype(np.float32).astype(BF16)
    lo = (m.astype(np.float32) - hi.astype(np.float32)).astype(BF16)
    return np.concatenate([hi, hi, lo], axis=1)


def _split3(v):
    hi = v.astype(BF16)
    return jnp.concatenate([hi, (v - hi.astype(F32)).astype(BF16), hi], axis=0)


def _dft_tables(kw, kwp):
    nf = DFT_N // 2 + 1
    ang = 2.0 * np.pi * np.arange(nf)[:, None] * np.arange(DFT_N)[None, :] / DFT_N
    fwd = np.zeros((2 * DFT_F, DFT_N))
    fwd[:nf], fwd[DFT_F:DFT_F + nf] = np.cos(ang), -np.sin(ang)
    weight = np.full((nf, 1), 2.0 / DFT_N)
    weight[0] = weight[-1] = 1.0 / DFT_N
    inv = np.zeros((DFT_N, 2 * DFT_F))
    inv[:, :nf], inv[:, DFT_F:DFT_F + nf] = (np.cos(ang) * weight).T, (-np.sin(ang) * weight).T
    inv_taps = np.zeros((kwp, 2 * DFT_F))
    inv_taps[:kw] = inv[kw - 1::-1][:kw]
    return {"fwd": _three_term(fwd), "taps": _three_term(fwd[:, :kwp]), "inv_out": _three_term(inv[HALO:HALO + CONV_BLOCK]),
            "inv_in": _three_term(inv[:CONV_BLOCK]), "inv_taps": _three_term(inv_taps)}


def _cmul(a, b, conj_b=False):
    ar, ai, br, bi = a[:DFT_F], a[DFT_F:], b[:DFT_F], b[DFT_F:]
    if conj_b:
        return jnp.concatenate([ar * br + ai * bi, ai * br - ar * bi], axis=0)
    return jnp.concatenate([ar * br - ai * bi, ar * bi + ai * br], axis=0)


def _cparams():
    return pltpu.CompilerParams(dimension_semantics=("arbitrary",), vmem_limit_bytes=VMEM_LIMIT)


def _full(shape):
    return pl.BlockSpec(shape, lambda i: (0,) * len(shape))


ANY = pl.BlockSpec(memory_space=pl.ANY)

VQ_SGU_G, VQ_SGU_B, VQ_CONV_B, VQ_CLN_G, VQ_CLN_B = range(5)
VD_LN1_G, VD_LN1_B, VD_LN2_G, VD_LN2_B = range(4)
RS_LN1, RS_SGU, RS_CONV = range(3)
RS_COLS = 8


def _fwd_mix(x, wi, wo, wst, bmat, cwf, tabs, vq, vd, mlp_w, alpha, tm):
    t, d = x.shape
    q = wi.shape[2]
    nc, n_pairs = tm // CHUNK, q // LANES
    n = t // tm
    n_in, n_saved = 11, 11
    assert tm % CONV_BLOCK == 0

    def body(x_ref, wi_hbm, wo_hbm, wst_ref, bmat_ref, cwf_ref, fwd_ref, taps_ref, inv_ref, vq_ref, vd_ref, *rest):
        (xb_ref, pag_ref, y_ref, xh_ref, rs_ref, zu_ref, mg_ref, vhat_ref, gv_ref, vnb_ref, yhat_ref) = rest[3:3 + n_saved]
        gathered = rest[3 + n_saved:6 + n_saved]
        wi_v, wo_v, hb_ref, gf_ref, send_sems, recv_sems = rest[6 + n_saved:]
        step = pl.program_id(0)

        @pl.when(step == 0)
        def _():
            _Gather(gathered, send_sems, recv_sems).start()
            pltpu.sync_copy(wi_hbm, wi_v)
            pltpu.sync_copy(wo_hbm, wo_v)
            hb_ref[...] = jnp.zeros_like(hb_ref)
            gf_ref[...] = _dot(taps_ref[...], _split3(cwf_ref[...]))

        @pl.when(step == (3 * n) // 4)
        def _():
            _Gather(gathered, send_sems, recv_sems).forward()

        xv = x_ref[...]
        xb = xv.astype(BF16)
        xb_ref[...] = xb
        pu, pv, pa, pg = (_dot(xb, wi_v[j]) for j in range(4))
        pag_ref[:, 0:q] = pa.astype(BF16)
        pag_ref[:, q:2 * q] = pg.astype(BF16)
        zu, gu = _gelu(pu)
        zv, gv = _gelu(pv)
        vhat, rstd_v = _ln_stats(zv)
        vnb = (vhat * vq_ref[VQ_SGU_G:VQ_SGU_G + 1, :] + vq_ref[VQ_SGU_B:VQ_SGU_B + 1, :]).astype(BF16)
        mixed = _mix(wst_ref, vnb, nc, n_pairs) + jnp.concatenate([bmat_ref[...]] * nc, axis=0)
        y_ref[:, 0:q] = (zu * mixed).astype(BF16)
        zu_ref[...] = zu
        mg_ref[...] = mixed * gu
        vhat_ref[...] = vhat
        gv_ref[...] = gv
        vnb_ref[...] = vnb

        hb_ref[HALO:HALO + tm, :] = pa * _sigmoid(pg)
        blocks = []
        for r0 in range(0, tm, CONV_BLOCK):
            spectrum = _dot(fwd_ref[...], _split3(hb_ref[r0:r0 + DFT_N, :]))
            blocks.append(_dot(inv_ref[...], _split3(_cmul(gf_ref[...], spectrum))))
        yc = jnp.concatenate(blocks, axis=0) + vq_ref[VQ_CONV_B:VQ_CONV_B + 1, :]
        hb_ref[0:HALO, :] = hb_ref[tm:tm + HALO, :]
        yhat, rstd_c = _ln_stats(yc)
        yhat_ref[...] = yhat
        yn = yhat * vq_ref[VQ_CLN_G:VQ_CLN_G + 1, :] + vq_ref[VQ_CLN_B:VQ_CLN_B + 1, :]
        y_ref[:, q:2 * q] = (yn * _sigmoid(yn)).astype(BF16)

        r1 = alpha * xv + _dot(y_ref[...], wo_v[...])
        xhat, rstd1 = _ln_stats(r1)
        xh_ref[...] = xhat
        col = lax.broadcasted_iota(jnp.int32, (tm, RS_COLS), 1)
        rs_ref[...] = jnp.where(col == RS_LN1, rstd1, jnp.where(col == RS_SGU, rstd_v, jnp.where(col == RS_CONV, rstd_c, 0.0)))

        @pl.when(step == n - 1)
        def _():
            _Gather(gathered, send_sems, recv_sems).finish()

    row = lambda w: pl.BlockSpec((tm, w), lambda i: (i, 0))
    widths = [(d, BF16), (2 * q, BF16), (d, BF16), (d, F32), (RS_COLS, F32), (q, F32), (q, F32), (q, F32), (q, F32), (q, BF16), (q, F32)]
    assert len(widths) == n_saved
    small_ins = [wst, bmat, cwf, tabs["fwd"], tabs["taps"], tabs["inv_out"], vq, vd]
    return pl.pallas_call(
        body, name="fwd_mix", grid=(n,),
        in_specs=[row(d), ANY, ANY] + [_full(a.shape) for a in small_ins] + [ANY] * 3,
        out_specs=[row(w) for w, _ in widths] + [ANY] * 3,
        out_shape=[jax.ShapeDtypeStruct((t, w), dt) for w, dt in widths] + [jax.ShapeDtypeStruct(b.shape, b.dtype) for b in mlp_w],
        scratch_shapes=[pltpu.VMEM(wi.shape, BF16), pltpu.VMEM(wo.shape, BF16), pltpu.VMEM((HALO + tm, q), F32),
                        pltpu.VMEM((2 * DFT_F, q), F32)] + _gather_sems(3),
        input_output_aliases={n_in + a: n_saved + a for a in range(3)},
        compiler_params=_cparams(),
    )(x, wi, wo, *small_ins, *mlp_w)


def _fwd_mlp(xh1, tgt, wg, wu, wd, vd, alpha, tm):
    t, d = xh1.shape
    ns, fp, _ = wg.shape
    n = t // tm

    def body(xh_ref, tgt_ref, wg_hbm, wu_hbm, wd_hbm, vd_ref,
             gp_ref, up_ref, x1b_ref, dr2_ref, loss_ref, dg2_ref, db2_ref, wg_v, wu_v, wd_v):
        @pl.when(pl.program_id(0) == 0)
        def _():
            pltpu.sync_copy(wg_hbm, wg_v)
            pltpu.sync_copy(wu_hbm, wu_v)
            pltpu.sync_copy(wd_hbm, wd_v)
            loss_ref[...] = jnp.zeros_like(loss_ref)
            dg2_ref[...] = jnp.zeros_like(dg2_ref)
            db2_ref[...] = jnp.zeros_like(db2_ref)

        x1 = xh_ref[...] * vd_ref[VD_LN1_G:VD_LN1_G + 1, :] + vd_ref[VD_LN1_B:VD_LN1_B + 1, :]
        x1b = x1.astype(BF16)
        x1b_ref[...] = x1b
        acc = alpha * x1
        for k in range(ns):
            gp = _dot_nt(x1b, wg_v[k])
            up = _dot_nt(x1b, wu_v[k])
            gp_ref[:, k * fp:(k + 1) * fp] = gp.astype(BF16)
            up_ref[:, k * fp:(k + 1) * fp] = up.astype(BF16)
            acc = acc + _dot((gp * _sigmoid(gp) * up).astype(BF16), wd_v[k])
        xh2, rstd2 = _ln_stats(acc)
        g2 = vd_ref[VD_LN2_G:VD_LN2_G + 1, :]
        err = xh2 * g2 + vd_ref[VD_LN2_B:VD_LN2_B + 1, :] - tgt_ref[...]
        loss_ref[...] += _colsum(jnp.sum(err * err, axis=1, keepdims=True)) * (0.5 / d)
        dy = err * (1.0 / d)
        dg2_ref[...] += _colsum(dy * xh2)
        db2_ref[...] += _colsum(dy)
        dr2_ref[...] = _ln_bwd(dy * g2, xh2, rstd2)

    row = lambda w: pl.BlockSpec((tm, w), lambda i: (i, 0))
    return pl.pallas_call(
        body, name="fwd_mlp", grid=(n,),
        in_specs=[row(d), row(d), ANY, ANY, ANY, _full(vd.shape)],
        out_specs=[row(ns * fp), row(ns * fp), row(d), row(d), _full((8, LANES)), _full((1, d)), _full((1, d))],
        out_shape=[jax.ShapeDtypeStruct((t, ns * fp), BF16), jax.ShapeDtypeStruct((t, ns * fp), BF16),
                   jax.ShapeDtypeStruct((t, d), BF16), jax.ShapeDtypeStruct((t, d), F32),
                   jax.ShapeDtypeStruct((8, LANES), F32), jax.ShapeDtypeStruct((1, d), F32), jax.ShapeDtypeStruct((1, d), F32)],
        scratch_shapes=[pltpu.VMEM(wg.shape, BF16), pltpu.VMEM(wu.shape, BF16), pltpu.VMEM(wd.shape, BF16)],
        compiler_params=_cparams(),
    )(xh1, tgt, wg, wu, wd, vd)


def _bwd_mlp_slab(k, dr2, prev, x1b, gp, up, wg, wu, wd, alpha, fs, tm):
    t, d = dr2.shape
    ns, fp, _ = wg.shape
    n = t // tm
    first = prev is None

    def body(*refs):
        if first:
            dr_ref, x1b_ref, gp_ref, up_ref, wg_ref, wu_ref, wd_ref, dx_ref, dwg_hbm, dwu_hbm, dwd_hbm, ag, au, ad = refs
        else:
            (dr_ref, dxp_ref, x1b_ref, gp_ref, up_ref, wg_ref, wu_ref, wd_ref, _, _, _,
             dx_ref, dwg_hbm, dwu_hbm, dwd_hbm, ag, au, ad) = refs

        @pl.when(pl.program_id(0) == 0)
        def _():
            ag[...] = jnp.zeros_like(ag)
            au[...] = jnp.zeros_like(au)
            ad[...] = jnp.zeros_like(ad)

        dr = dr_ref[...]
        drb = dr.astype(BF16)
        x1b = x1b_ref[...]
        gpv = gp_ref[...].astype(F32)
        upv = up_ref[...].astype(F32)
        dh = _dot_nt(drb, wd_ref[0])
        sg = _sigmoid(gpv)
        silu = gpv * sg
        ad[...] += _dot_tn((silu * upv).astype(BF16), drb)
        dgp = (dh * upv * (sg * (1.0 + gpv * (1.0 - sg)))).astype(BF16)
        dup = (dh * silu).astype(BF16)
        ag[...] += _dot_tn(dgp, x1b)
        au[...] += _dot_tn(dup, x1b)
        base = alpha * dr if first else dxp_ref[...]
        dx_ref[...] = base + _dot(dgp, wg_ref[0]) + _dot(dup, wu_ref[0])

        @pl.when(pl.program_id(0) == n - 1)
        def _():
            pltpu.sync_copy(ag.at[0:fs], dwg_hbm.at[k])
            pltpu.sync_copy(au.at[0:fs], dwu_hbm.at[k])
            pltpu.sync_copy(ad.at[0:fs], dwd_hbm.at[k])

    row = lambda w: pl.BlockSpec((tm, w), lambda i: (i, 0))
    slab = pl.BlockSpec((tm, fp), lambda i: (i, k))
    wrow = pl.BlockSpec((1, fp, d), lambda i: (k, 0, 0))
    ins = [dr2] + ([] if first else [prev[0]]) + [x1b, gp, up, wg, wu, wd] + ([] if first else list(prev[1:]))
    in_specs = [row(d)] + ([] if first else [row(d)]) + [row(d), slab, slab, wrow, wrow, wrow] + ([] if first else [ANY] * 3)
    return pl.pallas_call(
        body, name=f"bwd_mlp_{k}", grid=(n,),
        in_specs=in_specs,
        out_specs=[row(d), ANY, ANY, ANY],
        out_shape=[jax.ShapeDtypeStruct((t, d), F32)] + [jax.ShapeDtypeStruct((ns, fs, d), F32)] * 3,
        scratch_shapes=[pltpu.VMEM((fp, d), F32)] * 3,
        input_output_aliases={} if first else {8: 1, 9: 2, 10: 3},
        compiler_params=_cparams(),
    )(*ins)


def _bwd_mix(dx1, saved, wi, wo, wstt, cwf, tabs, vq, vd, mlp_sums, alpha, tm):
    xb, pag, y, xh1, rs, zu_s, mg_s, vhat_s, gv_s, vnb_s, yhat_s = saved
    t, d = xh1.shape
    q = wi.shape[2]
    nc, n_pairs = tm // CHUNK, q // LANES
    n = t // tm
    hb_per_tile = tm // HALO
    assert tm == CONV_BLOCK

    def body(dx1_ref, xb_ref, pag_ref, halo_ref, y_ref, xh_ref, rs_ref, zu_ref, mg_ref, vhat_ref, gv_ref, vnb_ref, yhat_ref,
             wi_hbm, wo_hbm, wstt_ref, cwf_ref, fwd_ref, taps_ref, inv_ref, inv_taps_ref, vq_ref, vd_ref, sum_g, sum_u, sum_d,
             gx_ref, dwi_hbm, dwo_hbm, dws_ref, dbs_ref, dcw_ref, dvq_ref, dvd_ref, got_g, got_u, got_d,
             wi_v, wo_v, awi, awo, hb_ref, dyb_ref, dyc_ref, dbm_ref, gf_ref, dgf_ref, send_sems, recv_sems):
        i = pl.program_id(0)
        exchange = lambda: _Exchange((sum_g, sum_u, sum_d), (got_g, got_u, got_d), send_sems, recv_sems)

        @pl.when(i == 0)
        def _():
            exchange().start()
            pltpu.sync_copy(wi_hbm, wi_v)
            pltpu.sync_copy(wo_hbm, wo_v)
            for r in (awi, awo, dws_ref, dbm_ref, dgf_ref, dvq_ref, dvd_ref, hb_ref, dyb_ref, dyc_ref):
                r[...] = jnp.zeros_like(r)
            gf_ref[...] = _dot(taps_ref[...], _split3(cwf_ref[...]))

        dx1v = dx1_ref[...]
        xh = xh_ref[...]
        rsv = rs_ref[...]
        dvd_ref[VD_LN1_G:VD_LN1_G + 1, :] += _colsum(dx1v * xh)
        dvd_ref[VD_LN1_B:VD_LN1_B + 1, :] += _colsum(dx1v)
        dr1 = _ln_bwd(dx1v * vd_ref[VD_LN1_G:VD_LN1_G + 1, :], xh, rsv[:, RS_LN1:RS_LN1 + 1])
        dr1b = dr1.astype(BF16)
        awo[...] += _dot_tn(y_ref[...], dr1b)
        dy = _dot_nt(dr1b, wo_v[...])

        vhat = vhat_ref[...]
        sgu_g = vq_ref[VQ_SGU_G:VQ_SGU_G + 1, :]
        doa = dy[:, 0:q]
        dm = doa * zu_ref[...]
        dpu = (doa * mg_ref[...]).astype(BF16)
        acc = dm[0:CHUNK]
        for c in range(1, nc):
            acc = acc + dm[c * CHUNK:(c + 1) * CHUNK]
        dbm_ref[...] += acc
        for p, g in enumerate(_mix_wgrad(dm, vnb_ref[...], nc, n_pairs)):
            dws_ref[p] += g
        dvn = _mix(wstt_ref, dm, nc, n_pairs)
        dvq_ref[VQ_SGU_G:VQ_SGU_G + 1, :] += _colsum(dvn * vhat)
        dvq_ref[VQ_SGU_B:VQ_SGU_B + 1, :] += _colsum(dvn)
        dpv = (_ln_bwd(dvn * sgu_g, vhat, rsv[:, RS_SGU:RS_SGU + 1]) * gv_ref[...]).astype(BF16)

        pa = pag_ref[:, 0:q].astype(F32)
        sg = _sigmoid(pag_ref[:, q:2 * q].astype(F32))
        hb_ref[HALO:HALO + tm, :] = pa * sg
        ha = halo_ref[:, 0:q].astype(F32)
        hg = halo_ref[:, q:2 * q].astype(F32)
        hb_ref[0:HALO, :] = jnp.where(i == n - 1, 0.0, ha * _sigmoid(hg))
        yhat = yhat_ref[...]
        cln_g = vq_ref[VQ_CLN_G:VQ_CLN_G + 1, :]
        yn = yhat * cln_g + vq_ref[VQ_CLN_B:VQ_CLN_B + 1, :]
        sy = _sigmoid(yn)
        dyn = dy[:, q:2 * q] * (sy * (1.0 + yn * (1.0 - sy)))
        dvq_ref[VQ_CLN_G:VQ_CLN_G + 1, :] += _colsum(dyn * yhat)
        dvq_ref[VQ_CLN_B:VQ_CLN_B + 1, :] += _colsum(dyn)
        dyc = _ln_bwd(dyn * cln_g, yhat, rsv[:, RS_CONV:RS_CONV + 1])
        dvq_ref[VQ_CONV_B:VQ_CONV_B + 1, :] += _colsum(dyc)
        dyb_ref[0:tm, :] = dyc
        dyc_ref[HALO:HALO + tm, :] = dyc
        spectrum = lambda ref: _dot(fwd_ref[...], _split3(ref[...]))
        dgf_ref[...] += _cmul(spectrum(dyc_ref), spectrum(hb_ref), conj_b=True)
        dh = _dot(inv_ref[...], _split3(_cmul(spectrum(dyb_ref), gf_ref[...], conj_b=True)))
        dyb_ref[tm:tm + HALO, :] = dyb_ref[0:HALO, :]
        da = (dh * sg).astype(BF16)
        dg = (dh * pa * (sg * (1.0 - sg))).astype(BF16)

        xb = xb_ref[...]
        gx = alpha * dr1
        for j, dpj in enumerate((dpu, dpv, da, dg)):
            awi[j] += _dot_tn(xb, dpj)
            gx = gx + _dot_nt(dpj, wi_v[j])
        gx_ref[...] = gx

        @pl.when(i == n - 1)
        def _():
            pltpu.sync_copy(awi, dwi_hbm)
            pltpu.sync_copy(awo, dwo_hbm)
            lane = lax.broadcasted_iota(jnp.int32, (CHUNK, LANES), 1)
            low = lane < HEAD_DIM
            dbs = jnp.zeros((CHUNK, LANES), F32)
            for p in range(n_pairs):
                grp = dbm_ref[:, p * LANES:(p + 1) * LANES]
                dbs = jnp.where(lane == 2 * p, jnp.sum(jnp.where(low, grp, 0.0), axis=1, keepdims=True), dbs)
                dbs = jnp.where(lane == 2 * p + 1, jnp.sum(jnp.where(low, 0.0, grp), axis=1, keepdims=True), dbs)
            dbs_ref[...] = dbs
            dcw_ref[...] = _dot(inv_taps_ref[...], _split3(dgf_ref[...]))
            exchange().finish()

    rev = lambda w: pl.BlockSpec((tm, w), lambda i: (n - 1 - i, 0))
    halo = pl.BlockSpec((HALO, 2 * q), lambda i: (jnp.maximum((n - 1 - i) * hb_per_tile - 1, 0), 0))
    small = [jax.ShapeDtypeStruct((n_pairs, 2 * CHUNK, CHUNK), F32), jax.ShapeDtypeStruct((CHUNK, LANES), F32),
             jax.ShapeDtypeStruct(cwf.shape, F32), jax.ShapeDtypeStruct(vq.shape, F32), jax.ShapeDtypeStruct(vd.shape, F32)]
    small_ins = [wstt, cwf, tabs["fwd"], tabs["taps"], tabs["inv_in"], tabs["inv_taps"], vq, vd]
    return pl.pallas_call(
        body, name="bwd_mix", grid=(n,),
        in_specs=[rev(d), rev(d), rev(2 * q), halo, rev(d), rev(d), rev(RS_COLS), rev(q), rev(q), rev(q), rev(q), rev(q), rev(q),
                  ANY, ANY] + [_full(a.shape) for a in small_ins] + [ANY] * 3,
        out_specs=[rev(d), ANY, ANY] + [_full(s.shape) for s in small] + [ANY] * 3,
        out_shape=[jax.ShapeDtypeStruct((t, d), F32), jax.ShapeDtypeStruct(wi.shape, F32), jax.ShapeDtypeStruct(wo.shape, F32)] + small
        + _exchange_shapes(mlp_sums),
        scratch_shapes=[pltpu.VMEM(wi.shape, BF16), pltpu.VMEM(wo.shape, BF16), pltpu.VMEM(wi.shape, F32), pltpu.VMEM(wo.shape, F32),
                        pltpu.VMEM((DFT_N, q), F32), pltpu.VMEM((DFT_N, q), F32), pltpu.VMEM((DFT_N, q), F32),
                        pltpu.VMEM((CHUNK, q), F32), pltpu.VMEM((2 * DFT_F, q), F32), pltpu.VMEM((2 * DFT_F, q), F32)]
        + _exchange_sems(3),
        compiler_params=_cparams(),
    )(dx1, xb, pag, pag, y, xh1, rs, zu_s, mg_s, vhat_s, gv_s, vnb_s, yhat_s, wi, wo, *small_ins, *mlp_sums)


def _prep(me_arr, w_in, w_out, w_gate_t, w_up_t, w_down, conv_w, fp, kwp):
    fs, d = w_down.shape
    kw, cshard = conv_w.shape

    def body(me_ref, wi_ref, wo_ref, wg_ref, wu_ref, wd_ref, cw_ref, oi, oo, og, ou, od, oc):
        oi[...] = wi_ref[...].astype(BF16)
        oo[...] = wo_ref[...].astype(BF16)
        for src, dst in ((wg_ref, og), (wu_ref, ou), (wd_ref, od)):
            dst[0:fs, :] = src[...].astype(BF16)
            if fp > fs:
                dst[fs:fp, :] = jnp.zeros((fp - fs, d), BF16)
        oc[0:kw, :] = cw_ref[...]
        oc[kw:kwp, :] = jnp.zeros((kwp - kw, cshard), F32)

    ins = (w_in, w_out, w_gate_t, w_up_t, w_down, conv_w)
    outs = [jax.ShapeDtypeStruct((4,) + w_in.shape, BF16), jax.ShapeDtypeStruct((4,) + w_out.shape, BF16),
            jax.ShapeDtypeStruct((4, fp, d), BF16), jax.ShapeDtypeStruct((4, fp, d), BF16), jax.ShapeDtypeStruct((4, fp, d), BF16),
            jax.ShapeDtypeStruct((4, kwp, cshard), F32)]
    grid_spec = pltpu.PrefetchScalarGridSpec(
        num_scalar_prefetch=1, grid=(1,),
        in_specs=[pl.BlockSpec(a.shape, lambda i, me: (0, 0)) for a in ins],
        out_specs=[pl.BlockSpec((None,) + o.shape[1:], lambda i, me: (me[0], 0, 0)) for o in outs])
    return pl.pallas_call(body, name="wprep", grid_spec=grid_spec, out_shape=outs, compiler_params=_cparams())(me_arr, *ins)


def _coords():
    return tuple(lax.axis_index(a) for a in MESH_AXES)


def _other_chips(x, y):
    return [(1 - x, y), (x, 1 - y), (1 - x, 1 - y)]


def _remote(src, dst, send_sem, recv_sem, to):
    return pltpu.make_async_remote_copy(src_ref=src, dst_ref=dst, send_sem=send_sem, recv_sem=recv_sem,
                                        device_id=to, device_id_type=MESH_ID)


def _hbm_call(body, name, ins, out_shape, scratch_shapes, aliases=None):
    return pl.pallas_call(
        body, name=name, in_specs=[ANY] * len(ins), out_specs=[ANY] * len(out_shape), out_shape=out_shape,
        scratch_shapes=scratch_shapes, input_output_aliases=aliases or {},
    )(*ins)


class _Gather:
    def __init__(self, bufs, send_sems, recv_sems):
        self.bufs, self.send_sems, self.recv_sems = bufs, send_sems, recv_sems
        self.x, self.y, self.c = _coords()

    def _copies(self, stage):
        x, y, c = self.x, self.y, self.c
        for a, buf in enumerate(self.bufs):
            hr = buf.shape[1] // 2
            for j, chip in enumerate(_other_chips(x, y)):
                if stage == "ici_out":
                    ref, k, to = buf.at[2 * x + y, pl.ds(c * hr, hr)], j, (*chip, c)
                elif stage == "ici_in":
                    ref, k, to = buf.at[2 * chip[0] + chip[1], pl.ds(c * hr, hr)], j, (*chip, c)
                elif stage == "d2d_out":
                    ref, k, to = buf.at[2 * chip[0] + chip[1], pl.ds(c * hr, hr)], 3 + j, (x, y, 1 - c)
                else:
                    ref, k, to = buf.at[2 * chip[0] + chip[1], pl.ds((1 - c) * hr, hr)], 3 + j, (x, y, 1 - c)
                yield _remote(ref, ref, self.send_sems.at[a, k], self.recv_sems.at[a, k], to)

    def start(self):
        for cp in self._copies("ici_out"):
            cp.start()

    def forward(self):
        for landed, onward in zip(self._copies("ici_in"), self._copies("d2d_out")):
            landed.wait_recv()
            onward.start()

    def finish(self):
        for cp in self._copies("d2d_in"):
            cp.wait_recv()
        for stage in ("ici_out", "d2d_out"):
            for cp in self._copies(stage):
                cp.wait_send()


def _gather_sems(n):
    return [pltpu.SemaphoreType.DMA((n, 6)), pltpu.SemaphoreType.DMA((n, 6))]


def _gather_shards(bufs):
    n = len(bufs)

    def body(*refs):
        g = _Gather(refs[n:2 * n], *refs[2 * n:])
        g.start()
        g.forward()
        g.finish()

    return _hbm_call(body, "gather_shards", bufs, [jax.ShapeDtypeStruct(s.shape, s.dtype) for s in bufs],
                     _gather_sems(n), aliases={a: a for a in range(n)})


def _pair_swap(name, arrs):
    n = len(arrs)

    def body(*refs):
        src, land = refs[:n], refs[n:2 * n]
        send_sems, recv_sems = refs[2 * n:]
        x, y, c = _coords()
        copies = []
        for a in range(n):
            s = src[a].at[pl.ds(0, arrs[a].shape[0]), 1 - c] if arrs[a].ndim == 4 else src[a].at[1 - c]
            copies.append(_remote(s, land[a], send_sems.at[a], recv_sems.at[a], (x, y, 1 - c)))
            copies[-1].start()
        for cp in copies:
            cp.wait()

    outs = [jax.ShapeDtypeStruct(s.shape[:-3] + s.shape[-2:], s.dtype) for s in arrs]
    return _hbm_call(body, name, arrs, outs, [pltpu.SemaphoreType.DMA((n,)), pltpu.SemaphoreType.DMA((n,))])


class _Exchange:
    def __init__(self, src, dst, send_sems, recv_sems):
        self.src, self.dst, self.send_sems, self.recv_sems = src, dst, send_sems, recv_sems
        self.x, self.y, self.c = _coords()

    def _copies(self, incoming):
        x, y, c = self.x, self.y, self.c
        for a, (s, d) in enumerate(zip(self.src, self.dst)):
            for j, chip in enumerate(_other_chips(x, y)):
                slot = 2 * chip[0] + chip[1]
                if incoming:
                    out, into = d.at[slot], d.at[slot]
                else:
                    out, into = (s.at[slot] if len(s.shape) == 3 else s), d.at[2 * x + y]
                yield _remote(out, into, self.send_sems.at[a, j], self.recv_sems.at[a, j], (*chip, c))

    def start(self):
        for cp in self._copies(False):
            cp.start()

    def finish(self):
        for cp in self._copies(True):
            cp.wait_recv()
        for cp in self._copies(False):
            cp.wait_send()


def _exchange_sems(n):
    return [pltpu.SemaphoreType.DMA((n, 3)), pltpu.SemaphoreType.DMA((n, 3))]


def _exchange_shapes(arrs):
    return [jax.ShapeDtypeStruct((4,) + s.shape[-2:], s.dtype) for s in arrs]


def _chip_exchange(arrs):
    n = len(arrs)

    def body(*refs):
        ex = _Exchange(refs[:n], refs[n:2 * n], *refs[2 * n:])
        ex.start()
        ex.finish()

    return _hbm_call(body, "chip_exchange", arrs, _exchange_shapes(arrs), _exchange_sems(n))


def _pair_gather(halves):
    n = len(halves)

    def body(*refs):
        src, dst = refs[:n], refs[n:2 * n]
        send_sems, recv_sems = refs[2 * n:]
        x, y, c = _coords()
        copies = [_remote(src[a], dst[a], send_sems.at[a], recv_sems.at[a], (x, y, 1 - c)) for a in range(n)]
        for cp in copies:
            cp.start()
        for cp in copies:
            cp.wait()

    outs = [jax.ShapeDtypeStruct(s.shape, s.dtype) for s in halves]
    return _hbm_call(body, "pair_gather", halves, outs, [pltpu.SemaphoreType.DMA((n,)), pltpu.SemaphoreType.DMA((n,))])


def _pair_sum(a, g, land, c_arr, out_dtype):
    nq, _, hr, cc = g.shape

    def body(c_ref, g_ref, l_ref, o_ref):
        o_ref[...] = (g_ref[...] + l_ref[...]).astype(out_dtype)

    spec = pl.BlockSpec((None, hr, cc), lambda qi, cr: (qi, 0, 0))
    grid_spec = pltpu.PrefetchScalarGridSpec(
        num_scalar_prefetch=1, grid=(nq,),
        in_specs=[pl.BlockSpec((None, None, hr, cc), lambda qi, cr: (qi, cr[0], 0, 0)), spec], out_specs=spec)
    return pl.pallas_call(body, name=f"pair_sum_{a}", grid_spec=grid_spec, out_shape=jax.ShapeDtypeStruct((nq, hr, cc), out_dtype),
                          compiler_params=_cparams())(c_arr, g, land)


def _chip_sum(a, parts, own, me_arr):
    _, hr, cc = parts.shape

    def body(me_ref, p_ref, own_ref, o_ref):
        for mine in range(4):
            @pl.when(me_ref[0] == mine)
            def _():
                term = lambda j: (own_ref if j == mine else p_ref.at[j])[...].astype(F32)
                o_ref[...] = ((term(0) + term(1)) + term(2)) + term(3)

    own_spec = (pl.BlockSpec((None, hr, cc), lambda i, me: (me[0], 0, 0)) if own.ndim == 3
                else pl.BlockSpec((hr, cc), lambda i, me: (0, 0)))
    grid_spec = pltpu.PrefetchScalarGridSpec(
        num_scalar_prefetch=1, grid=(1,),
        in_specs=[pl.BlockSpec((4, hr, cc), lambda i, me: (0, 0, 0)), own_spec],
        out_specs=pl.BlockSpec((hr, cc), lambda i, me: (0, 0)))
    return pl.pallas_call(body, name=f"chip_sum_{a}", grid_spec=grid_spec, out_shape=jax.ShapeDtypeStruct((hr, cc), F32),
                          compiler_params=_cparams())(me_arr, parts, own)


def _row_block(rows, cols, limit=1 << 20):
    best = 8
    for tr in range(8, rows + 1, 8):
        if rows % tr == 0 and tr * cols * 4 <= limit:
            best = tr
    return best


def _adamw(name, w, g_mine, g_other, m, v, c_arr):
    r, c = w.shape
    hr, cg = g_mine.shape
    tr = hr if r % hr == 0 and hr * cg * 4 <= (3 << 19) else math.gcd(_row_block(hr, cg), r)
    per_half = hr // tr
    bc1 = 1.0 - ADAM_B1 ** ADAM_STEP
    bc2 = 1.0 - ADAM_B2 ** ADAM_STEP

    def body(c_ref, w_ref, gm_ref, go_ref, m_ref, v_ref, go, do, mo, vo):
        gv = jnp.where(pl.program_id(0) // per_half == c_ref[0], gm_ref[:, 0:c], go_ref[:, 0:c])
        mn = ADAM_B1 * m_ref[...] + (1.0 - ADAM_B1) * gv
        vn = ADAM_B2 * v_ref[...] + (1.0 - ADAM_B2) * (gv * gv)
        go[...] = gv
        mo[...] = mn
        vo[...] = vn
        do[...] = -ADAM_LR * ((mn / bc1) / (jnp.sqrt(vn / bc2) + ADAM_EPS) + ADAM_WD * w_ref[...])

    blk = pl.BlockSpec((tr, c), lambda i, cr: (i, 0))
    gblk = pl.BlockSpec((tr, cg), lambda i, cr: (i % per_half, 0))
    grid_spec = pltpu.PrefetchScalarGridSpec(num_scalar_prefetch=1, grid=(r // tr,), in_specs=[blk, gblk, gblk, blk, blk],
                                             out_specs=[blk] * 4)
    return pl.pallas_call(body, name=f"adamw_{name}", grid_spec=grid_spec, out_shape=[jax.ShapeDtypeStruct((r, c), F32)] * 4,
                          compiler_params=_cparams())(c_arr, w, g_mine, g_other, m, v)


def _rows128(a):
    return a.reshape(-1, LANES)


def _pad_rows(a, rows):
    return jnp.pad(a, ((0, rows - a.shape[0]), (0, 0)))


def kernel(x, w_in, sgu_ln_g, sgu_ln_b, w_s, b_s, conv_w, conv_b, conv_ln_g, conv_ln_b, w_out, ln1_g, ln1_b, w_gate, w_up, w_down, ln2_g, ln2_b, loss_target, m_w_in, m_sgu_ln_g, m_sgu_ln_b, m_w_s, m_b_s, m_conv_w, m_conv_b, m_conv_ln_g, m_conv_ln_b, m_w_out, m_ln1_g, m_ln1_b, m_w_gate, m_w_up, m_w_down, m_ln2_g, m_ln2_b, v_w_in, v_sgu_ln_g, v_sgu_ln_b, v_w_s, v_b_s, v_conv_w, v_conv_b, v_conv_ln_g, v_conv_ln_b, v_w_out, v_ln1_g, v_ln1_b, v_w_gate, v_w_up, v_w_down, v_ln2_g, v_ln2_b):
    depth, d, q = w_in.shape
    assert depth == 1 and x.shape[0] == 1
    t = x.shape[1]
    heads = w_s.shape[1]
    kw, cshard = conv_w.shape[1], conv_w.shape[2]
    fs = w_gate.shape[2]
    fp = -(-fs // MXU_N) * MXU_N
    n_pairs = q // LANES
    assert heads * HEAD_DIM == q and q % LANES == 0 and w_s.shape[2] == CHUNK and 4 * cshard == q and kw - 1 <= HALO
    alpha = (2.0 * depth) ** 0.25
    tm = min(512, t)
    assert t % tm == 0 and tm % CHUNK == 0
    x2, tgt = x[0], loss_target[0]
    mx, my, mc = _coords()
    me = 2 * mx + my
    c_arr = jnp.reshape(mc, (1,)).astype(jnp.int32)

    kwp = -(-kw // 16) * 16
    me_arr = jnp.reshape(me, (1,)).astype(jnp.int32)
    wi, wo, wg, wu, wd, cw4 = _prep(me_arr, w_in[0], w_out[0], w_gate[0].T, w_up[0].T, w_down[0], conv_w[0], fp, kwp)
    wi, wo, cw4 = _gather_shards([wi, wo, cw4])
    wo = wo.reshape(d, d)
    cw = jnp.transpose(cw4, (1, 0, 2)).reshape(kwp, q)
    cwf = _pad_rows(cw[:kw][::-1], kwp)
    tabs = {name: jnp.asarray(tab) for name, tab in _dft_tables(kw, kwp).items()}

    wm = jnp.where(jnp.tril(jnp.ones((CHUNK, CHUNK), bool)), w_s[0], 0.0)
    wst = wm.reshape(n_pairs, 2 * CHUNK, CHUNK).astype(BF16)
    wstt = jnp.transpose(wm, (0, 2, 1)).reshape(n_pairs, 2 * CHUNK, CHUNK).astype(BF16)
    bmat = jnp.repeat(b_s[0].T, HEAD_DIM, axis=1)
    vq = _pad_rows(jnp.concatenate([sgu_ln_g, sgu_ln_b, conv_b, conv_ln_g, conv_ln_b], axis=0), 8)
    vd = _pad_rows(jnp.concatenate([ln1_g, ln1_b, ln2_g, ln2_b], axis=0), 8)

    *saved, wg, wu, wd = _fwd_mix(x2, wi, wo, wst, bmat, cwf, tabs, vq, vd, [wg, wu, wd], alpha, tm)
    gp, up, x1b, dr2, loss_part, dg2, db2 = _fwd_mlp(saved[3], tgt, wg, wu, wd, vd, alpha, tm)
    mlp_grads = None
    for k in range(4):
        mlp_grads = _bwd_mlp_slab(k, dr2, mlp_grads, x1b, gp, up, wg, wu, wd, alpha, fs, tm)
    dx1 = mlp_grads[0]
    mlp_halves = [b.reshape(4, 2, b.shape[1] // 2, b.shape[2]) for b in mlp_grads[1:]]
    mlp_sums = [_pair_sum(f"mlp{a}", h, l, c_arr, BF16) for a, (h, l) in enumerate(zip(mlp_halves, _pair_swap("pair_swap_mlp", mlp_halves)))]
    grad_x, dwi, dwo, dws, dbs, dcw, dvq, dvd, *mlp_parts = _bwd_mix(dx1, saved, wi, wo, wstt, cwf, tabs, vq, vd, mlp_sums, alpha, CONV_BLOCK)

    dws = jnp.where(jnp.tril(jnp.ones((CHUNK, CHUNK), bool)), dws.reshape(heads, CHUNK, CHUNK), 0.0)
    dvd = dvd.at[VD_LN2_G].set(dg2[0]).at[VD_LN2_B].set(db2[0])
    pieces = [_rows128(dws), dbs[:, :heads].T, _rows128(dcw), _rows128(dvq), _rows128(dvd), loss_part]
    sizes = [p.shape[0] for p in pieces]
    rows = -(-sum(sizes) // 16) * 16
    small = _pad_rows(jnp.concatenate(pieces, axis=0), rows)

    big = [dwi, dwo.reshape(4, d // 4, d)]
    halves = [b.reshape(4, 2, b.shape[1] // 2, b.shape[2]) for b in big] + [small.reshape(2, rows // 2, LANES)]
    landed = _pair_swap("pair_swap_mix", halves)
    sums = [_pair_sum(f"mix{a}", h, l, c_arr, BF16) for a, (h, l) in enumerate(zip(halves[:-1], landed[:-1]))]
    sums.append(_pair_sum("small", halves[-1][None], landed[-1][None], c_arr, F32)[0])
    parts = _chip_exchange(sums)
    parts = list(parts)
    parts, sums = parts[:2] + list(mlp_parts) + parts[2:], sums[:2] + list(mlp_sums) + sums[2:]
    mine = [_chip_sum(a, p, s, me_arr) for a, (p, s) in enumerate(zip(parts, sums))]
    other = _pair_gather(mine)

    def pack(ws, bs, vqs, vds):
        ps = [_rows128(ws[0]), bs[0], jnp.zeros((sizes[2], LANES), F32),
              _rows128(_pad_rows(jnp.concatenate(vqs, axis=0), 8)), _rows128(_pad_rows(jnp.concatenate(vds, axis=0), 8)),
              jnp.zeros((sizes[5], LANES), F32)]
        return _pad_rows(jnp.concatenate(ps, axis=0), rows)

    out = {}
    for a, (nm, w_, m_, v_) in enumerate((("w_in", w_in, m_w_in, v_w_in), ("w_out", w_out, m_w_out, v_w_out),
                                          ("w_gate", w_gate, m_w_gate, v_w_gate), ("w_up", w_up, m_w_up, v_w_up),
                                          ("w_down", w_down, m_w_down, v_w_down))):
        if nm in ("w_gate", "w_up"):
            out[nm] = [o.T for o in _adamw(nm, w_[0].T, mine[a], other[a], m_[0].T, v_[0].T, c_arr)]
        else:
            out[nm] = _adamw(nm, w_[0], mine[a], other[a], m_[0], v_[0], c_arr)
    packed = _adamw(
        "small",
        pack(w_s, b_s, [sgu_ln_g, sgu_ln_b, conv_b, conv_ln_g, conv_ln_b], [ln1_g, ln1_b, ln2_g, ln2_b]), mine[-1], other[-1],
        pack(m_w_s, m_b_s, [m_sgu_ln_g, m_sgu_ln_b, m_conv_b, m_conv_ln_g, m_conv_ln_b], [m_ln1_g, m_ln1_b, m_ln2_g, m_ln2_b]),
        pack(v_w_s, v_b_s, [v_sgu_ln_g, v_sgu_ln_b, v_conv_b, v_conv_ln_g, v_conv_ln_b], [v_ln1_g, v_ln1_b, v_ln2_g, v_ln2_b]),
        c_arr)

    offs = [sum(sizes[:i]) for i in range(len(sizes))]
    g_cw_full = packed[0][offs[2]:offs[2] + sizes[2]].reshape(kwp, q)
    g_cw = lax.dynamic_slice(g_cw_full, (0, me * cshard), (kwp, cshard))
    out["conv_w"] = _adamw("conv_w", _pad_rows(conv_w[0], kwp), g_cw, g_cw, _pad_rows(m_conv_w[0], kwp), _pad_rows(v_conv_w[0], kwp), c_arr)
    out["conv_w"] = [o[:kw] for o in out["conv_w"]]

    def unpack(p):
        vq_o = p[offs[3]:offs[3] + sizes[3]].reshape(8, q)
        vd_o = p[offs[4]:offs[4] + sizes[4]].reshape(8, d)
        return {"w_s": p[offs[0]:offs[0] + sizes[0]].reshape(heads, CHUNK, CHUNK), "b_s": p[offs[1]:offs[1] + sizes[1]],
                "sgu_ln_g": vq_o[VQ_SGU_G], "sgu_ln_b": vq_o[VQ_SGU_B], "conv_b": vq_o[VQ_CONV_B],
                "conv_ln_g": vq_o[VQ_CLN_G], "conv_ln_b": vq_o[VQ_CLN_B],
                "ln1_g": vd_o[VD_LN1_G], "ln1_b": vd_o[VD_LN1_B], "ln2_g": vd_o[VD_LN2_G], "ln2_b": vd_o[VD_LN2_B]}

    small_out = [unpack(p) for p in packed]
    loss = packed[0][offs[5], 0]
    names = ["w_in", "sgu_ln_g", "sgu_ln_b", "w_s", "b_s", "conv_w", "conv_b", "conv_ln_g", "conv_ln_b", "w_out",
             "ln1_g", "ln1_b", "w_gate", "w_up", "w_down", "ln2_g", "ln2_b"]
    result = [loss, grad_x[None]]
    for kind in range(4):
        for nm in names:
            val = out[nm][kind] if nm in out else small_out[kind][nm]
            result.append(val[None])
    return tuple(result)
```

```python
import functools
import math

import jax
import numpy as np
import jax.numpy as jnp
from jax import lax
from jax.experimental import pallas as pl
from jax.experimental.pallas import tpu as pltpu

F32 = jnp.float32
BF16 = jnp.bfloat16

LN_EPS = 1e-5
HEAD_DIM = 64
CHUNK = 128
HALO = 32
LANES = 128
MXU_N = 256
ADAM_LR, ADAM_B1, ADAM_B2, ADAM_EPS, ADAM_WD, ADAM_STEP = 0.001, 0.9, 0.999, 1e-08, 0.01, 10
VMEM_LIMIT = 60 * 1024 * 1024
MESH_AXES = ("x", "y", "c")
MESH_ID = pl.DeviceIdType.MESH


def _dot(a, b):
    return jnp.dot(a, b, preferred_element_type=F32)


def _dot_nt(a, b):
    return lax.dot_general(a, b, (((1,), (1,)), ((), ())), preferred_element_type=F32)


def _dot_tn(a, b):
    return lax.dot_general(a, b, (((0,), (0,)), ((), ())), preferred_element_type=F32)


def _sigmoid(v):
    return 1.0 / (1.0 + jnp.exp(-v))


def _gelu(v):
    cdf = 0.5 * (1.0 + lax.erf(v * (1.0 / math.sqrt(2.0))))
    pdf = jnp.exp(-0.5 * v * v) * (1.0 / math.sqrt(2.0 * math.pi))
    return v * cdf, cdf + v * pdf


def _ln_stats(v):
    mu = jnp.mean(v, axis=-1, keepdims=True)
    d = v - mu
    rstd = lax.rsqrt(jnp.mean(d * d, axis=-1, keepdims=True) + LN_EPS)
    return d * rstd, rstd


def _ln_bwd(dxhat, xhat, rstd):
    m1 = jnp.mean(dxhat, axis=-1, keepdims=True)
    m2 = jnp.mean(dxhat * xhat, axis=-1, keepdims=True)
    return rstd * (dxhat - m1 - xhat * m2)


def _colsum(v):
    return jnp.sum(v, axis=0, keepdims=True)


def _pair_lanes(v, nc, p):
    return jnp.concatenate([v[c * CHUNK:(c + 1) * CHUNK, p * LANES:(p + 1) * LANES] for c in range(nc)], axis=1)


def _unpair(parts, nc):
    rows = [jnp.concatenate([part[:, c * LANES:(c + 1) * LANES] for part in parts], axis=1) for c in range(nc)]
    return jnp.concatenate(rows, axis=0)


def _low_head(nc):
    lane = lax.broadcasted_iota(jnp.int32, (CHUNK, nc * LANES), 1)
    return (lane & (LANES - 1)) < HEAD_DIM


def _mix(wst_ref, v, nc, n_pairs):
    vb = v.astype(BF16)
    low = _low_head(nc)
    parts = []
    for p in range(n_pairs):
        r = _dot(wst_ref[p], _pair_lanes(vb, nc, p))
        parts.append(jnp.where(low, r[:CHUNK], r[CHUNK:]))
    return _unpair(parts, nc)


def _mix_wgrad(dm, vn, nc, n_pairs):
    low = _low_head(nc)
    vb = vn.astype(BF16)
    out = []
    for p in range(n_pairs):
        a = _pair_lanes(dm, nc, p)
        lhs = jnp.concatenate([jnp.where(low, a, 0.0), jnp.where(low, 0.0, a)], axis=0).astype(BF16)
        out.append(_dot_nt(lhs, _pair_lanes(vb, nc, p)))
    return out


SUBLANES = 8


CONV_BLOCK = 256
DFT_N = CONV_BLOCK + HALO
DFT_F = -(-(DFT_N // 2 + 1) // SUBLANES) * SUBLANES


def _terms(m, exact):
    hi = m.astype(np.float32).astype(BF16)
    lo = (m.astype(np.float32) - hi.astype(np.float32)).astype(BF16)
    return np.concatenate([hi, hi, lo] if exact else [hi, hi], axis=1)


def _split(v, exact=False):
    hi = v.astype(BF16)
    lo = (v - hi.astype(F32)).astype(BF16)
    return jnp.concatenate([hi, lo, hi] if exact else [hi, lo], axis=0)


def _dft_tables(kw, kwp, q):
    nf = DFT_N // 2 + 1
    ang = 2.0 * np.pi * np.arange(nf)[:, None] * np.arange(DFT_N)[None, :] / DFT_N
    fwd = np.zeros((2 * DFT_F, DFT_N))
    fwd[:nf], fwd[DFT_F:DFT_F + nf] = np.cos(ang), -np.sin(ang)
    weight = np.full((nf, 1), 2.0 / DFT_N)
    weight[0] = weight[-1] = 1.0 / DFT_N
    inv = np.zeros((DFT_N, 2 * DFT_F))
    inv[:, :nf], inv[:, DFT_F:DFT_F + nf] = (np.cos(ang) * weight).T, (-np.sin(ang) * weight).T
    inv_taps = np.zeros((kwp, 2 * DFT_F))
    inv_taps[:kw] = inv[kw - 1::-1][:kw]
    shift = np.zeros((2 * DFT_F, q), np.float32)
    shift[:nf], shift[DFT_F:DFT_F + nf] = np.cos(ang[:, HALO:HALO + 1]), -np.sin(ang[:, HALO:HALO + 1])
    return {"fwd": _terms(fwd, False), "fwd_halo": _terms(fwd[:, CONV_BLOCK:], False), "shift": shift,
            "inv_out": _terms(inv[HALO:HALO + CONV_BLOCK], False), "inv_in": _terms(inv[:CONV_BLOCK], False),
            "taps": _terms(fwd[:, :kwp], True), "inv_taps": _terms(inv_taps, True)}


def _cmul(a, b, conj_b=False):
    ar, ai, br, bi = a[:DFT_F], a[DFT_F:], b[:DFT_F], b[DFT_F:]
    if conj_b:
        return jnp.concatenate([ar * br + ai * bi, ai * br - ar * bi], axis=0)
    return jnp.concatenate([ar * br - ai * bi, ar * bi + ai * br], axis=0)


def _cparams():
    return pltpu.CompilerParams(dimension_semantics=("arbitrary",), vmem_limit_bytes=VMEM_LIMIT)


def _full(shape):
    return pl.BlockSpec(shape, lambda i: (0,) * len(shape))


ANY = pl.BlockSpec(memory_space=pl.ANY)

VQ_SGU_G, VQ_SGU_B, VQ_CONV_B, VQ_CLN_G, VQ_CLN_B = range(5)
VD_LN1_G, VD_LN1_B, VD_LN2_G, VD_LN2_B = range(4)
RS_LN1, RS_SGU, RS_CONV = range(3)
RS_COLS = 8


def _fwd_mix(x, wi, wo, wst, bmat, cwf, tabs, vq, vd, mlp_w, alpha, tm):
    t, d = x.shape
    q = wi.shape[2]
    nc, n_pairs = tm // CHUNK, q // LANES
    n = t // tm
    n_in, n_saved = 11, 12
    assert tm % CONV_BLOCK == 0

    def body(x_ref, wi_hbm, wo_hbm, wst_ref, bmat_ref, cwf_ref, fwd_ref, taps_ref, inv_ref, vq_ref, vd_ref, *rest):
        (xb_ref, pag_ref, y_ref, xh_ref, rs_ref, zu_ref, mg_ref, vhat_ref, gv_ref, vnb_ref, yhat_ref, hf_ref) = rest[3:3 + n_saved]
        gathered = rest[3 + n_saved:6 + n_saved]
        wi_v, wo_v, hb_ref, gf_ref, send_sems, recv_sems = rest[6 + n_saved:]
        step = pl.program_id(0)

        @pl.when(step == 0)
        def _():
            _Gather(gathered, send_sems, recv_sems).start()
            pltpu.sync_copy(wi_hbm, wi_v)
            pltpu.sync_copy(wo_hbm, wo_v)
            hb_ref[...] = jnp.zeros_like(hb_ref)
            gf_ref[...] = _dot(taps_ref[...], _split(cwf_ref[...], True))

        @pl.when(step == (3 * n) // 4)
        def _():
            _Gather(gathered, send_sems, recv_sems).forward()

        xv = x_ref[...]
        xb = xv.astype(BF16)
        xb_ref[...] = xb
        pu, pv, pa, pg = (_dot(xb, wi_v[j]) for j in range(4))
        pag_ref[:, 0:q] = pa.astype(BF16)
        pag_ref[:, q:2 * q] = pg.astype(BF16)
        zu, gu = _gelu(pu)
        zv, gv = _gelu(pv)
        vhat, rstd_v = _ln_stats(zv)
        vnb = (vhat * vq_ref[VQ_SGU_G:VQ_SGU_G + 1, :] + vq_ref[VQ_SGU_B:VQ_SGU_B + 1, :]).astype(BF16)
        mixed = _mix(wst_ref, vnb, nc, n_pairs) + jnp.concatenate([bmat_ref[...]] * nc, axis=0)
        y_ref[:, 0:q] = (zu * mixed).astype(BF16)
        zu_ref[...] = zu
        mg_ref[...] = mixed * gu
        vhat_ref[...] = vhat
        gv_ref[...] = gv
        vnb_ref[...] = vnb

        hb_ref[HALO:HALO + tm, :] = pa * _sigmoid(pg)
        blocks = []
        for r0 in range(0, tm, CONV_BLOCK):
            spectrum = _dot(fwd_ref[...], _split(hb_ref[r0:r0 + DFT_N, :]))
            hf_ref[r0 // CONV_BLOCK * 2 * DFT_F:(r0 // CONV_BLOCK + 1) * 2 * DFT_F, :] = spectrum
            blocks.append(_dot(inv_ref[...], _split(_cmul(gf_ref[...], spectrum))))
        yc = jnp.concatenate(blocks, axis=0) + vq_ref[VQ_CONV_B:VQ_CONV_B + 1, :]
        hb_ref[0:HALO, :] = hb_ref[tm:tm + HALO, :]
        yhat, rstd_c = _ln_stats(yc)
        yhat_ref[...] = yhat
        yn = yhat * vq_ref[VQ_CLN_G:VQ_CLN_G + 1, :] + vq_ref[VQ_CLN_B:VQ_CLN_B + 1, :]
        y_ref[:, q:2 * q] = (yn * _sigmoid(yn)).astype(BF16)

        r1 = alpha * xv + _dot(y_ref[...], wo_v[...])
        xhat, rstd1 = _ln_stats(r1)
        xh_ref[...] = xhat
        col = lax.broadcasted_iota(jnp.int32, (tm, RS_COLS), 1)
        rs_ref[...] = jnp.where(col == RS_LN1, rstd1, jnp.where(col == RS_SGU, rstd_v, jnp.where(col == RS_CONV, rstd_c, 0.0)))

        @pl.when(step == n - 1)
        def _():
            _Gather(gathered, send_sems, recv_sems).finish()

    row = lambda w: pl.BlockSpec((tm, w), lambda i: (i, 0))
    widths = [(d, BF16), (2 * q, BF16), (d, BF16), (d, F32), (RS_COLS, F32), (q, F32), (q, F32), (q, F32), (q, F32), (q, BF16), (q, F32)]
    assert len(widths) + 1 == n_saved
    small_ins = [wst, bmat, cwf, tabs["fwd"], tabs["taps"], tabs["inv_out"], vq, vd]
    return pl.pallas_call(
        body, name="fwd_mix", grid=(n,),
        in_specs=[row(d), ANY, ANY] + [_full(a.shape) for a in small_ins] + [ANY] * 3,
        out_specs=[row(w) for w, _ in widths] + [pl.BlockSpec((tm // CONV_BLOCK * 2 * DFT_F, q), lambda i: (i, 0))] + [ANY] * 3,
        out_shape=[jax.ShapeDtypeStruct((t, w), dt) for w, dt in widths] + [jax.ShapeDtypeStruct((t // CONV_BLOCK * 2 * DFT_F, q), F32)]
        + [jax.ShapeDtypeStruct(b.shape, b.dtype) for b in mlp_w],
        scratch_shapes=[pltpu.VMEM(wi.shape, BF16), pltpu.VMEM(wo.shape, BF16), pltpu.VMEM((HALO + tm, q), F32),
                        pltpu.VMEM((2 * DFT_F, q), F32)] + _gather_sems(3),
        input_output_aliases={n_in + a: n_saved + a for a in range(3)},
        compiler_params=_cparams(),
    )(x, wi, wo, *small_ins, *mlp_w)


def _fwd_mlp(xh1, tgt, wg, wu, wd, vd, alpha, tm):
    t, d = xh1.shape
    ns, fp, _ = wg.shape
    n = t // tm

    def body(xh_ref, tgt_ref, wg_hbm, wu_hbm, wd_hbm, vd_ref,
             gp_ref, up_ref, x1b_ref, dr2_ref, loss_ref, dg2_ref, db2_ref, wg_v, wu_v, wd_v):
        @pl.when(pl.program_id(0) == 0)
        def _():
            pltpu.sync_copy(wg_hbm, wg_v)
            pltpu.sync_copy(wu_hbm, wu_v)
            pltpu.sync_copy(wd_hbm, wd_v)
            loss_ref[...] = jnp.zeros_like(loss_ref)
            dg2_ref[...] = jnp.zeros_like(dg2_ref)
            db2_ref[...] = jnp.zeros_like(db2_ref)

        x1 = xh_ref[...] * vd_ref[VD_LN1_G:VD_LN1_G + 1, :] + vd_ref[VD_LN1_B:VD_LN1_B + 1, :]
        x1b = x1.astype(BF16)
        x1b_ref[...] = x1b
        acc = alpha * x1
        for k in range(ns):
            gp = _dot_nt(x1b, wg_v[k])
            up = _dot_nt(x1b, wu_v[k])
            gp_ref[:, k * fp:(k + 1) * fp] = gp.astype(BF16)
            up_ref[:, k * fp:(k + 1) * fp] = up.astype(BF16)
            acc = acc + _dot((gp * _sigmoid(gp) * up).astype(BF16), wd_v[k])
        xh2, rstd2 = _ln_stats(acc)
        g2 = vd_ref[VD_LN2_G:VD_LN2_G + 1, :]
        err = xh2 * g2 + vd_ref[VD_LN2_B:VD_LN2_B + 1, :] - tgt_ref[...]
        loss_ref[...] += _colsum(jnp.sum(err * err, axis=1, keepdims=True)) * (0.5 / d)
        dy = err * (1.0 / d)
        dg2_ref[...] += _colsum(dy * xh2)
        db2_ref[...] += _colsum(dy)
        dr2_ref[...] = _ln_bwd(dy * g2, xh2, rstd2)

    row = lambda w: pl.BlockSpec((tm, w), lambda i: (i, 0))
    return pl.pallas_call(
        body, name="fwd_mlp", grid=(n,),
        in_specs=[row(d), row(d), ANY, ANY, ANY, _full(vd.shape)],
        out_specs=[row(ns * fp), row(ns * fp), row(d), row(d), _full((8, LANES)), _full((1, d)), _full((1, d))],
        out_shape=[jax.ShapeDtypeStruct((t, ns * fp), BF16), jax.ShapeDtypeStruct((t, ns * fp), BF16),
                   jax.ShapeDtypeStruct((t, d), BF16), jax.ShapeDtypeStruct((t, d), F32),
                   jax.ShapeDtypeStruct((8, LANES), F32), jax.ShapeDtypeStruct((1, d), F32), jax.ShapeDtypeStruct((1, d), F32)],
        scratch_shapes=[pltpu.VMEM(wg.shape, BF16), pltpu.VMEM(wu.shape, BF16), pltpu.VMEM(wd.shape, BF16)],
        compiler_params=_cparams(),
    )(xh1, tgt, wg, wu, wd, vd)


def _bwd_mlp_slab(k, dr2, prev, x1b, gp, up, wg, wu, wd, alpha, fs, tm):
    t, d = dr2.shape
    ns, fp, _ = wg.shape
    n = t // tm
    first = prev is None

    def body(*refs):
        if first:
            dr_ref, x1b_ref, gp_ref, up_ref, wg_ref, wu_ref, wd_ref, dx_ref, dwg_hbm, dwu_hbm, dwd_hbm, ag, au, ad = refs
        else:
            (dr_ref, dxp_ref, x1b_ref, gp_ref, up_ref, wg_ref, wu_ref, wd_ref, _, _, _,
             dx_ref, dwg_hbm, dwu_hbm, dwd_hbm, ag, au, ad) = refs

        @pl.when(pl.program_id(0) == 0)
        def _():
            ag[...] = jnp.zeros_like(ag)
            au[...] = jnp.zeros_like(au)
            ad[...] = jnp.zeros_like(ad)

        dr = dr_ref[...]
        drb = dr.astype(BF16)
        x1b = x1b_ref[...]
        gpv = gp_ref[...].astype(F32)
        upv = up_ref[...].astype(F32)
        dh = _dot_nt(drb, wd_ref[0])
        sg = _sigmoid(gpv)
        silu = gpv * sg
        ad[...] += _dot_tn((silu * upv).astype(BF16), drb)
        dgp = (dh * upv * (sg * (1.0 + gpv * (1.0 - sg)))).astype(BF16)
        dup = (dh * silu).astype(BF16)
        ag[...] += _dot_tn(dgp, x1b)
        au[...] += _dot_tn(dup, x1b)
        base = alpha * dr if first else dxp_ref[...]
        dx_ref[...] = base + _dot(dgp, wg_ref[0]) + _dot(dup, wu_ref[0])

        @pl.when(pl.program_id(0) == n - 1)
        def _():
            pltpu.sync_copy(ag.at[0:fs], dwg_hbm.at[k])
            pltpu.sync_copy(au.at[0:fs], dwu_hbm.at[k])
            pltpu.sync_copy(ad.at[0:fs], dwd_hbm.at[k])

    row = lambda w: pl.BlockSpec((tm, w), lambda i: (i, 0))
    slab = pl.BlockSpec((tm, fp), lambda i: (i, k))
    wrow = pl.BlockSpec((1, fp, d), lambda i: (k, 0, 0))
    ins = [dr2] + ([] if first else [prev[0]]) + [x1b, gp, up, wg, wu, wd] + ([] if first else list(prev[1:]))
    in_specs = [row(d)] + ([] if first else [row(d)]) + [row(d), slab, slab, wrow, wrow, wrow] + ([] if first else [ANY] * 3)
    return pl.pallas_call(
        body, name=f"bwd_mlp_{k}", grid=(n,),
        in_specs=in_specs,
        out_specs=[row(d), ANY, ANY, ANY],
        out_shape=[jax.ShapeDtypeStruct((t, d), F32)] + [jax.ShapeDtypeStruct((ns, fs, d), F32)] * 3,
        scratch_shapes=[pltpu.VMEM((fp, d), F32)] * 3,
        input_output_aliases={} if first else {8: 1, 9: 2, 10: 3},
        compiler_params=_cparams(),
    )(*ins)


def _bwd_mix(dx1, saved, wi, wo, wstt, cwf, tabs, vq, vd, mlp_sums, alpha, tm):
    xb, pag, y, xh1, rs, zu_s, mg_s, vhat_s, gv_s, vnb_s, yhat_s, hf_s = saved
    t, d = xh1.shape
    q = wi.shape[2]
    nc, n_pairs = tm // CHUNK, q // LANES
    n = t // tm
    assert tm == CONV_BLOCK

    def body(dx1_ref, xb_ref, pag_ref, y_ref, xh_ref, rs_ref, zu_ref, mg_ref, vhat_ref, gv_ref, vnb_ref, yhat_ref, hf_ref,
             wi_hbm, wo_hbm, wstt_ref, cwf_ref, fwd_ref, fwd_halo_ref, shift_ref, taps_ref, inv_ref, inv_taps_ref, vq_ref, vd_ref,
             sum_g, sum_u, sum_d,
             gx_ref, dwi_hbm, dwo_hbm, dws_ref, dbs_ref, dcw_ref, dvq_ref, dvd_ref, got_g, got_u, got_d,
             wi_v, wo_v, awi, awo, dyb_ref, later_ref, dbm_ref, gf_ref, dgf_ref, send_sems, recv_sems):
        i = pl.program_id(0)
        exchange = lambda: _Exchange((sum_g, sum_u, sum_d), (got_g, got_u, got_d), send_sems, recv_sems)

        @pl.when(i == 0)
        def _():
            exchange().start()
            pltpu.sync_copy(wi_hbm, wi_v)
            pltpu.sync_copy(wo_hbm, wo_v)
            for r in (awi, awo, dws_ref, dbm_ref, dgf_ref, dvq_ref, dvd_ref, dyb_ref, later_ref):
                r[...] = jnp.zeros_like(r)
            gf_ref[...] = _dot(taps_ref[...], _split(cwf_ref[...], True))

        dx1v = dx1_ref[...]
        xh = xh_ref[...]
        rsv = rs_ref[...]
        dvd_ref[VD_LN1_G:VD_LN1_G + 1, :] += _colsum(dx1v * xh)
        dvd_ref[VD_LN1_B:VD_LN1_B + 1, :] += _colsum(dx1v)
        dr1 = _ln_bwd(dx1v * vd_ref[VD_LN1_G:VD_LN1_G + 1, :], xh, rsv[:, RS_LN1:RS_LN1 + 1])
        dr1b = dr1.astype(BF16)
        awo[...] += _dot_tn(y_ref[...], dr1b)
        dy = _dot_nt(dr1b, wo_v[...])

        vhat = vhat_ref[...]
        sgu_g = vq_ref[VQ_SGU_G:VQ_SGU_G + 1, :]
        doa = dy[:, 0:q]
        dm = doa * zu_ref[...]
        dpu = (doa * mg_ref[...]).astype(BF16)
        acc = dm[0:CHUNK]
        for c in range(1, nc):
            acc = acc + dm[c * CHUNK:(c + 1) * CHUNK]
        dbm_ref[...] += acc
        for p, g in enumerate(_mix_wgrad(dm, vnb_ref[...], nc, n_pairs)):
            dws_ref[p] += g
        dvn = _mix(wstt_ref, dm, nc, n_pairs)
        dvq_ref[VQ_SGU_G:VQ_SGU_G + 1, :] += _colsum(dvn * vhat)
        dvq_ref[VQ_SGU_B:VQ_SGU_B + 1, :] += _colsum(dvn)
        dpv = (_ln_bwd(dvn * sgu_g, vhat, rsv[:, RS_SGU:RS_SGU + 1]) * gv_ref[...]).astype(BF16)

        pa = pag_ref[:, 0:q].astype(F32)
        sg = _sigmoid(pag_ref[:, q:2 * q].astype(F32))
        yhat = yhat_ref[...]
        cln_g = vq_ref[VQ_CLN_G:VQ_CLN_G + 1, :]
        yn = yhat * cln_g + vq_ref[VQ_CLN_B:VQ_CLN_B + 1, :]
        sy = _sigmoid(yn)
        dyn = dy[:, q:2 * q] * (sy * (1.0 + yn * (1.0 - sy)))
        dvq_ref[VQ_CLN_G:VQ_CLN_G + 1, :] += _colsum(dyn * yhat)
        dvq_ref[VQ_CLN_B:VQ_CLN_B + 1, :] += _colsum(dyn)
        dyc = _ln_bwd(dyn * cln_g, yhat, rsv[:, RS_CONV:RS_CONV + 1])
        dvq_ref[VQ_CONV_B:VQ_CONV_B + 1, :] += _colsum(dyc)
        dyb_ref[0:tm, :] = dyc
        own = _dot(fwd_ref[...], _split(dyb_ref[...]))
        dgf_ref[...] += _cmul(_cmul(own, shift_ref[...]), hf_ref[...], conj_b=True)
        with_later = own + _dot(fwd_halo_ref[...], _split(later_ref[...]))
        dh = _dot(inv_ref[...], _split(_cmul(with_later, gf_ref[...], conj_b=True)))
        later_ref[...] = dyb_ref[0:HALO, :]
        da = (dh * sg).astype(BF16)
        dg = (dh * pa * (sg * (1.0 - sg))).astype(BF16)

        xb = xb_ref[...]
        gx = alpha * dr1
        for j, dpj in enumerate((dpu, dpv, da, dg)):
            awi[j] += _dot_tn(xb, dpj)
            gx = gx + _dot_nt(dpj, wi_v[j])
        gx_ref[...] = gx

        @pl.when(i == n - 1)
        def _():
            pltpu.sync_copy(awi, dwi_hbm)
            pltpu.sync_copy(awo, dwo_hbm)
            lane = lax.broadcasted_iota(jnp.int32, (CHUNK, LANES), 1)
            low = lane < HEAD_DIM
            dbs = jnp.zeros((CHUNK, LANES), F32)
            for p in range(n_pairs):
                grp = dbm_ref[:, p * LANES:(p + 1) * LANES]
                dbs = jnp.where(lane == 2 * p, jnp.sum(jnp.where(low, grp, 0.0), axis=1, keepdims=True), dbs)
                dbs = jnp.where(lane == 2 * p + 1, jnp.sum(jnp.where(low, 0.0, grp), axis=1, keepdims=True), dbs)
            dbs_ref[...] = dbs
            dcw_ref[...] = _dot(inv_taps_ref[...], _split(dgf_ref[...], True))
            exchange().finish()

    rev = lambda w: pl.BlockSpec((tm, w), lambda i: (n - 1 - i, 0))
    small = [jax.ShapeDtypeStruct((n_pairs, 2 * CHUNK, CHUNK), F32), jax.ShapeDtypeStruct((CHUNK, LANES), F32),
             jax.ShapeDtypeStruct(cwf.shape, F32), jax.ShapeDtypeStruct(vq.shape, F32), jax.ShapeDtypeStruct(vd.shape, F32)]
    small_ins = [wstt, cwf, tabs["fwd"], tabs["fwd_halo"], tabs["shift"], tabs["taps"], tabs["inv_in"], tabs["inv_taps"], vq, vd]
    return pl.pallas_call(
        body, name="bwd_mix", grid=(n,),
        in_specs=[rev(d), rev(d), rev(2 * q), rev(d), rev(d), rev(RS_COLS), rev(q), rev(q), rev(q), rev(q), rev(q), rev(q),
                  pl.BlockSpec((2 * DFT_F, q), lambda i: (n - 1 - i, 0)), ANY, ANY] + [_full(a.shape) for a in small_ins] + [ANY] * 3,
        out_specs=[rev(d), ANY, ANY] + [_full(s.shape) for s in small] + [ANY] * 3,
        out_shape=[jax.ShapeDtypeStruct((t, d), F32), jax.ShapeDtypeStruct(wi.shape, F32), jax.ShapeDtypeStruct(wo.shape, F32)] + small
        + _exchange_shapes(mlp_sums),
        scratch_shapes=[pltpu.VMEM(wi.shape, BF16), pltpu.VMEM(wo.shape, BF16), pltpu.VMEM(wi.shape, F32), pltpu.VMEM(wo.shape, F32),
                        pltpu.VMEM((DFT_N, q), F32), pltpu.VMEM((HALO, q), F32),
                        pltpu.VMEM((CHUNK, q), F32), pltpu.VMEM((2 * DFT_F, q), F32), pltpu.VMEM((2 * DFT_F, q), F32)]
        + _exchange_sems(3),
        compiler_params=_cparams(),
    )(dx1, xb, pag, y, xh1, rs, zu_s, mg_s, vhat_s, gv_s, vnb_s, yhat_s, hf_s, wi, wo, *small_ins, *mlp_sums)


def _prep(me_arr, w_in, w_out, w_gate_t, w_up_t, w_down, conv_w, fp, kwp):
    fs, d = w_down.shape
    kw, cshard = conv_w.shape

    def body(me_ref, wi_ref, wo_ref, wg_ref, wu_ref, wd_ref, cw_ref, oi, oo, og, ou, od, oc):
        oi[...] = wi_ref[...].astype(BF16)
        oo[...] = wo_ref[...].astype(BF16)
        for src, dst in ((wg_ref, og), (wu_ref, ou), (wd_ref, od)):
            dst[0:fs, :] = src[...].astype(BF16)
            if fp > fs:
                dst[fs:fp, :] = jnp.zeros((fp - fs, d), BF16)
        oc[0:kw, :] = cw_ref[...]
        oc[kw:kwp, :] = jnp.zeros((kwp - kw, cshard), F32)

    ins = (w_in, w_out, w_gate_t, w_up_t, w_down, conv_w)
    outs = [jax.ShapeDtypeStruct((4,) + w_in.shape, BF16), jax.ShapeDtypeStruct((4,) + w_out.shape, BF16),
            jax.ShapeDtypeStruct((4, fp, d), BF16), jax.ShapeDtypeStruct((4, fp, d), BF16), jax.ShapeDtypeStruct((4, fp, d), BF16),
            jax.ShapeDtypeStruct((4, kwp, cshard), F32)]
    grid_spec = pltpu.PrefetchScalarGridSpec(
        num_scalar_prefetch=1, grid=(1,),
        in_specs=[pl.BlockSpec(a.shape, lambda i, me: (0, 0)) for a in ins],
        out_specs=[pl.BlockSpec((None,) + o.shape[1:], lambda i, me: (me[0], 0, 0)) for o in outs])
    return pl.pallas_call(body, name="wprep", grid_spec=grid_spec, out_shape=outs, compiler_params=_cparams())(me_arr, *ins)


def _coords():
    return tuple(lax.axis_index(a) for a in MESH_AXES)


def _other_chips(x, y):
    return [(1 - x, y), (x, 1 - y), (1 - x, 1 - y)]


def _remote(src, dst, send_sem, recv_sem, to):
    return pltpu.make_async_remote_copy(src_ref=src, dst_ref=dst, send_sem=send_sem, recv_sem=recv_sem,
                                        device_id=to, device_id_type=MESH_ID)


def _hbm_call(body, name, ins, out_shape, scratch_shapes, aliases=None):
    return pl.pallas_call(
        body, name=name, in_specs=[ANY] * len(ins), out_specs=[ANY] * len(out_shape), out_shape=out_shape,
        scratch_shapes=scratch_shapes, input_output_aliases=aliases or {},
    )(*ins)


class _Gather:
    def __init__(self, bufs, send_sems, recv_sems):
        self.bufs, self.send_sems, self.recv_sems = bufs, send_sems, recv_sems
        self.x, self.y, self.c = _coords()

    def _copies(self, stage):
        x, y, c = self.x, self.y, self.c
        for a, buf in enumerate(self.bufs):
            hr = buf.shape[1] // 2
            for j, chip in enumerate(_other_chips(x, y)):
                if stage == "ici_out":
                    ref, k, to = buf.at[2 * x + y, pl.ds(c * hr, hr)], j, (*chip, c)
                elif stage == "ici_in":
                    ref, k, to = buf.at[2 * chip[0] + chip[1], pl.ds(c * hr, hr)], j, (*chip, c)
                elif stage == "d2d_out":
                    ref, k, to = buf.at[2 * chip[0] + chip[1], pl.ds(c * hr, hr)], 3 + j, (x, y, 1 - c)
                else:
                    ref, k, to = buf.at[2 * chip[0] + chip[1], pl.ds((1 - c) * hr, hr)], 3 + j, (x, y, 1 - c)
                yield _remote(ref, ref, self.send_sems.at[a, k], self.recv_sems.at[a, k], to)

    def start(self):
        for cp in self._copies("ici_out"):
            cp.start()

    def forward(self):
        for landed, onward in zip(self._copies("ici_in"), self._copies("d2d_out")):
            landed.wait_recv()
            onward.start()

    def finish(self):
        for cp in self._copies("d2d_in"):
            cp.wait_recv()
        for stage in ("ici_out", "d2d_out"):
            for cp in self._copies(stage):
                cp.wait_send()


def _gather_sems(n):
    return [pltpu.SemaphoreType.DMA((n, 6)), pltpu.SemaphoreType.DMA((n, 6))]


def _gather_shards(bufs):
    n = len(bufs)

    def body(*refs):
        g = _Gather(refs[n:2 * n], *refs[2 * n:])
        g.start()
        g.forward()
        g.finish()

    return _hbm_call(body, "gather_shards", bufs, [jax.ShapeDtypeStruct(s.shape, s.dtype) for s in bufs],
                     _gather_sems(n), aliases={a: a for a in range(n)})


def _pair_swap(name, arrs):
    n = len(arrs)

    def body(*refs):
        src, land = refs[:n], refs[n:2 * n]
        send_sems, recv_sems = refs[2 * n:]
        x, y, c = _coords()
        copies = []
        for a in range(n):
            s = src[a].at[pl.ds(0, arrs[a].shape[0]), 1 - c] if arrs[a].ndim == 4 else src[a].at[1 - c]
            copies.append(_remote(s, land[a], send_sems.at[a], recv_sems.at[a], (x, y, 1 - c)))
            copies[-1].start()
        for cp in copies:
            cp.wait()

    outs = [jax.ShapeDtypeStruct(s.shape[:-3] + s.shape[-2:], s.dtype) for s in arrs]
    return _hbm_call(body, name, arrs, outs, [pltpu.SemaphoreType.DMA((n,)), pltpu.SemaphoreType.DMA((n,))])


class _Exchange:
    def __init__(self, src, dst, send_sems, recv_sems):
        self.src, self.dst, self.send_sems, self.recv_sems = src, dst, send_sems, recv_sems
        self.x, self.y, self.c = _coords()

    def _copies(self, incoming):
        x, y, c = self.x, self.y, self.c
        for a, (s, d) in enumerate(zip(self.src, self.dst)):
            for j, chip in enumerate(_other_chips(x, y)):
                slot = 2 * chip[0] + chip[1]
                if incoming:
                    out, into = d.at[slot], d.at[slot]
                else:
                    out, into = (s.at[slot] if len(s.shape) == 3 else s), d.at[2 * x + y]
                yield _remote(out, into, self.send_sems.at[a, j], self.recv_sems.at[a, j], (*chip, c))

    def start(self):
        for cp in self._copies(False):
            cp.start()

    def finish(self):
        for cp in self._copies(True):
            cp.wait_recv()
        for cp in self._copies(False):
            cp.wait_send()


def _exchange_sems(n):
    return [pltpu.SemaphoreType.DMA((n, 3)), pltpu.SemaphoreType.DMA((n, 3))]


def _exchange_shapes(arrs):
    return [jax.ShapeDtypeStruct((4,) + s.shape[-2:], s.dtype) for s in arrs]


def _chip_exchange(arrs):
    n = len(arrs)

    def body(*refs):
        ex = _Exchange(refs[:n], refs[n:2 * n], *refs[2 * n:])
        ex.start()
        ex.finish()

    return _hbm_call(body, "chip_exchange", arrs, _exchange_shapes(arrs), _exchange_sems(n))


def _pair_gather(halves):
    n = len(halves)

    def body(*refs):
        src, dst = refs[:n], refs[n:2 * n]
        send_sems, recv_sems = refs[2 * n:]
        x, y, c = _coords()
        copies = [_remote(src[a], dst[a], send_sems.at[a], recv_sems.at[a], (x, y, 1 - c)) for a in range(n)]
        for cp in copies:
            cp.start()
        for cp in copies:
            cp.wait()

    outs = [jax.ShapeDtypeStruct(s.shape, s.dtype) for s in halves]
    return _hbm_call(body, "pair_gather", halves, outs, [pltpu.SemaphoreType.DMA((n,)), pltpu.SemaphoreType.DMA((n,))])


def _pair_sum(a, g, land, c_arr, out_dtype):
    nq, _, hr, cc = g.shape

    def body(c_ref, g_ref, l_ref, o_ref):
        o_ref[...] = (g_ref[...] + l_ref[...]).astype(out_dtype)

    spec = pl.BlockSpec((None, hr, cc), lambda qi, cr: (qi, 0, 0))
    grid_spec = pltpu.PrefetchScalarGridSpec(
        num_scalar_prefetch=1, grid=(nq,),
        in_specs=[pl.BlockSpec((None, None, hr, cc), lambda qi, cr: (qi, cr[0], 0, 0)), spec], out_specs=spec)
    return pl.pallas_call(body, name=f"pair_sum_{a}", grid_spec=grid_spec, out_shape=jax.ShapeDtypeStruct((nq, hr, cc), out_dtype),
                          compiler_params=_cparams())(c_arr, g, land)


def _chip_sum(a, parts, own, me_arr):
    _, hr, cc = parts.shape

    def body(me_ref, p_ref, own_ref, o_ref):
        for mine in range(4):
            @pl.when(me_ref[0] == mine)
            def _():
                term = lambda j: (own_ref if j == mine else p_ref.at[j])[...].astype(F32)
                o_ref[...] = ((term(0) + term(1)) + term(2)) + term(3)

    own_spec = (pl.BlockSpec((None, hr, cc), lambda i, me: (me[0], 0, 0)) if own.ndim == 3
                else pl.BlockSpec((hr, cc), lambda i, me: (0, 0)))
    grid_spec = pltpu.PrefetchScalarGridSpec(
        num_scalar_prefetch=1, grid=(1,),
        in_specs=[pl.BlockSpec((4, hr, cc), lambda i, me: (0, 0, 0)), own_spec],
        out_specs=pl.BlockSpec((hr, cc), lambda i, me: (0, 0)))
    return pl.pallas_call(body, name=f"chip_sum_{a}", grid_spec=grid_spec, out_shape=jax.ShapeDtypeStruct((hr, cc), F32),
                          compiler_params=_cparams())(me_arr, parts, own)


def _row_block(rows, cols, limit=1 << 20):
    best = 8
    for tr in range(8, rows + 1, 8):
        if rows % tr == 0 and tr * cols * 4 <= limit:
            best = tr
    return best


def _adamw(name, w, g_mine, g_other, m, v, c_arr):
    r, c = w.shape
    hr, cg = g_mine.shape
    tr = hr if r % hr == 0 and hr * cg * 4 <= (3 << 19) else math.gcd(_row_block(hr, cg), r)
    per_half = hr // tr
    bc1 = 1.0 - ADAM_B1 ** ADAM_STEP
    bc2 = 1.0 - ADAM_B2 ** ADAM_STEP

    def body(c_ref, w_ref, gm_ref, go_ref, m_ref, v_ref, go, do, mo, vo):
        gv = jnp.where(pl.program_id(0) // per_half == c_ref[0], gm_ref[:, 0:c], go_ref[:, 0:c])
        mn = ADAM_B1 * m_ref[...] + (1.0 - ADAM_B1) * gv
        vn = ADAM_B2 * v_ref[...] + (1.0 - ADAM_B2) * (gv * gv)
        go[...] = gv
        mo[...] = mn
        vo[...] = vn
        do[...] = -ADAM_LR * ((mn / bc1) / (jnp.sqrt(vn / bc2) + ADAM_EPS) + ADAM_WD * w_ref[...])

    blk = pl.BlockSpec((tr, c), lambda i, cr: (i, 0))
    gblk = pl.BlockSpec((tr, cg), lambda i, cr: (i % per_half, 0))
    grid_spec = pltpu.PrefetchScalarGridSpec(num_scalar_prefetch=1, grid=(r // tr,), in_specs=[blk, gblk, gblk, blk, blk],
                                             out_specs=[blk] * 4)
    return pl.pallas_call(body, name=f"adamw_{name}", grid_spec=grid_spec, out_shape=[jax.ShapeDtypeStruct((r, c), F32)] * 4,
                          compiler_params=_cparams())(c_arr, w, g_mine, g_other, m, v)


def _rows128(a):
    return a.reshape(-1, LANES)


def _pad_rows(a, rows):
    return jnp.pad(a, ((0, rows - a.shape[0]), (0, 0)))


def kernel(x, w_in, sgu_ln_g, sgu_ln_b, w_s, b_s, conv_w, conv_b, conv_ln_g, conv_ln_b, w_out, ln1_g, ln1_b, w_gate, w_up, w_down, ln2_g, ln2_b, loss_target, m_w_in, m_sgu_ln_g, m_sgu_ln_b, m_w_s, m_b_s, m_conv_w, m_conv_b, m_conv_ln_g, m_conv_ln_b, m_w_out, m_ln1_g, m_ln1_b, m_w_gate, m_w_up, m_w_down, m_ln2_g, m_ln2_b, v_w_in, v_sgu_ln_g, v_sgu_ln_b, v_w_s, v_b_s, v_conv_w, v_conv_b, v_conv_ln_g, v_conv_ln_b, v_w_out, v_ln1_g, v_ln1_b, v_w_gate, v_w_up, v_w_down, v_ln2_g, v_ln2_b):
    depth, d, q = w_in.shape
    assert depth == 1 and x.shape[0] == 1
    t = x.shape[1]
    heads = w_s.shape[1]
    kw, cshard = conv_w.shape[1], conv_w.shape[2]
    fs = w_gate.shape[2]
    fp = -(-fs // MXU_N) * MXU_N
    n_pairs = q // LANES
    assert heads * HEAD_DIM == q and q % LANES == 0 and w_s.shape[2] == CHUNK and 4 * cshard == q and kw - 1 <= HALO
    alpha = (2.0 * depth) ** 0.25
    tm = min(512, t)
    assert t % tm == 0 and tm % CHUNK == 0
    x2, tgt = x[0], loss_target[0]
    mx, my, mc = _coords()
    me = 2 * mx + my
    c_arr = jnp.reshape(mc, (1,)).astype(jnp.int32)

    kwp = -(-kw // 16) * 16
    me_arr = jnp.reshape(me, (1,)).astype(jnp.int32)
    wi, wo, wg, wu, wd, cw4 = _prep(me_arr, w_in[0], w_out[0], w_gate[0].T, w_up[0].T, w_down[0], conv_w[0], fp, kwp)
    wi, wo, cw4 = _gather_shards([wi, wo, cw4])
    wo = wo.reshape(d, d)
    cw = jnp.transpose(cw4, (1, 0, 2)).reshape(kwp, q)
    cwf = _pad_rows(cw[:kw][::-1], kwp)
    tabs = {name: jnp.asarray(tab) for name, tab in _dft_tables(kw, kwp, q).items()}

    wm = jnp.where(jnp.tril(jnp.ones((CHUNK, CHUNK), bool)), w_s[0], 0.0)
    wst = wm.reshape(n_pairs, 2 * CHUNK, CHUNK).astype(BF16)
    wstt = jnp.transpose(wm, (0, 2, 1)).reshape(n_pairs, 2 * CHUNK, CHUNK).astype(BF16)
    bmat = jnp.repeat(b_s[0].T, HEAD_DIM, axis=1)
    vq = _pad_rows(jnp.concatenate([sgu_ln_g, sgu_ln_b, conv_b, conv_ln_g, conv_ln_b], axis=0), 8)
    vd = _pad_rows(jnp.concatenate([ln1_g, ln1_b, ln2_g, ln2_b], axis=0), 8)

    *saved, wg, wu, wd = _fwd_mix(x2, wi, wo, wst, bmat, cwf, tabs, vq, vd, [wg, wu, wd], alpha, tm)
    gp, up, x1b, dr2, loss_part, dg2, db2 = _fwd_mlp(saved[3], tgt, wg, wu, wd, vd, alpha, tm)
    mlp_grads = None
    for k in range(4):
        mlp_grads = _bwd_mlp_slab(k, dr2, mlp_grads, x1b, gp, up, wg, wu, wd, alpha, fs, tm)
    dx1 = mlp_grads[0]
    mlp_halves = [b.reshape(4, 2, b.shape[1] // 2, b.shape[2]) for b in mlp_grads[1:]]
    mlp_sums = [_pair_sum(f"mlp{a}", h, l, c_arr, BF16) for a, (h, l) in enumerate(zip(mlp_halves, _pair_swap("pair_swap_mlp", mlp_halves)))]
    grad_x, dwi, dwo, dws, dbs, dcw, dvq, dvd, *mlp_parts = _bwd_mix(dx1, saved, wi, wo, wstt, cwf, tabs, vq, vd, mlp_sums, alpha, CONV_BLOCK)

    dws = jnp.where(jnp.tril(jnp.ones((CHUNK, CHUNK), bool)), dws.reshape(heads, CHUNK, CHUNK), 0.0)
    dvd = dvd.at[VD_LN2_G].set(dg2[0]).at[VD_LN2_B].set(db2[0])
    pieces = [_rows128(dws), dbs[:, :heads].T, _rows128(dcw), _rows128(dvq), _rows128(dvd), loss_part]
    sizes = [p.shape[0] for p in pieces]
    rows = -(-sum(sizes) // 16) * 16
    small = _pad_rows(jnp.concatenate(pieces, axis=0), rows)

    big = [dwi, dwo.reshape(4, d // 4, d)]
    halves = [b.reshape(4, 2, b.shape[1] // 2, b.shape[2]) for b in big] + [small.reshape(2, rows // 2, LANES)]
    landed = _pair_swap("pair_swap_mix", halves)
    sums = [_pair_sum(f"mix{a}", h, l, c_arr, BF16) for a, (h, l) in enumerate(zip(halves[:-1], landed[:-1]))]
    sums.append(_pair_sum("small", halves[-1][None], landed[-1][None], c_arr, F32)[0])
    parts = _chip_exchange(sums)
    parts = list(parts)
    parts, sums = parts[:2] + list(mlp_parts) + parts[2:], sums[:2] + list(mlp_sums) + sums[2:]
    mine = [_chip_sum(a, p, s, me_arr) for a, (p, s) in enumerate(zip(parts, sums))]
    other = _pair_gather(mine)

    def pack(ws, bs, vqs, vds):
        ps = [_rows128(ws[0]), bs[0], jnp.zeros((sizes[2], LANES), F32),
              _rows128(_pad_rows(jnp.concatenate(vqs, axis=0), 8)), _rows128(_pad_rows(jnp.concatenate(vds, axis=0), 8)),
              jnp.zeros((sizes[5], LANES), F32)]
        return _pad_rows(jnp.concatenate(ps, axis=0), rows)

    out = {}
    for a, (nm, w_, m_, v_) in enumerate((("w_in", w_in, m_w_in, v_w_in), ("w_out", w_out, m_w_out, v_w_out),
                                          ("w_gate", w_gate, m_w_gate, v_w_gate), ("w_up", w_up, m_w_up, v_w_up),
                                          ("w_down", w_down, m_w_down, v_w_down))):
        if nm in ("w_gate", "w_up"):
            out[nm] = [o.T for o in _adamw(nm, w_[0].T, mine[a], other[a], m_[0].T, v_[0].T, c_arr)]
        else:
            out[nm] = _adamw(nm, w_[0], mine[a], other[a], m_[0], v_[0], c_arr)
    packed = _adamw(
        "small",
        pack(w_s, b_s, [sgu_ln_g, sgu_ln_b, conv_b, conv_ln_g, conv_ln_b], [ln1_g, ln1_b, ln2_g, ln2_b]), mine[-1], other[-1],
        pack(m_w_s, m_b_s, [m_sgu_ln_g, m_sgu_ln_b, m_conv_b, m_conv_ln_g, m_conv_ln_b], [m_ln1_g, m_ln1_b, m_ln2_g, m_ln2_b]),
        pack(v_w_s, v_b_s, [v_sgu_ln_g, v_sgu_ln_b, v_conv_b, v_conv_ln_g, v_conv_ln_b], [v_ln1_g, v_ln1_b, v_ln2_g, v_ln2_b]),
        c_arr)

    offs = [sum(sizes[:i]) for i in range(len(sizes))]
    g_cw_full = packed[0][offs[2]:offs[2] + sizes[2]].reshape(kwp, q)
    g_cw = lax.dynamic_slice(g_cw_full, (0, me * cshard), (kwp, cshard))
    out["conv_w"] = _adamw("conv_w", _pad_rows(conv_w[0], kwp), g_cw, g_cw, _pad_rows(m_conv_w[0], kwp), _pad_rows(v_conv_w[0], kwp), c_arr)
    out["conv_w"] = [o[:kw] for o in out["conv_w"]]

    def unpack(p):
        vq_o = p[offs[3]:offs[3] + sizes[3]].reshape(8, q)
        vd_o = p[offs[4]:offs[4] + sizes[4]].reshape(8, d)
        return {"w_s": p[offs[0]:offs[0] + sizes[0]].reshape(heads, CHUNK, CHUNK), "b_s": p[offs[1]:offs[1] + sizes[1]],
                "sgu_ln_g": vq_o[VQ_SGU_G], "sgu_ln_b": vq_o[VQ_SGU_B], "conv_b": vq_o[VQ_CONV_B],
                "conv_ln_g": vq_o[VQ_CLN_G], "conv_ln_b": vq_o[VQ_CLN_B],
                "ln1_g": vd_o[VD_LN1_G], "ln1_b": vd_o[VD_LN1_B], "ln2_g": vd_o[VD_LN2_G], "ln2_b": vd_o[VD_LN2_B]}

    small_out = [unpack(p) for p in packed]
    loss = packed[0][offs[5], 0]
    names = ["w_in", "sgu_ln_g", "sgu_ln_b", "w_s", "b_s", "conv_w", "conv_b", "conv_ln_g", "conv_ln_b", "w_out",
             "ln1_g", "ln1_b", "w_gate", "w_up", "w_down", "ln2_g", "ln2_b"]
    result = [loss, grad_x[None]]
    for kind in range(4):
        for nm in names:
            val = out[nm][kind] if nm in out else small_out[kind][nm]
            result.append(val[None])
    return tuple(result)
```

```python
import functools
import math

import jax
import numpy as np
import jax.numpy as jnp
from jax import lax
from jax.experimental import pallas as pl
from jax.experimental.pallas import tpu as pltpu

F32 = jnp.float32
BF16 = jnp.bfloat16

LN_EPS = 1e-5
HEAD_DIM = 64
CHUNK = 128
HALO = 32
LANES = 128
MXU_N = 256
ADAM_LR, ADAM_B1, ADAM_B2, ADAM_EPS, ADAM_WD, ADAM_STEP = 0.001, 0.9, 0.999, 1e-08, 0.01, 10
VMEM_LIMIT = 60 * 1024 * 1024
MESH_AXES = ("x", "y", "c")
MESH_ID = pl.DeviceIdType.MESH


def _dot(a, b):
    return jnp.dot(a, b, preferred_element_type=F32)


def _dot_nt(a, b):
    return lax.dot_general(a, b, (((1,), (1,)), ((), ())), preferred_element_type=F32)


def _dot_tn(a, b):
    return lax.dot_general(a, b, (((0,), (0,)), ((), ())), preferred_element_type=F32)


def _sigmoid(v):
    return 1.0 / (1.0 + jnp.exp(-v))


def _gelu(v):
    cdf = 0.5 * (1.0 + lax.erf(v * (1.0 / math.sqrt(2.0))))
    pdf = jnp.exp(-0.5 * v * v) * (1.0 / math.sqrt(2.0 * math.pi))
    return v * cdf, cdf + v * pdf


def _ln_stats(v):
    mu = jnp.mean(v, axis=-1, keepdims=True)
    d = v - mu
    rstd = lax.rsqrt(jnp.mean(d * d, axis=-1, keepdims=True) + LN_EPS)
    return d * rstd, rstd


def _ln_bwd(dxhat, xhat, rstd):
    m1 = jnp.mean(dxhat, axis=-1, keepdims=True)
    m2 = jnp.mean(dxhat * xhat, axis=-1, keepdims=True)
    return rstd * (dxhat - m1 - xhat * m2)


def _colsum(v):
    return jnp.sum(v, axis=0, keepdims=True)


def _pair_lanes(v, nc, p):
    return jnp.concatenate([v[c * CHUNK:(c + 1) * CHUNK, p * LANES:(p + 1) * LANES] for c in range(nc)], axis=1)


def _unpair(parts, nc):
    rows = [jnp.concatenate([part[:, c * LANES:(c + 1) * LANES] for part in parts], axis=1) for c in range(nc)]
    return jnp.concatenate(rows, axis=0)


def _low_head(nc):
    lane = lax.broadcasted_iota(jnp.int32, (CHUNK, nc * LANES), 1)
    return (lane & (LANES - 1)) < HEAD_DIM


def _mix(wst_ref, v, nc, n_pairs):
    vb = v.astype(BF16)
    low = _low_head(nc)
    parts = []
    for p in range(n_pairs):
        r = _dot(wst_ref[p], _pair_lanes(vb, nc, p))
        parts.append(jnp.where(low, r[:CHUNK], r[CHUNK:]))
    return _unpair(parts, nc)


def _mix_wgrad(dm, vn, nc, n_pairs):
    low = _low_head(nc)
    vb = vn.astype(BF16)
    out = []
    for p in range(n_pairs):
        a = _pair_lanes(dm, nc, p)
        lhs = jnp.concatenate([jnp.where(low, a, 0.0), jnp.where(low, 0.0, a)], axis=0).astype(BF16)
        out.append(_dot_nt(lhs, _pair_lanes(vb, nc, p)))
    return out


SUBLANES = 8


CONV_BLOCK = 256
DFT_N = CONV_BLOCK + HALO
DFT_F = -(-(DFT_N // 2 + 1) // SUBLANES) * SUBLANES


def _terms(m, exact):
    hi = m.astype(np.float32).astype(BF16)
    lo = (m.astype(np.float32) - hi.astype(np.float32)).astype(BF16)
    return np.concatenate([hi, hi, lo] if exact else [hi, hi], axis=1)


def _split(v, exact=False):
    hi = v.astype(BF16)
    lo = (v - hi.astype(F32)).astype(BF16)
    return jnp.concatenate([hi, lo, hi] if exact else [hi, lo], axis=0)


def _dft_tables(kw, kwp, q):
    nf = DFT_N // 2 + 1
    ang = 2.0 * np.pi * np.arange(nf)[:, None] * np.arange(DFT_N)[None, :] / DFT_N
    fwd = np.zeros((2 * DFT_F, DFT_N))
    fwd[:nf], fwd[DFT_F:DFT_F + nf] = np.cos(ang), -np.sin(ang)
    weight = np.full((nf, 1), 2.0 / DFT_N)
    weight[0] = weight[-1] = 1.0 / DFT_N
    inv = np.zeros((DFT_N, 2 * DFT_F))
    inv[:, :nf], inv[:, DFT_F:DFT_F + nf] = (np.cos(ang) * weight).T, (-np.sin(ang) * weight).T
    inv_taps = np.zeros((kwp, 2 * DFT_F))
    inv_taps[:kw] = inv[kw - 1::-1][:kw]
    shift = np.zeros((2 * DFT_F, q), np.float32)
    shift[:nf], shift[DFT_F:DFT_F + nf] = np.cos(ang[:, HALO:HALO + 1]), -np.sin(ang[:, HALO:HALO + 1])
    return {"fwd": _terms(fwd, False), "fwd_halo": _terms(fwd[:, CONV_BLOCK:], False), "shift": shift,
            "inv_out": _terms(inv[HALO:HALO + CONV_BLOCK], False), "inv_in": _terms(inv[:CONV_BLOCK], False),
            "taps": _terms(fwd[:, :kwp], True), "inv_taps": _terms(inv_taps, True)}


def _cmul(a, b, conj_b=False):
    ar, ai, br, bi = a[:DFT_F], a[DFT_F:], b[:DFT_F], b[DFT_F:]
    if conj_b:
        return jnp.concatenate([ar * br + ai * bi, ai * br - ar * bi], axis=0)
    return jnp.concatenate([ar * br - ai * bi, ar * bi + ai * br], axis=0)


def _cparams():
    return pltpu.CompilerParams(dimension_semantics=("arbitrary",), vmem_limit_bytes=VMEM_LIMIT)


def _full(shape):
    return pl.BlockSpec(shape, lambda i: (0,) * len(shape))


ANY = pl.BlockSpec(memory_space=pl.ANY)

VQ_SGU_G, VQ_SGU_B, VQ_CONV_B, VQ_CLN_G, VQ_CLN_B = range(5)
VD_LN1_G, VD_LN1_B, VD_LN2_G, VD_LN2_B = range(4)
RS_LN1, RS_SGU, RS_CONV = range(3)
RS_COLS = 8


def _fwd_mix(x, wi, wo, wst, bmat, cwf, tabs, vq, vd, mlp_w, alpha, tm):
    t, d = x.shape
    q = wi.shape[2]
    nc, n_pairs = tm // CHUNK, q // LANES
    n = t // tm
    n_in, n_saved = 11, 12
    assert tm % CONV_BLOCK == 0

    def body(x_ref, wi_hbm, wo_hbm, wst_ref, bmat_ref, cwf_ref, fwd_ref, taps_ref, inv_ref, vq_ref, vd_ref, *rest):
        (xb_ref, pag_ref, y_ref, xh_ref, rs_ref, zu_ref, mg_ref, vhat_ref, gv_ref, vnb_ref, yhat_ref, hf_ref) = rest[3:3 + n_saved]
        gathered = rest[3 + n_saved:6 + n_saved]
        wi_v, wo_v, hb_ref, gf_ref, send_sems, recv_sems = rest[6 + n_saved:]
        step = pl.program_id(0)

        @pl.when(step == 0)
        def _():
            _Gather(gathered, send_sems, recv_sems).start()
            pltpu.sync_copy(wi_hbm, wi_v)
            pltpu.sync_copy(wo_hbm, wo_v)
            hb_ref[...] = jnp.zeros_like(hb_ref)
            gf_ref[...] = _dot(taps_ref[...], _split(cwf_ref[...], True))

        @pl.when(step == (3 * n) // 4)
        def _():
            _Gather(gathered, send_sems, recv_sems).forward()

        xv = x_ref[...]
        xb = xv.astype(BF16)
        xb_ref[...] = xb
        pu, pv, pa, pg = (_dot(xb, wi_v[j]) for j in range(4))
        pag_ref[:, 0:q] = pa.astype(BF16)
        pag_ref[:, q:2 * q] = pg.astype(BF16)
        zu, gu = _gelu(pu)
        zv, gv = _gelu(pv)
        vhat, rstd_v = _ln_stats(zv)
        vnb = (vhat * vq_ref[VQ_SGU_G:VQ_SGU_G + 1, :] + vq_ref[VQ_SGU_B:VQ_SGU_B + 1, :]).astype(BF16)
        mixed = _mix(wst_ref, vnb, nc, n_pairs) + jnp.concatenate([bmat_ref[...]] * nc, axis=0)
        y_ref[:, 0:q] = (zu * mixed).astype(BF16)
        zu_ref[...] = zu
        mg_ref[...] = mixed * gu
        vhat_ref[...] = vhat
        gv_ref[...] = gv
        vnb_ref[...] = vnb

        hb_ref[HALO:HALO + tm, :] = pa * _sigmoid(pg)
        blocks = []
        for r0 in range(0, tm, CONV_BLOCK):
            spectrum = _dot(fwd_ref[...], _split(hb_ref[r0:r0 + DFT_N, :]))
            hf_ref[r0 // CONV_BLOCK * 2 * DFT_F:(r0 // CONV_BLOCK + 1) * 2 * DFT_F, :] = spectrum
            blocks.append(_dot(inv_ref[...], _split(_cmul(gf_ref[...], spectrum))))
        yc = jnp.concatenate(blocks, axis=0) + vq_ref[VQ_CONV_B:VQ_CONV_B + 1, :]
        hb_ref[0:HALO, :] = hb_ref[tm:tm + HALO, :]
        yhat, rstd_c = _ln_stats(yc)
        yhat_ref[...] = yhat
        yn = yhat * vq_ref[VQ_CLN_G:VQ_CLN_G + 1, :] + vq_ref[VQ_CLN_B:VQ_CLN_B + 1, :]
        y_ref[:, q:2 * q] = (yn * _sigmoid(yn)).astype(BF16)

        r1 = alpha * xv + _dot(y_ref[...], wo_v[...])
        xhat, rstd1 = _ln_stats(r1)
        xh_ref[...] = xhat
        col = lax.broadcasted_iota(jnp.int32, (tm, RS_COLS), 1)
        rs_ref[...] = jnp.where(col == RS_LN1, rstd1, jnp.where(col == RS_SGU, rstd_v, jnp.where(col == RS_CONV, rstd_c, 0.0)))

        @pl.when(step == n - 1)
        def _():
            _Gather(gathered, send_sems, recv_sems).finish()

    row = lambda w: pl.BlockSpec((tm, w), lambda i: (i, 0))
    widths = [(d, BF16), (2 * q, BF16), (d, BF16), (d, F32), (RS_COLS, F32), (q, F32), (q, F32), (q, F32), (q, F32), (q, BF16), (q, F32)]
    assert len(widths) + 1 == n_saved
    small_ins = [wst, bmat, cwf, tabs["fwd"], tabs["taps"], tabs["inv_out"], vq, vd]
    return pl.pallas_call(
        body, name="fwd_mix", grid=(n,),
        in_specs=[row(d), ANY, ANY] + [_full(a.shape) for a in small_ins] + [ANY] * 3,
        out_specs=[row(w) for w, _ in widths] + [pl.BlockSpec((tm // CONV_BLOCK * 2 * DFT_F, q), lambda i: (i, 0))] + [ANY] * 3,
        out_shape=[jax.ShapeDtypeStruct((t, w), dt) for w, dt in widths] + [jax.ShapeDtypeStruct((t // CONV_BLOCK * 2 * DFT_F, q), F32)]
        + [jax.ShapeDtypeStruct(b.shape, b.dtype) for b in mlp_w],
        scratch_shapes=[pltpu.VMEM(wi.shape, BF16), pltpu.VMEM(wo.shape, BF16), pltpu.VMEM((HALO + tm, q), F32),
                        pltpu.VMEM((2 * DFT_F, q), F32)] + _gather_sems(3),
        input_output_aliases={n_in + a: n_saved + a for a in range(3)},
        compiler_params=_cparams(),
    )(x, wi, wo, *small_ins, *mlp_w)


def _hidden_slabs(f):
    assert f % MXU_N == 0
    tiles = f // MXU_N
    sizes = [(tiles // 4 + (1 if j < tiles % 4 else 0)) * MXU_N for j in range(4)]
    return [(sum(sizes[:j]), sz) for j, sz in enumerate(sizes) if sz]


def _fwd_mlp(xh1, tgt, wg, wu, wd, vd, alpha, slabs, tm):
    t, d = xh1.shape
    n = t // tm
    ns = len(slabs)
    half = tm // 2 if tm % 32 == 0 else tm

    def body(xh_ref, tgt_ref, wg_hbm, wu_hbm, wd_hbm, vd_ref, *rest):
        gp_refs, up_refs = rest[:ns], rest[ns:2 * ns]
        x1b_ref, dr2_ref, loss_ref, dg2_ref, db2_ref, wg_v, wu_v, wd_v = rest[2 * ns:]

        @pl.when(pl.program_id(0) == 0)
        def _():
            pltpu.sync_copy(wg_hbm, wg_v)
            pltpu.sync_copy(wu_hbm, wu_v)
            pltpu.sync_copy(wd_hbm, wd_v)
            loss_ref[...] = jnp.zeros_like(loss_ref)
            dg2_ref[...] = jnp.zeros_like(dg2_ref)
            db2_ref[...] = jnp.zeros_like(db2_ref)

        g2 = vd_ref[VD_LN2_G:VD_LN2_G + 1, :]
        for r0 in range(0, tm, half):
            rows = slice(r0, r0 + half)
            x1 = xh_ref[rows, :] * vd_ref[VD_LN1_G:VD_LN1_G + 1, :] + vd_ref[VD_LN1_B:VD_LN1_B + 1, :]
            x1b = x1.astype(BF16)
            x1b_ref[rows, :] = x1b
            acc = alpha * x1
            for (off, sz), gp_ref, up_ref in zip(slabs, gp_refs, up_refs):
                gp = _dot_nt(x1b, wg_v[off:off + sz, :])
                up = _dot_nt(x1b, wu_v[off:off + sz, :])
                gp_ref[rows, :] = gp.astype(BF16)
                up_ref[rows, :] = up.astype(BF16)
                acc = acc + _dot((gp * _sigmoid(gp) * up).astype(BF16), wd_v[off:off + sz, :])
            xh2, rstd2 = _ln_stats(acc)
            err = xh2 * g2 + vd_ref[VD_LN2_B:VD_LN2_B + 1, :] - tgt_ref[rows, :]
            loss_ref[...] += _colsum(jnp.sum(err * err, axis=1, keepdims=True)) * (0.5 / d)
            dy = err * (1.0 / d)
            dg2_ref[...] += _colsum(dy * xh2)
            db2_ref[...] += _colsum(dy)
            dr2_ref[rows, :] = _ln_bwd(dy * g2, xh2, rstd2)

    row = lambda w: pl.BlockSpec((tm, w), lambda i: (i, 0))
    act = [sz for _, sz in slabs] * 2
    return pl.pallas_call(
        body, name="fwd_mlp", grid=(n,),
        in_specs=[row(d), row(d), ANY, ANY, ANY, _full(vd.shape)],
        out_specs=[row(sz) for sz in act] + [row(d), row(d), _full((8, LANES)), _full((1, d)), _full((1, d))],
        out_shape=[jax.ShapeDtypeStruct((t, sz), BF16) for sz in act]
        + [jax.ShapeDtypeStruct((t, d), BF16), jax.ShapeDtypeStruct((t, d), F32),
           jax.ShapeDtypeStruct((8, LANES), F32), jax.ShapeDtypeStruct((1, d), F32), jax.ShapeDtypeStruct((1, d), F32)],
        scratch_shapes=[pltpu.VMEM(wg.shape, BF16), pltpu.VMEM(wu.shape, BF16), pltpu.VMEM(wd.shape, BF16)],
        compiler_params=_cparams(),
    )(xh1, tgt, wg, wu, wd, vd)


def _bwd_mlp_slab(j, slab, dr2, prev, x1b, gp, up, wg, wu, wd, alpha, tm):
    t, d = dr2.shape
    off, sz = slab
    n = t // tm
    first = prev is None

    def body(*refs):
        if first:
            dr_ref, x1b_ref, gp_ref, up_ref, wg_hbm, wu_hbm, wd_hbm = refs[:7]
        else:
            dr_ref, dxp_ref, x1b_ref, gp_ref, up_ref, wg_hbm, wu_hbm, wd_hbm = refs[:8]
        dx_ref, dwg_hbm, dwu_hbm, dwd_hbm, ag, au, ad, wg_v, wu_v, wd_v = refs[-10:]

        @pl.when(pl.program_id(0) == 0)
        def _():
            for src, dst in ((wg_hbm, wg_v), (wu_hbm, wu_v), (wd_hbm, wd_v)):
                pltpu.sync_copy(src.at[pl.ds(off, sz)], dst)
            ag[...] = jnp.zeros_like(ag)
            au[...] = jnp.zeros_like(au)
            ad[...] = jnp.zeros_like(ad)

        dr = dr_ref[...]
        drb = dr.astype(BF16)
        x1b = x1b_ref[...]
        gpv = gp_ref[...].astype(F32)
        upv = up_ref[...].astype(F32)
        dh = _dot_nt(drb, wd_v[...])
        sg = _sigmoid(gpv)
        silu = gpv * sg
        ad[...] += _dot_tn((silu * upv).astype(BF16), drb)
        dgp = (dh * upv * (sg * (1.0 + gpv * (1.0 - sg)))).astype(BF16)
        dup = (dh * silu).astype(BF16)
        ag[...] += _dot_tn(dgp, x1b)
        au[...] += _dot_tn(dup, x1b)
        base = alpha * dr if first else dxp_ref[...]
        dx_ref[...] = base + _dot(dgp, wg_v[...]) + _dot(dup, wu_v[...])

        @pl.when(pl.program_id(0) == n - 1)
        def _():
            for acc, dst in ((ag, dwg_hbm), (au, dwu_hbm), (ad, dwd_hbm)):
                pltpu.sync_copy(acc, dst.at[pl.ds(off, sz)])

    row = lambda w: pl.BlockSpec((tm, w), lambda i: (i, 0))
    ins = [dr2] + ([] if first else [prev[0]]) + [x1b, gp, up, wg, wu, wd] + ([] if first else list(prev[1:]))
    in_specs = [row(d)] + ([] if first else [row(d)]) + [row(d), row(sz), row(sz), ANY, ANY, ANY] + ([] if first else [ANY] * 3)
    return pl.pallas_call(
        body, name=f"bwd_mlp_{j}", grid=(n,),
        in_specs=in_specs,
        out_specs=[row(d), ANY, ANY, ANY],
        out_shape=[jax.ShapeDtypeStruct((t, d), F32)] + [jax.ShapeDtypeStruct(wg.shape, F32)] * 3,
        scratch_shapes=[pltpu.VMEM((sz, d), F32)] * 3 + [pltpu.VMEM((sz, d), BF16)] * 3,
        input_output_aliases={} if first else {8: 1, 9: 2, 10: 3},
        compiler_params=_cparams(),
    )(*ins)


def _bwd_mix(dx1, saved, wi, wo, wstt, cwf, tabs, vq, vd, mlp_sums, alpha, tm):
    xb, pag, y, xh1, rs, zu_s, mg_s, vhat_s, gv_s, vnb_s, yhat_s, hf_s = saved
    t, d = xh1.shape
    q = wi.shape[2]
    nc, n_pairs = tm // CHUNK, q // LANES
    n = t // tm
    assert tm == CONV_BLOCK

    def body(dx1_ref, xb_ref, pag_ref, y_ref, xh_ref, rs_ref, zu_ref, mg_ref, vhat_ref, gv_ref, vnb_ref, yhat_ref, hf_ref,
             wi_hbm, wo_hbm, wstt_ref, cwf_ref, fwd_ref, fwd_halo_ref, shift_ref, taps_ref, inv_ref, inv_taps_ref, vq_ref, vd_ref,
             sum_g, sum_u, sum_d,
             gx_ref, dwi_hbm, dwo_hbm, dws_ref, dbs_ref, dcw_ref, dvq_ref, dvd_ref, got_g, got_u, got_d,
             wi_v, wo_v, awi, awo, dyb_ref, later_ref, dbm_ref, gf_ref, dgf_ref, send_sems, recv_sems):
        i = pl.program_id(0)
        exchange = lambda: _Exchange((sum_g, sum_u, sum_d), (got_g, got_u, got_d), send_sems, recv_sems)

        @pl.when(i == 0)
        def _():
            exchange().start()
            pltpu.sync_copy(wi_hbm, wi_v)
            pltpu.sync_copy(wo_hbm, wo_v)
            for r in (awi, awo, dws_ref, dbm_ref, dgf_ref, dvq_ref, dvd_ref, dyb_ref, later_ref):
                r[...] = jnp.zeros_like(r)
            gf_ref[...] = _dot(taps_ref[...], _split(cwf_ref[...], True))

        dx1v = dx1_ref[...]
        xh = xh_ref[...]
        rsv = rs_ref[...]
        dvd_ref[VD_LN1_G:VD_LN1_G + 1, :] += _colsum(dx1v * xh)
        dvd_ref[VD_LN1_B:VD_LN1_B + 1, :] += _colsum(dx1v)
        dr1 = _ln_bwd(dx1v * vd_ref[VD_LN1_G:VD_LN1_G + 1, :], xh, rsv[:, RS_LN1:RS_LN1 + 1])
        dr1b = dr1.astype(BF16)
        awo[...] += _dot_tn(y_ref[...], dr1b)
        dy = _dot_nt(dr1b, wo_v[...])

        vhat = vhat_ref[...]
        sgu_g = vq_ref[VQ_SGU_G:VQ_SGU_G + 1, :]
        doa = dy[:, 0:q]
        dm = doa * zu_ref[...]
        dpu = (doa * mg_ref[...]).astype(BF16)
        acc = dm[0:CHUNK]
        for c in range(1, nc):
            acc = acc + dm[c * CHUNK:(c + 1) * CHUNK]
        dbm_ref[...] += acc
        for p, g in enumerate(_mix_wgrad(dm, vnb_ref[...], nc, n_pairs)):
            dws_ref[p] += g
        dvn = _mix(wstt_ref, dm, nc, n_pairs)
        dvq_ref[VQ_SGU_G:VQ_SGU_G + 1, :] += _colsum(dvn * vhat)
        dvq_ref[VQ_SGU_B:VQ_SGU_B + 1, :] += _colsum(dvn)
        dpv = (_ln_bwd(dvn * sgu_g, vhat, rsv[:, RS_SGU:RS_SGU + 1]) * gv_ref[...]).astype(BF16)

        pa = pag_ref[:, 0:q].astype(F32)
        sg = _sigmoid(pag_ref[:, q:2 * q].astype(F32))
        yhat = yhat_ref[...]
        cln_g = vq_ref[VQ_CLN_G:VQ_CLN_G + 1, :]
        yn = yhat * cln_g + vq_ref[VQ_CLN_B:VQ_CLN_B + 1, :]
        sy = _sigmoid(yn)
        dyn = dy[:, q:2 * q] * (sy * (1.0 + yn * (1.0 - sy)))
        dvq_ref[VQ_CLN_G:VQ_CLN_G + 1, :] += _colsum(dyn * yhat)
        dvq_ref[VQ_CLN_B:VQ_CLN_B + 1, :] += _colsum(dyn)
        dyc = _ln_bwd(dyn * cln_g, yhat, rsv[:, RS_CONV:RS_CONV + 1])
        dvq_ref[VQ_CONV_B:VQ_CONV_B + 1, :] += _colsum(dyc)
        dyb_ref[0:tm, :] = dyc
        own = _dot(fwd_ref[...], _split(dyb_ref[...]))
        dgf_ref[...] += _cmul(_cmul(own, shift_ref[...]), hf_ref[...], conj_b=True)
        with_later = own + _dot(fwd_halo_ref[...], _split(later_ref[...]))
        dh = _dot(inv_ref[...], _split(_cmul(with_later, gf_ref[...], conj_b=True)))
        later_ref[...] = dyb_ref[0:HALO, :]
        da = (dh * sg).astype(BF16)
        dg = (dh * pa * (sg * (1.0 - sg))).astype(BF16)

        xb = xb_ref[...]
        gx = alpha * dr1
        for j, dpj in enumerate((dpu, dpv, da, dg)):
            awi[j] += _dot_tn(xb, dpj)
            gx = gx + _dot_nt(dpj, wi_v[j])
        gx_ref[...] = gx

        @pl.when(i == n - 1)
        def _():
            pltpu.sync_copy(awi, dwi_hbm)
            pltpu.sync_copy(awo, dwo_hbm)
            lane = lax.broadcasted_iota(jnp.int32, (CHUNK, LANES), 1)
            low = lane < HEAD_DIM
            dbs = jnp.zeros((CHUNK, LANES), F32)
            for p in range(n_pairs):
                grp = dbm_ref[:, p * LANES:(p + 1) * LANES]
                dbs = jnp.where(lane == 2 * p, jnp.sum(jnp.where(low, grp, 0.0), axis=1, keepdims=True), dbs)
                dbs = jnp.where(lane == 2 * p + 1, jnp.sum(jnp.where(low, 0.0, grp), axis=1, keepdims=True), dbs)
            dbs_ref[...] = dbs
            dcw_ref[...] = _dot(inv_taps_ref[...], _split(dgf_ref[...], True))
            exchange().finish()

    rev = lambda w: pl.BlockSpec((tm, w), lambda i: (n - 1 - i, 0))
    small = [jax.ShapeDtypeStruct((n_pairs, 2 * CHUNK, CHUNK), F32), jax.ShapeDtypeStruct((CHUNK, LANES), F32),
             jax.ShapeDtypeStruct(cwf.shape, F32), jax.ShapeDtypeStruct(vq.shape, F32), jax.ShapeDtypeStruct(vd.shape, F32)]
    small_ins = [wstt, cwf, tabs["fwd"], tabs["fwd_halo"], tabs["shift"], tabs["taps"], tabs["inv_in"], tabs["inv_taps"], vq, vd]
    return pl.pallas_call(
        body, name="bwd_mix", grid=(n,),
        in_specs=[rev(d), rev(d), rev(2 * q), rev(d), rev(d), rev(RS_COLS), rev(q), rev(q), rev(q), rev(q), rev(q), rev(q),
                  pl.BlockSpec((2 * DFT_F, q), lambda i: (n - 1 - i, 0)), ANY, ANY] + [_full(a.shape) for a in small_ins] + [ANY] * 3,
        out_specs=[rev(d), ANY, ANY] + [_full(s.shape) for s in small] + [ANY] * 3,
        out_shape=[jax.ShapeDtypeStruct((t, d), F32), jax.ShapeDtypeStruct(wi.shape, F32), jax.ShapeDtypeStruct(wo.shape, F32)] + small
        + _exchange_shapes(mlp_sums),
        scratch_shapes=[pltpu.VMEM(wi.shape, BF16), pltpu.VMEM(wo.shape, BF16), pltpu.VMEM(wi.shape, F32), pltpu.VMEM(wo.shape, F32),
                        pltpu.VMEM((DFT_N, q), F32), pltpu.VMEM((HALO, q), F32),
                        pltpu.VMEM((CHUNK, q), F32), pltpu.VMEM((2 * DFT_F, q), F32), pltpu.VMEM((2 * DFT_F, q), F32)]
        + _exchange_sems(3),
        compiler_params=_cparams(),
    )(dx1, xb, pag, y, xh1, rs, zu_s, mg_s, vhat_s, gv_s, vnb_s, yhat_s, hf_s, wi, wo, *small_ins, *mlp_sums)


def _prep(me_arr, w_in, w_out, w_gate_t, w_up_t, w_down, conv_w, kwp):
    kw, cshard = conv_w.shape

    def body(me_ref, wi_ref, wo_ref, wg_ref, wu_ref, wd_ref, cw_ref, oi, oo, og, ou, od, oc):
        for src, dst in ((wi_ref, oi), (wo_ref, oo), (wg_ref, og), (wu_ref, ou), (wd_ref, od)):
            dst[...] = src[...].astype(BF16)
        oc[0:kw, :] = cw_ref[...]
        oc[kw:kwp, :] = jnp.zeros((kwp - kw, cshard), F32)

    ins = (w_in, w_out, w_gate_t, w_up_t, w_down, conv_w)
    outs = [jax.ShapeDtypeStruct((4,) + a.shape, BF16) for a in ins[:5]] + [jax.ShapeDtypeStruct((4, kwp, cshard), F32)]
    grid_spec = pltpu.PrefetchScalarGridSpec(
        num_scalar_prefetch=1, grid=(1,),
        in_specs=[pl.BlockSpec(a.shape, lambda i, me: (0, 0)) for a in ins],
        out_specs=[pl.BlockSpec((None,) + o.shape[1:], lambda i, me: (me[0], 0, 0)) for o in outs])
    return pl.pallas_call(body, name="wprep", grid_spec=grid_spec, out_shape=outs, compiler_params=_cparams())(me_arr, *ins)


def _coords():
    return tuple(lax.axis_index(a) for a in MESH_AXES)


def _other_chips(x, y):
    return [(1 - x, y), (x, 1 - y), (1 - x, 1 - y)]


def _remote(src, dst, send_sem, recv_sem, to):
    return pltpu.make_async_remote_copy(src_ref=src, dst_ref=dst, send_sem=send_sem, recv_sem=recv_sem,
                                        device_id=to, device_id_type=MESH_ID)


def _hbm_call(body, name, ins, out_shape, scratch_shapes, aliases=None):
    return pl.pallas_call(
        body, name=name, in_specs=[ANY] * len(ins), out_specs=[ANY] * len(out_shape), out_shape=out_shape,
        scratch_shapes=scratch_shapes, input_output_aliases=aliases or {},
    )(*ins)


class _Gather:
    def __init__(self, bufs, send_sems, recv_sems):
        self.bufs, self.send_sems, self.recv_sems = bufs, send_sems, recv_sems
        self.x, self.y, self.c = _coords()

    def _copies(self, stage):
        x, y, c = self.x, self.y, self.c
        for a, buf in enumerate(self.bufs):
            hr = buf.shape[1] // 2
            for j, chip in enumerate(_other_chips(x, y)):
                if stage == "ici_out":
                    ref, k, to = buf.at[2 * x + y, pl.ds(c * hr, hr)], j, (*chip, c)
                elif stage == "ici_in":
                    ref, k, to = buf.at[2 * chip[0] + chip[1], pl.ds(c * hr, hr)], j, (*chip, c)
                elif stage == "d2d_out":
                    ref, k, to = buf.at[2 * chip[0] + chip[1], pl.ds(c * hr, hr)], 3 + j, (x, y, 1 - c)
                else:
                    ref, k, to = buf.at[2 * chip[0] + chip[1], pl.ds((1 - c) * hr, hr)], 3 + j, (x, y, 1 - c)
                yield _remote(ref, ref, self.send_sems.at[a, k], self.recv_sems.at[a, k], to)

    def start(self):
        for cp in self._copies("ici_out"):
            cp.start()

    def forward(self):
        for landed, onward in zip(self._copies("ici_in"), self._copies("d2d_out")):
            landed.wait_recv()
            onward.start()

    def finish(self):
        for cp in self._copies("d2d_in"):
            cp.wait_recv()
        for stage in ("ici_out", "d2d_out"):
            for cp in self._copies(stage):
                cp.wait_send()


def _gather_sems(n):
    return [pltpu.SemaphoreType.DMA((n, 6)), pltpu.SemaphoreType.DMA((n, 6))]


def _gather_shards(bufs):
    n = len(bufs)

    def body(*refs):
        g = _Gather(refs[n:2 * n], *refs[2 * n:])
        g.start()
        g.forward()
        g.finish()

    return _hbm_call(body, "gather_shards", bufs, [jax.ShapeDtypeStruct(s.shape, s.dtype) for s in bufs],
                     _gather_sems(n), aliases={a: a for a in range(n)})


def _pair_swap(name, arrs):
    n = len(arrs)

    def body(*refs):
        src, land = refs[:n], refs[n:2 * n]
        send_sems, recv_sems = refs[2 * n:]
        x, y, c = _coords()
        copies = []
        for a in range(n):
            s = src[a].at[pl.ds(0, arrs[a].shape[0]), 1 - c] if arrs[a].ndim == 4 else src[a].at[1 - c]
            copies.append(_remote(s, land[a], send_sems.at[a], recv_sems.at[a], (x, y, 1 - c)))
            copies[-1].start()
        for cp in copies:
            cp.wait()

    outs = [jax.ShapeDtypeStruct(s.shape[:-3] + s.shape[-2:], s.dtype) for s in arrs]
    return _hbm_call(body, name, arrs, outs, [pltpu.SemaphoreType.DMA((n,)), pltpu.SemaphoreType.DMA((n,))])


class _Exchange:
    def __init__(self, src, dst, send_sems, recv_sems):
        self.src, self.dst, self.send_sems, self.recv_sems = src, dst, send_sems, recv_sems
        self.x, self.y, self.c = _coords()

    def _copies(self, incoming):
        x, y, c = self.x, self.y, self.c
        for a, (s, d) in enumerate(zip(self.src, self.dst)):
            for j, chip in enumerate(_other_chips(x, y)):
                slot = 2 * chip[0] + chip[1]
                if incoming:
                    out, into = d.at[slot], d.at[slot]
                else:
                    out, into = (s.at[slot] if len(s.shape) == 3 else s), d.at[2 * x + y]
                yield _remote(out, into, self.send_sems.at[a, j], self.recv_sems.at[a, j], (*chip, c))

    def start(self):
        for cp in self._copies(False):
            cp.start()

    def finish(self):
        for cp in self._copies(True):
            cp.wait_recv()
        for cp in self._copies(False):
            cp.wait_send()


def _exchange_sems(n):
    return [pltpu.SemaphoreType.DMA((n, 3)), pltpu.SemaphoreType.DMA((n, 3))]


def _exchange_shapes(arrs):
    return [jax.ShapeDtypeStruct((4,) + s.shape[-2:], s.dtype) for s in arrs]


def _chip_exchange(arrs):
    n = len(arrs)

    def body(*refs):
        ex = _Exchange(refs[:n], refs[n:2 * n], *refs[2 * n:])
        ex.start()
        ex.finish()

    return _hbm_call(body, "chip_exchange", arrs, _exchange_shapes(arrs), _exchange_sems(n))


def _pair_gather(halves):
    n = len(halves)

    def body(*refs):
        src, dst = refs[:n], refs[n:2 * n]
        send_sems, recv_sems = refs[2 * n:]
        x, y, c = _coords()
        copies = [_remote(src[a], dst[a], send_sems.at[a], recv_sems.at[a], (x, y, 1 - c)) for a in range(n)]
        for cp in copies:
            cp.start()
        for cp in copies:
            cp.wait()

    outs = [jax.ShapeDtypeStruct(s.shape, s.dtype) for s in halves]
    return _hbm_call(body, "pair_gather", halves, outs, [pltpu.SemaphoreType.DMA((n,)), pltpu.SemaphoreType.DMA((n,))])


def _pair_sum(a, g, land, c_arr, out_dtype):
    nq, _, hr, cc = g.shape

    def body(c_ref, g_ref, l_ref, o_ref):
        o_ref[...] = (g_ref[...] + l_ref[...]).astype(out_dtype)

    spec = pl.BlockSpec((None, hr, cc), lambda qi, cr: (qi, 0, 0))
    grid_spec = pltpu.PrefetchScalarGridSpec(
        num_scalar_prefetch=1, grid=(nq,),
        in_specs=[pl.BlockSpec((None, None, hr, cc), lambda qi, cr: (qi, cr[0], 0, 0)), spec], out_specs=spec)
    return pl.pallas_call(body, name=f"pair_sum_{a}", grid_spec=grid_spec, out_shape=jax.ShapeDtypeStruct((nq, hr, cc), out_dtype),
                          compiler_params=_cparams())(c_arr, g, land)


def _chip_sum(a, parts, own, me_arr):
    _, hr, cc = parts.shape

    def body(me_ref, p_ref, own_ref, o_ref):
        for mine in range(4):
            @pl.when(me_ref[0] == mine)
            def _():
                term = lambda j: (own_ref if j == mine else p_ref.at[j])[...].astype(F32)
                o_ref[...] = ((term(0) + term(1)) + term(2)) + term(3)

    own_spec = (pl.BlockSpec((None, hr, cc), lambda i, me: (me[0], 0, 0)) if own.ndim == 3
                else pl.BlockSpec((hr, cc), lambda i, me: (0, 0)))
    grid_spec = pltpu.PrefetchScalarGridSpec(
        num_scalar_prefetch=1, grid=(1,),
        in_specs=[pl.BlockSpec((4, hr, cc), lambda i, me: (0, 0, 0)), own_spec],
        out_specs=pl.BlockSpec((hr, cc), lambda i, me: (0, 0)))
    return pl.pallas_call(body, name=f"chip_sum_{a}", grid_spec=grid_spec, out_shape=jax.ShapeDtypeStruct((hr, cc), F32),
                          compiler_params=_cparams())(me_arr, parts, own)


def _row_block(rows, cols, limit=1 << 20):
    best = 8
    for tr in range(8, rows + 1, 8):
        if rows % tr == 0 and tr * cols * 4 <= limit:
            best = tr
    return best


def _adamw(name, w, g_mine, g_other, m, v, c_arr):
    r, c = w.shape
    hr, cg = g_mine.shape
    tr = hr if r % hr == 0 and hr * cg * 4 <= (3 << 19) else math.gcd(_row_block(hr, cg), r)
    per_half = hr // tr
    bc1 = 1.0 - ADAM_B1 ** ADAM_STEP
    bc2 = 1.0 - ADAM_B2 ** ADAM_STEP

    def body(c_ref, w_ref, gm_ref, go_ref, m_ref, v_ref, go, do, mo, vo):
        gv = jnp.where(pl.program_id(0) // per_half == c_ref[0], gm_ref[:, 0:c], go_ref[:, 0:c])
        mn = ADAM_B1 * m_ref[...] + (1.0 - ADAM_B1) * gv
        vn = ADAM_B2 * v_ref[...] + (1.0 - ADAM_B2) * (gv * gv)
        go[...] = gv
        mo[...] = mn
        vo[...] = vn
        do[...] = -ADAM_LR * ((mn / bc1) / (jnp.sqrt(vn / bc2) + ADAM_EPS) + ADAM_WD * w_ref[...])

    blk = pl.BlockSpec((tr, c), lambda i, cr: (i, 0))
    gblk = pl.BlockSpec((tr, cg), lambda i, cr: (i % per_half, 0))
    grid_spec = pltpu.PrefetchScalarGridSpec(num_scalar_prefetch=1, grid=(r // tr,), in_specs=[blk, gblk, gblk, blk, blk],
                                             out_specs=[blk] * 4)
    return pl.pallas_call(body, name=f"adamw_{name}", grid_spec=grid_spec, out_shape=[jax.ShapeDtypeStruct((r, c), F32)] * 4,
                          compiler_params=_cparams())(c_arr, w, g_mine, g_other, m, v)


def _rows128(a):
    return a.reshape(-1, LANES)


def _pad_rows(a, rows):
    return jnp.pad(a, ((0, rows - a.shape[0]), (0, 0)))


def kernel(x, w_in, sgu_ln_g, sgu_ln_b, w_s, b_s, conv_w, conv_b, conv_ln_g, conv_ln_b, w_out, ln1_g, ln1_b, w_gate, w_up, w_down, ln2_g, ln2_b, loss_target, m_w_in, m_sgu_ln_g, m_sgu_ln_b, m_w_s, m_b_s, m_conv_w, m_conv_b, m_conv_ln_g, m_conv_ln_b, m_w_out, m_ln1_g, m_ln1_b, m_w_gate, m_w_up, m_w_down, m_ln2_g, m_ln2_b, v_w_in, v_sgu_ln_g, v_sgu_ln_b, v_w_s, v_b_s, v_conv_w, v_conv_b, v_conv_ln_g, v_conv_ln_b, v_w_out, v_ln1_g, v_ln1_b, v_w_gate, v_w_up, v_w_down, v_ln2_g, v_ln2_b):
    depth, d, q = w_in.shape
    assert depth == 1 and x.shape[0] == 1
    t = x.shape[1]
    heads = w_s.shape[1]
    kw, cshard = conv_w.shape[1], conv_w.shape[2]
    fs = w_gate.shape[2]
    slabs = _hidden_slabs(4 * fs)
    n_pairs = q // LANES
    assert heads * HEAD_DIM == q and q % LANES == 0 and w_s.shape[2] == CHUNK and 4 * cshard == q and kw - 1 <= HALO
    alpha = (2.0 * depth) ** 0.25
    tm = min(512, t)
    assert t % tm == 0 and tm % CHUNK == 0
    x2, tgt = x[0], loss_target[0]
    mx, my, mc = _coords()
    me = 2 * mx + my
    c_arr = jnp.reshape(mc, (1,)).astype(jnp.int32)

    kwp = -(-kw // 16) * 16
    me_arr = jnp.reshape(me, (1,)).astype(jnp.int32)
    wi, wo, wg, wu, wd, cw4 = _prep(me_arr, w_in[0], w_out[0], w_gate[0].T, w_up[0].T, w_down[0], conv_w[0], kwp)
    wi, wo, cw4 = _gather_shards([wi, wo, cw4])
    wo = wo.reshape(d, d)
    cw = jnp.transpose(cw4, (1, 0, 2)).reshape(kwp, q)
    cwf = _pad_rows(cw[:kw][::-1], kwp)
    tabs = {name: jnp.asarray(tab) for name, tab in _dft_tables(kw, kwp, q).items()}

    wm = jnp.where(jnp.tril(jnp.ones((CHUNK, CHUNK), bool)), w_s[0], 0.0)
    wst = wm.reshape(n_pairs, 2 * CHUNK, CHUNK).astype(BF16)
    wstt = jnp.transpose(wm, (0, 2, 1)).reshape(n_pairs, 2 * CHUNK, CHUNK).astype(BF16)
    bmat = jnp.repeat(b_s[0].T, HEAD_DIM, axis=1)
    vq = _pad_rows(jnp.concatenate([sgu_ln_g, sgu_ln_b, conv_b, conv_ln_g, conv_ln_b], axis=0), 8)
    vd = _pad_rows(jnp.concatenate([ln1_g, ln1_b, ln2_g, ln2_b], axis=0), 8)

    *saved, wg, wu, wd = _fwd_mix(x2, wi, wo, wst, bmat, cwf, tabs, vq, vd, [wg, wu, wd], alpha, tm)
    wg, wu, wd = (w.reshape(4 * fs, d) for w in (wg, wu, wd))
    *acts, x1b, dr2, loss_part, dg2, db2 = _fwd_mlp(saved[3], tgt, wg, wu, wd, vd, alpha, slabs, tm)
    mlp_grads = None
    for j, slab in enumerate(slabs):
        mlp_grads = _bwd_mlp_slab(j, slab, dr2, mlp_grads, x1b, acts[j], acts[len(slabs) + j], wg, wu, wd, alpha, tm)
    dx1 = mlp_grads[0]
    mlp_halves = [b.reshape(4, 2, fs // 2, d) for b in mlp_grads[1:]]
    mlp_sums = [_pair_sum(f"mlp{a}", h, l, c_arr, BF16) for a, (h, l) in enumerate(zip(mlp_halves, _pair_swap("pair_swap_mlp", mlp_halves)))]
    grad_x, dwi, dwo, dws, dbs, dcw, dvq, dvd, *mlp_parts = _bwd_mix(dx1, saved, wi, wo, wstt, cwf, tabs, vq, vd, mlp_sums, alpha, CONV_BLOCK)

    dws = jnp.where(jnp.tril(jnp.ones((CHUNK, CHUNK), bool)), dws.reshape(heads, CHUNK, CHUNK), 0.0)
    dvd = dvd.at[VD_LN2_G].set(dg2[0]).at[VD_LN2_B].set(db2[0])
    pieces = [_rows128(dws), dbs[:, :heads].T, _rows128(dcw), _rows128(dvq), _rows128(dvd), loss_part]
    sizes = [p.shape[0] for p in pieces]
    rows = -(-sum(sizes) // 16) * 16
    small = _pad_rows(jnp.concatenate(pieces, axis=0), rows)

    big = [dwi, dwo.reshape(4, d // 4, d)]
    halves = [b.reshape(4, 2, b.shape[1] // 2, b.shape[2]) for b in big] + [small.reshape(2, rows // 2, LANES)]
    landed = _pair_swap("pair_swap_mix", halves)
    sums = [_pair_sum(f"mix{a}", h, l, c_arr, BF16) for a, (h, l) in enumerate(zip(halves[:-1], landed[:-1]))]
    sums.append(_pair_sum("small", halves[-1][None], landed[-1][None], c_arr, F32)[0])
    parts = _chip_exchange(sums)
    parts = list(parts)
    parts, sums = parts[:2] + list(mlp_parts) + parts[2:], sums[:2] + list(mlp_sums) + sums[2:]
    mine = [_chip_sum(a, p, s, me_arr) for a, (p, s) in enumerate(zip(parts, sums))]
    other = _pair_gather(mine)

    def pack(ws, bs, vqs, vds):
        ps = [_rows128(ws[0]), bs[0], jnp.zeros((sizes[2], LANES), F32),
              _rows128(_pad_rows(jnp.concatenate(vqs, axis=0), 8)), _rows128(_pad_rows(jnp.concatenate(vds, axis=0), 8)),
              jnp.zeros((sizes[5], LANES), F32)]
        return _pad_rows(jnp.concatenate(ps, axis=0), rows)

    out = {}
    for a, (nm, w_, m_, v_) in enumerate((("w_in", w_in, m_w_in, v_w_in), ("w_out", w_out, m_w_out, v_w_out),
                                          ("w_gate", w_gate, m_w_gate, v_w_gate), ("w_up", w_up, m_w_up, v_w_up),
                                          ("w_down", w_down, m_w_down, v_w_down))):
        if nm in ("w_gate", "w_up"):
            out[nm] = [o.T for o in _adamw(nm, w_[0].T, mine[a], other[a], m_[0].T, v_[0].T, c_arr)]
        else:
            out[nm] = _adamw(nm, w_[0], mine[a], other[a], m_[0], v_[0], c_arr)
    packed = _adamw(
        "small",
        pack(w_s, b_s, [sgu_ln_g, sgu_ln_b, conv_b, conv_ln_g, conv_ln_b], [ln1_g, ln1_b, ln2_g, ln2_b]), mine[-1], other[-1],
        pack(m_w_s, m_b_s, [m_sgu_ln_g, m_sgu_ln_b, m_conv_b, m_conv_ln_g, m_conv_ln_b], [m_ln1_g, m_ln1_b, m_ln2_g, m_ln2_b]),
        pack(v_w_s, v_b_s, [v_sgu_ln_g, v_sgu_ln_b, v_conv_b, v_conv_ln_g, v_conv_ln_b], [v_ln1_g, v_ln1_b, v_ln2_g, v_ln2_b]),
        c_arr)

    offs = [sum(sizes[:i]) for i in range(len(sizes))]
    g_cw_full = packed[0][offs[2]:offs[2] + sizes[2]].reshape(kwp, q)
    g_cw = lax.dynamic_slice(g_cw_full, (0, me * cshard), (kwp, cshard))
    out["conv_w"] = _adamw("conv_w", _pad_rows(conv_w[0], kwp), g_cw, g_cw, _pad_rows(m_conv_w[0], kwp), _pad_rows(v_conv_w[0], kwp), c_arr)
    out["conv_w"] = [o[:kw] for o in out["conv_w"]]

    def unpack(p):
        vq_o = p[offs[3]:offs[3] + sizes[3]].reshape(8, q)
        vd_o = p[offs[4]:offs[4] + sizes[4]].reshape(8, d)
        return {"w_s": p[offs[0]:offs[0] + sizes[0]].reshape(heads, CHUNK, CHUNK), "b_s": p[offs[1]:offs[1] + sizes[1]],
                "sgu_ln_g": vq_o[VQ_SGU_G], "sgu_ln_b": vq_o[VQ_SGU_B], "conv_b": vq_o[VQ_CONV_B],
                "conv_ln_g": vq_o[VQ_CLN_G], "conv_ln_b": vq_o[VQ_CLN_B],
                "ln1_g": vd_o[VD_LN1_G], "ln1_b": vd_o[VD_LN1_B], "ln2_g": vd_o[VD_LN2_G], "ln2_b": vd_o[VD_LN2_B]}

    small_out = [unpack(p) for p in packed]
    loss = packed[0][offs[5], 0]
    names = ["w_in", "sgu_ln_g", "sgu_ln_b", "w_s", "b_s", "conv_w", "conv_b", "conv_ln_g", "conv_ln_b", "w_out",
             "ln1_g", "ln1_b", "w_gate", "w_up", "w_down", "ln2_g", "ln2_b"]
    result = [loss, grad_x[None]]
    for kind in range(4):
        for nm in names:
            val = out[nm][kind] if nm in out else small_out[kind][nm]
            result.append(val[None])
    return tuple(result)
```

```python
import functools
import math

import jax
import numpy as np
import jax.numpy as jnp
from jax import lax
from jax.experimental import pallas as pl
from jax.experimental.pallas import tpu as pltpu

F32 = jnp.float32
BF16 = jnp.bfloat16

LN_EPS = 1e-5
HEAD_DIM = 64
CHUNK = 128
HALO = 32
LANES = 128
MXU_N = 256
ADAM_LR, ADAM_B1, ADAM_B2, ADAM_EPS, ADAM_WD, ADAM_STEP = 0.001, 0.9, 0.999, 1e-08, 0.01, 10
VMEM_LIMIT = 63 * 1024 * 1024
MESH_AXES = ("x", "y", "c")
MESH_ID = pl.DeviceIdType.MESH


def _dot(a, b):
    return jnp.dot(a, b, preferred_element_type=F32)


def _dot_nt(a, b):
    return lax.dot_general(a, b, (((1,), (1,)), ((), ())), preferred_element_type=F32)


def _dot_tn(a, b):
    return lax.dot_general(a, b, (((0,), (0,)), ((), ())), preferred_element_type=F32)


def _sigmoid(v):
    return 1.0 / (1.0 + jnp.exp(-v))


def _gelu(v):
    cdf = 0.5 * (1.0 + lax.erf(v * (1.0 / math.sqrt(2.0))))
    pdf = jnp.exp(-0.5 * v * v) * (1.0 / math.sqrt(2.0 * math.pi))
    return v * cdf, cdf + v * pdf


def _ln_stats(v):
    mu = jnp.mean(v, axis=-1, keepdims=True)
    d = v - mu
    rstd = lax.rsqrt(jnp.mean(d * d, axis=-1, keepdims=True) + LN_EPS)
    return d * rstd, rstd


def _ln_bwd(dxhat, xhat, rstd):
    m1 = jnp.mean(dxhat, axis=-1, keepdims=True)
    m2 = jnp.mean(dxhat * xhat, axis=-1, keepdims=True)
    return rstd * (dxhat - m1 - xhat * m2)


def _colsum(v):
    return jnp.sum(v, axis=0, keepdims=True)


def _pair_lanes(v, nc, p):
    return jnp.concatenate([v[c * CHUNK:(c + 1) * CHUNK, p * LANES:(p + 1) * LANES] for c in range(nc)], axis=1)


def _unpair(parts, nc):
    rows = [jnp.concatenate([part[:, c * LANES:(c + 1) * LANES] for part in parts], axis=1) for c in range(nc)]
    return jnp.concatenate(rows, axis=0)


def _low_head(nc):
    lane = lax.broadcasted_iota(jnp.int32, (CHUNK, nc * LANES), 1)
    return (lane & (LANES - 1)) < HEAD_DIM


def _mix(wst_ref, v, nc, n_pairs):
    vb = v.astype(BF16)
    low = _low_head(nc)
    parts = []
    for p in range(n_pairs):
        r = _dot(wst_ref[p], _pair_lanes(vb, nc, p))
        parts.append(jnp.where(low, r[:CHUNK], r[CHUNK:]))
    return _unpair(parts, nc)


def _mix_wgrad(dm, vn, nc, n_pairs):
    low = _low_head(nc)
    vb = vn.astype(BF16)
    out = []
    for p in range(n_pairs):
        a = _pair_lanes(dm, nc, p)
        lhs = jnp.concatenate([jnp.where(low, a, 0.0), jnp.where(low, 0.0, a)], axis=0).astype(BF16)
        out.append(_dot_nt(lhs, _pair_lanes(vb, nc, p)))
    return out


SUBLANES = 8


CONV_BLOCK = 256
DFT_N = CONV_BLOCK + HALO
DFT_F = -(-(DFT_N // 2 + 1) // SUBLANES) * SUBLANES


def _terms(m, exact):
    hi = m.astype(np.float32).astype(BF16)
    lo = (m.astype(np.float32) - hi.astype(np.float32)).astype(BF16)
    return np.concatenate([hi, hi, lo] if exact else [hi, hi], axis=1)


def _split(v, exact=False):
    hi = v.astype(BF16)
    lo = (v - hi.astype(F32)).astype(BF16)
    return jnp.concatenate([hi, lo, hi] if exact else [hi, lo], axis=0)


def _dft_tables(kw, kwp, q):
    nf = DFT_N // 2 + 1
    ang = 2.0 * np.pi * np.arange(nf)[:, None] * np.arange(DFT_N)[None, :] / DFT_N
    fwd = np.zeros((2 * DFT_F, DFT_N))
    fwd[:nf], fwd[DFT_F:DFT_F + nf] = np.cos(ang), -np.sin(ang)
    weight = np.full((nf, 1), 2.0 / DFT_N)
    weight[0] = weight[-1] = 1.0 / DFT_N
    inv = np.zeros((DFT_N, 2 * DFT_F))
    inv[:, :nf], inv[:, DFT_F:DFT_F + nf] = (np.cos(ang) * weight).T, (-np.sin(ang) * weight).T
    inv_taps = np.zeros((kwp, 2 * DFT_F))
    inv_taps[:kw] = inv[kw - 1::-1][:kw]
    shift = np.zeros((2 * DFT_F, q), np.float32)
    shift[:nf], shift[DFT_F:DFT_F + nf] = np.cos(ang[:, HALO:HALO + 1]), -np.sin(ang[:, HALO:HALO + 1])
    return {"fwd": _terms(fwd, False), "fwd_halo": _terms(fwd[:, CONV_BLOCK:], False), "shift": shift,
            "inv_out": _terms(inv[HALO:HALO + CONV_BLOCK], False), "inv_in": _terms(inv[:CONV_BLOCK], False),
            "taps": _terms(fwd[:, :kwp], True), "inv_taps": _terms(inv_taps, True)}


def _cmul(a, b, conj_b=False):
    ar, ai, br, bi = a[:DFT_F], a[DFT_F:], b[:DFT_F], b[DFT_F:]
    if conj_b:
        return jnp.concatenate([ar * br + ai * bi, ai * br - ar * bi], axis=0)
    return jnp.concatenate([ar * br - ai * bi, ar * bi + ai * br], axis=0)


def _interleave(sub_tiles):
    waiting, live = list(sub_tiles), []
    while waiting or live:
        if waiting:
            live.append(waiting.pop(0))
        for g in list(live):
            try:
                next(g)
            except StopIteration:
                live.remove(g)


def _cparams():
    return pltpu.CompilerParams(dimension_semantics=("arbitrary",), vmem_limit_bytes=VMEM_LIMIT)


def _full(shape):
    return pl.BlockSpec(shape, lambda i: (0,) * len(shape))


ANY = pl.BlockSpec(memory_space=pl.ANY)

VQ_SGU_G, VQ_SGU_B, VQ_CONV_B, VQ_CLN_G, VQ_CLN_B = range(5)
VD_LN1_G, VD_LN1_B, VD_LN2_G, VD_LN2_B = range(4)
RS_LN1, RS_SGU, RS_CONV = range(3)
RS_COLS = 8


def _fwd_mix(x, wi, wo, wst, bmat, cwf, tabs, vq, vd, mlp_w, alpha, tm):
    t, d = x.shape
    q = wi.shape[2]
    nc, n_pairs = CONV_BLOCK // CHUNK, q // LANES
    n = t // tm
    n_in, n_saved = 11, 12
    assert tm % CONV_BLOCK == 0

    def body(x_ref, wi_hbm, wo_hbm, wst_ref, bmat_ref, cwf_ref, fwd_ref, taps_ref, inv_ref, vq_ref, vd_ref, *rest):
        (xb_ref, pag_ref, y_ref, xh_ref, rs_ref, zu_ref, mg_ref, vhat_ref, gv_ref, vnb_ref, yhat_ref, hf_ref) = rest[3:3 + n_saved]
        gathered = rest[3 + n_saved:6 + n_saved]
        wi_v, wo_v, hb_ref, gf_ref, send_sems, recv_sems = rest[6 + n_saved:]
        step = pl.program_id(0)

        @pl.when(step == 0)
        def _():
            _Gather(gathered, send_sems, recv_sems).start()
            pltpu.sync_copy(wi_hbm, wi_v)
            pltpu.sync_copy(wo_hbm, wo_v)
            hb_ref[...] = jnp.zeros_like(hb_ref)
            gf_ref[...] = _dot(taps_ref[...], _split(cwf_ref[...], True))

        @pl.when(step == (3 * n) // 4)
        def _():
            _Gather(gathered, send_sems, recv_sems).forward()

        def sub_tile(b):
            rows = slice(b * CONV_BLOCK, (b + 1) * CONV_BLOCK)
            xv = x_ref[rows, :]
            xb = xv.astype(BF16)
            xb_ref[rows, :] = xb
            pu, pv, pa, pg = (_dot(xb, wi_v[j]) for j in range(4))
            yield
            pag_ref[rows, 0:q] = pa.astype(BF16)
            pag_ref[rows, q:2 * q] = pg.astype(BF16)
            zu, gu = _gelu(pu)
            zv, gv = _gelu(pv)
            vhat, rstd_v = _ln_stats(zv)
            vnb = (vhat * vq_ref[VQ_SGU_G:VQ_SGU_G + 1, :] + vq_ref[VQ_SGU_B:VQ_SGU_B + 1, :]).astype(BF16)
            hb_ref[HALO + b * CONV_BLOCK:HALO + (b + 1) * CONV_BLOCK, :] = pa * _sigmoid(pg)
            yield
            mixed = _mix(wst_ref, vnb, nc, n_pairs) + jnp.concatenate([bmat_ref[...]] * nc, axis=0)
            spectrum = _dot(fwd_ref[...], _split(hb_ref[b * CONV_BLOCK:b * CONV_BLOCK + DFT_N, :]))
            yield
            y_ref[rows, 0:q] = (zu * mixed).astype(BF16)
            zu_ref[rows, :] = zu
            mg_ref[rows, :] = mixed * gu
            vhat_ref[rows, :] = vhat
            gv_ref[rows, :] = gv
            vnb_ref[rows, :] = vnb
            hf_ref[b * 2 * DFT_F:(b + 1) * 2 * DFT_F, :] = spectrum
            product = _split(_cmul(gf_ref[...], spectrum))
            yield
            yc = _dot(inv_ref[...], product) + vq_ref[VQ_CONV_B:VQ_CONV_B + 1, :]
            yield
            yhat, rstd_c = _ln_stats(yc)
            yhat_ref[rows, :] = yhat
            yn = yhat * vq_ref[VQ_CLN_G:VQ_CLN_G + 1, :] + vq_ref[VQ_CLN_B:VQ_CLN_B + 1, :]
            y_ref[rows, q:2 * q] = (yn * _sigmoid(yn)).astype(BF16)
            yield
            r1 = alpha * xv + _dot(y_ref[rows, :], wo_v[...])
            yield
            xhat, rstd1 = _ln_stats(r1)
            xh_ref[rows, :] = xhat
            col = lax.broadcasted_iota(jnp.int32, (CONV_BLOCK, RS_COLS), 1)
            rs_ref[rows, :] = jnp.where(col == RS_LN1, rstd1, jnp.where(col == RS_SGU, rstd_v, jnp.where(col == RS_CONV, rstd_c, 0.0)))

        _interleave([sub_tile(b) for b in range(tm // CONV_BLOCK)])
        hb_ref[0:HALO, :] = hb_ref[tm:tm + HALO, :]

        @pl.when(step == n - 1)
        def _():
            _Gather(gathered, send_sems, recv_sems).finish()

    row = lambda w: pl.BlockSpec((tm, w), lambda i: (i, 0))
    widths = [(d, BF16), (2 * q, BF16), (d, BF16), (d, F32), (RS_COLS, F32), (q, F32), (q, F32), (q, F32), (q, F32), (q, BF16), (q, F32)]
    assert len(widths) + 1 == n_saved
    small_ins = [wst, bmat, cwf, tabs["fwd"], tabs["taps"], tabs["inv_out"], vq, vd]
    return pl.pallas_call(
        body, name="fwd_mix", grid=(n,),
        in_specs=[row(d), ANY, ANY] + [_full(a.shape) for a in small_ins] + [ANY] * 3,
        out_specs=[row(w) for w, _ in widths] + [pl.BlockSpec((tm // CONV_BLOCK * 2 * DFT_F, q), lambda i: (i, 0))] + [ANY] * 3,
        out_shape=[jax.ShapeDtypeStruct((t, w), dt) for w, dt in widths] + [jax.ShapeDtypeStruct((t // CONV_BLOCK * 2 * DFT_F, q), F32)]
        + [jax.ShapeDtypeStruct(b.shape, b.dtype) for b in mlp_w],
        scratch_shapes=[pltpu.VMEM(wi.shape, BF16), pltpu.VMEM(wo.shape, BF16), pltpu.VMEM((HALO + tm, q), F32),
                        pltpu.VMEM((2 * DFT_F, q), F32)] + _gather_sems(3),
        input_output_aliases={n_in + a: n_saved + a for a in range(3)},
        compiler_params=_cparams(),
    )(x, wi, wo, *small_ins, *mlp_w)


def _hidden_slabs(f):
    assert f % MXU_N == 0
    tiles = f // MXU_N
    sizes = [(tiles // 4 + (1 if j < tiles % 4 else 0)) * MXU_N for j in range(4)]
    return [(sum(sizes[:j]), sz) for j, sz in enumerate(sizes) if sz]


def _fwd_mlp(xh1, tgt, wg, wu, wd, vd, alpha, slabs, tm):
    t, d = xh1.shape
    n = t // tm
    ns = len(slabs)
    half = tm // 2 if tm % 32 == 0 else tm

    def body(xh_ref, tgt_ref, wg_hbm, wu_hbm, wd_hbm, vd_ref, *rest):
        gp_refs, up_refs = rest[:ns], rest[ns:2 * ns]
        x1b_ref, dr2_ref, loss_ref, dg2_ref, db2_ref, wg_v, wu_v, wd_v = rest[2 * ns:]

        @pl.when(pl.program_id(0) == 0)
        def _():
            pltpu.sync_copy(wg_hbm, wg_v)
            pltpu.sync_copy(wu_hbm, wu_v)
            pltpu.sync_copy(wd_hbm, wd_v)
            loss_ref[...] = jnp.zeros_like(loss_ref)
            dg2_ref[...] = jnp.zeros_like(dg2_ref)
            db2_ref[...] = jnp.zeros_like(db2_ref)

        g2 = vd_ref[VD_LN2_G:VD_LN2_G + 1, :]
        for r0 in range(0, tm, half):
            rows = slice(r0, r0 + half)
            x1 = xh_ref[rows, :] * vd_ref[VD_LN1_G:VD_LN1_G + 1, :] + vd_ref[VD_LN1_B:VD_LN1_B + 1, :]
            x1b = x1.astype(BF16)
            x1b_ref[rows, :] = x1b
            acc = alpha * x1
            for (off, sz), gp_ref, up_ref in zip(slabs, gp_refs, up_refs):
                gp = _dot_nt(x1b, wg_v[off:off + sz, :])
                up = _dot_nt(x1b, wu_v[off:off + sz, :])
                gp_ref[rows, :] = gp.astype(BF16)
                up_ref[rows, :] = up.astype(BF16)
                acc = acc + _dot((gp * _sigmoid(gp) * up).astype(BF16), wd_v[off:off + sz, :])
            xh2, rstd2 = _ln_stats(acc)
            err = xh2 * g2 + vd_ref[VD_LN2_B:VD_LN2_B + 1, :] - tgt_ref[rows, :]
            loss_ref[...] += _colsum(jnp.sum(err * err, axis=1, keepdims=True)) * (0.5 / d)
            dy = err * (1.0 / d)
            dg2_ref[...] += _colsum(dy * xh2)
            db2_ref[...] += _colsum(dy)
            dr2_ref[rows, :] = _ln_bwd(dy * g2, xh2, rstd2)

    row = lambda w: pl.BlockSpec((tm, w), lambda i: (i, 0))
    act = [sz for _, sz in slabs] * 2
    return pl.pallas_call(
        body, name="fwd_mlp", grid=(n,),
        in_specs=[row(d), row(d), ANY, ANY, ANY, _full(vd.shape)],
        out_specs=[row(sz) for sz in act] + [row(d), row(d), _full((8, LANES)), _full((1, d)), _full((1, d))],
        out_shape=[jax.ShapeDtypeStruct((t, sz), BF16) for sz in act]
        + [jax.ShapeDtypeStruct((t, d), BF16), jax.ShapeDtypeStruct((t, d), F32),
           jax.ShapeDtypeStruct((8, LANES), F32), jax.ShapeDtypeStruct((1, d), F32), jax.ShapeDtypeStruct((1, d), F32)],
        scratch_shapes=[pltpu.VMEM(wg.shape, BF16), pltpu.VMEM(wu.shape, BF16), pltpu.VMEM(wd.shape, BF16)],
        compiler_params=_cparams(),
    )(xh1, tgt, wg, wu, wd, vd)


def _bwd_mlp_slab(j, slab, dr2, prev, x1b, gp, up, wg, wu, wd, alpha, tm):
    t, d = dr2.shape
    off, sz = slab
    n = t // tm
    first = prev is None

    def body(*refs):
        if first:
            dr_ref, x1b_ref, gp_ref, up_ref, wg_hbm, wu_hbm, wd_hbm = refs[:7]
        else:
            dr_ref, dxp_ref, x1b_ref, gp_ref, up_ref, wg_hbm, wu_hbm, wd_hbm = refs[:8]
        dx_ref, dwg_hbm, dwu_hbm, dwd_hbm, ag, au, ad, wg_v, wu_v, wd_v = refs[-10:]

        @pl.when(pl.program_id(0) == 0)
        def _():
            for src, dst in ((wg_hbm, wg_v), (wu_hbm, wu_v), (wd_hbm, wd_v)):
                pltpu.sync_copy(src.at[pl.ds(off, sz)], dst)
            ag[...] = jnp.zeros_like(ag)
            au[...] = jnp.zeros_like(au)
            ad[...] = jnp.zeros_like(ad)

        dr = dr_ref[...]
        drb = dr.astype(BF16)
        x1b = x1b_ref[...]
        gpv = gp_ref[...].astype(F32)
        upv = up_ref[...].astype(F32)
        dh = _dot_nt(drb, wd_v[...])
        sg = _sigmoid(gpv)
        silu = gpv * sg
        ad[...] += _dot_tn((silu * upv).astype(BF16), drb)
        dgp = (dh * upv * (sg * (1.0 + gpv * (1.0 - sg)))).astype(BF16)
        dup = (dh * silu).astype(BF16)
        ag[...] += _dot_tn(dgp, x1b)
        au[...] += _dot_tn(dup, x1b)
        base = alpha * dr if first else dxp_ref[...]
        dx_ref[...] = base + _dot(dgp, wg_v[...]) + _dot(dup, wu_v[...])

        @pl.when(pl.program_id(0) == n - 1)
        def _():
            for acc, dst in ((ag, dwg_hbm), (au, dwu_hbm), (ad, dwd_hbm)):
                pltpu.sync_copy(acc, dst.at[pl.ds(off, sz)])

    row = lambda w: pl.BlockSpec((tm, w), lambda i: (i, 0))
    ins = [dr2] + ([] if first else [prev[0]]) + [x1b, gp, up, wg, wu, wd] + ([] if first else list(prev[1:]))
    in_specs = [row(d)] + ([] if first else [row(d)]) + [row(d), row(sz), row(sz), ANY, ANY, ANY] + ([] if first else [ANY] * 3)
    return pl.pallas_call(
        body, name=f"bwd_mlp_{j}", grid=(n,),
        in_specs=in_specs,
        out_specs=[row(d), ANY, ANY, ANY],
        out_shape=[jax.ShapeDtypeStruct((t, d), F32)] + [jax.ShapeDtypeStruct(wg.shape, F32)] * 3,
        scratch_shapes=[pltpu.VMEM((sz, d), F32)] * 3 + [pltpu.VMEM((sz, d), BF16)] * 3,
        input_output_aliases={} if first else {8: 1, 9: 2, 10: 3},
        compiler_params=_cparams(),
    )(*ins)


def _bwd_mix(dx1, saved, wi, wo, wstt, cwf, tabs, vq, vd, mlp_sums, alpha, tm):
    xb, pag, y, xh1, rs, zu_s, mg_s, vhat_s, gv_s, vnb_s, yhat_s, hf_s = saved
    t, d = xh1.shape
    q = wi.shape[2]
    nc, n_pairs = CONV_BLOCK // CHUNK, q // LANES
    n = t // tm
    nb = tm // CONV_BLOCK
    assert tm % CONV_BLOCK == 0

    def body(dx1_ref, xb_ref, pag_ref, y_ref, xh_ref, rs_ref, zu_ref, mg_ref, vhat_ref, gv_ref, vnb_ref, yhat_ref, hf_ref,
             wi_hbm, wo_hbm, wstt_ref, cwf_ref, fwd_ref, fwd_halo_ref, shift_ref, taps_ref, inv_ref, inv_taps_ref, vq_ref, vd_ref,
             sum_g, sum_u, sum_d,
             gx_ref, dwi_hbm, dwo_hbm, dws_ref, dbs_ref, dcw_ref, dvq_ref, dvd_ref, got_g, got_u, got_d,
             wi_v, wo_v, awi, awo, dyb_ref, later_ref, dbm_ref, gf_ref, dgf_ref, send_sems, recv_sems):
        i = pl.program_id(0)
        exchange = lambda: _Exchange((sum_g, sum_u, sum_d), (got_g, got_u, got_d), send_sems, recv_sems)

        @pl.when(i == 0)
        def _():
            exchange().start()
            pltpu.sync_copy(wi_hbm, wi_v)
            pltpu.sync_copy(wo_hbm, wo_v)
            for r in (awi, awo, dws_ref, dbm_ref, dgf_ref, dvq_ref, dvd_ref, dyb_ref, later_ref):
                r[...] = jnp.zeros_like(r)
            gf_ref[...] = _dot(taps_ref[...], _split(cwf_ref[...], True))

        dr1b_parts, dproj_parts = [None] * nb, [None] * nb

        def sub_tile(b):
            rows = slice(b * CONV_BLOCK, (b + 1) * CONV_BLOCK)
            dx1v = dx1_ref[rows, :]
            xh = xh_ref[rows, :]
            rsv = rs_ref[rows, :]
            dvd_ref[VD_LN1_G:VD_LN1_G + 1, :] += _colsum(dx1v * xh)
            dvd_ref[VD_LN1_B:VD_LN1_B + 1, :] += _colsum(dx1v)
            dr1 = _ln_bwd(dx1v * vd_ref[VD_LN1_G:VD_LN1_G + 1, :], xh, rsv[:, RS_LN1:RS_LN1 + 1])
            dr1b = dr1.astype(BF16)
            yield
            dy = _dot_nt(dr1b, wo_v[...])
            yield
            vhat = vhat_ref[rows, :]
            sgu_g = vq_ref[VQ_SGU_G:VQ_SGU_G + 1, :]
            doa = dy[:, 0:q]
            dm = doa * zu_ref[rows, :]
            dpu = (doa * mg_ref[rows, :]).astype(BF16)
            acc = dm[0:CHUNK]
            for c in range(1, nc):
                acc = acc + dm[c * CHUNK:(c + 1) * CHUNK]
            dbm_ref[...] += acc
            pa = pag_ref[rows, 0:q].astype(F32)
            sg = _sigmoid(pag_ref[rows, q:2 * q].astype(F32))
            yhat = yhat_ref[rows, :]
            cln_g = vq_ref[VQ_CLN_G:VQ_CLN_G + 1, :]
            yn = yhat * cln_g + vq_ref[VQ_CLN_B:VQ_CLN_B + 1, :]
            sy = _sigmoid(yn)
            dyn = dy[:, q:2 * q] * (sy * (1.0 + yn * (1.0 - sy)))
            dvq_ref[VQ_CLN_G:VQ_CLN_G + 1, :] += _colsum(dyn * yhat)
            dvq_ref[VQ_CLN_B:VQ_CLN_B + 1, :] += _colsum(dyn)
            dyc = _ln_bwd(dyn * cln_g, yhat, rsv[:, RS_CONV:RS_CONV + 1])
            dvq_ref[VQ_CONV_B:VQ_CONV_B + 1, :] += _colsum(dyc)
            dyb_ref[b, 0:CONV_BLOCK, :] = dyc
            yield
            wgrads = _mix_wgrad(dm, vnb_ref[rows, :], nc, n_pairs)
            dvn = _mix(wstt_ref, dm, nc, n_pairs)
            own = _dot(fwd_ref[...], _split(dyb_ref[b]))
            with_later = own + _dot(fwd_halo_ref[...], _split(later_ref[...]))
            later_ref[...] = dyb_ref[b, 0:HALO, :]
            yield
            for p, g in enumerate(wgrads):
                dws_ref[p] += g
            dvq_ref[VQ_SGU_G:VQ_SGU_G + 1, :] += _colsum(dvn * vhat)
            dvq_ref[VQ_SGU_B:VQ_SGU_B + 1, :] += _colsum(dvn)
            dpv = (_ln_bwd(dvn * sgu_g, vhat, rsv[:, RS_SGU:RS_SGU + 1]) * gv_ref[rows, :]).astype(BF16)
            dgf_ref[...] += _cmul(_cmul(own, shift_ref[...]), hf_ref[b * 2 * DFT_F:(b + 1) * 2 * DFT_F, :], conj_b=True)
            product = _split(_cmul(with_later, gf_ref[...], conj_b=True))
            yield
            dh = _dot(inv_ref[...], product)
            yield
            da = (dh * sg).astype(BF16)
            dg = (dh * pa * (sg * (1.0 - sg))).astype(BF16)
            yield
            gx = alpha * dr1
            for dpj, wj in zip((dpu, dpv, da, dg), range(4)):
                gx = gx + _dot_nt(dpj, wi_v[wj])
            gx_ref[rows, :] = gx
            dr1b_parts[b], dproj_parts[b] = dr1b, (dpu, dpv, da, dg)

        _interleave([sub_tile(b) for b in reversed(range(nb))])

        awo[...] += _dot_tn(y_ref[...], jnp.concatenate(dr1b_parts, axis=0))
        xb = xb_ref[...]
        for j in range(4):
            awi[j] += _dot_tn(xb, jnp.concatenate([part[j] for part in dproj_parts], axis=0))

        @pl.when(i == n - 1)
        def _():
            pltpu.sync_copy(awi, dwi_hbm)
            pltpu.sync_copy(awo, dwo_hbm)
            lane = lax.broadcasted_iota(jnp.int32, (CHUNK, LANES), 1)
            low = lane < HEAD_DIM
            dbs = jnp.zeros((CHUNK, LANES), F32)
            for p in range(n_pairs):
                grp = dbm_ref[:, p * LANES:(p + 1) * LANES]
                dbs = jnp.where(lane == 2 * p, jnp.sum(jnp.where(low, grp, 0.0), axis=1, keepdims=True), dbs)
                dbs = jnp.where(lane == 2 * p + 1, jnp.sum(jnp.where(low, 0.0, grp), axis=1, keepdims=True), dbs)
            dbs_ref[...] = dbs
            dcw_ref[...] = _dot(inv_taps_ref[...], _split(dgf_ref[...], True))
            exchange().finish()

    rev = lambda w: pl.BlockSpec((tm, w), lambda i: (n - 1 - i, 0))
    small = [jax.ShapeDtypeStruct((n_pairs, 2 * CHUNK, CHUNK), F32), jax.ShapeDtypeStruct((CHUNK, LANES), F32),
             jax.ShapeDtypeStruct(cwf.shape, F32), jax.ShapeDtypeStruct(vq.shape, F32), jax.ShapeDtypeStruct(vd.shape, F32)]
    small_ins = [wstt, cwf, tabs["fwd"], tabs["fwd_halo"], tabs["shift"], tabs["taps"], tabs["inv_in"], tabs["inv_taps"], vq, vd]
    return pl.pallas_call(
        body, name="bwd_mix", grid=(n,),
        in_specs=[rev(d), rev(d), rev(2 * q), rev(d), rev(d), rev(RS_COLS), rev(q), rev(q), rev(q), rev(q), rev(q), rev(q),
                  pl.BlockSpec((nb * 2 * DFT_F, q), lambda i: (n - 1 - i, 0)), ANY, ANY] + [_full(a.shape) for a in small_ins] + [ANY] * 3,
        out_specs=[rev(d), ANY, ANY] + [_full(s.shape) for s in small] + [ANY] * 3,
        out_shape=[jax.ShapeDtypeStruct((t, d), F32), jax.ShapeDtypeStruct(wi.shape, F32), jax.ShapeDtypeStruct(wo.shape, F32)] + small
        + _exchange_shapes(mlp_sums),
        scratch_shapes=[pltpu.VMEM(wi.shape, BF16), pltpu.VMEM(wo.shape, BF16), pltpu.VMEM(wi.shape, F32), pltpu.VMEM(wo.shape, F32),
                        pltpu.VMEM((nb, DFT_N, q), F32), pltpu.VMEM((HALO, q), F32),
                        pltpu.VMEM((CHUNK, q), F32), pltpu.VMEM((2 * DFT_F, q), F32), pltpu.VMEM((2 * DFT_F, q), F32)]
        + _exchange_sems(3),
        compiler_params=_cparams(),
    )(dx1, xb, pag, y, xh1, rs, zu_s, mg_s, vhat_s, gv_s, vnb_s, yhat_s, hf_s, wi, wo, *small_ins, *mlp_sums)


def _prep(me_arr, w_in, w_out, w_gate_t, w_up_t, w_down, conv_w, kwp):
    kw, cshard = conv_w.shape

    def body(me_ref, wi_ref, wo_ref, wg_ref, wu_ref, wd_ref, cw_ref, oi, oo, og, ou, od, oc):
        for src, dst in ((wi_ref, oi), (wo_ref, oo), (wg_ref, og), (wu_ref, ou), (wd_ref, od)):
            dst[...] = src[...].astype(BF16)
        oc[0:kw, :] = cw_ref[...]
        oc[kw:kwp, :] = jnp.zeros((kwp - kw, cshard), F32)

    ins = (w_in, w_out, w_gate_t, w_up_t, w_down, conv_w)
    outs = [jax.ShapeDtypeStruct((4,) + a.shape, BF16) for a in ins[:5]] + [jax.ShapeDtypeStruct((4, kwp, cshard), F32)]
    grid_spec = pltpu.PrefetchScalarGridSpec(
        num_scalar_prefetch=1, grid=(1,),
        in_specs=[pl.BlockSpec(a.shape, lambda i, me: (0, 0)) for a in ins],
        out_specs=[pl.BlockSpec((None,) + o.shape[1:], lambda i, me: (me[0], 0, 0)) for o in outs])
    return pl.pallas_call(body, name="wprep", grid_spec=grid_spec, out_shape=outs, compiler_params=_cparams())(me_arr, *ins)


def _coords():
    return tuple(lax.axis_index(a) for a in MESH_AXES)


def _other_chips(x, y):
    return [(1 - x, y), (x, 1 - y), (1 - x, 1 - y)]


def _remote(src, dst, send_sem, recv_sem, to):
    return pltpu.make_async_remote_copy(src_ref=src, dst_ref=dst, send_sem=send_sem, recv_sem=recv_sem,
                                        device_id=to, device_id_type=MESH_ID)


def _hbm_call(body, name, ins, out_shape, scratch_shapes, aliases=None):
    return pl.pallas_call(
        body, name=name, in_specs=[ANY] * len(ins), out_specs=[ANY] * len(out_shape), out_shape=out_shape,
        scratch_shapes=scratch_shapes, input_output_aliases=aliases or {},
    )(*ins)


class _Gather:
    def __init__(self, bufs, send_sems, recv_sems):
        self.bufs, self.send_sems, self.recv_sems = bufs, send_sems, recv_sems
        self.x, self.y, self.c = _coords()

    def _copies(self, stage):
        x, y, c = self.x, self.y, self.c
        for a, buf in enumerate(self.bufs):
            hr = buf.shape[1] // 2
            for j, chip in enumerate(_other_chips(x, y)):
                if stage == "ici_out":
                    ref, k, to = buf.at[2 * x + y, pl.ds(c * hr, hr)], j, (*chip, c)
                elif stage == "ici_in":
                    ref, k, to = buf.at[2 * chip[0] + chip[1], pl.ds(c * hr, hr)], j, (*chip, c)
                elif stage == "d2d_out":
                    ref, k, to = buf.at[2 * chip[0] + chip[1], pl.ds(c * hr, hr)], 3 + j, (x, y, 1 - c)
                else:
                    ref, k, to = buf.at[2 * chip[0] + chip[1], pl.ds((1 - c) * hr, hr)], 3 + j, (x, y, 1 - c)
                yield _remote(ref, ref, self.send_sems.at[a, k], self.recv_sems.at[a, k], to)

    def start(self):
        for cp in self._copies("ici_out"):
            cp.start()

    def forward(self):
        for landed, onward in zip(self._copies("ici_in"), self._copies("d2d_out")):
            landed.wait_recv()
            onward.start()

    def finish(self):
        for cp in self._copies("d2d_in"):
            cp.wait_recv()
        for stage in ("ici_out", "d2d_out"):
            for cp in self._copies(stage):
                cp.wait_send()


def _gather_sems(n):
    return [pltpu.SemaphoreType.DMA((n, 6)), pltpu.SemaphoreType.DMA((n, 6))]


def _gather_shards(bufs):
    n = len(bufs)

    def body(*refs):
        g = _Gather(refs[n:2 * n], *refs[2 * n:])
        g.start()
        g.forward()
        g.finish()

    return _hbm_call(body, "gather_shards", bufs, [jax.ShapeDtypeStruct(s.shape, s.dtype) for s in bufs],
                     _gather_sems(n), aliases={a: a for a in range(n)})


def _pair_swap(name, arrs):
    n = len(arrs)

    def body(*refs):
        src, land = refs[:n], refs[n:2 * n]
        send_sems, recv_sems = refs[2 * n:]
        x, y, c = _coords()
        copies = []
        for a in range(n):
            s = src[a].at[pl.ds(0, arrs[a].shape[0]), 1 - c] if arrs[a].ndim == 4 else src[a].at[1 - c]
            copies.append(_remote(s, land[a], send_sems.at[a], recv_sems.at[a], (x, y, 1 - c)))
            copies[-1].start()
        for cp in copies:
            cp.wait()

    outs = [jax.ShapeDtypeStruct(s.shape[:-3] + s.shape[-2:], s.dtype) for s in arrs]
    return _hbm_call(body, name, arrs, outs, [pltpu.SemaphoreType.DMA((n,)), pltpu.SemaphoreType.DMA((n,))])


class _Exchange:
    def __init__(self, src, dst, send_sems, recv_sems):
        self.src, self.dst, self.send_sems, self.recv_sems = src, dst, send_sems, recv_sems
        self.x, self.y, self.c = _coords()

    def _copies(self, incoming):
        x, y, c = self.x, self.y, self.c
        for a, (s, d) in enumerate(zip(self.src, self.dst)):
            for j, chip in enumerate(_other_chips(x, y)):
                slot = 2 * chip[0] + chip[1]
                if incoming:
                    out, into = d.at[slot], d.at[slot]
                else:
                    out, into = (s.at[slot] if len(s.shape) == 3 else s), d.at[2 * x + y]
                yield _remote(out, into, self.send_sems.at[a, j], self.recv_sems.at[a, j], (*chip, c))

    def start(self):
        for cp in self._copies(False):
            cp.start()

    def finish(self):
        for cp in self._copies(True):
            cp.wait_recv()
        for cp in self._copies(False):
            cp.wait_send()


def _exchange_sems(n):
    return [pltpu.SemaphoreType.DMA((n, 3)), pltpu.SemaphoreType.DMA((n, 3))]


def _exchange_shapes(arrs):
    return [jax.ShapeDtypeStruct((4,) + s.shape[-2:], s.dtype) for s in arrs]


def _chip_exchange(arrs):
    n = len(arrs)

    def body(*refs):
        ex = _Exchange(refs[:n], refs[n:2 * n], *refs[2 * n:])
        ex.start()
        ex.finish()

    return _hbm_call(body, "chip_exchange", arrs, _exchange_shapes(arrs), _exchange_sems(n))


def _pair_gather(halves):
    n = len(halves)

    def body(*refs):
        src, dst = refs[:n], refs[n:2 * n]
        send_sems, recv_sems = refs[2 * n:]
        x, y, c = _coords()
        copies = [_remote(src[a], dst[a], send_sems.at[a], recv_sems.at[a], (x, y, 1 - c)) for a in range(n)]
        for cp in copies:
            cp.start()
        for cp in copies:
            cp.wait()

    outs = [jax.ShapeDtypeStruct(s.shape, s.dtype) for s in halves]
    return _hbm_call(body, "pair_gather", halves, outs, [pltpu.SemaphoreType.DMA((n,)), pltpu.SemaphoreType.DMA((n,))])


def _pair_sum(a, g, land, c_arr, out_dtype):
    nq, _, hr, cc = g.shape

    def body(c_ref, g_ref, l_ref, o_ref):
        o_ref[...] = (g_ref[...] + l_ref[...]).astype(out_dtype)

    spec = pl.BlockSpec((None, hr, cc), lambda qi, cr: (qi, 0, 0))
    grid_spec = pltpu.PrefetchScalarGridSpec(
        num_scalar_prefetch=1, grid=(nq,),
        in_specs=[pl.BlockSpec((None, None, hr, cc), lambda qi, cr: (qi, cr[0], 0, 0)), spec], out_specs=spec)
    return pl.pallas_call(body, name=f"pair_sum_{a}", grid_spec=grid_spec, out_shape=jax.ShapeDtypeStruct((nq, hr, cc), out_dtype),
                          compiler_params=_cparams())(c_arr, g, land)


def _chip_sum(a, parts, own, me_arr):
    _, hr, cc = parts.shape

    def body(me_ref, p_ref, own_ref, o_ref):
        for mine in range(4):
            @pl.when(me_ref[0] == mine)
            def _():
                term = lambda j: (own_ref if j == mine else p_ref.at[j])[...].astype(F32)
                o_ref[...] = ((term(0) + term(1)) + term(2)) + term(3)

    own_spec = (pl.BlockSpec((None, hr, cc), lambda i, me: (me[0], 0, 0)) if own.ndim == 3
                else pl.BlockSpec((hr, cc), lambda i, me: (0, 0)))
    grid_spec = pltpu.PrefetchScalarGridSpec(
        num_scalar_prefetch=1, grid=(1,),
        in_specs=[pl.BlockSpec((4, hr, cc), lambda i, me: (0, 0, 0)), own_spec],
        out_specs=pl.BlockSpec((hr, cc), lambda i, me: (0, 0)))
    return pl.pallas_call(body, name=f"chip_sum_{a}", grid_spec=grid_spec, out_shape=jax.ShapeDtypeStruct((hr, cc), F32),
                          compiler_params=_cparams())(me_arr, parts, own)


def _row_block(rows, cols, limit=1 << 20):
    best = 8
    for tr in range(8, rows + 1, 8):
        if rows % tr == 0 and tr * cols * 4 <= limit:
            best = tr
    return best


def _adamw(name, w, g_mine, g_other, m, v, c_arr):
    r, c = w.shape
    hr, cg = g_mine.shape
    tr = hr if r % hr == 0 and hr * cg * 4 <= (3 << 19) else math.gcd(_row_block(hr, cg), r)
    per_half = hr // tr
    bc1 = 1.0 - ADAM_B1 ** ADAM_STEP
    bc2 = 1.0 - ADAM_B2 ** ADAM_STEP

    def body(c_ref, w_ref, gm_ref, go_ref, m_ref, v_ref, go, do, mo, vo):
        gv = jnp.where(pl.program_id(0) // per_half == c_ref[0], gm_ref[:, 0:c], go_ref[:, 0:c])
        mn = ADAM_B1 * m_ref[...] + (1.0 - ADAM_B1) * gv
        vn = ADAM_B2 * v_ref[...] + (1.0 - ADAM_B2) * (gv * gv)
        go[...] = gv
        mo[...] = mn
        vo[...] = vn
        do[...] = -ADAM_LR * ((mn / bc1) / (jnp.sqrt(vn / bc2) + ADAM_EPS) + ADAM_WD * w_ref[...])

    blk = pl.BlockSpec((tr, c), lambda i, cr: (i, 0))
    gblk = pl.BlockSpec((tr, cg), lambda i, cr: (i % per_half, 0))
    grid_spec = pltpu.PrefetchScalarGridSpec(num_scalar_prefetch=1, grid=(r // tr,), in_specs=[blk, gblk, gblk, blk, blk],
                                             out_specs=[blk] * 4)
    return pl.pallas_call(body, name=f"adamw_{name}", grid_spec=grid_spec, out_shape=[jax.ShapeDtypeStruct((r, c), F32)] * 4,
                          compiler_params=_cparams())(c_arr, w, g_mine, g_other, m, v)


def _rows128(a):
    return a.reshape(-1, LANES)


def _pad_rows(a, rows):
    return jnp.pad(a, ((0, rows - a.shape[0]), (0, 0)))


def kernel(x, w_in, sgu_ln_g, sgu_ln_b, w_s, b_s, conv_w, conv_b, conv_ln_g, conv_ln_b, w_out, ln1_g, ln1_b, w_gate, w_up, w_down, ln2_g, ln2_b, loss_target, m_w_in, m_sgu_ln_g, m_sgu_ln_b, m_w_s, m_b_s, m_conv_w, m_conv_b, m_conv_ln_g, m_conv_ln_b, m_w_out, m_ln1_g, m_ln1_b, m_w_gate, m_w_up, m_w_down, m_ln2_g, m_ln2_b, v_w_in, v_sgu_ln_g, v_sgu_ln_b, v_w_s, v_b_s, v_conv_w, v_conv_b, v_conv_ln_g, v_conv_ln_b, v_w_out, v_ln1_g, v_ln1_b, v_w_gate, v_w_up, v_w_down, v_ln2_g, v_ln2_b):
    depth, d, q = w_in.shape
    assert depth == 1 and x.shape[0] == 1
    t = x.shape[1]
    heads = w_s.shape[1]
    kw, cshard = conv_w.shape[1], conv_w.shape[2]
    fs = w_gate.shape[2]
    slabs = _hidden_slabs(4 * fs)
    n_pairs = q // LANES
    assert heads * HEAD_DIM == q and q % LANES == 0 and w_s.shape[2] == CHUNK and 4 * cshard == q and kw - 1 <= HALO
    alpha = (2.0 * depth) ** 0.25
    tm = min(512, t)
    assert t % tm == 0 and tm % CHUNK == 0
    x2, tgt = x[0], loss_target[0]
    mx, my, mc = _coords()
    me = 2 * mx + my
    c_arr = jnp.reshape(mc, (1,)).astype(jnp.int32)

    kwp = -(-kw // 16) * 16
    me_arr = jnp.reshape(me, (1,)).astype(jnp.int32)
    wi, wo, wg, wu, wd, cw4 = _prep(me_arr, w_in[0], w_out[0], w_gate[0].T, w_up[0].T, w_down[0], conv_w[0], kwp)
    wi, wo, cw4 = _gather_shards([wi, wo, cw4])
    wo = wo.reshape(d, d)
    cw = jnp.transpose(cw4, (1, 0, 2)).reshape(kwp, q)
    cwf = _pad_rows(cw[:kw][::-1], kwp)
    tabs = {name: jnp.asarray(tab) for name, tab in _dft_tables(kw, kwp, q).items()}

    wm = jnp.where(jnp.tril(jnp.ones((CHUNK, CHUNK), bool)), w_s[0], 0.0)
    wst = wm.reshape(n_pairs, 2 * CHUNK, CHUNK).astype(BF16)
    wstt = jnp.transpose(wm, (0, 2, 1)).reshape(n_pairs, 2 * CHUNK, CHUNK).astype(BF16)
    bmat = jnp.repeat(b_s[0].T, HEAD_DIM, axis=1)
    vq = _pad_rows(jnp.concatenate([sgu_ln_g, sgu_ln_b, conv_b, conv_ln_g, conv_ln_b], axis=0), 8)
    vd = _pad_rows(jnp.concatenate([ln1_g, ln1_b, ln2_g, ln2_b], axis=0), 8)

    *saved, wg, wu, wd = _fwd_mix(x2, wi, wo, wst, bmat, cwf, tabs, vq, vd, [wg, wu, wd], alpha, tm)
    wg, wu, wd = (w.reshape(4 * fs, d) for w in (wg, wu, wd))
    *acts, x1b, dr2, loss_part, dg2, db2 = _fwd_mlp(saved[3], tgt, wg, wu, wd, vd, alpha, slabs, tm)
    mlp_grads = None
    for j, slab in enumerate(slabs):
        mlp_grads = _bwd_mlp_slab(j, slab, dr2, mlp_grads, x1b, acts[j], acts[len(slabs) + j], wg, wu, wd, alpha, tm)
    dx1 = mlp_grads[0]
    mlp_halves = [b.reshape(4, 2, fs // 2, d) for b in mlp_grads[1:]]
    mlp_sums = [_pair_sum(f"mlp{a}", h, l, c_arr, BF16) for a, (h, l) in enumerate(zip(mlp_halves, _pair_swap("pair_swap_mlp", mlp_halves)))]
    grad_x, dwi, dwo, dws, dbs, dcw, dvq, dvd, *mlp_parts = _bwd_mix(dx1, saved, wi, wo, wstt, cwf, tabs, vq, vd, mlp_sums, alpha, tm)

    dws = jnp.where(jnp.tril(jnp.ones((CHUNK, CHUNK), bool)), dws.reshape(heads, CHUNK, CHUNK), 0.0)
    dvd = dvd.at[VD_LN2_G].set(dg2[0]).at[VD_LN2_B].set(db2[0])
    pieces = [_rows128(dws), dbs[:, :heads].T, _rows128(dcw), _rows128(dvq), _rows128(dvd), loss_part]
    sizes = [p.shape[0] for p in pieces]
    rows = -(-sum(sizes) // 16) * 16
    small = _pad_rows(jnp.concatenate(pieces, axis=0), rows)

    big = [dwi, dwo.reshape(4, d // 4, d)]
    halves = [b.reshape(4, 2, b.shape[1] // 2, b.shape[2]) for b in big] + [small.reshape(2, rows // 2, LANES)]
    landed = _pair_swap("pair_swap_mix", halves)
    sums = [_pair_sum(f"mix{a}", h, l, c_arr, BF16) for a, (h, l) in enumerate(zip(halves[:-1], landed[:-1]))]
    sums.append(_pair_sum("small", halves[-1][None], landed[-1][None], c_arr, F32)[0])
    parts = _chip_exchange(sums)
    parts = list(parts)
    parts, sums = parts[:2] + list(mlp_parts) + parts[2:], sums[:2] + list(mlp_sums) + sums[2:]
    mine = [_chip_sum(a, p, s, me_arr) for a, (p, s) in enumerate(zip(parts, sums))]
    other = _pair_gather(mine)

    def pack(ws, bs, vqs, vds):
        ps = [_rows128(ws[0]), bs[0], jnp.zeros((sizes[2], LANES), F32),
              _rows128(_pad_rows(jnp.concatenate(vqs, axis=0), 8)), _rows128(_pad_rows(jnp.concatenate(vds, axis=0), 8)),
              jnp.zeros((sizes[5], LANES), F32)]
        return _pad_rows(jnp.concatenate(ps, axis=0), rows)

    out = {}
    for a, (nm, w_, m_, v_) in enumerate((("w_in", w_in, m_w_in, v_w_in), ("w_out", w_out, m_w_out, v_w_out),
                                          ("w_gate", w_gate, m_w_gate, v_w_gate), ("w_up", w_up, m_w_up, v_w_up),
                                          ("w_down", w_down, m_w_down, v_w_down))):
        if nm in ("w_gate", "w_up"):
            out[nm] = [o.T for o in _adamw(nm, w_[0].T, mine[a], other[a], m_[0].T, v_[0].T, c_arr)]
        else:
            out[nm] = _adamw(nm, w_[0], mine[a], other[a], m_[0], v_[0], c_arr)
    packed = _adamw(
        "small",
        pack(w_s, b_s, [sgu_ln_g, sgu_ln_b, conv_b, conv_ln_g, conv_ln_b], [ln1_g, ln1_b, ln2_g, ln2_b]), mine[-1], other[-1],
        pack(m_w_s, m_b_s, [m_sgu_ln_g, m_sgu_ln_b, m_conv_b, m_conv_ln_g, m_conv_ln_b], [m_ln1_g, m_ln1_b, m_ln2_g, m_ln2_b]),
        pack(v_w_s, v_b_s, [v_sgu_ln_g, v_sgu_ln_b, v_conv_b, v_conv_ln_g, v_conv_ln_b], [v_ln1_g, v_ln1_b, v_ln2_g, v_ln2_b]),
        c_arr)

    offs = [sum(sizes[:i]) for i in range(len(sizes))]
    g_cw_full = packed[0][offs[2]:offs[2] + sizes[2]].reshape(kwp, q)
    g_cw = lax.dynamic_slice(g_cw_full, (0, me * cshard), (kwp, cshard))
    out["conv_w"] = _adamw("conv_w", _pad_rows(conv_w[0], kwp), g_cw, g_cw, _pad_rows(m_conv_w[0], kwp), _pad_rows(v_conv_w[0], kwp), c_arr)
    out["conv_w"] = [o[:kw] for o in out["conv_w"]]

    def unpack(p):
        vq_o = p[offs[3]:offs[3] + sizes[3]].reshape(8, q)
        vd_o = p[offs[4]:offs[4] + sizes[4]].reshape(8, d)
        return {"w_s": p[offs[0]:offs[0] + sizes[0]].reshape(heads, CHUNK, CHUNK), "b_s": p[offs[1]:offs[1] + sizes[1]],
                "sgu_ln_g": vq_o[VQ_SGU_G], "sgu_ln_b": vq_o[VQ_SGU_B], "conv_b": vq_o[VQ_CONV_B],
                "conv_ln_g": vq_o[VQ_CLN_G], "conv_ln_b": vq_o[VQ_CLN_B],
                "ln1_g": vd_o[VD_LN1_G], "ln1_b": vd_o[VD_LN1_B], "ln2_g": vd_o[VD_LN2_G], "ln2_b": vd_o[VD_LN2_B]}

    small_out = [unpack(p) for p in packed]
    loss = packed[0][offs[5], 0]
    names = ["w_in", "sgu_ln_g", "sgu_ln_b", "w_s", "b_s", "conv_w", "conv_b", "conv_ln_g", "conv_ln_b", "w_out",
             "ln1_g", "ln1_b", "w_gate", "w_up", "w_down", "ln2_g", "ln2_b"]
    result = [loss, grad_x[None]]
    for kind in range(4):
        for nm in names:
            val = out[nm][kind] if nm in out else small_out[kind][nm]
            result.append(val[None])
    return tuple(result)
```

```python
import functools
import math

import jax
import numpy as np
import jax.numpy as jnp
from jax import lax
from jax.experimental import pallas as pl
from jax.experimental.pallas import tpu as pltpu

F32 = jnp.float32
BF16 = jnp.bfloat16

LN_EPS = 1e-5
HEAD_DIM = 64
CHUNK = 128
HALO = 32
LANES = 128
MXU_N = 256
ADAM_LR, ADAM_B1, ADAM_B2, ADAM_EPS, ADAM_WD, ADAM_STEP = 0.001, 0.9, 0.999, 1e-08, 0.01, 10
VMEM_LIMIT = 63 * 1024 * 1024
MESH_AXES = ("x", "y", "c")
MESH_ID = pl.DeviceIdType.MESH


def _dot(a, b):
    return jnp.dot(a, b, preferred_element_type=F32)


def _dot_nt(a, b):
    return lax.dot_general(a, b, (((1,), (1,)), ((), ())), preferred_element_type=F32)


def _dot_tn(a, b):
    return lax.dot_general(a, b, (((0,), (0,)), ((), ())), preferred_element_type=F32)


def _sigmoid(v):
    return 1.0 / (1.0 + jnp.exp(-v))


def _gelu(v):
    cdf = 0.5 * (1.0 + lax.erf(v * (1.0 / math.sqrt(2.0))))
    pdf = jnp.exp(-0.5 * v * v) * (1.0 / math.sqrt(2.0 * math.pi))
    return v * cdf, cdf + v * pdf


def _ln_stats(v):
    mu = jnp.mean(v, axis=-1, keepdims=True)
    d = v - mu
    rstd = lax.rsqrt(jnp.mean(d * d, axis=-1, keepdims=True) + LN_EPS)
    return d * rstd, rstd


def _ln_bwd(dxhat, xhat, rstd):
    m1 = jnp.mean(dxhat, axis=-1, keepdims=True)
    m2 = jnp.mean(dxhat * xhat, axis=-1, keepdims=True)
    return rstd * (dxhat - m1 - xhat * m2)


def _colsum(v):
    return jnp.sum(v, axis=0, keepdims=True)


def _pair_lanes(v, nc, p):
    return jnp.concatenate([v[c * CHUNK:(c + 1) * CHUNK, p * LANES:(p + 1) * LANES] for c in range(nc)], axis=1)


def _unpair(parts, nc):
    rows = [jnp.concatenate([part[:, c * LANES:(c + 1) * LANES] for part in parts], axis=1) for c in range(nc)]
    return jnp.concatenate(rows, axis=0)


def _low_head(nc):
    lane = lax.broadcasted_iota(jnp.int32, (CHUNK, nc * LANES), 1)
    return (lane & (LANES - 1)) < HEAD_DIM


def _mix(wst_ref, v, nc, n_pairs):
    vb = v.astype(BF16)
    low = _low_head(nc)
    parts = []
    for p in range(n_pairs):
        r = _dot(wst_ref[p], _pair_lanes(vb, nc, p))
        parts.append(jnp.where(low, r[:CHUNK], r[CHUNK:]))
    return _unpair(parts, nc)


def _mix_wgrad(dm, vn, nc, n_pairs):
    low = _low_head(nc)
    vb = vn.astype(BF16)
    out = []
    for p in range(n_pairs):
        a = _pair_lanes(dm, nc, p)
        lhs = jnp.concatenate([jnp.where(low, a, 0.0), jnp.where(low, 0.0, a)], axis=0).astype(BF16)
        out.append(_dot_nt(lhs, _pair_lanes(vb, nc, p)))
    return out


SUBLANES = 8


CONV_BLOCK = 256
DFT_N = CONV_BLOCK + HALO
DFT_F = -(-(DFT_N // 2 + 1) // SUBLANES) * SUBLANES


def _terms(m, exact):
    hi = m.astype(np.float32).astype(BF16)
    lo = (m.astype(np.float32) - hi.astype(np.float32)).astype(BF16)
    return np.concatenate([hi, hi, lo] if exact else [hi, hi], axis=1)


def _split(v, exact=False):
    hi = v.astype(BF16)
    lo = (v - hi.astype(F32)).astype(BF16)
    return jnp.concatenate([hi, lo, hi] if exact else [hi, lo], axis=0)


def _dft_tables(kw, kwp, q):
    nf = DFT_N // 2 + 1
    ang = 2.0 * np.pi * np.arange(nf)[:, None] * np.arange(DFT_N)[None, :] / DFT_N
    fwd = np.zeros((2 * DFT_F, DFT_N))
    fwd[:nf], fwd[DFT_F:DFT_F + nf] = np.cos(ang), -np.sin(ang)
    weight = np.full((nf, 1), 2.0 / DFT_N)
    weight[0] = weight[-1] = 1.0 / DFT_N
    inv = np.zeros((DFT_N, 2 * DFT_F))
    inv[:, :nf], inv[:, DFT_F:DFT_F + nf] = (np.cos(ang) * weight).T, (-np.sin(ang) * weight).T
    inv_taps = np.zeros((kwp, 2 * DFT_F))
    inv_taps[:kw] = inv[kw - 1::-1][:kw]
    shift = np.zeros((2 * DFT_F, q), np.float32)
    shift[:nf], shift[DFT_F:DFT_F + nf] = np.cos(ang[:, HALO:HALO + 1]), -np.sin(ang[:, HALO:HALO + 1])
    return {"fwd": _terms(fwd, False), "fwd_halo": _terms(fwd[:, CONV_BLOCK:], False), "shift": shift,
            "inv_out": _terms(inv[HALO:HALO + CONV_BLOCK], False), "inv_in": _terms(inv[:CONV_BLOCK], False),
            "taps": _terms(fwd[:, :kwp], True), "inv_taps": _terms(inv_taps, True)}


def _cmul(a, b, conj_b=False):
    ar, ai, br, bi = a[:DFT_F], a[DFT_F:], b[:DFT_F], b[DFT_F:]
    if conj_b:
        return jnp.concatenate([ar * br + ai * bi, ai * br - ar * bi], axis=0)
    return jnp.concatenate([ar * br - ai * bi, ar * bi + ai * br], axis=0)


def _interleave(sub_tiles):
    waiting, live = list(sub_tiles), []
    while waiting or live:
        if waiting:
            live.append(waiting.pop(0))
        for g in list(live):
            try:
                next(g)
            except StopIteration:
                live.remove(g)


def _cparams():
    return pltpu.CompilerParams(dimension_semantics=("arbitrary",), vmem_limit_bytes=VMEM_LIMIT)


def _full(shape):
    return pl.BlockSpec(shape, lambda i: (0,) * len(shape))


ANY = pl.BlockSpec(memory_space=pl.ANY)

VQ_SGU_G, VQ_SGU_B, VQ_CONV_B, VQ_CLN_G, VQ_CLN_B = range(5)
VD_LN1_G, VD_LN1_B, VD_LN2_G, VD_LN2_B = range(4)
RS_LN1, RS_SGU, RS_CONV = range(3)
RS_COLS = 8


def _fwd_mix(x, wi, wo, wst, bmat, cwf, tabs, vq, vd, mlp_w, alpha, tm):
    t, d = x.shape
    q = wi.shape[2]
    nc, n_pairs = CONV_BLOCK // CHUNK, q // LANES
    n = t // tm
    n_in, n_saved = 11, 12
    assert tm % CONV_BLOCK == 0

    def body(x_ref, wi_hbm, wo_hbm, wst_ref, bmat_ref, cwf_ref, fwd_ref, taps_ref, inv_ref, vq_ref, vd_ref, *rest):
        (xb_ref, pag_ref, y_ref, xh_ref, rs_ref, zu_ref, mg_ref, vhat_ref, gv_ref, vnb_ref, yhat_ref, hf_ref) = rest[3:3 + n_saved]
        gathered = rest[3 + n_saved:6 + n_saved]
        wi_v, wo_v, hb_ref, gf_ref, send_sems, recv_sems = rest[6 + n_saved:]
        step = pl.program_id(0)

        @pl.when(step == 0)
        def _():
            _Gather(gathered, send_sems, recv_sems).start()
            pltpu.sync_copy(wi_hbm, wi_v)
            pltpu.sync_copy(wo_hbm, wo_v)
            hb_ref[...] = jnp.zeros_like(hb_ref)
            gf_ref[...] = _dot(taps_ref[...], _split(cwf_ref[...], True))

        @pl.when(step == (3 * n) // 4)
        def _():
            _Gather(gathered, send_sems, recv_sems).forward()

        def sub_tile(b):
            rows = slice(b * CONV_BLOCK, (b + 1) * CONV_BLOCK)
            xv = x_ref[rows, :]
            xb = xv.astype(BF16)
            xb_ref[rows, :] = xb
            pu, pv, pa, pg = (_dot(xb, wi_v[j]) for j in range(4))
            yield
            pag_ref[rows, 0:q] = pa.astype(BF16)
            pag_ref[rows, q:2 * q] = pg.astype(BF16)
            zu, gu = _gelu(pu)
            zv, gv = _gelu(pv)
            vhat, rstd_v = _ln_stats(zv)
            vnb = (vhat * vq_ref[VQ_SGU_G:VQ_SGU_G + 1, :] + vq_ref[VQ_SGU_B:VQ_SGU_B + 1, :]).astype(BF16)
            hb_ref[HALO + b * CONV_BLOCK:HALO + (b + 1) * CONV_BLOCK, :] = pa * _sigmoid(pg)
            yield
            mixed = _mix(wst_ref, vnb, nc, n_pairs) + jnp.concatenate([bmat_ref[...]] * nc, axis=0)
            spectrum = _dot(fwd_ref[...], _split(hb_ref[b * CONV_BLOCK:b * CONV_BLOCK + DFT_N, :]))
            yield
            y_ref[rows, 0:q] = (zu * mixed).astype(BF16)
            zu_ref[rows, :] = zu
            mg_ref[rows, :] = mixed * gu
            vhat_ref[rows, :] = vhat
            gv_ref[rows, :] = gv
            vnb_ref[rows, :] = vnb
            hf_ref[b * 2 * DFT_F:(b + 1) * 2 * DFT_F, :] = spectrum
            product = _split(_cmul(gf_ref[...], spectrum))
            yield
            yc = _dot(inv_ref[...], product) + vq_ref[VQ_CONV_B:VQ_CONV_B + 1, :]
            yield
            yhat, rstd_c = _ln_stats(yc)
            yhat_ref[rows, :] = yhat
            yn = yhat * vq_ref[VQ_CLN_G:VQ_CLN_G + 1, :] + vq_ref[VQ_CLN_B:VQ_CLN_B + 1, :]
            y_ref[rows, q:2 * q] = (yn * _sigmoid(yn)).astype(BF16)
            yield
            r1 = alpha * xv + _dot(y_ref[rows, :], wo_v[...])
            yield
            xhat, rstd1 = _ln_stats(r1)
            xh_ref[rows, :] = xhat
            col = lax.broadcasted_iota(jnp.int32, (CONV_BLOCK, RS_COLS), 1)
            rs_ref[rows, :] = jnp.where(col == RS_LN1, rstd1, jnp.where(col == RS_SGU, rstd_v, jnp.where(col == RS_CONV, rstd_c, 0.0)))

        _interleave([sub_tile(b) for b in range(tm // CONV_BLOCK)])
        hb_ref[0:HALO, :] = hb_ref[tm:tm + HALO, :]

        @pl.when(step == n - 1)
        def _():
            _Gather(gathered, send_sems, recv_sems).finish()

    row = lambda w: pl.BlockSpec((tm, w), lambda i: (i, 0))
    widths = [(d, BF16), (2 * q, BF16), (d, BF16), (d, F32), (RS_COLS, F32), (q, F32), (q, F32), (q, F32), (q, F32), (q, BF16), (q, F32)]
    assert len(widths) + 1 == n_saved
    small_ins = [wst, bmat, cwf, tabs["fwd"], tabs["taps"], tabs["inv_out"], vq, vd]
    return pl.pallas_call(
        body, name="fwd_mix", grid=(n,),
        in_specs=[row(d), ANY, ANY] + [_full(a.shape) for a in small_ins] + [ANY] * 3,
        out_specs=[row(w) for w, _ in widths] + [pl.BlockSpec((tm // CONV_BLOCK * 2 * DFT_F, q), lambda i: (i, 0))] + [ANY] * 3,
        out_shape=[jax.ShapeDtypeStruct((t, w), dt) for w, dt in widths] + [jax.ShapeDtypeStruct((t // CONV_BLOCK * 2 * DFT_F, q), F32)]
        + [jax.ShapeDtypeStruct(b.shape, b.dtype) for b in mlp_w],
        scratch_shapes=[pltpu.VMEM(wi.shape, BF16), pltpu.VMEM(wo.shape, BF16), pltpu.VMEM((HALO + tm, q), F32),
                        pltpu.VMEM((2 * DFT_F, q), F32)] + _gather_sems(3),
        input_output_aliases={n_in + a: n_saved + a for a in range(3)},
        compiler_params=_cparams(),
    )(x, wi, wo, *small_ins, *mlp_w)


def _hidden_slabs(f):
    assert f % MXU_N == 0
    tiles = f // MXU_N
    sizes = [(tiles // 4 + (1 if j < tiles % 4 else 0)) * MXU_N for j in range(4)]
    return [(sum(sizes[:j]), sz) for j, sz in enumerate(sizes) if sz]


def _fwd_mlp(xh1, tgt, wg, wu, wd, vd, alpha, slabs, tm):
    t, d = xh1.shape
    n = t // tm
    ns = len(slabs)
    half = tm // 2 if tm % 32 == 0 else tm

    def body(xh_ref, tgt_ref, wg_hbm, wu_hbm, wd_hbm, vd_ref, *rest):
        gp_refs, up_refs = rest[:ns], rest[ns:2 * ns]
        x1b_ref, dr2_ref, loss_ref, dg2_ref, db2_ref, wg_v, wu_v, wd_v = rest[2 * ns:]

        @pl.when(pl.program_id(0) == 0)
        def _():
            pltpu.sync_copy(wg_hbm, wg_v)
            pltpu.sync_copy(wu_hbm, wu_v)
            pltpu.sync_copy(wd_hbm, wd_v)
            loss_ref[...] = jnp.zeros_like(loss_ref)
            dg2_ref[...] = jnp.zeros_like(dg2_ref)
            db2_ref[...] = jnp.zeros_like(db2_ref)

        g2 = vd_ref[VD_LN2_G:VD_LN2_G + 1, :]
        for r0 in range(0, tm, half):
            rows = slice(r0, r0 + half)
            x1 = xh_ref[rows, :] * vd_ref[VD_LN1_G:VD_LN1_G + 1, :] + vd_ref[VD_LN1_B:VD_LN1_B + 1, :]
            x1b = x1.astype(BF16)
            x1b_ref[rows, :] = x1b
            acc = alpha * x1
            for (off, sz), gp_ref, up_ref in zip(slabs, gp_refs, up_refs):
                gp = _dot_nt(x1b, wg_v[off:off + sz, :])
                up = _dot_nt(x1b, wu_v[off:off + sz, :])
                gp_ref[rows, :] = gp.astype(BF16)
                up_ref[rows, :] = up.astype(BF16)
                acc = acc + _dot((gp * _sigmoid(gp) * up).astype(BF16), wd_v[off:off + sz, :])
            xh2, rstd2 = _ln_stats(acc)
            err = xh2 * g2 + vd_ref[VD_LN2_B:VD_LN2_B + 1, :] - tgt_ref[rows, :]
            loss_ref[...] += _colsum(jnp.sum(err * err, axis=1, keepdims=True)) * (0.5 / d)
            dy = err * (1.0 / d)
            dg2_ref[...] += _colsum(dy * xh2)
            db2_ref[...] += _colsum(dy)
            dr2_ref[rows, :] = _ln_bwd(dy * g2, xh2, rstd2)

    row = lambda w: pl.BlockSpec((tm, w), lambda i: (i, 0))
    act = [sz for _, sz in slabs] * 2
    return pl.pallas_call(
        body, name="fwd_mlp", grid=(n,),
        in_specs=[row(d), row(d), ANY, ANY, ANY, _full(vd.shape)],
        out_specs=[row(sz) for sz in act] + [row(d), row(d), _full((8, LANES)), _full((1, d)), _full((1, d))],
        out_shape=[jax.ShapeDtypeStruct((t, sz), BF16) for sz in act]
        + [jax.ShapeDtypeStruct((t, d), BF16), jax.ShapeDtypeStruct((t, d), F32),
           jax.ShapeDtypeStruct((8, LANES), F32), jax.ShapeDtypeStruct((1, d), F32), jax.ShapeDtypeStruct((1, d), F32)],
        scratch_shapes=[pltpu.VMEM(wg.shape, BF16), pltpu.VMEM(wu.shape, BF16), pltpu.VMEM(wd.shape, BF16)],
        compiler_params=_cparams(),
    )(xh1, tgt, wg, wu, wd, vd)


def _bwd_mlp_slab(j, slab, dr2, prev, x1b, gp, up, wg, wu, wd, alpha, tm):
    t, d = dr2.shape
    off, sz = slab
    n = t // tm
    first = prev is None

    def body(*refs):
        if first:
            dr_ref, x1b_ref, gp_ref, up_ref, wg_hbm, wu_hbm, wd_hbm = refs[:7]
        else:
            dr_ref, dxp_ref, x1b_ref, gp_ref, up_ref, wg_hbm, wu_hbm, wd_hbm = refs[:8]
        dx_ref, dwg_hbm, dwu_hbm, dwd_hbm, ag, au, ad, wg_v, wu_v, wd_v = refs[-10:]

        @pl.when(pl.program_id(0) == 0)
        def _():
            for src, dst in ((wg_hbm, wg_v), (wu_hbm, wu_v), (wd_hbm, wd_v)):
                pltpu.sync_copy(src.at[pl.ds(off, sz)], dst)
            ag[...] = jnp.zeros_like(ag)
            au[...] = jnp.zeros_like(au)
            ad[...] = jnp.zeros_like(ad)

        dr = dr_ref[...]
        drb = dr.astype(BF16)
        x1b = x1b_ref[...]
        gpv = gp_ref[...].astype(F32)
        upv = up_ref[...].astype(F32)
        dh = _dot_nt(drb, wd_v[...])
        sg = _sigmoid(gpv)
        silu = gpv * sg
        ad[...] += _dot_tn((silu * upv).astype(BF16), drb)
        dgp = (dh * upv * (sg * (1.0 + gpv * (1.0 - sg)))).astype(BF16)
        dup = (dh * silu).astype(BF16)
        ag[...] += _dot_tn(dgp, x1b)
        au[...] += _dot_tn(dup, x1b)
        base = alpha * dr if first else dxp_ref[...]
        dx_ref[...] = base + _dot(dgp, wg_v[...]) + _dot(dup, wu_v[...])

        @pl.when(pl.program_id(0) == n - 1)
        def _():
            for acc, dst in ((ag, dwg_hbm), (au, dwu_hbm), (ad, dwd_hbm)):
                pltpu.sync_copy(acc, dst.at[pl.ds(off, sz)])

    row = lambda w: pl.BlockSpec((tm, w), lambda i: (i, 0))
    ins = [dr2] + ([] if first else [prev[0]]) + [x1b, gp, up, wg, wu, wd] + ([] if first else list(prev[1:]))
    in_specs = [row(d)] + ([] if first else [row(d)]) + [row(d), row(sz), row(sz), ANY, ANY, ANY] + ([] if first else [ANY] * 3)
    return pl.pallas_call(
        body, name=f"bwd_mlp_{j}", grid=(n,),
        in_specs=in_specs,
        out_specs=[row(d), ANY, ANY, ANY],
        out_shape=[jax.ShapeDtypeStruct((t, d), F32)] + [jax.ShapeDtypeStruct(wg.shape, F32)] * 3,
        scratch_shapes=[pltpu.VMEM((sz, d), F32)] * 3 + [pltpu.VMEM((sz, d), BF16)] * 3,
        input_output_aliases={} if first else {8: 1, 9: 2, 10: 3},
        compiler_params=_cparams(),
    )(*ins)


def _bwd_mix(dx1, saved, wi, wo, wstt, cwf, tabs, vq, vd, token, alpha, tm):
    xb, pag, y, xh1, rs, zu_s, mg_s, vhat_s, gv_s, vnb_s, yhat_s, hf_s = saved
    t, d = xh1.shape
    q = wi.shape[2]
    nc, n_pairs = CONV_BLOCK // CHUNK, q // LANES
    n = t // tm
    nb = tm // CONV_BLOCK
    assert tm % CONV_BLOCK == 0

    def body(dx1_ref, xb_ref, pag_ref, y_ref, xh_ref, rs_ref, zu_ref, mg_ref, vhat_ref, gv_ref, vnb_ref, yhat_ref, hf_ref,
             wi_hbm, wo_hbm, wstt_ref, cwf_ref, fwd_ref, fwd_halo_ref, shift_ref, taps_ref, inv_ref, inv_taps_ref, vq_ref, vd_ref,
             token_ref,
             gx_ref, dwi_hbm, dwo_hbm, dws_ref, dbs_ref, dcw_ref, dvq_ref, dvd_ref,
             wi_v, wo_v, awi, awo, dyb_ref, later_ref, dbm_ref, gf_ref, dgf_ref):
        i = pl.program_id(0)

        @pl.when(i == 0)
        def _():
            pltpu.sync_copy(wi_hbm, wi_v)
            pltpu.sync_copy(wo_hbm, wo_v)
            for r in (awi, awo, dws_ref, dbm_ref, dgf_ref, dvq_ref, dvd_ref, dyb_ref, later_ref):
                r[...] = jnp.zeros_like(r)
            gf_ref[...] = _dot(taps_ref[...], _split(cwf_ref[...], True))

        dr1b_parts, dproj_parts = [None] * nb, [None] * nb

        def sub_tile(b):
            rows = slice(b * CONV_BLOCK, (b + 1) * CONV_BLOCK)
            dx1v = dx1_ref[rows, :]
            xh = xh_ref[rows, :]
            rsv = rs_ref[rows, :]
            dvd_ref[VD_LN1_G:VD_LN1_G + 1, :] += _colsum(dx1v * xh)
            dvd_ref[VD_LN1_B:VD_LN1_B + 1, :] += _colsum(dx1v)
            dr1 = _ln_bwd(dx1v * vd_ref[VD_LN1_G:VD_LN1_G + 1, :], xh, rsv[:, RS_LN1:RS_LN1 + 1])
            dr1b = dr1.astype(BF16)
            yield
            dy = _dot_nt(dr1b, wo_v[...])
            yield
            vhat = vhat_ref[rows, :]
            sgu_g = vq_ref[VQ_SGU_G:VQ_SGU_G + 1, :]
            doa = dy[:, 0:q]
            dm = doa * zu_ref[rows, :]
            dpu = (doa * mg_ref[rows, :]).astype(BF16)
            acc = dm[0:CHUNK]
            for c in range(1, nc):
                acc = acc + dm[c * CHUNK:(c + 1) * CHUNK]
            dbm_ref[...] += acc
            pa = pag_ref[rows, 0:q].astype(F32)
            sg = _sigmoid(pag_ref[rows, q:2 * q].astype(F32))
            yhat = yhat_ref[rows, :]
            cln_g = vq_ref[VQ_CLN_G:VQ_CLN_G + 1, :]
            yn = yhat * cln_g + vq_ref[VQ_CLN_B:VQ_CLN_B + 1, :]
            sy = _sigmoid(yn)
            dyn = dy[:, q:2 * q] * (sy * (1.0 + yn * (1.0 - sy)))
            dvq_ref[VQ_CLN_G:VQ_CLN_G + 1, :] += _colsum(dyn * yhat)
            dvq_ref[VQ_CLN_B:VQ_CLN_B + 1, :] += _colsum(dyn)
            dyc = _ln_bwd(dyn * cln_g, yhat, rsv[:, RS_CONV:RS_CONV + 1])
            dvq_ref[VQ_CONV_B:VQ_CONV_B + 1, :] += _colsum(dyc)
            dyb_ref[b, 0:CONV_BLOCK, :] = dyc
            yield
            wgrads = _mix_wgrad(dm, vnb_ref[rows, :], nc, n_pairs)
            dvn = _mix(wstt_ref, dm, nc, n_pairs)
            own = _dot(fwd_ref[...], _split(dyb_ref[b]))
            with_later = own + _dot(fwd_halo_ref[...], _split(later_ref[...]))
            later_ref[...] = dyb_ref[b, 0:HALO, :]
            yield
            for p, g in enumerate(wgrads):
                dws_ref[p] += g
            dvq_ref[VQ_SGU_G:VQ_SGU_G + 1, :] += _colsum(dvn * vhat)
            dvq_ref[VQ_SGU_B:VQ_SGU_B + 1, :] += _colsum(dvn)
            dpv = (_ln_bwd(dvn * sgu_g, vhat, rsv[:, RS_SGU:RS_SGU + 1]) * gv_ref[rows, :]).astype(BF16)
            dgf_ref[...] += _cmul(_cmul(own, shift_ref[...]), hf_ref[b * 2 * DFT_F:(b + 1) * 2 * DFT_F, :], conj_b=True)
            product = _split(_cmul(with_later, gf_ref[...], conj_b=True))
            yield
            dh = _dot(inv_ref[...], product)
            yield
            da = (dh * sg).astype(BF16)
            dg = (dh * pa * (sg * (1.0 - sg))).astype(BF16)
            yield
            gx = alpha * dr1
            for dpj, wj in zip((dpu, dpv, da, dg), range(4)):
                gx = gx + _dot_nt(dpj, wi_v[wj])
            gx_ref[rows, :] = gx
            dr1b_parts[b], dproj_parts[b] = dr1b, (dpu, dpv, da, dg)

        _interleave([sub_tile(b) for b in reversed(range(nb))])

        awo[...] += _dot_tn(y_ref[...], jnp.concatenate(dr1b_parts, axis=0))
        xb = xb_ref[...]
        for j in range(4):
            awi[j] += _dot_tn(xb, jnp.concatenate([part[j] for part in dproj_parts], axis=0))

        @pl.when(i == n - 1)
        def _():
            pltpu.sync_copy(awi, dwi_hbm)
            pltpu.sync_copy(awo, dwo_hbm)
            lane = lax.broadcasted_iota(jnp.int32, (CHUNK, LANES), 1)
            low = lane < HEAD_DIM
            dbs = jnp.zeros((CHUNK, LANES), F32)
            for p in range(n_pairs):
                grp = dbm_ref[:, p * LANES:(p + 1) * LANES]
                dbs = jnp.where(lane == 2 * p, jnp.sum(jnp.where(low, grp, 0.0), axis=1, keepdims=True), dbs)
                dbs = jnp.where(lane == 2 * p + 1, jnp.sum(jnp.where(low, 0.0, grp), axis=1, keepdims=True), dbs)
            dbs_ref[...] = dbs
            dcw_ref[...] = _dot(inv_taps_ref[...], _split(dgf_ref[...], True))

    rev = lambda w: pl.BlockSpec((tm, w), lambda i: (n - 1 - i, 0))
    small = [jax.ShapeDtypeStruct((n_pairs, 2 * CHUNK, CHUNK), F32), jax.ShapeDtypeStruct((CHUNK, LANES), F32),
             jax.ShapeDtypeStruct(cwf.shape, F32), jax.ShapeDtypeStruct(vq.shape, F32), jax.ShapeDtypeStruct(vd.shape, F32)]
    small_ins = [wstt, cwf, tabs["fwd"], tabs["fwd_halo"], tabs["shift"], tabs["taps"], tabs["inv_in"], tabs["inv_taps"], vq, vd]
    return pl.pallas_call(
        body, name="bwd_mix", grid=(n,),
        in_specs=[rev(d), rev(d), rev(2 * q), rev(d), rev(d), rev(RS_COLS), rev(q), rev(q), rev(q), rev(q), rev(q), rev(q),
                  pl.BlockSpec((nb * 2 * DFT_F, q), lambda i: (n - 1 - i, 0)), ANY, ANY] + [_full(a.shape) for a in small_ins] + [ANY],
        out_specs=[rev(d), ANY, ANY] + [_full(s.shape) for s in small],
        out_shape=[jax.ShapeDtypeStruct((t, d), F32), jax.ShapeDtypeStruct(wi.shape, F32), jax.ShapeDtypeStruct(wo.shape, F32)] + small,
        scratch_shapes=[pltpu.VMEM(wi.shape, BF16), pltpu.VMEM(wo.shape, BF16), pltpu.VMEM(wi.shape, F32), pltpu.VMEM(wo.shape, F32),
                        pltpu.VMEM((nb, DFT_N, q), F32), pltpu.VMEM((HALO, q), F32),
                        pltpu.VMEM((CHUNK, q), F32), pltpu.VMEM((2 * DFT_F, q), F32), pltpu.VMEM((2 * DFT_F, q), F32)],
        compiler_params=_cparams(),
    )(dx1, xb, pag, y, xh1, rs, zu_s, mg_s, vhat_s, gv_s, vnb_s, yhat_s, hf_s, wi, wo, *small_ins, token)


def _prep(me_arr, w_in, w_out, w_gate_t, w_up_t, w_down, conv_w, kwp):
    kw, cshard = conv_w.shape

    def body(me_ref, wi_ref, wo_ref, wg_ref, wu_ref, wd_ref, cw_ref, oi, oo, og, ou, od, oc):
        for src, dst in ((wi_ref, oi), (wo_ref, oo), (wg_ref, og), (wu_ref, ou), (wd_ref, od)):
            dst[...] = src[...].astype(BF16)
        oc[0:kw, :] = cw_ref[...]
        oc[kw:kwp, :] = jnp.zeros((kwp - kw, cshard), F32)

    ins = (w_in, w_out, w_gate_t, w_up_t, w_down, conv_w)
    outs = [jax.ShapeDtypeStruct((4,) + a.shape, BF16) for a in ins[:5]] + [jax.ShapeDtypeStruct((4, kwp, cshard), F32)]
    grid_spec = pltpu.PrefetchScalarGridSpec(
        num_scalar_prefetch=1, grid=(1,),
        in_specs=[pl.BlockSpec(a.shape, lambda i, me: (0, 0)) for a in ins],
        out_specs=[pl.BlockSpec((None,) + o.shape[1:], lambda i, me: (me[0], 0, 0)) for o in outs])
    return pl.pallas_call(body, name="wprep", grid_spec=grid_spec, out_shape=outs, compiler_params=_cparams())(me_arr, *ins)


def _coords():
    return tuple(lax.axis_index(a) for a in MESH_AXES)


def _other_chips(x, y):
    return [(1 - x, y), (x, 1 - y), (1 - x, 1 - y)]


def _remote(src, dst, send_sem, recv_sem, to):
    return pltpu.make_async_remote_copy(src_ref=src, dst_ref=dst, send_sem=send_sem, recv_sem=recv_sem,
                                        device_id=to, device_id_type=MESH_ID)


def _hbm_call(body, name, ins, out_shape, scratch_shapes, aliases=None):
    return pl.pallas_call(
        body, name=name, in_specs=[ANY] * len(ins), out_specs=[ANY] * len(out_shape), out_shape=out_shape,
        scratch_shapes=scratch_shapes, input_output_aliases=aliases or {},
    )(*ins)


class _Gather:
    def __init__(self, bufs, send_sems, recv_sems):
        self.bufs, self.send_sems, self.recv_sems = bufs, send_sems, recv_sems
        self.x, self.y, self.c = _coords()

    def _copies(self, stage):
        x, y, c = self.x, self.y, self.c
        for a, buf in enumerate(self.bufs):
            hr = buf.shape[1] // 2
            for j, chip in enumerate(_other_chips(x, y)):
                if stage == "ici_out":
                    ref, k, to = buf.at[2 * x + y, pl.ds(c * hr, hr)], j, (*chip, c)
                elif stage == "ici_in":
                    ref, k, to = buf.at[2 * chip[0] + chip[1], pl.ds(c * hr, hr)], j, (*chip, c)
                elif stage == "d2d_out":
                    ref, k, to = buf.at[2 * chip[0] + chip[1], pl.ds(c * hr, hr)], 3 + j, (x, y, 1 - c)
                else:
                    ref, k, to = buf.at[2 * chip[0] + chip[1], pl.ds((1 - c) * hr, hr)], 3 + j, (x, y, 1 - c)
                yield _remote(ref, ref, self.send_sems.at[a, k], self.recv_sems.at[a, k], to)

    def start(self):
        for cp in self._copies("ici_out"):
            cp.start()

    def forward(self):
        for landed, onward in zip(self._copies("ici_in"), self._copies("d2d_out")):
            landed.wait_recv()
            onward.start()

    def finish(self):
        for cp in self._copies("d2d_in"):
            cp.wait_recv()
        for stage in ("ici_out", "d2d_out"):
            for cp in self._copies(stage):
                cp.wait_send()


def _gather_sems(n):
    return [pltpu.SemaphoreType.DMA((n, 6)), pltpu.SemaphoreType.DMA((n, 6))]


def _gather_shards(bufs):
    n = len(bufs)

    def body(*refs):
        g = _Gather(refs[n:2 * n], *refs[2 * n:])
        g.start()
        g.forward()
        g.finish()

    return _hbm_call(body, "gather_shards", bufs, [jax.ShapeDtypeStruct(s.shape, s.dtype) for s in bufs],
                     _gather_sems(n), aliases={a: a for a in range(n)})


def _pair_swap(name, arrs):
    n = len(arrs)

    def body(*refs):
        src, land = refs[:n], refs[n:2 * n]
        send_sems, recv_sems = refs[2 * n:]
        x, y, c = _coords()
        copies = []
        for a in range(n):
            s = src[a].at[pl.ds(0, arrs[a].shape[0]), 1 - c] if arrs[a].ndim == 4 else src[a].at[1 - c]
            copies.append(_remote(s, land[a], send_sems.at[a], recv_sems.at[a], (x, y, 1 - c)))
            copies[-1].start()
        for cp in copies:
            cp.wait()

    outs = [jax.ShapeDtypeStruct(s.shape[:-3] + s.shape[-2:], s.dtype) for s in arrs]
    return _hbm_call(body, name, arrs, outs, [pltpu.SemaphoreType.DMA((n,)), pltpu.SemaphoreType.DMA((n,))])


class _Exchange:
    def __init__(self, src, dst, send_sems, recv_sems):
        self.src, self.dst, self.send_sems, self.recv_sems = src, dst, send_sems, recv_sems
        self.x, self.y, self.c = _coords()

    def _copies(self, incoming):
        x, y, c = self.x, self.y, self.c
        for a, (s, d) in enumerate(zip(self.src, self.dst)):
            for j, chip in enumerate(_other_chips(x, y)):
                slot = 2 * chip[0] + chip[1]
                if incoming:
                    out, into = d.at[slot], d.at[slot]
                else:
                    out, into = (s.at[slot] if len(s.shape) == 3 else s), d.at[2 * x + y]
                yield _remote(out, into, self.send_sems.at[a, j], self.recv_sems.at[a, j], (*chip, c))

    def start(self):
        for cp in self._copies(False):
            cp.start()

    def finish(self):
        for cp in self._copies(True):
            cp.wait_recv()
        for cp in self._copies(False):
            cp.wait_send()


def _exchange_shapes(arrs):
    return [jax.ShapeDtypeStruct((4,) + s.shape[-2:], s.dtype) for s in arrs]


class _FlatSems:
    def __init__(self, ref):
        self.ref = ref

    @property
    def at(self):
        return self

    def __getitem__(self, idx):
        return self.ref.at[3 * idx[0] + idx[1]]


HBM = pl.BlockSpec(memory_space=pltpu.HBM)
SEM = pl.BlockSpec(memory_space=pltpu.SEMAPHORE)
DATAFLOW = pltpu.SideEffectType.DATAFLOW_SIDE_EFFECTING


def _exchange_start(name, arrs):
    n = len(arrs)
    lands = _exchange_shapes(arrs)

    def body(*refs):
        src, land = refs[:n], refs[n:2 * n]
        send_sems, recv_sems = refs[2 * n:2 * n + 2]
        token = refs[-1]
        _Exchange(src, land, _FlatSems(send_sems), _FlatSems(recv_sems)).start()
        token[...] = jnp.zeros_like(token)

    hbm = lambda a: pltpu.with_memory_space_constraint(a, pltpu.HBM)
    outs = pl.pallas_call(
        body, name=name,
        out_shape=(pltpu.SemaphoreType.DMA((3 * n,)), pltpu.SemaphoreType.DMA((3 * n,)),
                   *[pltpu.HBM(a.shape, a.dtype) for a in arrs], *[pltpu.HBM(s.shape, s.dtype) for s in lands],
                   jax.ShapeDtypeStruct((SUBLANES, LANES), F32)),
        in_specs=[HBM] * (2 * n), out_specs=(SEM, SEM, *[HBM] * (2 * n), pl.BlockSpec(memory_space=pltpu.VMEM)),
        input_output_aliases={a: 2 + a for a in range(2 * n)},
        compiler_params=pltpu.CompilerParams(has_side_effects=DATAFLOW),
    )(*[hbm(a) for a in arrs], *[hbm(lax.empty(s.shape, s.dtype)) for s in lands])
    return outs[:-1], outs[-1]


def _exchange_wait(name, started, after):
    send_sems, recv_sems, *bufs = started
    n = len(bufs) // 2

    def body(*refs):
        src, land = refs[:n], refs[n:2 * n]
        send_sems, recv_sems = refs[2 * n:2 * n + 2]
        _Exchange(src, land, _FlatSems(send_sems), _FlatSems(recv_sems)).finish()

    outs = pl.pallas_call(
        body, name=name,
        out_shape=tuple(pltpu.HBM(b.shape, b.dtype) for b in bufs),
        in_specs=[HBM] * (2 * n) + [SEM, SEM, ANY], out_specs=tuple([HBM] * (2 * n)),
        input_output_aliases={a: a for a in range(2 * n)},
        compiler_params=pltpu.CompilerParams(has_side_effects=DATAFLOW),
    )(*bufs, send_sems, recv_sems, after)
    return list(outs[:n]), list(outs[n:])


def _pair_gather(name, halves):
    n = len(halves)

    def body(*refs):
        src, dst = refs[:n], refs[n:2 * n]
        send_sems, recv_sems = refs[2 * n:]
        x, y, c = _coords()
        copies = [_remote(src[a], dst[a], send_sems.at[a], recv_sems.at[a], (x, y, 1 - c)) for a in range(n)]
        for cp in copies:
            cp.start()
        for cp in copies:
            cp.wait()

    outs = [jax.ShapeDtypeStruct(s.shape, s.dtype) for s in halves]
    return _hbm_call(body, name, halves, outs, [pltpu.SemaphoreType.DMA((n,)), pltpu.SemaphoreType.DMA((n,))])


def _pair_sum(a, g, land, c_arr, out_dtype):
    nq, _, hr, cc = g.shape

    def body(c_ref, g_ref, l_ref, o_ref):
        o_ref[...] = (g_ref[...] + l_ref[...]).astype(out_dtype)

    spec = pl.BlockSpec((None, hr, cc), lambda qi, cr: (qi, 0, 0))
    grid_spec = pltpu.PrefetchScalarGridSpec(
        num_scalar_prefetch=1, grid=(nq,),
        in_specs=[pl.BlockSpec((None, None, hr, cc), lambda qi, cr: (qi, cr[0], 0, 0)), spec], out_specs=spec)
    return pl.pallas_call(body, name=f"pair_sum_{a}", grid_spec=grid_spec, out_shape=jax.ShapeDtypeStruct((nq, hr, cc), out_dtype),
                          compiler_params=_cparams())(c_arr, g, land)


def _chip_sum(a, parts, own, me_arr, after):
    _, hr, cc = parts.shape

    def body(me_ref, p_ref, own_ref, after_ref, o_ref):
        for mine in range(4):
            @pl.when(me_ref[0] == mine)
            def _():
                term = lambda j: (own_ref if j == mine else p_ref.at[j])[...].astype(F32)
                o_ref[...] = ((term(0) + term(1)) + term(2)) + term(3)

    own_spec = (pl.BlockSpec((None, hr, cc), lambda i, me: (me[0], 0, 0)) if own.ndim == 3
                else pl.BlockSpec((hr, cc), lambda i, me: (0, 0)))
    grid_spec = pltpu.PrefetchScalarGridSpec(
        num_scalar_prefetch=1, grid=(1,),
        in_specs=[pl.BlockSpec((4, hr, cc), lambda i, me: (0, 0, 0)), own_spec, ANY],
        out_specs=pl.BlockSpec((hr, cc), lambda i, me: (0, 0)))
    return pl.pallas_call(body, name=f"chip_sum_{a}", grid_spec=grid_spec, out_shape=jax.ShapeDtypeStruct((hr, cc), F32),
                          compiler_params=_cparams())(me_arr, parts, own, after)


def _row_block(rows, cols, limit=1 << 20):
    best = 8
    for tr in range(8, rows + 1, 8):
        if rows % tr == 0 and tr * cols * 4 <= limit:
            best = tr
    return best


def _adamw(name, w, g_mine, g_other, m, v, c_arr):
    r, c = w.shape
    hr, cg = g_mine.shape
    tr = hr if r % hr == 0 and hr * cg * 4 <= (3 << 19) else math.gcd(_row_block(hr, cg), r)
    per_half = hr // tr
    bc1 = 1.0 - ADAM_B1 ** ADAM_STEP
    bc2 = 1.0 - ADAM_B2 ** ADAM_STEP

    def body(c_ref, w_ref, gm_ref, go_ref, m_ref, v_ref, go, do, mo, vo):
        gv = jnp.where(pl.program_id(0) // per_half == c_ref[0], gm_ref[:, 0:c], go_ref[:, 0:c])
        mn = ADAM_B1 * m_ref[...] + (1.0 - ADAM_B1) * gv
        vn = ADAM_B2 * v_ref[...] + (1.0 - ADAM_B2) * (gv * gv)
        go[...] = gv
        mo[...] = mn
        vo[...] = vn
        do[...] = -ADAM_LR * ((mn / bc1) / (jnp.sqrt(vn / bc2) + ADAM_EPS) + ADAM_WD * w_ref[...])

    blk = pl.BlockSpec((tr, c), lambda i, cr: (i, 0))
    gblk = pl.BlockSpec((tr, cg), lambda i, cr: (i % per_half, 0))
    grid_spec = pltpu.PrefetchScalarGridSpec(num_scalar_prefetch=1, grid=(r // tr,), in_specs=[blk, gblk, gblk, blk, blk],
                                             out_specs=[blk] * 4)
    return pl.pallas_call(body, name=f"adamw_{name}", grid_spec=grid_spec, out_shape=[jax.ShapeDtypeStruct((r, c), F32)] * 4,
                          compiler_params=_cparams())(c_arr, w, g_mine, g_other, m, v)


def _rows128(a):
    return a.reshape(-1, LANES)


def _pad_rows(a, rows):
    return jnp.pad(a, ((0, rows - a.shape[0]), (0, 0)))


def kernel(x, w_in, sgu_ln_g, sgu_ln_b, w_s, b_s, conv_w, conv_b, conv_ln_g, conv_ln_b, w_out, ln1_g, ln1_b, w_gate, w_up, w_down, ln2_g, ln2_b, loss_target, m_w_in, m_sgu_ln_g, m_sgu_ln_b, m_w_s, m_b_s, m_conv_w, m_conv_b, m_conv_ln_g, m_conv_ln_b, m_w_out, m_ln1_g, m_ln1_b, m_w_gate, m_w_up, m_w_down, m_ln2_g, m_ln2_b, v_w_in, v_sgu_ln_g, v_sgu_ln_b, v_w_s, v_b_s, v_conv_w, v_conv_b, v_conv_ln_g, v_conv_ln_b, v_w_out, v_ln1_g, v_ln1_b, v_w_gate, v_w_up, v_w_down, v_ln2_g, v_ln2_b):
    depth, d, q = w_in.shape
    assert depth == 1 and x.shape[0] == 1
    t = x.shape[1]
    heads = w_s.shape[1]
    kw, cshard = conv_w.shape[1], conv_w.shape[2]
    fs = w_gate.shape[2]
    slabs = _hidden_slabs(4 * fs)
    n_pairs = q // LANES
    assert heads * HEAD_DIM == q and q % LANES == 0 and w_s.shape[2] == CHUNK and 4 * cshard == q and kw - 1 <= HALO
    alpha = (2.0 * depth) ** 0.25
    tm = min(512, t)
    assert t % tm == 0 and tm % CHUNK == 0
    x2, tgt = x[0], loss_target[0]
    mx, my, mc = _coords()
    me = 2 * mx + my
    c_arr = jnp.reshape(mc, (1,)).astype(jnp.int32)

    kwp = -(-kw // 16) * 16
    me_arr = jnp.reshape(me, (1,)).astype(jnp.int32)
    wi, wo, wg, wu, wd, cw4 = _prep(me_arr, w_in[0], w_out[0], w_gate[0].T, w_up[0].T, w_down[0], conv_w[0], kwp)
    wi, wo, cw4 = _gather_shards([wi, wo, cw4])
    wo = wo.reshape(d, d)
    cw = jnp.transpose(cw4, (1, 0, 2)).reshape(kwp, q)
    cwf = _pad_rows(cw[:kw][::-1], kwp)
    tabs = {name: jnp.asarray(tab) for name, tab in _dft_tables(kw, kwp, q).items()}

    wm = jnp.where(jnp.tril(jnp.ones((CHUNK, CHUNK), bool)), w_s[0], 0.0)
    wst = wm.reshape(n_pairs, 2 * CHUNK, CHUNK).astype(BF16)
    wstt = jnp.transpose(wm, (0, 2, 1)).reshape(n_pairs, 2 * CHUNK, CHUNK).astype(BF16)
    bmat = jnp.repeat(b_s[0].T, HEAD_DIM, axis=1)
    vq = _pad_rows(jnp.concatenate([sgu_ln_g, sgu_ln_b, conv_b, conv_ln_g, conv_ln_b], axis=0), 8)
    vd = _pad_rows(jnp.concatenate([ln1_g, ln1_b, ln2_g, ln2_b], axis=0), 8)

    *saved, wg, wu, wd = _fwd_mix(x2, wi, wo, wst, bmat, cwf, tabs, vq, vd, [wg, wu, wd], alpha, tm)
    wg, wu, wd = (w.reshape(4 * fs, d) for w in (wg, wu, wd))
    *acts, x1b, dr2, loss_part, dg2, db2 = _fwd_mlp(saved[3], tgt, wg, wu, wd, vd, alpha, slabs, tm)
    mlp_grads = None
    for j, slab in enumerate(slabs):
        mlp_grads = _bwd_mlp_slab(j, slab, dr2, mlp_grads, x1b, acts[j], acts[len(slabs) + j], wg, wu, wd, alpha, tm)
    dx1 = mlp_grads[0]
    mlp_halves = [b.reshape(4, 2, fs // 2, d) for b in mlp_grads[1:]]
    mlp_sums = [_pair_sum(f"mlp{a}", h, l, c_arr, BF16) for a, (h, l) in enumerate(zip(mlp_halves, _pair_swap("pair_swap_mlp", mlp_halves)))]
    mlp_started, token = _exchange_start("exchange_mlp_start", mlp_sums)
    grad_x, dwi, dwo, dws, dbs, dcw, dvq, dvd = _bwd_mix(dx1, saved, wi, wo, wstt, cwf, tabs, vq, vd, token, alpha, tm)
    mlp_sums, mlp_parts = _exchange_wait("exchange_mlp_wait", mlp_started, dwo)

    dws = jnp.where(jnp.tril(jnp.ones((CHUNK, CHUNK), bool)), dws.reshape(heads, CHUNK, CHUNK), 0.0)
    dvd = dvd.at[VD_LN2_G].set(dg2[0]).at[VD_LN2_B].set(db2[0])
    pieces = [_rows128(dws), dbs[:, :heads].T, _rows128(dcw), _rows128(dvq), _rows128(dvd), loss_part]
    sizes = [p.shape[0] for p in pieces]
    rows = -(-sum(sizes) // 16) * 16
    small = _pad_rows(jnp.concatenate(pieces, axis=0), rows)

    big = [dwi, dwo.reshape(4, d // 4, d)]
    halves = [b.reshape(4, 2, b.shape[1] // 2, b.shape[2]) for b in big] + [small.reshape(2, rows // 2, LANES)]
    landed = _pair_swap("pair_swap_mix", halves)
    sums = [_pair_sum(f"mix{a}", h, l, c_arr, BF16) for a, (h, l) in enumerate(zip(halves[:-1], landed[:-1]))]
    sums.append(_pair_sum("small", halves[-1][None], landed[-1][None], c_arr, F32)[0])
    mix_started, token = _exchange_start("exchange_mix_start", sums)

    out = {}

    def finish(first, names, parts, sums, after):
        mine = [_chip_sum(first + a, p, s, me_arr, after) for a, (p, s) in enumerate(zip(parts, sums))]
        other = _pair_gather(f"pair_gather_{first}", mine)
        for a, nm in enumerate(names):
            w_, m_, v_ = weights[nm]
            if nm in ("w_gate", "w_up"):
                out[nm] = [o.T for o in _adamw(nm, w_[0].T, mine[a], other[a], m_[0].T, v_[0].T, c_arr)]
            else:
                out[nm] = _adamw(nm, w_[0], mine[a], other[a], m_[0], v_[0], c_arr)
        return mine[-1], other[-1]

    weights = {"w_in": (w_in, m_w_in, v_w_in), "w_out": (w_out, m_w_out, v_w_out), "w_gate": (w_gate, m_w_gate, v_w_gate),
               "w_up": (w_up, m_w_up, v_w_up), "w_down": (w_down, m_w_down, v_w_down)}
    finish(2, ["w_gate", "w_up", "w_down"], mlp_parts, mlp_sums, token)
    sums, parts = _exchange_wait("exchange_mix_wait", mix_started, out["w_down"][1])
    small_mine, small_other = finish(5, ["w_in", "w_out"], parts, sums, parts[0])

    def pack(ws, bs, vqs, vds):
        ps = [_rows128(ws[0]), bs[0], jnp.zeros((sizes[2], LANES), F32),
              _rows128(_pad_rows(jnp.concatenate(vqs, axis=0), 8)), _rows128(_pad_rows(jnp.concatenate(vds, axis=0), 8)),
              jnp.zeros((sizes[5], LANES), F32)]
        return _pad_rows(jnp.concatenate(ps, axis=0), rows)

    packed = _adamw(
        "small",
        pack(w_s, b_s, [sgu_ln_g, sgu_ln_b, conv_b, conv_ln_g, conv_ln_b], [ln1_g, ln1_b, ln2_g, ln2_b]), small_mine, small_other,
        pack(m_w_s, m_b_s, [m_sgu_ln_g, m_sgu_ln_b, m_conv_b, m_conv_ln_g, m_conv_ln_b], [m_ln1_g, m_ln1_b, m_ln2_g, m_ln2_b]),
        pack(v_w_s, v_b_s, [v_sgu_ln_g, v_sgu_ln_b, v_conv_b, v_conv_ln_g, v_conv_ln_b], [v_ln1_g, v_ln1_b, v_ln2_g, v_ln2_b]),
        c_arr)

    offs = [sum(sizes[:i]) for i in range(len(sizes))]
    g_cw_full = packed[0][offs[2]:offs[2] + sizes[2]].reshape(kwp, q)
    g_cw = lax.dynamic_slice(g_cw_full, (0, me * cshard), (kwp, cshard))
    out["conv_w"] = _adamw("conv_w", _pad_rows(conv_w[0], kwp), g_cw, g_cw, _pad_rows(m_conv_w[0], kwp), _pad_rows(v_conv_w[0], kwp), c_arr)
    out["conv_w"] = [o[:kw] for o in out["conv_w"]]

    def unpack(p):
        vq_o = p[offs[3]:offs[3] + sizes[3]].reshape(8, q)
        vd_o = p[offs[4]:offs[4] + sizes[4]].reshape(8, d)
        return {"w_s": p[offs[0]:offs[0] + sizes[0]].reshape(heads, CHUNK, CHUNK), "b_s": p[offs[1]:offs[1] + sizes[1]],
                "sgu_ln_g": vq_o[VQ_SGU_G], "sgu_ln_b": vq_o[VQ_SGU_B], "conv_b": vq_o[VQ_CONV_B],
                "conv_ln_g": vq_o[VQ_CLN_G], "conv_ln_b": vq_o[VQ_CLN_B],
                "ln1_g": vd_o[VD_LN1_G], "ln1_b": vd_o[VD_LN1_B], "ln2_g": vd_o[VD_LN2_G], "ln2_b": vd_o[VD_LN2_B]}

    small_out = [unpack(p) for p in packed]
    loss = packed[0][offs[5], 0]
    names = ["w_in", "sgu_ln_g", "sgu_ln_b", "w_s", "b_s", "conv_w", "conv_b", "conv_ln_g", "conv_ln_b", "w_out",
             "ln1_g", "ln1_b", "w_gate", "w_up", "w_down", "ln2_g", "ln2_b"]
    result = [loss, grad_x[None]]
    for kind in range(4):
        for nm in names:
            val = out[nm][kind] if nm in out else small_out[kind][nm]
            result.append(val[None])
    return tuple(result)
```

```python
import functools
import math

import jax
import numpy as np
import jax.numpy as jnp
from jax import lax
from jax.experimental import pallas as pl
from jax.experimental.pallas import tpu as pltpu

F32 = jnp.float32
BF16 = jnp.bfloat16

LN_EPS = 1e-5
HEAD_DIM = 64
CHUNK = 128
HALO = 32
LANES = 128
MXU_N = 256
ADAM_LR, ADAM_B1, ADAM_B2, ADAM_EPS, ADAM_WD, ADAM_STEP = 0.001, 0.9, 0.999, 1e-08, 0.01, 10
VMEM_LIMIT = 63 * 1024 * 1024
MESH_AXES = ("x", "y", "c")
MESH_ID = pl.DeviceIdType.MESH


def _dot(a, b):
    return jnp.dot(a, b, preferred_element_type=F32)


def _dot_nt(a, b):
    return lax.dot_general(a, b, (((1,), (1,)), ((), ())), preferred_element_type=F32)


def _dot_tn(a, b):
    return lax.dot_general(a, b, (((0,), (0,)), ((), ())), preferred_element_type=F32)


def _sigmoid(v):
    return 1.0 / (1.0 + jnp.exp(-v))


def _gelu(v):
    cdf = 0.5 * (1.0 + lax.erf(v * (1.0 / math.sqrt(2.0))))
    pdf = jnp.exp(-0.5 * v * v) * (1.0 / math.sqrt(2.0 * math.pi))
    return v * cdf, cdf + v * pdf


def _ln_stats(v):
    mu = jnp.mean(v, axis=-1, keepdims=True)
    d = v - mu
    rstd = lax.rsqrt(jnp.mean(d * d, axis=-1, keepdims=True) + LN_EPS)
    return d * rstd, rstd


def _ln_bwd(dxhat, xhat, rstd):
    m1 = jnp.mean(dxhat, axis=-1, keepdims=True)
    m2 = jnp.mean(dxhat * xhat, axis=-1, keepdims=True)
    return rstd * (dxhat - m1 - xhat * m2)


def _colsum(v):
    return jnp.sum(v, axis=0, keepdims=True)


def _pair_lanes(v, nc, p):
    return jnp.concatenate([v[c * CHUNK:(c + 1) * CHUNK, p * LANES:(p + 1) * LANES] for c in range(nc)], axis=1)


def _unpair(parts, nc):
    rows = [jnp.concatenate([part[:, c * LANES:(c + 1) * LANES] for part in parts], axis=1) for c in range(nc)]
    return jnp.concatenate(rows, axis=0)


def _low_head(nc):
    lane = lax.broadcasted_iota(jnp.int32, (CHUNK, nc * LANES), 1)
    return (lane & (LANES - 1)) < HEAD_DIM


def _mix(wst_ref, v, nc, n_pairs):
    vb = v.astype(BF16)
    low = _low_head(nc)
    parts = []
    for p in range(n_pairs):
        r = _dot(wst_ref[p], _pair_lanes(vb, nc, p))
        parts.append(jnp.where(low, r[:CHUNK], r[CHUNK:]))
    return _unpair(parts, nc)


def _mix_wgrad(dm, vn, nc, n_pairs):
    low = _low_head(nc)
    vb = vn.astype(BF16)
    out = []
    for p in range(n_pairs):
        a = _pair_lanes(dm, nc, p)
        lhs = jnp.concatenate([jnp.where(low, a, 0.0), jnp.where(low, 0.0, a)], axis=0).astype(BF16)
        out.append(_dot_nt(lhs, _pair_lanes(vb, nc, p)))
    return out


SUBLANES = 8


CONV_BLOCK = 256
DFT_N = CONV_BLOCK + HALO
DFT_F = -(-(DFT_N // 2 + 1) // SUBLANES) * SUBLANES


def _terms(m, exact):
    hi = m.astype(np.float32).astype(BF16)
    lo = (m.astype(np.float32) - hi.astype(np.float32)).astype(BF16)
    return np.concatenate([hi, hi, lo] if exact else [hi, hi], axis=1)


def _split(v, exact=False):
    hi = v.astype(BF16)
    lo = (v - hi.astype(F32)).astype(BF16)
    return jnp.concatenate([hi, lo, hi] if exact else [hi, lo], axis=0)


def _dft_tables(kw, kwp, q):
    nf = DFT_N // 2 + 1
    ang = 2.0 * np.pi * np.arange(nf)[:, None] * np.arange(DFT_N)[None, :] / DFT_N
    fwd = np.zeros((2 * DFT_F, DFT_N))
    fwd[:nf], fwd[DFT_F:DFT_F + nf] = np.cos(ang), -np.sin(ang)
    weight = np.full((nf, 1), 2.0 / DFT_N)
    weight[0] = weight[-1] = 1.0 / DFT_N
    inv = np.zeros((DFT_N, 2 * DFT_F))
    inv[:, :nf], inv[:, DFT_F:DFT_F + nf] = (np.cos(ang) * weight).T, (-np.sin(ang) * weight).T
    inv_taps = np.zeros((kwp, 2 * DFT_F))
    inv_taps[:kw] = inv[kw - 1::-1][:kw]
    shift = np.zeros((2 * DFT_F, q), np.float32)
    shift[:nf], shift[DFT_F:DFT_F + nf] = np.cos(ang[:, HALO:HALO + 1]), -np.sin(ang[:, HALO:HALO + 1])
    return {"fwd": _terms(fwd, False), "fwd_halo": _terms(fwd[:, CONV_BLOCK:], False), "shift": shift,
            "inv_out": _terms(inv[HALO:HALO + CONV_BLOCK], False), "inv_in": _terms(inv[:CONV_BLOCK], False),
            "taps": _terms(fwd[:, :kwp], True), "inv_taps": _terms(inv_taps, True)}


def _cmul(a, b, conj_b=False):
    ar, ai, br, bi = a[:DFT_F], a[DFT_F:], b[:DFT_F], b[DFT_F:]
    if conj_b:
        return jnp.concatenate([ar * br + ai * bi, ai * br - ar * bi], axis=0)
    return jnp.concatenate([ar * br - ai * bi, ar * bi + ai * br], axis=0)


def _interleave(sub_tiles):
    waiting, live = list(sub_tiles), []
    while waiting or live:
        if waiting:
            live.append(waiting.pop(0))
        for g in list(live):
            try:
                next(g)
            except StopIteration:
                live.remove(g)


def _cparams():
    return pltpu.CompilerParams(dimension_semantics=("arbitrary",), vmem_limit_bytes=VMEM_LIMIT)


def _full(shape):
    return pl.BlockSpec(shape, lambda i: (0,) * len(shape))


ANY = pl.BlockSpec(memory_space=pl.ANY)

VQ_SGU_G, VQ_SGU_B, VQ_CONV_B, VQ_CLN_G, VQ_CLN_B = range(5)
VD_LN1_G, VD_LN1_B, VD_LN2_G, VD_LN2_B = range(4)
RS_LN1, RS_SGU, RS_CONV = range(3)
RS_COLS = 8


def _fwd_mix(x, wi, wo, wst, bmat, cwf, tabs, vq, vd, mlp_w, alpha, tm):
    t, d = x.shape
    q = wi.shape[2]
    nc, n_pairs = CONV_BLOCK // CHUNK, q // LANES
    n = t // tm
    n_in, n_saved = 11, 12
    assert tm % CONV_BLOCK == 0

    def body(x_ref, wi_hbm, wo_hbm, wst_ref, bmat_ref, cwf_ref, fwd_ref, taps_ref, inv_ref, vq_ref, vd_ref, *rest):
        (xb_ref, pag_ref, y_ref, xh_ref, rs_ref, zu_ref, mg_ref, vhat_ref, gv_ref, vnb_ref, yhat_ref, hf_ref) = rest[3:3 + n_saved]
        gathered = rest[3 + n_saved:6 + n_saved]
        wi_v, wo_v, hb_ref, gf_ref, send_sems, recv_sems = rest[6 + n_saved:]
        step = pl.program_id(0)

        @pl.when(step == 0)
        def _():
            _Gather(gathered, send_sems, recv_sems).start()
            pltpu.sync_copy(wi_hbm, wi_v)
            pltpu.sync_copy(wo_hbm, wo_v)
            hb_ref[...] = jnp.zeros_like(hb_ref)
            gf_ref[...] = _dot(taps_ref[...], _split(cwf_ref[...], True))

        @pl.when(step == (3 * n) // 4)
        def _():
            _Gather(gathered, send_sems, recv_sems).forward()

        def sub_tile(b):
            rows = slice(b * CONV_BLOCK, (b + 1) * CONV_BLOCK)
            xv = x_ref[rows, :]
            xb = xv.astype(BF16)
            xb_ref[rows, :] = xb
            pu, pv, pa, pg = (_dot(xb, wi_v[j]) for j in range(4))
            yield
            pag_ref[rows, 0:q] = pa.astype(BF16)
            pag_ref[rows, q:2 * q] = pg.astype(BF16)
            zu, gu = _gelu(pu)
            zv, gv = _gelu(pv)
            vhat, rstd_v = _ln_stats(zv)
            vnb = (vhat * vq_ref[VQ_SGU_G:VQ_SGU_G + 1, :] + vq_ref[VQ_SGU_B:VQ_SGU_B + 1, :]).astype(BF16)
            hb_ref[HALO + b * CONV_BLOCK:HALO + (b + 1) * CONV_BLOCK, :] = pa * _sigmoid(pg)
            yield
            mixed = _mix(wst_ref, vnb, nc, n_pairs) + jnp.concatenate([bmat_ref[...]] * nc, axis=0)
            spectrum = _dot(fwd_ref[...], _split(hb_ref[b * CONV_BLOCK:b * CONV_BLOCK + DFT_N, :]))
            yield
            y_ref[rows, 0:q] = (zu * mixed).astype(BF16)
            zu_ref[rows, :] = zu
            mg_ref[rows, :] = mixed * gu
            vhat_ref[rows, :] = vhat
            gv_ref[rows, :] = gv
            vnb_ref[rows, :] = vnb
            hf_ref[b * 2 * DFT_F:(b + 1) * 2 * DFT_F, :] = spectrum
            product = _split(_cmul(gf_ref[...], spectrum))
            yield
            yc = _dot(inv_ref[...], product) + vq_ref[VQ_CONV_B:VQ_CONV_B + 1, :]
            yield
            yhat, rstd_c = _ln_stats(yc)
            yhat_ref[rows, :] = yhat
            yn = yhat * vq_ref[VQ_CLN_G:VQ_CLN_G + 1, :] + vq_ref[VQ_CLN_B:VQ_CLN_B + 1, :]
            y_ref[rows, q:2 * q] = (yn * _sigmoid(yn)).astype(BF16)
            yield
            r1 = alpha * xv + _dot(y_ref[rows, :], wo_v[...])
            yield
            xhat, rstd1 = _ln_stats(r1)
            xh_ref[rows, :] = xhat
            col = lax.broadcasted_iota(jnp.int32, (CONV_BLOCK, RS_COLS), 1)
            rs_ref[rows, :] = jnp.where(col == RS_LN1, rstd1, jnp.where(col == RS_SGU, rstd_v, jnp.where(col == RS_CONV, rstd_c, 0.0)))

        _interleave([sub_tile(b) for b in range(tm // CONV_BLOCK)])
        hb_ref[0:HALO, :] = hb_ref[tm:tm + HALO, :]

        @pl.when(step == n - 1)
        def _():
            _Gather(gathered, send_sems, recv_sems).finish()

    row = lambda w: pl.BlockSpec((tm, w), lambda i: (i, 0))
    widths = [(d, BF16), (2 * q, BF16), (d, BF16), (d, F32), (RS_COLS, F32), (q, F32), (q, F32), (q, F32), (q, F32), (q, BF16), (q, F32)]
    assert len(widths) + 1 == n_saved
    small_ins = [wst, bmat, cwf, tabs["fwd"], tabs["taps"], tabs["inv_out"], vq, vd]
    return pl.pallas_call(
        body, name="fwd_mix", grid=(n,),
        in_specs=[row(d), ANY, ANY] + [_full(a.shape) for a in small_ins] + [ANY] * 3,
        out_specs=[row(w) for w, _ in widths] + [pl.BlockSpec((tm // CONV_BLOCK * 2 * DFT_F, q), lambda i: (i, 0))] + [ANY] * 3,
        out_shape=[jax.ShapeDtypeStruct((t, w), dt) for w, dt in widths] + [jax.ShapeDtypeStruct((t // CONV_BLOCK * 2 * DFT_F, q), F32)]
        + [jax.ShapeDtypeStruct(b.shape, b.dtype) for b in mlp_w],
        scratch_shapes=[pltpu.VMEM(wi.shape, BF16), pltpu.VMEM(wo.shape, BF16), pltpu.VMEM((HALO + tm, q), F32),
                        pltpu.VMEM((2 * DFT_F, q), F32)] + _gather_sems(3),
        input_output_aliases={n_in + a: n_saved + a for a in range(3)},
        compiler_params=_cparams(),
    )(x, wi, wo, *small_ins, *mlp_w)


def _hidden_slabs(f):
    assert f % MXU_N == 0
    tiles = f // MXU_N
    sizes = [(tiles // 4 + (1 if j < tiles % 4 else 0)) * MXU_N for j in range(4)]
    return [(sum(sizes[:j]), sz) for j, sz in enumerate(sizes) if sz]


def _fwd_mlp(xh1, tgt, wg, wu, wd, vd, alpha, slabs, tm):
    t, d = xh1.shape
    n = t // tm
    ns = len(slabs)
    half = tm // 2 if tm % 32 == 0 else tm

    def body(xh_ref, tgt_ref, wg_hbm, wu_hbm, wd_hbm, vd_ref, *rest):
        gp_refs, up_refs = rest[:ns], rest[ns:2 * ns]
        x1b_ref, dr2_ref, loss_ref, dg2_ref, db2_ref, wg_v, wu_v, wd_v = rest[2 * ns:]

        @pl.when(pl.program_id(0) == 0)
        def _():
            pltpu.sync_copy(wg_hbm, wg_v)
            pltpu.sync_copy(wu_hbm, wu_v)
            pltpu.sync_copy(wd_hbm, wd_v)
            loss_ref[...] = jnp.zeros_like(loss_ref)
            dg2_ref[...] = jnp.zeros_like(dg2_ref)
            db2_ref[...] = jnp.zeros_like(db2_ref)

        g2 = vd_ref[VD_LN2_G:VD_LN2_G + 1, :]
        for r0 in range(0, tm, half):
            rows = slice(r0, r0 + half)
            x1 = xh_ref[rows, :] * vd_ref[VD_LN1_G:VD_LN1_G + 1, :] + vd_ref[VD_LN1_B:VD_LN1_B + 1, :]
            x1b = x1.astype(BF16)
            x1b_ref[rows, :] = x1b
            acc = alpha * x1
            for (off, sz), gp_ref, up_ref in zip(slabs, gp_refs, up_refs):
                gp = _dot_nt(x1b, wg_v[off:off + sz, :])
                up = _dot_nt(x1b, wu_v[off:off + sz, :])
                gp_ref[rows, :] = gp.astype(BF16)
                up_ref[rows, :] = up.astype(BF16)
                acc = acc + _dot((gp * _sigmoid(gp) * up).astype(BF16), wd_v[off:off + sz, :])
            xh2, rstd2 = _ln_stats(acc)
            err = xh2 * g2 + vd_ref[VD_LN2_B:VD_LN2_B + 1, :] - tgt_ref[rows, :]
            loss_ref[...] += _colsum(jnp.sum(err * err, axis=1, keepdims=True)) * (0.5 / d)
            dy = err * (1.0 / d)
            dg2_ref[...] += _colsum(dy * xh2)
            db2_ref[...] += _colsum(dy)
            dr2_ref[rows, :] = _ln_bwd(dy * g2, xh2, rstd2)

    row = lambda w: pl.BlockSpec((tm, w), lambda i: (i, 0))
    act = [sz for _, sz in slabs] * 2
    return pl.pallas_call(
        body, name="fwd_mlp", grid=(n,),
        in_specs=[row(d), row(d), ANY, ANY, ANY, _full(vd.shape)],
        out_specs=[row(sz) for sz in act] + [row(d), row(d), _full((8, LANES)), _full((1, d)), _full((1, d))],
        out_shape=[jax.ShapeDtypeStruct((t, sz), BF16) for sz in act]
        + [jax.ShapeDtypeStruct((t, d), BF16), jax.ShapeDtypeStruct((t, d), F32),
           jax.ShapeDtypeStruct((8, LANES), F32), jax.ShapeDtypeStruct((1, d), F32), jax.ShapeDtypeStruct((1, d), F32)],
        scratch_shapes=[pltpu.VMEM(wg.shape, BF16), pltpu.VMEM(wu.shape, BF16), pltpu.VMEM(wd.shape, BF16)],
        compiler_params=_cparams(),
    )(xh1, tgt, wg, wu, wd, vd)


def _bwd_mlp_slab(j, slab, dr2, prev, x1b, gp, up, wg, wu, wd, alpha, tm):
    t, d = dr2.shape
    off, sz = slab
    n = t // tm
    first = prev is None

    def body(*refs):
        if first:
            dr_ref, x1b_ref, gp_ref, up_ref, wg_hbm, wu_hbm, wd_hbm = refs[:7]
        else:
            dr_ref, dxp_ref, x1b_ref, gp_ref, up_ref, wg_hbm, wu_hbm, wd_hbm = refs[:8]
        dx_ref, dwg_hbm, dwu_hbm, dwd_hbm, ag, au, ad, wg_v, wu_v, wd_v = refs[-10:]

        @pl.when(pl.program_id(0) == 0)
        def _():
            for src, dst in ((wg_hbm, wg_v), (wu_hbm, wu_v), (wd_hbm, wd_v)):
                pltpu.sync_copy(src.at[pl.ds(off, sz)], dst)
            ag[...] = jnp.zeros_like(ag)
            au[...] = jnp.zeros_like(au)
            ad[...] = jnp.zeros_like(ad)

        dr = dr_ref[...]
        drb = dr.astype(BF16)
        x1b = x1b_ref[...]
        gpv = gp_ref[...].astype(F32)
        upv = up_ref[...].astype(F32)
        dh = _dot_nt(drb, wd_v[...])
        sg = _sigmoid(gpv)
        silu = gpv * sg
        ad[...] += _dot_tn((silu * upv).astype(BF16), drb)
        dgp = (dh * upv * (sg * (1.0 + gpv * (1.0 - sg)))).astype(BF16)
        dup = (dh * silu).astype(BF16)
        ag[...] += _dot_tn(dgp, x1b)
        au[...] += _dot_tn(dup, x1b)
        base = alpha * dr if first else dxp_ref[...]
        dx_ref[...] = base + _dot(dgp, wg_v[...]) + _dot(dup, wu_v[...])

        @pl.when(pl.program_id(0) == n - 1)
        def _():
            for acc, dst in ((ag, dwg_hbm), (au, dwu_hbm), (ad, dwd_hbm)):
                pltpu.sync_copy(acc, dst.at[pl.ds(off, sz)])

    row = lambda w: pl.BlockSpec((tm, w), lambda i: (i, 0))
    ins = [dr2] + ([] if first else [prev[0]]) + [x1b, gp, up, wg, wu, wd] + ([] if first else list(prev[1:]))
    in_specs = [row(d)] + ([] if first else [row(d)]) + [row(d), row(sz), row(sz), ANY, ANY, ANY] + ([] if first else [ANY] * 3)
    return pl.pallas_call(
        body, name=f"bwd_mlp_{j}", grid=(n,),
        in_specs=in_specs,
        out_specs=[row(d), ANY, ANY, ANY],
        out_shape=[jax.ShapeDtypeStruct((t, d), F32)] + [jax.ShapeDtypeStruct(wg.shape, F32)] * 3,
        scratch_shapes=[pltpu.VMEM((sz, d), F32)] * 3 + [pltpu.VMEM((sz, d), BF16)] * 3,
        input_output_aliases={} if first else {8: 1, 9: 2, 10: 3},
        compiler_params=_cparams(),
    )(*ins)


def _bwd_mix(dx1, saved, wi, wo, wstt, cwf, tabs, vq, vd, token, alpha, tm):
    xb, pag, y, xh1, rs, zu_s, mg_s, vhat_s, gv_s, vnb_s, yhat_s, hf_s = saved
    t, d = xh1.shape
    q = wi.shape[2]
    nc, n_pairs = CONV_BLOCK // CHUNK, q // LANES
    n = t // tm
    nb = tm // CONV_BLOCK
    assert tm % CONV_BLOCK == 0

    def body(dx1_ref, xb_ref, pag_ref, y_ref, xh_ref, rs_ref, zu_ref, mg_ref, vhat_ref, gv_ref, vnb_ref, yhat_ref, hf_ref,
             wi_hbm, wo_hbm, wstt_ref, cwf_ref, fwd_ref, fwd_halo_ref, shift_ref, taps_ref, inv_ref, inv_taps_ref, vq_ref, vd_ref,
             token_ref,
             gx_ref, dwi_hbm, dwo_hbm, dws_ref, dbs_ref, dcw_ref, dvq_ref, dvd_ref,
             wi_v, wo_v, awi, awo, dyb_ref, later_ref, dbm_ref, gf_ref, dgf_ref):
        i = pl.program_id(0)

        @pl.when(i == 0)
        def _():
            pltpu.sync_copy(wi_hbm, wi_v)
            pltpu.sync_copy(wo_hbm, wo_v)
            for r in (awi, awo, dws_ref, dbm_ref, dgf_ref, dvq_ref, dvd_ref, dyb_ref, later_ref):
                r[...] = jnp.zeros_like(r)
            gf_ref[...] = _dot(taps_ref[...], _split(cwf_ref[...], True))

        dr1b_parts, dproj_parts = [None] * nb, [None] * nb

        def sub_tile(b):
            rows = slice(b * CONV_BLOCK, (b + 1) * CONV_BLOCK)
            dx1v = dx1_ref[rows, :]
            xh = xh_ref[rows, :]
            rsv = rs_ref[rows, :]
            dvd_ref[VD_LN1_G:VD_LN1_G + 1, :] += _colsum(dx1v * xh)
            dvd_ref[VD_LN1_B:VD_LN1_B + 1, :] += _colsum(dx1v)
            dr1 = _ln_bwd(dx1v * vd_ref[VD_LN1_G:VD_LN1_G + 1, :], xh, rsv[:, RS_LN1:RS_LN1 + 1])
            dr1b = dr1.astype(BF16)
            yield
            dy = _dot_nt(dr1b, wo_v[...])
            yield
            vhat = vhat_ref[rows, :]
            sgu_g = vq_ref[VQ_SGU_G:VQ_SGU_G + 1, :]
            doa = dy[:, 0:q]
            dm = doa * zu_ref[rows, :]
            dpu = (doa * mg_ref[rows, :]).astype(BF16)
            acc = dm[0:CHUNK]
            for c in range(1, nc):
                acc = acc + dm[c * CHUNK:(c + 1) * CHUNK]
            dbm_ref[...] += acc
            pa = pag_ref[rows, 0:q].astype(F32)
            sg = _sigmoid(pag_ref[rows, q:2 * q].astype(F32))
            yhat = yhat_ref[rows, :]
            cln_g = vq_ref[VQ_CLN_G:VQ_CLN_G + 1, :]
            yn = yhat * cln_g + vq_ref[VQ_CLN_B:VQ_CLN_B + 1, :]
            sy = _sigmoid(yn)
            dyn = dy[:, q:2 * q] * (sy * (1.0 + yn * (1.0 - sy)))
            dvq_ref[VQ_CLN_G:VQ_CLN_G + 1, :] += _colsum(dyn * yhat)
            dvq_ref[VQ_CLN_B:VQ_CLN_B + 1, :] += _colsum(dyn)
            dyc = _ln_bwd(dyn * cln_g, yhat, rsv[:, RS_CONV:RS_CONV + 1])
            dvq_ref[VQ_CONV_B:VQ_CONV_B + 1, :] += _colsum(dyc)
            dyb_ref[b, 0:CONV_BLOCK, :] = dyc
            yield
            wgrads = _mix_wgrad(dm, vnb_ref[rows, :], nc, n_pairs)
            dvn = _mix(wstt_ref, dm, nc, n_pairs)
            own = _dot(fwd_ref[...], _split(dyb_ref[b]))
            with_later = own + _dot(fwd_halo_ref[...], _split(later_ref[...]))
            later_ref[...] = dyb_ref[b, 0:HALO, :]
            yield
            for p, g in enumerate(wgrads):
                dws_ref[p] += g
            dvq_ref[VQ_SGU_G:VQ_SGU_G + 1, :] += _colsum(dvn * vhat)
            dvq_ref[VQ_SGU_B:VQ_SGU_B + 1, :] += _colsum(dvn)
            dpv = (_ln_bwd(dvn * sgu_g, vhat, rsv[:, RS_SGU:RS_SGU + 1]) * gv_ref[rows, :]).astype(BF16)
            dgf_ref[...] += _cmul(_cmul(own, shift_ref[...]), hf_ref[b * 2 * DFT_F:(b + 1) * 2 * DFT_F, :], conj_b=True)
            product = _split(_cmul(with_later, gf_ref[...], conj_b=True))
            yield
            dh = _dot(inv_ref[...], product)
            yield
            da = (dh * sg).astype(BF16)
            dg = (dh * pa * (sg * (1.0 - sg))).astype(BF16)
            yield
            gx = alpha * dr1
            for dpj, wj in zip((dpu, dpv, da, dg), range(4)):
                gx = gx + _dot_nt(dpj, wi_v[wj])
            gx_ref[rows, :] = gx
            dr1b_parts[b], dproj_parts[b] = dr1b, (dpu, dpv, da, dg)

        _interleave([sub_tile(b) for b in reversed(range(nb))])

        awo[...] += _dot_tn(y_ref[...], jnp.concatenate(dr1b_parts, axis=0))
        xb = xb_ref[...]
        for j in range(4):
            awi[j] += _dot_tn(xb, jnp.concatenate([part[j] for part in dproj_parts], axis=0))

        @pl.when(i == n - 1)
        def _():
            pltpu.sync_copy(awi, dwi_hbm)
            pltpu.sync_copy(awo, dwo_hbm)
            lane = lax.broadcasted_iota(jnp.int32, (CHUNK, LANES), 1)
            low = lane < HEAD_DIM
            dbs = jnp.zeros((CHUNK, LANES), F32)
            for p in range(n_pairs):
                grp = dbm_ref[:, p * LANES:(p + 1) * LANES]
                dbs = jnp.where(lane == 2 * p, jnp.sum(jnp.where(low, grp, 0.0), axis=1, keepdims=True), dbs)
                dbs = jnp.where(lane == 2 * p + 1, jnp.sum(jnp.where(low, 0.0, grp), axis=1, keepdims=True), dbs)
            dbs_ref[...] = dbs
            dcw_ref[...] = _dot(inv_taps_ref[...], _split(dgf_ref[...], True))

    rev = lambda w: pl.BlockSpec((tm, w), lambda i: (n - 1 - i, 0))
    small = [jax.ShapeDtypeStruct((n_pairs, 2 * CHUNK, CHUNK), F32), jax.ShapeDtypeStruct((CHUNK, LANES), F32),
             jax.ShapeDtypeStruct(cwf.shape, F32), jax.ShapeDtypeStruct(vq.shape, F32), jax.ShapeDtypeStruct(vd.shape, F32)]
    small_ins = [wstt, cwf, tabs["fwd"], tabs["fwd_halo"], tabs["shift"], tabs["taps"], tabs["inv_in"], tabs["inv_taps"], vq, vd]
    return pl.pallas_call(
        body, name="bwd_mix", grid=(n,),
        in_specs=[rev(d), rev(d), rev(2 * q), rev(d), rev(d), rev(RS_COLS), rev(q), rev(q), rev(q), rev(q), rev(q), rev(q),
                  pl.BlockSpec((nb * 2 * DFT_F, q), lambda i: (n - 1 - i, 0)), ANY, ANY] + [_full(a.shape) for a in small_ins] + [ANY],
        out_specs=[rev(d), ANY, ANY] + [_full(s.shape) for s in small],
        out_shape=[jax.ShapeDtypeStruct((t, d), F32), jax.ShapeDtypeStruct(wi.shape, F32), jax.ShapeDtypeStruct(wo.shape, F32)] + small,
        scratch_shapes=[pltpu.VMEM(wi.shape, BF16), pltpu.VMEM(wo.shape, BF16), pltpu.VMEM(wi.shape, F32), pltpu.VMEM(wo.shape, F32),
                        pltpu.VMEM((nb, DFT_N, q), F32), pltpu.VMEM((HALO, q), F32),
                        pltpu.VMEM((CHUNK, q), F32), pltpu.VMEM((2 * DFT_F, q), F32), pltpu.VMEM((2 * DFT_F, q), F32)],
        compiler_params=_cparams(),
    )(dx1, xb, pag, y, xh1, rs, zu_s, mg_s, vhat_s, gv_s, vnb_s, yhat_s, hf_s, wi, wo, *small_ins, token)


def _prep(me_arr, w_in, w_out, w_gate_t, w_up_t, w_down, conv_w, kwp):
    kw, cshard = conv_w.shape

    def body(me_ref, wi_ref, wo_ref, wg_ref, wu_ref, wd_ref, cw_ref, oi, oo, og, ou, od, oc):
        for src, dst in ((wi_ref, oi), (wo_ref, oo), (wg_ref, og), (wu_ref, ou), (wd_ref, od)):
            dst[...] = src[...].astype(BF16)
        oc[0:kw, :] = cw_ref[...]
        oc[kw:kwp, :] = jnp.zeros((kwp - kw, cshard), F32)

    ins = (w_in, w_out, w_gate_t, w_up_t, w_down, conv_w)
    outs = [jax.ShapeDtypeStruct((4,) + a.shape, BF16) for a in ins[:5]] + [jax.ShapeDtypeStruct((4, kwp, cshard), F32)]
    grid_spec = pltpu.PrefetchScalarGridSpec(
        num_scalar_prefetch=1, grid=(1,),
        in_specs=[pl.BlockSpec(a.shape, lambda i, me: (0, 0)) for a in ins],
        out_specs=[pl.BlockSpec((None,) + o.shape[1:], lambda i, me: (me[0], 0, 0)) for o in outs])
    return pl.pallas_call(body, name="wprep", grid_spec=grid_spec, out_shape=outs, compiler_params=_cparams())(me_arr, *ins)


def _coords():
    return tuple(lax.axis_index(a) for a in MESH_AXES)


def _other_chips(x, y):
    return [(1 - x, y), (x, 1 - y), (1 - x, 1 - y)]


def _remote(src, dst, send_sem, recv_sem, to):
    return pltpu.make_async_remote_copy(src_ref=src, dst_ref=dst, send_sem=send_sem, recv_sem=recv_sem,
                                        device_id=to, device_id_type=MESH_ID)


def _hbm_call(body, name, ins, out_shape, scratch_shapes, aliases=None):
    return pl.pallas_call(
        body, name=name, in_specs=[ANY] * len(ins), out_specs=[ANY] * len(out_shape), out_shape=out_shape,
        scratch_shapes=scratch_shapes, input_output_aliases=aliases or {},
    )(*ins)


class _Gather:
    def __init__(self, bufs, send_sems, recv_sems):
        self.bufs, self.send_sems, self.recv_sems = bufs, send_sems, recv_sems
        self.x, self.y, self.c = _coords()

    def _copies(self, stage):
        x, y, c = self.x, self.y, self.c
        for a, buf in enumerate(self.bufs):
            hr = buf.shape[1] // 2
            for j, chip in enumerate(_other_chips(x, y)):
                if stage == "ici_out":
                    ref, k, to = buf.at[2 * x + y, pl.ds(c * hr, hr)], j, (*chip, c)
                elif stage == "ici_in":
                    ref, k, to = buf.at[2 * chip[0] + chip[1], pl.ds(c * hr, hr)], j, (*chip, c)
                elif stage == "d2d_out":
                    ref, k, to = buf.at[2 * chip[0] + chip[1], pl.ds(c * hr, hr)], 3 + j, (x, y, 1 - c)
                else:
                    ref, k, to = buf.at[2 * chip[0] + chip[1], pl.ds((1 - c) * hr, hr)], 3 + j, (x, y, 1 - c)
                yield _remote(ref, ref, self.send_sems.at[a, k], self.recv_sems.at[a, k], to)

    def start(self):
        for cp in self._copies("ici_out"):
            cp.start()

    def forward(self):
        for landed, onward in zip(self._copies("ici_in"), self._copies("d2d_out")):
            landed.wait_recv()
            onward.start()

    def finish(self):
        for cp in self._copies("d2d_in"):
            cp.wait_recv()
        for stage in ("ici_out", "d2d_out"):
            for cp in self._copies(stage):
                cp.wait_send()


def _gather_sems(n):
    return [pltpu.SemaphoreType.DMA((n, 6)), pltpu.SemaphoreType.DMA((n, 6))]


def _gather_shards(bufs):
    n = len(bufs)

    def body(*refs):
        g = _Gather(refs[n:2 * n], *refs[2 * n:])
        g.start()
        g.forward()
        g.finish()

    return _hbm_call(body, "gather_shards", bufs, [jax.ShapeDtypeStruct(s.shape, s.dtype) for s in bufs],
                     _gather_sems(n), aliases={a: a for a in range(n)})


def _pair_swap(name, arrs):
    n = len(arrs)

    def body(*refs):
        src, land = refs[:n], refs[n:2 * n]
        send_sems, recv_sems = refs[2 * n:]
        x, y, c = _coords()
        copies = []
        for a in range(n):
            s = src[a].at[pl.ds(0, arrs[a].shape[0]), 1 - c] if arrs[a].ndim == 4 else src[a].at[1 - c]
            copies.append(_remote(s, land[a], send_sems.at[a], recv_sems.at[a], (x, y, 1 - c)))
            copies[-1].start()
        for cp in copies:
            cp.wait()

    outs = [jax.ShapeDtypeStruct(s.shape[:-3] + s.shape[-2:], s.dtype) for s in arrs]
    return _hbm_call(body, name, arrs, outs, [pltpu.SemaphoreType.DMA((n,)), pltpu.SemaphoreType.DMA((n,))])


class _Exchange:
    def __init__(self, src, dst, send_sems, recv_sems):
        self.src, self.dst, self.send_sems, self.recv_sems = src, dst, send_sems, recv_sems
        self.x, self.y, self.c = _coords()

    def _copies(self, incoming):
        x, y, c = self.x, self.y, self.c
        for a, (s, d) in enumerate(zip(self.src, self.dst)):
            for j, chip in enumerate(_other_chips(x, y)):
                slot = 2 * chip[0] + chip[1]
                if incoming:
                    out, into = d.at[slot], d.at[slot]
                else:
                    out, into = (s.at[slot] if len(s.shape) == 3 else s), d.at[2 * x + y]
                yield _remote(out, into, self.send_sems.at[a, j], self.recv_sems.at[a, j], (*chip, c))

    def start(self):
        for cp in self._copies(False):
            cp.start()

    def finish(self):
        for cp in self._copies(True):
            cp.wait_recv()
        for cp in self._copies(False):
            cp.wait_send()


def _exchange_shapes(arrs):
    return [jax.ShapeDtypeStruct((4,) + s.shape[-2:], s.dtype) for s in arrs]


class _FlatSems:
    def __init__(self, ref):
        self.ref = ref

    @property
    def at(self):
        return self

    def __getitem__(self, idx):
        return self.ref.at[3 * idx[0] + idx[1]]


HBM = pl.BlockSpec(memory_space=pltpu.HBM)
SEM = pl.BlockSpec(memory_space=pltpu.SEMAPHORE)
DATAFLOW = pltpu.SideEffectType.DATAFLOW_SIDE_EFFECTING


def _exchange_start(name, arrs):
    n = len(arrs)
    lands = _exchange_shapes(arrs)

    def body(*refs):
        src, land = refs[:n], refs[n:2 * n]
        send_sems, recv_sems = refs[2 * n:2 * n + 2]
        token = refs[-1]
        _Exchange(src, land, _FlatSems(send_sems), _FlatSems(recv_sems)).start()
        token[...] = jnp.zeros_like(token)

    hbm = lambda a: pltpu.with_memory_space_constraint(a, pltpu.HBM)
    outs = pl.pallas_call(
        body, name=name,
        out_shape=(pltpu.SemaphoreType.DMA((3 * n,)), pltpu.SemaphoreType.DMA((3 * n,)),
                   *[pltpu.HBM(a.shape, a.dtype) for a in arrs], *[pltpu.HBM(s.shape, s.dtype) for s in lands],
                   jax.ShapeDtypeStruct((SUBLANES, LANES), F32)),
        in_specs=[HBM] * (2 * n), out_specs=(SEM, SEM, *[HBM] * (2 * n), pl.BlockSpec(memory_space=pltpu.VMEM)),
        input_output_aliases={a: 2 + a for a in range(2 * n)},
        compiler_params=pltpu.CompilerParams(has_side_effects=DATAFLOW),
    )(*[hbm(a) for a in arrs], *[hbm(lax.empty(s.shape, s.dtype)) for s in lands])
    return outs[:-1], outs[-1]


def _exchange_wait(name, started, after):
    send_sems, recv_sems, *bufs = started
    n = len(bufs) // 2

    def body(*refs):
        src, land = refs[:n], refs[n:2 * n]
        send_sems, recv_sems = refs[2 * n:2 * n + 2]
        _Exchange(src, land, _FlatSems(send_sems), _FlatSems(recv_sems)).finish()

    outs = pl.pallas_call(
        body, name=name,
        out_shape=tuple(pltpu.HBM(b.shape, b.dtype) for b in bufs),
        in_specs=[HBM] * (2 * n) + [SEM, SEM] + [ANY] * len(after), out_specs=tuple([HBM] * (2 * n)),
        input_output_aliases={a: a for a in range(2 * n)},
        compiler_params=pltpu.CompilerParams(has_side_effects=DATAFLOW),
    )(*bufs, send_sems, recv_sems, *after)
    return list(outs[:n]), list(outs[n:])


def _pair_gather(name, halves):
    n = len(halves)

    def body(*refs):
        src, dst = refs[:n], refs[n:2 * n]
        send_sems, recv_sems = refs[2 * n:]
        x, y, c = _coords()
        copies = [_remote(src[a], dst[a], send_sems.at[a], recv_sems.at[a], (x, y, 1 - c)) for a in range(n)]
        for cp in copies:
            cp.start()
        for cp in copies:
            cp.wait()

    outs = [jax.ShapeDtypeStruct(s.shape, s.dtype) for s in halves]
    return _hbm_call(body, name, halves, outs, [pltpu.SemaphoreType.DMA((n,)), pltpu.SemaphoreType.DMA((n,))])


def _pair_sum(a, g, land, c_arr, out_dtype):
    nq, _, hr, cc = g.shape

    def body(c_ref, g_ref, l_ref, o_ref):
        o_ref[...] = (g_ref[...] + l_ref[...]).astype(out_dtype)

    spec = pl.BlockSpec((None, hr, cc), lambda qi, cr: (qi, 0, 0))
    grid_spec = pltpu.PrefetchScalarGridSpec(
        num_scalar_prefetch=1, grid=(nq,),
        in_specs=[pl.BlockSpec((None, None, hr, cc), lambda qi, cr: (qi, cr[0], 0, 0)), spec], out_specs=spec)
    return pl.pallas_call(body, name=f"pair_sum_{a}", grid_spec=grid_spec, out_shape=jax.ShapeDtypeStruct((nq, hr, cc), out_dtype),
                          compiler_params=_cparams())(c_arr, g, land)


def _chip_sum(a, parts, own, me_arr, after):
    _, hr, cc = parts.shape

    def body(me_ref, p_ref, own_ref, after_ref, o_ref):
        for mine in range(4):
            @pl.when(me_ref[0] == mine)
            def _():
                term = lambda j: (own_ref if j == mine else p_ref.at[j])[...].astype(F32)
                o_ref[...] = ((term(0) + term(1)) + term(2)) + term(3)

    own_spec = (pl.BlockSpec((None, hr, cc), lambda i, me: (me[0], 0, 0)) if own.ndim == 3
                else pl.BlockSpec((hr, cc), lambda i, me: (0, 0)))
    grid_spec = pltpu.PrefetchScalarGridSpec(
        num_scalar_prefetch=1, grid=(1,),
        in_specs=[pl.BlockSpec((4, hr, cc), lambda i, me: (0, 0, 0)), own_spec, ANY],
        out_specs=pl.BlockSpec((hr, cc), lambda i, me: (0, 0)))
    return pl.pallas_call(body, name=f"chip_sum_{a}", grid_spec=grid_spec, out_shape=jax.ShapeDtypeStruct((hr, cc), F32),
                          compiler_params=_cparams())(me_arr, parts, own, after)


def _row_block(rows, cols, limit=1 << 20):
    best = 8
    for tr in range(8, rows + 1, 8):
        if rows % tr == 0 and tr * cols * 4 <= limit:
            best = tr
    return best


def _adamw(name, w, g_mine, g_other, m, v, c_arr):
    r, c = w.shape
    hr, cg = g_mine.shape
    tr = hr if r % hr == 0 and hr * cg * 4 <= (3 << 19) else math.gcd(_row_block(hr, cg), r)
    per_half = hr // tr
    bc1 = 1.0 - ADAM_B1 ** ADAM_STEP
    bc2 = 1.0 - ADAM_B2 ** ADAM_STEP

    def body(c_ref, w_ref, gm_ref, go_ref, m_ref, v_ref, go, do, mo, vo):
        gv = jnp.where(pl.program_id(0) // per_half == c_ref[0], gm_ref[:, 0:c], go_ref[:, 0:c])
        mn = ADAM_B1 * m_ref[...] + (1.0 - ADAM_B1) * gv
        vn = ADAM_B2 * v_ref[...] + (1.0 - ADAM_B2) * (gv * gv)
        go[...] = gv
        mo[...] = mn
        vo[...] = vn
        do[...] = -ADAM_LR * ((mn / bc1) / (jnp.sqrt(vn / bc2) + ADAM_EPS) + ADAM_WD * w_ref[...])

    blk = pl.BlockSpec((tr, c), lambda i, cr: (i, 0))
    gblk = pl.BlockSpec((tr, cg), lambda i, cr: (i % per_half, 0))
    grid_spec = pltpu.PrefetchScalarGridSpec(num_scalar_prefetch=1, grid=(r // tr,), in_specs=[blk, gblk, gblk, blk, blk],
                                             out_specs=[blk] * 4)
    return pl.pallas_call(body, name=f"adamw_{name}", grid_spec=grid_spec, out_shape=[jax.ShapeDtypeStruct((r, c), F32)] * 4,
                          compiler_params=_cparams())(c_arr, w, g_mine, g_other, m, v)


def _rows128(a):
    return a.reshape(-1, LANES)


def _pad_rows(a, rows):
    return jnp.pad(a, ((0, rows - a.shape[0]), (0, 0)))


def kernel(x, w_in, sgu_ln_g, sgu_ln_b, w_s, b_s, conv_w, conv_b, conv_ln_g, conv_ln_b, w_out, ln1_g, ln1_b, w_gate, w_up, w_down, ln2_g, ln2_b, loss_target, m_w_in, m_sgu_ln_g, m_sgu_ln_b, m_w_s, m_b_s, m_conv_w, m_conv_b, m_conv_ln_g, m_conv_ln_b, m_w_out, m_ln1_g, m_ln1_b, m_w_gate, m_w_up, m_w_down, m_ln2_g, m_ln2_b, v_w_in, v_sgu_ln_g, v_sgu_ln_b, v_w_s, v_b_s, v_conv_w, v_conv_b, v_conv_ln_g, v_conv_ln_b, v_w_out, v_ln1_g, v_ln1_b, v_w_gate, v_w_up, v_w_down, v_ln2_g, v_ln2_b):
    depth, d, q = w_in.shape
    assert depth == 1 and x.shape[0] == 1
    t = x.shape[1]
    heads = w_s.shape[1]
    kw, cshard = conv_w.shape[1], conv_w.shape[2]
    fs = w_gate.shape[2]
    slabs = _hidden_slabs(4 * fs)
    n_pairs = q // LANES
    assert heads * HEAD_DIM == q and q % LANES == 0 and w_s.shape[2] == CHUNK and 4 * cshard == q and kw - 1 <= HALO
    alpha = (2.0 * depth) ** 0.25
    tm = min(512, t)
    assert t % tm == 0 and tm % CHUNK == 0
    x2, tgt = x[0], loss_target[0]
    mx, my, mc = _coords()
    me = 2 * mx + my
    c_arr = jnp.reshape(mc, (1,)).astype(jnp.int32)

    kwp = -(-kw // 16) * 16
    me_arr = jnp.reshape(me, (1,)).astype(jnp.int32)
    wi, wo, wg, wu, wd, cw4 = _prep(me_arr, w_in[0], w_out[0], w_gate[0].T, w_up[0].T, w_down[0], conv_w[0], kwp)
    wi, wo, cw4 = _gather_shards([wi, wo, cw4])
    wo = wo.reshape(d, d)
    cw = jnp.transpose(cw4, (1, 0, 2)).reshape(kwp, q)
    cwf = _pad_rows(cw[:kw][::-1], kwp)
    tabs = {name: jnp.asarray(tab) for name, tab in _dft_tables(kw, kwp, q).items()}

    wm = jnp.where(jnp.tril(jnp.ones((CHUNK, CHUNK), bool)), w_s[0], 0.0)
    wst = wm.reshape(n_pairs, 2 * CHUNK, CHUNK).astype(BF16)
    wstt = jnp.transpose(wm, (0, 2, 1)).reshape(n_pairs, 2 * CHUNK, CHUNK).astype(BF16)
    bmat = jnp.repeat(b_s[0].T, HEAD_DIM, axis=1)
    vq = _pad_rows(jnp.concatenate([sgu_ln_g, sgu_ln_b, conv_b, conv_ln_g, conv_ln_b], axis=0), 8)
    vd = _pad_rows(jnp.concatenate([ln1_g, ln1_b, ln2_g, ln2_b], axis=0), 8)

    *saved, wg, wu, wd = _fwd_mix(x2, wi, wo, wst, bmat, cwf, tabs, vq, vd, [wg, wu, wd], alpha, tm)
    wg, wu, wd = (w.reshape(4 * fs, d) for w in (wg, wu, wd))
    *acts, x1b, dr2, loss_part, dg2, db2 = _fwd_mlp(saved[3], tgt, wg, wu, wd, vd, alpha, slabs, tm)
    mlp_grads = None
    for j, slab in enumerate(slabs):
        mlp_grads = _bwd_mlp_slab(j, slab, dr2, mlp_grads, x1b, acts[j], acts[len(slabs) + j], wg, wu, wd, alpha, min(t, 2 * tm))
    dx1 = mlp_grads[0]
    mlp_halves = [b.reshape(4, 2, fs // 2, d) for b in mlp_grads[1:]]
    mlp_sums = [_pair_sum(f"mlp{a}", h, l, c_arr, BF16) for a, (h, l) in enumerate(zip(mlp_halves, _pair_swap("pair_swap_mlp", mlp_halves)))]
    mlp_started, token = _exchange_start("exchange_mlp_start", mlp_sums)
    grad_x, dwi, dwo, dws, dbs, dcw, dvq, dvd = _bwd_mix(dx1, saved, wi, wo, wstt, cwf, tabs, vq, vd, token, alpha, tm)
    mlp_sums, mlp_parts = _exchange_wait("exchange_mlp_wait", mlp_started, [dwo])

    dws = jnp.where(jnp.tril(jnp.ones((CHUNK, CHUNK), bool)), dws.reshape(heads, CHUNK, CHUNK), 0.0)
    dvd = dvd.at[VD_LN2_G].set(dg2[0]).at[VD_LN2_B].set(db2[0])
    pieces = [_rows128(dws), dbs[:, :heads].T, _rows128(dcw), _rows128(dvq), _rows128(dvd), loss_part]
    sizes = [p.shape[0] for p in pieces]
    rows = -(-sum(sizes) // 16) * 16
    small = _pad_rows(jnp.concatenate(pieces, axis=0), rows)

    big = [dwi, dwo.reshape(4, d // 4, d)]
    halves = [b.reshape(4, 2, b.shape[1] // 2, b.shape[2]) for b in big] + [small.reshape(2, rows // 2, LANES)]
    landed = _pair_swap("pair_swap_mix", halves)
    sums = [_pair_sum(f"mix{a}", h, l, c_arr, BF16) for a, (h, l) in enumerate(zip(halves[:-1], landed[:-1]))]
    sums.append(_pair_sum("small", halves[-1][None], landed[-1][None], c_arr, F32)[0])
    mix_started, token = _exchange_start("exchange_mix_start", sums)

    out, raw = {}, {}

    def finish(first, names, parts, sums, after):
        mine = [_chip_sum(first + a, p, s, me_arr, after) for a, (p, s) in enumerate(zip(parts, sums))]
        other = _pair_gather(f"pair_gather_{first}", mine)
        for a, nm in enumerate(names):
            w_, m_, v_ = weights[nm]
            if nm in ("w_gate", "w_up"):
                raw[nm] = _adamw(nm, w_[0].T, mine[a], other[a], m_[0].T, v_[0].T, c_arr)
                out[nm] = [o.T for o in raw[nm]]
            else:
                raw[nm] = out[nm] = _adamw(nm, w_[0], mine[a], other[a], m_[0], v_[0], c_arr)
        return mine[-1], other[-1]

    weights = {"w_in": (w_in, m_w_in, v_w_in), "w_out": (w_out, m_w_out, v_w_out), "w_gate": (w_gate, m_w_gate, v_w_gate),
               "w_up": (w_up, m_w_up, v_w_up), "w_down": (w_down, m_w_down, v_w_down)}
    finish(2, ["w_gate", "w_up", "w_down"], mlp_parts, mlp_sums, token)
    sums, parts = _exchange_wait("exchange_mix_wait", mix_started, [raw[nm][1] for nm in ("w_gate", "w_up", "w_down")])
    small_mine, small_other = finish(5, ["w_in", "w_out"], parts, sums, parts[0])

    def pack(ws, bs, vqs, vds):
        ps = [_rows128(ws[0]), bs[0], jnp.zeros((sizes[2], LANES), F32),
              _rows128(_pad_rows(jnp.concatenate(vqs, axis=0), 8)), _rows128(_pad_rows(jnp.concatenate(vds, axis=0), 8)),
              jnp.zeros((sizes[5], LANES), F32)]
        return _pad_rows(jnp.concatenate(ps, axis=0), rows)

    packed = _adamw(
        "small",
        pack(w_s, b_s, [sgu_ln_g, sgu_ln_b, conv_b, conv_ln_g, conv_ln_b], [ln1_g, ln1_b, ln2_g, ln2_b]), small_mine, small_other,
        pack(m_w_s, m_b_s, [m_sgu_ln_g, m_sgu_ln_b, m_conv_b, m_conv_ln_g, m_conv_ln_b], [m_ln1_g, m_ln1_b, m_ln2_g, m_ln2_b]),
        pack(v_w_s, v_b_s, [v_sgu_ln_g, v_sgu_ln_b, v_conv_b, v_conv_ln_g, v_conv_ln_b], [v_ln1_g, v_ln1_b, v_ln2_g, v_ln2_b]),
        c_arr)

    offs = [sum(sizes[:i]) for i in range(len(sizes))]
    g_cw_full = packed[0][offs[2]:offs[2] + sizes[2]].reshape(kwp, q)
    g_cw = lax.dynamic_slice(g_cw_full, (0, me * cshard), (kwp, cshard))
    out["conv_w"] = _adamw("conv_w", _pad_rows(conv_w[0], kwp), g_cw, g_cw, _pad_rows(m_conv_w[0], kwp), _pad_rows(v_conv_w[0], kwp), c_arr)
    out["conv_w"] = [o[:kw] for o in out["conv_w"]]

    def unpack(p):
        vq_o = p[offs[3]:offs[3] + sizes[3]].reshape(8, q)
        vd_o = p[offs[4]:offs[4] + sizes[4]].reshape(8, d)
        return {"w_s": p[offs[0]:offs[0] + sizes[0]].reshape(heads, CHUNK, CHUNK), "b_s": p[offs[1]:offs[1] + sizes[1]],
                "sgu_ln_g": vq_o[VQ_SGU_G], "sgu_ln_b": vq_o[VQ_SGU_B], "conv_b": vq_o[VQ_CONV_B],
                "conv_ln_g": vq_o[VQ_CLN_G], "conv_ln_b": vq_o[VQ_CLN_B],
                "ln1_g": vd_o[VD_LN1_G], "ln1_b": vd_o[VD_LN1_B], "ln2_g": vd_o[VD_LN2_G], "ln2_b": vd_o[VD_LN2_B]}

    small_out = [unpack(p) for p in packed]
    loss = packed[0][offs[5], 0]
    names = ["w_in", "sgu_ln_g", "sgu_ln_b", "w_s", "b_s", "conv_w", "conv_b", "conv_ln_g", "conv_ln_b", "w_out",
             "ln1_g", "ln1_b", "w_gate", "w_up", "w_down", "ln2_g", "ln2_b"]
    result = [loss, grad_x[None]]
    for kind in range(4):
        for nm in names:
            val = out[nm][kind] if nm in out else small_out[kind][nm]
            result.append(val[None])
    return tuple(result)
```

```python
import functools
import math

import jax
import numpy as np
import jax.numpy as jnp
from jax import lax
from jax.experimental import pallas as pl
from jax.experimental.pallas import tpu as pltpu

F32 = jnp.float32
BF16 = jnp.bfloat16

LN_EPS = 1e-5
HEAD_DIM = 64
CHUNK = 128
HALO = 32
LANES = 128
MXU_N = 256
ADAM_LR, ADAM_B1, ADAM_B2, ADAM_EPS, ADAM_WD, ADAM_STEP = 0.001, 0.9, 0.999, 1e-08, 0.01, 10
VMEM_LIMIT = 63 * 1024 * 1024
MESH_AXES = ("x", "y", "c")
MESH_ID = pl.DeviceIdType.MESH


def _dot(a, b):
    return jnp.dot(a, b, preferred_element_type=F32)


def _dot_nt(a, b):
    return lax.dot_general(a, b, (((1,), (1,)), ((), ())), preferred_element_type=F32)


def _dot_tn(a, b):
    return lax.dot_general(a, b, (((0,), (0,)), ((), ())), preferred_element_type=F32)


def _sigmoid(v):
    return 1.0 / (1.0 + jnp.exp(-v))


def _gelu(v):
    cdf = 0.5 * (1.0 + lax.erf(v * (1.0 / math.sqrt(2.0))))
    pdf = jnp.exp(-0.5 * v * v) * (1.0 / math.sqrt(2.0 * math.pi))
    return v * cdf, cdf + v * pdf


def _ln_stats(v):
    mu = jnp.mean(v, axis=-1, keepdims=True)
    d = v - mu
    rstd = lax.rsqrt(jnp.mean(d * d, axis=-1, keepdims=True) + LN_EPS)
    return d * rstd, rstd


def _ln_bwd(dxhat, xhat, rstd):
    m1 = jnp.mean(dxhat, axis=-1, keepdims=True)
    m2 = jnp.mean(dxhat * xhat, axis=-1, keepdims=True)
    return rstd * (dxhat - m1 - xhat * m2)


def _colsum(v):
    return jnp.sum(v, axis=0, keepdims=True)


def _pair_lanes(v, nc, p):
    return jnp.concatenate([v[c * CHUNK:(c + 1) * CHUNK, p * LANES:(p + 1) * LANES] for c in range(nc)], axis=1)


def _unpair(parts, nc):
    rows = [jnp.concatenate([part[:, c * LANES:(c + 1) * LANES] for part in parts], axis=1) for c in range(nc)]
    return jnp.concatenate(rows, axis=0)


def _low_head(nc):
    lane = lax.broadcasted_iota(jnp.int32, (CHUNK, nc * LANES), 1)
    return (lane & (LANES - 1)) < HEAD_DIM


def _mix(wst_ref, v, nc, n_pairs):
    vb = v.astype(BF16)
    low = _low_head(nc)
    parts = []
    for p in range(n_pairs):
        r = _dot(wst_ref[p], _pair_lanes(vb, nc, p))
        parts.append(jnp.where(low, r[:CHUNK], r[CHUNK:]))
    return _unpair(parts, nc)


def _mix_wgrad(dm, vn, nc, n_pairs):
    low = _low_head(nc)
    vb = vn.astype(BF16)
    out = []
    for p in range(n_pairs):
        a = _pair_lanes(dm, nc, p)
        lhs = jnp.concatenate([jnp.where(low, a, 0.0), jnp.where(low, 0.0, a)], axis=0).astype(BF16)
        out.append(_dot_nt(lhs, _pair_lanes(vb, nc, p)))
    return out


SUBLANES = 8


CONV_BLOCK = 256
DFT_N = CONV_BLOCK + HALO
DFT_F = -(-(DFT_N // 2 + 1) // SUBLANES) * SUBLANES


def _terms(m, exact):
    hi = m.astype(np.float32).astype(BF16)
    lo = (m.astype(np.float32) - hi.astype(np.float32)).astype(BF16)
    return np.concatenate([hi, hi, lo] if exact else [hi, hi], axis=1)


def _split(v, exact=False):
    hi = v.astype(BF16)
    lo = (v - hi.astype(F32)).astype(BF16)
    return jnp.concatenate([hi, lo, hi] if exact else [hi, lo], axis=0)


def _dft_tables(kw, kwp, q):
    nf = DFT_N // 2 + 1
    ang = 2.0 * np.pi * np.arange(nf)[:, None] * np.arange(DFT_N)[None, :] / DFT_N
    fwd = np.zeros((2 * DFT_F, DFT_N))
    fwd[:nf], fwd[DFT_F:DFT_F + nf] = np.cos(ang), -np.sin(ang)
    weight = np.full((nf, 1), 2.0 / DFT_N)
    weight[0] = weight[-1] = 1.0 / DFT_N
    inv = np.zeros((DFT_N, 2 * DFT_F))
    inv[:, :nf], inv[:, DFT_F:DFT_F + nf] = (np.cos(ang) * weight).T, (-np.sin(ang) * weight).T
    inv_taps = np.zeros((kwp, 2 * DFT_F))
    inv_taps[:kw] = inv[kw - 1::-1][:kw]
    shift = np.zeros((2 * DFT_F, q), np.float32)
    shift[:nf], shift[DFT_F:DFT_F + nf] = np.cos(ang[:, HALO:HALO + 1]), -np.sin(ang[:, HALO:HALO + 1])
    return {"fwd": _terms(fwd, False), "fwd_halo": _terms(fwd[:, CONV_BLOCK:], False), "shift": shift,
            "inv_out": _terms(inv[HALO:HALO + CONV_BLOCK], False), "inv_in": _terms(inv[:CONV_BLOCK], False),
            "taps": _terms(fwd[:, :kwp], True), "inv_taps": _terms(inv_taps, True)}


def _cmul(a, b, conj_b=False):
    ar, ai, br, bi = a[:DFT_F], a[DFT_F:], b[:DFT_F], b[DFT_F:]
    if conj_b:
        return jnp.concatenate([ar * br + ai * bi, ai * br - ar * bi], axis=0)
    return jnp.concatenate([ar * br - ai * bi, ar * bi + ai * br], axis=0)


def _interleave(sub_tiles):
    waiting, live = list(sub_tiles), []
    while waiting or live:
        if waiting:
            live.append(waiting.pop(0))
        for g in list(live):
            try:
                next(g)
            except StopIteration:
                live.remove(g)


def _cparams():
    return pltpu.CompilerParams(dimension_semantics=("arbitrary",), vmem_limit_bytes=VMEM_LIMIT)


def _full(shape):
    return pl.BlockSpec(shape, lambda i: (0,) * len(shape))


ANY = pl.BlockSpec(memory_space=pl.ANY)

VQ_SGU_G, VQ_SGU_B, VQ_CONV_B, VQ_CLN_G, VQ_CLN_B = range(5)
VD_LN1_G, VD_LN1_B, VD_LN2_G, VD_LN2_B = range(4)
RS_LN1, RS_SGU, RS_CONV = range(3)
RS_COLS = 8


def _fwd_mix(x, wi, wo, wst, bmat, cwf, tabs, vq, vd, mlp_w, alpha, tm):
    t, d = x.shape
    q = wi.shape[2]
    nc, n_pairs = CONV_BLOCK // CHUNK, q // LANES
    n = t // tm
    n_in, n_saved = 11, 12
    assert tm % CONV_BLOCK == 0

    def body(x_ref, wi_hbm, wo_hbm, wst_ref, bmat_ref, cwf_ref, fwd_ref, taps_ref, inv_ref, vq_ref, vd_ref, *rest):
        (xb_ref, pag_ref, y_ref, xh_ref, rs_ref, zu_ref, mg_ref, vhat_ref, gv_ref, vnb_ref, yhat_ref, hf_ref) = rest[3:3 + n_saved]
        gathered = rest[3 + n_saved:6 + n_saved]
        wi_v, wo_v, hb_ref, gf_ref, send_sems, recv_sems = rest[6 + n_saved:]
        step = pl.program_id(0)

        @pl.when(step == 0)
        def _():
            _Gather(gathered, send_sems, recv_sems).start()
            pltpu.sync_copy(wi_hbm, wi_v)
            pltpu.sync_copy(wo_hbm, wo_v)
            hb_ref[...] = jnp.zeros_like(hb_ref)
            gf_ref[...] = _dot(taps_ref[...], _split(cwf_ref[...], True))

        @pl.when(step == (3 * n) // 4)
        def _():
            _Gather(gathered, send_sems, recv_sems).forward()

        def sub_tile(b):
            rows = slice(b * CONV_BLOCK, (b + 1) * CONV_BLOCK)
            xv = x_ref[rows, :]
            xb = xv.astype(BF16)
            xb_ref[rows, :] = xb
            pu, pv, pa, pg = (_dot(xb, wi_v[j]) for j in range(4))
            yield
            pag_ref[rows, 0:q] = pa.astype(BF16)
            pag_ref[rows, q:2 * q] = pg.astype(BF16)
            zu, gu = _gelu(pu)
            zv, gv = _gelu(pv)
            vhat, rstd_v = _ln_stats(zv)
            vnb = (vhat * vq_ref[VQ_SGU_G:VQ_SGU_G + 1, :] + vq_ref[VQ_SGU_B:VQ_SGU_B + 1, :]).astype(BF16)
            hb_ref[HALO + b * CONV_BLOCK:HALO + (b + 1) * CONV_BLOCK, :] = pa * _sigmoid(pg)
            yield
            mixed = _mix(wst_ref, vnb, nc, n_pairs) + jnp.concatenate([bmat_ref[...]] * nc, axis=0)
            spectrum = _dot(fwd_ref[...], _split(hb_ref[b * CONV_BLOCK:b * CONV_BLOCK + DFT_N, :]))
            yield
            y_ref[rows, 0:q] = (zu * mixed).astype(BF16)
            zu_ref[rows, :] = zu
            mg_ref[rows, :] = mixed * gu
            vhat_ref[rows, :] = vhat
            gv_ref[rows, :] = gv
            vnb_ref[rows, :] = vnb
            hf_ref[b * 2 * DFT_F:(b + 1) * 2 * DFT_F, :] = spectrum
            product = _split(_cmul(gf_ref[...], spectrum))
            yield
            yc = _dot(inv_ref[...], product) + vq_ref[VQ_CONV_B:VQ_CONV_B + 1, :]
            yield
            yhat, rstd_c = _ln_stats(yc)
            yhat_ref[rows, :] = yhat
            yn = yhat * vq_ref[VQ_CLN_G:VQ_CLN_G + 1, :] + vq_ref[VQ_CLN_B:VQ_CLN_B + 1, :]
            y_ref[rows, q:2 * q] = (yn * _sigmoid(yn)).astype(BF16)
            yield
            r1 = alpha * xv + _dot(y_ref[rows, :], wo_v[...])
            yield
            xhat, rstd1 = _ln_stats(r1)
            xh_ref[rows, :] = xhat
            col = lax.broadcasted_iota(jnp.int32, (CONV_BLOCK, RS_COLS), 1)
            rs_ref[rows, :] = jnp.where(col == RS_LN1, rstd1, jnp.where(col == RS_SGU, rstd_v, jnp.where(col == RS_CONV, rstd_c, 0.0)))

        _interleave([sub_tile(b) for b in range(tm // CONV_BLOCK)])
        hb_ref[0:HALO, :] = hb_ref[tm:tm + HALO, :]

        @pl.when(step == n - 1)
        def _():
            _Gather(gathered, send_sems, recv_sems).finish()

    row = lambda w: pl.BlockSpec((tm, w), lambda i: (i, 0))
    widths = [(d, BF16), (2 * q, BF16), (d, BF16), (d, F32), (RS_COLS, F32), (q, F32), (q, F32), (q, F32), (q, F32), (q, BF16), (q, F32)]
    assert len(widths) + 1 == n_saved
    small_ins = [wst, bmat, cwf, tabs["fwd"], tabs["taps"], tabs["inv_out"], vq, vd]
    return pl.pallas_call(
        body, name="fwd_mix", grid=(n,),
        in_specs=[row(d), ANY, ANY] + [_full(a.shape) for a in small_ins] + [ANY] * 3,
        out_specs=[row(w) for w, _ in widths] + [pl.BlockSpec((tm // CONV_BLOCK * 2 * DFT_F, q), lambda i: (i, 0))] + [ANY] * 3,
        out_shape=[jax.ShapeDtypeStruct((t, w), dt) for w, dt in widths] + [jax.ShapeDtypeStruct((t // CONV_BLOCK * 2 * DFT_F, q), F32)]
        + [jax.ShapeDtypeStruct(b.shape, b.dtype) for b in mlp_w],
        scratch_shapes=[pltpu.VMEM(wi.shape, BF16), pltpu.VMEM(wo.shape, BF16), pltpu.VMEM((HALO + tm, q), F32),
                        pltpu.VMEM((2 * DFT_F, q), F32)] + _gather_sems(3),
        input_output_aliases={n_in + a: n_saved + a for a in range(3)},
        compiler_params=_cparams(),
    )(x, wi, wo, *small_ins, *mlp_w)


def _hidden_slabs(f):
    assert f % MXU_N == 0
    tiles = f // MXU_N
    sizes = [(tiles // 4 + (1 if j < tiles % 4 else 0)) * MXU_N for j in range(4)]
    return [(sum(sizes[:j]), sz) for j, sz in enumerate(sizes) if sz]


def _fwd_mlp(xh1, tgt, wg, wu, wd, vd, alpha, slabs, tm):
    t, d = xh1.shape
    n = t // tm
    ns = len(slabs)
    half = tm // 2 if tm % 32 == 0 else tm

    def body(xh_ref, tgt_ref, wg_hbm, wu_hbm, wd_hbm, vd_ref, *rest):
        gp_refs, up_refs = rest[:ns], rest[ns:2 * ns]
        x1b_ref, dr2_ref, loss_ref, dg2_ref, db2_ref, wg_v, wu_v, wd_v = rest[2 * ns:]

        @pl.when(pl.program_id(0) == 0)
        def _():
            pltpu.sync_copy(wg_hbm, wg_v)
            pltpu.sync_copy(wu_hbm, wu_v)
            pltpu.sync_copy(wd_hbm, wd_v)
            loss_ref[...] = jnp.zeros_like(loss_ref)
            dg2_ref[...] = jnp.zeros_like(dg2_ref)
            db2_ref[...] = jnp.zeros_like(db2_ref)

        g2 = vd_ref[VD_LN2_G:VD_LN2_G + 1, :]

        for r0 in range(0, tm, half):
            rows = slice(r0, r0 + half)
            x1 = xh_ref[rows, :] * vd_ref[VD_LN1_G:VD_LN1_G + 1, :] + vd_ref[VD_LN1_B:VD_LN1_B + 1, :]
            x1b = x1.astype(BF16)
            x1b_ref[rows, :] = x1b
            acc = alpha * x1
            for (off, sz), gp_ref, up_ref in zip(slabs, gp_refs, up_refs):
                gp = _dot_nt(x1b, wg_v[off:off + sz, :])
                up = _dot_nt(x1b, wu_v[off:off + sz, :])
                gp_ref[rows, :] = gp.astype(BF16)
                up_ref[rows, :] = up.astype(BF16)
                acc = acc + _dot((gp * _sigmoid(gp) * up).astype(BF16), wd_v[off:off + sz, :])
            xh2, rstd2 = _ln_stats(acc)
            err = xh2 * g2 + vd_ref[VD_LN2_B:VD_LN2_B + 1, :] - tgt_ref[rows, :]
            loss_ref[...] += _colsum(jnp.sum(err * err, axis=1, keepdims=True)) * (0.5 / d)
            dy = err * (1.0 / d)
            dg2_ref[...] += _colsum(dy * xh2)
            db2_ref[...] += _colsum(dy)
            dr2_ref[rows, :] = _ln_bwd(dy * g2, xh2, rstd2)

    row = lambda w: pl.BlockSpec((tm, w), lambda i: (i, 0))
    act = [sz for _, sz in slabs] * 2
    return pl.pallas_call(
        body, name="fwd_mlp", grid=(n,),
        in_specs=[row(d), row(d), ANY, ANY, ANY, _full(vd.shape)],
        out_specs=[row(sz) for sz in act] + [row(d), row(d), _full((8, LANES)), _full((1, d)), _full((1, d))],
        out_shape=[jax.ShapeDtypeStruct((t, sz), BF16) for sz in act]
        + [jax.ShapeDtypeStruct((t, d), BF16), jax.ShapeDtypeStruct((t, d), F32),
           jax.ShapeDtypeStruct((8, LANES), F32), jax.ShapeDtypeStruct((1, d), F32), jax.ShapeDtypeStruct((1, d), F32)],
        scratch_shapes=[pltpu.VMEM(wg.shape, BF16), pltpu.VMEM(wu.shape, BF16), pltpu.VMEM(wd.shape, BF16)],
        compiler_params=_cparams(),
    )(xh1, tgt, wg, wu, wd, vd)


def _bwd_mlp_slab(j, slab, dr2, prev, x1b, gp, up, wg, wu, wd, alpha, tm):
    t, d = dr2.shape
    off, sz = slab
    n = t // tm
    first = prev is None

    def body(*refs):
        if first:
            dr_ref, x1b_ref, gp_ref, up_ref, wg_hbm, wu_hbm, wd_hbm = refs[:7]
        else:
            dr_ref, dxp_ref, x1b_ref, gp_ref, up_ref, wg_hbm, wu_hbm, wd_hbm = refs[:8]
        dx_ref, dwg_hbm, dwu_hbm, dwd_hbm, dwg16_hbm, dwu16_hbm, dwd16_hbm, ag, au, ad, wg_v, wu_v, wd_v = refs[-13:]

        @pl.when(pl.program_id(0) == 0)
        def _():
            for src, dst in ((wg_hbm, wg_v), (wu_hbm, wu_v), (wd_hbm, wd_v)):
                pltpu.sync_copy(src.at[pl.ds(off, sz)], dst)
            ag[...] = jnp.zeros_like(ag)
            au[...] = jnp.zeros_like(au)
            ad[...] = jnp.zeros_like(ad)

        dr = dr_ref[...]
        drb = dr.astype(BF16)
        x1b = x1b_ref[...]
        gpv = gp_ref[...].astype(F32)
        upv = up_ref[...].astype(F32)
        dh = _dot_nt(drb, wd_v[...])
        sg = _sigmoid(gpv)
        silu = gpv * sg
        ad[...] += _dot_tn((silu * upv).astype(BF16), drb)
        dgp = (dh * upv * (sg * (1.0 + gpv * (1.0 - sg)))).astype(BF16)
        dup = (dh * silu).astype(BF16)
        ag[...] += _dot_tn(dgp, x1b)
        au[...] += _dot_tn(dup, x1b)
        base = alpha * dr if first else dxp_ref[...]
        dx_ref[...] = base + _dot(dgp, wg_v[...]) + _dot(dup, wu_v[...])

        @pl.when(pl.program_id(0) == n - 1)
        def _():
            for acc, dst, stage, dst16 in ((ag, dwg_hbm, wg_v, dwg16_hbm), (au, dwu_hbm, wu_v, dwu16_hbm), (ad, dwd_hbm, wd_v, dwd16_hbm)):
                pltpu.sync_copy(acc, dst.at[pl.ds(off, sz)])
                stage[...] = acc[...].astype(BF16)
                pltpu.sync_copy(stage, dst16.at[pl.ds(off, sz)])

    row = lambda w: pl.BlockSpec((tm, w), lambda i: (i, 0))
    ins = [dr2] + ([] if first else [prev[0]]) + [x1b, gp, up, wg, wu, wd] + ([] if first else list(prev[1:]))
    in_specs = [row(d)] + ([] if first else [row(d)]) + [row(d), row(sz), row(sz), ANY, ANY, ANY] + ([] if first else [ANY] * 6)
    return pl.pallas_call(
        body, name=f"bwd_mlp_{j}", grid=(n,),
        in_specs=in_specs,
        out_specs=[row(d)] + [ANY] * 6,
        out_shape=[jax.ShapeDtypeStruct((t, d), F32)] + [jax.ShapeDtypeStruct(wg.shape, F32)] * 3 + [jax.ShapeDtypeStruct(wg.shape, BF16)] * 3,
        scratch_shapes=[pltpu.VMEM((sz, d), F32)] * 3 + [pltpu.VMEM((sz, d), BF16)] * 3,
        input_output_aliases={} if first else {8 + a: 1 + a for a in range(6)},
        compiler_params=_cparams(),
    )(*ins)


def _bwd_mix(dx1, saved, wi, wo, wstt, cwf, tabs, vq, vd, token, alpha, tm):
    xb, pag, y, xh1, rs, zu_s, mg_s, vhat_s, gv_s, vnb_s, yhat_s, hf_s = saved
    t, d = xh1.shape
    q = wi.shape[2]
    nc, n_pairs = CONV_BLOCK // CHUNK, q // LANES
    n = t // tm
    nb = tm // CONV_BLOCK
    assert tm % CONV_BLOCK == 0

    def body(dx1_ref, xb_ref, pag_ref, y_ref, xh_ref, rs_ref, zu_ref, mg_ref, vhat_ref, gv_ref, vnb_ref, yhat_ref, hf_ref,
             wi_hbm, wo_hbm, wstt_ref, cwf_ref, fwd_ref, fwd_halo_ref, shift_ref, taps_ref, inv_ref, inv_taps_ref, vq_ref, vd_ref,
             token_ref,
             gx_ref, dwi_hbm, dwo_hbm, dws_ref, dbs_ref, dcw_ref, dvq_ref, dvd_ref,
             wi_v, wo_v, awi, awo, dyb_ref, later_ref, dbm_ref, gf_ref, dgf_ref):
        i = pl.program_id(0)

        @pl.when(i == 0)
        def _():
            pltpu.sync_copy(wi_hbm, wi_v)
            pltpu.sync_copy(wo_hbm, wo_v)
            for r in (awi, awo, dws_ref, dbm_ref, dgf_ref, dvq_ref, dvd_ref, dyb_ref, later_ref):
                r[...] = jnp.zeros_like(r)
            gf_ref[...] = _dot(taps_ref[...], _split(cwf_ref[...], True))

        dr1b_parts, dproj_parts = [None] * nb, [None] * nb

        def sub_tile(b):
            rows = slice(b * CONV_BLOCK, (b + 1) * CONV_BLOCK)
            dx1v = dx1_ref[rows, :]
            xh = xh_ref[rows, :]
            rsv = rs_ref[rows, :]
            dvd_ref[VD_LN1_G:VD_LN1_G + 1, :] += _colsum(dx1v * xh)
            dvd_ref[VD_LN1_B:VD_LN1_B + 1, :] += _colsum(dx1v)
            dr1 = _ln_bwd(dx1v * vd_ref[VD_LN1_G:VD_LN1_G + 1, :], xh, rsv[:, RS_LN1:RS_LN1 + 1])
            dr1b = dr1.astype(BF16)
            yield
            dy = _dot_nt(dr1b, wo_v[...])
            yield
            vhat = vhat_ref[rows, :]
            sgu_g = vq_ref[VQ_SGU_G:VQ_SGU_G + 1, :]
            doa = dy[:, 0:q]
            dm = doa * zu_ref[rows, :]
            dpu = (doa * mg_ref[rows, :]).astype(BF16)
            acc = dm[0:CHUNK]
            for c in range(1, nc):
                acc = acc + dm[c * CHUNK:(c + 1) * CHUNK]
            dbm_ref[...] += acc
            pa = pag_ref[rows, 0:q].astype(F32)
            sg = _sigmoid(pag_ref[rows, q:2 * q].astype(F32))
            yhat = yhat_ref[rows, :]
            cln_g = vq_ref[VQ_CLN_G:VQ_CLN_G + 1, :]
            yn = yhat * cln_g + vq_ref[VQ_CLN_B:VQ_CLN_B + 1, :]
            sy = _sigmoid(yn)
            dyn = dy[:, q:2 * q] * (sy * (1.0 + yn * (1.0 - sy)))
            dvq_ref[VQ_CLN_G:VQ_CLN_G + 1, :] += _colsum(dyn * yhat)
            dvq_ref[VQ_CLN_B:VQ_CLN_B + 1, :] += _colsum(dyn)
            dyc = _ln_bwd(dyn * cln_g, yhat, rsv[:, RS_CONV:RS_CONV + 1])
            dvq_ref[VQ_CONV_B:VQ_CONV_B + 1, :] += _colsum(dyc)
            dyb_ref[b, 0:CONV_BLOCK, :] = dyc
            yield
            wgrads = _mix_wgrad(dm, vnb_ref[rows, :], nc, n_pairs)
            dvn = _mix(wstt_ref, dm, nc, n_pairs)
            own = _dot(fwd_ref[...], _split(dyb_ref[b]))
            with_later = own + _dot(fwd_halo_ref[...], _split(later_ref[...]))
            later_ref[...] = dyb_ref[b, 0:HALO, :]
            yield
            for p, g in enumerate(wgrads):
                dws_ref[p] += g
            dvq_ref[VQ_SGU_G:VQ_SGU_G + 1, :] += _colsum(dvn * vhat)
            dvq_ref[VQ_SGU_B:VQ_SGU_B + 1, :] += _colsum(dvn)
            dpv = (_ln_bwd(dvn * sgu_g, vhat, rsv[:, RS_SGU:RS_SGU + 1]) * gv_ref[rows, :]).astype(BF16)
            dgf_ref[...] += _cmul(_cmul(own, shift_ref[...]), hf_ref[b * 2 * DFT_F:(b + 1) * 2 * DFT_F, :], conj_b=True)
            product = _split(_cmul(with_later, gf_ref[...], conj_b=True))
            yield
            dh = _dot(inv_ref[...], product)
            yield
            da = (dh * sg).astype(BF16)
            dg = (dh * pa * (sg * (1.0 - sg))).astype(BF16)
            yield
            gx = alpha * dr1
            for dpj, wj in zip((dpu, dpv, da, dg), range(4)):
                gx = gx + _dot_nt(dpj, wi_v[wj])
            gx_ref[rows, :] = gx
            dr1b_parts[b], dproj_parts[b] = dr1b, (dpu, dpv, da, dg)

        _interleave([sub_tile(b) for b in reversed(range(nb))])

        awo[...] += _dot_tn(y_ref[...], jnp.concatenate(dr1b_parts, axis=0))
        xb = xb_ref[...]
        for j in range(4):
            awi[j] += _dot_tn(xb, jnp.concatenate([part[j] for part in dproj_parts], axis=0))

        @pl.when(i == n - 1)
        def _():
            pltpu.sync_copy(awi, dwi_hbm)
            pltpu.sync_copy(awo, dwo_hbm)
            lane = lax.broadcasted_iota(jnp.int32, (CHUNK, LANES), 1)
            low = lane < HEAD_DIM
            dbs = jnp.zeros((CHUNK, LANES), F32)
            for p in range(n_pairs):
                grp = dbm_ref[:, p * LANES:(p + 1) * LANES]
                dbs = jnp.where(lane == 2 * p, jnp.sum(jnp.where(low, grp, 0.0), axis=1, keepdims=True), dbs)
                dbs = jnp.where(lane == 2 * p + 1, jnp.sum(jnp.where(low, 0.0, grp), axis=1, keepdims=True), dbs)
            dbs_ref[...] = dbs
            dcw_ref[...] = _dot(inv_taps_ref[...], _split(dgf_ref[...], True))

    rev = lambda w: pl.BlockSpec((tm, w), lambda i: (n - 1 - i, 0))
    small = [jax.ShapeDtypeStruct((n_pairs, 2 * CHUNK, CHUNK), F32), jax.ShapeDtypeStruct((CHUNK, LANES), F32),
             jax.ShapeDtypeStruct(cwf.shape, F32), jax.ShapeDtypeStruct(vq.shape, F32), jax.ShapeDtypeStruct(vd.shape, F32)]
    small_ins = [wstt, cwf, tabs["fwd"], tabs["fwd_halo"], tabs["shift"], tabs["taps"], tabs["inv_in"], tabs["inv_taps"], vq, vd]
    return pl.pallas_call(
        body, name="bwd_mix", grid=(n,),
        in_specs=[rev(d), rev(d), rev(2 * q), rev(d), rev(d), rev(RS_COLS), rev(q), rev(q), rev(q), rev(q), rev(q), rev(q),
                  pl.BlockSpec((nb * 2 * DFT_F, q), lambda i: (n - 1 - i, 0)), ANY, ANY] + [_full(a.shape) for a in small_ins] + [ANY],
        out_specs=[rev(d), ANY, ANY] + [_full(s.shape) for s in small],
        out_shape=[jax.ShapeDtypeStruct((t, d), F32), jax.ShapeDtypeStruct(wi.shape, F32), jax.ShapeDtypeStruct(wo.shape, F32)] + small,
        scratch_shapes=[pltpu.VMEM(wi.shape, BF16), pltpu.VMEM(wo.shape, BF16), pltpu.VMEM(wi.shape, F32), pltpu.VMEM(wo.shape, F32),
                        pltpu.VMEM((nb, DFT_N, q), F32), pltpu.VMEM((HALO, q), F32),
                        pltpu.VMEM((CHUNK, q), F32), pltpu.VMEM((2 * DFT_F, q), F32), pltpu.VMEM((2 * DFT_F, q), F32)],
        compiler_params=_cparams(),
    )(dx1, xb, pag, y, xh1, rs, zu_s, mg_s, vhat_s, gv_s, vnb_s, yhat_s, hf_s, wi, wo, *small_ins, token)


def _prep(me_arr, w_in, w_out, w_gate_t, w_up_t, w_down, conv_w, kwp):
    kw, cshard = conv_w.shape

    def body(me_ref, wi_ref, wo_ref, wg_ref, wu_ref, wd_ref, cw_ref, oi, oo, og, ou, od, oc):
        for src, dst in ((wi_ref, oi), (wo_ref, oo), (wg_ref, og), (wu_ref, ou), (wd_ref, od)):
            dst[...] = src[...].astype(BF16)
        oc[0:kw, :] = cw_ref[...]
        oc[kw:kwp, :] = jnp.zeros((kwp - kw, cshard), F32)

    ins = (w_in, w_out, w_gate_t, w_up_t, w_down, conv_w)
    outs = [jax.ShapeDtypeStruct((4,) + a.shape, BF16) for a in ins[:5]] + [jax.ShapeDtypeStruct((4, kwp, cshard), F32)]
    grid_spec = pltpu.PrefetchScalarGridSpec(
        num_scalar_prefetch=1, grid=(1,),
        in_specs=[pl.BlockSpec(a.shape, lambda i, me: (0, 0)) for a in ins],
        out_specs=[pl.BlockSpec((None,) + o.shape[1:], lambda i, me: (me[0], 0, 0)) for o in outs])
    return pl.pallas_call(body, name="wprep", grid_spec=grid_spec, out_shape=outs, compiler_params=_cparams())(me_arr, *ins)


def _coords():
    return tuple(lax.axis_index(a) for a in MESH_AXES)


def _other_chips(x, y):
    return [(1 - x, y), (x, 1 - y), (1 - x, 1 - y)]


def _remote(src, dst, send_sem, recv_sem, to):
    return pltpu.make_async_remote_copy(src_ref=src, dst_ref=dst, send_sem=send_sem, recv_sem=recv_sem,
                                        device_id=to, device_id_type=MESH_ID)


def _hbm_call(body, name, ins, out_shape, scratch_shapes, aliases=None):
    return pl.pallas_call(
        body, name=name, in_specs=[ANY] * len(ins), out_specs=[ANY] * len(out_shape), out_shape=out_shape,
        scratch_shapes=scratch_shapes, input_output_aliases=aliases or {},
    )(*ins)


class _Gather:
    def __init__(self, bufs, send_sems, recv_sems):
        self.bufs, self.send_sems, self.recv_sems = bufs, send_sems, recv_sems
        self.x, self.y, self.c = _coords()

    def _copies(self, stage):
        x, y, c = self.x, self.y, self.c
        for a, buf in enumerate(self.bufs):
            hr = buf.shape[1] // 2
            for j, chip in enumerate(_other_chips(x, y)):
                if stage == "ici_out":
                    ref, k, to = buf.at[2 * x + y, pl.ds(c * hr, hr)], j, (*chip, c)
                elif stage == "ici_in":
                    ref, k, to = buf.at[2 * chip[0] + chip[1], pl.ds(c * hr, hr)], j, (*chip, c)
                elif stage == "d2d_out":
                    ref, k, to = buf.at[2 * chip[0] + chip[1], pl.ds(c * hr, hr)], 3 + j, (x, y, 1 - c)
                else:
                    ref, k, to = buf.at[2 * chip[0] + chip[1], pl.ds((1 - c) * hr, hr)], 3 + j, (x, y, 1 - c)
                yield _remote(ref, ref, self.send_sems.at[a, k], self.recv_sems.at[a, k], to)

    def start(self):
        for cp in self._copies("ici_out"):
            cp.start()

    def forward(self):
        for landed, onward in zip(self._copies("ici_in"), self._copies("d2d_out")):
            landed.wait_recv()
            onward.start()

    def finish(self):
        for cp in self._copies("d2d_in"):
            cp.wait_recv()
        for stage in ("ici_out", "d2d_out"):
            for cp in self._copies(stage):
                cp.wait_send()


def _gather_sems(n):
    return [pltpu.SemaphoreType.DMA((n, 6)), pltpu.SemaphoreType.DMA((n, 6))]


def _gather_shards(bufs):
    n = len(bufs)

    def body(*refs):
        g = _Gather(refs[n:2 * n], *refs[2 * n:])
        g.start()
        g.forward()
        g.finish()

    return _hbm_call(body, "gather_shards", bufs, [jax.ShapeDtypeStruct(s.shape, s.dtype) for s in bufs],
                     _gather_sems(n), aliases={a: a for a in range(n)})


def _pair_swap(name, arrs):
    n = len(arrs)

    def body(*refs):
        src, land = refs[:n], refs[n:2 * n]
        send_sems, recv_sems = refs[2 * n:]
        x, y, c = _coords()
        copies = []
        for a in range(n):
            s = src[a].at[pl.ds(0, arrs[a].shape[0]), 1 - c] if arrs[a].ndim == 4 else src[a].at[1 - c]
            copies.append(_remote(s, land[a], send_sems.at[a], recv_sems.at[a], (x, y, 1 - c)))
            copies[-1].start()
        for cp in copies:
            cp.wait()

    outs = [jax.ShapeDtypeStruct(s.shape[:-3] + s.shape[-2:], s.dtype) for s in arrs]
    return _hbm_call(body, name, arrs, outs, [pltpu.SemaphoreType.DMA((n,)), pltpu.SemaphoreType.DMA((n,))])


class _Exchange:
    def __init__(self, src, dst, send_sems, recv_sems):
        self.src, self.dst, self.send_sems, self.recv_sems = src, dst, send_sems, recv_sems
        self.x, self.y, self.c = _coords()

    def _copies(self, incoming):
        x, y, c = self.x, self.y, self.c
        for a, (s, d) in enumerate(zip(self.src, self.dst)):
            for j, chip in enumerate(_other_chips(x, y)):
                slot = 2 * chip[0] + chip[1]
                if incoming:
                    out, into = d.at[slot], d.at[slot]
                else:
                    out, into = (s.at[slot] if len(s.shape) == 3 else s), d.at[2 * x + y]
                yield _remote(out, into, self.send_sems.at[a, j], self.recv_sems.at[a, j], (*chip, c))

    def start(self):
        for cp in self._copies(False):
            cp.start()

    def finish(self):
        for cp in self._copies(True):
            cp.wait_recv()
        for cp in self._copies(False):
            cp.wait_send()


def _exchange_shapes(arrs):
    return [jax.ShapeDtypeStruct((4,) + s.shape[-2:], s.dtype) for s in arrs]


class _FlatSems:
    def __init__(self, ref):
        self.ref = ref

    @property
    def at(self):
        return self

    def __getitem__(self, idx):
        return self.ref.at[3 * idx[0] + idx[1]]


HBM = pl.BlockSpec(memory_space=pltpu.HBM)
SEM = pl.BlockSpec(memory_space=pltpu.SEMAPHORE)
DATAFLOW = pltpu.SideEffectType.DATAFLOW_SIDE_EFFECTING


def _exchange_start(name, arrs):
    n = len(arrs)
    lands = _exchange_shapes(arrs)

    def body(*refs):
        src, land = refs[:n], refs[n:2 * n]
        send_sems, recv_sems = refs[2 * n:2 * n + 2]
        token = refs[-1]
        _Exchange(src, land, _FlatSems(send_sems), _FlatSems(recv_sems)).start()
        token[...] = jnp.zeros_like(token)

    hbm = lambda a: pltpu.with_memory_space_constraint(a, pltpu.HBM)
    outs = pl.pallas_call(
        body, name=name,
        out_shape=(pltpu.SemaphoreType.DMA((3 * n,)), pltpu.SemaphoreType.DMA((3 * n,)),
                   *[pltpu.HBM(a.shape, a.dtype) for a in arrs], *[pltpu.HBM(s.shape, s.dtype) for s in lands],
                   jax.ShapeDtypeStruct((SUBLANES, LANES), F32)),
        in_specs=[HBM] * (2 * n), out_specs=(SEM, SEM, *[HBM] * (2 * n), pl.BlockSpec(memory_space=pltpu.VMEM)),
        input_output_aliases={a: 2 + a for a in range(2 * n)},
        compiler_params=pltpu.CompilerParams(has_side_effects=DATAFLOW),
    )(*[hbm(a) for a in arrs], *[hbm(lax.empty(s.shape, s.dtype)) for s in lands])
    return outs[:-1], outs[-1]


def _exchange_wait(name, started, after):
    send_sems, recv_sems, *bufs = started
    n = len(bufs) // 2

    def body(*refs):
        src, land = refs[:n], refs[n:2 * n]
        send_sems, recv_sems = refs[2 * n:2 * n + 2]
        _Exchange(src, land, _FlatSems(send_sems), _FlatSems(recv_sems)).finish()

    outs = pl.pallas_call(
        body, name=name,
        out_shape=tuple(pltpu.HBM(b.shape, b.dtype) for b in bufs),
        in_specs=[HBM] * (2 * n) + [SEM, SEM] + [ANY] * len(after), out_specs=tuple([HBM] * (2 * n)),
        input_output_aliases={a: a for a in range(2 * n)},
        compiler_params=pltpu.CompilerParams(has_side_effects=DATAFLOW),
    )(*bufs, send_sems, recv_sems, *after)
    return list(outs[:n]), list(outs[n:])


def _pair_gather(name, halves):
    n = len(halves)

    def body(*refs):
        src, dst = refs[:n], refs[n:2 * n]
        send_sems, recv_sems = refs[2 * n:]
        x, y, c = _coords()
        copies = [_remote(src[a], dst[a], send_sems.at[a], recv_sems.at[a], (x, y, 1 - c)) for a in range(n)]
        for cp in copies:
            cp.start()
        for cp in copies:
            cp.wait()

    outs = [jax.ShapeDtypeStruct(s.shape, s.dtype) for s in halves]
    return _hbm_call(body, name, halves, outs, [pltpu.SemaphoreType.DMA((n,)), pltpu.SemaphoreType.DMA((n,))])


def _pair_sum(a, g, land, c_arr, out_dtype):
    nq, _, hr, cc = g.shape

    def body(c_ref, g_ref, l_ref, o_ref):
        o_ref[...] = (g_ref[...] + l_ref[...].astype(F32)).astype(out_dtype)

    spec = pl.BlockSpec((None, hr, cc), lambda qi, cr: (qi, 0, 0))
    grid_spec = pltpu.PrefetchScalarGridSpec(
        num_scalar_prefetch=1, grid=(nq,),
        in_specs=[pl.BlockSpec((None, None, hr, cc), lambda qi, cr: (qi, cr[0], 0, 0)), spec], out_specs=spec)
    return pl.pallas_call(body, name=f"pair_sum_{a}", grid_spec=grid_spec, out_shape=jax.ShapeDtypeStruct((nq, hr, cc), out_dtype),
                          compiler_params=_cparams())(c_arr, g, land)


def _chip_sum(a, parts, own, me_arr, after):
    _, hr, cc = parts.shape

    def body(me_ref, p_ref, own_ref, after_ref, o_ref):
        for mine in range(4):
            @pl.when(me_ref[0] == mine)
            def _():
                term = lambda j: (own_ref if j == mine else p_ref.at[j])[...].astype(F32)
                o_ref[...] = ((term(0) + term(1)) + term(2)) + term(3)

    own_spec = (pl.BlockSpec((None, hr, cc), lambda i, me: (me[0], 0, 0)) if own.ndim == 3
                else pl.BlockSpec((hr, cc), lambda i, me: (0, 0)))
    grid_spec = pltpu.PrefetchScalarGridSpec(
        num_scalar_prefetch=1, grid=(1,),
        in_specs=[pl.BlockSpec((4, hr, cc), lambda i, me: (0, 0, 0)), own_spec, ANY],
        out_specs=pl.BlockSpec((hr, cc), lambda i, me: (0, 0)))
    return pl.pallas_call(body, name=f"chip_sum_{a}", grid_spec=grid_spec, out_shape=jax.ShapeDtypeStruct((hr, cc), F32),
                          compiler_params=_cparams())(me_arr, parts, own, after)


def _row_block(rows, cols, limit=1 << 20):
    best = 8
    for tr in range(8, rows + 1, 8):
        if rows % tr == 0 and tr * cols * 4 <= limit:
            best = tr
    return best


def _adamw(name, w, g_mine, g_other, m, v, c_arr):
    r, c = w.shape
    hr, cg = g_mine.shape
    tr = hr if r % hr == 0 and hr * cg * 4 <= (3 << 19) else math.gcd(_row_block(hr, cg), r)
    per_half = hr // tr
    bc1 = 1.0 - ADAM_B1 ** ADAM_STEP
    bc2 = 1.0 - ADAM_B2 ** ADAM_STEP

    def body(c_ref, w_ref, gm_ref, go_ref, m_ref, v_ref, go, do, mo, vo):
        gv = jnp.where(pl.program_id(0) // per_half == c_ref[0], gm_ref[:, 0:c], go_ref[:, 0:c])
        mn = ADAM_B1 * m_ref[...] + (1.0 - ADAM_B1) * gv
        vn = ADAM_B2 * v_ref[...] + (1.0 - ADAM_B2) * (gv * gv)
        go[...] = gv
        mo[...] = mn
        vo[...] = vn
        do[...] = -ADAM_LR * ((mn / bc1) / (jnp.sqrt(vn / bc2) + ADAM_EPS) + ADAM_WD * w_ref[...])

    blk = pl.BlockSpec((tr, c), lambda i, cr: (i, 0))
    gblk = pl.BlockSpec((tr, cg), lambda i, cr: (i % per_half, 0))
    grid_spec = pltpu.PrefetchScalarGridSpec(num_scalar_prefetch=1, grid=(r // tr,), in_specs=[blk, gblk, gblk, blk, blk],
                                             out_specs=[blk] * 4)
    return pl.pallas_call(body, name=f"adamw_{name}", grid_spec=grid_spec, out_shape=[jax.ShapeDtypeStruct((r, c), F32)] * 4,
                          compiler_params=_cparams())(c_arr, w, g_mine, g_other, m, v)


def _rows128(a):
    return a.reshape(-1, LANES)


def _pad_rows(a, rows):
    return jnp.pad(a, ((0, rows - a.shape[0]), (0, 0)))


def kernel(x, w_in, sgu_ln_g, sgu_ln_b, w_s, b_s, conv_w, conv_b, conv_ln_g, conv_ln_b, w_out, ln1_g, ln1_b, w_gate, w_up, w_down, ln2_g, ln2_b, loss_target, m_w_in, m_sgu_ln_g, m_sgu_ln_b, m_w_s, m_b_s, m_conv_w, m_conv_b, m_conv_ln_g, m_conv_ln_b, m_w_out, m_ln1_g, m_ln1_b, m_w_gate, m_w_up, m_w_down, m_ln2_g, m_ln2_b, v_w_in, v_sgu_ln_g, v_sgu_ln_b, v_w_s, v_b_s, v_conv_w, v_conv_b, v_conv_ln_g, v_conv_ln_b, v_w_out, v_ln1_g, v_ln1_b, v_w_gate, v_w_up, v_w_down, v_ln2_g, v_ln2_b):
    depth, d, q = w_in.shape
    assert depth == 1 and x.shape[0] == 1
    t = x.shape[1]
    heads = w_s.shape[1]
    kw, cshard = conv_w.shape[1], conv_w.shape[2]
    fs = w_gate.shape[2]
    slabs = _hidden_slabs(4 * fs)
    n_pairs = q // LANES
    assert heads * HEAD_DIM == q and q % LANES == 0 and w_s.shape[2] == CHUNK and 4 * cshard == q and kw - 1 <= HALO
    alpha = (2.0 * depth) ** 0.25
    tm = min(512, t)
    assert t % tm == 0 and tm % CHUNK == 0
    x2, tgt = x[0], loss_target[0]
    mx, my, mc = _coords()
    me = 2 * mx + my
    c_arr = jnp.reshape(mc, (1,)).astype(jnp.int32)

    kwp = -(-kw // 16) * 16
    me_arr = jnp.reshape(me, (1,)).astype(jnp.int32)
    wi, wo, wg, wu, wd, cw4 = _prep(me_arr, w_in[0], w_out[0], w_gate[0].T, w_up[0].T, w_down[0], conv_w[0], kwp)
    wi, wo, cw4 = _gather_shards([wi, wo, cw4])
    wo = wo.reshape(d, d)
    cw = jnp.transpose(cw4, (1, 0, 2)).reshape(kwp, q)
    cwf = _pad_rows(cw[:kw][::-1], kwp)
    tabs = {name: jnp.asarray(tab) for name, tab in _dft_tables(kw, kwp, q).items()}

    wm = jnp.where(jnp.tril(jnp.ones((CHUNK, CHUNK), bool)), w_s[0], 0.0)
    wst = wm.reshape(n_pairs, 2 * CHUNK, CHUNK).astype(BF16)
    wstt = jnp.transpose(wm, (0, 2, 1)).reshape(n_pairs, 2 * CHUNK, CHUNK).astype(BF16)
    bmat = jnp.repeat(b_s[0].T, HEAD_DIM, axis=1)
    vq = _pad_rows(jnp.concatenate([sgu_ln_g, sgu_ln_b, conv_b, conv_ln_g, conv_ln_b], axis=0), 8)
    vd = _pad_rows(jnp.concatenate([ln1_g, ln1_b, ln2_g, ln2_b], axis=0), 8)

    *saved, wg, wu, wd = _fwd_mix(x2, wi, wo, wst, bmat, cwf, tabs, vq, vd, [wg, wu, wd], alpha, tm)
    wg, wu, wd = (w.reshape(4 * fs, d) for w in (wg, wu, wd))
    *acts, x1b, dr2, loss_part, dg2, db2 = _fwd_mlp(saved[3], tgt, wg, wu, wd, vd, alpha, slabs, tm)
    mlp_grads = None
    for j, slab in enumerate(slabs):
        mlp_grads = _bwd_mlp_slab(j, slab, dr2, mlp_grads, x1b, acts[j], acts[len(slabs) + j], wg, wu, wd, alpha, tm)
    dx1 = mlp_grads[0]
    mlp_halves = [b.reshape(4, 2, fs // 2, d) for b in mlp_grads[1:]]
    landed = _pair_swap("pair_swap_mlp", mlp_halves[3:])
    mlp_sums = [_pair_sum(f"mlp{a}", h, l, c_arr, BF16) for a, (h, l) in enumerate(zip(mlp_halves[:3], landed))]
    mlp_started, token = _exchange_start("exchange_mlp_start", mlp_sums)
    grad_x, dwi, dwo, dws, dbs, dcw, dvq, dvd = _bwd_mix(dx1, saved, wi, wo, wstt, cwf, tabs, vq, vd, token, alpha, tm)
    mlp_sums, mlp_parts = _exchange_wait("exchange_mlp_wait", mlp_started, [dwo])

    dws = jnp.where(jnp.tril(jnp.ones((CHUNK, CHUNK), bool)), dws.reshape(heads, CHUNK, CHUNK), 0.0)
    dvd = dvd.at[VD_LN2_G].set(dg2[0]).at[VD_LN2_B].set(db2[0])
    pieces = [_rows128(dws), dbs[:, :heads].T, _rows128(dcw), _rows128(dvq), _rows128(dvd), loss_part]
    sizes = [p.shape[0] for p in pieces]
    rows = -(-sum(sizes) // 16) * 16
    small = _pad_rows(jnp.concatenate(pieces, axis=0), rows)

    big = [dwi, dwo.reshape(4, d // 4, d)]
    halves = [b.reshape(4, 2, b.shape[1] // 2, b.shape[2]) for b in big] + [small.reshape(2, rows // 2, LANES)]
    landed = _pair_swap("pair_swap_mix", halves)
    sums = [_pair_sum(f"mix{a}", h, l, c_arr, BF16) for a, (h, l) in enumerate(zip(halves[:-1], landed[:-1]))]
    sums.append(_pair_sum("small", halves[-1][None], landed[-1][None], c_arr, F32)[0])
    mix_started, token = _exchange_start("exchange_mix_start", sums)

    out, raw = {}, {}

    def finish(first, names, parts, sums, after):
        mine = [_chip_sum(first + a, p, s, me_arr, after) for a, (p, s) in enumerate(zip(parts, sums))]
        other = _pair_gather(f"pair_gather_{first}", mine)
        for a, nm in enumerate(names):
            w_, m_, v_ = weights[nm]
            if nm in ("w_gate", "w_up"):
                raw[nm] = _adamw(nm, w_[0].T, mine[a], other[a], m_[0].T, v_[0].T, c_arr)
                out[nm] = [o.T for o in raw[nm]]
            else:
                raw[nm] = out[nm] = _adamw(nm, w_[0], mine[a], other[a], m_[0], v_[0], c_arr)
        return mine[-1], other[-1]

    weights = {"w_in": (w_in, m_w_in, v_w_in), "w_out": (w_out, m_w_out, v_w_out), "w_gate": (w_gate, m_w_gate, v_w_gate),
               "w_up": (w_up, m_w_up, v_w_up), "w_down": (w_down, m_w_down, v_w_down)}
    finish(2, ["w_gate", "w_up", "w_down"], mlp_parts, mlp_sums, token)
    sums, parts = _exchange_wait("exchange_mix_wait", mix_started, [raw[nm][1] for nm in ("w_gate", "w_up", "w_down")])
    small_mine, small_other = finish(5, ["w_in", "w_out"], parts, sums, parts[0])

    def pack(ws, bs, vqs, vds):
        ps = [_rows128(ws[0]), bs[0], jnp.zeros((sizes[2], LANES), F32),
              _rows128(_pad_rows(jnp.concatenate(vqs, axis=0), 8)), _rows128(_pad_rows(jnp.concatenate(vds, axis=0), 8)),
              jnp.zeros((sizes[5], LANES), F32)]
        return _pad_rows(jnp.concatenate(ps, axis=0), rows)

    packed = _adamw(
        "small",
        pack(w_s, b_s, [sgu_ln_g, sgu_ln_b, conv_b, conv_ln_g, conv_ln_b], [ln1_g, ln1_b, ln2_g, ln2_b]), small_mine, small_other,
        pack(m_w_s, m_b_s, [m_sgu_ln_g, m_sgu_ln_b, m_conv_b, m_conv_ln_g, m_conv_ln_b], [m_ln1_g, m_ln1_b, m_ln2_g, m_ln2_b]),
        pack(v_w_s, v_b_s, [v_sgu_ln_g, v_sgu_ln_b, v_conv_b, v_conv_ln_g, v_conv_ln_b], [v_ln1_g, v_ln1_b, v_ln2_g, v_ln2_b]),
        c_arr)

    offs = [sum(sizes[:i]) for i in range(len(sizes))]
    g_cw_full = packed[0][offs[2]:offs[2] + sizes[2]].reshape(kwp, q)
    g_cw = lax.dynamic_slice(g_cw_full, (0, me * cshard), (kwp, cshard))
    out["conv_w"] = _adamw("conv_w", _pad_rows(conv_w[0], kwp), g_cw, g_cw, _pad_rows(m_conv_w[0], kwp), _pad_rows(v_conv_w[0], kwp), c_arr)
    out["conv_w"] = [o[:kw] for o in out["conv_w"]]

    def unpack(p):
        vq_o = p[offs[3]:offs[3] + sizes[3]].reshape(8, q)
        vd_o = p[offs[4]:offs[4] + sizes[4]].reshape(8, d)
        return {"w_s": p[offs[0]:offs[0] + sizes[0]].reshape(heads, CHUNK, CHUNK), "b_s": p[offs[1]:offs[1] + sizes[1]],
                "sgu_ln_g": vq_o[VQ_SGU_G], "sgu_ln_b": vq_o[VQ_SGU_B], "conv_b": vq_o[VQ_CONV_B],
                "conv_ln_g": vq_o[VQ_CLN_G], "conv_ln_b": vq_o[VQ_CLN_B],
                "ln1_g": vd_o[VD_LN1_G], "ln1_b": vd_o[VD_LN1_B], "ln2_g": vd_o[VD_LN2_G], "ln2_b": vd_o[VD_LN2_B]}

    small_out = [unpack(p) for p in packed]
    loss = packed[0][offs[5], 0]
    names = ["w_in", "sgu_ln_g", "sgu_ln_b", "w_s", "b_s", "conv_w", "conv_b", "conv_ln_g", "conv_ln_b", "w_out",
             "ln1_g", "ln1_b", "w_gate", "w_up", "w_down", "ln2_g", "ln2_b"]
    result = [loss, grad_x[None]]
    for kind in range(4):
        for nm in names:
            val = out[nm][kind] if nm in out else small_out[kind][nm]
            result.append(val[None])
    return tuple(result)
```

```python
import functools
import math

import jax
import numpy as np
import jax.numpy as jnp
from jax import lax
from jax.experimental import pallas as pl
from jax.experimental.pallas import tpu as pltpu

F32 = jnp.float32
BF16 = jnp.bfloat16

LN_EPS = 1e-5
HEAD_DIM = 64
CHUNK = 128
HALO = 32
LANES = 128
MXU_N = 256
ADAM_LR, ADAM_B1, ADAM_B2, ADAM_EPS, ADAM_WD, ADAM_STEP = 0.001, 0.9, 0.999, 1e-08, 0.01, 10
VMEM_LIMIT = 63 * 1024 * 1024
MESH_AXES = ("x", "y", "c")
MESH_ID = pl.DeviceIdType.MESH


def _dot(a, b):
    return jnp.dot(a, b, preferred_element_type=F32)


def _dot_nt(a, b):
    return lax.dot_general(a, b, (((1,), (1,)), ((), ())), preferred_element_type=F32)


def _dot_tn(a, b):
    return lax.dot_general(a, b, (((0,), (0,)), ((), ())), preferred_element_type=F32)


def _sigmoid(v):
    return 1.0 / (1.0 + jnp.exp(-v))


def _gelu(v):
    cdf = 0.5 * (1.0 + lax.erf(v * (1.0 / math.sqrt(2.0))))
    pdf = jnp.exp(-0.5 * v * v) * (1.0 / math.sqrt(2.0 * math.pi))
    return v * cdf, cdf + v * pdf


def _ln_stats(v):
    mu = jnp.mean(v, axis=-1, keepdims=True)
    d = v - mu
    rstd = lax.rsqrt(jnp.mean(d * d, axis=-1, keepdims=True) + LN_EPS)
    return d * rstd, rstd


def _ln_bwd(dxhat, xhat, rstd):
    m1 = jnp.mean(dxhat, axis=-1, keepdims=True)
    m2 = jnp.mean(dxhat * xhat, axis=-1, keepdims=True)
    return rstd * (dxhat - m1 - xhat * m2)


def _colsum(v):
    return jnp.sum(v, axis=0, keepdims=True)


def _pair_lanes(v, nc, p):
    return jnp.concatenate([v[c * CHUNK:(c + 1) * CHUNK, p * LANES:(p + 1) * LANES] for c in range(nc)], axis=1)


def _unpair(parts, nc):
    rows = [jnp.concatenate([part[:, c * LANES:(c + 1) * LANES] for part in parts], axis=1) for c in range(nc)]
    return jnp.concatenate(rows, axis=0)


def _low_head(nc):
    lane = lax.broadcasted_iota(jnp.int32, (CHUNK, nc * LANES), 1)
    return (lane & (LANES - 1)) < HEAD_DIM


def _mix(wst_ref, v, nc, n_pairs):
    vb = v.astype(BF16)
    low = _low_head(nc)
    parts = []
    for p in range(n_pairs):
        r = _dot(wst_ref[p], _pair_lanes(vb, nc, p))
        parts.append(jnp.where(low, r[:CHUNK], r[CHUNK:]))
    return _unpair(parts, nc)


def _mix_wgrad(dm, vn, nc, n_pairs):
    low = _low_head(nc)
    vb = vn.astype(BF16)
    out = []
    for p in range(n_pairs):
        a = _pair_lanes(dm, nc, p)
        lhs = jnp.concatenate([jnp.where(low, a, 0.0), jnp.where(low, 0.0, a)], axis=0).astype(BF16)
        out.append(_dot_nt(lhs, _pair_lanes(vb, nc, p)))
    return out


SUBLANES = 8


CONV_BLOCK = 256
DFT_N = CONV_BLOCK + HALO
DFT_F = -(-(DFT_N // 2 + 1) // SUBLANES) * SUBLANES


def _terms(m, exact):
    hi = m.astype(np.float32).astype(BF16)
    lo = (m.astype(np.float32) - hi.astype(np.float32)).astype(BF16)
    return np.concatenate([hi, hi, lo] if exact else [hi, hi], axis=1)


def _split(v, exact=False):
    hi = v.astype(BF16)
    lo = (v - hi.astype(F32)).astype(BF16)
    return jnp.concatenate([hi, lo, hi] if exact else [hi, lo], axis=0)


def _dft_tables(kw, kwp, q):
    nf = DFT_N // 2 + 1
    ang = 2.0 * np.pi * np.arange(nf)[:, None] * np.arange(DFT_N)[None, :] / DFT_N
    fwd = np.zeros((2 * DFT_F, DFT_N))
    fwd[:nf], fwd[DFT_F:DFT_F + nf] = np.cos(ang), -np.sin(ang)
    weight = np.full((nf, 1), 2.0 / DFT_N)
    weight[0] = weight[-1] = 1.0 / DFT_N
    inv = np.zeros((DFT_N, 2 * DFT_F))
    inv[:, :nf], inv[:, DFT_F:DFT_F + nf] = (np.cos(ang) * weight).T, (-np.sin(ang) * weight).T
    inv_taps = np.zeros((kwp, 2 * DFT_F))
    inv_taps[:kw] = inv[kw - 1::-1][:kw]
    shift = np.zeros((2 * DFT_F, q), np.float32)
    shift[:nf], shift[DFT_F:DFT_F + nf] = np.cos(ang[:, HALO:HALO + 1]), -np.sin(ang[:, HALO:HALO + 1])
    return {"fwd": _terms(fwd, False), "fwd_halo": _terms(fwd[:, CONV_BLOCK:], False), "shift": shift,
            "inv_out": _terms(inv[HALO:HALO + CONV_BLOCK], False), "inv_in": _terms(inv[:CONV_BLOCK], False),
            "taps": _terms(fwd[:, :kwp], True), "inv_taps": _terms(inv_taps, True)}


def _cmul(a, b, conj_b=False):
    ar, ai, br, bi = a[:DFT_F], a[DFT_F:], b[:DFT_F], b[DFT_F:]
    if conj_b:
        return jnp.concatenate([ar * br + ai * bi, ai * br - ar * bi], axis=0)
    return jnp.concatenate([ar * br - ai * bi, ar * bi + ai * br], axis=0)


def _interleave(sub_tiles):
    waiting, live = list(sub_tiles), []
    while waiting or live:
        if waiting:
            live.append(waiting.pop(0))
        for g in list(live):
            try:
                next(g)
            except StopIteration:
                live.remove(g)


def _start_copies(sems, pairs, first=0):
    copies = [pltpu.make_async_copy(src, dst, sems.at[first + k]) for k, (src, dst) in enumerate(pairs)]
    for cp in copies:
        cp.start()
    return copies


def _cparams():
    return pltpu.CompilerParams(dimension_semantics=("arbitrary",), vmem_limit_bytes=VMEM_LIMIT)


def _full(shape):
    return pl.BlockSpec(shape, lambda i: (0,) * len(shape))


ANY = pl.BlockSpec(memory_space=pl.ANY)

VQ_SGU_G, VQ_SGU_B, VQ_CONV_B, VQ_CLN_G, VQ_CLN_B = range(5)
VD_LN1_G, VD_LN1_B, VD_LN2_G, VD_LN2_B = range(4)
RS_LN1, RS_SGU, RS_CONV = range(3)
RS_COLS = 8


def _fwd_mix(x, wi, wo, wst, bmat, cwf, tabs, vq, vd, mlp_w, alpha, tm):
    t, d = x.shape
    q = wi.shape[2]
    nc, n_pairs = CONV_BLOCK // CHUNK, q // LANES
    n = t // tm
    n_in, n_saved = 11, 12
    assert tm % CONV_BLOCK == 0

    def body(x_ref, wi_hbm, wo_hbm, wst_ref, bmat_ref, cwf_ref, fwd_ref, taps_ref, inv_ref, vq_ref, vd_ref, *rest):
        (xb_ref, pag_ref, y_ref, xh_ref, rs_ref, zu_ref, mg_ref, vhat_ref, gv_ref, vnb_ref, yhat_ref, hf_ref) = rest[3:3 + n_saved]
        gathered = rest[3 + n_saved:6 + n_saved]
        wi_v, wo_v, hb_ref, gf_ref, send_sems, recv_sems, copy_sems = rest[6 + n_saved:]
        step = pl.program_id(0)

        @pl.when(step == 0)
        def _():
            loads = _start_copies(copy_sems, [(wi_hbm, wi_v), (wo_hbm, wo_v)])
            _Gather(gathered, send_sems, recv_sems).start()
            hb_ref[...] = jnp.zeros_like(hb_ref)
            gf_ref[...] = _dot(taps_ref[...], _split(cwf_ref[...], True))
            for cp in loads:
                cp.wait()

        @pl.when(step == (3 * n) // 4)
        def _():
            _Gather(gathered, send_sems, recv_sems).forward()

        def sub_tile(b):
            rows = slice(b * CONV_BLOCK, (b + 1) * CONV_BLOCK)
            xv = x_ref[rows, :]
            xb = xv.astype(BF16)
            xb_ref[rows, :] = xb
            pu, pv, pa, pg = (_dot(xb, wi_v[j]) for j in range(4))
            yield
            pag_ref[rows, 0:q] = pa.astype(BF16)
            pag_ref[rows, q:2 * q] = pg.astype(BF16)
            zu, gu = _gelu(pu)
            zv, gv = _gelu(pv)
            vhat, rstd_v = _ln_stats(zv)
            vnb = (vhat * vq_ref[VQ_SGU_G:VQ_SGU_G + 1, :] + vq_ref[VQ_SGU_B:VQ_SGU_B + 1, :]).astype(BF16)
            hb_ref[HALO + b * CONV_BLOCK:HALO + (b + 1) * CONV_BLOCK, :] = pa * _sigmoid(pg)
            yield
            mixed = _mix(wst_ref, vnb, nc, n_pairs) + jnp.concatenate([bmat_ref[...]] * nc, axis=0)
            spectrum = _dot(fwd_ref[...], _split(hb_ref[b * CONV_BLOCK:b * CONV_BLOCK + DFT_N, :]))
            yield
            y_ref[rows, 0:q] = (zu * mixed).astype(BF16)
            zu_ref[rows, :] = zu
            mg_ref[rows, :] = mixed * gu
            vhat_ref[rows, :] = vhat
            gv_ref[rows, :] = gv
            vnb_ref[rows, :] = vnb
            hf_ref[b * 2 * DFT_F:(b + 1) * 2 * DFT_F, :] = spectrum
            product = _split(_cmul(gf_ref[...], spectrum))
            yield
            yc = _dot(inv_ref[...], product) + vq_ref[VQ_CONV_B:VQ_CONV_B + 1, :]
            yield
            yhat, rstd_c = _ln_stats(yc)
            yhat_ref[rows, :] = yhat
            yn = yhat * vq_ref[VQ_CLN_G:VQ_CLN_G + 1, :] + vq_ref[VQ_CLN_B:VQ_CLN_B + 1, :]
            y_ref[rows, q:2 * q] = (yn * _sigmoid(yn)).astype(BF16)
            yield
            r1 = alpha * xv + _dot(y_ref[rows, :], wo_v[...])
            yield
            xhat, rstd1 = _ln_stats(r1)
            xh_ref[rows, :] = xhat
            col = lax.broadcasted_iota(jnp.int32, (CONV_BLOCK, RS_COLS), 1)
            rs_ref[rows, :] = jnp.where(col == RS_LN1, rstd1, jnp.where(col == RS_SGU, rstd_v, jnp.where(col == RS_CONV, rstd_c, 0.0)))

        _interleave([sub_tile(b) for b in range(tm // CONV_BLOCK)])
        hb_ref[0:HALO, :] = hb_ref[tm:tm + HALO, :]

        @pl.when(step == n - 1)
        def _():
            _Gather(gathered, send_sems, recv_sems).finish()

    row = lambda w: pl.BlockSpec((tm, w), lambda i: (i, 0))
    widths = [(d, BF16), (2 * q, BF16), (d, BF16), (d, F32), (RS_COLS, F32), (q, F32), (q, F32), (q, F32), (q, F32), (q, BF16), (q, F32)]
    assert len(widths) + 1 == n_saved
    small_ins = [wst, bmat, cwf, tabs["fwd"], tabs["taps"], tabs["inv_out"], vq, vd]
    return pl.pallas_call(
        body, name="fwd_mix", grid=(n,),
        in_specs=[row(d), ANY, ANY] + [_full(a.shape) for a in small_ins] + [ANY] * 3,
        out_specs=[row(w) for w, _ in widths] + [pl.BlockSpec((tm // CONV_BLOCK * 2 * DFT_F, q), lambda i: (i, 0))] + [ANY] * 3,
        out_shape=[jax.ShapeDtypeStruct((t, w), dt) for w, dt in widths] + [jax.ShapeDtypeStruct((t // CONV_BLOCK * 2 * DFT_F, q), F32)]
        + [jax.ShapeDtypeStruct(b.shape, b.dtype) for b in mlp_w],
        scratch_shapes=[pltpu.VMEM(wi.shape, BF16), pltpu.VMEM(wo.shape, BF16), pltpu.VMEM((HALO + tm, q), F32),
                        pltpu.VMEM((2 * DFT_F, q), F32)] + _gather_sems(3) + [pltpu.SemaphoreType.DMA((2,))],
        input_output_aliases={n_in + a: n_saved + a for a in range(3)},
        compiler_params=_cparams(),
    )(x, wi, wo, *small_ins, *mlp_w)


def _hidden_slabs(f):
    assert f % MXU_N == 0
    tiles = f // MXU_N
    sizes = [(tiles // 4 + (1 if j < tiles % 4 else 0)) * MXU_N for j in range(4)]
    return [(sum(sizes[:j]), sz) for j, sz in enumerate(sizes) if sz]


def _fwd_mlp(xh1, tgt, wg, wu, wd, vd, alpha, slabs, tm):
    t, d = xh1.shape
    n = t // tm
    ns = len(slabs)
    half = tm // 2 if tm % 32 == 0 else tm

    def body(xh_ref, tgt_ref, wg_hbm, wu_hbm, wd_hbm, vd_ref, *rest):
        gp_refs, up_refs = rest[:ns], rest[ns:2 * ns]
        x1b_ref, dr2_ref, loss_ref, dg2_ref, db2_ref, wg_v, wu_v, wd_v, copy_sems = rest[2 * ns:]

        @pl.when(pl.program_id(0) == 0)
        def _():
            loads = _start_copies(copy_sems, [(wg_hbm, wg_v), (wu_hbm, wu_v), (wd_hbm, wd_v)])
            loss_ref[...] = jnp.zeros_like(loss_ref)
            dg2_ref[...] = jnp.zeros_like(dg2_ref)
            db2_ref[...] = jnp.zeros_like(db2_ref)
            for cp in loads:
                cp.wait()

        g2 = vd_ref[VD_LN2_G:VD_LN2_G + 1, :]

        for r0 in range(0, tm, half):
            rows = slice(r0, r0 + half)
            x1 = xh_ref[rows, :] * vd_ref[VD_LN1_G:VD_LN1_G + 1, :] + vd_ref[VD_LN1_B:VD_LN1_B + 1, :]
            x1b = x1.astype(BF16)
            x1b_ref[rows, :] = x1b
            acc = alpha * x1
            for (off, sz), gp_ref, up_ref in zip(slabs, gp_refs, up_refs):
                gp = _dot_nt(x1b, wg_v[off:off + sz, :])
                up = _dot_nt(x1b, wu_v[off:off + sz, :])
                gp_ref[rows, :] = gp.astype(BF16)
                up_ref[rows, :] = up.astype(BF16)
                acc = acc + _dot((gp * _sigmoid(gp) * up).astype(BF16), wd_v[off:off + sz, :])
            xh2, rstd2 = _ln_stats(acc)
            err = xh2 * g2 + vd_ref[VD_LN2_B:VD_LN2_B + 1, :] - tgt_ref[rows, :]
            loss_ref[...] += _colsum(jnp.sum(err * err, axis=1, keepdims=True)) * (0.5 / d)
            dy = err * (1.0 / d)
            dg2_ref[...] += _colsum(dy * xh2)
            db2_ref[...] += _colsum(dy)
            dr2_ref[rows, :] = _ln_bwd(dy * g2, xh2, rstd2)

    row = lambda w: pl.BlockSpec((tm, w), lambda i: (i, 0))
    act = [sz for _, sz in slabs] * 2
    return pl.pallas_call(
        body, name="fwd_mlp", grid=(n,),
        in_specs=[row(d), row(d), ANY, ANY, ANY, _full(vd.shape)],
        out_specs=[row(sz) for sz in act] + [row(d), row(d), _full((8, LANES)), _full((1, d)), _full((1, d))],
        out_shape=[jax.ShapeDtypeStruct((t, sz), BF16) for sz in act]
        + [jax.ShapeDtypeStruct((t, d), BF16), jax.ShapeDtypeStruct((t, d), F32),
           jax.ShapeDtypeStruct((8, LANES), F32), jax.ShapeDtypeStruct((1, d), F32), jax.ShapeDtypeStruct((1, d), F32)],
        scratch_shapes=[pltpu.VMEM(wg.shape, BF16), pltpu.VMEM(wu.shape, BF16), pltpu.VMEM(wd.shape, BF16), pltpu.SemaphoreType.DMA((3,))],
        compiler_params=_cparams(),
    )(xh1, tgt, wg, wu, wd, vd)


def _bwd_mlp_slab(j, slab, dr2, prev, x1b, gp, up, wg, wu, wd, alpha, tm):
    t, d = dr2.shape
    off, sz = slab
    n = t // tm
    first = prev is None

    def body(*refs):
        if first:
            dr_ref, x1b_ref, gp_ref, up_ref, wg_hbm, wu_hbm, wd_hbm = refs[:7]
        else:
            dr_ref, dxp_ref, x1b_ref, gp_ref, up_ref, wg_hbm, wu_hbm, wd_hbm = refs[:8]
        dx_ref, dwg_hbm, dwu_hbm, dwd_hbm, dwg16_hbm, dwu16_hbm, dwd16_hbm, ag, au, ad, wg_v, wu_v, wd_v, copy_sems = refs[-14:]

        @pl.when(pl.program_id(0) == 0)
        def _():
            loads = _start_copies(copy_sems, [(src.at[pl.ds(off, sz)], dst) for src, dst in ((wg_hbm, wg_v), (wu_hbm, wu_v), (wd_hbm, wd_v))])
            ag[...] = jnp.zeros_like(ag)
            au[...] = jnp.zeros_like(au)
            ad[...] = jnp.zeros_like(ad)
            for cp in loads:
                cp.wait()

        dr = dr_ref[...]
        drb = dr.astype(BF16)
        x1b = x1b_ref[...]
        gpv = gp_ref[...].astype(F32)
        upv = up_ref[...].astype(F32)
        dh = _dot_nt(drb, wd_v[...])
        sg = _sigmoid(gpv)
        silu = gpv * sg
        ad[...] += _dot_tn((silu * upv).astype(BF16), drb)
        dgp = (dh * upv * (sg * (1.0 + gpv * (1.0 - sg)))).astype(BF16)
        dup = (dh * silu).astype(BF16)
        ag[...] += _dot_tn(dgp, x1b)
        au[...] += _dot_tn(dup, x1b)
        base = alpha * dr if first else dxp_ref[...]
        dx_ref[...] = base + _dot(dgp, wg_v[...]) + _dot(dup, wu_v[...])

        @pl.when(pl.program_id(0) == n - 1)
        def _():
            rows = pl.ds(off, sz)
            stores = _start_copies(copy_sems, [(ag, dwg_hbm.at[rows]), (au, dwu_hbm.at[rows]), (ad, dwd_hbm.at[rows])])
            for acc, stage in ((ag, wg_v), (au, wu_v), (ad, wd_v)):
                stage[...] = acc[...].astype(BF16)
            stores += _start_copies(copy_sems, [(wg_v, dwg16_hbm.at[rows]), (wu_v, dwu16_hbm.at[rows]), (wd_v, dwd16_hbm.at[rows])], first=3)
            for cp in stores:
                cp.wait()

    row = lambda w: pl.BlockSpec((tm, w), lambda i: (i, 0))
    ins = [dr2] + ([] if first else [prev[0]]) + [x1b, gp, up, wg, wu, wd] + ([] if first else list(prev[1:]))
    in_specs = [row(d)] + ([] if first else [row(d)]) + [row(d), row(sz), row(sz), ANY, ANY, ANY] + ([] if first else [ANY] * 6)
    return pl.pallas_call(
        body, name=f"bwd_mlp_{j}", grid=(n,),
        in_specs=in_specs,
        out_specs=[row(d)] + [ANY] * 6,
        out_shape=[jax.ShapeDtypeStruct((t, d), F32)] + [jax.ShapeDtypeStruct(wg.shape, F32)] * 3 + [jax.ShapeDtypeStruct(wg.shape, BF16)] * 3,
        scratch_shapes=[pltpu.VMEM((sz, d), F32)] * 3 + [pltpu.VMEM((sz, d), BF16)] * 3 + [pltpu.SemaphoreType.DMA((6,))],
        input_output_aliases={} if first else {8 + a: 1 + a for a in range(6)},
        compiler_params=_cparams(),
    )(*ins)


def _bwd_mix(dx1, saved, wi, wo, wstt, cwf, tabs, vq, vd, token, alpha, tm):
    xb, pag, y, xh1, rs, zu_s, mg_s, vhat_s, gv_s, vnb_s, yhat_s, hf_s = saved
    t, d = xh1.shape
    q = wi.shape[2]
    nc, n_pairs = CONV_BLOCK // CHUNK, q // LANES
    n = t // tm
    nb = tm // CONV_BLOCK
    assert tm % CONV_BLOCK == 0

    def body(dx1_ref, xb_ref, pag_ref, y_ref, xh_ref, rs_ref, zu_ref, mg_ref, vhat_ref, gv_ref, vnb_ref, yhat_ref, hf_ref,
             wi_hbm, wo_hbm, wstt_ref, cwf_ref, fwd_ref, fwd_halo_ref, shift_ref, taps_ref, inv_ref, inv_taps_ref, vq_ref, vd_ref,
             token_ref,
             gx_ref, dwi_hbm, dwo_hbm, dws_ref, dbs_ref, dcw_ref, dvq_ref, dvd_ref,
             wi_v, wo_v, awi, awo, dyb_ref, later_ref, dbm_ref, gf_ref, dgf_ref, copy_sems):
        i = pl.program_id(0)

        @pl.when(i == 0)
        def _():
            loads = _start_copies(copy_sems, [(wi_hbm, wi_v), (wo_hbm, wo_v)])
            for r in (awi, awo, dws_ref, dbm_ref, dgf_ref, dvq_ref, dvd_ref, dyb_ref, later_ref):
                r[...] = jnp.zeros_like(r)
            gf_ref[...] = _dot(taps_ref[...], _split(cwf_ref[...], True))
            for cp in loads:
                cp.wait()

        dr1b_parts, dproj_parts = [None] * nb, [None] * nb

        def sub_tile(b):
            rows = slice(b * CONV_BLOCK, (b + 1) * CONV_BLOCK)
            dx1v = dx1_ref[rows, :]
            xh = xh_ref[rows, :]
            rsv = rs_ref[rows, :]
            dvd_ref[VD_LN1_G:VD_LN1_G + 1, :] += _colsum(dx1v * xh)
            dvd_ref[VD_LN1_B:VD_LN1_B + 1, :] += _colsum(dx1v)
            dr1 = _ln_bwd(dx1v * vd_ref[VD_LN1_G:VD_LN1_G + 1, :], xh, rsv[:, RS_LN1:RS_LN1 + 1])
            dr1b = dr1.astype(BF16)
            yield
            dy = _dot_nt(dr1b, wo_v[...])
            yield
            vhat = vhat_ref[rows, :]
            sgu_g = vq_ref[VQ_SGU_G:VQ_SGU_G + 1, :]
            doa = dy[:, 0:q]
            dm = doa * zu_ref[rows, :]
            dpu = (doa * mg_ref[rows, :]).astype(BF16)
            acc = dm[0:CHUNK]
            for c in range(1, nc):
                acc = acc + dm[c * CHUNK:(c + 1) * CHUNK]
            dbm_ref[...] += acc
            pa = pag_ref[rows, 0:q].astype(F32)
            sg = _sigmoid(pag_ref[rows, q:2 * q].astype(F32))
            yhat = yhat_ref[rows, :]
            cln_g = vq_ref[VQ_CLN_G:VQ_CLN_G + 1, :]
            yn = yhat * cln_g + vq_ref[VQ_CLN_B:VQ_CLN_B + 1, :]
            sy = _sigmoid(yn)
            dyn = dy[:, q:2 * q] * (sy * (1.0 + yn * (1.0 - sy)))
            dvq_ref[VQ_CLN_G:VQ_CLN_G + 1, :] += _colsum(dyn * yhat)
            dvq_ref[VQ_CLN_B:VQ_CLN_B + 1, :] += _colsum(dyn)
            dyc = _ln_bwd(dyn * cln_g, yhat, rsv[:, RS_CONV:RS_CONV + 1])
            dvq_ref[VQ_CONV_B:VQ_CONV_B + 1, :] += _colsum(dyc)
            dyb_ref[b, 0:CONV_BLOCK, :] = dyc
            yield
            wgrads = _mix_wgrad(dm, vnb_ref[rows, :], nc, n_pairs)
            dvn = _mix(wstt_ref, dm, nc, n_pairs)
            own = _dot(fwd_ref[...], _split(dyb_ref[b]))
            with_later = own + _dot(fwd_halo_ref[...], _split(later_ref[...]))
            later_ref[...] = dyb_ref[b, 0:HALO, :]
            yield
            for p, g in enumerate(wgrads):
                dws_ref[p] += g
            dvq_ref[VQ_SGU_G:VQ_SGU_G + 1, :] += _colsum(dvn * vhat)
            dvq_ref[VQ_SGU_B:VQ_SGU_B + 1, :] += _colsum(dvn)
            dpv = (_ln_bwd(dvn * sgu_g, vhat, rsv[:, RS_SGU:RS_SGU + 1]) * gv_ref[rows, :]).astype(BF16)
            dgf_ref[...] += _cmul(_cmul(own, shift_ref[...]), hf_ref[b * 2 * DFT_F:(b + 1) * 2 * DFT_F, :], conj_b=True)
            product = _split(_cmul(with_later, gf_ref[...], conj_b=True))
            yield
            dh = _dot(inv_ref[...], product)
            yield
            da = (dh * sg).astype(BF16)
            dg = (dh * pa * (sg * (1.0 - sg))).astype(BF16)
            yield
            gx = alpha * dr1
            for dpj, wj in zip((dpu, dpv, da, dg), range(4)):
                gx = gx + _dot_nt(dpj, wi_v[wj])
            gx_ref[rows, :] = gx
            dr1b_parts[b], dproj_parts[b] = dr1b, (dpu, dpv, da, dg)

        _interleave([sub_tile(b) for b in reversed(range(nb))])

        awo[...] += _dot_tn(y_ref[...], jnp.concatenate(dr1b_parts, axis=0))
        xb = xb_ref[...]
        for j in range(4):
            awi[j] += _dot_tn(xb, jnp.concatenate([part[j] for part in dproj_parts], axis=0))

        @pl.when(i == n - 1)
        def _():
            stores = _start_copies(copy_sems, [(awi, dwi_hbm), (awo, dwo_hbm)])
            lane = lax.broadcasted_iota(jnp.int32, (CHUNK, LANES), 1)
            low = lane < HEAD_DIM
            dbs = jnp.zeros((CHUNK, LANES), F32)
            for p in range(n_pairs):
                grp = dbm_ref[:, p * LANES:(p + 1) * LANES]
                dbs = jnp.where(lane == 2 * p, jnp.sum(jnp.where(low, grp, 0.0), axis=1, keepdims=True), dbs)
                dbs = jnp.where(lane == 2 * p + 1, jnp.sum(jnp.where(low, 0.0, grp), axis=1, keepdims=True), dbs)
            dbs_ref[...] = dbs
            dcw_ref[...] = _dot(inv_taps_ref[...], _split(dgf_ref[...], True))
            for cp in stores:
                cp.wait()

    rev = lambda w: pl.BlockSpec((tm, w), lambda i: (n - 1 - i, 0))
    small = [jax.ShapeDtypeStruct((n_pairs, 2 * CHUNK, CHUNK), F32), jax.ShapeDtypeStruct((CHUNK, LANES), F32),
             jax.ShapeDtypeStruct(cwf.shape, F32), jax.ShapeDtypeStruct(vq.shape, F32), jax.ShapeDtypeStruct(vd.shape, F32)]
    small_ins = [wstt, cwf, tabs["fwd"], tabs["fwd_halo"], tabs["shift"], tabs["taps"], tabs["inv_in"], tabs["inv_taps"], vq, vd]
    return pl.pallas_call(
        body, name="bwd_mix", grid=(n,),
        in_specs=[rev(d), rev(d), rev(2 * q), rev(d), rev(d), rev(RS_COLS), rev(q), rev(q), rev(q), rev(q), rev(q), rev(q),
                  pl.BlockSpec((nb * 2 * DFT_F, q), lambda i: (n - 1 - i, 0)), ANY, ANY] + [_full(a.shape) for a in small_ins] + [ANY],
        out_specs=[rev(d), ANY, ANY] + [_full(s.shape) for s in small],
        out_shape=[jax.ShapeDtypeStruct((t, d), F32), jax.ShapeDtypeStruct(wi.shape, F32), jax.ShapeDtypeStruct(wo.shape, F32)] + small,
        scratch_shapes=[pltpu.VMEM(wi.shape, BF16), pltpu.VMEM(wo.shape, BF16), pltpu.VMEM(wi.shape, F32), pltpu.VMEM(wo.shape, F32),
                        pltpu.VMEM((nb, DFT_N, q), F32), pltpu.VMEM((HALO, q), F32),
                        pltpu.VMEM((CHUNK, q), F32), pltpu.VMEM((2 * DFT_F, q), F32), pltpu.VMEM((2 * DFT_F, q), F32),
                        pltpu.SemaphoreType.DMA((2,))],
        compiler_params=_cparams(),
    )(dx1, xb, pag, y, xh1, rs, zu_s, mg_s, vhat_s, gv_s, vnb_s, yhat_s, hf_s, wi, wo, *small_ins, token)


def _prep(me_arr, w_in, w_out, w_gate_t, w_up_t, w_down, conv_w, kwp):
    kw, cshard = conv_w.shape

    def body(me_ref, wi_ref, wo_ref, wg_ref, wu_ref, wd_ref, cw_ref, oi, oo, og, ou, od, oc):
        for src, dst in ((wi_ref, oi), (wo_ref, oo), (wg_ref, og), (wu_ref, ou), (wd_ref, od)):
            dst[...] = src[...].astype(BF16)
        oc[0:kw, :] = cw_ref[...]
        oc[kw:kwp, :] = jnp.zeros((kwp - kw, cshard), F32)

    ins = (w_in, w_out, w_gate_t, w_up_t, w_down, conv_w)
    outs = [jax.ShapeDtypeStruct((4,) + a.shape, BF16) for a in ins[:5]] + [jax.ShapeDtypeStruct((4, kwp, cshard), F32)]
    grid_spec = pltpu.PrefetchScalarGridSpec(
        num_scalar_prefetch=1, grid=(1,),
        in_specs=[pl.BlockSpec(a.shape, lambda i, me: (0, 0)) for a in ins],
        out_specs=[pl.BlockSpec((None,) + o.shape[1:], lambda i, me: (me[0], 0, 0)) for o in outs])
    return pl.pallas_call(body, name="wprep", grid_spec=grid_spec, out_shape=outs, compiler_params=_cparams())(me_arr, *ins)


def _coords():
    return tuple(lax.axis_index(a) for a in MESH_AXES)


def _other_chips(x, y):
    return [(1 - x, y), (x, 1 - y), (1 - x, 1 - y)]


def _remote(src, dst, send_sem, recv_sem, to):
    return pltpu.make_async_remote_copy(src_ref=src, dst_ref=dst, send_sem=send_sem, recv_sem=recv_sem,
                                        device_id=to, device_id_type=MESH_ID)


def _hbm_call(body, name, ins, out_shape, scratch_shapes, aliases=None):
    return pl.pallas_call(
        body, name=name, in_specs=[ANY] * len(ins), out_specs=[ANY] * len(out_shape), out_shape=out_shape,
        scratch_shapes=scratch_shapes, input_output_aliases=aliases or {},
    )(*ins)


class _Gather:
    def __init__(self, bufs, send_sems, recv_sems):
        self.bufs, self.send_sems, self.recv_sems = bufs, send_sems, recv_sems
        self.x, self.y, self.c = _coords()

    def _copies(self, stage):
        x, y, c = self.x, self.y, self.c
        for a, buf in enumerate(self.bufs):
            hr = buf.shape[1] // 2
            for j, chip in enumerate(_other_chips(x, y)):
                if stage == "ici_out":
                    ref, k, to = buf.at[2 * x + y, pl.ds(c * hr, hr)], j, (*chip, c)
                elif stage == "ici_in":
                    ref, k, to = buf.at[2 * chip[0] + chip[1], pl.ds(c * hr, hr)], j, (*chip, c)
                elif stage == "d2d_out":
                    ref, k, to = buf.at[2 * chip[0] + chip[1], pl.ds(c * hr, hr)], 3 + j, (x, y, 1 - c)
                else:
                    ref, k, to = buf.at[2 * chip[0] + chip[1], pl.ds((1 - c) * hr, hr)], 3 + j, (x, y, 1 - c)
                yield _remote(ref, ref, self.send_sems.at[a, k], self.recv_sems.at[a, k], to)

    def start(self):
        for cp in self._copies("ici_out"):
            cp.start()

    def forward(self):
        for landed, onward in zip(self._copies("ici_in"), self._copies("d2d_out")):
            landed.wait_recv()
            onward.start()

    def finish(self):
        for cp in self._copies("d2d_in"):
            cp.wait_recv()
        for stage in ("ici_out", "d2d_out"):
            for cp in self._copies(stage):
                cp.wait_send()


def _gather_sems(n):
    return [pltpu.SemaphoreType.DMA((n, 6)), pltpu.SemaphoreType.DMA((n, 6))]


def _gather_shards(bufs):
    n = len(bufs)

    def body(*refs):
        g = _Gather(refs[n:2 * n], *refs[2 * n:])
        g.start()
        g.forward()
        g.finish()

    return _hbm_call(body, "gather_shards", bufs, [jax.ShapeDtypeStruct(s.shape, s.dtype) for s in bufs],
                     _gather_sems(n), aliases={a: a for a in range(n)})


def _pair_swap(name, arrs):
    n = len(arrs)

    def body(*refs):
        src, land = refs[:n], refs[n:2 * n]
        send_sems, recv_sems = refs[2 * n:]
        x, y, c = _coords()
        copies = []
        for a in range(n):
            s = src[a].at[pl.ds(0, arrs[a].shape[0]), 1 - c] if arrs[a].ndim == 4 else src[a].at[1 - c]
            copies.append(_remote(s, land[a], send_sems.at[a], recv_sems.at[a], (x, y, 1 - c)))
            copies[-1].start()
        for cp in copies:
            cp.wait()

    outs = [jax.ShapeDtypeStruct(s.shape[:-3] + s.shape[-2:], s.dtype) for s in arrs]
    return _hbm_call(body, name, arrs, outs, [pltpu.SemaphoreType.DMA((n,)), pltpu.SemaphoreType.DMA((n,))])


class _Exchange:
    def __init__(self, src, dst, send_sems, recv_sems):
        self.src, self.dst, self.send_sems, self.recv_sems = src, dst, send_sems, recv_sems
        self.x, self.y, self.c = _coords()

    def _copies(self, incoming):
        x, y, c = self.x, self.y, self.c
        for a, (s, d) in enumerate(zip(self.src, self.dst)):
            for j, chip in enumerate(_other_chips(x, y)):
                slot = 2 * chip[0] + chip[1]
                if incoming:
                    out, into = d.at[slot], d.at[slot]
                else:
                    out, into = (s.at[slot] if len(s.shape) == 3 else s), d.at[2 * x + y]
                yield _remote(out, into, self.send_sems.at[a, j], self.recv_sems.at[a, j], (*chip, c))

    def start(self):
        for cp in self._copies(False):
            cp.start()

    def finish(self):
        for cp in self._copies(True):
            cp.wait_recv()
        for cp in self._copies(False):
            cp.wait_send()


def _exchange_shapes(arrs):
    return [jax.ShapeDtypeStruct((4,) + s.shape[-2:], s.dtype) for s in arrs]


class _FlatSems:
    def __init__(self, ref):
        self.ref = ref

    @property
    def at(self):
        return self

    def __getitem__(self, idx):
        return self.ref.at[3 * idx[0] + idx[1]]


HBM = pl.BlockSpec(memory_space=pltpu.HBM)
SEM = pl.BlockSpec(memory_space=pltpu.SEMAPHORE)
DATAFLOW = pltpu.SideEffectType.DATAFLOW_SIDE_EFFECTING


def _exchange_start(name, arrs):
    n = len(arrs)
    lands = _exchange_shapes(arrs)

    def body(*refs):
        src, land = refs[:n], refs[n:2 * n]
        send_sems, recv_sems = refs[2 * n:2 * n + 2]
        token = refs[-1]
        _Exchange(src, land, _FlatSems(send_sems), _FlatSems(recv_sems)).start()
        token[...] = jnp.zeros_like(token)

    hbm = lambda a: pltpu.with_memory_space_constraint(a, pltpu.HBM)
    outs = pl.pallas_call(
        body, name=name,
        out_shape=(pltpu.SemaphoreType.DMA((3 * n,)), pltpu.SemaphoreType.DMA((3 * n,)),
                   *[pltpu.HBM(a.shape, a.dtype) for a in arrs], *[pltpu.HBM(s.shape, s.dtype) for s in lands],
                   jax.ShapeDtypeStruct((SUBLANES, LANES), F32)),
        in_specs=[HBM] * (2 * n), out_specs=(SEM, SEM, *[HBM] * (2 * n), pl.BlockSpec(memory_space=pltpu.VMEM)),
        input_output_aliases={a: 2 + a for a in range(2 * n)},
        compiler_params=pltpu.CompilerParams(has_side_effects=DATAFLOW),
    )(*[hbm(a) for a in arrs], *[hbm(lax.empty(s.shape, s.dtype)) for s in lands])
    return outs[:-1], outs[-1]


def _exchange_wait(name, started, after):
    send_sems, recv_sems, *bufs = started
    n = len(bufs) // 2

    def body(*refs):
        src, land = refs[:n], refs[n:2 * n]
        send_sems, recv_sems = refs[2 * n:2 * n + 2]
        _Exchange(src, land, _FlatSems(send_sems), _FlatSems(recv_sems)).finish()

    outs = pl.pallas_call(
        body, name=name,
        out_shape=tuple(pltpu.HBM(b.shape, b.dtype) for b in bufs),
        in_specs=[HBM] * (2 * n) + [SEM, SEM] + [ANY] * len(after), out_specs=tuple([HBM] * (2 * n)),
        input_output_aliases={a: a for a in range(2 * n)},
        compiler_params=pltpu.CompilerParams(has_side_effects=DATAFLOW),
    )(*bufs, send_sems, recv_sems, *after)
    return list(outs[:n]), list(outs[n:])


def _pair_gather(name, halves):
    n = len(halves)

    def body(*refs):
        src, dst = refs[:n], refs[n:2 * n]
        send_sems, recv_sems = refs[2 * n:]
        x, y, c = _coords()
        copies = [_remote(src[a], dst[a], send_sems.at[a], recv_sems.at[a], (x, y, 1 - c)) for a in range(n)]
        for cp in copies:
            cp.start()
        for cp in copies:
            cp.wait()

    outs = [jax.ShapeDtypeStruct(s.shape, s.dtype) for s in halves]
    return _hbm_call(body, name, halves, outs, [pltpu.SemaphoreType.DMA((n,)), pltpu.SemaphoreType.DMA((n,))])


def _pair_sum(a, g, land, c_arr, out_dtype):
    nq, _, hr, cc = g.shape

    def body(c_ref, g_ref, l_ref, o_ref):
        o_ref[...] = (g_ref[...] + l_ref[...].astype(F32)).astype(out_dtype)

    spec = pl.BlockSpec((None, hr, cc), lambda qi, cr: (qi, 0, 0))
    grid_spec = pltpu.PrefetchScalarGridSpec(
        num_scalar_prefetch=1, grid=(nq,),
        in_specs=[pl.BlockSpec((None, None, hr, cc), lambda qi, cr: (qi, cr[0], 0, 0)), spec], out_specs=spec)
    return pl.pallas_call(body, name=f"pair_sum_{a}", grid_spec=grid_spec, out_shape=jax.ShapeDtypeStruct((nq, hr, cc), out_dtype),
                          compiler_params=_cparams())(c_arr, g, land)


def _chip_sum(a, parts, own, me_arr, after):
    _, hr, cc = parts.shape

    def body(me_ref, p_ref, own_ref, after_ref, o_ref):
        for mine in range(4):
            @pl.when(me_ref[0] == mine)
            def _():
                term = lambda j: (own_ref if j == mine else p_ref.at[j])[...].astype(F32)
                o_ref[...] = ((term(0) + term(1)) + term(2)) + term(3)

    own_spec = (pl.BlockSpec((None, hr, cc), lambda i, me: (me[0], 0, 0)) if own.ndim == 3
                else pl.BlockSpec((hr, cc), lambda i, me: (0, 0)))
    grid_spec = pltpu.PrefetchScalarGridSpec(
        num_scalar_prefetch=1, grid=(1,),
        in_specs=[pl.BlockSpec((4, hr, cc), lambda i, me: (0, 0, 0)), own_spec, ANY],
        out_specs=pl.BlockSpec((hr, cc), lambda i, me: (0, 0)))
    return pl.pallas_call(body, name=f"chip_sum_{a}", grid_spec=grid_spec, out_shape=jax.ShapeDtypeStruct((hr, cc), F32),
                          compiler_params=_cparams())(me_arr, parts, own, after)


def _row_block(rows, cols, limit=1 << 20):
    best = 8
    for tr in range(8, rows + 1, 8):
        if rows % tr == 0 and tr * cols * 4 <= limit:
            best = tr
    return best


def _adamw(name, w, g_mine, g_other, m, v, c_arr):
    r, c = w.shape
    hr, cg = g_mine.shape
    tr = hr if r % hr == 0 and hr * cg * 4 <= (3 << 19) else math.gcd(_row_block(hr, cg), r)
    per_half = hr // tr
    bc1 = 1.0 - ADAM_B1 ** ADAM_STEP
    bc2 = 1.0 - ADAM_B2 ** ADAM_STEP

    def body(c_ref, w_ref, gm_ref, go_ref, m_ref, v_ref, go, do, mo, vo):
        gv = jnp.where(pl.program_id(0) // per_half == c_ref[0], gm_ref[:, 0:c], go_ref[:, 0:c])
        mn = ADAM_B1 * m_ref[...] + (1.0 - ADAM_B1) * gv
        vn = ADAM_B2 * v_ref[...] + (1.0 - ADAM_B2) * (gv * gv)
        go[...] = gv
        mo[...] = mn
        vo[...] = vn
        do[...] = -ADAM_LR * ((mn / bc1) / (jnp.sqrt(vn / bc2) + ADAM_EPS) + ADAM_WD * w_ref[...])

    blk = pl.BlockSpec((tr, c), lambda i, cr: (i, 0))
    gblk = pl.BlockSpec((tr, cg), lambda i, cr: (i % per_half, 0))
    grid_spec = pltpu.PrefetchScalarGridSpec(num_scalar_prefetch=1, grid=(r // tr,), in_specs=[blk, gblk, gblk, blk, blk],
                                             out_specs=[blk] * 4)
    return pl.pallas_call(body, name=f"adamw_{name}", grid_spec=grid_spec, out_shape=[jax.ShapeDtypeStruct((r, c), F32)] * 4,
                          compiler_params=_cparams())(c_arr, w, g_mine, g_other, m, v)


def _rows128(a):
    return a.reshape(-1, LANES)


def _pad_rows(a, rows):
    return jnp.pad(a, ((0, rows - a.shape[0]), (0, 0)))


def kernel(x, w_in, sgu_ln_g, sgu_ln_b, w_s, b_s, conv_w, conv_b, conv_ln_g, conv_ln_b, w_out, ln1_g, ln1_b, w_gate, w_up, w_down, ln2_g, ln2_b, loss_target, m_w_in, m_sgu_ln_g, m_sgu_ln_b, m_w_s, m_b_s, m_conv_w, m_conv_b, m_conv_ln_g, m_conv_ln_b, m_w_out, m_ln1_g, m_ln1_b, m_w_gate, m_w_up, m_w_down, m_ln2_g, m_ln2_b, v_w_in, v_sgu_ln_g, v_sgu_ln_b, v_w_s, v_b_s, v_conv_w, v_conv_b, v_conv_ln_g, v_conv_ln_b, v_w_out, v_ln1_g, v_ln1_b, v_w_gate, v_w_up, v_w_down, v_ln2_g, v_ln2_b):
    depth, d, q = w_in.shape
    assert depth == 1 and x.shape[0] == 1
    t = x.shape[1]
    heads = w_s.shape[1]
    kw, cshard = conv_w.shape[1], conv_w.shape[2]
    fs = w_gate.shape[2]
    slabs = _hidden_slabs(4 * fs)
    n_pairs = q // LANES
    assert heads * HEAD_DIM == q and q % LANES == 0 and w_s.shape[2] == CHUNK and 4 * cshard == q and kw - 1 <= HALO
    alpha = (2.0 * depth) ** 0.25
    tm = min(512, t)
    assert t % tm == 0 and tm % CHUNK == 0
    x2, tgt = x[0], loss_target[0]
    mx, my, mc = _coords()
    me = 2 * mx + my
    c_arr = jnp.reshape(mc, (1,)).astype(jnp.int32)

    kwp = -(-kw // 16) * 16
    me_arr = jnp.reshape(me, (1,)).astype(jnp.int32)
    wi, wo, wg, wu, wd, cw4 = _prep(me_arr, w_in[0], w_out[0], w_gate[0].T, w_up[0].T, w_down[0], conv_w[0], kwp)
    wi, wo, cw4 = _gather_shards([wi, wo, cw4])
    wo = wo.reshape(d, d)
    cw = jnp.transpose(cw4, (1, 0, 2)).reshape(kwp, q)
    cwf = _pad_rows(cw[:kw][::-1], kwp)
    tabs = {name: jnp.asarray(tab) for name, tab in _dft_tables(kw, kwp, q).items()}

    wm = jnp.where(jnp.tril(jnp.ones((CHUNK, CHUNK), bool)), w_s[0], 0.0)
    wst = wm.reshape(n_pairs, 2 * CHUNK, CHUNK).astype(BF16)
    wstt = jnp.transpose(wm, (0, 2, 1)).reshape(n_pairs, 2 * CHUNK, CHUNK).astype(BF16)
    bmat = jnp.repeat(b_s[0].T, HEAD_DIM, axis=1)
    vq = _pad_rows(jnp.concatenate([sgu_ln_g, sgu_ln_b, conv_b, conv_ln_g, conv_ln_b], axis=0), 8)
    vd = _pad_rows(jnp.concatenate([ln1_g, ln1_b, ln2_g, ln2_b], axis=0), 8)

    *saved, wg, wu, wd = _fwd_mix(x2, wi, wo, wst, bmat, cwf, tabs, vq, vd, [wg, wu, wd], alpha, tm)
    wg, wu, wd = (w.reshape(4 * fs, d) for w in (wg, wu, wd))
    *acts, x1b, dr2, loss_part, dg2, db2 = _fwd_mlp(saved[3], tgt, wg, wu, wd, vd, alpha, slabs, tm)
    mlp_grads = None
    for j, slab in enumerate(slabs):
        mlp_grads = _bwd_mlp_slab(j, slab, dr2, mlp_grads, x1b, acts[j], acts[len(slabs) + j], wg, wu, wd, alpha, tm)
    dx1 = mlp_grads[0]
    mlp_halves = [b.reshape(4, 2, fs // 2, d) for b in mlp_grads[1:]]
    landed = _pair_swap("pair_swap_mlp", mlp_halves[3:])
    mlp_sums = [_pair_sum(f"mlp{a}", h, l, c_arr, BF16) for a, (h, l) in enumerate(zip(mlp_halves[:3], landed))]
    mlp_started, token = _exchange_start("exchange_mlp_start", mlp_sums)
    grad_x, dwi, dwo, dws, dbs, dcw, dvq, dvd = _bwd_mix(dx1, saved, wi, wo, wstt, cwf, tabs, vq, vd, token, alpha, tm)
    mlp_sums, mlp_parts = _exchange_wait("exchange_mlp_wait", mlp_started, [dwo])

    dws = jnp.where(jnp.tril(jnp.ones((CHUNK, CHUNK), bool)), dws.reshape(heads, CHUNK, CHUNK), 0.0)
    dvd = dvd.at[VD_LN2_G].set(dg2[0]).at[VD_LN2_B].set(db2[0])
    pieces = [_rows128(dws), dbs[:, :heads].T, _rows128(dcw), _rows128(dvq), _rows128(dvd), loss_part]
    sizes = [p.shape[0] for p in pieces]
    rows = -(-sum(sizes) // 16) * 16
    small = _pad_rows(jnp.concatenate(pieces, axis=0), rows)

    big = [dwi, dwo.reshape(4, d // 4, d)]
    halves = [b.reshape(4, 2, b.shape[1] // 2, b.shape[2]) for b in big] + [small.reshape(2, rows // 2, LANES)]
    landed = _pair_swap("pair_swap_mix", halves)
    sums = [_pair_sum(f"mix{a}", h, l, c_arr, BF16) for a, (h, l) in enumerate(zip(halves[:-1], landed[:-1]))]
    sums.append(_pair_sum("small", halves[-1][None], landed[-1][None], c_arr, F32)[0])
    mix_started, token = _exchange_start("exchange_mix_start", sums)

    out, raw = {}, {}

    def finish(first, names, parts, sums, after):
        mine = [_chip_sum(first + a, p, s, me_arr, after) for a, (p, s) in enumerate(zip(parts, sums))]
        other = _pair_gather(f"pair_gather_{first}", mine)
        for a, nm in enumerate(names):
            w_, m_, v_ = weights[nm]
            if nm in ("w_gate", "w_up"):
                raw[nm] = _adamw(nm, w_[0].T, mine[a], other[a], m_[0].T, v_[0].T, c_arr)
                out[nm] = [o.T for o in raw[nm]]
            else:
                raw[nm] = out[nm] = _adamw(nm, w_[0], mine[a], other[a], m_[0], v_[0], c_arr)
        return mine[-1], other[-1]

    weights = {"w_in": (w_in, m_w_in, v_w_in), "w_out": (w_out, m_w_out, v_w_out), "w_gate": (w_gate, m_w_gate, v_w_gate),
               "w_up": (w_up, m_w_up, v_w_up), "w_down": (w_down, m_w_down, v_w_down)}
    finish(2, ["w_gate", "w_up", "w_down"], mlp_parts, mlp_sums, token)
    sums, parts = _exchange_wait("exchange_mix_wait", mix_started, [raw[nm][1] for nm in ("w_gate", "w_up", "w_down")])
    small_mine, small_other = finish(5, ["w_in", "w_out"], parts, sums, parts[0])

    def pack(ws, bs, vqs, vds):
        ps = [_rows128(ws[0]), bs[0], jnp.zeros((sizes[2], LANES), F32),
              _rows128(_pad_rows(jnp.concatenate(vqs, axis=0), 8)), _rows128(_pad_rows(jnp.concatenate(vds, axis=0), 8)),
              jnp.zeros((sizes[5], LANES), F32)]
        return _pad_rows(jnp.concatenate(ps, axis=0), rows)

    packed = _adamw(
        "small",
        pack(w_s, b_s, [sgu_ln_g, sgu_ln_b, conv_b, conv_ln_g, conv_ln_b], [ln1_g, ln1_b, ln2_g, ln2_b]), small_mine, small_other,
        pack(m_w_s, m_b_s, [m_sgu_ln_g, m_sgu_ln_b, m_conv_b, m_conv_ln_g, m_conv_ln_b], [m_ln1_g, m_ln1_b, m_ln2_g, m_ln2_b]),
        pack(v_w_s, v_b_s, [v_sgu_ln_g, v_sgu_ln_b, v_conv_b, v_conv_ln_g, v_conv_ln_b], [v_ln1_g, v_ln1_b, v_ln2_g, v_ln2_b]),
        c_arr)

    offs = [sum(sizes[:i]) for i in range(len(sizes))]
    g_cw_full = packed[0][offs[2]:offs[2] + sizes[2]].reshape(kwp, q)
    g_cw = lax.dynamic_slice(g_cw_full, (0, me * cshard), (kwp, cshard))
    out["conv_w"] = _adamw("conv_w", _pad_rows(conv_w[0], kwp), g_cw, g_cw, _pad_rows(m_conv_w[0], kwp), _pad_rows(v_conv_w[0], kwp), c_arr)
    out["conv_w"] = [o[:kw] for o in out["conv_w"]]

    def unpack(p):
        vq_o = p[offs[3]:offs[3] + sizes[3]].reshape(8, q)
        vd_o = p[offs[4]:offs[4] + sizes[4]].reshape(8, d)
        return {"w_s": p[offs[0]:offs[0] + sizes[0]].reshape(heads, CHUNK, CHUNK), "b_s": p[offs[1]:offs[1] + sizes[1]],
                "sgu_ln_g": vq_o[VQ_SGU_G], "sgu_ln_b": vq_o[VQ_SGU_B], "conv_b": vq_o[VQ_CONV_B],
                "conv_ln_g": vq_o[VQ_CLN_G], "conv_ln_b": vq_o[VQ_CLN_B],
                "ln1_g": vd_o[VD_LN1_G], "ln1_b": vd_o[VD_LN1_B], "ln2_g": vd_o[VD_LN2_G], "ln2_b": vd_o[VD_LN2_B]}

    small_out = [unpack(p) for p in packed]
    loss = packed[0][offs[5], 0]
    names = ["w_in", "sgu_ln_g", "sgu_ln_b", "w_s", "b_s", "conv_w", "conv_b", "conv_ln_g", "conv_ln_b", "w_out",
             "ln1_g", "ln1_b", "w_gate", "w_up", "w_down", "ln2_g", "ln2_b"]
    result = [loss, grad_x[None]]
    for kind in range(4):
        for nm in names:
            val = out[nm][kind] if nm in out else small_out[kind][nm]
            result.append(val[None])
    return tuple(result)
```

```python
import functools
import math

import jax
import numpy as np
import jax.numpy as jnp
from jax import lax
from jax.experimental import pallas as pl
from jax.experimental.pallas import tpu as pltpu

F32 = jnp.float32
BF16 = jnp.bfloat16

LN_EPS = 1e-5
HEAD_DIM = 64
CHUNK = 128
HALO = 32
LANES = 128
MXU_N = 256
ADAM_LR, ADAM_B1, ADAM_B2, ADAM_EPS, ADAM_WD, ADAM_STEP = 0.001, 0.9, 0.999, 1e-08, 0.01, 10
VMEM_LIMIT = 63 * 1024 * 1024
MESH_AXES = ("x", "y", "c")
MESH_ID = pl.DeviceIdType.MESH


def _dot(a, b):
    return jnp.dot(a, b, preferred_element_type=F32)


def _dot_nt(a, b):
    return lax.dot_general(a, b, (((1,), (1,)), ((), ())), preferred_element_type=F32)


def _dot_tn(a, b):
    return lax.dot_general(a, b, (((0,), (0,)), ((), ())), preferred_element_type=F32)


def _sigmoid(v):
    return 1.0 / (1.0 + jnp.exp(-v))


def _gelu(v):
    cdf = 0.5 * (1.0 + lax.erf(v * (1.0 / math.sqrt(2.0))))
    pdf = jnp.exp(-0.5 * v * v) * (1.0 / math.sqrt(2.0 * math.pi))
    return v * cdf, cdf + v * pdf


def _ln_stats(v):
    mu = jnp.mean(v, axis=-1, keepdims=True)
    d = v - mu
    rstd = lax.rsqrt(jnp.mean(d * d, axis=-1, keepdims=True) + LN_EPS)
    return d * rstd, rstd


def _ln_bwd(dxhat, xhat, rstd):
    m1 = jnp.mean(dxhat, axis=-1, keepdims=True)
    m2 = jnp.mean(dxhat * xhat, axis=-1, keepdims=True)
    return rstd * (dxhat - m1 - xhat * m2)


def _colsum(v):
    return jnp.sum(v, axis=0, keepdims=True)


def _pair_lanes(v, nc, p):
    return jnp.concatenate([v[c * CHUNK:(c + 1) * CHUNK, p * LANES:(p + 1) * LANES] for c in range(nc)], axis=1)


def _unpair(parts, nc):
    rows = [jnp.concatenate([part[:, c * LANES:(c + 1) * LANES] for part in parts], axis=1) for c in range(nc)]
    return jnp.concatenate(rows, axis=0)


def _low_head(nc):
    lane = lax.broadcasted_iota(jnp.int32, (CHUNK, nc * LANES), 1)
    return (lane & (LANES - 1)) < HEAD_DIM


def _mix(wst_ref, v, nc, n_pairs):
    vb = v.astype(BF16)
    low = _low_head(nc)
    parts = []
    for p in range(n_pairs):
        r = _dot(wst_ref[p], _pair_lanes(vb, nc, p))
        parts.append(jnp.where(low, r[:CHUNK], r[CHUNK:]))
    return _unpair(parts, nc)


def _mix_wgrad(dm, vn, nc, n_pairs):
    low = _low_head(nc)
    vb = vn.astype(BF16)
    out = []
    for p in range(n_pairs):
        a = _pair_lanes(dm, nc, p)
        lhs = jnp.concatenate([jnp.where(low, a, 0.0), jnp.where(low, 0.0, a)], axis=0).astype(BF16)
        out.append(_dot_nt(lhs, _pair_lanes(vb, nc, p)))
    return out


SUBLANES = 8


CONV_BLOCK = 256
DFT_N = CONV_BLOCK + HALO
DFT_F = -(-(DFT_N // 2 + 1) // SUBLANES) * SUBLANES


def _terms(m, exact):
    hi = m.astype(np.float32).astype(BF16)
    lo = (m.astype(np.float32) - hi.astype(np.float32)).astype(BF16)
    return np.concatenate([hi, hi, lo] if exact else [hi, hi], axis=1)


def _split(v, exact=False):
    hi = v.astype(BF16)
    lo = (v - hi.astype(F32)).astype(BF16)
    return jnp.concatenate([hi, lo, hi] if exact else [hi, lo], axis=0)


def _dft_tables(kw, kwp, q):
    nf = DFT_N // 2 + 1
    ang = 2.0 * np.pi * np.arange(nf)[:, None] * np.arange(DFT_N)[None, :] / DFT_N
    fwd = np.zeros((2 * DFT_F, DFT_N))
    fwd[:nf], fwd[DFT_F:DFT_F + nf] = np.cos(ang), -np.sin(ang)
    weight = np.full((nf, 1), 2.0 / DFT_N)
    weight[0] = weight[-1] = 1.0 / DFT_N
    inv = np.zeros((DFT_N, 2 * DFT_F))
    inv[:, :nf], inv[:, DFT_F:DFT_F + nf] = (np.cos(ang) * weight).T, (-np.sin(ang) * weight).T
    inv_taps = np.zeros((kwp, 2 * DFT_F))
    inv_taps[:kw] = inv[kw - 1::-1][:kw]
    shift = np.zeros((2 * DFT_F, q), np.float32)
    shift[:nf], shift[DFT_F:DFT_F + nf] = np.cos(ang[:, HALO:HALO + 1]), -np.sin(ang[:, HALO:HALO + 1])
    return {"fwd": _terms(fwd, False), "fwd_halo": _terms(fwd[:, CONV_BLOCK:], False), "shift": shift,
            "inv_out": _terms(inv[HALO:HALO + CONV_BLOCK], False), "inv_in": _terms(inv[:CONV_BLOCK], False),
            "taps": _terms(fwd[:, :kwp], True), "inv_taps": _terms(inv_taps, True)}


def _cmul(a, b, conj_b=False):
    ar, ai, br, bi = a[:DFT_F], a[DFT_F:], b[:DFT_F], b[DFT_F:]
    if conj_b:
        return jnp.concatenate([ar * br + ai * bi, ai * br - ar * bi], axis=0)
    return jnp.concatenate([ar * br - ai * bi, ar * bi + ai * br], axis=0)


def _interleave(sub_tiles):
    waiting, live = list(sub_tiles), []
    while waiting or live:
        if waiting:
            live.append(waiting.pop(0))
        for g in reversed(list(live)):
            try:
                next(g)
            except StopIteration:
                live.remove(g)


def _start_copies(sems, pairs, first=0):
    copies = [pltpu.make_async_copy(src, dst, sems.at[first + k]) for k, (src, dst) in enumerate(pairs)]
    for cp in copies:
        cp.start()
    return copies


def _cparams():
    return pltpu.CompilerParams(dimension_semantics=("arbitrary",), vmem_limit_bytes=VMEM_LIMIT)


def _full(shape):
    return pl.BlockSpec(shape, lambda i: (0,) * len(shape))


ANY = pl.BlockSpec(memory_space=pl.ANY)

VQ_SGU_G, VQ_SGU_B, VQ_CONV_B, VQ_CLN_G, VQ_CLN_B = range(5)
VD_LN1_G, VD_LN1_B, VD_LN2_G, VD_LN2_B = range(4)
RS_LN1, RS_SGU, RS_CONV = range(3)
RS_COLS = 8


def _fwd_mix(x, wi, wo, wst, bmat, cwf, tabs, vq, vd, mlp_w, alpha, tm):
    t, d = x.shape
    q = wi.shape[2]
    nc, n_pairs = CONV_BLOCK // CHUNK, q // LANES
    n = t // tm
    n_in, n_saved = 11, 12
    assert tm % CONV_BLOCK == 0

    def body(x_ref, wi_hbm, wo_hbm, wst_ref, bmat_ref, cwf_ref, fwd_ref, taps_ref, inv_ref, vq_ref, vd_ref, *rest):
        (xb_ref, pag_ref, y_ref, xh_ref, rs_ref, zu_ref, mg_ref, vhat_ref, gv_ref, vnb_ref, yhat_ref, hf_ref) = rest[3:3 + n_saved]
        gathered = rest[3 + n_saved:6 + n_saved]
        wi_v, wo_v, hb_ref, gf_ref, send_sems, recv_sems, copy_sems = rest[6 + n_saved:]
        step = pl.program_id(0)

        @pl.when(step == 0)
        def _():
            loads = _start_copies(copy_sems, [(wi_hbm, wi_v), (wo_hbm, wo_v)])
            _Gather(gathered, send_sems, recv_sems).start()
            hb_ref[...] = jnp.zeros_like(hb_ref)
            gf_ref[...] = _dot(taps_ref[...], _split(cwf_ref[...], True))
            for cp in loads:
                cp.wait()

        @pl.when(step == (3 * n) // 4)
        def _():
            _Gather(gathered, send_sems, recv_sems).forward()

        def sub_tile(b):
            rows = slice(b * CONV_BLOCK, (b + 1) * CONV_BLOCK)
            xv = x_ref[rows, :]
            xb = xv.astype(BF16)
            xb_ref[rows, :] = xb
            pu, pv, pa, pg = (_dot(xb, wi_v[j]) for j in range(4))
            yield
            pag_ref[rows, 0:q] = pa.astype(BF16)
            pag_ref[rows, q:2 * q] = pg.astype(BF16)
            zu, gu = _gelu(pu)
            zv, gv = _gelu(pv)
            vhat, rstd_v = _ln_stats(zv)
            vnb = (vhat * vq_ref[VQ_SGU_G:VQ_SGU_G + 1, :] + vq_ref[VQ_SGU_B:VQ_SGU_B + 1, :]).astype(BF16)
            hb_ref[HALO + b * CONV_BLOCK:HALO + (b + 1) * CONV_BLOCK, :] = pa * _sigmoid(pg)
            yield
            mixed = _mix(wst_ref, vnb, nc, n_pairs) + jnp.concatenate([bmat_ref[...]] * nc, axis=0)
            spectrum = _dot(fwd_ref[...], _split(hb_ref[b * CONV_BLOCK:b * CONV_BLOCK + DFT_N, :]))
            yield
            y_ref[rows, 0:q] = (zu * mixed).astype(BF16)
            zu_ref[rows, :] = zu.astype(BF16)
            mg_ref[rows, :] = (mixed * gu).astype(BF16)
            vhat_ref[rows, :] = vhat
            gv_ref[rows, :] = gv.astype(BF16)
            vnb_ref[rows, :] = vnb
            hf_ref[b * 2 * DFT_F:(b + 1) * 2 * DFT_F, :] = spectrum
            product = _split(_cmul(gf_ref[...], spectrum))
            yield
            yc = _dot(inv_ref[...], product) + vq_ref[VQ_CONV_B:VQ_CONV_B + 1, :]
            yield
            yhat, rstd_c = _ln_stats(yc)
            yhat_ref[rows, :] = yhat
            yn = yhat * vq_ref[VQ_CLN_G:VQ_CLN_G + 1, :] + vq_ref[VQ_CLN_B:VQ_CLN_B + 1, :]
            y_ref[rows, q:2 * q] = (yn * _sigmoid(yn)).astype(BF16)
            yield
            r1 = alpha * xv + _dot(y_ref[rows, :], wo_v[...])
            yield
            xhat, rstd1 = _ln_stats(r1)
            xh_ref[rows, :] = xhat
            col = lax.broadcasted_iota(jnp.int32, (CONV_BLOCK, RS_COLS), 1)
            rs_ref[rows, :] = jnp.where(col == RS_LN1, rstd1, jnp.where(col == RS_SGU, rstd_v, jnp.where(col == RS_CONV, rstd_c, 0.0)))

        _interleave([sub_tile(b) for b in range(tm // CONV_BLOCK)])
        hb_ref[0:HALO, :] = hb_ref[tm:tm + HALO, :]

        @pl.when(step == n - 1)
        def _():
            _Gather(gathered, send_sems, recv_sems).finish()

    row = lambda w: pl.BlockSpec((tm, w), lambda i: (i, 0))
    widths = [(d, BF16), (2 * q, BF16), (d, BF16), (d, F32), (RS_COLS, F32), (q, BF16), (q, BF16), (q, F32), (q, BF16), (q, BF16), (q, F32)]
    assert len(widths) + 1 == n_saved
    small_ins = [wst, bmat, cwf, tabs["fwd"], tabs["taps"], tabs["inv_out"], vq, vd]
    return pl.pallas_call(
        body, name="fwd_mix", grid=(n,),
        in_specs=[row(d), ANY, ANY] + [_full(a.shape) for a in small_ins] + [ANY] * 3,
        out_specs=[row(w) for w, _ in widths] + [pl.BlockSpec((tm // CONV_BLOCK * 2 * DFT_F, q), lambda i: (i, 0))] + [ANY] * 3,
        out_shape=[jax.ShapeDtypeStruct((t, w), dt) for w, dt in widths] + [jax.ShapeDtypeStruct((t // CONV_BLOCK * 2 * DFT_F, q), F32)]
        + [jax.ShapeDtypeStruct(b.shape, b.dtype) for b in mlp_w],
        scratch_shapes=[pltpu.VMEM(wi.shape, BF16), pltpu.VMEM(wo.shape, BF16), pltpu.VMEM((HALO + tm, q), F32),
                        pltpu.VMEM((2 * DFT_F, q), F32)] + _gather_sems(3) + [pltpu.SemaphoreType.DMA((2,))],
        input_output_aliases={n_in + a: n_saved + a for a in range(3)},
        compiler_params=_cparams(),
    )(x, wi, wo, *small_ins, *mlp_w)


MLP_SLABS = 4


def _hidden_slabs(f):
    assert f % MXU_N == 0
    tiles = f // MXU_N
    sizes = [(tiles // MLP_SLABS + (1 if j < tiles % MLP_SLABS else 0)) * MXU_N for j in range(MLP_SLABS)]
    return [(sum(sizes[:j]), sz) for j, sz in enumerate(sizes) if sz]


def _fwd_mlp(xh1, tgt, wg, wu, wd, vd, alpha, slabs, tm):
    t, d = xh1.shape
    n = t // tm
    ns = len(slabs)
    half = tm // 2 if tm % 32 == 0 else tm

    def body(xh_ref, tgt_ref, wg_hbm, wu_hbm, wd_hbm, vd_ref, *rest):
        gp_refs, up_refs = rest[:ns], rest[ns:2 * ns]
        x1b_ref, dr2_ref, loss_ref, dg2_ref, db2_ref, wg_v, wu_v, wd_v, copy_sems = rest[2 * ns:]

        @pl.when(pl.program_id(0) == 0)
        def _():
            loads = _start_copies(copy_sems, [(wg_hbm, wg_v), (wu_hbm, wu_v), (wd_hbm, wd_v)])
            loss_ref[...] = jnp.zeros_like(loss_ref)
            dg2_ref[...] = jnp.zeros_like(dg2_ref)
            db2_ref[...] = jnp.zeros_like(db2_ref)
            for cp in loads:
                cp.wait()

        g2 = vd_ref[VD_LN2_G:VD_LN2_G + 1, :]

        for r0 in range(0, tm, half):
            rows = slice(r0, r0 + half)
            x1 = xh_ref[rows, :] * vd_ref[VD_LN1_G:VD_LN1_G + 1, :] + vd_ref[VD_LN1_B:VD_LN1_B + 1, :]
            x1b = x1.astype(BF16)
            x1b_ref[rows, :] = x1b
            acc = alpha * x1
            for (off, sz), gp_ref, up_ref in zip(slabs, gp_refs, up_refs):
                gp = _dot_nt(x1b, wg_v[off:off + sz, :])
                up = _dot_nt(x1b, wu_v[off:off + sz, :])
                gp_ref[rows, :] = gp.astype(BF16)
                up_ref[rows, :] = up.astype(BF16)
                acc = acc + _dot((gp * _sigmoid(gp) * up).astype(BF16), wd_v[off:off + sz, :])
            xh2, rstd2 = _ln_stats(acc)
            err = xh2 * g2 + vd_ref[VD_LN2_B:VD_LN2_B + 1, :] - tgt_ref[rows, :]
            loss_ref[...] += _colsum(jnp.sum(err * err, axis=1, keepdims=True)) * (0.5 / d)
            dy = err * (1.0 / d)
            dg2_ref[...] += _colsum(dy * xh2)
            db2_ref[...] += _colsum(dy)
            dr2_ref[rows, :] = _ln_bwd(dy * g2, xh2, rstd2)

    row = lambda w: pl.BlockSpec((tm, w), lambda i: (i, 0))
    act = [sz for _, sz in slabs] * 2
    return pl.pallas_call(
        body, name="fwd_mlp", grid=(n,),
        in_specs=[row(d), row(d), ANY, ANY, ANY, _full(vd.shape)],
        out_specs=[row(sz) for sz in act] + [row(d), row(d), _full((8, LANES)), _full((1, d)), _full((1, d))],
        out_shape=[jax.ShapeDtypeStruct((t, sz), BF16) for sz in act]
        + [jax.ShapeDtypeStruct((t, d), BF16), jax.ShapeDtypeStruct((t, d), F32),
           jax.ShapeDtypeStruct((8, LANES), F32), jax.ShapeDtypeStruct((1, d), F32), jax.ShapeDtypeStruct((1, d), F32)],
        scratch_shapes=[pltpu.VMEM(wg.shape, BF16), pltpu.VMEM(wu.shape, BF16), pltpu.VMEM(wd.shape, BF16), pltpu.SemaphoreType.DMA((3,))],
        compiler_params=_cparams(),
    )(xh1, tgt, wg, wu, wd, vd)


def _bwd_mlp_slab(j, slab, dr2, prev, x1b, gp, up, wg, wu, wd, alpha, tm):
    t, d = dr2.shape
    off, sz = slab
    n = t // tm
    first = prev is None

    def body(*refs):
        if first:
            dr_ref, x1b_ref, gp_ref, up_ref, wg_hbm, wu_hbm, wd_hbm = refs[:7]
        else:
            dr_ref, dxp_ref, x1b_ref, gp_ref, up_ref, wg_hbm, wu_hbm, wd_hbm = refs[:8]
        dx_ref, dwg_hbm, dwu_hbm, dwd_hbm, dwg16_hbm, dwu16_hbm, dwd16_hbm, ag, au, ad, wg_v, wu_v, wd_v, copy_sems = refs[-14:]

        @pl.when(pl.program_id(0) == 0)
        def _():
            loads = _start_copies(copy_sems, [(src.at[pl.ds(off, sz)], dst) for src, dst in ((wg_hbm, wg_v), (wu_hbm, wu_v), (wd_hbm, wd_v))])
            ag[...] = jnp.zeros_like(ag)
            au[...] = jnp.zeros_like(au)
            ad[...] = jnp.zeros_like(ad)
            for cp in loads:
                cp.wait()

        dr = dr_ref[...]
        drb = dr.astype(BF16)
        x1b = x1b_ref[...]
        gpv = gp_ref[...].astype(F32)
        upv = up_ref[...].astype(F32)
        dh = _dot_nt(drb, wd_v[...])
        sg = _sigmoid(gpv)
        silu = gpv * sg
        ad[...] += _dot_tn((silu * upv).astype(BF16), drb)
        dgp = (dh * upv * (sg * (1.0 + gpv * (1.0 - sg)))).astype(BF16)
        dup = (dh * silu).astype(BF16)
        ag[...] += _dot_tn(dgp, x1b)
        au[...] += _dot_tn(dup, x1b)
        base = alpha * dr if first else dxp_ref[...]
        dx_ref[...] = base + _dot(dgp, wg_v[...]) + _dot(dup, wu_v[...])

        @pl.when(pl.program_id(0) == n - 1)
        def _():
            rows = pl.ds(off, sz)
            stores = _start_copies(copy_sems, [(ag, dwg_hbm.at[rows]), (au, dwu_hbm.at[rows]), (ad, dwd_hbm.at[rows])])
            for acc, stage in ((ag, wg_v), (au, wu_v), (ad, wd_v)):
                stage[...] = acc[...].astype(BF16)
            stores += _start_copies(copy_sems, [(wg_v, dwg16_hbm.at[rows]), (wu_v, dwu16_hbm.at[rows]), (wd_v, dwd16_hbm.at[rows])], first=3)
            for cp in stores:
                cp.wait()


    row = lambda w: pl.BlockSpec((tm, w), lambda i: (i, 0))
    ins = [dr2] + ([] if first else [prev[0]]) + [x1b, gp, up, wg, wu, wd] + ([] if first else list(prev[1:]))
    in_specs = [row(d)] + ([] if first else [row(d)]) + [row(d), row(sz), row(sz), ANY, ANY, ANY] + ([] if first else [ANY] * 6)
    return pl.pallas_call(
        body, name=f"bwd_mlp_{j}", grid=(n,),
        in_specs=in_specs,
        out_specs=[row(d)] + [ANY] * 6,
        out_shape=[jax.ShapeDtypeStruct((t, d), F32)] + [jax.ShapeDtypeStruct(wg.shape, F32)] * 3 + [jax.ShapeDtypeStruct(wg.shape, BF16)] * 3,
        scratch_shapes=[pltpu.VMEM((sz, d), F32)] * 3 + [pltpu.VMEM((sz, d), BF16)] * 3 + [pltpu.SemaphoreType.DMA((6,))],
        input_output_aliases={} if first else {8 + a: 1 + a for a in range(6)},
        compiler_params=_cparams(),
    )(*ins)


def _bwd_mix(dx1, saved, wi, wo, wstt, cwf, tabs, vq, vd, token, alpha, tm):
    xb, pag, y, xh1, rs, zu_s, mg_s, vhat_s, gv_s, vnb_s, yhat_s, hf_s = saved
    t, d = xh1.shape
    q = wi.shape[2]
    nc, n_pairs = CONV_BLOCK // CHUNK, q // LANES
    n = t // tm
    nb = tm // CONV_BLOCK
    assert tm % CONV_BLOCK == 0

    def body(dx1_ref, xb_ref, pag_ref, y_ref, xh_ref, rs_ref, zu_ref, mg_ref, vhat_ref, gv_ref, vnb_ref, yhat_ref, hf_ref,
             wi_hbm, wo_hbm, wstt_ref, cwf_ref, fwd_ref, fwd_halo_ref, shift_ref, taps_ref, inv_ref, inv_taps_ref, vq_ref, vd_ref,
             token_ref,
             gx_ref, dwi_hbm, dwo_hbm, dws_ref, dbs_ref, dcw_ref, dvq_ref, dvd_ref,
             wi_v, wo_v, awi, awo, dyb_ref, later_ref, dbm_ref, gf_ref, dgf_ref, copy_sems):
        i = pl.program_id(0)

        @pl.when(i == 0)
        def _():
            loads = _start_copies(copy_sems, [(wi_hbm, wi_v), (wo_hbm, wo_v)])
            for r in (awi, awo, dws_ref, dbm_ref, dgf_ref, dvq_ref, dvd_ref, dyb_ref, later_ref):
                r[...] = jnp.zeros_like(r)
            gf_ref[...] = _dot(taps_ref[...], _split(cwf_ref[...], True))
            for cp in loads:
                cp.wait()

        dr1b_parts, dproj_parts = [None] * nb, [None] * nb

        def sub_tile(b):
            rows = slice(b * CONV_BLOCK, (b + 1) * CONV_BLOCK)
            dx1v = dx1_ref[rows, :]
            xh = xh_ref[rows, :]
            rsv = rs_ref[rows, :]
            dvd_ref[VD_LN1_G:VD_LN1_G + 1, :] += _colsum(dx1v * xh)
            dvd_ref[VD_LN1_B:VD_LN1_B + 1, :] += _colsum(dx1v)
            dr1 = _ln_bwd(dx1v * vd_ref[VD_LN1_G:VD_LN1_G + 1, :], xh, rsv[:, RS_LN1:RS_LN1 + 1])
            dr1b = dr1.astype(BF16)
            yield
            dy = _dot_nt(dr1b, wo_v[...])
            yield
            vhat = vhat_ref[rows, :]
            sgu_g = vq_ref[VQ_SGU_G:VQ_SGU_G + 1, :]
            doa = dy[:, 0:q]
            dm = doa * zu_ref[rows, :].astype(F32)
            dpu = (doa * mg_ref[rows, :].astype(F32)).astype(BF16)
            acc = dm[0:CHUNK]
            for c in range(1, nc):
                acc = acc + dm[c * CHUNK:(c + 1) * CHUNK]
            dbm_ref[...] += acc
            pa = pag_ref[rows, 0:q].astype(F32)
            sg = _sigmoid(pag_ref[rows, q:2 * q].astype(F32))
            yhat = yhat_ref[rows, :]
            cln_g = vq_ref[VQ_CLN_G:VQ_CLN_G + 1, :]
            yn = yhat * cln_g + vq_ref[VQ_CLN_B:VQ_CLN_B + 1, :]
            sy = _sigmoid(yn)
            dyn = dy[:, q:2 * q] * (sy * (1.0 + yn * (1.0 - sy)))
            dvq_ref[VQ_CLN_G:VQ_CLN_G + 1, :] += _colsum(dyn * yhat)
            dvq_ref[VQ_CLN_B:VQ_CLN_B + 1, :] += _colsum(dyn)
            dyc = _ln_bwd(dyn * cln_g, yhat, rsv[:, RS_CONV:RS_CONV + 1])
            dvq_ref[VQ_CONV_B:VQ_CONV_B + 1, :] += _colsum(dyc)
            dyb_ref[b, 0:CONV_BLOCK, :] = dyc
            yield
            wgrads = _mix_wgrad(dm, vnb_ref[rows, :], nc, n_pairs)
            dvn = _mix(wstt_ref, dm, nc, n_pairs)
            own = _dot(fwd_ref[...], _split(dyb_ref[b]))
            with_later = own + _dot(fwd_halo_ref[...], _split(later_ref[...]))
            later_ref[...] = dyb_ref[b, 0:HALO, :]
            yield
            for p, g in enumerate(wgrads):
                dws_ref[p] += g
            dvq_ref[VQ_SGU_G:VQ_SGU_G + 1, :] += _colsum(dvn * vhat)
            dvq_ref[VQ_SGU_B:VQ_SGU_B + 1, :] += _colsum(dvn)
            dpv = (_ln_bwd(dvn * sgu_g, vhat, rsv[:, RS_SGU:RS_SGU + 1]) * gv_ref[rows, :].astype(F32)).astype(BF16)
            dgf_ref[...] += _cmul(_cmul(own, shift_ref[...]), hf_ref[b * 2 * DFT_F:(b + 1) * 2 * DFT_F, :], conj_b=True)
            product = _split(_cmul(with_later, gf_ref[...], conj_b=True))
            yield
            dh = _dot(inv_ref[...], product)
            yield
            da = (dh * sg).astype(BF16)
            dg = (dh * pa * (sg * (1.0 - sg))).astype(BF16)
            yield
            gx = alpha * dr1
            for dpj, wj in zip((dpu, dpv, da, dg), range(4)):
                gx = gx + _dot_nt(dpj, wi_v[wj])
            gx_ref[rows, :] = gx
            dr1b_parts[b], dproj_parts[b] = dr1b, (dpu, dpv, da, dg)

        _interleave([sub_tile(b) for b in reversed(range(nb))])

        awo[...] += _dot_tn(y_ref[...], jnp.concatenate(dr1b_parts, axis=0))
        xb = xb_ref[...]
        for j in range(4):
            awi[j] += _dot_tn(xb, jnp.concatenate([part[j] for part in dproj_parts], axis=0))

        @pl.when(i == n - 1)
        def _():
            stores = _start_copies(copy_sems, [(awi, dwi_hbm), (awo, dwo_hbm)])
            lane = lax.broadcasted_iota(jnp.int32, (CHUNK, LANES), 1)
            low = lane < HEAD_DIM
            dbs = jnp.zeros((CHUNK, LANES), F32)
            for p in range(n_pairs):
                grp = dbm_ref[:, p * LANES:(p + 1) * LANES]
                dbs = jnp.where(lane == 2 * p, jnp.sum(jnp.where(low, grp, 0.0), axis=1, keepdims=True), dbs)
                dbs = jnp.where(lane == 2 * p + 1, jnp.sum(jnp.where(low, 0.0, grp), axis=1, keepdims=True), dbs)
            dbs_ref[...] = dbs
            dcw_ref[...] = _dot(inv_taps_ref[...], _split(dgf_ref[...], True))
            for cp in stores:
                cp.wait()

    rev = lambda w: pl.BlockSpec((tm, w), lambda i: (n - 1 - i, 0))
    small = [jax.ShapeDtypeStruct((n_pairs, 2 * CHUNK, CHUNK), F32), jax.ShapeDtypeStruct((CHUNK, LANES), F32),
             jax.ShapeDtypeStruct(cwf.shape, F32), jax.ShapeDtypeStruct(vq.shape, F32), jax.ShapeDtypeStruct(vd.shape, F32)]
    small_ins = [wstt, cwf, tabs["fwd"], tabs["fwd_halo"], tabs["shift"], tabs["taps"], tabs["inv_in"], tabs["inv_taps"], vq, vd]
    return pl.pallas_call(
        body, name="bwd_mix", grid=(n,),
        in_specs=[rev(d), rev(d), rev(2 * q), rev(d), rev(d), rev(RS_COLS), rev(q), rev(q), rev(q), rev(q), rev(q), rev(q),
                  pl.BlockSpec((nb * 2 * DFT_F, q), lambda i: (n - 1 - i, 0)), ANY, ANY] + [_full(a.shape) for a in small_ins] + [ANY],
        out_specs=[rev(d), ANY, ANY] + [_full(s.shape) for s in small],
        out_shape=[jax.ShapeDtypeStruct((t, d), F32), jax.ShapeDtypeStruct(wi.shape, F32), jax.ShapeDtypeStruct(wo.shape, F32)] + small,
        scratch_shapes=[pltpu.VMEM(wi.shape, BF16), pltpu.VMEM(wo.shape, BF16), pltpu.VMEM(wi.shape, F32), pltpu.VMEM(wo.shape, F32),
                        pltpu.VMEM((nb, DFT_N, q), F32), pltpu.VMEM((HALO, q), F32),
                        pltpu.VMEM((CHUNK, q), F32), pltpu.VMEM((2 * DFT_F, q), F32), pltpu.VMEM((2 * DFT_F, q), F32),
                        pltpu.SemaphoreType.DMA((2,))],
        compiler_params=_cparams(),
    )(dx1, xb, pag, y, xh1, rs, zu_s, mg_s, vhat_s, gv_s, vnb_s, yhat_s, hf_s, wi, wo, *small_ins, token)


def _prep(me_arr, w_in, w_out, w_gate_t, w_up_t, w_down, conv_w, kwp):
    kw, cshard = conv_w.shape

    def body(me_ref, wi_ref, wo_ref, wg_ref, wu_ref, wd_ref, cw_ref, oi, oo, og, ou, od, oc):
        for src, dst in ((wi_ref, oi), (wo_ref, oo), (wg_ref, og), (wu_ref, ou), (wd_ref, od)):
            dst[...] = src[...].astype(BF16)
        oc[0:kw, :] = cw_ref[...]
        oc[kw:kwp, :] = jnp.zeros((kwp - kw, cshard), F32)

    ins = (w_in, w_out, w_gate_t, w_up_t, w_down, conv_w)
    outs = [jax.ShapeDtypeStruct((4,) + a.shape, BF16) for a in ins[:5]] + [jax.ShapeDtypeStruct((4, kwp, cshard), F32)]
    grid_spec = pltpu.PrefetchScalarGridSpec(
        num_scalar_prefetch=1, grid=(1,),
        in_specs=[pl.BlockSpec(a.shape, lambda i, me: (0, 0)) for a in ins],
        out_specs=[pl.BlockSpec((None,) + o.shape[1:], lambda i, me: (me[0], 0, 0)) for o in outs])
    return pl.pallas_call(body, name="wprep", grid_spec=grid_spec, out_shape=outs, compiler_params=_cparams())(me_arr, *ins)


def _coords():
    return tuple(lax.axis_index(a) for a in MESH_AXES)


def _other_chips(x, y):
    return [(1 - x, y), (x, 1 - y), (1 - x, 1 - y)]


def _remote(src, dst, send_sem, recv_sem, to):
    return pltpu.make_async_remote_copy(src_ref=src, dst_ref=dst, send_sem=send_sem, recv_sem=recv_sem,
                                        device_id=to, device_id_type=MESH_ID)


def _hbm_call(body, name, ins, out_shape, scratch_shapes, aliases=None):
    return pl.pallas_call(
        body, name=name, in_specs=[ANY] * len(ins), out_specs=[ANY] * len(out_shape), out_shape=out_shape,
        scratch_shapes=scratch_shapes, input_output_aliases=aliases or {},
    )(*ins)


class _Gather:
    def __init__(self, bufs, send_sems, recv_sems):
        self.bufs, self.send_sems, self.recv_sems = bufs, send_sems, recv_sems
        self.x, self.y, self.c = _coords()

    def _copies(self, stage):
        x, y, c = self.x, self.y, self.c
        for a, buf in enumerate(self.bufs):
            hr = buf.shape[1] // 2
            for j, chip in enumerate(_other_chips(x, y)):
                if stage == "ici_out":
                    ref, k, to = buf.at[2 * x + y, pl.ds(c * hr, hr)], j, (*chip, c)
                elif stage == "ici_in":
                    ref, k, to = buf.at[2 * chip[0] + chip[1], pl.ds(c * hr, hr)], j, (*chip, c)
                elif stage == "d2d_out":
                    ref, k, to = buf.at[2 * chip[0] + chip[1], pl.ds(c * hr, hr)], 3 + j, (x, y, 1 - c)
                else:
                    ref, k, to = buf.at[2 * chip[0] + chip[1], pl.ds((1 - c) * hr, hr)], 3 + j, (x, y, 1 - c)
                yield _remote(ref, ref, self.send_sems.at[a, k], self.recv_sems.at[a, k], to)

    def start(self):
        for cp in self._copies("ici_out"):
            cp.start()

    def forward(self):
        for landed, onward in zip(self._copies("ici_in"), self._copies("d2d_out")):
            landed.wait_recv()
            onward.start()

    def finish(self):
        for cp in self._copies("d2d_in"):
            cp.wait_recv()
        for stage in ("ici_out", "d2d_out"):
            for cp in self._copies(stage):
                cp.wait_send()


def _gather_sems(n):
    return [pltpu.SemaphoreType.DMA((n, 6)), pltpu.SemaphoreType.DMA((n, 6))]


def _gather_shards(bufs):
    n = len(bufs)

    def body(*refs):
        g = _Gather(refs[n:2 * n], *refs[2 * n:])
        g.start()
        g.forward()
        g.finish()

    return _hbm_call(body, "gather_shards", bufs, [jax.ShapeDtypeStruct(s.shape, s.dtype) for s in bufs],
                     _gather_sems(n), aliases={a: a for a in range(n)})


def _pair_swap(name, arrs):
    n = len(arrs)

    def body(*refs):
        src, land = refs[:n], refs[n:2 * n]
        send_sems, recv_sems = refs[2 * n:]
        x, y, c = _coords()
        copies = []
        for a in range(n):
            s = src[a].at[pl.ds(0, arrs[a].shape[0]), 1 - c] if arrs[a].ndim == 4 else src[a].at[1 - c]
            copies.append(_remote(s, land[a], send_sems.at[a], recv_sems.at[a], (x, y, 1 - c)))
            copies[-1].start()
        for cp in copies:
            cp.wait()

    outs = [jax.ShapeDtypeStruct(s.shape[:-3] + s.shape[-2:], s.dtype) for s in arrs]
    return _hbm_call(body, name, arrs, outs, [pltpu.SemaphoreType.DMA((n,)), pltpu.SemaphoreType.DMA((n,))])


class _Exchange:
    def __init__(self, src, dst, send_sems, recv_sems):
        self.src, self.dst, self.send_sems, self.recv_sems = src, dst, send_sems, recv_sems
        self.x, self.y, self.c = _coords()

    def _copies(self, incoming):
        x, y, c = self.x, self.y, self.c
        for a, (s, d) in enumerate(zip(self.src, self.dst)):
            for j, chip in enumerate(_other_chips(x, y)):
                slot = 2 * chip[0] + chip[1]
                if incoming:
                    out, into = d.at[slot], d.at[slot]
                else:
                    out, into = (s.at[slot] if len(s.shape) == 3 else s), d.at[2 * x + y]
                yield _remote(out, into, self.send_sems.at[a, j], self.recv_sems.at[a, j], (*chip, c))

    def start(self):
        for cp in self._copies(False):
            cp.start()

    def finish(self):
        for cp in self._copies(True):
            cp.wait_recv()
        for cp in self._copies(False):
            cp.wait_send()


def _exchange_shapes(arrs):
    return [jax.ShapeDtypeStruct((4,) + s.shape[-2:], s.dtype) for s in arrs]


class _FlatSems:
    def __init__(self, ref):
        self.ref = ref

    @property
    def at(self):
        return self

    def __getitem__(self, idx):
        return self.ref.at[3 * idx[0] + idx[1]]


HBM = pl.BlockSpec(memory_space=pltpu.HBM)
SEM = pl.BlockSpec(memory_space=pltpu.SEMAPHORE)
DATAFLOW = pltpu.SideEffectType.DATAFLOW_SIDE_EFFECTING


def _exchange_start(name, arrs):
    n = len(arrs)
    lands = _exchange_shapes(arrs)

    def body(*refs):
        src, land = refs[:n], refs[n:2 * n]
        send_sems, recv_sems = refs[2 * n:2 * n + 2]
        token = refs[-1]
        _Exchange(src, land, _FlatSems(send_sems), _FlatSems(recv_sems)).start()
        token[...] = jnp.zeros_like(token)

    hbm = lambda a: pltpu.with_memory_space_constraint(a, pltpu.HBM)
    outs = pl.pallas_call(
        body, name=name,
        out_shape=(pltpu.SemaphoreType.DMA((3 * n,)), pltpu.SemaphoreType.DMA((3 * n,)),
                   *[pltpu.HBM(a.shape, a.dtype) for a in arrs], *[pltpu.HBM(s.shape, s.dtype) for s in lands],
                   jax.ShapeDtypeStruct((SUBLANES, LANES), F32)),
        in_specs=[HBM] * (2 * n), out_specs=(SEM, SEM, *[HBM] * (2 * n), pl.BlockSpec(memory_space=pltpu.VMEM)),
        input_output_aliases={a: 2 + a for a in range(2 * n)},
        compiler_params=pltpu.CompilerParams(has_side_effects=DATAFLOW),
    )(*[hbm(a) for a in arrs], *[hbm(lax.empty(s.shape, s.dtype)) for s in lands])
    return outs[:-1], outs[-1]


def _exchange_wait(name, started, after):
    send_sems, recv_sems, *bufs = started
    n = len(bufs) // 2

    def body(*refs):
        src, land = refs[:n], refs[n:2 * n]
        send_sems, recv_sems = refs[2 * n:2 * n + 2]
        _Exchange(src, land, _FlatSems(send_sems), _FlatSems(recv_sems)).finish()

    outs = pl.pallas_call(
        body, name=name,
        out_shape=tuple(pltpu.HBM(b.shape, b.dtype) for b in bufs),
        in_specs=[HBM] * (2 * n) + [SEM, SEM] + [ANY] * len(after), out_specs=tuple([HBM] * (2 * n)),
        input_output_aliases={a: a for a in range(2 * n)},
        compiler_params=pltpu.CompilerParams(has_side_effects=DATAFLOW),
    )(*bufs, send_sems, recv_sems, *after)
    return list(outs[:n]), list(outs[n:])


def _pair_gather(name, halves):
    n = len(halves)

    def body(*refs):
        src, dst = refs[:n], refs[n:2 * n]
        send_sems, recv_sems = refs[2 * n:]
        x, y, c = _coords()
        copies = [_remote(src[a], dst[a], send_sems.at[a], recv_sems.at[a], (x, y, 1 - c)) for a in range(n)]
        for cp in copies:
            cp.start()
        for cp in copies:
            cp.wait()

    outs = [jax.ShapeDtypeStruct(s.shape, s.dtype) for s in halves]
    return _hbm_call(body, name, halves, outs, [pltpu.SemaphoreType.DMA((n,)), pltpu.SemaphoreType.DMA((n,))])


def _pair_sum(a, g, land, c_arr, out_dtype):
    nq, _, hr, cc = g.shape

    def body(c_ref, g_ref, l_ref, o_ref):
        o_ref[...] = (g_ref[...] + l_ref[...].astype(F32)).astype(out_dtype)

    spec = pl.BlockSpec((None, hr, cc), lambda qi, cr: (qi, 0, 0))
    grid_spec = pltpu.PrefetchScalarGridSpec(
        num_scalar_prefetch=1, grid=(nq,),
        in_specs=[pl.BlockSpec((None, None, hr, cc), lambda qi, cr: (qi, cr[0], 0, 0)), spec], out_specs=spec)
    return pl.pallas_call(body, name=f"pair_sum_{a}", grid_spec=grid_spec, out_shape=jax.ShapeDtypeStruct((nq, hr, cc), out_dtype),
                          compiler_params=_cparams())(c_arr, g, land)


def _chip_sum(a, parts, own, me_arr, after):
    _, hr, cc = parts.shape

    def body(me_ref, p_ref, own_ref, after_ref, o_ref):
        for mine in range(4):
            @pl.when(me_ref[0] == mine)
            def _():
                term = lambda j: (own_ref if j == mine else p_ref.at[j])[...].astype(F32)
                o_ref[...] = ((term(0) + term(1)) + term(2)) + term(3)

    own_spec = (pl.BlockSpec((None, hr, cc), lambda i, me: (me[0], 0, 0)) if own.ndim == 3
                else pl.BlockSpec((hr, cc), lambda i, me: (0, 0)))
    grid_spec = pltpu.PrefetchScalarGridSpec(
        num_scalar_prefetch=1, grid=(1,),
        in_specs=[pl.BlockSpec((4, hr, cc), lambda i, me: (0, 0, 0)), own_spec, ANY],
        out_specs=pl.BlockSpec((hr, cc), lambda i, me: (0, 0)))
    return pl.pallas_call(body, name=f"chip_sum_{a}", grid_spec=grid_spec, out_shape=jax.ShapeDtypeStruct((hr, cc), F32),
                          compiler_params=_cparams())(me_arr, parts, own, after)


def _row_block(rows, cols, limit=1 << 20):
    best = 8
    for tr in range(8, rows + 1, 8):
        if rows % tr == 0 and tr * cols * 4 <= limit:
            best = tr
    return best


def _adamw(name, w, g_mine, g_other, m, v, c_arr):
    r, c = w.shape
    hr, cg = g_mine.shape
    tr = hr if r % hr == 0 and hr * cg * 4 <= (3 << 19) else math.gcd(_row_block(hr, cg), r)
    per_half = hr // tr
    bc1 = 1.0 - ADAM_B1 ** ADAM_STEP
    bc2 = 1.0 - ADAM_B2 ** ADAM_STEP

    def body(c_ref, w_ref, gm_ref, go_ref, m_ref, v_ref, go, do, mo, vo):
        gv = jnp.where(pl.program_id(0) // per_half == c_ref[0], gm_ref[:, 0:c], go_ref[:, 0:c])
        mn = ADAM_B1 * m_ref[...] + (1.0 - ADAM_B1) * gv
        vn = ADAM_B2 * v_ref[...] + (1.0 - ADAM_B2) * (gv * gv)
        go[...] = gv
        mo[...] = mn
        vo[...] = vn
        do[...] = -ADAM_LR * ((mn / bc1) / (jnp.sqrt(vn / bc2) + ADAM_EPS) + ADAM_WD * w_ref[...])

    blk = pl.BlockSpec((tr, c), lambda i, cr: (i, 0))
    gblk = pl.BlockSpec((tr, cg), lambda i, cr: (i % per_half, 0))
    grid_spec = pltpu.PrefetchScalarGridSpec(num_scalar_prefetch=1, grid=(r // tr,), in_specs=[blk, gblk, gblk, blk, blk],
                                             out_specs=[blk] * 4)
    return pl.pallas_call(body, name=f"adamw_{name}", grid_spec=grid_spec, out_shape=[jax.ShapeDtypeStruct((r, c), F32)] * 4,
                          compiler_params=_cparams())(c_arr, w, g_mine, g_other, m, v)


def _rows128(a):
    return a.reshape(-1, LANES)


def _pad_rows(a, rows):
    return jnp.pad(a, ((0, rows - a.shape[0]), (0, 0)))


def kernel(x, w_in, sgu_ln_g, sgu_ln_b, w_s, b_s, conv_w, conv_b, conv_ln_g, conv_ln_b, w_out, ln1_g, ln1_b, w_gate, w_up, w_down, ln2_g, ln2_b, loss_target, m_w_in, m_sgu_ln_g, m_sgu_ln_b, m_w_s, m_b_s, m_conv_w, m_conv_b, m_conv_ln_g, m_conv_ln_b, m_w_out, m_ln1_g, m_ln1_b, m_w_gate, m_w_up, m_w_down, m_ln2_g, m_ln2_b, v_w_in, v_sgu_ln_g, v_sgu_ln_b, v_w_s, v_b_s, v_conv_w, v_conv_b, v_conv_ln_g, v_conv_ln_b, v_w_out, v_ln1_g, v_ln1_b, v_w_gate, v_w_up, v_w_down, v_ln2_g, v_ln2_b):
    depth, d, q = w_in.shape
    assert depth == 1 and x.shape[0] == 1
    t = x.shape[1]
    heads = w_s.shape[1]
    kw, cshard = conv_w.shape[1], conv_w.shape[2]
    fs = w_gate.shape[2]
    slabs = _hidden_slabs(4 * fs)
    n_pairs = q // LANES
    assert heads * HEAD_DIM == q and q % LANES == 0 and w_s.shape[2] == CHUNK and 4 * cshard == q and kw - 1 <= HALO
    alpha = (2.0 * depth) ** 0.25
    tm = min(512, t)
    assert t % tm == 0 and tm % CHUNK == 0
    x2, tgt = x[0], loss_target[0]
    mx, my, mc = _coords()
    me = 2 * mx + my
    c_arr = jnp.reshape(mc, (1,)).astype(jnp.int32)

    kwp = -(-kw // 16) * 16
    me_arr = jnp.reshape(me, (1,)).astype(jnp.int32)
    wi, wo, wg, wu, wd, cw4 = _prep(me_arr, w_in[0], w_out[0], w_gate[0].T, w_up[0].T, w_down[0], conv_w[0], kwp)
    wi, wo, cw4 = _gather_shards([wi, wo, cw4])
    wo = wo.reshape(d, d)
    cw = jnp.transpose(cw4, (1, 0, 2)).reshape(kwp, q)
    cwf = _pad_rows(cw[:kw][::-1], kwp)
    tabs = {name: jnp.asarray(tab) for name, tab in _dft_tables(kw, kwp, q).items()}

    wm = jnp.where(jnp.tril(jnp.ones((CHUNK, CHUNK), bool)), w_s[0], 0.0)
    wst = wm.reshape(n_pairs, 2 * CHUNK, CHUNK).astype(BF16)
    wstt = jnp.transpose(wm, (0, 2, 1)).reshape(n_pairs, 2 * CHUNK, CHUNK).astype(BF16)
    bmat = jnp.repeat(b_s[0].T, HEAD_DIM, axis=1)
    vq = _pad_rows(jnp.concatenate([sgu_ln_g, sgu_ln_b, conv_b, conv_ln_g, conv_ln_b], axis=0), 8)
    vd = _pad_rows(jnp.concatenate([ln1_g, ln1_b, ln2_g, ln2_b], axis=0), 8)

    *saved, wg, wu, wd = _fwd_mix(x2, wi, wo, wst, bmat, cwf, tabs, vq, vd, [wg, wu, wd], alpha, tm)
    wg, wu, wd = (w.reshape(4 * fs, d) for w in (wg, wu, wd))
    *acts, x1b, dr2, loss_part, dg2, db2 = _fwd_mlp(saved[3], tgt, wg, wu, wd, vd, alpha, slabs, tm)
    mlp_grads = None
    for j, slab in enumerate(slabs):
        mlp_grads = _bwd_mlp_slab(j, slab, dr2, mlp_grads, x1b, acts[j], acts[len(slabs) + j], wg, wu, wd, alpha, tm)
    dx1 = mlp_grads[0]
    mlp_halves = [b.reshape(4, 2, fs // 2, d) for b in mlp_grads[1:]]
    landed = _pair_swap("pair_swap_mlp", mlp_halves[3:])
    mlp_sums = [_pair_sum(f"mlp{a}", h, l, c_arr, BF16) for a, (h, l) in enumerate(zip(mlp_halves[:3], landed))]
    mlp_started, token = _exchange_start("exchange_mlp_start", mlp_sums)
    grad_x, dwi, dwo, dws, dbs, dcw, dvq, dvd = _bwd_mix(dx1, saved, wi, wo, wstt, cwf, tabs, vq, vd, token, alpha, tm)
    mlp_sums, mlp_parts = _exchange_wait("exchange_mlp_wait", mlp_started, [dwo])

    dws = jnp.where(jnp.tril(jnp.ones((CHUNK, CHUNK), bool)), dws.reshape(heads, CHUNK, CHUNK), 0.0)
    dvd = dvd.at[VD_LN2_G].set(dg2[0]).at[VD_LN2_B].set(db2[0])
    pieces = [_rows128(dws), dbs[:, :heads].T, _rows128(dcw), _rows128(dvq), _rows128(dvd), loss_part]
    sizes = [p.shape[0] for p in pieces]
    rows = -(-sum(sizes) // 16) * 16
    small = _pad_rows(jnp.concatenate(pieces, axis=0), rows)

    big = [dwi, dwo.reshape(4, d // 4, d)]
    halves = [b.reshape(4, 2, b.shape[1] // 2, b.shape[2]) for b in big] + [small.reshape(2, rows // 2, LANES)]
    landed = _pair_swap("pair_swap_mix", halves)
    sums = [_pair_sum(f"mix{a}", h, l, c_arr, BF16) for a, (h, l) in enumerate(zip(halves[:-1], landed[:-1]))]
    sums.append(_pair_sum("small", halves[-1][None], landed[-1][None], c_arr, F32)[0])
    mix_started, token = _exchange_start("exchange_mix_start", sums)

    out, raw = {}, {}

    def finish(first, names, parts, sums, after):
        mine = [_chip_sum(first + a, p, s, me_arr, after) for a, (p, s) in enumerate(zip(parts, sums))]
        other = _pair_gather(f"pair_gather_{first}", mine)
        for a, nm in enumerate(names):
            w_, m_, v_ = weights[nm]
            if nm in ("w_gate", "w_up"):
                raw[nm] = _adamw(nm, w_[0].T, mine[a], other[a], m_[0].T, v_[0].T, c_arr)
                out[nm] = [o.T for o in raw[nm]]
            else:
                raw[nm] = out[nm] = _adamw(nm, w_[0], mine[a], other[a], m_[0], v_[0], c_arr)
        return mine[-1], other[-1]

    weights = {"w_in": (w_in, m_w_in, v_w_in), "w_out": (w_out, m_w_out, v_w_out), "w_gate": (w_gate, m_w_gate, v_w_gate),
               "w_up": (w_up, m_w_up, v_w_up), "w_down": (w_down, m_w_down, v_w_down)}
    finish(2, ["w_gate", "w_up", "w_down"], mlp_parts, mlp_sums, token)
    sums, parts = _exchange_wait("exchange_mix_wait", mix_started, [raw[nm][1] for nm in ("w_gate", "w_up", "w_down")])
    small_mine, small_other = finish(5, ["w_in", "w_out"], parts, sums, parts[0])

    def pack(ws, bs, vqs, vds):
        ps = [_rows128(ws[0]), bs[0], jnp.zeros((sizes[2], LANES), F32),
              _rows128(_pad_rows(jnp.concatenate(vqs, axis=0), 8)), _rows128(_pad_rows(jnp.concatenate(vds, axis=0), 8)),
              jnp.zeros((sizes[5], LANES), F32)]
        return _pad_rows(jnp.concatenate(ps, axis=0), rows)

    packed = _adamw(
        "small",
        pack(w_s, b_s, [sgu_ln_g, sgu_ln_b, conv_b, conv_ln_g, conv_ln_b], [ln1_g, ln1_b, ln2_g, ln2_b]), small_mine, small_other,
        pack(m_w_s, m_b_s, [m_sgu_ln_g, m_sgu_ln_b, m_conv_b, m_conv_ln_g, m_conv_ln_b], [m_ln1_g, m_ln1_b, m_ln2_g, m_ln2_b]),
        pack(v_w_s, v_b_s, [v_sgu_ln_g, v_sgu_ln_b, v_conv_b, v_conv_ln_g, v_conv_ln_b], [v_ln1_g, v_ln1_b, v_ln2_g, v_ln2_b]),
        c_arr)

    offs = [sum(sizes[:i]) for i in range(len(sizes))]
    g_cw_full = packed[0][offs[2]:offs[2] + sizes[2]].reshape(kwp, q)
    g_cw = lax.dynamic_slice(g_cw_full, (0, me * cshard), (kwp, cshard))
    out["conv_w"] = _adamw("conv_w", _pad_rows(conv_w[0], kwp), g_cw, g_cw, _pad_rows(m_conv_w[0], kwp), _pad_rows(v_conv_w[0], kwp), c_arr)
    out["conv_w"] = [o[:kw] for o in out["conv_w"]]

    def unpack(p):
        vq_o = p[offs[3]:offs[3] + sizes[3]].reshape(8, q)
        vd_o = p[offs[4]:offs[4] + sizes[4]].reshape(8, d)
        return {"w_s": p[offs[0]:offs[0] + sizes[0]].reshape(heads, CHUNK, CHUNK), "b_s": p[offs[1]:offs[1] + sizes[1]],
                "sgu_ln_g": vq_o[VQ_SGU_G], "sgu_ln_b": vq_o[VQ_SGU_B], "conv_b": vq_o[VQ_CONV_B],
                "conv_ln_g": vq_o[VQ_CLN_G], "conv_ln_b": vq_o[VQ_CLN_B],
                "ln1_g": vd_o[VD_LN1_G], "ln1_b": vd_o[VD_LN1_B], "ln2_g": vd_o[VD_LN2_G], "ln2_b": vd_o[VD_LN2_B]}

    small_out = [unpack(p) for p in packed]
    loss = packed[0][offs[5], 0]
    names = ["w_in", "sgu_ln_g", "sgu_ln_b", "w_s", "b_s", "conv_w", "conv_b", "conv_ln_g", "conv_ln_b", "w_out",
             "ln1_g", "ln1_b", "w_gate", "w_up", "w_down", "ln2_g", "ln2_b"]
    result = [loss, grad_x[None]]
    for kind in range(4):
        for nm in names:
            val = out[nm][kind] if nm in out else small_out[kind][nm]
            result.append(val[None])
    return tuple(result)
```

```python
import functools
import math

import jax
import numpy as np
import jax.numpy as jnp
from jax import lax
from jax.experimental import pallas as pl
from jax.experimental.pallas import tpu as pltpu

F32 = jnp.float32
BF16 = jnp.bfloat16

LN_EPS = 1e-5
HEAD_DIM = 64
CHUNK = 128
HALO = 32
LANES = 128
MXU_N = 256
ADAM_LR, ADAM_B1, ADAM_B2, ADAM_EPS, ADAM_WD, ADAM_STEP = 0.001, 0.9, 0.999, 1e-08, 0.01, 10
VMEM_LIMIT = 63 * 1024 * 1024
MESH_AXES = ("x", "y", "c")
MESH_ID = pl.DeviceIdType.MESH


def _dot(a, b):
    return jnp.dot(a, b, preferred_element_type=F32)


def _dot_nt(a, b):
    return lax.dot_general(a, b, (((1,), (1,)), ((), ())), preferred_element_type=F32)


def _dot_tn(a, b):
    return lax.dot_general(a, b, (((0,), (0,)), ((), ())), preferred_element_type=F32)


def _sigmoid(v):
    return 1.0 / (1.0 + jnp.exp(-v))


def _gelu(v):
    cdf = 0.5 * (1.0 + lax.erf(v * (1.0 / math.sqrt(2.0))))
    pdf = jnp.exp(-0.5 * v * v) * (1.0 / math.sqrt(2.0 * math.pi))
    return v * cdf, cdf + v * pdf


def _ln_stats(v):
    mu = jnp.mean(v, axis=-1, keepdims=True)
    d = v - mu
    rstd = lax.rsqrt(jnp.mean(d * d, axis=-1, keepdims=True) + LN_EPS)
    return d * rstd, rstd


def _ln_bwd(dxhat, xhat, rstd):
    m1 = jnp.mean(dxhat, axis=-1, keepdims=True)
    m2 = jnp.mean(dxhat * xhat, axis=-1, keepdims=True)
    return rstd * (dxhat - m1 - xhat * m2)


def _colsum(v):
    return jnp.sum(v, axis=0, keepdims=True)


def _pair_lanes(v, nc, p):
    return jnp.concatenate([v[c * CHUNK:(c + 1) * CHUNK, p * LANES:(p + 1) * LANES] for c in range(nc)], axis=1)


def _unpair(parts, nc):
    rows = [jnp.concatenate([part[:, c * LANES:(c + 1) * LANES] for part in parts], axis=1) for c in range(nc)]
    return jnp.concatenate(rows, axis=0)


def _low_head(nc):
    lane = lax.broadcasted_iota(jnp.int32, (CHUNK, nc * LANES), 1)
    return (lane & (LANES - 1)) < HEAD_DIM


def _mix(wst_ref, v, nc, n_pairs):
    vb = v.astype(BF16)
    low = _low_head(nc)
    parts = []
    for p in range(n_pairs):
        r = _dot(wst_ref[p], _pair_lanes(vb, nc, p))
        parts.append(jnp.where(low, r[:CHUNK], r[CHUNK:]))
    return _unpair(parts, nc)


def _mix_wgrad(dm, vn, nc, n_pairs):
    low = _low_head(nc)
    vb = vn.astype(BF16)
    out = []
    for p in range(n_pairs):
        a = _pair_lanes(dm, nc, p)
        lhs = jnp.concatenate([jnp.where(low, a, 0.0), jnp.where(low, 0.0, a)], axis=0).astype(BF16)
        out.append(_dot_nt(lhs, _pair_lanes(vb, nc, p)))
    return out


SUBLANES = 8


CONV_BLOCK = 256
DFT_N = CONV_BLOCK + HALO
DFT_F = -(-(DFT_N // 2 + 1) // SUBLANES) * SUBLANES


def _terms(m, exact):
    hi = m.astype(np.float32).astype(BF16)
    lo = (m.astype(np.float32) - hi.astype(np.float32)).astype(BF16)
    return np.concatenate([hi, hi, lo] if exact else [hi, hi], axis=1)


def _split(v, exact=False):
    hi = v.astype(BF16)
    lo = (v - hi.astype(F32)).astype(BF16)
    return jnp.concatenate([hi, lo, hi] if exact else [hi, lo], axis=0)


def _dft_tables(kw, kwp, q):
    nf = DFT_N // 2 + 1
    ang = 2.0 * np.pi * np.arange(nf)[:, None] * np.arange(DFT_N)[None, :] / DFT_N
    fwd = np.zeros((2 * DFT_F, DFT_N))
    fwd[:nf], fwd[DFT_F:DFT_F + nf] = np.cos(ang), -np.sin(ang)
    weight = np.full((nf, 1), 2.0 / DFT_N)
    weight[0] = weight[-1] = 1.0 / DFT_N
    inv = np.zeros((DFT_N, 2 * DFT_F))
    inv[:, :nf], inv[:, DFT_F:DFT_F + nf] = (np.cos(ang) * weight).T, (-np.sin(ang) * weight).T
    inv_taps = np.zeros((kwp, 2 * DFT_F))
    inv_taps[:kw] = inv[kw - 1::-1][:kw]
    shift = np.zeros((2 * DFT_F, q), np.float32)
    shift[:nf], shift[DFT_F:DFT_F + nf] = np.cos(ang[:, HALO:HALO + 1]), -np.sin(ang[:, HALO:HALO + 1])
    return {"fwd": _terms(fwd, False), "fwd_halo": _terms(fwd[:, CONV_BLOCK:], False), "shift": shift,
            "inv_out": _terms(inv[HALO:HALO + CONV_BLOCK], False), "inv_in": _terms(inv[:CONV_BLOCK], False),
            "taps": _terms(fwd[:, :kwp], True), "inv_taps": _terms(inv_taps, True)}


def _cmul(a, b, conj_b=False):
    ar, ai, br, bi = a[:DFT_F], a[DFT_F:], b[:DFT_F], b[DFT_F:]
    if conj_b:
        return jnp.concatenate([ar * br + ai * bi, ai * br - ar * bi], axis=0)
    return jnp.concatenate([ar * br - ai * bi, ar * bi + ai * br], axis=0)


def _interleave(sub_tiles):
    waiting, live = list(sub_tiles), []
    while waiting or live:
        if waiting:
            live.append(waiting.pop(0))
        for g in list(live):
            try:
                next(g)
            except StopIteration:
                live.remove(g)


def _start_copies(sems, pairs, first=0):
    copies = [pltpu.make_async_copy(src, dst, sems.at[first + k]) for k, (src, dst) in enumerate(pairs)]
    for cp in copies:
        cp.start()
    return copies


def _cparams():
    return pltpu.CompilerParams(dimension_semantics=("arbitrary",), vmem_limit_bytes=VMEM_LIMIT)


def _full(shape):
    return pl.BlockSpec(shape, lambda i: (0,) * len(shape))


ANY = pl.BlockSpec(memory_space=pl.ANY)

VQ_SGU_G, VQ_SGU_B, VQ_CONV_B, VQ_CLN_G, VQ_CLN_B = range(5)
VD_LN1_G, VD_LN1_B, VD_LN2_G, VD_LN2_B = range(4)
RS_LN1, RS_SGU, RS_CONV = range(3)
RS_COLS = 8


def _fwd_mix(x, wi, wo, wst, bmat, cwf, tabs, vq, vd, mlp_w, alpha, tm):
    t, d = x.shape
    q = wi.shape[2]
    nc, n_pairs = CONV_BLOCK // CHUNK, q // LANES
    n = t // tm
    n_in, n_saved = 11, 12
    assert tm % CONV_BLOCK == 0

    def body(x_ref, wi_hbm, wo_hbm, wst_ref, bmat_ref, cwf_ref, fwd_ref, taps_ref, inv_ref, vq_ref, vd_ref, *rest):
        (xb_ref, pag_ref, y_ref, xh_ref, rs_ref, zu_ref, mg_ref, vhat_ref, gv_ref, vnb_ref, yhat_ref, hf_ref) = rest[3:3 + n_saved]
        gathered = rest[3 + n_saved:6 + n_saved]
        wi_v, wo_v, hb_ref, gf_ref, send_sems, recv_sems, copy_sems = rest[6 + n_saved:]
        step = pl.program_id(0)

        @pl.when(step == 0)
        def _():
            loads = _start_copies(copy_sems, [(wi_hbm, wi_v), (wo_hbm, wo_v)])
            _Gather(gathered, send_sems, recv_sems).start()
            hb_ref[...] = jnp.zeros_like(hb_ref)
            gf_ref[...] = _dot(taps_ref[...], _split(cwf_ref[...], True))
            for cp in loads:
                cp.wait()

        @pl.when(step == (3 * n) // 4)
        def _():
            _Gather(gathered, send_sems, recv_sems).forward()

        def sub_tile(b):
            rows = slice(b * CONV_BLOCK, (b + 1) * CONV_BLOCK)
            xv = x_ref[rows, :]
            xb = xv.astype(BF16)
            xb_ref[rows, :] = xb
            pu, pv, pa, pg = (_dot(xb, wi_v[j]) for j in range(4))
            yield
            pag_ref[rows, 0:q] = pa.astype(BF16)
            pag_ref[rows, q:2 * q] = pg.astype(BF16)
            zu, gu = _gelu(pu)
            zv, gv = _gelu(pv)
            vhat, rstd_v = _ln_stats(zv)
            vnb = (vhat * vq_ref[VQ_SGU_G:VQ_SGU_G + 1, :] + vq_ref[VQ_SGU_B:VQ_SGU_B + 1, :]).astype(BF16)
            hb_ref[HALO + b * CONV_BLOCK:HALO + (b + 1) * CONV_BLOCK, :] = pa * _sigmoid(pg)
            yield
            mixed = _mix(wst_ref, vnb, nc, n_pairs) + jnp.concatenate([bmat_ref[...]] * nc, axis=0)
            spectrum = _dot(fwd_ref[...], _split(hb_ref[b * CONV_BLOCK:b * CONV_BLOCK + DFT_N, :]))
            yield
            y_ref[rows, 0:q] = (zu * mixed).astype(BF16)
            zu_ref[rows, :] = zu
            mg_ref[rows, :] = mixed * gu
            vhat_ref[rows, :] = vhat
            gv_ref[rows, :] = gv
            vnb_ref[rows, :] = vnb
            hf_ref[b * 2 * DFT_F:(b + 1) * 2 * DFT_F, :] = spectrum
            product = _split(_cmul(gf_ref[...], spectrum))
            yield
            yc = _dot(inv_ref[...], product) + vq_ref[VQ_CONV_B:VQ_CONV_B + 1, :]
            yield
            yhat, rstd_c = _ln_stats(yc)
            yhat_ref[rows, :] = yhat
            yn = yhat * vq_ref[VQ_CLN_G:VQ_CLN_G + 1, :] + vq_ref[VQ_CLN_B:VQ_CLN_B + 1, :]
            y_ref[rows, q:2 * q] = (yn * _sigmoid(yn)).astype(BF16)
            yield
            r1 = alpha * xv + _dot(y_ref[rows, :], wo_v[...])
            yield
            xhat, rstd1 = _ln_stats(r1)
            xh_ref[rows, :] = xhat
            col = lax.broadcasted_iota(jnp.int32, (CONV_BLOCK, RS_COLS), 1)
            rs_ref[rows, :] = jnp.where(col == RS_LN1, rstd1, jnp.where(col == RS_SGU, rstd_v, jnp.where(col == RS_CONV, rstd_c, 0.0)))

        _interleave([sub_tile(b) for b in range(tm // CONV_BLOCK)])
        hb_ref[0:HALO, :] = hb_ref[tm:tm + HALO, :]

        @pl.when(step == n - 1)
        def _():
            _Gather(gathered, send_sems, recv_sems).finish()

    row = lambda w: pl.BlockSpec((tm, w), lambda i: (i, 0))
    widths = [(d, BF16), (2 * q, BF16), (d, BF16), (d, F32), (RS_COLS, F32), (q, F32), (q, F32), (q, F32), (q, F32), (q, BF16), (q, F32)]
    assert len(widths) + 1 == n_saved
    small_ins = [wst, bmat, cwf, tabs["fwd"], tabs["taps"], tabs["inv_out"], vq, vd]
    return pl.pallas_call(
        body, name="fwd_mix", grid=(n,),
        in_specs=[row(d), ANY, ANY] + [_full(a.shape) for a in small_ins] + [ANY] * 3,
        out_specs=[row(w) for w, _ in widths] + [pl.BlockSpec((tm // CONV_BLOCK * 2 * DFT_F, q), lambda i: (i, 0))] + [ANY] * 3,
        out_shape=[jax.ShapeDtypeStruct((t, w), dt) for w, dt in widths] + [jax.ShapeDtypeStruct((t // CONV_BLOCK * 2 * DFT_F, q), F32)]
        + [jax.ShapeDtypeStruct(b.shape, b.dtype) for b in mlp_w],
        scratch_shapes=[pltpu.VMEM(wi.shape, BF16), pltpu.VMEM(wo.shape, BF16), pltpu.VMEM((HALO + tm, q), F32),
                        pltpu.VMEM((2 * DFT_F, q), F32)] + _gather_sems(3) + [pltpu.SemaphoreType.DMA((2,))],
        input_output_aliases={n_in + a: n_saved + a for a in range(3)},
        compiler_params=_cparams(),
    )(x, wi, wo, *small_ins, *mlp_w)


MLP_SLABS = 4


def _hidden_slabs(f):
    assert f % MXU_N == 0
    tiles = f // MXU_N
    sizes = [(tiles // MLP_SLABS + (1 if j < tiles % MLP_SLABS else 0)) * MXU_N for j in range(MLP_SLABS)]
    return [(sum(sizes[:j]), sz) for j, sz in enumerate(sizes) if sz]


def _fwd_mlp(xh1, tgt, wg, wu, wd, vd, alpha, slabs, tm):
    t, d = xh1.shape
    n = t // tm
    ns = len(slabs)
    half = tm // 2 if tm % 32 == 0 else tm

    def body(xh_ref, tgt_ref, wg_hbm, wu_hbm, wd_hbm, vd_ref, *rest):
        gp_refs, up_refs = rest[:ns], rest[ns:2 * ns]
        x1b_ref, dr2_ref, loss_ref, dg2_ref, db2_ref, wg_v, wu_v, wd_v, copy_sems = rest[2 * ns:]

        @pl.when(pl.program_id(0) == 0)
        def _():
            loads = _start_copies(copy_sems, [(wg_hbm, wg_v), (wu_hbm, wu_v), (wd_hbm, wd_v)])
            loss_ref[...] = jnp.zeros_like(loss_ref)
            dg2_ref[...] = jnp.zeros_like(dg2_ref)
            db2_ref[...] = jnp.zeros_like(db2_ref)
            for cp in loads:
                cp.wait()

        g2 = vd_ref[VD_LN2_G:VD_LN2_G + 1, :]

        for r0 in range(0, tm, half):
            rows = slice(r0, r0 + half)
            x1 = xh_ref[rows, :] * vd_ref[VD_LN1_G:VD_LN1_G + 1, :] + vd_ref[VD_LN1_B:VD_LN1_B + 1, :]
            x1b = x1.astype(BF16)
            x1b_ref[rows, :] = x1b
            acc = alpha * x1
            for (off, sz), gp_ref, up_ref in zip(slabs, gp_refs, up_refs):
                gp = _dot_nt(x1b, wg_v[off:off + sz, :])
                up = _dot_nt(x1b, wu_v[off:off + sz, :])
                gp_ref[rows, :] = gp.astype(BF16)
                up_ref[rows, :] = up.astype(BF16)
                acc = acc + _dot((gp * _sigmoid(gp) * up).astype(BF16), wd_v[off:off + sz, :])
            xh2, rstd2 = _ln_stats(acc)
            err = xh2 * g2 + vd_ref[VD_LN2_B:VD_LN2_B + 1, :] - tgt_ref[rows, :]
            loss_ref[...] += _colsum(jnp.sum(err * err, axis=1, keepdims=True)) * (0.5 / d)
            dy = err * (1.0 / d)
            dg2_ref[...] += _colsum(dy * xh2)
            db2_ref[...] += _colsum(dy)
            dr2_ref[rows, :] = _ln_bwd(dy * g2, xh2, rstd2)

    row = lambda w: pl.BlockSpec((tm, w), lambda i: (i, 0))
    act = [sz for _, sz in slabs] * 2
    return pl.pallas_call(
        body, name="fwd_mlp", grid=(n,),
        in_specs=[row(d), row(d), ANY, ANY, ANY, _full(vd.shape)],
        out_specs=[row(sz) for sz in act] + [row(d), row(d), _full((8, LANES)), _full((1, d)), _full((1, d))],
        out_shape=[jax.ShapeDtypeStruct((t, sz), BF16) for sz in act]
        + [jax.ShapeDtypeStruct((t, d), BF16), jax.ShapeDtypeStruct((t, d), F32),
           jax.ShapeDtypeStruct((8, LANES), F32), jax.ShapeDtypeStruct((1, d), F32), jax.ShapeDtypeStruct((1, d), F32)],
        scratch_shapes=[pltpu.VMEM(wg.shape, BF16), pltpu.VMEM(wu.shape, BF16), pltpu.VMEM(wd.shape, BF16), pltpu.SemaphoreType.DMA((3,))],
        compiler_params=_cparams(),
    )(xh1, tgt, wg, wu, wd, vd)


def _bwd_mlp_slab(j, slab, dr2, prev, x1b, gp, up, wg, wu, wd, alpha, tm):
    t, d = dr2.shape
    off, sz = slab
    n = t // tm
    first = prev is None

    def body(*refs):
        if first:
            dr_ref, x1b_ref, gp_ref, up_ref, wg_hbm, wu_hbm, wd_hbm = refs[:7]
        else:
            dr_ref, dxp_ref, x1b_ref, gp_ref, up_ref, wg_hbm, wu_hbm, wd_hbm = refs[:8]
        dx_ref, dwg_hbm, dwu_hbm, dwd_hbm, dwg16_hbm, dwu16_hbm, dwd16_hbm, ag, au, ad, wg_v, wu_v, wd_v, copy_sems = refs[-14:]

        @pl.when(pl.program_id(0) == 0)
        def _():
            loads = _start_copies(copy_sems, [(src.at[pl.ds(off, sz)], dst) for src, dst in ((wg_hbm, wg_v), (wu_hbm, wu_v), (wd_hbm, wd_v))])
            ag[...] = jnp.zeros_like(ag)
            au[...] = jnp.zeros_like(au)
            ad[...] = jnp.zeros_like(ad)
            for cp in loads:
                cp.wait()

        dr = dr_ref[...]
        drb = dr.astype(BF16)
        x1b = x1b_ref[...]
        gpv = gp_ref[...].astype(F32)
        upv = up_ref[...].astype(F32)
        dh = _dot_nt(drb, wd_v[...])
        sg = _sigmoid(gpv)
        silu = gpv * sg
        ad[...] += _dot_tn((silu * upv).astype(BF16), drb)
        dgp = (dh * upv * (sg * (1.0 + gpv * (1.0 - sg)))).astype(BF16)
        dup = (dh * silu).astype(BF16)
        ag[...] += _dot_tn(dgp, x1b)
        au[...] += _dot_tn(dup, x1b)
        base = alpha * dr if first else dxp_ref[...]
        dx_ref[...] = base + _dot(dgp, wg_v[...]) + _dot(dup, wu_v[...])

        @pl.when(pl.program_id(0) == n - 1)
        def _():
            rows = pl.ds(off, sz)
            stores = _start_copies(copy_sems, [(ag, dwg_hbm.at[rows]), (au, dwu_hbm.at[rows]), (ad, dwd_hbm.at[rows])])
            for acc, stage in ((ag, wg_v), (au, wu_v), (ad, wd_v)):
                stage[...] = acc[...].astype(BF16)
            stores += _start_copies(copy_sems, [(wg_v, dwg16_hbm.at[rows]), (wu_v, dwu16_hbm.at[rows]), (wd_v, dwd16_hbm.at[rows])], first=3)
            for cp in stores:
                cp.wait()


    row = lambda w: pl.BlockSpec((tm, w), lambda i: (i, 0))
    ins = [dr2] + ([] if first else [prev[0]]) + [x1b, gp, up, wg, wu, wd] + ([] if first else list(prev[1:]))
    in_specs = [row(d)] + ([] if first else [row(d)]) + [row(d), row(sz), row(sz), ANY, ANY, ANY] + ([] if first else [ANY] * 6)
    return pl.pallas_call(
        body, name=f"bwd_mlp_{j}", grid=(n,),
        in_specs=in_specs,
        out_specs=[row(d)] + [ANY] * 6,
        out_shape=[jax.ShapeDtypeStruct((t, d), F32)] + [jax.ShapeDtypeStruct(wg.shape, F32)] * 3 + [jax.ShapeDtypeStruct(wg.shape, BF16)] * 3,
        scratch_shapes=[pltpu.VMEM((sz, d), F32)] * 3 + [pltpu.VMEM((sz, d), BF16)] * 3 + [pltpu.SemaphoreType.DMA((6,))],
        input_output_aliases={} if first else {8 + a: 1 + a for a in range(6)},
        compiler_params=_cparams(),
    )(*ins)


SMALL_VD = CHUNK
SMALL_CW = CHUNK + 8
SMALL_VQ = CHUNK + 8
SMALL_LOSS = CHUNK + 16
SMALL_BS = CHUNK + 24


def _small_rows(kwp):
    return -(-(SMALL_CW + max(kwp, 24 + SUBLANES)) // 16) * 16


def _bwd_mix(dx1, saved, wi, wo, wstt, cwf, tabs, vq, vd, mlp_small, token, alpha, tm):
    xb, pag, y, xh1, rs, zu_s, mg_s, vhat_s, gv_s, vnb_s, yhat_s, hf_s = saved
    t, d = xh1.shape
    q = wi.shape[2]
    nc, n_pairs = CONV_BLOCK // CHUNK, q // LANES
    n = t // tm
    nb = tm // CONV_BLOCK
    assert tm % CONV_BLOCK == 0

    def body(dx1_ref, xb_ref, pag_ref, y_ref, xh_ref, rs_ref, zu_ref, mg_ref, vhat_ref, gv_ref, vnb_ref, yhat_ref, hf_ref,
             wi_hbm, wo_hbm, wstt_ref, cwf_ref, fwd_ref, fwd_halo_ref, shift_ref, taps_ref, inv_ref, inv_taps_ref, vq_ref, vd_ref,
             loss_ref, dg2_ref, db2_ref, token_ref,
             gx_ref, dwi_hbm, dwo_hbm, small_hbm,
             wi_v, wo_v, awi, awo, dyb_ref, later_ref, dbm_ref, gf_ref, dgf_ref, small_ref, copy_sems):
        i = pl.program_id(0)

        @pl.when(i == 0)
        def _():
            loads = _start_copies(copy_sems, [(wi_hbm, wi_v), (wo_hbm, wo_v)])
            for r in (awi, awo, small_ref, dbm_ref, dgf_ref, dyb_ref, later_ref):
                r[...] = jnp.zeros_like(r)
            gf_ref[...] = _dot(taps_ref[...], _split(cwf_ref[...], True))
            for cp in loads:
                cp.wait()

        dr1b_parts, dproj_parts = [None] * nb, [None] * nb

        def sub_tile(b):
            rows = slice(b * CONV_BLOCK, (b + 1) * CONV_BLOCK)
            dx1v = dx1_ref[rows, :]
            xh = xh_ref[rows, :]
            rsv = rs_ref[rows, :]
            small_ref[SMALL_VD + VD_LN1_G:SMALL_VD + VD_LN1_G + 1, :] += _colsum(dx1v * xh)
            small_ref[SMALL_VD + VD_LN1_B:SMALL_VD + VD_LN1_B + 1, :] += _colsum(dx1v)
            dr1 = _ln_bwd(dx1v * vd_ref[VD_LN1_G:VD_LN1_G + 1, :], xh, rsv[:, RS_LN1:RS_LN1 + 1])
            dr1b = dr1.astype(BF16)
            yield
            dy = _dot_nt(dr1b, wo_v[...])
            yield
            vhat = vhat_ref[rows, :]
            sgu_g = vq_ref[VQ_SGU_G:VQ_SGU_G + 1, :]
            doa = dy[:, 0:q]
            dm = doa * zu_ref[rows, :]
            dpu = (doa * mg_ref[rows, :]).astype(BF16)
            acc = dm[0:CHUNK]
            for c in range(1, nc):
                acc = acc + dm[c * CHUNK:(c + 1) * CHUNK]
            dbm_ref[...] += acc
            pa = pag_ref[rows, 0:q].astype(F32)
            sg = _sigmoid(pag_ref[rows, q:2 * q].astype(F32))
            yhat = yhat_ref[rows, :]
            cln_g = vq_ref[VQ_CLN_G:VQ_CLN_G + 1, :]
            yn = yhat * cln_g + vq_ref[VQ_CLN_B:VQ_CLN_B + 1, :]
            sy = _sigmoid(yn)
            dyn = dy[:, q:2 * q] * (sy * (1.0 + yn * (1.0 - sy)))
            small_ref[SMALL_VQ + VQ_CLN_G:SMALL_VQ + VQ_CLN_G + 1, q:2 * q] += _colsum(dyn * yhat)
            small_ref[SMALL_VQ + VQ_CLN_B:SMALL_VQ + VQ_CLN_B + 1, q:2 * q] += _colsum(dyn)
            dyc = _ln_bwd(dyn * cln_g, yhat, rsv[:, RS_CONV:RS_CONV + 1])
            small_ref[SMALL_VQ + VQ_CONV_B:SMALL_VQ + VQ_CONV_B + 1, q:2 * q] += _colsum(dyc)
            dyb_ref[b, 0:CONV_BLOCK, :] = dyc
            yield
            wgrads = _mix_wgrad(dm, vnb_ref[rows, :], nc, n_pairs)
            dvn = _mix(wstt_ref, dm, nc, n_pairs)
            own = _dot(fwd_ref[...], _split(dyb_ref[b]))
            with_later = own + _dot(fwd_halo_ref[...], _split(later_ref[...]))
            later_ref[...] = dyb_ref[b, 0:HALO, :]
            yield
            for p, g in enumerate(wgrads):
                for half in range(2):
                    small_ref[0:CHUNK, (2 * p + half) * CHUNK:(2 * p + half + 1) * CHUNK] += g[half * CHUNK:(half + 1) * CHUNK]
            small_ref[SMALL_VQ + VQ_SGU_G:SMALL_VQ + VQ_SGU_G + 1, q:2 * q] += _colsum(dvn * vhat)
            small_ref[SMALL_VQ + VQ_SGU_B:SMALL_VQ + VQ_SGU_B + 1, q:2 * q] += _colsum(dvn)
            dpv = (_ln_bwd(dvn * sgu_g, vhat, rsv[:, RS_SGU:RS_SGU + 1]) * gv_ref[rows, :]).astype(BF16)
            dgf_ref[...] += _cmul(_cmul(own, shift_ref[...]), hf_ref[b * 2 * DFT_F:(b + 1) * 2 * DFT_F, :], conj_b=True)
            product = _split(_cmul(with_later, gf_ref[...], conj_b=True))
            yield
            dh = _dot(inv_ref[...], product)
            yield
            da = (dh * sg).astype(BF16)
            dg = (dh * pa * (sg * (1.0 - sg))).astype(BF16)
            yield
            gx = alpha * dr1
            for dpj, wj in zip((dpu, dpv, da, dg), range(4)):
                gx = gx + _dot_nt(dpj, wi_v[wj])
            gx_ref[rows, :] = gx
            dr1b_parts[b], dproj_parts[b] = dr1b, (dpu, dpv, da, dg)

        _interleave([sub_tile(b) for b in reversed(range(nb))])

        awo[...] += _dot_tn(y_ref[...], jnp.concatenate(dr1b_parts, axis=0))
        xb = xb_ref[...]
        for j in range(4):
            awi[j] += _dot_tn(xb, jnp.concatenate([part[j] for part in dproj_parts], axis=0))

        @pl.when(i == n - 1)
        def _():
            stores = _start_copies(copy_sems, [(awi, dwi_hbm), (awo, dwo_hbm)])
            lane = lax.broadcasted_iota(jnp.int32, (CHUNK, LANES), 1)
            low = lane < HEAD_DIM
            dbs = jnp.zeros((CHUNK, LANES), F32)
            for p in range(n_pairs):
                grp = dbm_ref[:, p * LANES:(p + 1) * LANES]
                dbs = jnp.where(lane == 2 * p, jnp.sum(jnp.where(low, grp, 0.0), axis=1, keepdims=True), dbs)
                dbs = jnp.where(lane == 2 * p + 1, jnp.sum(jnp.where(low, 0.0, grp), axis=1, keepdims=True), dbs)
            tril = lax.broadcasted_iota(jnp.int32, (CHUNK, CHUNK), 0) >= lax.broadcasted_iota(jnp.int32, (CHUNK, CHUNK), 1)
            for h in range(2 * n_pairs):
                block = small_ref[0:CHUNK, h * CHUNK:(h + 1) * CHUNK]
                small_ref[0:CHUNK, h * CHUNK:(h + 1) * CHUNK] = jnp.where(tril, block, 0.0)
            small_ref[SMALL_VD + VD_LN2_G:SMALL_VD + VD_LN2_G + 1, :] = dg2_ref[...]
            small_ref[SMALL_VD + VD_LN2_B:SMALL_VD + VD_LN2_B + 1, :] = db2_ref[...]
            small_ref[SMALL_CW:SMALL_CW + kwp, 0:q] = _dot(inv_taps_ref[...], _split(dgf_ref[...], True))
            small_ref[SMALL_LOSS:SMALL_LOSS + SUBLANES, q:q + LANES] = loss_ref[...]
            small_ref[SMALL_BS:SMALL_BS + SUBLANES, q:q + LANES] = jnp.transpose(dbs)[0:SUBLANES]
            stores += _start_copies(copy_sems, [(small_ref, small_hbm)], first=2)
            for cp in stores:
                cp.wait()

    rev = lambda w: pl.BlockSpec((tm, w), lambda i: (n - 1 - i, 0))
    kwp = cwf.shape[0]
    small = jax.ShapeDtypeStruct((_small_rows(kwp), 2 * q), F32)
    small_ins = [wstt, cwf, tabs["fwd"], tabs["fwd_halo"], tabs["shift"], tabs["taps"], tabs["inv_in"], tabs["inv_taps"], vq, vd,
                 *mlp_small]
    return pl.pallas_call(
        body, name="bwd_mix", grid=(n,),
        in_specs=[rev(d), rev(d), rev(2 * q), rev(d), rev(d), rev(RS_COLS), rev(q), rev(q), rev(q), rev(q), rev(q), rev(q),
                  pl.BlockSpec((nb * 2 * DFT_F, q), lambda i: (n - 1 - i, 0)), ANY, ANY] + [_full(a.shape) for a in small_ins] + [ANY],
        out_specs=[rev(d), ANY, ANY, ANY],
        out_shape=[jax.ShapeDtypeStruct((t, d), F32), jax.ShapeDtypeStruct(wi.shape, F32), jax.ShapeDtypeStruct(wo.shape, F32), small],
        scratch_shapes=[pltpu.VMEM(wi.shape, BF16), pltpu.VMEM(wo.shape, BF16), pltpu.VMEM(wi.shape, F32), pltpu.VMEM(wo.shape, F32),
                        pltpu.VMEM((nb, DFT_N, q), F32), pltpu.VMEM((HALO, q), F32),
                        pltpu.VMEM((CHUNK, q), F32), pltpu.VMEM((2 * DFT_F, q), F32), pltpu.VMEM((2 * DFT_F, q), F32),
                        pltpu.VMEM(small.shape, F32), pltpu.SemaphoreType.DMA((3,))],
        compiler_params=_cparams(),
    )(dx1, xb, pag, y, xh1, rs, zu_s, mg_s, vhat_s, gv_s, vnb_s, yhat_s, hf_s, wi, wo, *small_ins, token)


def _prep(me_arr, w_in, w_out, w_gate_t, w_up_t, w_down, conv_w, kwp):
    kw, cshard = conv_w.shape

    def body(me_ref, wi_ref, wo_ref, wg_ref, wu_ref, wd_ref, cw_ref, oi, oo, og, ou, od, oc):
        for src, dst in ((wi_ref, oi), (wo_ref, oo), (wg_ref, og), (wu_ref, ou), (wd_ref, od)):
            dst[...] = src[...].astype(BF16)
        oc[0:kw, :] = cw_ref[...]
        oc[kw:kwp, :] = jnp.zeros((kwp - kw, cshard), F32)

    ins = (w_in, w_out, w_gate_t, w_up_t, w_down, conv_w)
    outs = [jax.ShapeDtypeStruct((4,) + a.shape, BF16) for a in ins[:5]] + [jax.ShapeDtypeStruct((4, kwp, cshard), F32)]
    grid_spec = pltpu.PrefetchScalarGridSpec(
        num_scalar_prefetch=1, grid=(1,),
        in_specs=[pl.BlockSpec(a.shape, lambda i, me: (0, 0)) for a in ins],
        out_specs=[pl.BlockSpec((None,) + o.shape[1:], lambda i, me: (me[0], 0, 0)) for o in outs])
    return pl.pallas_call(body, name="wprep", grid_spec=grid_spec, out_shape=outs, compiler_params=_cparams())(me_arr, *ins)


def _coords():
    return tuple(lax.axis_index(a) for a in MESH_AXES)


def _other_chips(x, y):
    return [(1 - x, y), (x, 1 - y), (1 - x, 1 - y)]


def _remote(src, dst, send_sem, recv_sem, to):
    return pltpu.make_async_remote_copy(src_ref=src, dst_ref=dst, send_sem=send_sem, recv_sem=recv_sem,
                                        device_id=to, device_id_type=MESH_ID)


def _hbm_call(body, name, ins, out_shape, scratch_shapes, aliases=None):
    return pl.pallas_call(
        body, name=name, in_specs=[ANY] * len(ins), out_specs=[ANY] * len(out_shape), out_shape=out_shape,
        scratch_shapes=scratch_shapes, input_output_aliases=aliases or {},
    )(*ins)


class _Gather:
    def __init__(self, bufs, send_sems, recv_sems):
        self.bufs, self.send_sems, self.recv_sems = bufs, send_sems, recv_sems
        self.x, self.y, self.c = _coords()

    def _copies(self, stage):
        x, y, c = self.x, self.y, self.c
        for a, buf in enumerate(self.bufs):
            hr = buf.shape[1] // 2
            for j, chip in enumerate(_other_chips(x, y)):
                if stage == "ici_out":
                    ref, k, to = buf.at[2 * x + y, pl.ds(c * hr, hr)], j, (*chip, c)
                elif stage == "ici_in":
                    ref, k, to = buf.at[2 * chip[0] + chip[1], pl.ds(c * hr, hr)], j, (*chip, c)
                elif stage == "d2d_out":
                    ref, k, to = buf.at[2 * chip[0] + chip[1], pl.ds(c * hr, hr)], 3 + j, (x, y, 1 - c)
                else:
                    ref, k, to = buf.at[2 * chip[0] + chip[1], pl.ds((1 - c) * hr, hr)], 3 + j, (x, y, 1 - c)
                yield _remote(ref, ref, self.send_sems.at[a, k], self.recv_sems.at[a, k], to)

    def start(self):
        for cp in self._copies("ici_out"):
            cp.start()

    def forward(self):
        for landed, onward in zip(self._copies("ici_in"), self._copies("d2d_out")):
            landed.wait_recv()
            onward.start()

    def finish(self):
        for cp in self._copies("d2d_in"):
            cp.wait_recv()
        for stage in ("ici_out", "d2d_out"):
            for cp in self._copies(stage):
                cp.wait_send()


def _gather_sems(n):
    return [pltpu.SemaphoreType.DMA((n, 6)), pltpu.SemaphoreType.DMA((n, 6))]


def _gather_shards(bufs):
    n = len(bufs)

    def body(*refs):
        g = _Gather(refs[n:2 * n], *refs[2 * n:])
        g.start()
        g.forward()
        g.finish()

    return _hbm_call(body, "gather_shards", bufs, [jax.ShapeDtypeStruct(s.shape, s.dtype) for s in bufs],
                     _gather_sems(n), aliases={a: a for a in range(n)})


def _pair_swap(name, arrs):
    n = len(arrs)

    def body(*refs):
        src, land = refs[:n], refs[n:2 * n]
        send_sems, recv_sems = refs[2 * n:]
        x, y, c = _coords()
        copies = []
        for a in range(n):
            s = src[a].at[pl.ds(0, arrs[a].shape[0]), 1 - c] if arrs[a].ndim == 4 else src[a].at[1 - c]
            copies.append(_remote(s, land[a], send_sems.at[a], recv_sems.at[a], (x, y, 1 - c)))
            copies[-1].start()
        for cp in copies:
            cp.wait()

    outs = [jax.ShapeDtypeStruct(s.shape[:-3] + s.shape[-2:], s.dtype) for s in arrs]
    return _hbm_call(body, name, arrs, outs, [pltpu.SemaphoreType.DMA((n,)), pltpu.SemaphoreType.DMA((n,))])


class _Exchange:
    def __init__(self, src, dst, send_sems, recv_sems):
        self.src, self.dst, self.send_sems, self.recv_sems = src, dst, send_sems, recv_sems
        self.x, self.y, self.c = _coords()

    def _copies(self, incoming):
        x, y, c = self.x, self.y, self.c
        for a, (s, d) in enumerate(zip(self.src, self.dst)):
            for j, chip in enumerate(_other_chips(x, y)):
                slot = 2 * chip[0] + chip[1]
                if incoming:
                    out, into = d.at[slot], d.at[slot]
                else:
                    out, into = (s.at[slot] if len(s.shape) == 3 else s), d.at[2 * x + y]
                yield _remote(out, into, self.send_sems.at[a, j], self.recv_sems.at[a, j], (*chip, c))

    def start(self):
        for cp in self._copies(False):
            cp.start()

    def finish(self):
        for cp in self._copies(True):
            cp.wait_recv()
        for cp in self._copies(False):
            cp.wait_send()


def _exchange_shapes(arrs):
    return [jax.ShapeDtypeStruct((4,) + s.shape[-2:], s.dtype) for s in arrs]


class _FlatSems:
    def __init__(self, ref):
        self.ref = ref

    @property
    def at(self):
        return self

    def __getitem__(self, idx):
        return self.ref.at[3 * idx[0] + idx[1]]


HBM = pl.BlockSpec(memory_space=pltpu.HBM)
SEM = pl.BlockSpec(memory_space=pltpu.SEMAPHORE)
DATAFLOW = pltpu.SideEffectType.DATAFLOW_SIDE_EFFECTING


def _exchange_start(name, arrs):
    n = len(arrs)
    lands = _exchange_shapes(arrs)

    def body(*refs):
        src, land = refs[:n], refs[n:2 * n]
        send_sems, recv_sems = refs[2 * n:2 * n + 2]
        token = refs[-1]
        _Exchange(src, land, _FlatSems(send_sems), _FlatSems(recv_sems)).start()
        token[...] = jnp.zeros_like(token)

    hbm = lambda a: pltpu.with_memory_space_constraint(a, pltpu.HBM)
    outs = pl.pallas_call(
        body, name=name,
        out_shape=(pltpu.SemaphoreType.DMA((3 * n,)), pltpu.SemaphoreType.DMA((3 * n,)),
                   *[pltpu.HBM(a.shape, a.dtype) for a in arrs], *[pltpu.HBM(s.shape, s.dtype) for s in lands],
                   jax.ShapeDtypeStruct((SUBLANES, LANES), F32)),
        in_specs=[HBM] * (2 * n), out_specs=(SEM, SEM, *[HBM] * (2 * n), pl.BlockSpec(memory_space=pltpu.VMEM)),
        input_output_aliases={a: 2 + a for a in range(2 * n)},
        compiler_params=pltpu.CompilerParams(has_side_effects=DATAFLOW),
    )(*[hbm(a) for a in arrs], *[hbm(lax.empty(s.shape, s.dtype)) for s in lands])
    return outs[:-1], outs[-1]


def _exchange_wait(name, started, after):
    send_sems, recv_sems, *bufs = started
    n = len(bufs) // 2

    def body(*refs):
        src, land = refs[:n], refs[n:2 * n]
        send_sems, recv_sems = refs[2 * n:2 * n + 2]
        _Exchange(src, land, _FlatSems(send_sems), _FlatSems(recv_sems)).finish()

    outs = pl.pallas_call(
        body, name=name,
        out_shape=tuple(pltpu.HBM(b.shape, b.dtype) for b in bufs),
        in_specs=[HBM] * (2 * n) + [SEM, SEM] + [ANY] * len(after), out_specs=tuple([HBM] * (2 * n)),
        input_output_aliases={a: a for a in range(2 * n)},
        compiler_params=pltpu.CompilerParams(has_side_effects=DATAFLOW),
    )(*bufs, send_sems, recv_sems, *after)
    return list(outs[:n]), list(outs[n:])


def _pair_gather(name, halves):
    n = len(halves)

    def body(*refs):
        src, dst = refs[:n], refs[n:2 * n]
        send_sems, recv_sems = refs[2 * n:]
        x, y, c = _coords()
        copies = [_remote(src[a], dst[a], send_sems.at[a], recv_sems.at[a], (x, y, 1 - c)) for a in range(n)]
        for cp in copies:
            cp.start()
        for cp in copies:
            cp.wait()

    outs = [jax.ShapeDtypeStruct(s.shape, s.dtype) for s in halves]
    return _hbm_call(body, name, halves, outs, [pltpu.SemaphoreType.DMA((n,)), pltpu.SemaphoreType.DMA((n,))])


def _pair_sum(a, g, land, c_arr, out_dtype):
    nq, _, hr, cc = g.shape

    def body(c_ref, g_ref, l_ref, o_ref):
        o_ref[...] = (g_ref[...] + l_ref[...].astype(F32)).astype(out_dtype)

    spec = pl.BlockSpec((None, hr, cc), lambda qi, cr: (qi, 0, 0))
    grid_spec = pltpu.PrefetchScalarGridSpec(
        num_scalar_prefetch=1, grid=(nq,),
        in_specs=[pl.BlockSpec((None, None, hr, cc), lambda qi, cr: (qi, cr[0], 0, 0)), spec], out_specs=spec)
    return pl.pallas_call(body, name=f"pair_sum_{a}", grid_spec=grid_spec, out_shape=jax.ShapeDtypeStruct((nq, hr, cc), out_dtype),
                          compiler_params=_cparams())(c_arr, g, land)


def _chip_sum(a, parts, own, me_arr, after):
    _, hr, cc = parts.shape

    def body(me_ref, p_ref, own_ref, after_ref, o_ref):
        for mine in range(4):
            @pl.when(me_ref[0] == mine)
            def _():
                term = lambda j: (own_ref if j == mine else p_ref.at[j])[...].astype(F32)
                o_ref[...] = ((term(0) + term(1)) + term(2)) + term(3)

    own_spec = (pl.BlockSpec((None, hr, cc), lambda i, me: (me[0], 0, 0)) if own.ndim == 3
                else pl.BlockSpec((hr, cc), lambda i, me: (0, 0)))
    grid_spec = pltpu.PrefetchScalarGridSpec(
        num_scalar_prefetch=1, grid=(1,),
        in_specs=[pl.BlockSpec((4, hr, cc), lambda i, me: (0, 0, 0)), own_spec, ANY],
        out_specs=pl.BlockSpec((hr, cc), lambda i, me: (0, 0)))
    return pl.pallas_call(body, name=f"chip_sum_{a}", grid_spec=grid_spec, out_shape=jax.ShapeDtypeStruct((hr, cc), F32),
                          compiler_params=_cparams())(me_arr, parts, own, after)


def _row_block(rows, cols, limit=1 << 20):
    best = 8
    for tr in range(8, rows + 1, 8):
        if rows % tr == 0 and tr * cols * 4 <= limit:
            best = tr
    return best


def _adamw(name, w, g_mine, g_other, m, v, c_arr):
    r, c = w.shape
    hr, cg = g_mine.shape
    tr = hr if r % hr == 0 and hr * cg * 4 <= (3 << 19) else math.gcd(_row_block(hr, cg), r)
    per_half = hr // tr
    bc1 = 1.0 - ADAM_B1 ** ADAM_STEP
    bc2 = 1.0 - ADAM_B2 ** ADAM_STEP

    def body(c_ref, w_ref, gm_ref, go_ref, m_ref, v_ref, go, do, mo, vo):
        gv = jnp.where(pl.program_id(0) // per_half == c_ref[0], gm_ref[:, 0:c], go_ref[:, 0:c])
        mn = ADAM_B1 * m_ref[...] + (1.0 - ADAM_B1) * gv
        vn = ADAM_B2 * v_ref[...] + (1.0 - ADAM_B2) * (gv * gv)
        go[...] = gv
        mo[...] = mn
        vo[...] = vn
        do[...] = -ADAM_LR * ((mn / bc1) / (jnp.sqrt(vn / bc2) + ADAM_EPS) + ADAM_WD * w_ref[...])

    blk = pl.BlockSpec((tr, c), lambda i, cr: (i, 0))
    gblk = pl.BlockSpec((tr, cg), lambda i, cr: (i % per_half, 0))
    grid_spec = pltpu.PrefetchScalarGridSpec(num_scalar_prefetch=1, grid=(r // tr,), in_specs=[blk, gblk, gblk, blk, blk],
                                             out_specs=[blk] * 4)
    return pl.pallas_call(body, name=f"adamw_{name}", grid_spec=grid_spec, out_shape=[jax.ShapeDtypeStruct((r, c), F32)] * 4,
                          compiler_params=_cparams())(c_arr, w, g_mine, g_other, m, v)


SMALL_Q = ("sgu_ln_g", "sgu_ln_b", "conv_b", "conv_ln_g", "conv_ln_b")
SMALL_D = ("ln1_g", "ln1_b", "ln2_g", "ln2_b")


def _adamw_small(g_mine, g_other, c_arr, me_arr, params):
    names = list(SMALL_Q) + list(SMALL_D) + ["w_s", "b_s", "conv_w"]
    hr, width = g_mine.shape
    q = width // 2
    heads = params["w_s"][0].shape[1]
    kw, cshard = params["conv_w"][0].shape[1:]
    bc1 = 1.0 - ADAM_B1 ** ADAM_STEP
    bc2 = 1.0 - ADAM_B2 ** ADAM_STEP

    def update(w, g, m, v):
        mn = ADAM_B1 * m + (1.0 - ADAM_B1) * g
        vn = ADAM_B2 * v + (1.0 - ADAM_B2) * (g * g)
        return g, -ADAM_LR * ((mn / bc1) / (jnp.sqrt(vn / bc2) + ADAM_EPS) + ADAM_WD * w), mn, vn

    def body(c_ref, me_ref, gm_ref, go_ref, *refs):
        ins = {nm: refs[3 * k:3 * k + 3] for k, nm in enumerate(names)}
        outs = {nm: refs[3 * len(names) + 4 * k:3 * len(names) + 4 * k + 4] for k, nm in enumerate(names)}
        loss_ref, cw_ref = refs[-2:]
        first, second = gm_ref[...], go_ref[...]
        low = c_ref[0] == 0
        g_all = jnp.concatenate([jnp.where(low, first, second), jnp.where(low, second, first)], axis=0)

        def apply(nm, g, at):
            w, m, v = (r[at] for r in ins[nm])
            for o, val in zip(outs[nm], update(w, g, m, v)):
                o[at] = val

        for row, nm in enumerate(SMALL_Q):
            apply(nm, g_all[SMALL_VQ + row:SMALL_VQ + row + 1, q:2 * q], ...)
        for row, nm in enumerate(SMALL_D):
            apply(nm, g_all[SMALL_VD + row:SMALL_VD + row + 1, :], ...)
        for h in range(heads):
            apply("w_s", g_all[0:CHUNK, h * CHUNK:(h + 1) * CHUNK], (0, h))
        apply("b_s", g_all[SMALL_BS:SMALL_BS + heads, q:q + LANES], 0)
        cw_ref[...] = jnp.zeros_like(cw_ref)
        for chip in range(4):
            @pl.when(me_ref[0] == chip)
            def _():
                cw_ref[...] = g_all[SMALL_CW:SMALL_CW + cw_ref.shape[0], chip * cshard:(chip + 1) * cshard]
        apply("conv_w", cw_ref[0:kw, :], 0)
        loss_ref[...] = g_all[SMALL_LOSS:SMALL_LOSS + SUBLANES, q:q + LANES]

    arrays = [a for nm in names for a in params[nm]]
    out_shape = [jax.ShapeDtypeStruct(params[nm][0].shape, F32) for nm in names for _ in range(4)] + [jax.ShapeDtypeStruct((SUBLANES, LANES), F32)]
    whole = lambda shape: pl.BlockSpec(shape, lambda i, c, me: (0,) * len(shape))
    grid_spec = pltpu.PrefetchScalarGridSpec(
        num_scalar_prefetch=2, grid=(1,),
        in_specs=[whole(g_mine.shape), whole(g_other.shape)] + [whole(a.shape) for a in arrays],
        out_specs=[whole(s.shape) for s in out_shape],
        scratch_shapes=[pltpu.VMEM((-(-kw // SUBLANES) * SUBLANES, cshard), F32)])
    res = pl.pallas_call(body, name="adamw_small", grid_spec=grid_spec, out_shape=out_shape, compiler_params=_cparams())(
        c_arr, me_arr, g_mine, g_other, *arrays)
    return {nm: list(res[4 * k:4 * k + 4]) for k, nm in enumerate(names)}, res[-1]


def _pad_rows(a, rows):
    return jnp.pad(a, ((0, rows - a.shape[0]), (0, 0)))


def kernel(x, w_in, sgu_ln_g, sgu_ln_b, w_s, b_s, conv_w, conv_b, conv_ln_g, conv_ln_b, w_out, ln1_g, ln1_b, w_gate, w_up, w_down, ln2_g, ln2_b, loss_target, m_w_in, m_sgu_ln_g, m_sgu_ln_b, m_w_s, m_b_s, m_conv_w, m_conv_b, m_conv_ln_g, m_conv_ln_b, m_w_out, m_ln1_g, m_ln1_b, m_w_gate, m_w_up, m_w_down, m_ln2_g, m_ln2_b, v_w_in, v_sgu_ln_g, v_sgu_ln_b, v_w_s, v_b_s, v_conv_w, v_conv_b, v_conv_ln_g, v_conv_ln_b, v_w_out, v_ln1_g, v_ln1_b, v_w_gate, v_w_up, v_w_down, v_ln2_g, v_ln2_b):
    depth, d, q = w_in.shape
    assert depth == 1 and x.shape[0] == 1
    t = x.shape[1]
    heads = w_s.shape[1]
    kw, cshard = conv_w.shape[1], conv_w.shape[2]
    fs = w_gate.shape[2]
    slabs = _hidden_slabs(4 * fs)
    n_pairs = q // LANES
    assert heads * HEAD_DIM == q and q % LANES == 0 and w_s.shape[2] == CHUNK and 4 * cshard == q and kw - 1 <= HALO
    alpha = (2.0 * depth) ** 0.25
    tm = min(512, t)
    assert t % tm == 0 and tm % CHUNK == 0
    x2, tgt = x[0], loss_target[0]
    mx, my, mc = _coords()
    me = 2 * mx + my
    c_arr = jnp.reshape(mc, (1,)).astype(jnp.int32)

    kwp = -(-kw // 16) * 16
    me_arr = jnp.reshape(me, (1,)).astype(jnp.int32)
    wi, wo, wg, wu, wd, cw4 = _prep(me_arr, w_in[0], w_out[0], w_gate[0].T, w_up[0].T, w_down[0], conv_w[0], kwp)
    wi, wo, cw4 = _gather_shards([wi, wo, cw4])
    wo = wo.reshape(d, d)
    cw = jnp.transpose(cw4, (1, 0, 2)).reshape(kwp, q)
    cwf = _pad_rows(cw[:kw][::-1], kwp)
    tabs = {name: jnp.asarray(tab) for name, tab in _dft_tables(kw, kwp, q).items()}

    wm = jnp.where(jnp.tril(jnp.ones((CHUNK, CHUNK), bool)), w_s[0], 0.0)
    wst = wm.reshape(n_pairs, 2 * CHUNK, CHUNK).astype(BF16)
    wstt = jnp.transpose(wm, (0, 2, 1)).reshape(n_pairs, 2 * CHUNK, CHUNK).astype(BF16)
    bmat = jnp.repeat(b_s[0].T, HEAD_DIM, axis=1)
    vq = _pad_rows(jnp.concatenate([sgu_ln_g, sgu_ln_b, conv_b, conv_ln_g, conv_ln_b], axis=0), 8)
    vd = _pad_rows(jnp.concatenate([ln1_g, ln1_b, ln2_g, ln2_b], axis=0), 8)

    *saved, wg, wu, wd = _fwd_mix(x2, wi, wo, wst, bmat, cwf, tabs, vq, vd, [wg, wu, wd], alpha, tm)
    wg, wu, wd = (w.reshape(4 * fs, d) for w in (wg, wu, wd))
    *acts, x1b, dr2, loss_part, dg2, db2 = _fwd_mlp(saved[3], tgt, wg, wu, wd, vd, alpha, slabs, tm)
    mlp_grads = None
    for j, slab in enumerate(slabs):
        mlp_grads = _bwd_mlp_slab(j, slab, dr2, mlp_grads, x1b, acts[j], acts[len(slabs) + j], wg, wu, wd, alpha, tm)
    dx1 = mlp_grads[0]
    mlp_halves = [b.reshape(4, 2, fs // 2, d) for b in mlp_grads[1:]]
    landed = _pair_swap("pair_swap_mlp", mlp_halves[3:])
    mlp_sums = [_pair_sum(f"mlp{a}", h, l, c_arr, BF16) for a, (h, l) in enumerate(zip(mlp_halves[:3], landed))]
    mlp_started, token = _exchange_start("exchange_mlp_start", mlp_sums)
    grad_x, dwi, dwo, small = _bwd_mix(dx1, saved, wi, wo, wstt, cwf, tabs, vq, vd, (loss_part, dg2, db2), token, alpha, tm)
    mlp_sums, mlp_parts = _exchange_wait("exchange_mlp_wait", mlp_started, [dwo])

    big = [dwi, dwo.reshape(4, d // 4, d)]
    halves = [b.reshape(4, 2, b.shape[1] // 2, b.shape[2]) for b in big] + [small.reshape(2, small.shape[0] // 2, small.shape[1])]
    landed = _pair_swap("pair_swap_mix", halves)
    sums = [_pair_sum(f"mix{a}", h, l, c_arr, BF16) for a, (h, l) in enumerate(zip(halves[:-1], landed[:-1]))]
    sums.append(_pair_sum("small", halves[-1][None], landed[-1][None], c_arr, F32)[0])
    mix_started, token = _exchange_start("exchange_mix_start", sums)

    out, raw = {}, {}

    def finish(first, names, parts, sums, after):
        mine = [_chip_sum(first + a, p, s, me_arr, after) for a, (p, s) in enumerate(zip(parts, sums))]
        other = _pair_gather(f"pair_gather_{first}", mine)
        for a, nm in enumerate(names):
            w_, m_, v_ = weights[nm]
            if nm in ("w_gate", "w_up"):
                raw[nm] = _adamw(nm, w_[0].T, mine[a], other[a], m_[0].T, v_[0].T, c_arr)
                out[nm] = [o.T for o in raw[nm]]
            else:
                raw[nm] = out[nm] = _adamw(nm, w_[0], mine[a], other[a], m_[0], v_[0], c_arr)
        return mine[-1], other[-1]

    weights = {"w_in": (w_in, m_w_in, v_w_in), "w_out": (w_out, m_w_out, v_w_out), "w_gate": (w_gate, m_w_gate, v_w_gate),
               "w_up": (w_up, m_w_up, v_w_up), "w_down": (w_down, m_w_down, v_w_down)}
    finish(2, ["w_gate", "w_up", "w_down"], mlp_parts, mlp_sums, token)
    sums, parts = _exchange_wait("exchange_mix_wait", mix_started, [raw[nm][1] for nm in ("w_gate", "w_up", "w_down")])
    small_mine, small_other = finish(5, ["w_in", "w_out"], parts, sums, parts[0])

    small_params = {
        "sgu_ln_g": (sgu_ln_g, m_sgu_ln_g, v_sgu_ln_g), "sgu_ln_b": (sgu_ln_b, m_sgu_ln_b, v_sgu_ln_b),
        "conv_b": (conv_b, m_conv_b, v_conv_b), "conv_ln_g": (conv_ln_g, m_conv_ln_g, v_conv_ln_g),
        "conv_ln_b": (conv_ln_b, m_conv_ln_b, v_conv_ln_b), "ln1_g": (ln1_g, m_ln1_g, v_ln1_g), "ln1_b": (ln1_b, m_ln1_b, v_ln1_b),
        "ln2_g": (ln2_g, m_ln2_g, v_ln2_g), "ln2_b": (ln2_b, m_ln2_b, v_ln2_b), "w_s": (w_s, m_w_s, v_w_s),
        "b_s": (b_s, m_b_s, v_b_s), "conv_w": (conv_w, m_conv_w, v_conv_w)}
    small_out, loss_block = _adamw_small(small_mine, small_other, c_arr, me_arr, small_params)
    loss = loss_block[0, 0]
    names = ["w_in", "sgu_ln_g", "sgu_ln_b", "w_s", "b_s", "conv_w", "conv_b", "conv_ln_g", "conv_ln_b", "w_out",
             "ln1_g", "ln1_b", "w_gate", "w_up", "w_down", "ln2_g", "ln2_b"]
    result = [loss, grad_x[None]]
    for kind in range(4):
        for nm in names:
            result.append(out[nm][kind][None] if nm in out else small_out[nm][kind])
    return tuple(result)
```

```python
import math

import jax
import numpy as np
import jax.numpy as jnp
from jax import lax
from jax.experimental import pallas as pl
from jax.experimental.pallas import tpu as pltpu

F32 = jnp.float32
BF16 = jnp.bfloat16

LN_EPS = 1e-5
HEAD_DIM = 64
CHUNK = 128
HALO = 32
LANES = 128
MXU_N = 256
ADAM_LR, ADAM_B1, ADAM_B2, ADAM_EPS, ADAM_WD, ADAM_STEP = 0.001, 0.9, 0.999, 1e-08, 0.01, 10
VMEM_LIMIT = 63 * 1024 * 1024
MESH_AXES = ("x", "y", "c")
MESH_ID = pl.DeviceIdType.MESH


def _dot(a, b):
    return jnp.dot(a, b, preferred_element_type=F32)


def _dot_nt(a, b):
    return lax.dot_general(a, b, (((1,), (1,)), ((), ())), preferred_element_type=F32)


def _dot_tn(a, b):
    return lax.dot_general(a, b, (((0,), (0,)), ((), ())), preferred_element_type=F32)


def _sigmoid(v):
    return 1.0 / (1.0 + jnp.exp(-v))


def _gelu(v):
    cdf = 0.5 * (1.0 + lax.erf(v * (1.0 / math.sqrt(2.0))))
    pdf = jnp.exp(-0.5 * v * v) * (1.0 / math.sqrt(2.0 * math.pi))
    return v * cdf, cdf + v * pdf


def _ln_stats(v):
    mu = jnp.mean(v, axis=-1, keepdims=True)
    d = v - mu
    rstd = lax.rsqrt(jnp.mean(d * d, axis=-1, keepdims=True) + LN_EPS)
    return d * rstd, rstd


def _ln_bwd(dxhat, xhat, rstd):
    m1 = jnp.mean(dxhat, axis=-1, keepdims=True)
    m2 = jnp.mean(dxhat * xhat, axis=-1, keepdims=True)
    return rstd * (dxhat - m1 - xhat * m2)


def _colsum(v):
    return jnp.sum(v, axis=0, keepdims=True)


def _pair_lanes(v, nc, p):
    return jnp.concatenate([v[c * CHUNK:(c + 1) * CHUNK, p * LANES:(p + 1) * LANES] for c in range(nc)], axis=1)


def _unpair(parts, nc):
    rows = [jnp.concatenate([part[:, c * LANES:(c + 1) * LANES] for part in parts], axis=1) for c in range(nc)]
    return jnp.concatenate(rows, axis=0)


def _low_head(nc):
    lane = lax.broadcasted_iota(jnp.int32, (CHUNK, nc * LANES), 1)
    return (lane & (LANES - 1)) < HEAD_DIM


def _mix(wst_ref, v, nc, n_pairs):
    vb = v.astype(BF16)
    low = _low_head(nc)
    parts = []
    for p in range(n_pairs):
        r = _dot(wst_ref[p], _pair_lanes(vb, nc, p))
        parts.append(jnp.where(low, r[:CHUNK], r[CHUNK:]))
    return _unpair(parts, nc)


def _mix_wgrad(dm, vn, nc, n_pairs):
    low = _low_head(nc)
    vb = vn.astype(BF16)
    out = []
    for p in range(n_pairs):
        a = _pair_lanes(dm, nc, p)
        lhs = jnp.concatenate([jnp.where(low, a, 0.0), jnp.where(low, 0.0, a)], axis=0).astype(BF16)
        out.append(_dot_nt(lhs, _pair_lanes(vb, nc, p)))
    return out


SUBLANES = 8


CONV_BLOCK = 256
DFT_N = CONV_BLOCK + HALO
DFT_F = -(-(DFT_N // 2 + 1) // SUBLANES) * SUBLANES


def _terms(m, exact):
    hi = m.astype(np.float32).astype(BF16)
    lo = (m.astype(np.float32) - hi.astype(np.float32)).astype(BF16)
    return np.concatenate([hi, hi, lo] if exact else [hi, hi], axis=1)


def _split(v, exact=False):
    hi = v.astype(BF16)
    lo = (v - hi.astype(F32)).astype(BF16)
    return jnp.concatenate([hi, lo, hi] if exact else [hi, lo], axis=0)


def _dft_tables(kw, kwp, q):
    nf = DFT_N // 2 + 1
    ang = 2.0 * np.pi * np.arange(nf)[:, None] * np.arange(DFT_N)[None, :] / DFT_N
    fwd = np.zeros((2 * DFT_F, DFT_N))
    fwd[:nf], fwd[DFT_F:DFT_F + nf] = np.cos(ang), -np.sin(ang)
    weight = np.full((nf, 1), 2.0 / DFT_N)
    weight[0] = weight[-1] = 1.0 / DFT_N
    inv = np.zeros((DFT_N, 2 * DFT_F))
    inv[:, :nf], inv[:, DFT_F:DFT_F + nf] = (np.cos(ang) * weight).T, (-np.sin(ang) * weight).T
    inv_taps = np.zeros((kwp, 2 * DFT_F))
    inv_taps[:kw] = inv[kw - 1::-1][:kw]
    shift = np.zeros((2 * DFT_F, q), np.float32)
    shift[:nf], shift[DFT_F:DFT_F + nf] = np.cos(ang[:, HALO:HALO + 1]), -np.sin(ang[:, HALO:HALO + 1])
    return {"fwd": _terms(fwd, False), "fwd_halo": _terms(fwd[:, CONV_BLOCK:], False), "shift": shift,
            "inv_out": _terms(inv[HALO:HALO + CONV_BLOCK], False), "inv_in": _terms(inv[:CONV_BLOCK], False),
            "taps": _terms(fwd[:, :kwp], True), "inv_taps": _terms(inv_taps, True)}


def _cmul(a, b, conj_b=False):
    ar, ai, br, bi = a[:DFT_F], a[DFT_F:], b[:DFT_F], b[DFT_F:]
    if conj_b:
        return jnp.concatenate([ar * br + ai * bi, ai * br - ar * bi], axis=0)
    return jnp.concatenate([ar * br - ai * bi, ar * bi + ai * br], axis=0)


def _interleave(sub_tiles):
    waiting, live = list(sub_tiles), []
    while waiting or live:
        if waiting:
            live.append(waiting.pop(0))
        for g in list(live):
            try:
                next(g)
            except StopIteration:
                live.remove(g)


def _start_copies(sems, pairs, first=0):
    copies = [pltpu.make_async_copy(src, dst, sems.at[first + k]) for k, (src, dst) in enumerate(pairs)]
    for cp in copies:
        cp.start()
    return copies


def _cparams():
    return pltpu.CompilerParams(dimension_semantics=("arbitrary",), vmem_limit_bytes=VMEM_LIMIT)


def _full(shape):
    return pl.BlockSpec(shape, lambda i: (0,) * len(shape))


ANY = pl.BlockSpec(memory_space=pl.ANY)

VQ_SGU_G, VQ_SGU_B, VQ_CONV_B, VQ_CLN_G, VQ_CLN_B = range(5)
VD_LN1_G, VD_LN1_B, VD_LN2_G, VD_LN2_B = range(4)
RS_LN1, RS_SGU, RS_CONV = range(3)
RS_COLS = LANES


def _saved_widths(d, q):
    f32 = [d, q, q, q, q, q, RS_COLS]
    bf16 = [d, 2 * q, d, q]
    return f32, bf16


def _saved_views(f32_ref, bf16_ref, d, q):
    views = []
    for ref, widths in zip((f32_ref, bf16_ref), _saved_widths(d, q)):
        for k, w in enumerate(widths):
            views.append(ref.at[pl.ds(0, ref.shape[0]), pl.ds(sum(widths[:k]), w)])
    return views
def _fwd_mix(x, wi, wo, wst, bmat, cwf, tabs, vq, vd, mlp_w, alpha, tm):
    t, d = x.shape
    q = wi.shape[2]
    nc, n_pairs = CONV_BLOCK // CHUNK, q // LANES
    n = t // tm
    n_in, n_saved = 11, 3
    assert tm % CONV_BLOCK == 0

    def body(x_ref, wi_hbm, wo_hbm, wst_ref, bmat_ref, cwf_ref, fwd_ref, taps_ref, inv_ref, vq_ref, vd_ref, *rest):
        f32_ref, bf16_ref, hf_ref = rest[3:3 + n_saved]
        xh_ref, zu_ref, mg_ref, vhat_ref, gv_ref, yhat_ref, rs_ref, xb_ref, pag_ref, y_ref, vnb_ref = _saved_views(f32_ref, bf16_ref, d, q)
        gathered = rest[3 + n_saved:6 + n_saved]
        wi_v, wo_v, hb_ref, gf_ref, send_sems, recv_sems, copy_sems = rest[6 + n_saved:]
        step = pl.program_id(0)

        @pl.when(step == 0)
        def _():
            loads = _start_copies(copy_sems, [(wi_hbm, wi_v), (wo_hbm, wo_v)])
            _Gather(gathered, send_sems, recv_sems).start()
            hb_ref[...] = jnp.zeros_like(hb_ref)
            gf_ref[...] = _dot(taps_ref[...], _split(cwf_ref[...], True))
            for cp in loads:
                cp.wait()

        @pl.when(step == (3 * n) // 4)
        def _():
            _Gather(gathered, send_sems, recv_sems).forward()

        def sub_tile(b):
            rows = slice(b * CONV_BLOCK, (b + 1) * CONV_BLOCK)
            xv = x_ref[rows, :]
            xb = xv.astype(BF16)
            xb_ref[rows, :] = xb
            pu, pv, pa, pg = (_dot(xb, wi_v[j]) for j in range(4))
            yield
            pag_ref[rows, 0:q] = pa.astype(BF16)
            pag_ref[rows, q:2 * q] = pg.astype(BF16)
            zu, gu = _gelu(pu)
            zv, gv = _gelu(pv)
            vhat, rstd_v = _ln_stats(zv)
            vnb = (vhat * vq_ref[VQ_SGU_G:VQ_SGU_G + 1, :] + vq_ref[VQ_SGU_B:VQ_SGU_B + 1, :]).astype(BF16)
            hb_ref[HALO + b * CONV_BLOCK:HALO + (b + 1) * CONV_BLOCK, :] = pa * _sigmoid(pg)
            yield
            mixed = _mix(wst_ref, vnb, nc, n_pairs) + jnp.concatenate([bmat_ref[...]] * nc, axis=0)
            spectrum = _dot(fwd_ref[...], _split(hb_ref[b * CONV_BLOCK:b * CONV_BLOCK + DFT_N, :]))
            yield
            y_ref[rows, 0:q] = (zu * mixed).astype(BF16)
            zu_ref[rows, :] = zu
            mg_ref[rows, :] = mixed * gu
            vhat_ref[rows, :] = vhat
            gv_ref[rows, :] = gv
            vnb_ref[rows, :] = vnb
            hf_ref[b * 2 * DFT_F:(b + 1) * 2 * DFT_F, :] = spectrum
            product = _split(_cmul(gf_ref[...], spectrum))
            yield
            yc = _dot(inv_ref[...], product) + vq_ref[VQ_CONV_B:VQ_CONV_B + 1, :]
            yield
            yhat, rstd_c = _ln_stats(yc)
            yhat_ref[rows, :] = yhat
            yn = yhat * vq_ref[VQ_CLN_G:VQ_CLN_G + 1, :] + vq_ref[VQ_CLN_B:VQ_CLN_B + 1, :]
            y_ref[rows, q:2 * q] = (yn * _sigmoid(yn)).astype(BF16)
            yield
            r1 = alpha * xv + _dot(y_ref[rows, :], wo_v[...])
            yield
            xhat, rstd1 = _ln_stats(r1)
            xh_ref[rows, :] = xhat
            col = lax.broadcasted_iota(jnp.int32, (CONV_BLOCK, RS_COLS), 1)
            rs_ref[rows, :] = jnp.where(col == RS_LN1, rstd1, jnp.where(col == RS_SGU, rstd_v, jnp.where(col == RS_CONV, rstd_c, 0.0)))

        _interleave([sub_tile(b) for b in range(tm // CONV_BLOCK)])
        hb_ref[0:HALO, :] = hb_ref[tm:tm + HALO, :]

        @pl.when(step == n - 1)
        def _():
            _Gather(gathered, send_sems, recv_sems).finish()

    row = lambda w: pl.BlockSpec((tm, w), lambda i: (i, 0))
    widths = [(sum(w), dt) for w, dt in zip(_saved_widths(d, q), (F32, BF16))]
    small_ins = [wst, bmat, cwf, tabs["fwd"], tabs["taps"], tabs["inv_out"], vq, vd]
    return pl.pallas_call(
        body, name="fwd_mix", grid=(n,),
        in_specs=[row(d), ANY, ANY] + [_full(a.shape) for a in small_ins] + [ANY] * 3,
        out_specs=[row(w) for w, _ in widths] + [pl.BlockSpec((tm // CONV_BLOCK * 2 * DFT_F, q), lambda i: (i, 0))] + [ANY] * 3,
        out_shape=[jax.ShapeDtypeStruct((t, w), dt) for w, dt in widths] + [jax.ShapeDtypeStruct((t // CONV_BLOCK * 2 * DFT_F, q), F32)]
        + [jax.ShapeDtypeStruct(b.shape, b.dtype) for b in mlp_w],
        scratch_shapes=[pltpu.VMEM(wi.shape, BF16), pltpu.VMEM(wo.shape, BF16), pltpu.VMEM((HALO + tm, q), F32),
                        pltpu.VMEM((2 * DFT_F, q), F32)] + _gather_sems(3) + [pltpu.SemaphoreType.DMA((2,))],
        input_output_aliases={n_in + a: n_saved + a for a in range(3)},
        compiler_params=_cparams(),
    )(x, wi, wo, *small_ins, *mlp_w)


MLP_SLABS = 4


def _hidden_slabs(f):
    assert f % MXU_N == 0
    tiles = f // MXU_N
    sizes = [(tiles // MLP_SLABS + (1 if j < tiles % MLP_SLABS else 0)) * MXU_N for j in range(MLP_SLABS)]
    return [(sum(sizes[:j]), sz) for j, sz in enumerate(sizes) if sz]


def _fwd_mlp(saved_f32, tgt, wg, wu, wd, vd, alpha, slabs, tm):
    t, d = tgt.shape
    n = t // tm
    ns = len(slabs)
    half = tm // 2 if tm % 32 == 0 else tm

    def body(xh_ref, tgt_ref, wg_hbm, wu_hbm, wd_hbm, vd_ref, *rest):
        gp_refs = [r.at[pl.ds(0, tm), pl.ds(0, sz)] for r, (_, sz) in zip(rest[:ns], slabs)]
        up_refs = [r.at[pl.ds(0, tm), pl.ds(sz, sz)] for r, (_, sz) in zip(rest[:ns], slabs)]
        x1b_ref, dr2_ref, loss_ref, dg2_ref, db2_ref, wg_v, wu_v, wd_v, copy_sems = rest[ns:]

        @pl.when(pl.program_id(0) == 0)
        def _():
            loads = _start_copies(copy_sems, [(wg_hbm, wg_v), (wu_hbm, wu_v), (wd_hbm, wd_v)])
            loss_ref[...] = jnp.zeros_like(loss_ref)
            dg2_ref[...] = jnp.zeros_like(dg2_ref)
            db2_ref[...] = jnp.zeros_like(db2_ref)
            for cp in loads:
                cp.wait()

        g2 = vd_ref[VD_LN2_G:VD_LN2_G + 1, :]

        for r0 in range(0, tm, half):
            rows = slice(r0, r0 + half)
            x1 = xh_ref[rows, :] * vd_ref[VD_LN1_G:VD_LN1_G + 1, :] + vd_ref[VD_LN1_B:VD_LN1_B + 1, :]
            x1b = x1.astype(BF16)
            x1b_ref[rows, :] = x1b
            acc = alpha * x1
            for (off, sz), gp_ref, up_ref in zip(slabs, gp_refs, up_refs):
                gp = _dot_nt(x1b, wg_v[off:off + sz, :])
                up = _dot_nt(x1b, wu_v[off:off + sz, :])
                gp_ref[rows, :] = gp.astype(BF16)
                up_ref[rows, :] = up.astype(BF16)
                acc = acc + _dot((gp * _sigmoid(gp) * up).astype(BF16), wd_v[off:off + sz, :])
            xh2, rstd2 = _ln_stats(acc)
            err = xh2 * g2 + vd_ref[VD_LN2_B:VD_LN2_B + 1, :] - tgt_ref[rows, :]
            loss_ref[...] += _colsum(jnp.sum(err * err, axis=1, keepdims=True)) * (0.5 / d)
            dy = err * (1.0 / d)
            dg2_ref[...] += _colsum(dy * xh2)
            db2_ref[...] += _colsum(dy)
            dr2_ref[rows, :] = _ln_bwd(dy * g2, xh2, rstd2)

    row = lambda w: pl.BlockSpec((tm, w), lambda i: (i, 0))
    act = [2 * sz for _, sz in slabs]
    return pl.pallas_call(
        body, name="fwd_mlp", grid=(n,),
        in_specs=[row(d), row(d), ANY, ANY, ANY, _full(vd.shape)],
        out_specs=[row(sz) for sz in act] + [row(d), row(d), _full((8, LANES)), _full((1, d)), _full((1, d))],
        out_shape=[jax.ShapeDtypeStruct((t, sz), BF16) for sz in act]
        + [jax.ShapeDtypeStruct((t, d), BF16), jax.ShapeDtypeStruct((t, d), F32),
           jax.ShapeDtypeStruct((8, LANES), F32), jax.ShapeDtypeStruct((1, d), F32), jax.ShapeDtypeStruct((1, d), F32)],
        scratch_shapes=[pltpu.VMEM(wg.shape, BF16), pltpu.VMEM(wu.shape, BF16), pltpu.VMEM(wd.shape, BF16), pltpu.SemaphoreType.DMA((3,))],
        compiler_params=_cparams(),
    )(saved_f32, tgt, wg, wu, wd, vd)


def _bwd_mlp_slab(j, slab, dr2, prev, x1b, gate_up, wg, wu, wd, alpha, tm):
    t, d = dr2.shape
    off, sz = slab
    n = t // tm
    first = prev is None

    def body(*refs):
        if first:
            dr_ref, x1b_ref, gu_ref, wg_hbm, wu_hbm, wd_hbm = refs[:6]
        else:
            dr_ref, dxp_ref, x1b_ref, gu_ref, wg_hbm, wu_hbm, wd_hbm = refs[:7]
        dx_ref, dwg_hbm, dwu_hbm, dwd_hbm, dwg16_hbm, dwu16_hbm, dwd16_hbm, ag, au, ad, wg_v, wu_v, wd_v, copy_sems = refs[-14:]

        @pl.when(pl.program_id(0) == 0)
        def _():
            loads = _start_copies(copy_sems, [(src.at[pl.ds(off, sz)], dst) for src, dst in ((wg_hbm, wg_v), (wu_hbm, wu_v), (wd_hbm, wd_v))])
            ag[...] = jnp.zeros_like(ag)
            au[...] = jnp.zeros_like(au)
            ad[...] = jnp.zeros_like(ad)
            for cp in loads:
                cp.wait()

        dr = dr_ref[...]
        drb = dr.astype(BF16)
        x1b = x1b_ref[...]
        gpv = gu_ref[:, 0:sz].astype(F32)
        upv = gu_ref[:, sz:2 * sz].astype(F32)
        dh = _dot_nt(drb, wd_v[...])
        sg = _sigmoid(gpv)
        silu = gpv * sg
        ad[...] += _dot_tn((silu * upv).astype(BF16), drb)
        dgp = (dh * upv * (sg * (1.0 + gpv * (1.0 - sg)))).astype(BF16)
        dup = (dh * silu).astype(BF16)
        ag[...] += _dot_tn(dgp, x1b)
        au[...] += _dot_tn(dup, x1b)
        base = alpha * dr if first else dxp_ref[...]
        dx_ref[...] = base + _dot(dgp, wg_v[...]) + _dot(dup, wu_v[...])

        @pl.when(pl.program_id(0) == n - 1)
        def _():
            rows = pl.ds(off, sz)
            stores = _start_copies(copy_sems, [(ag, dwg_hbm.at[rows]), (au, dwu_hbm.at[rows]), (ad, dwd_hbm.at[rows])])
            for acc, stage in ((ag, wg_v), (au, wu_v), (ad, wd_v)):
                stage[...] = acc[...].astype(BF16)
            stores += _start_copies(copy_sems, [(wg_v, dwg16_hbm.at[rows]), (wu_v, dwu16_hbm.at[rows]), (wd_v, dwd16_hbm.at[rows])], first=3)
            for cp in stores:
                cp.wait()


    row = lambda w: pl.BlockSpec((tm, w), lambda i: (i, 0))
    ins = [dr2] + ([] if first else [prev[0]]) + [x1b, gate_up, wg, wu, wd] + ([] if first else list(prev[1:]))
    in_specs = [row(d)] + ([] if first else [row(d)]) + [row(d), row(2 * sz), ANY, ANY, ANY] + ([] if first else [ANY] * 6)
    return pl.pallas_call(
        body, name=f"bwd_mlp_{j}", grid=(n,),
        in_specs=in_specs,
        out_specs=[row(d)] + [ANY] * 6,
        out_shape=[jax.ShapeDtypeStruct((t, d), F32)] + [jax.ShapeDtypeStruct(wg.shape, F32)] * 3 + [jax.ShapeDtypeStruct(wg.shape, BF16)] * 3,
        scratch_shapes=[pltpu.VMEM((sz, d), F32)] * 3 + [pltpu.VMEM((sz, d), BF16)] * 3 + [pltpu.SemaphoreType.DMA((6,))],
        input_output_aliases={} if first else {7 + a: 1 + a for a in range(6)},
        compiler_params=_cparams(),
    )(*ins)


SMALL_VD = CHUNK
SMALL_CW = CHUNK + 8
SMALL_VQ = CHUNK + 8
SMALL_LOSS = CHUNK + 16
SMALL_BS = CHUNK + 24


def _small_rows(kwp):
    return -(-(SMALL_CW + max(kwp, 24 + SUBLANES)) // 16) * 16


def _bwd_mix(dx1, saved, wi, wo, wstt, cwf, tabs, vq, vd, mlp_small, token, alpha, tm):
    saved_f32, saved_bf16, hf_s = saved
    t, d = dx1.shape
    q = wi.shape[2]
    nc, n_pairs = CONV_BLOCK // CHUNK, q // LANES
    n = t // tm
    nb = tm // CONV_BLOCK
    assert tm % CONV_BLOCK == 0

    def body(dx1_ref, f32_ref, bf16_ref, hf_ref,
             wi_hbm, wo_hbm, wstt_ref, cwf_ref, fwd_ref, fwd_halo_ref, shift_ref, taps_ref, inv_ref, inv_taps_ref, vq_ref, vd_ref,
             loss_ref, dg2_ref, db2_ref, token_ref,
             gx_ref, dwi_hbm, dwo_hbm, small_hbm,
             wi_v, wo_v, awi, awo, dyb_ref, later_ref, dbm_ref, gf_ref, dgf_ref, small_ref, copy_sems):
        xh_ref, zu_ref, mg_ref, vhat_ref, gv_ref, yhat_ref, rs_ref, xb_ref, pag_ref, y_ref, vnb_ref = _saved_views(f32_ref, bf16_ref, d, q)
        i = pl.program_id(0)

        @pl.when(i == 0)
        def _():
            loads = _start_copies(copy_sems, [(wi_hbm, wi_v), (wo_hbm, wo_v)])
            for r in (awi, awo, small_ref, dbm_ref, dgf_ref, dyb_ref, later_ref):
                r[...] = jnp.zeros_like(r)
            gf_ref[...] = _dot(taps_ref[...], _split(cwf_ref[...], True))
            for cp in loads:
                cp.wait()

        dr1b_parts, dproj_parts = [None] * nb, [None] * nb

        def sub_tile(b):
            rows = slice(b * CONV_BLOCK, (b + 1) * CONV_BLOCK)
            dx1v = dx1_ref[rows, :]
            xh = xh_ref[rows, :]
            rsv = rs_ref[rows, :]
            small_ref[SMALL_VD + VD_LN1_G:SMALL_VD + VD_LN1_G + 1, :] += _colsum(dx1v * xh)
            small_ref[SMALL_VD + VD_LN1_B:SMALL_VD + VD_LN1_B + 1, :] += _colsum(dx1v)
            dr1 = _ln_bwd(dx1v * vd_ref[VD_LN1_G:VD_LN1_G + 1, :], xh, rsv[:, RS_LN1:RS_LN1 + 1])
            dr1b = dr1.astype(BF16)
            yield
            dy = _dot_nt(dr1b, wo_v[...])
            yield
            vhat = vhat_ref[rows, :]
            sgu_g = vq_ref[VQ_SGU_G:VQ_SGU_G + 1, :]
            doa = dy[:, 0:q]
            dm = doa * zu_ref[rows, :]
            dpu = (doa * mg_ref[rows, :]).astype(BF16)
            acc = dm[0:CHUNK]
            for c in range(1, nc):
                acc = acc + dm[c * CHUNK:(c + 1) * CHUNK]
            dbm_ref[...] += acc
            pa = pag_ref[rows, 0:q].astype(F32)
            sg = _sigmoid(pag_ref[rows, q:2 * q].astype(F32))
            yhat = yhat_ref[rows, :]
            cln_g = vq_ref[VQ_CLN_G:VQ_CLN_G + 1, :]
            yn = yhat * cln_g + vq_ref[VQ_CLN_B:VQ_CLN_B + 1, :]
            sy = _sigmoid(yn)
            dyn = dy[:, q:2 * q] * (sy * (1.0 + yn * (1.0 - sy)))
            small_ref[SMALL_VQ + VQ_CLN_G:SMALL_VQ + VQ_CLN_G + 1, q:2 * q] += _colsum(dyn * yhat)
            small_ref[SMALL_VQ + VQ_CLN_B:SMALL_VQ + VQ_CLN_B + 1, q:2 * q] += _colsum(dyn)
            dyc = _ln_bwd(dyn * cln_g, yhat, rsv[:, RS_CONV:RS_CONV + 1])
            small_ref[SMALL_VQ + VQ_CONV_B:SMALL_VQ + VQ_CONV_B + 1, q:2 * q] += _colsum(dyc)
            dyb_ref[b, 0:CONV_BLOCK, :] = dyc
            yield
            wgrads = _mix_wgrad(dm, vnb_ref[rows, :], nc, n_pairs)
            dvn = _mix(wstt_ref, dm, nc, n_pairs)
            own = _dot(fwd_ref[...], _split(dyb_ref[b]))
            with_later = own + _dot(fwd_halo_ref[...], _split(later_ref[...]))
            later_ref[...] = dyb_ref[b, 0:HALO, :]
            yield
            for p, g in enumerate(wgrads):
                for half in range(2):
                    small_ref[0:CHUNK, (2 * p + half) * CHUNK:(2 * p + half + 1) * CHUNK] += g[half * CHUNK:(half + 1) * CHUNK]
            small_ref[SMALL_VQ + VQ_SGU_G:SMALL_VQ + VQ_SGU_G + 1, q:2 * q] += _colsum(dvn * vhat)
            small_ref[SMALL_VQ + VQ_SGU_B:SMALL_VQ + VQ_SGU_B + 1, q:2 * q] += _colsum(dvn)
            dpv = (_ln_bwd(dvn * sgu_g, vhat, rsv[:, RS_SGU:RS_SGU + 1]) * gv_ref[rows, :]).astype(BF16)
            dgf_ref[...] += _cmul(_cmul(own, shift_ref[...]), hf_ref[b * 2 * DFT_F:(b + 1) * 2 * DFT_F, :], conj_b=True)
            product = _split(_cmul(with_later, gf_ref[...], conj_b=True))
            yield
            dh = _dot(inv_ref[...], product)
            yield
            da = (dh * sg).astype(BF16)
            dg = (dh * pa * (sg * (1.0 - sg))).astype(BF16)
            yield
            gx = alpha * dr1
            for dpj, wj in zip((dpu, dpv, da, dg), range(4)):
                gx = gx + _dot_nt(dpj, wi_v[wj])
            gx_ref[rows, :] = gx
            dr1b_parts[b], dproj_parts[b] = dr1b, (dpu, dpv, da, dg)

        _interleave([sub_tile(b) for b in reversed(range(nb))])

        awo[...] += _dot_tn(y_ref[...], jnp.concatenate(dr1b_parts, axis=0))
        xb = xb_ref[...]
        for j in range(4):
            awi[j] += _dot_tn(xb, jnp.concatenate([part[j] for part in dproj_parts], axis=0))

        @pl.when(i == n - 1)
        def _():
            stores = _start_copies(copy_sems, [(awi, dwi_hbm), (awo, dwo_hbm)])
            lane = lax.broadcasted_iota(jnp.int32, (CHUNK, LANES), 1)
            low = lane < HEAD_DIM
            dbs = jnp.zeros((CHUNK, LANES), F32)
            for p in range(n_pairs):
                grp = dbm_ref[:, p * LANES:(p + 1) * LANES]
                dbs = jnp.where(lane == 2 * p, jnp.sum(jnp.where(low, grp, 0.0), axis=1, keepdims=True), dbs)
                dbs = jnp.where(lane == 2 * p + 1, jnp.sum(jnp.where(low, 0.0, grp), axis=1, keepdims=True), dbs)
            tril = lax.broadcasted_iota(jnp.int32, (CHUNK, CHUNK), 0) >= lax.broadcasted_iota(jnp.int32, (CHUNK, CHUNK), 1)
            for h in range(2 * n_pairs):
                block = small_ref[0:CHUNK, h * CHUNK:(h + 1) * CHUNK]
                small_ref[0:CHUNK, h * CHUNK:(h + 1) * CHUNK] = jnp.where(tril, block, 0.0)
            small_ref[SMALL_VD + VD_LN2_G:SMALL_VD + VD_LN2_G + 1, :] = dg2_ref[...]
            small_ref[SMALL_VD + VD_LN2_B:SMALL_VD + VD_LN2_B + 1, :] = db2_ref[...]
            small_ref[SMALL_CW:SMALL_CW + kwp, 0:q] = _dot(inv_taps_ref[...], _split(dgf_ref[...], True))
            small_ref[SMALL_LOSS:SMALL_LOSS + SUBLANES, q:q + LANES] = loss_ref[...]
            small_ref[SMALL_BS:SMALL_BS + SUBLANES, q:q + LANES] = jnp.transpose(dbs)[0:SUBLANES]
            stores += _start_copies(copy_sems, [(small_ref, small_hbm)], first=2)
            for cp in stores:
                cp.wait()

    rev = lambda w: pl.BlockSpec((tm, w), lambda i: (n - 1 - i, 0))
    kwp = cwf.shape[0]
    small = jax.ShapeDtypeStruct((_small_rows(kwp), 2 * q), F32)
    small_ins = [wstt, cwf, tabs["fwd"], tabs["fwd_halo"], tabs["shift"], tabs["taps"], tabs["inv_in"], tabs["inv_taps"], vq, vd,
                 *mlp_small]
    return pl.pallas_call(
        body, name="bwd_mix", grid=(n,),
        in_specs=[rev(d), rev(saved_f32.shape[1]), rev(saved_bf16.shape[1]),
                  pl.BlockSpec((nb * 2 * DFT_F, q), lambda i: (n - 1 - i, 0)), ANY, ANY] + [_full(a.shape) for a in small_ins] + [ANY],
        out_specs=[rev(d), ANY, ANY, ANY],
        out_shape=[jax.ShapeDtypeStruct((t, d), F32), jax.ShapeDtypeStruct(wi.shape, F32), jax.ShapeDtypeStruct(wo.shape, F32), small],
        scratch_shapes=[pltpu.VMEM(wi.shape, BF16), pltpu.VMEM(wo.shape, BF16), pltpu.VMEM(wi.shape, F32), pltpu.VMEM(wo.shape, F32),
                        pltpu.VMEM((nb, DFT_N, q), F32), pltpu.VMEM((HALO, q), F32),
                        pltpu.VMEM((CHUNK, q), F32), pltpu.VMEM((2 * DFT_F, q), F32), pltpu.VMEM((2 * DFT_F, q), F32),
                        pltpu.VMEM(small.shape, F32), pltpu.SemaphoreType.DMA((3,))],
        compiler_params=_cparams(),
    )(dx1, saved_f32, saved_bf16, hf_s, wi, wo, *small_ins, token)


def _prep(me_arr, w_in, w_out, w_gate_t, w_up_t, w_down, conv_w, kwp):
    kw, cshard = conv_w.shape

    def body(me_ref, wi_ref, wo_ref, wg_ref, wu_ref, wd_ref, cw_ref, oi, oo, og, ou, od, oc):
        for src, dst in ((wi_ref, oi), (wo_ref, oo), (wg_ref, og), (wu_ref, ou), (wd_ref, od)):
            dst[...] = src[...].astype(BF16)
        oc[0:kw, :] = cw_ref[...]
        oc[kw:kwp, :] = jnp.zeros((kwp - kw, cshard), F32)

    ins = (w_in, w_out, w_gate_t, w_up_t, w_down, conv_w)
    outs = [jax.ShapeDtypeStruct((4,) + a.shape, BF16) for a in ins[:5]] + [jax.ShapeDtypeStruct((4, kwp, cshard), F32)]
    grid_spec = pltpu.PrefetchScalarGridSpec(
        num_scalar_prefetch=1, grid=(1,),
        in_specs=[pl.BlockSpec(a.shape, lambda i, me: (0, 0)) for a in ins],
        out_specs=[pl.BlockSpec((None,) + o.shape[1:], lambda i, me: (me[0], 0, 0)) for o in outs])
    return pl.pallas_call(body, name="wprep", grid_spec=grid_spec, out_shape=outs, compiler_params=_cparams())(me_arr, *ins)


def _coords():
    return tuple(lax.axis_index(a) for a in MESH_AXES)


def _other_chips(x, y):
    return [(1 - x, y), (x, 1 - y), (1 - x, 1 - y)]


def _remote(src, dst, send_sem, recv_sem, to):
    return pltpu.make_async_remote_copy(src_ref=src, dst_ref=dst, send_sem=send_sem, recv_sem=recv_sem,
                                        device_id=to, device_id_type=MESH_ID)


def _hbm_call(body, name, ins, out_shape, scratch_shapes, aliases=None):
    return pl.pallas_call(
        body, name=name, in_specs=[ANY] * len(ins), out_specs=[ANY] * len(out_shape), out_shape=out_shape,
        scratch_shapes=scratch_shapes, input_output_aliases=aliases or {},
    )(*ins)


class _Gather:
    def __init__(self, bufs, send_sems, recv_sems):
        self.bufs, self.send_sems, self.recv_sems = bufs, send_sems, recv_sems
        self.x, self.y, self.c = _coords()

    def _copies(self, stage):
        x, y, c = self.x, self.y, self.c
        for a, buf in enumerate(self.bufs):
            hr = buf.shape[1] // 2
            for j, chip in enumerate(_other_chips(x, y)):
                if stage == "ici_out":
                    ref, k, to = buf.at[2 * x + y, pl.ds(c * hr, hr)], j, (*chip, c)
                elif stage == "ici_in":
                    ref, k, to = buf.at[2 * chip[0] + chip[1], pl.ds(c * hr, hr)], j, (*chip, c)
                elif stage == "d2d_out":
                    ref, k, to = buf.at[2 * chip[0] + chip[1], pl.ds(c * hr, hr)], 3 + j, (x, y, 1 - c)
                else:
                    ref, k, to = buf.at[2 * chip[0] + chip[1], pl.ds((1 - c) * hr, hr)], 3 + j, (x, y, 1 - c)
                yield _remote(ref, ref, self.send_sems.at[a, k], self.recv_sems.at[a, k], to)

    def start(self):
        for cp in self._copies("ici_out"):
            cp.start()

    def forward(self):
        for landed, onward in zip(self._copies("ici_in"), self._copies("d2d_out")):
            landed.wait_recv()
            onward.start()

    def finish(self):
        for cp in self._copies("d2d_in"):
            cp.wait_recv()
        for stage in ("ici_out", "d2d_out"):
            for cp in self._copies(stage):
                cp.wait_send()


def _gather_sems(n):
    return [pltpu.SemaphoreType.DMA((n, 6)), pltpu.SemaphoreType.DMA((n, 6))]


def _gather_shards(bufs):
    n = len(bufs)

    def body(*refs):
        g = _Gather(refs[n:2 * n], *refs[2 * n:])
        g.start()
        g.forward()
        g.finish()

    return _hbm_call(body, "gather_shards", bufs, [jax.ShapeDtypeStruct(s.shape, s.dtype) for s in bufs],
                     _gather_sems(n), aliases={a: a for a in range(n)})


def _pair_swap(name, arrs):
    n = len(arrs)

    def body(*refs):
        src, land = refs[:n], refs[n:2 * n]
        send_sems, recv_sems = refs[2 * n:]
        x, y, c = _coords()
        copies = []
        for a in range(n):
            s = src[a].at[pl.ds(0, arrs[a].shape[0]), 1 - c] if arrs[a].ndim == 4 else src[a].at[1 - c]
            copies.append(_remote(s, land[a], send_sems.at[a], recv_sems.at[a], (x, y, 1 - c)))
            copies[-1].start()
        for cp in copies:
            cp.wait()

    outs = [jax.ShapeDtypeStruct(s.shape[:-3] + s.shape[-2:], s.dtype) for s in arrs]
    return _hbm_call(body, name, arrs, outs, [pltpu.SemaphoreType.DMA((n,)), pltpu.SemaphoreType.DMA((n,))])


class _Exchange:
    def __init__(self, src, dst, send_sems, recv_sems):
        self.src, self.dst, self.send_sems, self.recv_sems = src, dst, send_sems, recv_sems
        self.x, self.y, self.c = _coords()

    def _copies(self, incoming):
        x, y, c = self.x, self.y, self.c
        for a, (s, d) in enumerate(zip(self.src, self.dst)):
            for j, chip in enumerate(_other_chips(x, y)):
                slot = 2 * chip[0] + chip[1]
                if incoming:
                    out, into = d.at[slot], d.at[slot]
                else:
                    out, into = (s.at[slot] if len(s.shape) == 3 else s), d.at[2 * x + y]
                yield _remote(out, into, self.send_sems.at[a, j], self.recv_sems.at[a, j], (*chip, c))

    def start(self):
        for cp in self._copies(False):
            cp.start()

    def finish(self):
        for cp in self._copies(True):
            cp.wait_recv()
        for cp in self._copies(False):
            cp.wait_send()


def _exchange_shapes(arrs):
    return [jax.ShapeDtypeStruct((4,) + s.shape[-2:], s.dtype) for s in arrs]


class _FlatSems:
    def __init__(self, ref):
        self.ref = ref

    @property
    def at(self):
        return self

    def __getitem__(self, idx):
        return self.ref.at[3 * idx[0] + idx[1]]


HBM = pl.BlockSpec(memory_space=pltpu.HBM)
SEM = pl.BlockSpec(memory_space=pltpu.SEMAPHORE)
DATAFLOW = pltpu.SideEffectType.DATAFLOW_SIDE_EFFECTING


def _exchange_start(name, arrs):
    n = len(arrs)
    lands = _exchange_shapes(arrs)

    def body(*refs):
        src, land = refs[:n], refs[n:2 * n]
        send_sems, recv_sems = refs[2 * n:2 * n + 2]
        token = refs[-1]
        _Exchange(src, land, _FlatSems(send_sems), _FlatSems(recv_sems)).start()
        token[...] = jnp.zeros_like(token)

    hbm = lambda a: pltpu.with_memory_space_constraint(a, pltpu.HBM)
    outs = pl.pallas_call(
        body, name=name,
        out_shape=(pltpu.SemaphoreType.DMA((3 * n,)), pltpu.SemaphoreType.DMA((3 * n,)),
                   *[pltpu.HBM(a.shape, a.dtype) for a in arrs], *[pltpu.HBM(s.shape, s.dtype) for s in lands],
                   jax.ShapeDtypeStruct((SUBLANES, LANES), F32)),
        in_specs=[HBM] * (2 * n), out_specs=(SEM, SEM, *[HBM] * (2 * n), pl.BlockSpec(memory_space=pltpu.VMEM)),
        input_output_aliases={a: 2 + a for a in range(2 * n)},
        compiler_params=pltpu.CompilerParams(has_side_effects=DATAFLOW),
    )(*[hbm(a) for a in arrs], *[hbm(lax.empty(s.shape, s.dtype)) for s in lands])
    return outs[:-1], outs[-1]


def _exchange_wait(name, started, after):
    send_sems, recv_sems, *bufs = started
    n = len(bufs) // 2

    def body(*refs):
        src, land = refs[:n], refs[n:2 * n]
        send_sems, recv_sems = refs[2 * n:2 * n + 2]
        _Exchange(src, land, _FlatSems(send_sems), _FlatSems(recv_sems)).finish()

    outs = pl.pallas_call(
        body, name=name,
        out_shape=tuple(pltpu.HBM(b.shape, b.dtype) for b in bufs),
        in_specs=[HBM] * (2 * n) + [SEM, SEM] + [ANY] * len(after), out_specs=tuple([HBM] * (2 * n)),
        input_output_aliases={a: a for a in range(2 * n)},
        compiler_params=pltpu.CompilerParams(has_side_effects=DATAFLOW),
    )(*bufs, send_sems, recv_sems, *after)
    return list(outs[:n]), list(outs[n:])


def _pair_gather(name, halves):
    n = len(halves)

    def body(*refs):
        src, dst = refs[:n], refs[n:2 * n]
        send_sems, recv_sems = refs[2 * n:]
        x, y, c = _coords()
        copies = [_remote(src[a], dst[a], send_sems.at[a], recv_sems.at[a], (x, y, 1 - c)) for a in range(n)]
        for cp in copies:
            cp.start()
        for cp in copies:
            cp.wait()

    outs = [jax.ShapeDtypeStruct(s.shape, s.dtype) for s in halves]
    return _hbm_call(body, name, halves, outs, [pltpu.SemaphoreType.DMA((n,)), pltpu.SemaphoreType.DMA((n,))])


def _pair_sum(a, g, land, c_arr, out_dtype):
    nq, _, hr, cc = g.shape

    def body(c_ref, g_ref, l_ref, o_ref):
        o_ref[...] = (g_ref[...] + l_ref[...].astype(F32)).astype(out_dtype)

    spec = pl.BlockSpec((None, hr, cc), lambda qi, cr: (qi, 0, 0))
    grid_spec = pltpu.PrefetchScalarGridSpec(
        num_scalar_prefetch=1, grid=(nq,),
        in_specs=[pl.BlockSpec((None, None, hr, cc), lambda qi, cr: (qi, cr[0], 0, 0)), spec], out_specs=spec)
    return pl.pallas_call(body, name=f"pair_sum_{a}", grid_spec=grid_spec, out_shape=jax.ShapeDtypeStruct((nq, hr, cc), out_dtype),
                          compiler_params=_cparams())(c_arr, g, land)


def _chip_sum(a, parts, own, me_arr, after):
    _, hr, cc = parts.shape

    def body(me_ref, p_ref, own_ref, after_ref, o_ref):
        for mine in range(4):
            @pl.when(me_ref[0] == mine)
            def _():
                term = lambda j: (own_ref if j == mine else p_ref.at[j])[...].astype(F32)
                o_ref[...] = ((term(0) + term(1)) + term(2)) + term(3)

    own_spec = (pl.BlockSpec((None, hr, cc), lambda i, me: (me[0], 0, 0)) if own.ndim == 3
                else pl.BlockSpec((hr, cc), lambda i, me: (0, 0)))
    grid_spec = pltpu.PrefetchScalarGridSpec(
        num_scalar_prefetch=1, grid=(1,),
        in_specs=[pl.BlockSpec((4, hr, cc), lambda i, me: (0, 0, 0)), own_spec, ANY],
        out_specs=pl.BlockSpec((hr, cc), lambda i, me: (0, 0)))
    return pl.pallas_call(body, name=f"chip_sum_{a}", grid_spec=grid_spec, out_shape=jax.ShapeDtypeStruct((hr, cc), F32),
                          compiler_params=_cparams())(me_arr, parts, own, after)


def _row_block(rows, cols, limit=1 << 20):
    best = 8
    for tr in range(8, rows + 1, 8):
        if rows % tr == 0 and tr * cols * 4 <= limit:
            best = tr
    return best


def _adamw(name, w, g_mine, g_other, m, v, c_arr):
    r, c = w.shape
    hr, cg = g_mine.shape
    tr = hr if r % hr == 0 and hr * cg * 4 <= (3 << 19) else math.gcd(_row_block(hr, cg), r)
    per_half = hr // tr
    bc1 = 1.0 - ADAM_B1 ** ADAM_STEP
    bc2 = 1.0 - ADAM_B2 ** ADAM_STEP

    def body(c_ref, w_ref, gm_ref, go_ref, m_ref, v_ref, go, do, mo, vo):
        gv = jnp.where(pl.program_id(0) // per_half == c_ref[0], gm_ref[:, 0:c], go_ref[:, 0:c])
        mn = ADAM_B1 * m_ref[...] + (1.0 - ADAM_B1) * gv
        vn = ADAM_B2 * v_ref[...] + (1.0 - ADAM_B2) * (gv * gv)
        go[...] = gv
        mo[...] = mn
        vo[...] = vn
        do[...] = -ADAM_LR * ((mn / bc1) / (jnp.sqrt(vn / bc2) + ADAM_EPS) + ADAM_WD * w_ref[...])

    blk = pl.BlockSpec((tr, c), lambda i, cr: (i, 0))
    gblk = pl.BlockSpec((tr, cg), lambda i, cr: (i % per_half, 0))
    grid_spec = pltpu.PrefetchScalarGridSpec(num_scalar_prefetch=1, grid=(r // tr,), in_specs=[blk, gblk, gblk, blk, blk],
                                             out_specs=[blk] * 4)
    return pl.pallas_call(body, name=f"adamw_{name}", grid_spec=grid_spec, out_shape=[jax.ShapeDtypeStruct((r, c), F32)] * 4,
                          compiler_params=_cparams())(c_arr, w, g_mine, g_other, m, v)


SMALL_Q = ("sgu_ln_g", "sgu_ln_b", "conv_b", "conv_ln_g", "conv_ln_b")
SMALL_D = ("ln1_g", "ln1_b", "ln2_g", "ln2_b")


def _adamw_small(g_mine, g_other, c_arr, me_arr, params):
    names = list(SMALL_Q) + list(SMALL_D) + ["w_s", "b_s", "conv_w"]
    hr, width = g_mine.shape
    q = width // 2
    heads = params["w_s"][0].shape[1]
    kw, cshard = params["conv_w"][0].shape[1:]
    bc1 = 1.0 - ADAM_B1 ** ADAM_STEP
    bc2 = 1.0 - ADAM_B2 ** ADAM_STEP

    def update(w, g, m, v):
        mn = ADAM_B1 * m + (1.0 - ADAM_B1) * g
        vn = ADAM_B2 * v + (1.0 - ADAM_B2) * (g * g)
        return g, -ADAM_LR * ((mn / bc1) / (jnp.sqrt(vn / bc2) + ADAM_EPS) + ADAM_WD * w), mn, vn

    def body(c_ref, me_ref, gm_ref, go_ref, *refs):
        ins = {nm: refs[3 * k:3 * k + 3] for k, nm in enumerate(names)}
        outs = {nm: refs[3 * len(names) + 4 * k:3 * len(names) + 4 * k + 4] for k, nm in enumerate(names)}
        loss_ref, cw_ref = refs[-2:]
        first, second = gm_ref[...], go_ref[...]
        low = c_ref[0] == 0
        g_all = jnp.concatenate([jnp.where(low, first, second), jnp.where(low, second, first)], axis=0)

        def apply(nm, g, at):
            w, m, v = (r[at] for r in ins[nm])
            for o, val in zip(outs[nm], update(w, g, m, v)):
                o[at] = val

        for row, nm in enumerate(SMALL_Q):
            apply(nm, g_all[SMALL_VQ + row:SMALL_VQ + row + 1, q:2 * q], ...)
        for row, nm in enumerate(SMALL_D):
            apply(nm, g_all[SMALL_VD + row:SMALL_VD + row + 1, :], ...)
        for h in range(heads):
            apply("w_s", g_all[0:CHUNK, h * CHUNK:(h + 1) * CHUNK], (0, h))
        apply("b_s", g_all[SMALL_BS:SMALL_BS + heads, q:q + LANES], 0)
        cw_ref[...] = jnp.zeros_like(cw_ref)
        for chip in range(4):
            @pl.when(me_ref[0] == chip)
            def _():
                cw_ref[...] = g_all[SMALL_CW:SMALL_CW + cw_ref.shape[0], chip * cshard:(chip + 1) * cshard]
        apply("conv_w", cw_ref[0:kw, :], 0)
        loss_ref[...] = g_all[SMALL_LOSS:SMALL_LOSS + SUBLANES, q:q + LANES]

    arrays = [a for nm in names for a in params[nm]]
    out_shape = [jax.ShapeDtypeStruct(params[nm][0].shape, F32) for nm in names for _ in range(4)] + [jax.ShapeDtypeStruct((SUBLANES, LANES), F32)]
    whole = lambda shape: pl.BlockSpec(shape, lambda i, c, me: (0,) * len(shape))
    grid_spec = pltpu.PrefetchScalarGridSpec(
        num_scalar_prefetch=2, grid=(1,),
        in_specs=[whole(g_mine.shape), whole(g_other.shape)] + [whole(a.shape) for a in arrays],
        out_specs=[whole(s.shape) for s in out_shape],
        scratch_shapes=[pltpu.VMEM((-(-kw // SUBLANES) * SUBLANES, cshard), F32)])
    res = pl.pallas_call(body, name="adamw_small", grid_spec=grid_spec, out_shape=out_shape, compiler_params=_cparams())(
        c_arr, me_arr, g_mine, g_other, *arrays)
    return {nm: list(res[4 * k:4 * k + 4]) for k, nm in enumerate(names)}, res[-1]


def _pad_rows(a, rows):
    return jnp.pad(a, ((0, rows - a.shape[0]), (0, 0)))


def kernel(x, w_in, sgu_ln_g, sgu_ln_b, w_s, b_s, conv_w, conv_b, conv_ln_g, conv_ln_b, w_out, ln1_g, ln1_b, w_gate, w_up, w_down, ln2_g, ln2_b, loss_target, m_w_in, m_sgu_ln_g, m_sgu_ln_b, m_w_s, m_b_s, m_conv_w, m_conv_b, m_conv_ln_g, m_conv_ln_b, m_w_out, m_ln1_g, m_ln1_b, m_w_gate, m_w_up, m_w_down, m_ln2_g, m_ln2_b, v_w_in, v_sgu_ln_g, v_sgu_ln_b, v_w_s, v_b_s, v_conv_w, v_conv_b, v_conv_ln_g, v_conv_ln_b, v_w_out, v_ln1_g, v_ln1_b, v_w_gate, v_w_up, v_w_down, v_ln2_g, v_ln2_b):
    depth, d, q = w_in.shape
    assert depth == 1 and x.shape[0] == 1
    t = x.shape[1]
    heads = w_s.shape[1]
    kw, cshard = conv_w.shape[1], conv_w.shape[2]
    fs = w_gate.shape[2]
    slabs = _hidden_slabs(4 * fs)
    n_pairs = q // LANES
    assert heads * HEAD_DIM == q and q % LANES == 0 and w_s.shape[2] == CHUNK and 4 * cshard == q and kw - 1 <= HALO
    alpha = (2.0 * depth) ** 0.25
    tm = min(512, t)
    assert t % tm == 0 and tm % CHUNK == 0
    x2, tgt = x[0], loss_target[0]
    mx, my, mc = _coords()
    me = 2 * mx + my
    c_arr = jnp.reshape(mc, (1,)).astype(jnp.int32)

    kwp = -(-kw // 16) * 16
    me_arr = jnp.reshape(me, (1,)).astype(jnp.int32)
    wi, wo, wg, wu, wd, cw4 = _prep(me_arr, w_in[0], w_out[0], w_gate[0].T, w_up[0].T, w_down[0], conv_w[0], kwp)
    wi, wo, cw4 = _gather_shards([wi, wo, cw4])
    wo = wo.reshape(d, d)
    cw = jnp.transpose(cw4, (1, 0, 2)).reshape(kwp, q)
    cwf = _pad_rows(cw[:kw][::-1], kwp)
    tabs = {name: jnp.asarray(tab) for name, tab in _dft_tables(kw, kwp, q).items()}

    wm = jnp.where(jnp.tril(jnp.ones((CHUNK, CHUNK), bool)), w_s[0], 0.0)
    wst = wm.reshape(n_pairs, 2 * CHUNK, CHUNK).astype(BF16)
    wstt = jnp.transpose(wm, (0, 2, 1)).reshape(n_pairs, 2 * CHUNK, CHUNK).astype(BF16)
    bmat = jnp.repeat(b_s[0].T, HEAD_DIM, axis=1)
    vq = _pad_rows(jnp.concatenate([sgu_ln_g, sgu_ln_b, conv_b, conv_ln_g, conv_ln_b], axis=0), 8)
    vd = _pad_rows(jnp.concatenate([ln1_g, ln1_b, ln2_g, ln2_b], axis=0), 8)

    *saved, wg, wu, wd = _fwd_mix(x2, wi, wo, wst, bmat, cwf, tabs, vq, vd, [wg, wu, wd], alpha, tm)
    wg, wu, wd = (w.reshape(4 * fs, d) for w in (wg, wu, wd))
    *acts, x1b, dr2, loss_part, dg2, db2 = _fwd_mlp(saved[0], tgt, wg, wu, wd, vd, alpha, slabs, tm)
    mlp_grads = None
    for j, slab in enumerate(slabs):
        mlp_grads = _bwd_mlp_slab(j, slab, dr2, mlp_grads, x1b, acts[j], wg, wu, wd, alpha, tm)
    dx1 = mlp_grads[0]
    mlp_halves = [b.reshape(4, 2, fs // 2, d) for b in mlp_grads[1:]]
    landed = _pair_swap("pair_swap_mlp", mlp_halves[3:])
    mlp_sums = [_pair_sum(f"mlp{a}", h, l, c_arr, BF16) for a, (h, l) in enumerate(zip(mlp_halves[:3], landed))]
    mlp_started, token = _exchange_start("exchange_mlp_start", mlp_sums)
    grad_x, dwi, dwo, small = _bwd_mix(dx1, saved, wi, wo, wstt, cwf, tabs, vq, vd, (loss_part, dg2, db2), token, alpha, tm)
    mlp_sums, mlp_parts = _exchange_wait("exchange_mlp_wait", mlp_started, [dwo])

    big = [dwi, dwo.reshape(4, d // 4, d)]
    halves = [b.reshape(4, 2, b.shape[1] // 2, b.shape[2]) for b in big] + [small.reshape(2, small.shape[0] // 2, small.shape[1])]
    landed = _pair_swap("pair_swap_mix", halves)
    sums = [_pair_sum(f"mix{a}", h, l, c_arr, BF16) for a, (h, l) in enumerate(zip(halves[:-1], landed[:-1]))]
    sums.append(_pair_sum("small", halves[-1][None], landed[-1][None], c_arr, F32)[0])
    mix_started, token = _exchange_start("exchange_mix_start", sums)

    out, raw = {}, {}

    def finish(first, names, parts, sums, after):
        mine = [_chip_sum(first + a, p, s, me_arr, after) for a, (p, s) in enumerate(zip(parts, sums))]
        other = _pair_gather(f"pair_gather_{first}", mine)
        for a, nm in enumerate(names):
            w_, m_, v_ = weights[nm]
            if nm in ("w_gate", "w_up"):
                raw[nm] = _adamw(nm, w_[0].T, mine[a], other[a], m_[0].T, v_[0].T, c_arr)
                out[nm] = [o.T for o in raw[nm]]
            else:
                raw[nm] = out[nm] = _adamw(nm, w_[0], mine[a], other[a], m_[0], v_[0], c_arr)
        return mine[-1], other[-1]

    weights = {"w_in": (w_in, m_w_in, v_w_in), "w_out": (w_out, m_w_out, v_w_out), "w_gate": (w_gate, m_w_gate, v_w_gate),
               "w_up": (w_up, m_w_up, v_w_up), "w_down": (w_down, m_w_down, v_w_down)}
    finish(2, ["w_gate", "w_up", "w_down"], mlp_parts, mlp_sums, token)
    sums, parts = _exchange_wait("exchange_mix_wait", mix_started, [raw[nm][1] for nm in ("w_gate", "w_up", "w_down")])
    small_mine, small_other = finish(5, ["w_in", "w_out"], parts, sums, parts[0])

    small_params = {
        "sgu_ln_g": (sgu_ln_g, m_sgu_ln_g, v_sgu_ln_g), "sgu_ln_b": (sgu_ln_b, m_sgu_ln_b, v_sgu_ln_b),
        "conv_b": (conv_b, m_conv_b, v_conv_b), "conv_ln_g": (conv_ln_g, m_conv_ln_g, v_conv_ln_g),
        "conv_ln_b": (conv_ln_b, m_conv_ln_b, v_conv_ln_b), "ln1_g": (ln1_g, m_ln1_g, v_ln1_g), "ln1_b": (ln1_b, m_ln1_b, v_ln1_b),
        "ln2_g": (ln2_g, m_ln2_g, v_ln2_g), "ln2_b": (ln2_b, m_ln2_b, v_ln2_b), "w_s": (w_s, m_w_s, v_w_s),
        "b_s": (b_s, m_b_s, v_b_s), "conv_w": (conv_w, m_conv_w, v_conv_w)}
    small_out, loss_block = _adamw_small(small_mine, small_other, c_arr, me_arr, small_params)
    loss = loss_block[0, 0]
    names = ["w_in", "sgu_ln_g", "sgu_ln_b", "w_s", "b_s", "conv_w", "conv_b", "conv_ln_g", "conv_ln_b", "w_out",
             "ln1_g", "ln1_b", "w_gate", "w_up", "w_down", "ln2_g", "ln2_b"]
    result = [loss, grad_x[None]]
    for kind in range(4):
        for nm in names:
            result.append(out[nm][kind][None] if nm in out else small_out[nm][kind])
    return tuple(result)
```

```python
import math

import jax
import numpy as np
import jax.numpy as jnp
from jax import lax
from jax.experimental import pallas as pl
from jax.experimental.pallas import tpu as pltpu

F32 = jnp.float32
BF16 = jnp.bfloat16

LN_EPS = 1e-5
HEAD_DIM = 64
CHUNK = 128
HALO = 32
LANES = 128
MXU_N = 256
ADAM_LR, ADAM_B1, ADAM_B2, ADAM_EPS, ADAM_WD, ADAM_STEP = 0.001, 0.9, 0.999, 1e-08, 0.01, 10
VMEM_LIMIT = 63 * 1024 * 1024
MESH_AXES = ("x", "y", "c")
MESH_ID = pl.DeviceIdType.MESH


def _dot(a, b):
    return jnp.dot(a, b, preferred_element_type=F32)


def _dot_nt(a, b):
    return lax.dot_general(a, b, (((1,), (1,)), ((), ())), preferred_element_type=F32)


def _dot_tn(a, b):
    return lax.dot_general(a, b, (((0,), (0,)), ((), ())), preferred_element_type=F32)


def _sigmoid(v):
    return 1.0 / (1.0 + jnp.exp(-v))


def _gelu(v):
    cdf = 0.5 * (1.0 + lax.erf(v * (1.0 / math.sqrt(2.0))))
    pdf = jnp.exp(-0.5 * v * v) * (1.0 / math.sqrt(2.0 * math.pi))
    return v * cdf, cdf + v * pdf


def _ln_stats(v):
    mu = jnp.mean(v, axis=-1, keepdims=True)
    d = v - mu
    rstd = lax.rsqrt(jnp.mean(d * d, axis=-1, keepdims=True) + LN_EPS)
    return d * rstd, rstd


def _ln_bwd(dxhat, xhat, rstd):
    m1 = jnp.mean(dxhat, axis=-1, keepdims=True)
    m2 = jnp.mean(dxhat * xhat, axis=-1, keepdims=True)
    return rstd * (dxhat - m1 - xhat * m2)


def _colsum(v):
    return jnp.sum(v, axis=0, keepdims=True)


def _pair_lanes(v, nc, p):
    return jnp.concatenate([v[c * CHUNK:(c + 1) * CHUNK, p * LANES:(p + 1) * LANES] for c in range(nc)], axis=1)


def _unpair(parts, nc):
    rows = [jnp.concatenate([part[:, c * LANES:(c + 1) * LANES] for part in parts], axis=1) for c in range(nc)]
    return jnp.concatenate(rows, axis=0)


def _low_head(nc):
    lane = lax.broadcasted_iota(jnp.int32, (CHUNK, nc * LANES), 1)
    return (lane & (LANES - 1)) < HEAD_DIM


def _mix(wst_ref, v, nc, n_pairs):
    vb = v.astype(BF16)
    low = _low_head(nc)
    parts = []
    for p in range(n_pairs):
        r = _dot(wst_ref[p], _pair_lanes(vb, nc, p))
        parts.append(jnp.where(low, r[:CHUNK], r[CHUNK:]))
    return _unpair(parts, nc)


def _mix_wgrad(dm, vn, nc, n_pairs):
    low = _low_head(nc)
    vb = vn.astype(BF16)
    out = []
    for p in range(n_pairs):
        a = _pair_lanes(dm, nc, p)
        lhs = jnp.concatenate([jnp.where(low, a, 0.0), jnp.where(low, 0.0, a)], axis=0).astype(BF16)
        out.append(_dot_nt(lhs, _pair_lanes(vb, nc, p)))
    return out


SUBLANES = 8


CONV_BLOCK = 256
DFT_N = CONV_BLOCK + HALO
DFT_F = -(-(DFT_N // 2 + 1) // SUBLANES) * SUBLANES


def _terms(m, exact):
    hi = m.astype(np.float32).astype(BF16)
    lo = (m.astype(np.float32) - hi.astype(np.float32)).astype(BF16)
    return np.concatenate([hi, hi, lo] if exact else [hi], axis=1)


def _split(v, exact=False):
    hi = v.astype(BF16)
    lo = (v - hi.astype(F32)).astype(BF16)
    return jnp.concatenate([hi, lo, hi], axis=0) if exact else hi


def _dft_tables(kw, kwp, q):
    nf = DFT_N // 2 + 1
    ang = 2.0 * np.pi * np.arange(nf)[:, None] * np.arange(DFT_N)[None, :] / DFT_N
    fwd = np.zeros((2 * DFT_F, DFT_N))
    fwd[:nf], fwd[DFT_F:DFT_F + nf] = np.cos(ang), -np.sin(ang)
    weight = np.full((nf, 1), 2.0 / DFT_N)
    weight[0] = weight[-1] = 1.0 / DFT_N
    inv = np.zeros((DFT_N, 2 * DFT_F))
    inv[:, :nf], inv[:, DFT_F:DFT_F + nf] = (np.cos(ang) * weight).T, (-np.sin(ang) * weight).T
    inv_taps = np.zeros((kwp, 2 * DFT_F))
    inv_taps[:kw] = inv[kw - 1::-1][:kw]
    shift = np.zeros((2 * DFT_F, q), np.float32)
    shift[:nf], shift[DFT_F:DFT_F + nf] = np.cos(ang[:, HALO:HALO + 1]), -np.sin(ang[:, HALO:HALO + 1])
    return {"fwd": _terms(fwd, False), "fwd_halo": _terms(fwd[:, CONV_BLOCK:], False), "shift": shift,
            "inv_out": _terms(inv[HALO:HALO + CONV_BLOCK], False), "inv_in": _terms(inv[:CONV_BLOCK], False),
            "taps": _terms(fwd[:, :kwp], True), "inv_taps": _terms(inv_taps, True)}


def _cmul(a, b, conj_b=False):
    ar, ai, br, bi = a[:DFT_F], a[DFT_F:], b[:DFT_F], b[DFT_F:]
    if conj_b:
        return jnp.concatenate([ar * br + ai * bi, ai * br - ar * bi], axis=0)
    return jnp.concatenate([ar * br - ai * bi, ar * bi + ai * br], axis=0)


def _interleave(sub_tiles):
    waiting, live = list(sub_tiles), []
    while waiting or live:
        if waiting:
            live.append(waiting.pop(0))
        for g in list(live):
            try:
                next(g)
            except StopIteration:
                live.remove(g)


def _start_copies(sems, pairs, first=0):
    copies = [pltpu.make_async_copy(src, dst, sems.at[first + k]) for k, (src, dst) in enumerate(pairs)]
    for cp in copies:
        cp.start()
    return copies


def _cparams():
    return pltpu.CompilerParams(dimension_semantics=("arbitrary",), vmem_limit_bytes=VMEM_LIMIT)


def _full(shape):
    return pl.BlockSpec(shape, lambda i: (0,) * len(shape))


ANY = pl.BlockSpec(memory_space=pl.ANY)

VQ_SGU_G, VQ_SGU_B, VQ_CONV_B, VQ_CLN_G, VQ_CLN_B = range(5)
VD_LN1_G, VD_LN1_B, VD_LN2_G, VD_LN2_B = range(4)
RS_LN1, RS_SGU, RS_CONV = range(3)
RS_COLS = LANES


def _saved_widths(d, q):
    f32 = [d, q, q, q, q, q, RS_COLS]
    bf16 = [d, 2 * q, d, q]
    return f32, bf16


def _saved_views(f32_ref, bf16_ref, d, q):
    views = []
    for ref, widths in zip((f32_ref, bf16_ref), _saved_widths(d, q)):
        for k, w in enumerate(widths):
            views.append(ref.at[pl.ds(0, ref.shape[0]), pl.ds(sum(widths[:k]), w)])
    return views
def _fwd_mix(x, wi, wo, wst, bmat, cwf, tabs, vq, vd, mlp_w, alpha, tm):
    t, d = x.shape
    q = wi.shape[2]
    nc, n_pairs = CONV_BLOCK // CHUNK, q // LANES
    n = t // tm
    n_in, n_saved = 11, 3
    assert tm % CONV_BLOCK == 0

    def body(x_ref, wi_hbm, wo_hbm, wst_ref, bmat_ref, cwf_ref, fwd_ref, taps_ref, inv_ref, vq_ref, vd_ref, *rest):
        f32_ref, bf16_ref, hf_ref = rest[3:3 + n_saved]
        xh_ref, zu_ref, mg_ref, vhat_ref, gv_ref, yhat_ref, rs_ref, xb_ref, pag_ref, y_ref, vnb_ref = _saved_views(f32_ref, bf16_ref, d, q)
        gathered = rest[3 + n_saved:6 + n_saved]
        wi_v, wo_v, hb_ref, gf_ref, send_sems, recv_sems, copy_sems = rest[6 + n_saved:]
        step = pl.program_id(0)

        @pl.when(step == 0)
        def _():
            loads = _start_copies(copy_sems, [(wi_hbm, wi_v), (wo_hbm, wo_v)])
            _Gather(gathered, send_sems, recv_sems).start()
            hb_ref[...] = jnp.zeros_like(hb_ref)
            gf_ref[...] = _dot(taps_ref[...], _split(cwf_ref[...], True))
            for cp in loads:
                cp.wait()

        @pl.when(step == (3 * n) // 4)
        def _():
            _Gather(gathered, send_sems, recv_sems).forward()

        def sub_tile(b):
            rows = slice(b * CONV_BLOCK, (b + 1) * CONV_BLOCK)
            xv = x_ref[rows, :]
            xb = xv.astype(BF16)
            xb_ref[rows, :] = xb
            pu, pv, pa, pg = (_dot(xb, wi_v[j]) for j in range(4))
            yield
            pag_ref[rows, 0:q] = pa.astype(BF16)
            pag_ref[rows, q:2 * q] = pg.astype(BF16)
            zu, gu = _gelu(pu)
            zv, gv = _gelu(pv)
            vhat, rstd_v = _ln_stats(zv)
            vnb = (vhat * vq_ref[VQ_SGU_G:VQ_SGU_G + 1, :] + vq_ref[VQ_SGU_B:VQ_SGU_B + 1, :]).astype(BF16)
            hb_ref[HALO + b * CONV_BLOCK:HALO + (b + 1) * CONV_BLOCK, :] = pa * _sigmoid(pg)
            yield
            mixed = _mix(wst_ref, vnb, nc, n_pairs) + jnp.concatenate([bmat_ref[...]] * nc, axis=0)
            spectrum = _dot(fwd_ref[...], _split(hb_ref[b * CONV_BLOCK:b * CONV_BLOCK + DFT_N, :]))
            yield
            y_ref[rows, 0:q] = (zu * mixed).astype(BF16)
            zu_ref[rows, :] = zu
            mg_ref[rows, :] = mixed * gu
            vhat_ref[rows, :] = vhat
            gv_ref[rows, :] = gv
            vnb_ref[rows, :] = vnb
            hf_ref[b * 2 * DFT_F:(b + 1) * 2 * DFT_F, :] = spectrum
            product = _split(_cmul(gf_ref[...], spectrum))
            yield
            yc = _dot(inv_ref[...], product) + vq_ref[VQ_CONV_B:VQ_CONV_B + 1, :]
            yield
            yhat, rstd_c = _ln_stats(yc)
            yhat_ref[rows, :] = yhat
            yn = yhat * vq_ref[VQ_CLN_G:VQ_CLN_G + 1, :] + vq_ref[VQ_CLN_B:VQ_CLN_B + 1, :]
            y_ref[rows, q:2 * q] = (yn * _sigmoid(yn)).astype(BF16)
            yield
            r1 = alpha * xv + _dot(y_ref[rows, :], wo_v[...])
            yield
            xhat, rstd1 = _ln_stats(r1)
            xh_ref[rows, :] = xhat
            col = lax.broadcasted_iota(jnp.int32, (CONV_BLOCK, RS_COLS), 1)
            rs_ref[rows, :] = jnp.where(col == RS_LN1, rstd1, jnp.where(col == RS_SGU, rstd_v, jnp.where(col == RS_CONV, rstd_c, 0.0)))

        _interleave([sub_tile(b) for b in range(tm // CONV_BLOCK)])
        hb_ref[0:HALO, :] = hb_ref[tm:tm + HALO, :]

        @pl.when(step == n - 1)
        def _():
            _Gather(gathered, send_sems, recv_sems).finish()

    row = lambda w: pl.BlockSpec((tm, w), lambda i: (i, 0))
    widths = [(sum(w), dt) for w, dt in zip(_saved_widths(d, q), (F32, BF16))]
    small_ins = [wst, bmat, cwf, tabs["fwd"], tabs["taps"], tabs["inv_out"], vq, vd]
    return pl.pallas_call(
        body, name="fwd_mix", grid=(n,),
        in_specs=[row(d), ANY, ANY] + [_full(a.shape) for a in small_ins] + [ANY] * 3,
        out_specs=[row(w) for w, _ in widths] + [pl.BlockSpec((tm // CONV_BLOCK * 2 * DFT_F, q), lambda i: (i, 0))] + [ANY] * 3,
        out_shape=[jax.ShapeDtypeStruct((t, w), dt) for w, dt in widths] + [jax.ShapeDtypeStruct((t // CONV_BLOCK * 2 * DFT_F, q), F32)]
        + [jax.ShapeDtypeStruct(b.shape, b.dtype) for b in mlp_w],
        scratch_shapes=[pltpu.VMEM(wi.shape, BF16), pltpu.VMEM(wo.shape, BF16), pltpu.VMEM((HALO + tm, q), F32),
                        pltpu.VMEM((2 * DFT_F, q), F32)] + _gather_sems(3) + [pltpu.SemaphoreType.DMA((2,))],
        input_output_aliases={n_in + a: n_saved + a for a in range(3)},
        compiler_params=_cparams(),
    )(x, wi, wo, *small_ins, *mlp_w)


MLP_SLABS = 4


def _hidden_slabs(f):
    assert f % MXU_N == 0
    tiles = f // MXU_N
    sizes = [(tiles // MLP_SLABS + (1 if j < tiles % MLP_SLABS else 0)) * MXU_N for j in range(MLP_SLABS)]
    return [(sum(sizes[:j]), sz) for j, sz in enumerate(sizes) if sz]


def _fwd_mlp(saved_f32, tgt, wg, wu, wd, vd, alpha, slabs, tm):
    t, d = tgt.shape
    n = t // tm
    ns = len(slabs)
    half = tm // 2 if tm % 32 == 0 else tm

    def body(xh_ref, tgt_ref, wg_hbm, wu_hbm, wd_hbm, vd_ref, *rest):
        gp_refs = [r.at[pl.ds(0, tm), pl.ds(0, sz)] for r, (_, sz) in zip(rest[:ns], slabs)]
        up_refs = [r.at[pl.ds(0, tm), pl.ds(sz, sz)] for r, (_, sz) in zip(rest[:ns], slabs)]
        x1b_ref, dr2_ref, loss_ref, dg2_ref, db2_ref, wg_v, wu_v, wd_v, copy_sems = rest[ns:]

        @pl.when(pl.program_id(0) == 0)
        def _():
            loads = _start_copies(copy_sems, [(wg_hbm, wg_v), (wu_hbm, wu_v), (wd_hbm, wd_v)])
            loss_ref[...] = jnp.zeros_like(loss_ref)
            dg2_ref[...] = jnp.zeros_like(dg2_ref)
            db2_ref[...] = jnp.zeros_like(db2_ref)
            for cp in loads:
                cp.wait()

        g2 = vd_ref[VD_LN2_G:VD_LN2_G + 1, :]

        for r0 in range(0, tm, half):
            rows = slice(r0, r0 + half)
            x1 = xh_ref[rows, :] * vd_ref[VD_LN1_G:VD_LN1_G + 1, :] + vd_ref[VD_LN1_B:VD_LN1_B + 1, :]
            x1b = x1.astype(BF16)
            x1b_ref[rows, :] = x1b
            acc = alpha * x1
            for (off, sz), gp_ref, up_ref in zip(slabs, gp_refs, up_refs):
                gp = _dot_nt(x1b, wg_v[off:off + sz, :])
                up = _dot_nt(x1b, wu_v[off:off + sz, :])
                gp_ref[rows, :] = gp.astype(BF16)
                up_ref[rows, :] = up.astype(BF16)
                acc = acc + _dot((gp * _sigmoid(gp) * up).astype(BF16), wd_v[off:off + sz, :])
            xh2, rstd2 = _ln_stats(acc)
            err = xh2 * g2 + vd_ref[VD_LN2_B:VD_LN2_B + 1, :] - tgt_ref[rows, :]
            loss_ref[...] += _colsum(jnp.sum(err * err, axis=1, keepdims=True)) * (0.5 / d)
            dy = err * (1.0 / d)
            dg2_ref[...] += _colsum(dy * xh2)
            db2_ref[...] += _colsum(dy)
            dr2_ref[rows, :] = _ln_bwd(dy * g2, xh2, rstd2)

    row = lambda w: pl.BlockSpec((tm, w), lambda i: (i, 0))
    act = [2 * sz for _, sz in slabs]
    return pl.pallas_call(
        body, name="fwd_mlp", grid=(n,),
        in_specs=[row(d), row(d), ANY, ANY, ANY, _full(vd.shape)],
        out_specs=[row(sz) for sz in act] + [row(d), row(d), _full((8, LANES)), _full((1, d)), _full((1, d))],
        out_shape=[jax.ShapeDtypeStruct((t, sz), BF16) for sz in act]
        + [jax.ShapeDtypeStruct((t, d), BF16), jax.ShapeDtypeStruct((t, d), F32),
           jax.ShapeDtypeStruct((8, LANES), F32), jax.ShapeDtypeStruct((1, d), F32), jax.ShapeDtypeStruct((1, d), F32)],
        scratch_shapes=[pltpu.VMEM(wg.shape, BF16), pltpu.VMEM(wu.shape, BF16), pltpu.VMEM(wd.shape, BF16), pltpu.SemaphoreType.DMA((3,))],
        compiler_params=_cparams(),
    )(saved_f32, tgt, wg, wu, wd, vd)


def _bwd_mlp_slab(j, slab, dr2, prev, x1b, gate_up, wg, wu, wd, alpha, tm):
    t, d = dr2.shape
    off, sz = slab
    n = t // tm
    first = prev is None

    def body(*refs):
        if first:
            dr_ref, x1b_ref, gu_ref, wg_hbm, wu_hbm, wd_hbm = refs[:6]
        else:
            dr_ref, dxp_ref, x1b_ref, gu_ref, wg_hbm, wu_hbm, wd_hbm = refs[:7]
        dx_ref, dwg_hbm, dwu_hbm, dwd_hbm, dwg16_hbm, dwu16_hbm, dwd16_hbm, ag, au, ad, wg_v, wu_v, wd_v, copy_sems = refs[-14:]

        @pl.when(pl.program_id(0) == 0)
        def _():
            loads = _start_copies(copy_sems, [(src.at[pl.ds(off, sz)], dst) for src, dst in ((wg_hbm, wg_v), (wu_hbm, wu_v), (wd_hbm, wd_v))])
            ag[...] = jnp.zeros_like(ag)
            au[...] = jnp.zeros_like(au)
            ad[...] = jnp.zeros_like(ad)
            for cp in loads:
                cp.wait()

        dr = dr_ref[...]
        drb = dr.astype(BF16)
        x1b = x1b_ref[...]
        gpv = gu_ref[:, 0:sz].astype(F32)
        upv = gu_ref[:, sz:2 * sz].astype(F32)
        dh = _dot_nt(drb, wd_v[...])
        sg = _sigmoid(gpv)
        silu = gpv * sg
        ad[...] += _dot_tn((silu * upv).astype(BF16), drb)
        dgp = (dh * upv * (sg * (1.0 + gpv * (1.0 - sg)))).astype(BF16)
        dup = (dh * silu).astype(BF16)
        ag[...] += _dot_tn(dgp, x1b)
        au[...] += _dot_tn(dup, x1b)
        base = alpha * dr if first else dxp_ref[...]
        dx_ref[...] = base + _dot(dgp, wg_v[...]) + _dot(dup, wu_v[...])

        @pl.when(pl.program_id(0) == n - 1)
        def _():
            rows = pl.ds(off, sz)
            stores = _start_copies(copy_sems, [(ag, dwg_hbm.at[rows]), (au, dwu_hbm.at[rows]), (ad, dwd_hbm.at[rows])])
            for acc, stage in ((ag, wg_v), (au, wu_v), (ad, wd_v)):
                stage[...] = acc[...].astype(BF16)
            stores += _start_copies(copy_sems, [(wg_v, dwg16_hbm.at[rows]), (wu_v, dwu16_hbm.at[rows]), (wd_v, dwd16_hbm.at[rows])], first=3)
            for cp in stores:
                cp.wait()


    row = lambda w: pl.BlockSpec((tm, w), lambda i: (i, 0))
    ins = [dr2] + ([] if first else [prev[0]]) + [x1b, gate_up, wg, wu, wd] + ([] if first else list(prev[1:]))
    in_specs = [row(d)] + ([] if first else [row(d)]) + [row(d), row(2 * sz), ANY, ANY, ANY] + ([] if first else [ANY] * 6)
    return pl.pallas_call(
        body, name=f"bwd_mlp_{j}", grid=(n,),
        in_specs=in_specs,
        out_specs=[row(d)] + [ANY] * 6,
        out_shape=[jax.ShapeDtypeStruct((t, d), F32)] + [jax.ShapeDtypeStruct(wg.shape, F32)] * 3 + [jax.ShapeDtypeStruct(wg.shape, BF16)] * 3,
        scratch_shapes=[pltpu.VMEM((sz, d), F32)] * 3 + [pltpu.VMEM((sz, d), BF16)] * 3 + [pltpu.SemaphoreType.DMA((6,))],
        input_output_aliases={} if first else {7 + a: 1 + a for a in range(6)},
        compiler_params=_cparams(),
    )(*ins)


SMALL_VD = CHUNK
SMALL_CW = CHUNK + 8
SMALL_VQ = CHUNK + 8
SMALL_LOSS = CHUNK + 16
SMALL_BS = CHUNK + 24


def _small_rows(kwp):
    return -(-(SMALL_CW + max(kwp, 24 + SUBLANES)) // 16) * 16


def _bwd_mix(dx1, saved, wi, wo, wstt, cwf, tabs, vq, vd, mlp_small, token, alpha, tm):
    saved_f32, saved_bf16, hf_s = saved
    t, d = dx1.shape
    q = wi.shape[2]
    nc, n_pairs = CONV_BLOCK // CHUNK, q // LANES
    n = t // tm
    nb = tm // CONV_BLOCK
    assert tm % CONV_BLOCK == 0

    def body(dx1_ref, f32_ref, bf16_ref, hf_ref,
             wi_hbm, wo_hbm, wstt_ref, cwf_ref, fwd_ref, fwd_halo_ref, shift_ref, taps_ref, inv_ref, inv_taps_ref, vq_ref, vd_ref,
             loss_ref, dg2_ref, db2_ref, token_ref,
             gx_ref, dwi_hbm, dwo_hbm, small_hbm,
             wi_v, wo_v, awi, awo, dyb_ref, later_ref, dbm_ref, gf_ref, dgf_ref, small_ref, copy_sems):
        xh_ref, zu_ref, mg_ref, vhat_ref, gv_ref, yhat_ref, rs_ref, xb_ref, pag_ref, y_ref, vnb_ref = _saved_views(f32_ref, bf16_ref, d, q)
        i = pl.program_id(0)

        @pl.when(i == 0)
        def _():
            loads = _start_copies(copy_sems, [(wi_hbm, wi_v), (wo_hbm, wo_v)])
            for r in (awi, awo, small_ref, dbm_ref, dgf_ref, dyb_ref, later_ref):
                r[...] = jnp.zeros_like(r)
            gf_ref[...] = _dot(taps_ref[...], _split(cwf_ref[...], True))
            for cp in loads:
                cp.wait()

        dr1b_parts, dproj_parts = [None] * nb, [None] * nb

        def sub_tile(b):
            rows = slice(b * CONV_BLOCK, (b + 1) * CONV_BLOCK)
            dx1v = dx1_ref[rows, :]
            xh = xh_ref[rows, :]
            rsv = rs_ref[rows, :]
            small_ref[SMALL_VD + VD_LN1_G:SMALL_VD + VD_LN1_G + 1, :] += _colsum(dx1v * xh)
            small_ref[SMALL_VD + VD_LN1_B:SMALL_VD + VD_LN1_B + 1, :] += _colsum(dx1v)
            dr1 = _ln_bwd(dx1v * vd_ref[VD_LN1_G:VD_LN1_G + 1, :], xh, rsv[:, RS_LN1:RS_LN1 + 1])
            dr1b = dr1.astype(BF16)
            yield
            dy = _dot_nt(dr1b, wo_v[...])
            yield
            vhat = vhat_ref[rows, :]
            sgu_g = vq_ref[VQ_SGU_G:VQ_SGU_G + 1, :]
            doa = dy[:, 0:q]
            dm = doa * zu_ref[rows, :]
            dpu = (doa * mg_ref[rows, :]).astype(BF16)
            acc = dm[0:CHUNK]
            for c in range(1, nc):
                acc = acc + dm[c * CHUNK:(c + 1) * CHUNK]
            dbm_ref[...] += acc
            pa = pag_ref[rows, 0:q].astype(F32)
            sg = _sigmoid(pag_ref[rows, q:2 * q].astype(F32))
            yhat = yhat_ref[rows, :]
            cln_g = vq_ref[VQ_CLN_G:VQ_CLN_G + 1, :]
            yn = yhat * cln_g + vq_ref[VQ_CLN_B:VQ_CLN_B + 1, :]
            sy = _sigmoid(yn)
            dyn = dy[:, q:2 * q] * (sy * (1.0 + yn * (1.0 - sy)))
            small_ref[SMALL_VQ + VQ_CLN_G:SMALL_VQ + VQ_CLN_G + 1, q:2 * q] += _colsum(dyn * yhat)
            small_ref[SMALL_VQ + VQ_CLN_B:SMALL_VQ + VQ_CLN_B + 1, q:2 * q] += _colsum(dyn)
            dyc = _ln_bwd(dyn * cln_g, yhat, rsv[:, RS_CONV:RS_CONV + 1])
            small_ref[SMALL_VQ + VQ_CONV_B:SMALL_VQ + VQ_CONV_B + 1, q:2 * q] += _colsum(dyc)
            dyb_ref[b, 0:CONV_BLOCK, :] = dyc
            yield
            wgrads = _mix_wgrad(dm, vnb_ref[rows, :], nc, n_pairs)
            dvn = _mix(wstt_ref, dm, nc, n_pairs)
            own = _dot(fwd_ref[...], _split(dyb_ref[b]))
            with_later = own + _dot(fwd_halo_ref[...], _split(later_ref[...]))
            later_ref[...] = dyb_ref[b, 0:HALO, :]
            yield
            for p, g in enumerate(wgrads):
                for half in range(2):
                    small_ref[0:CHUNK, (2 * p + half) * CHUNK:(2 * p + half + 1) * CHUNK] += g[half * CHUNK:(half + 1) * CHUNK]
            small_ref[SMALL_VQ + VQ_SGU_G:SMALL_VQ + VQ_SGU_G + 1, q:2 * q] += _colsum(dvn * vhat)
            small_ref[SMALL_VQ + VQ_SGU_B:SMALL_VQ + VQ_SGU_B + 1, q:2 * q] += _colsum(dvn)
            dpv = (_ln_bwd(dvn * sgu_g, vhat, rsv[:, RS_SGU:RS_SGU + 1]) * gv_ref[rows, :]).astype(BF16)
            dgf_ref[...] += _cmul(_cmul(own, shift_ref[...]), hf_ref[b * 2 * DFT_F:(b + 1) * 2 * DFT_F, :], conj_b=True)
            product = _split(_cmul(with_later, gf_ref[...], conj_b=True))
            yield
            dh = _dot(inv_ref[...], product)
            yield
            da = (dh * sg).astype(BF16)
            dg = (dh * pa * (sg * (1.0 - sg))).astype(BF16)
            yield
            gx = alpha * dr1
            for dpj, wj in zip((dpu, dpv, da, dg), range(4)):
                gx = gx + _dot_nt(dpj, wi_v[wj])
            gx_ref[rows, :] = gx
            dr1b_parts[b], dproj_parts[b] = dr1b, (dpu, dpv, da, dg)

        _interleave([sub_tile(b) for b in reversed(range(nb))])

        awo[...] += _dot_tn(y_ref[...], jnp.concatenate(dr1b_parts, axis=0))
        xb = xb_ref[...]
        for j in range(4):
            awi[j] += _dot_tn(xb, jnp.concatenate([part[j] for part in dproj_parts], axis=0))

        @pl.when(i == n - 1)
        def _():
            stores = _start_copies(copy_sems, [(awi, dwi_hbm), (awo, dwo_hbm)])
            lane = lax.broadcasted_iota(jnp.int32, (CHUNK, LANES), 1)
            low = lane < HEAD_DIM
            dbs = jnp.zeros((CHUNK, LANES), F32)
            for p in range(n_pairs):
                grp = dbm_ref[:, p * LANES:(p + 1) * LANES]
                dbs = jnp.where(lane == 2 * p, jnp.sum(jnp.where(low, grp, 0.0), axis=1, keepdims=True), dbs)
                dbs = jnp.where(lane == 2 * p + 1, jnp.sum(jnp.where(low, 0.0, grp), axis=1, keepdims=True), dbs)
            tril = lax.broadcasted_iota(jnp.int32, (CHUNK, CHUNK), 0) >= lax.broadcasted_iota(jnp.int32, (CHUNK, CHUNK), 1)
            for h in range(2 * n_pairs):
                block = small_ref[0:CHUNK, h * CHUNK:(h + 1) * CHUNK]
                small_ref[0:CHUNK, h * CHUNK:(h + 1) * CHUNK] = jnp.where(tril, block, 0.0)
            small_ref[SMALL_VD + VD_LN2_G:SMALL_VD + VD_LN2_G + 1, :] = dg2_ref[...]
            small_ref[SMALL_VD + VD_LN2_B:SMALL_VD + VD_LN2_B + 1, :] = db2_ref[...]
            small_ref[SMALL_CW:SMALL_CW + kwp, 0:q] = _dot(inv_taps_ref[...], _split(dgf_ref[...], True))
            small_ref[SMALL_LOSS:SMALL_LOSS + SUBLANES, q:q + LANES] = loss_ref[...]
            small_ref[SMALL_BS:SMALL_BS + SUBLANES, q:q + LANES] = jnp.transpose(dbs)[0:SUBLANES]
            stores += _start_copies(copy_sems, [(small_ref, small_hbm)], first=2)
            for cp in stores:
                cp.wait()

    rev = lambda w: pl.BlockSpec((tm, w), lambda i: (n - 1 - i, 0))
    kwp = cwf.shape[0]
    small = jax.ShapeDtypeStruct((_small_rows(kwp), 2 * q), F32)
    small_ins = [wstt, cwf, tabs["fwd"], tabs["fwd_halo"], tabs["shift"], tabs["taps"], tabs["inv_in"], tabs["inv_taps"], vq, vd,
                 *mlp_small]
    return pl.pallas_call(
        body, name="bwd_mix", grid=(n,),
        in_specs=[rev(d), rev(saved_f32.shape[1]), rev(saved_bf16.shape[1]),
                  pl.BlockSpec((nb * 2 * DFT_F, q), lambda i: (n - 1 - i, 0)), ANY, ANY] + [_full(a.shape) for a in small_ins] + [ANY],
        out_specs=[rev(d), ANY, ANY, ANY],
        out_shape=[jax.ShapeDtypeStruct((t, d), F32), jax.ShapeDtypeStruct(wi.shape, F32), jax.ShapeDtypeStruct(wo.shape, F32), small],
        scratch_shapes=[pltpu.VMEM(wi.shape, BF16), pltpu.VMEM(wo.shape, BF16), pltpu.VMEM(wi.shape, F32), pltpu.VMEM(wo.shape, F32),
                        pltpu.VMEM((nb, DFT_N, q), F32), pltpu.VMEM((HALO, q), F32),
                        pltpu.VMEM((CHUNK, q), F32), pltpu.VMEM((2 * DFT_F, q), F32), pltpu.VMEM((2 * DFT_F, q), F32),
                        pltpu.VMEM(small.shape, F32), pltpu.SemaphoreType.DMA((3,))],
        compiler_params=_cparams(),
    )(dx1, saved_f32, saved_bf16, hf_s, wi, wo, *small_ins, token)


def _prep(me_arr, w_in, w_out, w_gate_t, w_up_t, w_down, conv_w, kwp):
    kw, cshard = conv_w.shape

    def body(me_ref, wi_ref, wo_ref, wg_ref, wu_ref, wd_ref, cw_ref, oi, oo, og, ou, od, oc):
        for src, dst in ((wi_ref, oi), (wo_ref, oo), (wg_ref, og), (wu_ref, ou), (wd_ref, od)):
            dst[...] = src[...].astype(BF16)
        oc[0:kw, :] = cw_ref[...]
        oc[kw:kwp, :] = jnp.zeros((kwp - kw, cshard), F32)

    ins = (w_in, w_out, w_gate_t, w_up_t, w_down, conv_w)
    outs = [jax.ShapeDtypeStruct((4,) + a.shape, BF16) for a in ins[:5]] + [jax.ShapeDtypeStruct((4, kwp, cshard), F32)]
    grid_spec = pltpu.PrefetchScalarGridSpec(
        num_scalar_prefetch=1, grid=(1,),
        in_specs=[pl.BlockSpec(a.shape, lambda i, me: (0, 0)) for a in ins],
        out_specs=[pl.BlockSpec((None,) + o.shape[1:], lambda i, me: (me[0], 0, 0)) for o in outs])
    return pl.pallas_call(body, name="wprep", grid_spec=grid_spec, out_shape=outs, compiler_params=_cparams())(me_arr, *ins)


def _coords():
    return tuple(lax.axis_index(a) for a in MESH_AXES)


def _other_chips(x, y):
    return [(1 - x, y), (x, 1 - y), (1 - x, 1 - y)]


def _remote(src, dst, send_sem, recv_sem, to):
    return pltpu.make_async_remote_copy(src_ref=src, dst_ref=dst, send_sem=send_sem, recv_sem=recv_sem,
                                        device_id=to, device_id_type=MESH_ID)


def _hbm_call(body, name, ins, out_shape, scratch_shapes, aliases=None):
    return pl.pallas_call(
        body, name=name, in_specs=[ANY] * len(ins), out_specs=[ANY] * len(out_shape), out_shape=out_shape,
        scratch_shapes=scratch_shapes, input_output_aliases=aliases or {},
    )(*ins)


class _Gather:
    def __init__(self, bufs, send_sems, recv_sems):
        self.bufs, self.send_sems, self.recv_sems = bufs, send_sems, recv_sems
        self.x, self.y, self.c = _coords()

    def _copies(self, stage):
        x, y, c = self.x, self.y, self.c
        for a, buf in enumerate(self.bufs):
            hr = buf.shape[1] // 2
            for j, chip in enumerate(_other_chips(x, y)):
                if stage == "ici_out":
                    ref, k, to = buf.at[2 * x + y, pl.ds(c * hr, hr)], j, (*chip, c)
                elif stage == "ici_in":
                    ref, k, to = buf.at[2 * chip[0] + chip[1], pl.ds(c * hr, hr)], j, (*chip, c)
                elif stage == "d2d_out":
                    ref, k, to = buf.at[2 * chip[0] + chip[1], pl.ds(c * hr, hr)], 3 + j, (x, y, 1 - c)
                else:
                    ref, k, to = buf.at[2 * chip[0] + chip[1], pl.ds((1 - c) * hr, hr)], 3 + j, (x, y, 1 - c)
                yield _remote(ref, ref, self.send_sems.at[a, k], self.recv_sems.at[a, k], to)

    def start(self):
        for cp in self._copies("ici_out"):
            cp.start()

    def forward(self):
        for landed, onward in zip(self._copies("ici_in"), self._copies("d2d_out")):
            landed.wait_recv()
            onward.start()

    def finish(self):
        for cp in self._copies("d2d_in"):
            cp.wait_recv()
        for stage in ("ici_out", "d2d_out"):
            for cp in self._copies(stage):
                cp.wait_send()


def _gather_sems(n):
    return [pltpu.SemaphoreType.DMA((n, 6)), pltpu.SemaphoreType.DMA((n, 6))]


def _gather_shards(bufs):
    n = len(bufs)

    def body(*refs):
        g = _Gather(refs[n:2 * n], *refs[2 * n:])
        g.start()
        g.forward()
        g.finish()

    return _hbm_call(body, "gather_shards", bufs, [jax.ShapeDtypeStruct(s.shape, s.dtype) for s in bufs],
                     _gather_sems(n), aliases={a: a for a in range(n)})


def _pair_swap(name, arrs):
    n = len(arrs)

    def body(*refs):
        src, land = refs[:n], refs[n:2 * n]
        send_sems, recv_sems = refs[2 * n:]
        x, y, c = _coords()
        copies = []
        for a in range(n):
            s = src[a].at[pl.ds(0, arrs[a].shape[0]), 1 - c] if arrs[a].ndim == 4 else src[a].at[1 - c]
            copies.append(_remote(s, land[a], send_sems.at[a], recv_sems.at[a], (x, y, 1 - c)))
            copies[-1].start()
        for cp in copies:
            cp.wait()

    outs = [jax.ShapeDtypeStruct(s.shape[:-3] + s.shape[-2:], s.dtype) for s in arrs]
    return _hbm_call(body, name, arrs, outs, [pltpu.SemaphoreType.DMA((n,)), pltpu.SemaphoreType.DMA((n,))])


class _Exchange:
    def __init__(self, src, dst, send_sems, recv_sems):
        self.src, self.dst, self.send_sems, self.recv_sems = src, dst, send_sems, recv_sems
        self.x, self.y, self.c = _coords()

    def _copies(self, incoming):
        x, y, c = self.x, self.y, self.c
        for a, (s, d) in enumerate(zip(self.src, self.dst)):
            for j, chip in enumerate(_other_chips(x, y)):
                slot = 2 * chip[0] + chip[1]
                if incoming:
                    out, into = d.at[slot], d.at[slot]
                else:
                    out, into = (s.at[slot] if len(s.shape) == 3 else s), d.at[2 * x + y]
                yield _remote(out, into, self.send_sems.at[a, j], self.recv_sems.at[a, j], (*chip, c))

    def start(self):
        for cp in self._copies(False):
            cp.start()

    def finish(self):
        for cp in self._copies(True):
            cp.wait_recv()
        for cp in self._copies(False):
            cp.wait_send()


def _exchange_shapes(arrs):
    return [jax.ShapeDtypeStruct((4,) + s.shape[-2:], s.dtype) for s in arrs]


class _FlatSems:
    def __init__(self, ref):
        self.ref = ref

    @property
    def at(self):
        return self

    def __getitem__(self, idx):
        return self.ref.at[3 * idx[0] + idx[1]]


HBM = pl.BlockSpec(memory_space=pltpu.HBM)
SEM = pl.BlockSpec(memory_space=pltpu.SEMAPHORE)
DATAFLOW = pltpu.SideEffectType.DATAFLOW_SIDE_EFFECTING


def _exchange_start(name, arrs):
    n = len(arrs)
    lands = _exchange_shapes(arrs)

    def body(*refs):
        src, land = refs[:n], refs[n:2 * n]
        send_sems, recv_sems = refs[2 * n:2 * n + 2]
        token = refs[-1]
        _Exchange(src, land, _FlatSems(send_sems), _FlatSems(recv_sems)).start()
        token[...] = jnp.zeros_like(token)

    hbm = lambda a: pltpu.with_memory_space_constraint(a, pltpu.HBM)
    outs = pl.pallas_call(
        body, name=name,
        out_shape=(pltpu.SemaphoreType.DMA((3 * n,)), pltpu.SemaphoreType.DMA((3 * n,)),
                   *[pltpu.HBM(a.shape, a.dtype) for a in arrs], *[pltpu.HBM(s.shape, s.dtype) for s in lands],
                   jax.ShapeDtypeStruct((SUBLANES, LANES), F32)),
        in_specs=[HBM] * (2 * n), out_specs=(SEM, SEM, *[HBM] * (2 * n), pl.BlockSpec(memory_space=pltpu.VMEM)),
        input_output_aliases={a: 2 + a for a in range(2 * n)},
        compiler_params=pltpu.CompilerParams(has_side_effects=DATAFLOW),
    )(*[hbm(a) for a in arrs], *[hbm(lax.empty(s.shape, s.dtype)) for s in lands])
    return outs[:-1], outs[-1]


def _exchange_wait(name, started, after):
    send_sems, recv_sems, *bufs = started
    n = len(bufs) // 2

    def body(*refs):
        src, land = refs[:n], refs[n:2 * n]
        send_sems, recv_sems = refs[2 * n:2 * n + 2]
        _Exchange(src, land, _FlatSems(send_sems), _FlatSems(recv_sems)).finish()

    outs = pl.pallas_call(
        body, name=name,
        out_shape=tuple(pltpu.HBM(b.shape, b.dtype) for b in bufs),
        in_specs=[HBM] * (2 * n) + [SEM, SEM] + [ANY] * len(after), out_specs=tuple([HBM] * (2 * n)),
        input_output_aliases={a: a for a in range(2 * n)},
        compiler_params=pltpu.CompilerParams(has_side_effects=DATAFLOW),
    )(*bufs, send_sems, recv_sems, *after)
    return list(outs[:n]), list(outs[n:])


def _pair_gather(name, halves):
    n = len(halves)

    def body(*refs):
        src, dst = refs[:n], refs[n:2 * n]
        send_sems, recv_sems = refs[2 * n:]
        x, y, c = _coords()
        copies = [_remote(src[a], dst[a], send_sems.at[a], recv_sems.at[a], (x, y, 1 - c)) for a in range(n)]
        for cp in copies:
            cp.start()
        for cp in copies:
            cp.wait()

    outs = [jax.ShapeDtypeStruct(s.shape, s.dtype) for s in halves]
    return _hbm_call(body, name, halves, outs, [pltpu.SemaphoreType.DMA((n,)), pltpu.SemaphoreType.DMA((n,))])


def _pair_sum(a, g, land, c_arr, out_dtype):
    nq, _, hr, cc = g.shape

    def body(c_ref, g_ref, l_ref, o_ref):
        o_ref[...] = (g_ref[...] + l_ref[...].astype(F32)).astype(out_dtype)

    spec = pl.BlockSpec((None, hr, cc), lambda qi, cr: (qi, 0, 0))
    grid_spec = pltpu.PrefetchScalarGridSpec(
        num_scalar_prefetch=1, grid=(nq,),
        in_specs=[pl.BlockSpec((None, None, hr, cc), lambda qi, cr: (qi, cr[0], 0, 0)), spec], out_specs=spec)
    return pl.pallas_call(body, name=f"pair_sum_{a}", grid_spec=grid_spec, out_shape=jax.ShapeDtypeStruct((nq, hr, cc), out_dtype),
                          compiler_params=_cparams())(c_arr, g, land)


def _chip_sum(a, parts, own, me_arr, after):
    _, hr, cc = parts.shape

    def body(me_ref, p_ref, own_ref, after_ref, o_ref):
        for mine in range(4):
            @pl.when(me_ref[0] == mine)
            def _():
                term = lambda j: (own_ref if j == mine else p_ref.at[j])[...].astype(F32)
                o_ref[...] = ((term(0) + term(1)) + term(2)) + term(3)

    own_spec = (pl.BlockSpec((None, hr, cc), lambda i, me: (me[0], 0, 0)) if own.ndim == 3
                else pl.BlockSpec((hr, cc), lambda i, me: (0, 0)))
    grid_spec = pltpu.PrefetchScalarGridSpec(
        num_scalar_prefetch=1, grid=(1,),
        in_specs=[pl.BlockSpec((4, hr, cc), lambda i, me: (0, 0, 0)), own_spec, ANY],
        out_specs=pl.BlockSpec((hr, cc), lambda i, me: (0, 0)))
    return pl.pallas_call(body, name=f"chip_sum_{a}", grid_spec=grid_spec, out_shape=jax.ShapeDtypeStruct((hr, cc), F32),
                          compiler_params=_cparams())(me_arr, parts, own, after)


def _row_block(rows, cols, limit=1 << 20):
    best = 8
    for tr in range(8, rows + 1, 8):
        if rows % tr == 0 and tr * cols * 4 <= limit:
            best = tr
    return best


def _adamw(name, w, g_mine, g_other, m, v, c_arr):
    r, c = w.shape
    hr, cg = g_mine.shape
    tr = hr if r % hr == 0 and hr * cg * 4 <= (3 << 19) else math.gcd(_row_block(hr, cg), r)
    per_half = hr // tr
    bc1 = 1.0 - ADAM_B1 ** ADAM_STEP
    bc2 = 1.0 - ADAM_B2 ** ADAM_STEP

    def body(c_ref, w_ref, gm_ref, go_ref, m_ref, v_ref, go, do, mo, vo):
        gv = jnp.where(pl.program_id(0) // per_half == c_ref[0], gm_ref[:, 0:c], go_ref[:, 0:c])
        mn = ADAM_B1 * m_ref[...] + (1.0 - ADAM_B1) * gv
        vn = ADAM_B2 * v_ref[...] + (1.0 - ADAM_B2) * (gv * gv)
        go[...] = gv
        mo[...] = mn
        vo[...] = vn
        do[...] = -ADAM_LR * ((mn / bc1) / (jnp.sqrt(vn / bc2) + ADAM_EPS) + ADAM_WD * w_ref[...])

    blk = pl.BlockSpec((tr, c), lambda i, cr: (i, 0))
    gblk = pl.BlockSpec((tr, cg), lambda i, cr: (i % per_half, 0))
    grid_spec = pltpu.PrefetchScalarGridSpec(num_scalar_prefetch=1, grid=(r // tr,), in_specs=[blk, gblk, gblk, blk, blk],
                                             out_specs=[blk] * 4)
    return pl.pallas_call(body, name=f"adamw_{name}", grid_spec=grid_spec, out_shape=[jax.ShapeDtypeStruct((r, c), F32)] * 4,
                          compiler_params=_cparams())(c_arr, w, g_mine, g_other, m, v)


SMALL_Q = ("sgu_ln_g", "sgu_ln_b", "conv_b", "conv_ln_g", "conv_ln_b")
SMALL_D = ("ln1_g", "ln1_b", "ln2_g", "ln2_b")


def _adamw_small(g_mine, g_other, c_arr, me_arr, params):
    names = list(SMALL_Q) + list(SMALL_D) + ["w_s", "b_s", "conv_w"]
    hr, width = g_mine.shape
    q = width // 2
    heads = params["w_s"][0].shape[1]
    kw, cshard = params["conv_w"][0].shape[1:]
    bc1 = 1.0 - ADAM_B1 ** ADAM_STEP
    bc2 = 1.0 - ADAM_B2 ** ADAM_STEP

    def update(w, g, m, v):
        mn = ADAM_B1 * m + (1.0 - ADAM_B1) * g
        vn = ADAM_B2 * v + (1.0 - ADAM_B2) * (g * g)
        return g, -ADAM_LR * ((mn / bc1) / (jnp.sqrt(vn / bc2) + ADAM_EPS) + ADAM_WD * w), mn, vn

    def body(c_ref, me_ref, gm_ref, go_ref, *refs):
        ins = {nm: refs[3 * k:3 * k + 3] for k, nm in enumerate(names)}
        outs = {nm: refs[3 * len(names) + 4 * k:3 * len(names) + 4 * k + 4] for k, nm in enumerate(names)}
        loss_ref, cw_ref = refs[-2:]
        first, second = gm_ref[...], go_ref[...]
        low = c_ref[0] == 0
        g_all = jnp.concatenate([jnp.where(low, first, second), jnp.where(low, second, first)], axis=0)

        def apply(nm, g, at):
            w, m, v = (r[at] for r in ins[nm])
            for o, val in zip(outs[nm], update(w, g, m, v)):
                o[at] = val

        for row, nm in enumerate(SMALL_Q):
            apply(nm, g_all[SMALL_VQ + row:SMALL_VQ + row + 1, q:2 * q], ...)
        for row, nm in enumerate(SMALL_D):
            apply(nm, g_all[SMALL_VD + row:SMALL_VD + row + 1, :], ...)
        for h in range(heads):
            apply("w_s", g_all[0:CHUNK, h * CHUNK:(h + 1) * CHUNK], (0, h))
        apply("b_s", g_all[SMALL_BS:SMALL_BS + heads, q:q + LANES], 0)
        cw_ref[...] = jnp.zeros_like(cw_ref)
        for chip in range(4):
            @pl.when(me_ref[0] == chip)
            def _():
                cw_ref[...] = g_all[SMALL_CW:SMALL_CW + cw_ref.shape[0], chip * cshard:(chip + 1) * cshard]
        apply("conv_w", cw_ref[0:kw, :], 0)
        loss_ref[...] = g_all[SMALL_LOSS:SMALL_LOSS + SUBLANES, q:q + LANES]

    arrays = [a for nm in names for a in params[nm]]
    out_shape = [jax.ShapeDtypeStruct(params[nm][0].shape, F32) for nm in names for _ in range(4)] + [jax.ShapeDtypeStruct((SUBLANES, LANES), F32)]
    whole = lambda shape: pl.BlockSpec(shape, lambda i, c, me: (0,) * len(shape))
    grid_spec = pltpu.PrefetchScalarGridSpec(
        num_scalar_prefetch=2, grid=(1,),
        in_specs=[whole(g_mine.shape), whole(g_other.shape)] + [whole(a.shape) for a in arrays],
        out_specs=[whole(s.shape) for s in out_shape],
        scratch_shapes=[pltpu.VMEM((-(-kw // SUBLANES) * SUBLANES, cshard), F32)])
    res = pl.pallas_call(body, name="adamw_small", grid_spec=grid_spec, out_shape=out_shape, compiler_params=_cparams())(
        c_arr, me_arr, g_mine, g_other, *arrays)
    return {nm: list(res[4 * k:4 * k + 4]) for k, nm in enumerate(names)}, res[-1]


def _pad_rows(a, rows):
    return jnp.pad(a, ((0, rows - a.shape[0]), (0, 0)))


def kernel(x, w_in, sgu_ln_g, sgu_ln_b, w_s, b_s, conv_w, conv_b, conv_ln_g, conv_ln_b, w_out, ln1_g, ln1_b, w_gate, w_up, w_down, ln2_g, ln2_b, loss_target, m_w_in, m_sgu_ln_g, m_sgu_ln_b, m_w_s, m_b_s, m_conv_w, m_conv_b, m_conv_ln_g, m_conv_ln_b, m_w_out, m_ln1_g, m_ln1_b, m_w_gate, m_w_up, m_w_down, m_ln2_g, m_ln2_b, v_w_in, v_sgu_ln_g, v_sgu_ln_b, v_w_s, v_b_s, v_conv_w, v_conv_b, v_conv_ln_g, v_conv_ln_b, v_w_out, v_ln1_g, v_ln1_b, v_w_gate, v_w_up, v_w_down, v_ln2_g, v_ln2_b):
    depth, d, q = w_in.shape
    assert depth == 1 and x.shape[0] == 1
    t = x.shape[1]
    heads = w_s.shape[1]
    kw, cshard = conv_w.shape[1], conv_w.shape[2]
    fs = w_gate.shape[2]
    slabs = _hidden_slabs(4 * fs)
    n_pairs = q // LANES
    assert heads * HEAD_DIM == q and q % LANES == 0 and w_s.shape[2] == CHUNK and 4 * cshard == q and kw - 1 <= HALO
    alpha = (2.0 * depth) ** 0.25
    tm = min(512, t)
    assert t % tm == 0 and tm % CHUNK == 0
    x2, tgt = x[0], loss_target[0]
    mx, my, mc = _coords()
    me = 2 * mx + my
    c_arr = jnp.reshape(mc, (1,)).astype(jnp.int32)

    kwp = -(-kw // 16) * 16
    me_arr = jnp.reshape(me, (1,)).astype(jnp.int32)
    wi, wo, wg, wu, wd, cw4 = _prep(me_arr, w_in[0], w_out[0], w_gate[0].T, w_up[0].T, w_down[0], conv_w[0], kwp)
    wi, wo, cw4 = _gather_shards([wi, wo, cw4])
    wo = wo.reshape(d, d)
    cw = jnp.transpose(cw4, (1, 0, 2)).reshape(kwp, q)
    cwf = _pad_rows(cw[:kw][::-1], kwp)
    tabs = {name: jnp.asarray(tab) for name, tab in _dft_tables(kw, kwp, q).items()}

    wm = jnp.where(jnp.tril(jnp.ones((CHUNK, CHUNK), bool)), w_s[0], 0.0)
    wst = wm.reshape(n_pairs, 2 * CHUNK, CHUNK).astype(BF16)
    wstt = jnp.transpose(wm, (0, 2, 1)).reshape(n_pairs, 2 * CHUNK, CHUNK).astype(BF16)
    bmat = jnp.repeat(b_s[0].T, HEAD_DIM, axis=1)
    vq = _pad_rows(jnp.concatenate([sgu_ln_g, sgu_ln_b, conv_b, conv_ln_g, conv_ln_b], axis=0), 8)
    vd = _pad_rows(jnp.concatenate([ln1_g, ln1_b, ln2_g, ln2_b], axis=0), 8)

    *saved, wg, wu, wd = _fwd_mix(x2, wi, wo, wst, bmat, cwf, tabs, vq, vd, [wg, wu, wd], alpha, tm)
    wg, wu, wd = (w.reshape(4 * fs, d) for w in (wg, wu, wd))
    *acts, x1b, dr2, loss_part, dg2, db2 = _fwd_mlp(saved[0], tgt, wg, wu, wd, vd, alpha, slabs, tm)
    mlp_grads = None
    for j, slab in enumerate(slabs):
        mlp_grads = _bwd_mlp_slab(j, slab, dr2, mlp_grads, x1b, acts[j], wg, wu, wd, alpha, tm)
    dx1 = mlp_grads[0]
    mlp_halves = [b.reshape(4, 2, fs // 2, d) for b in mlp_grads[1:]]
    landed = _pair_swap("pair_swap_mlp", mlp_halves[3:])
    mlp_sums = [_pair_sum(f"mlp{a}", h, l, c_arr, BF16) for a, (h, l) in enumerate(zip(mlp_halves[:3], landed))]
    mlp_started, token = _exchange_start("exchange_mlp_start", mlp_sums)
    grad_x, dwi, dwo, small = _bwd_mix(dx1, saved, wi, wo, wstt, cwf, tabs, vq, vd, (loss_part, dg2, db2), token, alpha, tm)
    mlp_sums, mlp_parts = _exchange_wait("exchange_mlp_wait", mlp_started, [dwo])

    big = [dwi, dwo.reshape(4, d // 4, d)]
    halves = [b.reshape(4, 2, b.shape[1] // 2, b.shape[2]) for b in big] + [small.reshape(2, small.shape[0] // 2, small.shape[1])]
    landed = _pair_swap("pair_swap_mix", halves)
    sums = [_pair_sum(f"mix{a}", h, l, c_arr, BF16) for a, (h, l) in enumerate(zip(halves[:-1], landed[:-1]))]
    sums.append(_pair_sum("small", halves[-1][None], landed[-1][None], c_arr, F32)[0])
    mix_started, token = _exchange_start("exchange_mix_start", sums)

    out, raw = {}, {}

    def finish(first, names, parts, sums, after):
        mine = [_chip_sum(first + a, p, s, me_arr, after) for a, (p, s) in enumerate(zip(parts, sums))]
        other = _pair_gather(f"pair_gather_{first}", mine)
        for a, nm in enumerate(names):
            w_, m_, v_ = weights[nm]
            if nm in ("w_gate", "w_up"):
                raw[nm] = _adamw(nm, w_[0].T, mine[a], other[a], m_[0].T, v_[0].T, c_arr)
                out[nm] = [o.T for o in raw[nm]]
            else:
                raw[nm] = out[nm] = _adamw(nm, w_[0], mine[a], other[a], m_[0], v_[0], c_arr)
        return mine[-1], other[-1]

    weights = {"w_in": (w_in, m_w_in, v_w_in), "w_out": (w_out, m_w_out, v_w_out), "w_gate": (w_gate, m_w_gate, v_w_gate),
               "w_up": (w_up, m_w_up, v_w_up), "w_down": (w_down, m_w_down, v_w_down)}
    finish(2, ["w_gate", "w_up", "w_down"], mlp_parts, mlp_sums, token)
    sums, parts = _exchange_wait("exchange_mix_wait", mix_started, [raw[nm][1] for nm in ("w_gate", "w_up", "w_down")])
    small_mine, small_other = finish(5, ["w_in", "w_out"], parts, sums, parts[0])

    small_params = {
        "sgu_ln_g": (sgu_ln_g, m_sgu_ln_g, v_sgu_ln_g), "sgu_ln_b": (sgu_ln_b, m_sgu_ln_b, v_sgu_ln_b),
        "conv_b": (conv_b, m_conv_b, v_conv_b), "conv_ln_g": (conv_ln_g, m_conv_ln_g, v_conv_ln_g),
        "conv_ln_b": (conv_ln_b, m_conv_ln_b, v_conv_ln_b), "ln1_g": (ln1_g, m_ln1_g, v_ln1_g), "ln1_b": (ln1_b, m_ln1_b, v_ln1_b),
        "ln2_g": (ln2_g, m_ln2_g, v_ln2_g), "ln2_b": (ln2_b, m_ln2_b, v_ln2_b), "w_s": (w_s, m_w_s, v_w_s),
        "b_s": (b_s, m_b_s, v_b_s), "conv_w": (conv_w, m_conv_w, v_conv_w)}
    small_out, loss_block = _adamw_small(small_mine, small_other, c_arr, me_arr, small_params)
    loss = loss_block[0, 0]
    names = ["w_in", "sgu_ln_g", "sgu_ln_b", "w_s", "b_s", "conv_w", "conv_b", "conv_ln_g", "conv_ln_b", "w_out",
             "ln1_g", "ln1_b", "w_gate", "w_up", "w_down", "ln2_g", "ln2_b"]
    result = [loss, grad_x[None]]
    for kind in range(4):
        for nm in names:
            result.append(out[nm][kind][None] if nm in out else small_out[nm][kind])
    return tuple(result)
```

```python
import math

import jax
import numpy as np
import jax.numpy as jnp
from jax import lax
from jax.experimental import pallas as pl
from jax.experimental.pallas import tpu as pltpu

F32 = jnp.float32
BF16 = jnp.bfloat16

LN_EPS = 1e-5
HEAD_DIM = 64
CHUNK = 128
HALO = 32
LANES = 128
MXU_N = 256
ADAM_LR, ADAM_B1, ADAM_B2, ADAM_EPS, ADAM_WD, ADAM_STEP = 0.001, 0.9, 0.999, 1e-08, 0.01, 10
VMEM_LIMIT = 63 * 1024 * 1024
MESH_AXES = ("x", "y", "c")
MESH_ID = pl.DeviceIdType.MESH


def _dot(a, b):
    return jnp.dot(a, b, preferred_element_type=F32)


def _dot_nt(a, b):
    return lax.dot_general(a, b, (((1,), (1,)), ((), ())), preferred_element_type=F32)


def _dot_tn(a, b):
    return lax.dot_general(a, b, (((0,), (0,)), ((), ())), preferred_element_type=F32)


def _sigmoid(v):
    return 1.0 / (1.0 + jnp.exp(-v))


def _gelu(v):
    cdf = 0.5 * (1.0 + lax.erf(v * (1.0 / math.sqrt(2.0))))
    pdf = jnp.exp(-0.5 * v * v) * (1.0 / math.sqrt(2.0 * math.pi))
    return v * cdf, cdf + v * pdf


def _ln_stats(v):
    mu = jnp.mean(v, axis=-1, keepdims=True)
    d = v - mu
    rstd = lax.rsqrt(jnp.mean(d * d, axis=-1, keepdims=True) + LN_EPS)
    return d * rstd, rstd


def _ln_bwd(dxhat, xhat, rstd):
    m1 = jnp.mean(dxhat, axis=-1, keepdims=True)
    m2 = jnp.mean(dxhat * xhat, axis=-1, keepdims=True)
    return rstd * (dxhat - m1 - xhat * m2)


def _colsum(v):
    return jnp.sum(v, axis=0, keepdims=True)


def _pair_lanes(v, nc, p):
    return jnp.concatenate([v[c * CHUNK:(c + 1) * CHUNK, p * LANES:(p + 1) * LANES] for c in range(nc)], axis=1)


def _unpair(parts, nc):
    rows = [jnp.concatenate([part[:, c * LANES:(c + 1) * LANES] for part in parts], axis=1) for c in range(nc)]
    return jnp.concatenate(rows, axis=0)


def _low_head(nc):
    lane = lax.broadcasted_iota(jnp.int32, (CHUNK, nc * LANES), 1)
    return (lane & (LANES - 1)) < HEAD_DIM


def _mix(wst_ref, v, nc, n_pairs):
    vb = v.astype(BF16)
    low = _low_head(nc)
    parts = []
    for p in range(n_pairs):
        r = _dot(wst_ref[p], _pair_lanes(vb, nc, p))
        parts.append(jnp.where(low, r[:CHUNK], r[CHUNK:]))
    return _unpair(parts, nc)


def _mix_wgrad(dm, vn, nc, n_pairs):
    low = _low_head(nc)
    vb = vn.astype(BF16)
    out = []
    for p in range(n_pairs):
        a = _pair_lanes(dm, nc, p)
        lhs = jnp.concatenate([jnp.where(low, a, 0.0), jnp.where(low, 0.0, a)], axis=0).astype(BF16)
        out.append(_dot_nt(lhs, _pair_lanes(vb, nc, p)))
    return out


SUBLANES = 8


CONV_BLOCK = 256
DFT_N = CONV_BLOCK + HALO
DFT_F = -(-(DFT_N // 2 + 1) // SUBLANES) * SUBLANES


def _terms(m, exact):
    hi = m.astype(np.float32).astype(BF16)
    lo = (m.astype(np.float32) - hi.astype(np.float32)).astype(BF16)
    return np.concatenate([hi, hi, lo] if exact else [hi], axis=1)


def _split(v, exact=False):
    hi = v.astype(BF16)
    if not exact:
        return hi
    lo = (v - hi.astype(F32)).astype(BF16)
    return jnp.concatenate([hi, lo, hi], axis=0)


def _dft_tables(kw, kwp, q):
    nf = DFT_N // 2 + 1
    ang = 2.0 * np.pi * np.arange(nf)[:, None] * np.arange(DFT_N)[None, :] / DFT_N
    fwd = np.zeros((2 * DFT_F, DFT_N))
    fwd[:nf], fwd[DFT_F:DFT_F + nf] = np.cos(ang), -np.sin(ang)
    weight = np.full((nf, 1), 2.0 / DFT_N)
    weight[0] = weight[-1] = 1.0 / DFT_N
    inv = np.zeros((DFT_N, 2 * DFT_F))
    inv[:, :nf], inv[:, DFT_F:DFT_F + nf] = (np.cos(ang) * weight).T, (-np.sin(ang) * weight).T
    inv_taps = np.zeros((kwp, 2 * DFT_F))
    inv_taps[:kw] = inv[kw - 1::-1][:kw]
    shift = np.zeros((2 * DFT_F, q), np.float32)
    shift[:nf], shift[DFT_F:DFT_F + nf] = np.cos(ang[:, HALO:HALO + 1]), -np.sin(ang[:, HALO:HALO + 1])
    return {"fwd": _terms(fwd, False), "fwd_halo": _terms(fwd[:, CONV_BLOCK:], False), "shift": shift,
            "inv_out": _terms(inv[HALO:HALO + CONV_BLOCK], False), "inv_in": _terms(inv[:CONV_BLOCK], False),
            "taps": _terms(fwd[:, :kwp], True), "inv_taps": _terms(inv_taps, True)}


def _cmul(a, b, conj_b=False):
    ar, ai, br, bi = a[:DFT_F], a[DFT_F:], b[:DFT_F], b[DFT_F:]
    if conj_b:
        return jnp.concatenate([ar * br + ai * bi, ai * br - ar * bi], axis=0)
    return jnp.concatenate([ar * br - ai * bi, ar * bi + ai * br], axis=0)


def _interleave(sub_tiles):
    waiting, live = list(sub_tiles), []
    while waiting or live:
        if waiting:
            live.append(waiting.pop(0))
        for g in list(live):
            try:
                next(g)
            except StopIteration:
                live.remove(g)


def _start_copies(sems, pairs, first=0):
    copies = [pltpu.make_async_copy(src, dst, sems.at[first + k]) for k, (src, dst) in enumerate(pairs)]
    for cp in copies:
        cp.start()
    return copies


def _cparams():
    return pltpu.CompilerParams(dimension_semantics=("arbitrary",), vmem_limit_bytes=VMEM_LIMIT)


def _full(shape):
    return pl.BlockSpec(shape, lambda i: (0,) * len(shape))


ANY = pl.BlockSpec(memory_space=pl.ANY)

VQ_SGU_G, VQ_SGU_B, VQ_CONV_B, VQ_CLN_G, VQ_CLN_B = range(5)
VD_LN1_G, VD_LN1_B, VD_LN2_G, VD_LN2_B = range(4)
RS_LN1, RS_SGU, RS_CONV = range(3)
RS_COLS = LANES


def _saved_widths(d, q):
    f32 = [d, q, q, q, q, q, RS_COLS]
    bf16 = [d, 2 * q, d, q]
    return f32, bf16


def _saved_views(f32_ref, bf16_ref, d, q):
    views = []
    for ref, widths in zip((f32_ref, bf16_ref), _saved_widths(d, q)):
        for k, w in enumerate(widths):
            views.append(ref.at[pl.ds(0, ref.shape[0]), pl.ds(sum(widths[:k]), w)])
    return views
def _fwd_mix(x, wi, wo, wst, bmat, cwf, tabs, vq, vd, mlp_w, alpha, tm):
    t, d = x.shape
    q = wi.shape[2]
    nc, n_pairs = CONV_BLOCK // CHUNK, q // LANES
    n = t // tm
    n_in, n_saved = 11, 3
    assert tm % CONV_BLOCK == 0

    def body(x_ref, wi_hbm, wo_hbm, wst_ref, bmat_ref, cwf_ref, fwd_ref, taps_ref, inv_ref, vq_ref, vd_ref, *rest):
        f32_ref, bf16_ref, hf_ref = rest[3:3 + n_saved]
        xh_ref, zu_ref, mg_ref, vhat_ref, gv_ref, yhat_ref, rs_ref, xb_ref, pag_ref, y_ref, vnb_ref = _saved_views(f32_ref, bf16_ref, d, q)
        gathered = rest[3 + n_saved:6 + n_saved]
        wi_v, wo_v, hb_ref, gf_ref, send_sems, recv_sems, copy_sems = rest[6 + n_saved:]
        step = pl.program_id(0)

        @pl.when(step == 0)
        def _():
            loads = _start_copies(copy_sems, [(wi_hbm, wi_v), (wo_hbm, wo_v)])
            _Gather(gathered, send_sems, recv_sems).start()
            hb_ref[...] = jnp.zeros_like(hb_ref)
            gf_ref[...] = _dot(taps_ref[...], _split(cwf_ref[...], True))
            for cp in loads:
                cp.wait()

        @pl.when(step == (3 * n) // 4)
        def _():
            _Gather(gathered, send_sems, recv_sems).forward()

        def sub_tile(b):
            rows = slice(b * CONV_BLOCK, (b + 1) * CONV_BLOCK)
            xv = x_ref[rows, :]
            xb = xv.astype(BF16)
            xb_ref[rows, :] = xb
            pu, pv, pa, pg = (_dot(xb, wi_v[j]) for j in range(4))
            yield
            pag_ref[rows, 0:q] = pa.astype(BF16)
            pag_ref[rows, q:2 * q] = pg.astype(BF16)
            zu, gu = _gelu(pu)
            zv, gv = _gelu(pv)
            vhat, rstd_v = _ln_stats(zv)
            vnb = (vhat * vq_ref[VQ_SGU_G:VQ_SGU_G + 1, :] + vq_ref[VQ_SGU_B:VQ_SGU_B + 1, :]).astype(BF16)
            hb_ref[HALO + b * CONV_BLOCK:HALO + (b + 1) * CONV_BLOCK, :] = pa * _sigmoid(pg)
            yield
            mixed = _mix(wst_ref, vnb, nc, n_pairs) + jnp.concatenate([bmat_ref[...]] * nc, axis=0)
            spectrum = _dot(fwd_ref[...], _split(hb_ref[b * CONV_BLOCK:b * CONV_BLOCK + DFT_N, :]))
            yield
            y_ref[rows, 0:q] = (zu * mixed).astype(BF16)
            zu_ref[rows, :] = zu
            mg_ref[rows, :] = mixed * gu
            vhat_ref[rows, :] = vhat
            gv_ref[rows, :] = gv
            vnb_ref[rows, :] = vnb
            hf_ref[b * 2 * DFT_F:(b + 1) * 2 * DFT_F, :] = spectrum
            product = _split(_cmul(gf_ref[...], spectrum))
            yield
            yc = _dot(inv_ref[...], product) + vq_ref[VQ_CONV_B:VQ_CONV_B + 1, :]
            yield
            yhat, rstd_c = _ln_stats(yc)
            yhat_ref[rows, :] = yhat
            yn = yhat * vq_ref[VQ_CLN_G:VQ_CLN_G + 1, :] + vq_ref[VQ_CLN_B:VQ_CLN_B + 1, :]
            y_ref[rows, q:2 * q] = (yn * _sigmoid(yn)).astype(BF16)
            yield
            r1 = alpha * xv + _dot(y_ref[rows, :], wo_v[...])
            yield
            xhat, rstd1 = _ln_stats(r1)
            xh_ref[rows, :] = xhat
            col = lax.broadcasted_iota(jnp.int32, (CONV_BLOCK, RS_COLS), 1)
            rs_ref[rows, :] = jnp.where(col == RS_LN1, rstd1, jnp.where(col == RS_SGU, rstd_v, jnp.where(col == RS_CONV, rstd_c, 0.0)))

        _interleave([sub_tile(b) for b in range(tm // CONV_BLOCK)])
        hb_ref[0:HALO, :] = hb_ref[tm:tm + HALO, :]

        @pl.when(step == n - 1)
        def _():
            _Gather(gathered, send_sems, recv_sems).finish()

    row = lambda w: pl.BlockSpec((tm, w), lambda i: (i, 0))
    widths = [(sum(w), dt) for w, dt in zip(_saved_widths(d, q), (F32, BF16))]
    small_ins = [wst, bmat, cwf, tabs["fwd"], tabs["taps"], tabs["inv_out"], vq, vd]
    return pl.pallas_call(
        body, name="fwd_mix", grid=(n,),
        in_specs=[row(d), ANY, ANY] + [_full(a.shape) for a in small_ins] + [ANY] * 3,
        out_specs=[row(w) for w, _ in widths] + [pl.BlockSpec((tm // CONV_BLOCK * 2 * DFT_F, q), lambda i: (i, 0))] + [ANY] * 3,
        out_shape=[jax.ShapeDtypeStruct((t, w), dt) for w, dt in widths] + [jax.ShapeDtypeStruct((t // CONV_BLOCK * 2 * DFT_F, q), F32)]
        + [jax.ShapeDtypeStruct(b.shape, b.dtype) for b in mlp_w],
        scratch_shapes=[pltpu.VMEM(wi.shape, BF16), pltpu.VMEM(wo.shape, BF16), pltpu.VMEM((HALO + tm, q), F32),
                        pltpu.VMEM((2 * DFT_F, q), F32)] + _gather_sems(3) + [pltpu.SemaphoreType.DMA((2,))],
        input_output_aliases={n_in + a: n_saved + a for a in range(3)},
        compiler_params=_cparams(),
    )(x, wi, wo, *small_ins, *mlp_w)


MLP_SLABS = 4


def _hidden_slabs(f):
    assert f % MXU_N == 0
    tiles = f // MXU_N
    sizes = [(tiles // MLP_SLABS + (1 if j < tiles % MLP_SLABS else 0)) * MXU_N for j in range(MLP_SLABS)]
    return [(sum(sizes[:j]), sz) for j, sz in enumerate(sizes) if sz]


def _fwd_mlp(saved_f32, tgt, wg, wu, wd, vd, alpha, slabs, tm):
    t, d = tgt.shape
    n = t // tm
    ns = len(slabs)
    half = tm // 2 if tm % 32 == 0 else tm

    def body(xh_ref, tgt_ref, wg_hbm, wu_hbm, wd_hbm, vd_ref, *rest):
        gp_refs = [r.at[pl.ds(0, tm), pl.ds(0, sz)] for r, (_, sz) in zip(rest[:ns], slabs)]
        up_refs = [r.at[pl.ds(0, tm), pl.ds(sz, sz)] for r, (_, sz) in zip(rest[:ns], slabs)]
        x1b_ref, dr2_ref, loss_ref, dg2_ref, db2_ref, wg_v, wu_v, wd_v, copy_sems = rest[ns:]

        @pl.when(pl.program_id(0) == 0)
        def _():
            loads = _start_copies(copy_sems, [(wg_hbm, wg_v), (wu_hbm, wu_v), (wd_hbm, wd_v)])
            loss_ref[...] = jnp.zeros_like(loss_ref)
            dg2_ref[...] = jnp.zeros_like(dg2_ref)
            db2_ref[...] = jnp.zeros_like(db2_ref)
            for cp in loads:
                cp.wait()

        g2 = vd_ref[VD_LN2_G:VD_LN2_G + 1, :]

        for r0 in range(0, tm, half):
            rows = slice(r0, r0 + half)
            x1 = xh_ref[rows, :] * vd_ref[VD_LN1_G:VD_LN1_G + 1, :] + vd_ref[VD_LN1_B:VD_LN1_B + 1, :]
            x1b = x1.astype(BF16)
            x1b_ref[rows, :] = x1b
            acc = alpha * x1
            for (off, sz), gp_ref, up_ref in zip(slabs, gp_refs, up_refs):
                gp = _dot_nt(x1b, wg_v[off:off + sz, :])
                up = _dot_nt(x1b, wu_v[off:off + sz, :])
                gp_ref[rows, :] = gp.astype(BF16)
                up_ref[rows, :] = up.astype(BF16)
                acc = acc + _dot((gp * _sigmoid(gp) * up).astype(BF16), wd_v[off:off + sz, :])
            xh2, rstd2 = _ln_stats(acc)
            err = xh2 * g2 + vd_ref[VD_LN2_B:VD_LN2_B + 1, :] - tgt_ref[rows, :]
            loss_ref[...] += _colsum(jnp.sum(err * err, axis=1, keepdims=True)) * (0.5 / d)
            dy = err * (1.0 / d)
            dg2_ref[...] += _colsum(dy * xh2)
            db2_ref[...] += _colsum(dy)
            dr2_ref[rows, :] = _ln_bwd(dy * g2, xh2, rstd2)

    row = lambda w: pl.BlockSpec((tm, w), lambda i: (i, 0))
    act = [2 * sz for _, sz in slabs]
    return pl.pallas_call(
        body, name="fwd_mlp", grid=(n,),
        in_specs=[row(d), row(d), ANY, ANY, ANY, _full(vd.shape)],
        out_specs=[row(sz) for sz in act] + [row(d), row(d), _full((8, LANES)), _full((1, d)), _full((1, d))],
        out_shape=[jax.ShapeDtypeStruct((t, sz), BF16) for sz in act]
        + [jax.ShapeDtypeStruct((t, d), BF16), jax.ShapeDtypeStruct((t, d), F32),
           jax.ShapeDtypeStruct((8, LANES), F32), jax.ShapeDtypeStruct((1, d), F32), jax.ShapeDtypeStruct((1, d), F32)],
        scratch_shapes=[pltpu.VMEM(wg.shape, BF16), pltpu.VMEM(wu.shape, BF16), pltpu.VMEM(wd.shape, BF16), pltpu.SemaphoreType.DMA((3,))],
        compiler_params=_cparams(),
    )(saved_f32, tgt, wg, wu, wd, vd)


def _bwd_mlp_slab(j, slab, dr2, prev, x1b, gate_up, wg, wu, wd, alpha, tm):
    t, d = dr2.shape
    off, sz = slab
    n = t // tm
    first = prev is None

    def body(*refs):
        if first:
            dr_ref, x1b_ref, gu_ref, wg_hbm, wu_hbm, wd_hbm = refs[:6]
        else:
            dr_ref, dxp_ref, x1b_ref, gu_ref, wg_hbm, wu_hbm, wd_hbm = refs[:7]
        dx_ref, dwg_hbm, dwu_hbm, dwd_hbm, dwg16_hbm, dwu16_hbm, dwd16_hbm, ag, au, ad, wg_v, wu_v, wd_v, copy_sems = refs[-14:]

        @pl.when(pl.program_id(0) == 0)
        def _():
            loads = _start_copies(copy_sems, [(src.at[pl.ds(off, sz)], dst) for src, dst in ((wg_hbm, wg_v), (wu_hbm, wu_v), (wd_hbm, wd_v))])
            ag[...] = jnp.zeros_like(ag)
            au[...] = jnp.zeros_like(au)
            ad[...] = jnp.zeros_like(ad)
            for cp in loads:
                cp.wait()

        dr = dr_ref[...]
        drb = dr.astype(BF16)
        x1b = x1b_ref[...]
        gpv = gu_ref[:, 0:sz].astype(F32)
        upv = gu_ref[:, sz:2 * sz].astype(F32)
        dh = _dot_nt(drb, wd_v[...])
        sg = _sigmoid(gpv)
        silu = gpv * sg
        ad[...] += _dot_tn((silu * upv).astype(BF16), drb)
        dgp = (dh * upv * (sg * (1.0 + gpv * (1.0 - sg)))).astype(BF16)
        dup = (dh * silu).astype(BF16)
        ag[...] += _dot_tn(dgp, x1b)
        au[...] += _dot_tn(dup, x1b)
        base = alpha * dr if first else dxp_ref[...]
        dx_ref[...] = base + _dot(dgp, wg_v[...]) + _dot(dup, wu_v[...])

        @pl.when(pl.program_id(0) == n - 1)
        def _():
            rows = pl.ds(off, sz)
            stores = _start_copies(copy_sems, [(ag, dwg_hbm.at[rows]), (au, dwu_hbm.at[rows]), (ad, dwd_hbm.at[rows])])
            for acc, stage in ((ag, wg_v), (au, wu_v), (ad, wd_v)):
                stage[...] = acc[...].astype(BF16)
            stores += _start_copies(copy_sems, [(wg_v, dwg16_hbm.at[rows]), (wu_v, dwu16_hbm.at[rows]), (wd_v, dwd16_hbm.at[rows])], first=3)
            for cp in stores:
                cp.wait()


    row = lambda w: pl.BlockSpec((tm, w), lambda i: (i, 0))
    ins = [dr2] + ([] if first else [prev[0]]) + [x1b, gate_up, wg, wu, wd] + ([] if first else list(prev[1:]))
    in_specs = [row(d)] + ([] if first else [row(d)]) + [row(d), row(2 * sz), ANY, ANY, ANY] + ([] if first else [ANY] * 6)
    return pl.pallas_call(
        body, name=f"bwd_mlp_{j}", grid=(n,),
        in_specs=in_specs,
        out_specs=[row(d)] + [ANY] * 6,
        out_shape=[jax.ShapeDtypeStruct((t, d), F32)] + [jax.ShapeDtypeStruct(wg.shape, F32)] * 3 + [jax.ShapeDtypeStruct(wg.shape, BF16)] * 3,
        scratch_shapes=[pltpu.VMEM((sz, d), F32)] * 3 + [pltpu.VMEM((sz, d), BF16)] * 3 + [pltpu.SemaphoreType.DMA((6,))],
        input_output_aliases={} if first else {7 + a: 1 + a for a in range(6)},
        compiler_params=_cparams(),
    )(*ins)


SMALL_VD = CHUNK
SMALL_CW = CHUNK + 8
SMALL_VQ = CHUNK + 8
SMALL_LOSS = CHUNK + 16
SMALL_BS = CHUNK + 24


def _small_rows(kwp):
    return -(-(SMALL_CW + max(kwp, 24 + SUBLANES)) // 16) * 16


def _bwd_mix(dx1, saved, wi, wo, wstt, cwf, tabs, vq, vd, mlp_small, token, alpha, tm):
    saved_f32, saved_bf16, hf_s = saved
    t, d = dx1.shape
    q = wi.shape[2]
    nc, n_pairs = CONV_BLOCK // CHUNK, q // LANES
    n = t // tm
    nb = tm // CONV_BLOCK
    assert tm % CONV_BLOCK == 0

    def body(dx1_ref, f32_ref, bf16_ref, hf_ref,
             wi_hbm, wo_hbm, wstt_ref, cwf_ref, fwd_ref, fwd_halo_ref, shift_ref, taps_ref, inv_ref, inv_taps_ref, vq_ref, vd_ref,
             loss_ref, dg2_ref, db2_ref, token_ref,
             gx_ref, dwi_hbm, dwo_hbm, small_hbm, dwi16_hbm, dwo16_hbm,
             wi_v, wo_v, awi, awo, dyb_ref, later_ref, dbm_ref, gf_ref, dgf_ref, small_ref, copy_sems):
        xh_ref, zu_ref, mg_ref, vhat_ref, gv_ref, yhat_ref, rs_ref, xb_ref, pag_ref, y_ref, vnb_ref = _saved_views(f32_ref, bf16_ref, d, q)
        i = pl.program_id(0)

        @pl.when(i == 0)
        def _():
            loads = _start_copies(copy_sems, [(wi_hbm, wi_v), (wo_hbm, wo_v)])
            for r in (awi, awo, small_ref, dbm_ref, dgf_ref, dyb_ref, later_ref):
                r[...] = jnp.zeros_like(r)
            gf_ref[...] = _dot(taps_ref[...], _split(cwf_ref[...], True))
            for cp in loads:
                cp.wait()

        dr1b_parts, dproj_parts = [None] * nb, [None] * nb

        def sub_tile(b):
            rows = slice(b * CONV_BLOCK, (b + 1) * CONV_BLOCK)
            dx1v = dx1_ref[rows, :]
            xh = xh_ref[rows, :]
            rsv = rs_ref[rows, :]
            small_ref[SMALL_VD + VD_LN1_G:SMALL_VD + VD_LN1_G + 1, :] += _colsum(dx1v * xh)
            small_ref[SMALL_VD + VD_LN1_B:SMALL_VD + VD_LN1_B + 1, :] += _colsum(dx1v)
            dr1 = _ln_bwd(dx1v * vd_ref[VD_LN1_G:VD_LN1_G + 1, :], xh, rsv[:, RS_LN1:RS_LN1 + 1])
            dr1b = dr1.astype(BF16)
            yield
            dy = _dot_nt(dr1b, wo_v[...])
            yield
            vhat = vhat_ref[rows, :]
            sgu_g = vq_ref[VQ_SGU_G:VQ_SGU_G + 1, :]
            doa = dy[:, 0:q]
            dm = doa * zu_ref[rows, :]
            dpu = (doa * mg_ref[rows, :]).astype(BF16)
            acc = dm[0:CHUNK]
            for c in range(1, nc):
                acc = acc + dm[c * CHUNK:(c + 1) * CHUNK]
            dbm_ref[...] += acc
            pa = pag_ref[rows, 0:q].astype(F32)
            sg = _sigmoid(pag_ref[rows, q:2 * q].astype(F32))
            yhat = yhat_ref[rows, :]
            cln_g = vq_ref[VQ_CLN_G:VQ_CLN_G + 1, :]
            yn = yhat * cln_g + vq_ref[VQ_CLN_B:VQ_CLN_B + 1, :]
            sy = _sigmoid(yn)
            dyn = dy[:, q:2 * q] * (sy * (1.0 + yn * (1.0 - sy)))
            small_ref[SMALL_VQ + VQ_CLN_G:SMALL_VQ + VQ_CLN_G + 1, q:2 * q] += _colsum(dyn * yhat)
            small_ref[SMALL_VQ + VQ_CLN_B:SMALL_VQ + VQ_CLN_B + 1, q:2 * q] += _colsum(dyn)
            dyc = _ln_bwd(dyn * cln_g, yhat, rsv[:, RS_CONV:RS_CONV + 1])
            small_ref[SMALL_VQ + VQ_CONV_B:SMALL_VQ + VQ_CONV_B + 1, q:2 * q] += _colsum(dyc)
            dyb_ref[b, 0:CONV_BLOCK, :] = dyc
            yield
            wgrads = _mix_wgrad(dm, vnb_ref[rows, :], nc, n_pairs)
            dvn = _mix(wstt_ref, dm, nc, n_pairs)
            own = _dot(fwd_ref[...], _split(dyb_ref[b]))
            with_later = own + _dot(fwd_halo_ref[...], _split(later_ref[...]))
            later_ref[...] = dyb_ref[b, 0:HALO, :]
            yield
            for p, g in enumerate(wgrads):
                for half in range(2):
                    small_ref[0:CHUNK, (2 * p + half) * CHUNK:(2 * p + half + 1) * CHUNK] += g[half * CHUNK:(half + 1) * CHUNK]
            small_ref[SMALL_VQ + VQ_SGU_G:SMALL_VQ + VQ_SGU_G + 1, q:2 * q] += _colsum(dvn * vhat)
            small_ref[SMALL_VQ + VQ_SGU_B:SMALL_VQ + VQ_SGU_B + 1, q:2 * q] += _colsum(dvn)
            dpv = (_ln_bwd(dvn * sgu_g, vhat, rsv[:, RS_SGU:RS_SGU + 1]) * gv_ref[rows, :]).astype(BF16)
            dgf_ref[...] += _cmul(_cmul(own, shift_ref[...]), hf_ref[b * 2 * DFT_F:(b + 1) * 2 * DFT_F, :], conj_b=True)
            product = _split(_cmul(with_later, gf_ref[...], conj_b=True))
            yield
            dh = _dot(inv_ref[...], product)
            yield
            da = (dh * sg).astype(BF16)
            dg = (dh * pa * (sg * (1.0 - sg))).astype(BF16)
            yield
            gx = alpha * dr1
            for dpj, wj in zip((dpu, dpv, da, dg), range(4)):
                gx = gx + _dot_nt(dpj, wi_v[wj])
            gx_ref[rows, :] = gx
            dr1b_parts[b], dproj_parts[b] = dr1b, (dpu, dpv, da, dg)

        _interleave([sub_tile(b) for b in reversed(range(nb))])

        awo[...] += _dot_tn(y_ref[...], jnp.concatenate(dr1b_parts, axis=0))
        xb = xb_ref[...]
        for j in range(4):
            awi[j] += _dot_tn(xb, jnp.concatenate([part[j] for part in dproj_parts], axis=0))

        @pl.when(i == n - 1)
        def _():
            stores = _start_copies(copy_sems, [(awi, dwi_hbm), (awo, dwo_hbm)])
            wi_v[...] = awi[...].astype(BF16)
            wo_v[...] = awo[...].astype(BF16)
            stores += _start_copies(copy_sems, [(wi_v, dwi16_hbm), (wo_v, dwo16_hbm)], first=3)
            lane = lax.broadcasted_iota(jnp.int32, (CHUNK, LANES), 1)
            low = lane < HEAD_DIM
            dbs = jnp.zeros((CHUNK, LANES), F32)
            for p in range(n_pairs):
                grp = dbm_ref[:, p * LANES:(p + 1) * LANES]
                dbs = jnp.where(lane == 2 * p, jnp.sum(jnp.where(low, grp, 0.0), axis=1, keepdims=True), dbs)
                dbs = jnp.where(lane == 2 * p + 1, jnp.sum(jnp.where(low, 0.0, grp), axis=1, keepdims=True), dbs)
            tril = lax.broadcasted_iota(jnp.int32, (CHUNK, CHUNK), 0) >= lax.broadcasted_iota(jnp.int32, (CHUNK, CHUNK), 1)
            for h in range(2 * n_pairs):
                block = small_ref[0:CHUNK, h * CHUNK:(h + 1) * CHUNK]
                small_ref[0:CHUNK, h * CHUNK:(h + 1) * CHUNK] = jnp.where(tril, block, 0.0)
            small_ref[SMALL_VD + VD_LN2_G:SMALL_VD + VD_LN2_G + 1, :] = dg2_ref[...]
            small_ref[SMALL_VD + VD_LN2_B:SMALL_VD + VD_LN2_B + 1, :] = db2_ref[...]
            small_ref[SMALL_CW:SMALL_CW + kwp, 0:q] = _dot(inv_taps_ref[...], _split(dgf_ref[...], True))
            small_ref[SMALL_LOSS:SMALL_LOSS + SUBLANES, q:q + LANES] = loss_ref[...]
            small_ref[SMALL_BS:SMALL_BS + SUBLANES, q:q + LANES] = jnp.transpose(dbs)[0:SUBLANES]
            stores += _start_copies(copy_sems, [(small_ref, small_hbm)], first=2)
            for cp in stores:
                cp.wait()

    rev = lambda w: pl.BlockSpec((tm, w), lambda i: (n - 1 - i, 0))
    kwp = cwf.shape[0]
    small = jax.ShapeDtypeStruct((_small_rows(kwp), 2 * q), F32)
    small_ins = [wstt, cwf, tabs["fwd"], tabs["fwd_halo"], tabs["shift"], tabs["taps"], tabs["inv_in"], tabs["inv_taps"], vq, vd,
                 *mlp_small]
    return pl.pallas_call(
        body, name="bwd_mix", grid=(n,),
        in_specs=[rev(d), rev(saved_f32.shape[1]), rev(saved_bf16.shape[1]),
                  pl.BlockSpec((nb * 2 * DFT_F, q), lambda i: (n - 1 - i, 0)), ANY, ANY] + [_full(a.shape) for a in small_ins] + [ANY],
        out_specs=[rev(d)] + [ANY] * 5,
        out_shape=[jax.ShapeDtypeStruct((t, d), F32), jax.ShapeDtypeStruct(wi.shape, F32), jax.ShapeDtypeStruct(wo.shape, F32), small,
                   jax.ShapeDtypeStruct(wi.shape, BF16), jax.ShapeDtypeStruct(wo.shape, BF16)],
        scratch_shapes=[pltpu.VMEM(wi.shape, BF16), pltpu.VMEM(wo.shape, BF16), pltpu.VMEM(wi.shape, F32), pltpu.VMEM(wo.shape, F32),
                        pltpu.VMEM((nb, DFT_N, q), F32), pltpu.VMEM((HALO, q), F32),
                        pltpu.VMEM((CHUNK, q), F32), pltpu.VMEM((2 * DFT_F, q), F32), pltpu.VMEM((2 * DFT_F, q), F32),
                        pltpu.VMEM(small.shape, F32), pltpu.SemaphoreType.DMA((5,))],
        compiler_params=_cparams(),
    )(dx1, saved_f32, saved_bf16, hf_s, wi, wo, *small_ins, token)


def _prep(me_arr, w_in, w_out, w_gate_t, w_up_t, w_down, conv_w, kwp):
    kw, cshard = conv_w.shape

    def body(me_ref, wi_ref, wo_ref, wg_ref, wu_ref, wd_ref, cw_ref, oi, oo, og, ou, od, oc):
        for src, dst in ((wi_ref, oi), (wo_ref, oo), (wg_ref, og), (wu_ref, ou), (wd_ref, od)):
            dst[...] = src[...].astype(BF16)
        oc[0:kw, :] = cw_ref[...]
        oc[kw:kwp, :] = jnp.zeros((kwp - kw, cshard), F32)

    ins = (w_in, w_out, w_gate_t, w_up_t, w_down, conv_w)
    outs = [jax.ShapeDtypeStruct((4,) + a.shape, BF16) for a in ins[:5]] + [jax.ShapeDtypeStruct((4, kwp, cshard), F32)]
    grid_spec = pltpu.PrefetchScalarGridSpec(
        num_scalar_prefetch=1, grid=(1,),
        in_specs=[pl.BlockSpec(a.shape, lambda i, me: (0, 0)) for a in ins],
        out_specs=[pl.BlockSpec((None,) + o.shape[1:], lambda i, me: (me[0], 0, 0)) for o in outs])
    return pl.pallas_call(body, name="wprep", grid_spec=grid_spec, out_shape=outs, compiler_params=_cparams())(me_arr, *ins)


def _coords():
    return tuple(lax.axis_index(a) for a in MESH_AXES)


def _other_chips(x, y):
    return [(1 - x, y), (x, 1 - y), (1 - x, 1 - y)]


def _remote(src, dst, send_sem, recv_sem, to):
    return pltpu.make_async_remote_copy(src_ref=src, dst_ref=dst, send_sem=send_sem, recv_sem=recv_sem,
                                        device_id=to, device_id_type=MESH_ID)


def _hbm_call(body, name, ins, out_shape, scratch_shapes, aliases=None):
    return pl.pallas_call(
        body, name=name, in_specs=[ANY] * len(ins), out_specs=[ANY] * len(out_shape), out_shape=out_shape,
        scratch_shapes=scratch_shapes, input_output_aliases=aliases or {},
    )(*ins)


class _Gather:
    def __init__(self, bufs, send_sems, recv_sems):
        self.bufs, self.send_sems, self.recv_sems = bufs, send_sems, recv_sems
        self.x, self.y, self.c = _coords()

    def _copies(self, stage):
        x, y, c = self.x, self.y, self.c
        for a, buf in enumerate(self.bufs):
            hr = buf.shape[1] // 2
            for j, chip in enumerate(_other_chips(x, y)):
                if stage == "ici_out":
                    ref, k, to = buf.at[2 * x + y, pl.ds(c * hr, hr)], j, (*chip, c)
                elif stage == "ici_in":
                    ref, k, to = buf.at[2 * chip[0] + chip[1], pl.ds(c * hr, hr)], j, (*chip, c)
                elif stage == "d2d_out":
                    ref, k, to = buf.at[2 * chip[0] + chip[1], pl.ds(c * hr, hr)], 3 + j, (x, y, 1 - c)
                else:
                    ref, k, to = buf.at[2 * chip[0] + chip[1], pl.ds((1 - c) * hr, hr)], 3 + j, (x, y, 1 - c)
                yield _remote(ref, ref, self.send_sems.at[a, k], self.recv_sems.at[a, k], to)

    def start(self):
        for cp in self._copies("ici_out"):
            cp.start()

    def forward(self):
        for landed, onward in zip(self._copies("ici_in"), self._copies("d2d_out")):
            landed.wait_recv()
            onward.start()

    def finish(self):
        for cp in self._copies("d2d_in"):
            cp.wait_recv()
        for stage in ("ici_out", "d2d_out"):
            for cp in self._copies(stage):
                cp.wait_send()


def _gather_sems(n):
    return [pltpu.SemaphoreType.DMA((n, 6)), pltpu.SemaphoreType.DMA((n, 6))]


def _gather_shards(bufs):
    n = len(bufs)

    def body(*refs):
        g = _Gather(refs[n:2 * n], *refs[2 * n:])
        g.start()
        g.forward()
        g.finish()

    return _hbm_call(body, "gather_shards", bufs, [jax.ShapeDtypeStruct(s.shape, s.dtype) for s in bufs],
                     _gather_sems(n), aliases={a: a for a in range(n)})


def _pair_swap(name, arrs):
    n = len(arrs)

    def body(*refs):
        src, land = refs[:n], refs[n:2 * n]
        send_sems, recv_sems = refs[2 * n:]
        x, y, c = _coords()
        copies = []
        for a in range(n):
            s = src[a].at[pl.ds(0, arrs[a].shape[0]), 1 - c] if arrs[a].ndim == 4 else src[a].at[1 - c]
            copies.append(_remote(s, land[a], send_sems.at[a], recv_sems.at[a], (x, y, 1 - c)))
            copies[-1].start()
        for cp in copies:
            cp.wait()

    outs = [jax.ShapeDtypeStruct(s.shape[:-3] + s.shape[-2:], s.dtype) for s in arrs]
    return _hbm_call(body, name, arrs, outs, [pltpu.SemaphoreType.DMA((n,)), pltpu.SemaphoreType.DMA((n,))])


class _Exchange:
    def __init__(self, src, dst, send_sems, recv_sems):
        self.src, self.dst, self.send_sems, self.recv_sems = src, dst, send_sems, recv_sems
        self.x, self.y, self.c = _coords()

    def _copies(self, incoming):
        x, y, c = self.x, self.y, self.c
        for a, (s, d) in enumerate(zip(self.src, self.dst)):
            for j, chip in enumerate(_other_chips(x, y)):
                slot = 2 * chip[0] + chip[1]
                if incoming:
                    out, into = d.at[slot], d.at[slot]
                else:
                    out, into = (s.at[slot] if len(s.shape) == 3 else s), d.at[2 * x + y]
                yield _remote(out, into, self.send_sems.at[a, j], self.recv_sems.at[a, j], (*chip, c))

    def start(self):
        for cp in self._copies(False):
            cp.start()

    def finish(self):
        for cp in self._copies(True):
            cp.wait_recv()
        for cp in self._copies(False):
            cp.wait_send()


def _exchange_shapes(arrs):
    return [jax.ShapeDtypeStruct((4,) + s.shape[-2:], s.dtype) for s in arrs]


class _FlatSems:
    def __init__(self, ref):
        self.ref = ref

    @property
    def at(self):
        return self

    def __getitem__(self, idx):
        return self.ref.at[3 * idx[0] + idx[1]]


HBM = pl.BlockSpec(memory_space=pltpu.HBM)
SEM = pl.BlockSpec(memory_space=pltpu.SEMAPHORE)
DATAFLOW = pltpu.SideEffectType.DATAFLOW_SIDE_EFFECTING


def _exchange_start(name, arrs):
    n = len(arrs)
    lands = _exchange_shapes(arrs)

    def body(*refs):
        src, land = refs[:n], refs[n:2 * n]
        send_sems, recv_sems = refs[2 * n:2 * n + 2]
        token = refs[-1]
        _Exchange(src, land, _FlatSems(send_sems), _FlatSems(recv_sems)).start()
        token[...] = jnp.zeros_like(token)

    hbm = lambda a: pltpu.with_memory_space_constraint(a, pltpu.HBM)
    outs = pl.pallas_call(
        body, name=name,
        out_shape=(pltpu.SemaphoreType.DMA((3 * n,)), pltpu.SemaphoreType.DMA((3 * n,)),
                   *[pltpu.HBM(a.shape, a.dtype) for a in arrs], *[pltpu.HBM(s.shape, s.dtype) for s in lands],
                   jax.ShapeDtypeStruct((SUBLANES, LANES), F32)),
        in_specs=[HBM] * (2 * n), out_specs=(SEM, SEM, *[HBM] * (2 * n), pl.BlockSpec(memory_space=pltpu.VMEM)),
        input_output_aliases={a: 2 + a for a in range(2 * n)},
        compiler_params=pltpu.CompilerParams(has_side_effects=DATAFLOW),
    )(*[hbm(a) for a in arrs], *[hbm(lax.empty(s.shape, s.dtype)) for s in lands])
    return outs[:-1], outs[-1]


def _exchange_wait(name, started, after):
    send_sems, recv_sems, *bufs = started
    n = len(bufs) // 2

    def body(*refs):
        src, land = refs[:n], refs[n:2 * n]
        send_sems, recv_sems = refs[2 * n:2 * n + 2]
        _Exchange(src, land, _FlatSems(send_sems), _FlatSems(recv_sems)).finish()

    outs = pl.pallas_call(
        body, name=name,
        out_shape=tuple(pltpu.HBM(b.shape, b.dtype) for b in bufs),
        in_specs=[HBM] * (2 * n) + [SEM, SEM] + [ANY] * len(after), out_specs=tuple([HBM] * (2 * n)),
        input_output_aliases={a: a for a in range(2 * n)},
        compiler_params=pltpu.CompilerParams(has_side_effects=DATAFLOW),
    )(*bufs, send_sems, recv_sems, *after)
    return list(outs[:n]), list(outs[n:])


def _pair_gather(name, halves):
    n = len(halves)

    def body(*refs):
        src, dst = refs[:n], refs[n:2 * n]
        send_sems, recv_sems = refs[2 * n:]
        x, y, c = _coords()
        copies = [_remote(src[a], dst[a], send_sems.at[a], recv_sems.at[a], (x, y, 1 - c)) for a in range(n)]
        for cp in copies:
            cp.start()
        for cp in copies:
            cp.wait()

    outs = [jax.ShapeDtypeStruct(s.shape, s.dtype) for s in halves]
    return _hbm_call(body, name, halves, outs, [pltpu.SemaphoreType.DMA((n,)), pltpu.SemaphoreType.DMA((n,))])


def _pair_sum(a, g, land, c_arr, out_dtype):
    nq, _, hr, cc = g.shape

    def body(c_ref, g_ref, l_ref, o_ref):
        o_ref[...] = (g_ref[...] + l_ref[...].astype(F32)).astype(out_dtype)

    spec = pl.BlockSpec((None, hr, cc), lambda qi, cr: (qi, 0, 0))
    grid_spec = pltpu.PrefetchScalarGridSpec(
        num_scalar_prefetch=1, grid=(nq,),
        in_specs=[pl.BlockSpec((None, None, hr, cc), lambda qi, cr: (qi, cr[0], 0, 0)), spec], out_specs=spec)
    return pl.pallas_call(body, name=f"pair_sum_{a}", grid_spec=grid_spec, out_shape=jax.ShapeDtypeStruct((nq, hr, cc), out_dtype),
                          compiler_params=_cparams())(c_arr, g, land)


def _chip_sum(a, parts, own, me_arr, after):
    _, hr, cc = parts.shape

    def body(me_ref, p_ref, own_ref, after_ref, o_ref):
        for mine in range(4):
            @pl.when(me_ref[0] == mine)
            def _():
                term = lambda j: (own_ref if j == mine else p_ref.at[j])[...].astype(F32)
                o_ref[...] = ((term(0) + term(1)) + term(2)) + term(3)

    own_spec = (pl.BlockSpec((None, hr, cc), lambda i, me: (me[0], 0, 0)) if own.ndim == 3
                else pl.BlockSpec((hr, cc), lambda i, me: (0, 0)))
    grid_spec = pltpu.PrefetchScalarGridSpec(
        num_scalar_prefetch=1, grid=(1,),
        in_specs=[pl.BlockSpec((4, hr, cc), lambda i, me: (0, 0, 0)), own_spec, ANY],
        out_specs=pl.BlockSpec((hr, cc), lambda i, me: (0, 0)))
    return pl.pallas_call(body, name=f"chip_sum_{a}", grid_spec=grid_spec, out_shape=jax.ShapeDtypeStruct((hr, cc), F32),
                          compiler_params=_cparams())(me_arr, parts, own, after)


def _row_block(rows, cols, limit=1 << 20):
    best = 8
    for tr in range(8, rows + 1, 8):
        if rows % tr == 0 and tr * cols * 4 <= limit:
            best = tr
    return best


def _adamw(name, w, g_mine, g_other, m, v, c_arr):
    r, c = w.shape
    hr, cg = g_mine.shape
    tr = hr if r % hr == 0 and hr * cg * 4 <= (3 << 19) else math.gcd(_row_block(hr, cg), r)
    per_half = hr // tr
    bc1 = 1.0 - ADAM_B1 ** ADAM_STEP
    bc2 = 1.0 - ADAM_B2 ** ADAM_STEP

    def body(c_ref, w_ref, gm_ref, go_ref, m_ref, v_ref, go, do, mo, vo):
        gv = jnp.where(pl.program_id(0) // per_half == c_ref[0], gm_ref[:, 0:c], go_ref[:, 0:c])
        mn = ADAM_B1 * m_ref[...] + (1.0 - ADAM_B1) * gv
        vn = ADAM_B2 * v_ref[...] + (1.0 - ADAM_B2) * (gv * gv)
        go[...] = gv
        mo[...] = mn
        vo[...] = vn
        do[...] = -ADAM_LR * ((mn / bc1) / (jnp.sqrt(vn / bc2) + ADAM_EPS) + ADAM_WD * w_ref[...])

    blk = pl.BlockSpec((tr, c), lambda i, cr: (i, 0))
    gblk = pl.BlockSpec((tr, cg), lambda i, cr: (i % per_half, 0))
    grid_spec = pltpu.PrefetchScalarGridSpec(num_scalar_prefetch=1, grid=(r // tr,), in_specs=[blk, gblk, gblk, blk, blk],
                                             out_specs=[blk] * 4)
    return pl.pallas_call(body, name=f"adamw_{name}", grid_spec=grid_spec, out_shape=[jax.ShapeDtypeStruct((r, c), F32)] * 4,
                          compiler_params=_cparams())(c_arr, w, g_mine, g_other, m, v)


SMALL_Q = ("sgu_ln_g", "sgu_ln_b", "conv_b", "conv_ln_g", "conv_ln_b")
SMALL_D = ("ln1_g", "ln1_b", "ln2_g", "ln2_b")


def _adamw_small(g_mine, g_other, c_arr, me_arr, params):
    names = list(SMALL_Q) + list(SMALL_D) + ["w_s", "b_s", "conv_w"]
    hr, width = g_mine.shape
    q = width // 2
    heads = params["w_s"][0].shape[1]
    kw, cshard = params["conv_w"][0].shape[1:]
    bc1 = 1.0 - ADAM_B1 ** ADAM_STEP
    bc2 = 1.0 - ADAM_B2 ** ADAM_STEP

    def update(w, g, m, v):
        mn = ADAM_B1 * m + (1.0 - ADAM_B1) * g
        vn = ADAM_B2 * v + (1.0 - ADAM_B2) * (g * g)
        return g, -ADAM_LR * ((mn / bc1) / (jnp.sqrt(vn / bc2) + ADAM_EPS) + ADAM_WD * w), mn, vn

    def body(c_ref, me_ref, gm_ref, go_ref, *refs):
        ins = {nm: refs[3 * k:3 * k + 3] for k, nm in enumerate(names)}
        outs = {nm: refs[3 * len(names) + 4 * k:3 * len(names) + 4 * k + 4] for k, nm in enumerate(names)}
        loss_ref, cw_ref = refs[-2:]
        first, second = gm_ref[...], go_ref[...]
        low = c_ref[0] == 0
        g_all = jnp.concatenate([jnp.where(low, first, second), jnp.where(low, second, first)], axis=0)

        def apply(nm, g, at):
            w, m, v = (r[at] for r in ins[nm])
            for o, val in zip(outs[nm], update(w, g, m, v)):
                o[at] = val

        for row, nm in enumerate(SMALL_Q):
            apply(nm, g_all[SMALL_VQ + row:SMALL_VQ + row + 1, q:2 * q], ...)
        for row, nm in enumerate(SMALL_D):
            apply(nm, g_all[SMALL_VD + row:SMALL_VD + row + 1, :], ...)
        for h in range(heads):
            apply("w_s", g_all[0:CHUNK, h * CHUNK:(h + 1) * CHUNK], (0, h))
        apply("b_s", g_all[SMALL_BS:SMALL_BS + heads, q:q + LANES], 0)
        cw_ref[...] = jnp.zeros_like(cw_ref)
        for chip in range(4):
            @pl.when(me_ref[0] == chip)
            def _():
                cw_ref[...] = g_all[SMALL_CW:SMALL_CW + cw_ref.shape[0], chip * cshard:(chip + 1) * cshard]
        apply("conv_w", cw_ref[0:kw, :], 0)
        loss_ref[...] = g_all[SMALL_LOSS:SMALL_LOSS + SUBLANES, q:q + LANES]

    arrays = [a for nm in names for a in params[nm]]
    out_shape = [jax.ShapeDtypeStruct(params[nm][0].shape, F32) for nm in names for _ in range(4)] + [jax.ShapeDtypeStruct((SUBLANES, LANES), F32)]
    whole = lambda shape: pl.BlockSpec(shape, lambda i, c, me: (0,) * len(shape))
    grid_spec = pltpu.PrefetchScalarGridSpec(
        num_scalar_prefetch=2, grid=(1,),
        in_specs=[whole(g_mine.shape), whole(g_other.shape)] + [whole(a.shape) for a in arrays],
        out_specs=[whole(s.shape) for s in out_shape],
        scratch_shapes=[pltpu.VMEM((-(-kw // SUBLANES) * SUBLANES, cshard), F32)])
    res = pl.pallas_call(body, name="adamw_small", grid_spec=grid_spec, out_shape=out_shape, compiler_params=_cparams())(
        c_arr, me_arr, g_mine, g_other, *arrays)
    return {nm: list(res[4 * k:4 * k + 4]) for k, nm in enumerate(names)}, res[-1]


def _pad_rows(a, rows):
    return jnp.pad(a, ((0, rows - a.shape[0]), (0, 0)))


def kernel(x, w_in, sgu_ln_g, sgu_ln_b, w_s, b_s, conv_w, conv_b, conv_ln_g, conv_ln_b, w_out, ln1_g, ln1_b, w_gate, w_up, w_down, ln2_g, ln2_b, loss_target, m_w_in, m_sgu_ln_g, m_sgu_ln_b, m_w_s, m_b_s, m_conv_w, m_conv_b, m_conv_ln_g, m_conv_ln_b, m_w_out, m_ln1_g, m_ln1_b, m_w_gate, m_w_up, m_w_down, m_ln2_g, m_ln2_b, v_w_in, v_sgu_ln_g, v_sgu_ln_b, v_w_s, v_b_s, v_conv_w, v_conv_b, v_conv_ln_g, v_conv_ln_b, v_w_out, v_ln1_g, v_ln1_b, v_w_gate, v_w_up, v_w_down, v_ln2_g, v_ln2_b):
    depth, d, q = w_in.shape
    assert depth == 1 and x.shape[0] == 1
    t = x.shape[1]
    heads = w_s.shape[1]
    kw, cshard = conv_w.shape[1], conv_w.shape[2]
    fs = w_gate.shape[2]
    slabs = _hidden_slabs(4 * fs)
    n_pairs = q // LANES
    assert heads * HEAD_DIM == q and q % LANES == 0 and w_s.shape[2] == CHUNK and 4 * cshard == q and kw - 1 <= HALO
    alpha = (2.0 * depth) ** 0.25
    tm = min(512, t)
    assert t % tm == 0 and tm % CHUNK == 0
    x2, tgt = x[0], loss_target[0]
    mx, my, mc = _coords()
    me = 2 * mx + my
    c_arr = jnp.reshape(mc, (1,)).astype(jnp.int32)

    kwp = -(-kw // 16) * 16
    me_arr = jnp.reshape(me, (1,)).astype(jnp.int32)
    wi, wo, wg, wu, wd, cw4 = _prep(me_arr, w_in[0], w_out[0], w_gate[0].T, w_up[0].T, w_down[0], conv_w[0], kwp)
    wi, wo, cw4 = _gather_shards([wi, wo, cw4])
    wo = wo.reshape(d, d)
    cw = jnp.transpose(cw4, (1, 0, 2)).reshape(kwp, q)
    cwf = _pad_rows(cw[:kw][::-1], kwp)
    tabs = {name: jnp.asarray(tab) for name, tab in _dft_tables(kw, kwp, q).items()}

    wm = jnp.where(jnp.tril(jnp.ones((CHUNK, CHUNK), bool)), w_s[0], 0.0)
    wst = wm.reshape(n_pairs, 2 * CHUNK, CHUNK).astype(BF16)
    wstt = jnp.transpose(wm, (0, 2, 1)).reshape(n_pairs, 2 * CHUNK, CHUNK).astype(BF16)
    bmat = jnp.repeat(b_s[0].T, HEAD_DIM, axis=1)
    vq = _pad_rows(jnp.concatenate([sgu_ln_g, sgu_ln_b, conv_b, conv_ln_g, conv_ln_b], axis=0), 8)
    vd = _pad_rows(jnp.concatenate([ln1_g, ln1_b, ln2_g, ln2_b], axis=0), 8)

    *saved, wg, wu, wd = _fwd_mix(x2, wi, wo, wst, bmat, cwf, tabs, vq, vd, [wg, wu, wd], alpha, tm)
    wg, wu, wd = (w.reshape(4 * fs, d) for w in (wg, wu, wd))
    *acts, x1b, dr2, loss_part, dg2, db2 = _fwd_mlp(saved[0], tgt, wg, wu, wd, vd, alpha, slabs, tm)
    mlp_grads = None
    for j, slab in enumerate(slabs):
        mlp_grads = _bwd_mlp_slab(j, slab, dr2, mlp_grads, x1b, acts[j], wg, wu, wd, alpha, tm)
    dx1 = mlp_grads[0]
    mlp_halves = [b.reshape(4, 2, fs // 2, d) for b in mlp_grads[1:]]
    landed = _pair_swap("pair_swap_mlp", mlp_halves[3:])
    mlp_sums = [_pair_sum(f"mlp{a}", h, l, c_arr, BF16) for a, (h, l) in enumerate(zip(mlp_halves[:3], landed))]
    mlp_started, token = _exchange_start("exchange_mlp_start", mlp_sums)
    grad_x, dwi, dwo, small, dwi16, dwo16 = _bwd_mix(dx1, saved, wi, wo, wstt, cwf, tabs, vq, vd, (loss_part, dg2, db2), token, alpha, tm)
    mlp_sums, mlp_parts = _exchange_wait("exchange_mlp_wait", mlp_started, [dwo])

    by_halves = lambda b: b.reshape(4, 2, b.shape[1] // 2, b.shape[2])
    halves = [by_halves(dwi), by_halves(dwo.reshape(4, d // 4, d)), small.reshape(2, small.shape[0] // 2, small.shape[1])]
    landed = _pair_swap("pair_swap_mix", [by_halves(dwi16), by_halves(dwo16.reshape(4, d // 4, d)), halves[-1]])
    sums = [_pair_sum(f"mix{a}", h, l, c_arr, BF16) for a, (h, l) in enumerate(zip(halves[:-1], landed[:-1]))]
    sums.append(_pair_sum("small", halves[-1][None], landed[-1][None], c_arr, F32)[0])
    mix_started, token = _exchange_start("exchange_mix_start", sums)

    out, raw = {}, {}

    def finish(first, names, parts, sums, after):
        mine = [_chip_sum(first + a, p, s, me_arr, after) for a, (p, s) in enumerate(zip(parts, sums))]
        other = _pair_gather(f"pair_gather_{first}", mine)
        for a, nm in enumerate(names):
            w_, m_, v_ = weights[nm]
            if nm in ("w_gate", "w_up"):
                raw[nm] = _adamw(nm, w_[0].T, mine[a], other[a], m_[0].T, v_[0].T, c_arr)
                out[nm] = [o.T for o in raw[nm]]
            else:
                raw[nm] = out[nm] = _adamw(nm, w_[0], mine[a], other[a], m_[0], v_[0], c_arr)
        return mine[-1], other[-1]

    weights = {"w_in": (w_in, m_w_in, v_w_in), "w_out": (w_out, m_w_out, v_w_out), "w_gate": (w_gate, m_w_gate, v_w_gate),
               "w_up": (w_up, m_w_up, v_w_up), "w_down": (w_down, m_w_down, v_w_down)}
    finish(2, ["w_gate", "w_up", "w_down"], mlp_parts, mlp_sums, token)
    sums, parts = _exchange_wait("exchange_mix_wait", mix_started, [raw[nm][1] for nm in ("w_gate", "w_up", "w_down")])
    small_mine, small_other = finish(5, ["w_in", "w_out"], parts, sums, parts[0])

    small_params = {
        "sgu_ln_g": (sgu_ln_g, m_sgu_ln_g, v_sgu_ln_g), "sgu_ln_b": (sgu_ln_b, m_sgu_ln_b, v_sgu_ln_b),
        "conv_b": (conv_b, m_conv_b, v_conv_b), "conv_ln_g": (conv_ln_g, m_conv_ln_g, v_conv_ln_g),
        "conv_ln_b": (conv_ln_b, m_conv_ln_b, v_conv_ln_b), "ln1_g": (ln1_g, m_ln1_g, v_ln1_g), "ln1_b": (ln1_b, m_ln1_b, v_ln1_b),
        "ln2_g": (ln2_g, m_ln2_g, v_ln2_g), "ln2_b": (ln2_b, m_ln2_b, v_ln2_b), "w_s": (w_s, m_w_s, v_w_s),
        "b_s": (b_s, m_b_s, v_b_s), "conv_w": (conv_w, m_conv_w, v_conv_w)}
    small_out, loss_block = _adamw_small(small_mine, small_other, c_arr, me_arr, small_params)
    loss = loss_block[0, 0]
    names = ["w_in", "sgu_ln_g", "sgu_ln_b", "w_s", "b_s", "conv_w", "conv_b", "conv_ln_g", "conv_ln_b", "w_out",
             "ln1_g", "ln1_b", "w_gate", "w_up", "w_down", "ln2_g", "ln2_b"]
    result = [loss, grad_x[None]]
    for kind in range(4):
        for nm in names:
            result.append(out[nm][kind][None] if nm in out else small_out[nm][kind])
    return tuple(result)
```

```python
import math

import jax
import numpy as np
import jax.numpy as jnp
from jax import lax
from jax.experimental import pallas as pl
from jax.experimental.pallas import tpu as pltpu

F32 = jnp.float32
BF16 = jnp.bfloat16

LN_EPS = 1e-5
HEAD_DIM = 64
CHUNK = 128
HALO = 32
LANES = 128
MXU_N = 256
ADAM_LR, ADAM_B1, ADAM_B2, ADAM_EPS, ADAM_WD, ADAM_STEP = 0.001, 0.9, 0.999, 1e-08, 0.01, 10
VMEM_LIMIT = 63 * 1024 * 1024
MESH_AXES = ("x", "y", "c")
MESH_ID = pl.DeviceIdType.MESH


def _dot(a, b):
    return jnp.dot(a, b, preferred_element_type=F32)


def _dot_nt(a, b):
    return lax.dot_general(a, b, (((1,), (1,)), ((), ())), preferred_element_type=F32)


def _dot_tn(a, b):
    return lax.dot_general(a, b, (((0,), (0,)), ((), ())), preferred_element_type=F32)


def _sigmoid(v):
    return 1.0 / (1.0 + jnp.exp(-v))


def _gelu(v):
    cdf = 0.5 * (1.0 + lax.erf(v * (1.0 / math.sqrt(2.0))))
    pdf = jnp.exp(-0.5 * v * v) * (1.0 / math.sqrt(2.0 * math.pi))
    return v * cdf, cdf + v * pdf


def _ln_stats(v):
    mu = jnp.mean(v, axis=-1, keepdims=True)
    d = v - mu
    rstd = lax.rsqrt(jnp.mean(d * d, axis=-1, keepdims=True) + LN_EPS)
    return d * rstd, rstd


def _ln_bwd(dxhat, xhat, rstd):
    m1 = jnp.mean(dxhat, axis=-1, keepdims=True)
    m2 = jnp.mean(dxhat * xhat, axis=-1, keepdims=True)
    return rstd * (dxhat - m1 - xhat * m2)


def _colsum(v):
    return jnp.sum(v, axis=0, keepdims=True)


def _pair_lanes(v, nc, p):
    return jnp.concatenate([v[c * CHUNK:(c + 1) * CHUNK, p * LANES:(p + 1) * LANES] for c in range(nc)], axis=1)


def _unpair(parts, nc):
    rows = [jnp.concatenate([part[:, c * LANES:(c + 1) * LANES] for part in parts], axis=1) for c in range(nc)]
    return jnp.concatenate(rows, axis=0)


def _low_head(nc):
    lane = lax.broadcasted_iota(jnp.int32, (CHUNK, nc * LANES), 1)
    return (lane & (LANES - 1)) < HEAD_DIM


def _mix(wst_ref, v, nc, n_pairs):
    vb = v.astype(BF16)
    low = _low_head(nc)
    parts = []
    for p in range(n_pairs):
        r = _dot(wst_ref[p], _pair_lanes(vb, nc, p))
        parts.append(jnp.where(low, r[:CHUNK], r[CHUNK:]))
    return _unpair(parts, nc)


def _mix_wgrad(dm, vn, nc, n_pairs):
    low = _low_head(nc)
    vb = vn.astype(BF16)
    out = []
    for p in range(n_pairs):
        a = _pair_lanes(dm, nc, p)
        lhs = jnp.concatenate([jnp.where(low, a, 0.0), jnp.where(low, 0.0, a)], axis=0).astype(BF16)
        out.append(_dot_nt(lhs, _pair_lanes(vb, nc, p)))
    return out


SUBLANES = 8


CONV_BLOCK = 256
DFT_N = CONV_BLOCK + HALO
DFT_F = -(-(DFT_N // 2 + 1) // SUBLANES) * SUBLANES


def _terms(m, exact):
    hi = m.astype(np.float32).astype(BF16)
    lo = (m.astype(np.float32) - hi.astype(np.float32)).astype(BF16)
    return np.concatenate([hi, hi, lo] if exact else [hi], axis=1)


def _split(v, exact=False):
    hi = v.astype(BF16)
    if not exact:
        return hi
    lo = (v - hi.astype(F32)).astype(BF16)
    return jnp.concatenate([hi, lo, hi], axis=0)


def _dft_tables(kw, kwp, q):
    nf = DFT_N // 2 + 1
    ang = 2.0 * np.pi * np.arange(nf)[:, None] * np.arange(DFT_N)[None, :] / DFT_N
    fwd = np.zeros((2 * DFT_F, DFT_N))
    fwd[:nf], fwd[DFT_F:DFT_F + nf] = np.cos(ang), -np.sin(ang)
    weight = np.full((nf, 1), 2.0 / DFT_N)
    weight[0] = weight[-1] = 1.0 / DFT_N
    inv = np.zeros((DFT_N, 2 * DFT_F))
    inv[:, :nf], inv[:, DFT_F:DFT_F + nf] = (np.cos(ang) * weight).T, (-np.sin(ang) * weight).T
    inv_taps = np.zeros((kwp, 2 * DFT_F))
    inv_taps[:kw] = inv[kw - 1::-1][:kw]
    shift = np.zeros((2 * DFT_F, q), np.float32)
    shift[:nf], shift[DFT_F:DFT_F + nf] = np.cos(ang[:, HALO:HALO + 1]), -np.sin(ang[:, HALO:HALO + 1])
    return {"fwd": _terms(fwd, False), "fwd_halo": _terms(fwd[:, CONV_BLOCK:], False), "shift": shift,
            "inv_out": _terms(inv[HALO:HALO + CONV_BLOCK], False), "inv_in": _terms(inv[:CONV_BLOCK], False),
            "taps": _terms(fwd[:, :kwp], True), "inv_taps": _terms(inv_taps, True)}


def _cmul(a, b, conj_b=False):
    ar, ai, br, bi = a[:DFT_F], a[DFT_F:], b[:DFT_F], b[DFT_F:]
    if conj_b:
        return jnp.concatenate([ar * br + ai * bi, ai * br - ar * bi], axis=0)
    return jnp.concatenate([ar * br - ai * bi, ar * bi + ai * br], axis=0)


def _interleave(sub_tiles):
    waiting, live = list(sub_tiles), []
    while waiting or live:
        if waiting:
            live.append(waiting.pop(0))
        for g in list(live):
            try:
                next(g)
            except StopIteration:
                live.remove(g)


def _start_copies(sems, pairs, first=0):
    copies = [pltpu.make_async_copy(src, dst, sems.at[first + k]) for k, (src, dst) in enumerate(pairs)]
    for cp in copies:
        cp.start()
    return copies


def _cparams():
    return pltpu.CompilerParams(dimension_semantics=("arbitrary",), vmem_limit_bytes=VMEM_LIMIT)


def _full(shape):
    return pl.BlockSpec(shape, lambda i: (0,) * len(shape))


ANY = pl.BlockSpec(memory_space=pl.ANY)

VQ_SGU_G, VQ_SGU_B, VQ_CONV_B, VQ_CLN_G, VQ_CLN_B = range(5)
VD_LN1_G, VD_LN1_B, VD_LN2_G, VD_LN2_B = range(4)
RS_LN1, RS_SGU, RS_CONV = range(3)
RS_COLS = LANES


def _saved_widths(d, q):
    f32 = [d, q, q, q, q, q, RS_COLS]
    bf16 = [d, 2 * q, d, q]
    return f32, bf16


def _saved_views(f32_ref, bf16_ref, d, q):
    views = []
    for ref, widths in zip((f32_ref, bf16_ref), _saved_widths(d, q)):
        for k, w in enumerate(widths):
            views.append(ref.at[pl.ds(0, ref.shape[0]), pl.ds(sum(widths[:k]), w)])
    return views
def _fwd_mix(x, wi, wo, wst, bmat, cwf, tabs, vq, vd, mlp_w, alpha, tm):
    t, d = x.shape
    q = wi.shape[2]
    nc, n_pairs = CONV_BLOCK // CHUNK, q // LANES
    n = t // tm
    n_in, n_saved = 11, 3
    assert tm % CONV_BLOCK == 0

    def body(x_ref, wi_hbm, wo_hbm, wst_ref, bmat_ref, cwf_ref, fwd_ref, taps_ref, inv_ref, vq_ref, vd_ref, *rest):
        f32_ref, bf16_ref, hf_ref = rest[3:3 + n_saved]
        xh_ref, zu_ref, mg_ref, vhat_ref, gv_ref, yhat_ref, rs_ref, xb_ref, pag_ref, y_ref, vnb_ref = _saved_views(f32_ref, bf16_ref, d, q)
        gathered = rest[3 + n_saved:6 + n_saved]
        wi_v, wo_v, hb_ref, gf_ref, send_sems, recv_sems, copy_sems = rest[6 + n_saved:]
        step = pl.program_id(0)

        @pl.when(step == 0)
        def _():
            loads = _start_copies(copy_sems, [(wi_hbm, wi_v), (wo_hbm, wo_v)])
            _Gather(gathered, send_sems, recv_sems).start()
            hb_ref[...] = jnp.zeros_like(hb_ref)
            gf_ref[...] = _dot(taps_ref[...], _split(cwf_ref[...], True))
            for cp in loads:
                cp.wait()

        @pl.when(step == (3 * n) // 4)
        def _():
            _Gather(gathered, send_sems, recv_sems).forward()

        def sub_tile(b):
            rows = slice(b * CONV_BLOCK, (b + 1) * CONV_BLOCK)
            xv = x_ref[rows, :]
            xb = xv.astype(BF16)
            xb_ref[rows, :] = xb
            pu, pv, pa, pg = (_dot(xb, wi_v[j]) for j in range(4))
            yield
            pag_ref[rows, 0:q] = pa.astype(BF16)
            pag_ref[rows, q:2 * q] = pg.astype(BF16)
            zu, gu = _gelu(pu)
            zv, gv = _gelu(pv)
            vhat, rstd_v = _ln_stats(zv)
            vnb = (vhat * vq_ref[VQ_SGU_G:VQ_SGU_G + 1, :] + vq_ref[VQ_SGU_B:VQ_SGU_B + 1, :]).astype(BF16)
            hb_ref[HALO + b * CONV_BLOCK:HALO + (b + 1) * CONV_BLOCK, :] = pa * _sigmoid(pg)
            yield
            mixed = _mix(wst_ref, vnb, nc, n_pairs) + jnp.concatenate([bmat_ref[...]] * nc, axis=0)
            spectrum = _dot(fwd_ref[...], _split(hb_ref[b * CONV_BLOCK:b * CONV_BLOCK + DFT_N, :]))
            yield
            y_ref[rows, 0:q] = (zu * mixed).astype(BF16)
            zu_ref[rows, :] = zu
            mg_ref[rows, :] = mixed * gu
            vhat_ref[rows, :] = vhat
            gv_ref[rows, :] = gv
            vnb_ref[rows, :] = vnb
            hf_ref[b * 2 * DFT_F:(b + 1) * 2 * DFT_F, :] = spectrum
            product = _split(_cmul(gf_ref[...], spectrum))
            yield
            yc = _dot(inv_ref[...], product) + vq_ref[VQ_CONV_B:VQ_CONV_B + 1, :]
            yield
            yhat, rstd_c = _ln_stats(yc)
            yhat_ref[rows, :] = yhat
            yn = yhat * vq_ref[VQ_CLN_G:VQ_CLN_G + 1, :] + vq_ref[VQ_CLN_B:VQ_CLN_B + 1, :]
            y_ref[rows, q:2 * q] = (yn * _sigmoid(yn)).astype(BF16)
            yield
            r1 = alpha * xv + _dot(y_ref[rows, :], wo_v[...])
            yield
            xhat, rstd1 = _ln_stats(r1)
            xh_ref[rows, :] = xhat
            col = lax.broadcasted_iota(jnp.int32, (CONV_BLOCK, RS_COLS), 1)
            rs_ref[rows, :] = jnp.where(col == RS_LN1, rstd1, jnp.where(col == RS_SGU, rstd_v, jnp.where(col == RS_CONV, rstd_c, 0.0)))

        _interleave([sub_tile(b) for b in range(tm // CONV_BLOCK)])
        hb_ref[0:HALO, :] = hb_ref[tm:tm + HALO, :]

        @pl.when(step == n - 1)
        def _():
            _Gather(gathered, send_sems, recv_sems).finish()

    row = lambda w: pl.BlockSpec((tm, w), lambda i: (i, 0))
    widths = [(sum(w), dt) for w, dt in zip(_saved_widths(d, q), (F32, BF16))]
    small_ins = [wst, bmat, cwf, tabs["fwd"], tabs["taps"], tabs["inv_out"], vq, vd]
    return pl.pallas_call(
        body, name="fwd_mix", grid=(n,),
        in_specs=[row(d), ANY, ANY] + [_full(a.shape) for a in small_ins] + [ANY] * 3,
        out_specs=[row(w) for w, _ in widths] + [pl.BlockSpec((tm // CONV_BLOCK * 2 * DFT_F, q), lambda i: (i, 0))] + [ANY] * 3,
        out_shape=[jax.ShapeDtypeStruct((t, w), dt) for w, dt in widths] + [jax.ShapeDtypeStruct((t // CONV_BLOCK * 2 * DFT_F, q), F32)]
        + [jax.ShapeDtypeStruct(b.shape, b.dtype) for b in mlp_w],
        scratch_shapes=[pltpu.VMEM(wi.shape, BF16), pltpu.VMEM(wo.shape, BF16), pltpu.VMEM((HALO + tm, q), F32),
                        pltpu.VMEM((2 * DFT_F, q), F32)] + _gather_sems(3) + [pltpu.SemaphoreType.DMA((2,))],
        input_output_aliases={n_in + a: n_saved + a for a in range(3)},
        compiler_params=_cparams(),
    )(x, wi, wo, *small_ins, *mlp_w)


MLP_SLABS = 4


def _hidden_slabs(f):
    assert f % MXU_N == 0
    tiles = f // MXU_N
    sizes = [(tiles // MLP_SLABS + (1 if j < tiles % MLP_SLABS else 0)) * MXU_N for j in range(MLP_SLABS)]
    return [(sum(sizes[:j]), sz) for j, sz in enumerate(sizes) if sz]


def _fwd_mlp(saved_f32, tgt, wg, wu, wd, vd, alpha, slabs, tm):
    t, d = tgt.shape
    n = t // tm
    ns = len(slabs)
    half = tm // 2 if tm % 32 == 0 else tm

    def body(xh_ref, tgt_ref, wg_hbm, wu_hbm, wd_hbm, vd_ref, *rest):
        gp_refs = [r.at[pl.ds(0, tm), pl.ds(0, sz)] for r, (_, sz) in zip(rest[:ns], slabs)]
        up_refs = [r.at[pl.ds(0, tm), pl.ds(sz, sz)] for r, (_, sz) in zip(rest[:ns], slabs)]
        x1b_ref, dr2_ref, loss_ref, dg2_ref, db2_ref, wg_v, wu_v, wd_v, copy_sems = rest[ns:]

        @pl.when(pl.program_id(0) == 0)
        def _():
            loads = _start_copies(copy_sems, [(wg_hbm, wg_v), (wu_hbm, wu_v), (wd_hbm, wd_v)])
            loss_ref[...] = jnp.zeros_like(loss_ref)
            dg2_ref[...] = jnp.zeros_like(dg2_ref)
            db2_ref[...] = jnp.zeros_like(db2_ref)
            for cp in loads:
                cp.wait()

        g2 = vd_ref[VD_LN2_G:VD_LN2_G + 1, :]

        for r0 in range(0, tm, half):
            rows = slice(r0, r0 + half)
            x1 = xh_ref[rows, :] * vd_ref[VD_LN1_G:VD_LN1_G + 1, :] + vd_ref[VD_LN1_B:VD_LN1_B + 1, :]
            x1b = x1.astype(BF16)
            x1b_ref[rows, :] = x1b
            acc = alpha * x1
            for (off, sz), gp_ref, up_ref in zip(slabs, gp_refs, up_refs):
                gp = _dot_nt(x1b, wg_v[off:off + sz, :])
                up = _dot_nt(x1b, wu_v[off:off + sz, :])
                gp_ref[rows, :] = gp.astype(BF16)
                up_ref[rows, :] = up.astype(BF16)
                acc = acc + _dot((gp * _sigmoid(gp) * up).astype(BF16), wd_v[off:off + sz, :])
            xh2, rstd2 = _ln_stats(acc)
            err = xh2 * g2 + vd_ref[VD_LN2_B:VD_LN2_B + 1, :] - tgt_ref[rows, :]
            loss_ref[...] += _colsum(jnp.sum(err * err, axis=1, keepdims=True)) * (0.5 / d)
            dy = err * (1.0 / d)
            dg2_ref[...] += _colsum(dy * xh2)
            db2_ref[...] += _colsum(dy)
            dr2_ref[rows, :] = _ln_bwd(dy * g2, xh2, rstd2)

    row = lambda w: pl.BlockSpec((tm, w), lambda i: (i, 0))
    act = [2 * sz for _, sz in slabs]
    return pl.pallas_call(
        body, name="fwd_mlp", grid=(n,),
        in_specs=[row(d), row(d), ANY, ANY, ANY, _full(vd.shape)],
        out_specs=[row(sz) for sz in act] + [row(d), row(d), _full((8, LANES)), _full((1, d)), _full((1, d))],
        out_shape=[jax.ShapeDtypeStruct((t, sz), BF16) for sz in act]
        + [jax.ShapeDtypeStruct((t, d), BF16), jax.ShapeDtypeStruct((t, d), F32),
           jax.ShapeDtypeStruct((8, LANES), F32), jax.ShapeDtypeStruct((1, d), F32), jax.ShapeDtypeStruct((1, d), F32)],
        scratch_shapes=[pltpu.VMEM(wg.shape, BF16), pltpu.VMEM(wu.shape, BF16), pltpu.VMEM(wd.shape, BF16), pltpu.SemaphoreType.DMA((3,))],
        compiler_params=_cparams(),
    )(saved_f32, tgt, wg, wu, wd, vd)


def _bwd_mlp_slab(j, slab, dr2, prev, x1b, gate_up, wg, wu, wd, alpha, tm):
    t, d = dr2.shape
    off, sz = slab
    n = t // tm
    first = prev is None

    def body(*refs):
        if first:
            dr_ref, x1b_ref, gu_ref, wg_hbm, wu_hbm, wd_hbm = refs[:6]
        else:
            dr_ref, dxp_ref, x1b_ref, gu_ref, wg_hbm, wu_hbm, wd_hbm = refs[:7]
        dx_ref, dwg_hbm, dwu_hbm, dwd_hbm, dwg16_hbm, dwu16_hbm, dwd16_hbm, ag, au, ad, wg_v, wu_v, wd_v, copy_sems = refs[-14:]

        @pl.when(pl.program_id(0) == 0)
        def _():
            loads = _start_copies(copy_sems, [(src.at[pl.ds(off, sz)], dst) for src, dst in ((wg_hbm, wg_v), (wu_hbm, wu_v), (wd_hbm, wd_v))])
            ag[...] = jnp.zeros_like(ag)
            au[...] = jnp.zeros_like(au)
            ad[...] = jnp.zeros_like(ad)
            for cp in loads:
                cp.wait()

        dr = dr_ref[...]
        drb = dr.astype(BF16)
        x1b = x1b_ref[...]
        gpv = gu_ref[:, 0:sz].astype(F32)
        upv = gu_ref[:, sz:2 * sz].astype(F32)
        dh = _dot_nt(drb, wd_v[...])
        sg = _sigmoid(gpv)
        silu = gpv * sg
        ad[...] += _dot_tn((silu * upv).astype(BF16), drb)
        dgp = (dh * upv * (sg * (1.0 + gpv * (1.0 - sg)))).astype(BF16)
        dup = (dh * silu).astype(BF16)
        ag[...] += _dot_tn(dgp, x1b)
        au[...] += _dot_tn(dup, x1b)
        base = alpha * dr if first else dxp_ref[...]
        dx_ref[...] = base + _dot(dgp, wg_v[...]) + _dot(dup, wu_v[...])

        @pl.when(pl.program_id(0) == n - 1)
        def _():
            rows = pl.ds(off, sz)
            stores = _start_copies(copy_sems, [(ag, dwg_hbm.at[rows]), (au, dwu_hbm.at[rows]), (ad, dwd_hbm.at[rows])])
            for acc, stage in ((ag, wg_v), (au, wu_v), (ad, wd_v)):
                stage[...] = acc[...].astype(BF16)
            stores += _start_copies(copy_sems, [(wg_v, dwg16_hbm.at[rows]), (wu_v, dwu16_hbm.at[rows]), (wd_v, dwd16_hbm.at[rows])], first=3)
            for cp in stores:
                cp.wait()


    row = lambda w: pl.BlockSpec((tm, w), lambda i: (i, 0))
    ins = [dr2] + ([] if first else [prev[0]]) + [x1b, gate_up, wg, wu, wd] + ([] if first else list(prev[1:]))
    in_specs = [row(d)] + ([] if first else [row(d)]) + [row(d), row(2 * sz), ANY, ANY, ANY] + ([] if first else [ANY] * 6)
    return pl.pallas_call(
        body, name=f"bwd_mlp_{j}", grid=(n,),
        in_specs=in_specs,
        out_specs=[row(d)] + [ANY] * 6,
        out_shape=[jax.ShapeDtypeStruct((t, d), F32)] + [jax.ShapeDtypeStruct(wg.shape, F32)] * 3 + [jax.ShapeDtypeStruct(wg.shape, BF16)] * 3,
        scratch_shapes=[pltpu.VMEM((sz, d), F32)] * 3 + [pltpu.VMEM((sz, d), BF16)] * 3 + [pltpu.SemaphoreType.DMA((6,))],
        input_output_aliases={} if first else {7 + a: 1 + a for a in range(6)},
        compiler_params=_cparams(),
    )(*ins)


SMALL_VD = CHUNK
SMALL_CW = CHUNK + 8
SMALL_VQ = CHUNK + 8
SMALL_LOSS = CHUNK + 16
SMALL_BS = CHUNK + 24


def _small_rows(kwp):
    return -(-(SMALL_CW + max(kwp, 24 + SUBLANES)) // 16) * 16


def _bwd_mix(dx1, saved, wi, wo, wstt, cwf, tabs, vq, vd, mlp_small, token, alpha, tm):
    saved_f32, saved_bf16, hf_s = saved
    t, d = dx1.shape
    q = wi.shape[2]
    nc, n_pairs = CONV_BLOCK // CHUNK, q // LANES
    n = t // tm
    nb = tm // CONV_BLOCK
    assert tm % CONV_BLOCK == 0

    def body(dx1_ref, f32_ref, bf16_ref, hf_ref,
             wi_hbm, wo_hbm, wstt_ref, cwf_ref, fwd_ref, fwd_halo_ref, shift_ref, taps_ref, inv_ref, inv_taps_ref, vq_ref, vd_ref,
             loss_ref, dg2_ref, db2_ref, token_ref,
             gx_ref, dwi_hbm, dwo_hbm, small_hbm, dwi16_hbm, dwo16_hbm,
             wi_v, wo_v, awi, awo, dyb_ref, later_ref, dbm_ref, gf_ref, dgf_ref, small_ref, copy_sems):
        xh_ref, zu_ref, mg_ref, vhat_ref, gv_ref, yhat_ref, rs_ref, xb_ref, pag_ref, y_ref, vnb_ref = _saved_views(f32_ref, bf16_ref, d, q)
        i = pl.program_id(0)

        @pl.when(i == 0)
        def _():
            loads = _start_copies(copy_sems, [(wi_hbm, wi_v), (wo_hbm, wo_v)])
            for r in (awi, awo, small_ref, dbm_ref, dgf_ref, dyb_ref, later_ref):
                r[...] = jnp.zeros_like(r)
            gf_ref[...] = _dot(taps_ref[...], _split(cwf_ref[...], True))
            for cp in loads:
                cp.wait()

        dr1b_parts, dproj_parts = [None] * nb, [None] * nb

        def sub_tile(b):
            rows = slice(b * CONV_BLOCK, (b + 1) * CONV_BLOCK)
            dx1v = dx1_ref[rows, :]
            xh = xh_ref[rows, :]
            rsv = rs_ref[rows, :]
            small_ref[SMALL_VD + VD_LN1_G:SMALL_VD + VD_LN1_G + 1, :] += _colsum(dx1v * xh)
            small_ref[SMALL_VD + VD_LN1_B:SMALL_VD + VD_LN1_B + 1, :] += _colsum(dx1v)
            dr1 = _ln_bwd(dx1v * vd_ref[VD_LN1_G:VD_LN1_G + 1, :], xh, rsv[:, RS_LN1:RS_LN1 + 1])
            dr1b = dr1.astype(BF16)
            yield
            dy = _dot_nt(dr1b, wo_v[...])
            yield
            vhat = vhat_ref[rows, :]
            sgu_g = vq_ref[VQ_SGU_G:VQ_SGU_G + 1, :]
            doa = dy[:, 0:q]
            dm = doa * zu_ref[rows, :]
            dpu = (doa * mg_ref[rows, :]).astype(BF16)
            acc = dm[0:CHUNK]
            for c in range(1, nc):
                acc = acc + dm[c * CHUNK:(c + 1) * CHUNK]
            dbm_ref[...] += acc
            pa = pag_ref[rows, 0:q].astype(F32)
            sg = _sigmoid(pag_ref[rows, q:2 * q].astype(F32))
            yhat = yhat_ref[rows, :]
            cln_g = vq_ref[VQ_CLN_G:VQ_CLN_G + 1, :]
            yn = yhat * cln_g + vq_ref[VQ_CLN_B:VQ_CLN_B + 1, :]
            sy = _sigmoid(yn)
            dyn = dy[:, q:2 * q] * (sy * (1.0 + yn * (1.0 - sy)))
            small_ref[SMALL_VQ + VQ_CLN_G:SMALL_VQ + VQ_CLN_G + 1, q:2 * q] += _colsum(dyn * yhat)
            small_ref[SMALL_VQ + VQ_CLN_B:SMALL_VQ + VQ_CLN_B + 1, q:2 * q] += _colsum(dyn)
            dyc = _ln_bwd(dyn * cln_g, yhat, rsv[:, RS_CONV:RS_CONV + 1])
            small_ref[SMALL_VQ + VQ_CONV_B:SMALL_VQ + VQ_CONV_B + 1, q:2 * q] += _colsum(dyc)
            dyb_ref[b, 0:CONV_BLOCK, :] = dyc
            yield
            wgrads = _mix_wgrad(dm, vnb_ref[rows, :], nc, n_pairs)
            dvn = _mix(wstt_ref, dm, nc, n_pairs)
            own = _dot(fwd_ref[...], _split(dyb_ref[b]))
            with_later = own + _dot(fwd_halo_ref[...], _split(later_ref[...]))
            later_ref[...] = dyb_ref[b, 0:HALO, :]
            yield
            for p, g in enumerate(wgrads):
                for half in range(2):
                    small_ref[0:CHUNK, (2 * p + half) * CHUNK:(2 * p + half + 1) * CHUNK] += g[half * CHUNK:(half + 1) * CHUNK]
            small_ref[SMALL_VQ + VQ_SGU_G:SMALL_VQ + VQ_SGU_G + 1, q:2 * q] += _colsum(dvn * vhat)
            small_ref[SMALL_VQ + VQ_SGU_B:SMALL_VQ + VQ_SGU_B + 1, q:2 * q] += _colsum(dvn)
            dpv = (_ln_bwd(dvn * sgu_g, vhat, rsv[:, RS_SGU:RS_SGU + 1]) * gv_ref[rows, :]).astype(BF16)
            dgf_ref[...] += _cmul(_cmul(own, shift_ref[...]), hf_ref[b * 2 * DFT_F:(b + 1) * 2 * DFT_F, :], conj_b=True)
            product = _split(_cmul(with_later, gf_ref[...], conj_b=True))
            yield
            dh = _dot(inv_ref[...], product)
            yield
            da = (dh * sg).astype(BF16)
            dg = (dh * pa * (sg * (1.0 - sg))).astype(BF16)
            yield
            gx = alpha * dr1
            for dpj, wj in zip((dpu, dpv, da, dg), range(4)):
                gx = gx + _dot_nt(dpj, wi_v[wj])
            gx_ref[rows, :] = gx
            dr1b_parts[b], dproj_parts[b] = dr1b, (dpu, dpv, da, dg)

        _interleave([sub_tile(b) for b in reversed(range(nb))])

        awo[...] += _dot_tn(y_ref[...], jnp.concatenate(dr1b_parts, axis=0))
        xb = xb_ref[...]
        for j in range(4):
            awi[j] += _dot_tn(xb, jnp.concatenate([part[j] for part in dproj_parts], axis=0))

        @pl.when(i == n - 1)
        def _():
            stores = _start_copies(copy_sems, [(awi, dwi_hbm), (awo, dwo_hbm)])
            wi_v[...] = awi[...].astype(BF16)
            wo_v[...] = awo[...].astype(BF16)
            stores += _start_copies(copy_sems, [(wi_v, dwi16_hbm), (wo_v, dwo16_hbm)], first=3)
            lane = lax.broadcasted_iota(jnp.int32, (CHUNK, LANES), 1)
            low = lane < HEAD_DIM
            dbs = jnp.zeros((CHUNK, LANES), F32)
            for p in range(n_pairs):
                grp = dbm_ref[:, p * LANES:(p + 1) * LANES]
                dbs = jnp.where(lane == 2 * p, jnp.sum(jnp.where(low, grp, 0.0), axis=1, keepdims=True), dbs)
                dbs = jnp.where(lane == 2 * p + 1, jnp.sum(jnp.where(low, 0.0, grp), axis=1, keepdims=True), dbs)
            tril = lax.broadcasted_iota(jnp.int32, (CHUNK, CHUNK), 0) >= lax.broadcasted_iota(jnp.int32, (CHUNK, CHUNK), 1)
            for h in range(2 * n_pairs):
                block = small_ref[0:CHUNK, h * CHUNK:(h + 1) * CHUNK]
                small_ref[0:CHUNK, h * CHUNK:(h + 1) * CHUNK] = jnp.where(tril, block, 0.0)
            small_ref[SMALL_VD + VD_LN2_G:SMALL_VD + VD_LN2_G + 1, :] = dg2_ref[...]
            small_ref[SMALL_VD + VD_LN2_B:SMALL_VD + VD_LN2_B + 1, :] = db2_ref[...]
            small_ref[SMALL_CW:SMALL_CW + kwp, 0:q] = _dot(inv_taps_ref[...], _split(dgf_ref[...], True))
            small_ref[SMALL_LOSS:SMALL_LOSS + SUBLANES, q:q + LANES] = loss_ref[...]
            small_ref[SMALL_BS:SMALL_BS + SUBLANES, q:q + LANES] = jnp.transpose(dbs)[0:SUBLANES]
            stores += _start_copies(copy_sems, [(small_ref, small_hbm)], first=2)
            for cp in stores:
                cp.wait()

    rev = lambda w: pl.BlockSpec((tm, w), lambda i: (n - 1 - i, 0))
    kwp = cwf.shape[0]
    small = jax.ShapeDtypeStruct((_small_rows(kwp), 2 * q), F32)
    small_ins = [wstt, cwf, tabs["fwd"], tabs["fwd_halo"], tabs["shift"], tabs["taps"], tabs["inv_in"], tabs["inv_taps"], vq, vd,
                 *mlp_small]
    return pl.pallas_call(
        body, name="bwd_mix", grid=(n,),
        in_specs=[rev(d), rev(saved_f32.shape[1]), rev(saved_bf16.shape[1]),
                  pl.BlockSpec((nb * 2 * DFT_F, q), lambda i: (n - 1 - i, 0)), ANY, ANY] + [_full(a.shape) for a in small_ins] + [ANY],
        out_specs=[rev(d)] + [ANY] * 5,
        out_shape=[jax.ShapeDtypeStruct((t, d), F32), jax.ShapeDtypeStruct(wi.shape, F32), jax.ShapeDtypeStruct(wo.shape, F32), small,
                   jax.ShapeDtypeStruct(wi.shape, BF16), jax.ShapeDtypeStruct(wo.shape, BF16)],
        scratch_shapes=[pltpu.VMEM(wi.shape, BF16), pltpu.VMEM(wo.shape, BF16), pltpu.VMEM(wi.shape, F32), pltpu.VMEM(wo.shape, F32),
                        pltpu.VMEM((nb, DFT_N, q), F32), pltpu.VMEM((HALO, q), F32),
                        pltpu.VMEM((CHUNK, q), F32), pltpu.VMEM((2 * DFT_F, q), F32), pltpu.VMEM((2 * DFT_F, q), F32),
                        pltpu.VMEM(small.shape, F32), pltpu.SemaphoreType.DMA((5,))],
        compiler_params=_cparams(),
    )(dx1, saved_f32, saved_bf16, hf_s, wi, wo, *small_ins, token)


def _prep(me_arr, w_in, w_out, w_gate_t, w_up_t, w_down, conv_w, kwp):
    kw, cshard = conv_w.shape

    def body(me_ref, wi_ref, wo_ref, wg_ref, wu_ref, wd_ref, cw_ref, oi, oo, og, ou, od, oc):
        for src, dst in ((wi_ref, oi), (wo_ref, oo), (wg_ref, og), (wu_ref, ou), (wd_ref, od)):
            dst[...] = src[...].astype(BF16)
        oc[0:kw, :] = cw_ref[...]
        oc[kw:kwp, :] = jnp.zeros((kwp - kw, cshard), F32)

    ins = (w_in, w_out, w_gate_t, w_up_t, w_down, conv_w)
    outs = [jax.ShapeDtypeStruct((4,) + a.shape, BF16) for a in ins[:5]] + [jax.ShapeDtypeStruct((4, kwp, cshard), F32)]
    grid_spec = pltpu.PrefetchScalarGridSpec(
        num_scalar_prefetch=1, grid=(1,),
        in_specs=[pl.BlockSpec(a.shape, lambda i, me: (0, 0)) for a in ins],
        out_specs=[pl.BlockSpec((None,) + o.shape[1:], lambda i, me: (me[0], 0, 0)) for o in outs])
    return pl.pallas_call(body, name="wprep", grid_spec=grid_spec, out_shape=outs, compiler_params=_cparams())(me_arr, *ins)


def _coords():
    return tuple(lax.axis_index(a) for a in MESH_AXES)


def _other_chips(x, y):
    return [(1 - x, y), (x, 1 - y), (1 - x, 1 - y)]


def _remote(src, dst, send_sem, recv_sem, to):
    return pltpu.make_async_remote_copy(src_ref=src, dst_ref=dst, send_sem=send_sem, recv_sem=recv_sem,
                                        device_id=to, device_id_type=MESH_ID)


def _hbm_call(body, name, ins, out_shape, scratch_shapes, aliases=None):
    return pl.pallas_call(
        body, name=name, in_specs=[ANY] * len(ins), out_specs=[ANY] * len(out_shape), out_shape=out_shape,
        scratch_shapes=scratch_shapes, input_output_aliases=aliases or {},
    )(*ins)


class _Gather:
    def __init__(self, bufs, send_sems, recv_sems):
        self.bufs, self.send_sems, self.recv_sems = bufs, send_sems, recv_sems
        self.x, self.y, self.c = _coords()

    def _copies(self, stage):
        x, y, c = self.x, self.y, self.c
        for a, buf in enumerate(self.bufs):
            hr = buf.shape[1] // 2
            for j, chip in enumerate(_other_chips(x, y)):
                if stage == "ici_out":
                    ref, k, to = buf.at[2 * x + y, pl.ds(c * hr, hr)], j, (*chip, c)
                elif stage == "ici_in":
                    ref, k, to = buf.at[2 * chip[0] + chip[1], pl.ds(c * hr, hr)], j, (*chip, c)
                elif stage == "d2d_out":
                    ref, k, to = buf.at[2 * chip[0] + chip[1], pl.ds(c * hr, hr)], 3 + j, (x, y, 1 - c)
                else:
                    ref, k, to = buf.at[2 * chip[0] + chip[1], pl.ds((1 - c) * hr, hr)], 3 + j, (x, y, 1 - c)
                yield _remote(ref, ref, self.send_sems.at[a, k], self.recv_sems.at[a, k], to)

    def start(self):
        for cp in self._copies("ici_out"):
            cp.start()

    def forward(self):
        for landed, onward in zip(self._copies("ici_in"), self._copies("d2d_out")):
            landed.wait_recv()
            onward.start()

    def finish(self):
        for cp in self._copies("d2d_in"):
            cp.wait_recv()
        for stage in ("ici_out", "d2d_out"):
            for cp in self._copies(stage):
                cp.wait_send()


def _gather_sems(n):
    return [pltpu.SemaphoreType.DMA((n, 6)), pltpu.SemaphoreType.DMA((n, 6))]


def _gather_shards(bufs):
    n = len(bufs)

    def body(*refs):
        g = _Gather(refs[n:2 * n], *refs[2 * n:])
        g.start()
        g.forward()
        g.finish()

    return _hbm_call(body, "gather_shards", bufs, [jax.ShapeDtypeStruct(s.shape, s.dtype) for s in bufs],
                     _gather_sems(n), aliases={a: a for a in range(n)})


def _pair_reduce(name, partials, payloads, c_arr, out_dtypes):
    n = len(partials)
    counts = [g.shape[0] for g in partials]
    first = [sum(counts[:a]) for a in range(n)]
    steps = sum(counts)

    def body(c_ref, *refs):
        own, travelling, out, land = (refs[k * n:(k + 1) * n] for k in range(4))
        send_sems, recv_sems = refs[4 * n:]
        i = pl.program_id(0)
        x, y, c = _coords()

        def copy(a, q):
            return _remote(travelling[a].at[q, 1 - c], land[a].at[q], send_sems.at[first[a] + q], recv_sems.at[first[a] + q],
                           (x, y, 1 - c))

        blocks = [(a, q) for a in range(n) for q in range(counts[a])]

        @pl.when(i == 0)
        def _():
            for a, q in blocks:
                copy(a, q).start()

        for a in range(n):
            @pl.when((i >= first[a]) & (i < first[a] + counts[a]))
            def _(a=a):
                q = i - first[a]
                copy(a, q).wait_recv()
                out[a][...] = (own[a][...] + land[a][q].astype(F32)).astype(out_dtypes[a])

        @pl.when(i == steps - 1)
        def _():
            for a, q in blocks:
                copy(a, q).wait_send()

    at = lambda a, i: jnp.clip(i - first[a], 0, counts[a] - 1)
    in_specs = [pl.BlockSpec((None, None) + g.shape[2:], lambda i, cr, a=a: (at(a, i), cr[0], 0, 0)) for a, g in enumerate(partials)]
    out_specs = [pl.BlockSpec((None,) + g.shape[2:], lambda i, cr, a=a: (at(a, i), 0, 0)) for a, g in enumerate(partials)]
    grid_spec = pltpu.PrefetchScalarGridSpec(
        num_scalar_prefetch=1, grid=(steps,), in_specs=in_specs + [ANY] * n, out_specs=out_specs,
        scratch_shapes=[pltpu.VMEM((g.shape[0],) + g.shape[2:], p.dtype) for g, p in zip(partials, payloads)]
        + [pltpu.SemaphoreType.DMA((steps,)), pltpu.SemaphoreType.DMA((steps,))])
    out_shape = [jax.ShapeDtypeStruct((g.shape[0],) + g.shape[2:], dt) for g, dt in zip(partials, out_dtypes)]
    return list(pl.pallas_call(body, name=name, grid_spec=grid_spec, out_shape=out_shape, compiler_params=_cparams())(
        c_arr, *partials, *payloads))


class _Exchange:
    def __init__(self, src, dst, send_sems, recv_sems):
        self.src, self.dst, self.send_sems, self.recv_sems = src, dst, send_sems, recv_sems
        self.x, self.y, self.c = _coords()

    def _copies(self, incoming):
        x, y, c = self.x, self.y, self.c
        for a, (s, d) in enumerate(zip(self.src, self.dst)):
            for j, chip in enumerate(_other_chips(x, y)):
                slot = 2 * chip[0] + chip[1]
                if incoming:
                    out, into = d.at[slot], d.at[slot]
                else:
                    out, into = (s.at[slot] if len(s.shape) == 3 else s), d.at[2 * x + y]
                yield _remote(out, into, self.send_sems.at[a, j], self.recv_sems.at[a, j], (*chip, c))

    def start(self):
        for cp in self._copies(False):
            cp.start()

    def finish(self):
        for cp in self._copies(True):
            cp.wait_recv()
        for cp in self._copies(False):
            cp.wait_send()


def _exchange_shapes(arrs):
    return [jax.ShapeDtypeStruct((4,) + s.shape[-2:], s.dtype) for s in arrs]


class _FlatSems:
    def __init__(self, ref):
        self.ref = ref

    @property
    def at(self):
        return self

    def __getitem__(self, idx):
        return self.ref.at[3 * idx[0] + idx[1]]


HBM = pl.BlockSpec(memory_space=pltpu.HBM)
SEM = pl.BlockSpec(memory_space=pltpu.SEMAPHORE)
DATAFLOW = pltpu.SideEffectType.DATAFLOW_SIDE_EFFECTING


def _exchange_start(name, arrs):
    n = len(arrs)
    lands = _exchange_shapes(arrs)

    def body(*refs):
        src, land = refs[:n], refs[n:2 * n]
        send_sems, recv_sems = refs[2 * n:2 * n + 2]
        token = refs[-1]
        _Exchange(src, land, _FlatSems(send_sems), _FlatSems(recv_sems)).start()
        token[...] = jnp.zeros_like(token)

    hbm = lambda a: pltpu.with_memory_space_constraint(a, pltpu.HBM)
    outs = pl.pallas_call(
        body, name=name,
        out_shape=(pltpu.SemaphoreType.DMA((3 * n,)), pltpu.SemaphoreType.DMA((3 * n,)),
                   *[pltpu.HBM(a.shape, a.dtype) for a in arrs], *[pltpu.HBM(s.shape, s.dtype) for s in lands],
                   jax.ShapeDtypeStruct((SUBLANES, LANES), F32)),
        in_specs=[HBM] * (2 * n), out_specs=(SEM, SEM, *[HBM] * (2 * n), pl.BlockSpec(memory_space=pltpu.VMEM)),
        input_output_aliases={a: 2 + a for a in range(2 * n)},
        compiler_params=pltpu.CompilerParams(has_side_effects=DATAFLOW),
    )(*[hbm(a) for a in arrs], *[hbm(lax.empty(s.shape, s.dtype)) for s in lands])
    return outs[:-1], outs[-1]


def _exchange_wait(name, started, after):
    send_sems, recv_sems, *bufs = started
    n = len(bufs) // 2

    def body(*refs):
        src, land = refs[:n], refs[n:2 * n]
        send_sems, recv_sems = refs[2 * n:2 * n + 2]
        _Exchange(src, land, _FlatSems(send_sems), _FlatSems(recv_sems)).finish()

    outs = pl.pallas_call(
        body, name=name,
        out_shape=tuple(pltpu.HBM(b.shape, b.dtype) for b in bufs),
        in_specs=[HBM] * (2 * n) + [SEM, SEM] + [ANY] * len(after), out_specs=tuple([HBM] * (2 * n)),
        input_output_aliases={a: a for a in range(2 * n)},
        compiler_params=pltpu.CompilerParams(has_side_effects=DATAFLOW),
    )(*bufs, send_sems, recv_sems, *after)
    return list(outs[:n]), list(outs[n:])


def _chip_sum(a, parts, own, me_arr, after):
    _, hr, cc = parts.shape

    def body(me_ref, p_ref, own_ref, after_ref, o_ref):
        for mine in range(4):
            @pl.when(me_ref[0] == mine)
            def _():
                term = lambda j: (own_ref if j == mine else p_ref.at[j])[...].astype(F32)
                o_ref[...] = ((term(0) + term(1)) + term(2)) + term(3)

    own_spec = (pl.BlockSpec((None, hr, cc), lambda i, me: (me[0], 0, 0)) if own.ndim == 3
                else pl.BlockSpec((hr, cc), lambda i, me: (0, 0)))
    grid_spec = pltpu.PrefetchScalarGridSpec(
        num_scalar_prefetch=1, grid=(1,),
        in_specs=[pl.BlockSpec((4, hr, cc), lambda i, me: (0, 0, 0)), own_spec, ANY],
        out_specs=pl.BlockSpec((hr, cc), lambda i, me: (0, 0)))
    return pl.pallas_call(body, name=f"chip_sum_{a}", grid_spec=grid_spec, out_shape=jax.ShapeDtypeStruct((hr, cc), F32),
                          compiler_params=_cparams())(me_arr, parts, own, after)


def _adamw(name, w, g_mine, m, v, c_arr):
    hr, c = g_mine.shape
    assert w.shape == (2 * hr, c)
    bc1 = 1.0 - ADAM_B1 ** ADAM_STEP
    bc2 = 1.0 - ADAM_B2 ** ADAM_STEP

    def body(c_ref, w_ref, gm_ref, m_ref, v_ref, go, do, mo, vo, land, send_sem, recv_sem):
        x, y, core = _coords()
        trade = _remote(gm_ref, land, send_sem, recv_sem, (x, y, 1 - core))

        def update(gv):
            mn = ADAM_B1 * m_ref[...] + (1.0 - ADAM_B1) * gv
            vn = ADAM_B2 * v_ref[...] + (1.0 - ADAM_B2) * (gv * gv)
            go[...] = gv
            mo[...] = mn
            vo[...] = vn
            do[...] = -ADAM_LR * ((mn / bc1) / (jnp.sqrt(vn / bc2) + ADAM_EPS) + ADAM_WD * w_ref[...])

        @pl.when(pl.program_id(0) == 0)
        def _():
            trade.start()
            update(gm_ref[...])

        @pl.when(pl.program_id(0) == 1)
        def _():
            trade.wait()
            update(land[...])

    blk = pl.BlockSpec((hr, c), lambda i, cr: ((cr[0] + i) % 2, 0))
    grid_spec = pltpu.PrefetchScalarGridSpec(
        num_scalar_prefetch=1, grid=(2,), in_specs=[blk, pl.BlockSpec((hr, c), lambda i, cr: (0, 0)), blk, blk], out_specs=[blk] * 4,
        scratch_shapes=[pltpu.VMEM((hr, c), F32), pltpu.SemaphoreType.DMA(()), pltpu.SemaphoreType.DMA(())])
    return pl.pallas_call(body, name=f"adamw_{name}", grid_spec=grid_spec, out_shape=[jax.ShapeDtypeStruct(w.shape, F32)] * 4,
                          compiler_params=_cparams())(c_arr, w, g_mine, m, v)


SMALL_Q = ("sgu_ln_g", "sgu_ln_b", "conv_b", "conv_ln_g", "conv_ln_b")
SMALL_D = ("ln1_g", "ln1_b", "ln2_g", "ln2_b")


def _adamw_small(g_mine, c_arr, me_arr, params):
    names = list(SMALL_Q) + list(SMALL_D) + ["w_s", "b_s", "conv_w"]
    hr, width = g_mine.shape
    q = width // 2
    heads = params["w_s"][0].shape[1]
    kw, cshard = params["conv_w"][0].shape[1:]
    bc1 = 1.0 - ADAM_B1 ** ADAM_STEP
    bc2 = 1.0 - ADAM_B2 ** ADAM_STEP

    def update(w, g, m, v):
        mn = ADAM_B1 * m + (1.0 - ADAM_B1) * g
        vn = ADAM_B2 * v + (1.0 - ADAM_B2) * (g * g)
        return g, -ADAM_LR * ((mn / bc1) / (jnp.sqrt(vn / bc2) + ADAM_EPS) + ADAM_WD * w), mn, vn

    def body(c_ref, me_ref, gm_ref, *refs):
        ins = {nm: refs[3 * k:3 * k + 3] for k, nm in enumerate(names)}
        outs = {nm: refs[3 * len(names) + 4 * k:3 * len(names) + 4 * k + 4] for k, nm in enumerate(names)}
        loss_ref, cw_ref, land, send_sem, recv_sem = refs[-5:]
        x, y, core = _coords()
        trade = _remote(gm_ref, land, send_sem, recv_sem, (x, y, 1 - core))
        trade.start()
        trade.wait()
        first, second = gm_ref[...], land[...]
        low = c_ref[0] == 0
        g_all = jnp.concatenate([jnp.where(low, first, second), jnp.where(low, second, first)], axis=0)

        def apply(nm, g, at):
            w, m, v = (r[at] for r in ins[nm])
            for o, val in zip(outs[nm], update(w, g, m, v)):
                o[at] = val

        for row, nm in enumerate(SMALL_Q):
            apply(nm, g_all[SMALL_VQ + row:SMALL_VQ + row + 1, q:2 * q], ...)
        for row, nm in enumerate(SMALL_D):
            apply(nm, g_all[SMALL_VD + row:SMALL_VD + row + 1, :], ...)
        for h in range(heads):
            apply("w_s", g_all[0:CHUNK, h * CHUNK:(h + 1) * CHUNK], (0, h))
        apply("b_s", g_all[SMALL_BS:SMALL_BS + heads, q:q + LANES], 0)
        cw_ref[...] = jnp.zeros_like(cw_ref)
        for chip in range(4):
            @pl.when(me_ref[0] == chip)
            def _():
                cw_ref[...] = g_all[SMALL_CW:SMALL_CW + cw_ref.shape[0], chip * cshard:(chip + 1) * cshard]
        apply("conv_w", cw_ref[0:kw, :], 0)
        loss_ref[...] = g_all[SMALL_LOSS:SMALL_LOSS + SUBLANES, q:q + LANES]

    arrays = [a for nm in names for a in params[nm]]
    out_shape = [jax.ShapeDtypeStruct(params[nm][0].shape, F32) for nm in names for _ in range(4)] + [jax.ShapeDtypeStruct((SUBLANES, LANES), F32)]
    whole = lambda shape: pl.BlockSpec(shape, lambda i, c, me: (0,) * len(shape))
    grid_spec = pltpu.PrefetchScalarGridSpec(
        num_scalar_prefetch=2, grid=(1,),
        in_specs=[whole(g_mine.shape)] + [whole(a.shape) for a in arrays],
        out_specs=[whole(s.shape) for s in out_shape],
        scratch_shapes=[pltpu.VMEM((-(-kw // SUBLANES) * SUBLANES, cshard), F32), pltpu.VMEM(g_mine.shape, F32),
                        pltpu.SemaphoreType.DMA(()), pltpu.SemaphoreType.DMA(())])
    res = pl.pallas_call(body, name="adamw_small", grid_spec=grid_spec, out_shape=out_shape, compiler_params=_cparams())(
        c_arr, me_arr, g_mine, *arrays)
    return {nm: list(res[4 * k:4 * k + 4]) for k, nm in enumerate(names)}, res[-1]


def _pad_rows(a, rows):
    return jnp.pad(a, ((0, rows - a.shape[0]), (0, 0)))


def kernel(x, w_in, sgu_ln_g, sgu_ln_b, w_s, b_s, conv_w, conv_b, conv_ln_g, conv_ln_b, w_out, ln1_g, ln1_b, w_gate, w_up, w_down, ln2_g, ln2_b, loss_target, m_w_in, m_sgu_ln_g, m_sgu_ln_b, m_w_s, m_b_s, m_conv_w, m_conv_b, m_conv_ln_g, m_conv_ln_b, m_w_out, m_ln1_g, m_ln1_b, m_w_gate, m_w_up, m_w_down, m_ln2_g, m_ln2_b, v_w_in, v_sgu_ln_g, v_sgu_ln_b, v_w_s, v_b_s, v_conv_w, v_conv_b, v_conv_ln_g, v_conv_ln_b, v_w_out, v_ln1_g, v_ln1_b, v_w_gate, v_w_up, v_w_down, v_ln2_g, v_ln2_b):
    depth, d, q = w_in.shape
    assert depth == 1 and x.shape[0] == 1
    t = x.shape[1]
    heads = w_s.shape[1]
    kw, cshard = conv_w.shape[1], conv_w.shape[2]
    fs = w_gate.shape[2]
    slabs = _hidden_slabs(4 * fs)
    n_pairs = q // LANES
    assert heads * HEAD_DIM == q and q % LANES == 0 and w_s.shape[2] == CHUNK and 4 * cshard == q and kw - 1 <= HALO
    alpha = (2.0 * depth) ** 0.25
    tm = min(512, t)
    assert t % tm == 0 and tm % CHUNK == 0
    x2, tgt = x[0], loss_target[0]
    mx, my, mc = _coords()
    me = 2 * mx + my
    c_arr = jnp.reshape(mc, (1,)).astype(jnp.int32)

    kwp = -(-kw // 16) * 16
    me_arr = jnp.reshape(me, (1,)).astype(jnp.int32)
    wi, wo, wg, wu, wd, cw4 = _prep(me_arr, w_in[0], w_out[0], w_gate[0].T, w_up[0].T, w_down[0], conv_w[0], kwp)
    wi, wo, cw4 = _gather_shards([wi, wo, cw4])
    wo = wo.reshape(d, d)
    cw = jnp.transpose(cw4, (1, 0, 2)).reshape(kwp, q)
    cwf = _pad_rows(cw[:kw][::-1], kwp)
    tabs = {name: jnp.asarray(tab) for name, tab in _dft_tables(kw, kwp, q).items()}

    wm = jnp.where(jnp.tril(jnp.ones((CHUNK, CHUNK), bool)), w_s[0], 0.0)
    wst = wm.reshape(n_pairs, 2 * CHUNK, CHUNK).astype(BF16)
    wstt = jnp.transpose(wm, (0, 2, 1)).reshape(n_pairs, 2 * CHUNK, CHUNK).astype(BF16)
    bmat = jnp.repeat(b_s[0].T, HEAD_DIM, axis=1)
    vq = _pad_rows(jnp.concatenate([sgu_ln_g, sgu_ln_b, conv_b, conv_ln_g, conv_ln_b], axis=0), 8)
    vd = _pad_rows(jnp.concatenate([ln1_g, ln1_b, ln2_g, ln2_b], axis=0), 8)

    *saved, wg, wu, wd = _fwd_mix(x2, wi, wo, wst, bmat, cwf, tabs, vq, vd, [wg, wu, wd], alpha, tm)
    wg, wu, wd = (w.reshape(4 * fs, d) for w in (wg, wu, wd))
    *acts, x1b, dr2, loss_part, dg2, db2 = _fwd_mlp(saved[0], tgt, wg, wu, wd, vd, alpha, slabs, tm)
    mlp_grads = None
    for j, slab in enumerate(slabs):
        mlp_grads = _bwd_mlp_slab(j, slab, dr2, mlp_grads, x1b, acts[j], wg, wu, wd, alpha, tm)
    dx1 = mlp_grads[0]
    mlp_halves = [b.reshape(4, 2, fs // 2, d) for b in mlp_grads[1:]]
    mlp_sums = _pair_reduce("pair_reduce_mlp", mlp_halves[:3], mlp_halves[3:], c_arr, [BF16] * 3)
    mlp_started, token = _exchange_start("exchange_mlp_start", mlp_sums)
    grad_x, dwi, dwo, small, dwi16, dwo16 = _bwd_mix(dx1, saved, wi, wo, wstt, cwf, tabs, vq, vd, (loss_part, dg2, db2), token, alpha, tm)
    mlp_sums, mlp_parts = _exchange_wait("exchange_mlp_wait", mlp_started, [dwo])

    by_halves = lambda b: b.reshape(4, 2, b.shape[1] // 2, b.shape[2])
    halves = [by_halves(dwi), by_halves(dwo.reshape(4, d // 4, d)), small.reshape(1, 2, small.shape[0] // 2, small.shape[1])]
    travelling = [by_halves(dwi16), by_halves(dwo16.reshape(4, d // 4, d)), halves[-1]]
    *sums, small_sum = _pair_reduce("pair_reduce_mix", halves, travelling, c_arr, [BF16, BF16, F32])
    sums.append(small_sum[0])
    mix_started, token = _exchange_start("exchange_mix_start", sums)

    out, raw = {}, {}

    def finish(first, names, parts, sums, after):
        mine = [_chip_sum(first + a, p, s, me_arr, after) for a, (p, s) in enumerate(zip(parts, sums))]
        for a, nm in enumerate(names):
            w_, m_, v_ = weights[nm]
            if nm in ("w_gate", "w_up"):
                raw[nm] = _adamw(nm, w_[0].T, mine[a], m_[0].T, v_[0].T, c_arr)
                out[nm] = [o.T for o in raw[nm]]
            else:
                raw[nm] = out[nm] = _adamw(nm, w_[0], mine[a], m_[0], v_[0], c_arr)
        return mine[-1]

    weights = {"w_in": (w_in, m_w_in, v_w_in), "w_out": (w_out, m_w_out, v_w_out), "w_gate": (w_gate, m_w_gate, v_w_gate),
               "w_up": (w_up, m_w_up, v_w_up), "w_down": (w_down, m_w_down, v_w_down)}
    finish(2, ["w_gate", "w_up", "w_down"], mlp_parts, mlp_sums, token)
    sums, parts = _exchange_wait("exchange_mix_wait", mix_started, [raw[nm][1] for nm in ("w_gate", "w_up", "w_down")])
    small_mine = finish(5, ["w_in", "w_out"], parts, sums, parts[0])

    small_params = {
        "sgu_ln_g": (sgu_ln_g, m_sgu_ln_g, v_sgu_ln_g), "sgu_ln_b": (sgu_ln_b, m_sgu_ln_b, v_sgu_ln_b),
        "conv_b": (conv_b, m_conv_b, v_conv_b), "conv_ln_g": (conv_ln_g, m_conv_ln_g, v_conv_ln_g),
        "conv_ln_b": (conv_ln_b, m_conv_ln_b, v_conv_ln_b), "ln1_g": (ln1_g, m_ln1_g, v_ln1_g), "ln1_b": (ln1_b, m_ln1_b, v_ln1_b),
        "ln2_g": (ln2_g, m_ln2_g, v_ln2_g), "ln2_b": (ln2_b, m_ln2_b, v_ln2_b), "w_s": (w_s, m_w_s, v_w_s),
        "b_s": (b_s, m_b_s, v_b_s), "conv_w": (conv_w, m_conv_w, v_conv_w)}
    small_out, loss_block = _adamw_small(small_mine, c_arr, me_arr, small_params)
    loss = loss_block[0, 0]
    names = ["w_in", "sgu_ln_g", "sgu_ln_b", "w_s", "b_s", "conv_w", "conv_b", "conv_ln_g", "conv_ln_b", "w_out",
             "ln1_g", "ln1_b", "w_gate", "w_up", "w_down", "ln2_g", "ln2_b"]
    result = [loss, grad_x[None]]
    for kind in range(4):
        for nm in names:
            result.append(out[nm][kind][None] if nm in out else small_out[nm][kind])
    return tuple(result)
```

```python
import math

import jax
import numpy as np
import jax.numpy as jnp
from jax import lax
from jax.experimental import pallas as pl
from jax.experimental.pallas import tpu as pltpu

F32 = jnp.float32
BF16 = jnp.bfloat16

LN_EPS = 1e-5
HEAD_DIM = 64
CHUNK = 128
HALO = 32
LANES = 128
MXU_N = 256
ADAM_LR, ADAM_B1, ADAM_B2, ADAM_EPS, ADAM_WD, ADAM_STEP = 0.001, 0.9, 0.999, 1e-08, 0.01, 10
VMEM_LIMIT = 63 * 1024 * 1024
MESH_AXES = ("x", "y", "c")
MESH_ID = pl.DeviceIdType.MESH


def _dot(a, b):
    return jnp.dot(a, b, preferred_element_type=F32)


def _dot_nt(a, b):
    return lax.dot_general(a, b, (((1,), (1,)), ((), ())), preferred_element_type=F32)


def _dot_tn(a, b):
    return lax.dot_general(a, b, (((0,), (0,)), ((), ())), preferred_element_type=F32)


def _sigmoid(v):
    return 1.0 / (1.0 + jnp.exp(-v))


def _gelu(v):
    cdf = 0.5 * (1.0 + lax.erf(v * (1.0 / math.sqrt(2.0))))
    pdf = jnp.exp(-0.5 * v * v) * (1.0 / math.sqrt(2.0 * math.pi))
    return v * cdf, cdf + v * pdf


def _ln_stats(v):
    mu = jnp.mean(v, axis=-1, keepdims=True)
    d = v - mu
    rstd = lax.rsqrt(jnp.mean(d * d, axis=-1, keepdims=True) + LN_EPS)
    return d * rstd, rstd


def _ln_bwd(dxhat, xhat, rstd):
    m1 = jnp.mean(dxhat, axis=-1, keepdims=True)
    m2 = jnp.mean(dxhat * xhat, axis=-1, keepdims=True)
    return rstd * (dxhat - m1 - xhat * m2)


def _colsum(v):
    return jnp.sum(v, axis=0, keepdims=True)


def _pair_lanes(v, nc, p):
    return jnp.concatenate([v[c * CHUNK:(c + 1) * CHUNK, p * LANES:(p + 1) * LANES] for c in range(nc)], axis=1)


def _unpair(parts, nc):
    rows = [jnp.concatenate([part[:, c * LANES:(c + 1) * LANES] for part in parts], axis=1) for c in range(nc)]
    return jnp.concatenate(rows, axis=0)


def _low_head(nc):
    lane = lax.broadcasted_iota(jnp.int32, (CHUNK, nc * LANES), 1)
    return (lane & (LANES - 1)) < HEAD_DIM


def _mix(wst_ref, v, nc, n_pairs):
    vb = v.astype(BF16)
    low = _low_head(nc)
    parts = []
    for p in range(n_pairs):
        r = _dot(wst_ref[p], _pair_lanes(vb, nc, p))
        parts.append(jnp.where(low, r[:CHUNK], r[CHUNK:]))
    return _unpair(parts, nc)


def _mix_wgrad(dm, vn, nc, n_pairs):
    low = _low_head(nc)
    vb = vn.astype(BF16)
    out = []
    for p in range(n_pairs):
        a = _pair_lanes(dm, nc, p)
        lhs = jnp.concatenate([jnp.where(low, a, 0.0), jnp.where(low, 0.0, a)], axis=0).astype(BF16)
        out.append(_dot_nt(lhs, _pair_lanes(vb, nc, p)))
    return out


SUBLANES = 8


CONV_BLOCK = 256
DFT_N = CONV_BLOCK + HALO
DFT_F = -(-(DFT_N // 2 + 1) // SUBLANES) * SUBLANES


def _terms(m, exact):
    hi = m.astype(np.float32).astype(BF16)
    lo = (m.astype(np.float32) - hi.astype(np.float32)).astype(BF16)
    return np.concatenate([hi, hi, lo] if exact else [hi], axis=1)


def _split(v, exact=False):
    hi = v.astype(BF16)
    if not exact:
        return hi
    lo = (v - hi.astype(F32)).astype(BF16)
    return jnp.concatenate([hi, lo, hi], axis=0)


def _dft_tables(kw, kwp, q):
    nf = DFT_N // 2 + 1
    ang = 2.0 * np.pi * np.arange(nf)[:, None] * np.arange(DFT_N)[None, :] / DFT_N
    fwd = np.zeros((2 * DFT_F, DFT_N))
    fwd[:nf], fwd[DFT_F:DFT_F + nf] = np.cos(ang), -np.sin(ang)
    weight = np.full((nf, 1), 2.0 / DFT_N)
    weight[0] = weight[-1] = 1.0 / DFT_N
    inv = np.zeros((DFT_N, 2 * DFT_F))
    inv[:, :nf], inv[:, DFT_F:DFT_F + nf] = (np.cos(ang) * weight).T, (-np.sin(ang) * weight).T
    inv_taps = np.zeros((kwp, 2 * DFT_F))
    inv_taps[:kw] = inv[kw - 1::-1][:kw]
    shift = np.zeros((2 * DFT_F, q), np.float32)
    shift[:nf], shift[DFT_F:DFT_F + nf] = np.cos(ang[:, HALO:HALO + 1]), -np.sin(ang[:, HALO:HALO + 1])
    return {"fwd": _terms(fwd, False), "fwd_halo": _terms(fwd[:, CONV_BLOCK:], False), "shift": shift,
            "inv_out": _terms(inv[HALO:HALO + CONV_BLOCK], False), "inv_in": _terms(inv[:CONV_BLOCK], False),
            "taps": _terms(fwd[:, :kwp], True), "inv_taps": _terms(inv_taps, True)}


def _cmul(a, b, conj_b=False):
    ar, ai, br, bi = a[:DFT_F], a[DFT_F:], b[:DFT_F], b[DFT_F:]
    if conj_b:
        return jnp.concatenate([ar * br + ai * bi, ai * br - ar * bi], axis=0)
    return jnp.concatenate([ar * br - ai * bi, ar * bi + ai * br], axis=0)


def _interleave(sub_tiles):
    waiting, live = list(sub_tiles), []
    while waiting or live:
        if waiting:
            live.append(waiting.pop(0))
        for g in list(live):
            try:
                next(g)
            except StopIteration:
                live.remove(g)


def _start_copies(sems, pairs, first=0):
    copies = [pltpu.make_async_copy(src, dst, sems.at[first + k]) for k, (src, dst) in enumerate(pairs)]
    for cp in copies:
        cp.start()
    return copies


def _cparams():
    return pltpu.CompilerParams(dimension_semantics=("arbitrary",), vmem_limit_bytes=VMEM_LIMIT)


def _full(shape):
    return pl.BlockSpec(shape, lambda i: (0,) * len(shape))


ANY = pl.BlockSpec(memory_space=pl.ANY)

VQ_SGU_G, VQ_SGU_B, VQ_CONV_B, VQ_CLN_G, VQ_CLN_B = range(5)
VD_LN1_G, VD_LN1_B, VD_LN2_G, VD_LN2_B = range(4)
RS_LN1, RS_SGU, RS_CONV = range(3)
RS_COLS = LANES


def _saved_widths(d, q):
    f32 = [d, q, q, q, q, q, RS_COLS]
    bf16 = [d, 2 * q, d, q]
    return f32, bf16


def _saved_views(f32_ref, bf16_ref, d, q):
    views = []
    for ref, widths in zip((f32_ref, bf16_ref), _saved_widths(d, q)):
        for k, w in enumerate(widths):
            views.append(ref.at[pl.ds(0, ref.shape[0]), pl.ds(sum(widths[:k]), w)])
    return views
def _fwd_mix(x, wi, wo, wst, bmat, cwf, tabs, vq, vd, mlp_w, alpha, tm):
    t, d = x.shape
    q = wi.shape[2]
    nc, n_pairs = CONV_BLOCK // CHUNK, q // LANES
    n = t // tm
    n_in, n_saved = 11, 3
    assert tm % CONV_BLOCK == 0

    def body(x_ref, wi_hbm, wo_hbm, wst_ref, bmat_ref, cwf_ref, fwd_ref, taps_ref, inv_ref, vq_ref, vd_ref, *rest):
        f32_ref, bf16_ref, hf_ref = rest[3:3 + n_saved]
        xh_ref, zu_ref, mg_ref, vhat_ref, gv_ref, yhat_ref, rs_ref, xb_ref, pag_ref, y_ref, vnb_ref = _saved_views(f32_ref, bf16_ref, d, q)
        gathered = rest[3 + n_saved:6 + n_saved]
        wi_v, wo_v, hb_ref, gf_ref, send_sems, recv_sems, copy_sems = rest[6 + n_saved:]
        step = pl.program_id(0)

        @pl.when(step == 0)
        def _():
            loads = _start_copies(copy_sems, [(wi_hbm, wi_v), (wo_hbm, wo_v)])
            _Gather(gathered, send_sems, recv_sems).start()
            hb_ref[...] = jnp.zeros_like(hb_ref)
            gf_ref[...] = _dot(taps_ref[...], _split(cwf_ref[...], True))
            for cp in loads:
                cp.wait()

        @pl.when(step == (3 * n) // 4)
        def _():
            _Gather(gathered, send_sems, recv_sems).forward()

        def sub_tile(b):
            rows = slice(b * CONV_BLOCK, (b + 1) * CONV_BLOCK)
            xv = x_ref[rows, :]
            xb = xv.astype(BF16)
            xb_ref[rows, :] = xb
            pu, pv, pa, pg = (_dot(xb, wi_v[j]) for j in range(4))
            yield
            pag_ref[rows, 0:q] = pa.astype(BF16)
            pag_ref[rows, q:2 * q] = pg.astype(BF16)
            zu, gu = _gelu(pu)
            zv, gv = _gelu(pv)
            vhat, rstd_v = _ln_stats(zv)
            vnb = (vhat * vq_ref[VQ_SGU_G:VQ_SGU_G + 1, :] + vq_ref[VQ_SGU_B:VQ_SGU_B + 1, :]).astype(BF16)
            hb_ref[HALO + b * CONV_BLOCK:HALO + (b + 1) * CONV_BLOCK, :] = pa * _sigmoid(pg)
            yield
            mixed = _mix(wst_ref, vnb, nc, n_pairs) + jnp.concatenate([bmat_ref[...]] * nc, axis=0)
            spectrum = _dot(fwd_ref[...], _split(hb_ref[b * CONV_BLOCK:b * CONV_BLOCK + DFT_N, :]))
            yield
            y_ref[rows, 0:q] = (zu * mixed).astype(BF16)
            zu_ref[rows, :] = zu
            mg_ref[rows, :] = mixed * gu
            vhat_ref[rows, :] = vhat
            gv_ref[rows, :] = gv
            vnb_ref[rows, :] = vnb
            hf_ref[b * 2 * DFT_F:(b + 1) * 2 * DFT_F, :] = spectrum
            product = _split(_cmul(gf_ref[...], spectrum))
            yield
            yc = _dot(inv_ref[...], product) + vq_ref[VQ_CONV_B:VQ_CONV_B + 1, :]
            yield
            yhat, rstd_c = _ln_stats(yc)
            yhat_ref[rows, :] = yhat
            yn = yhat * vq_ref[VQ_CLN_G:VQ_CLN_G + 1, :] + vq_ref[VQ_CLN_B:VQ_CLN_B + 1, :]
            y_ref[rows, q:2 * q] = (yn * _sigmoid(yn)).astype(BF16)
            yield
            r1 = alpha * xv + _dot(y_ref[rows, :], wo_v[...])
            yield
            xhat, rstd1 = _ln_stats(r1)
            xh_ref[rows, :] = xhat
            col = lax.broadcasted_iota(jnp.int32, (CONV_BLOCK, RS_COLS), 1)
            rs_ref[rows, :] = jnp.where(col == RS_LN1, rstd1, jnp.where(col == RS_SGU, rstd_v, jnp.where(col == RS_CONV, rstd_c, 0.0)))

        _interleave([sub_tile(b) for b in range(tm // CONV_BLOCK)])
        hb_ref[0:HALO, :] = hb_ref[tm:tm + HALO, :]

        @pl.when(step == n - 1)
        def _():
            _Gather(gathered, send_sems, recv_sems).finish()

    row = lambda w: pl.BlockSpec((tm, w), lambda i: (i, 0))
    widths = [(sum(w), dt) for w, dt in zip(_saved_widths(d, q), (F32, BF16))]
    small_ins = [wst, bmat, cwf, tabs["fwd"], tabs["taps"], tabs["inv_out"], vq, vd]
    return pl.pallas_call(
        body, name="fwd_mix", grid=(n,),
        in_specs=[row(d), ANY, ANY] + [_full(a.shape) for a in small_ins] + [ANY] * 3,
        out_specs=[row(w) for w, _ in widths] + [pl.BlockSpec((tm // CONV_BLOCK * 2 * DFT_F, q), lambda i: (i, 0))] + [ANY] * 3,
        out_shape=[jax.ShapeDtypeStruct((t, w), dt) for w, dt in widths] + [jax.ShapeDtypeStruct((t // CONV_BLOCK * 2 * DFT_F, q), F32)]
        + [jax.ShapeDtypeStruct(b.shape, b.dtype) for b in mlp_w],
        scratch_shapes=[pltpu.VMEM(wi.shape, BF16), pltpu.VMEM(wo.shape, BF16), pltpu.VMEM((HALO + tm, q), F32),
                        pltpu.VMEM((2 * DFT_F, q), F32)] + _gather_sems(3) + [pltpu.SemaphoreType.DMA((2,))],
        input_output_aliases={n_in + a: n_saved + a for a in range(3)},
        compiler_params=_cparams(),
    )(x, wi, wo, *small_ins, *mlp_w)


MLP_SLABS = 4


def _hidden_slabs(f):
    assert f % MXU_N == 0
    tiles = f // MXU_N
    sizes = [(tiles // MLP_SLABS + (1 if j < tiles % MLP_SLABS else 0)) * MXU_N for j in range(MLP_SLABS)]
    return [(sum(sizes[:j]), sz) for j, sz in enumerate(sizes) if sz]


def _fwd_mlp(saved_f32, tgt, wg, wu, wd, vd, alpha, slabs, tm):
    t, d = tgt.shape
    n = t // tm
    ns = len(slabs)
    half = tm // 2 if tm % 32 == 0 else tm

    def body(xh_ref, tgt_ref, wg_hbm, wu_hbm, wd_hbm, vd_ref, *rest):
        gp_refs = [r.at[pl.ds(0, tm), pl.ds(0, sz)] for r, (_, sz) in zip(rest[:ns], slabs)]
        up_refs = [r.at[pl.ds(0, tm), pl.ds(sz, sz)] for r, (_, sz) in zip(rest[:ns], slabs)]
        x1b_ref, dr2_ref, loss_ref, dg2_ref, db2_ref, wg_v, wu_v, wd_v, copy_sems = rest[ns:]

        @pl.when(pl.program_id(0) == 0)
        def _():
            loads = _start_copies(copy_sems, [(wg_hbm, wg_v), (wu_hbm, wu_v), (wd_hbm, wd_v)])
            loss_ref[...] = jnp.zeros_like(loss_ref)
            dg2_ref[...] = jnp.zeros_like(dg2_ref)
            db2_ref[...] = jnp.zeros_like(db2_ref)
            for cp in loads:
                cp.wait()

        g2 = vd_ref[VD_LN2_G:VD_LN2_G + 1, :]

        for r0 in range(0, tm, half):
            rows = slice(r0, r0 + half)
            x1 = xh_ref[rows, :] * vd_ref[VD_LN1_G:VD_LN1_G + 1, :] + vd_ref[VD_LN1_B:VD_LN1_B + 1, :]
            x1b = x1.astype(BF16)
            x1b_ref[rows, :] = x1b
            acc = alpha * x1
            for (off, sz), gp_ref, up_ref in zip(slabs, gp_refs, up_refs):
                gp = _dot_nt(x1b, wg_v[off:off + sz, :])
                up = _dot_nt(x1b, wu_v[off:off + sz, :])
                gp_ref[rows, :] = gp.astype(BF16)
                up_ref[rows, :] = up.astype(BF16)
                acc = acc + _dot((gp * _sigmoid(gp) * up).astype(BF16), wd_v[off:off + sz, :])
            xh2, rstd2 = _ln_stats(acc)
            err = xh2 * g2 + vd_ref[VD_LN2_B:VD_LN2_B + 1, :] - tgt_ref[rows, :]
            loss_ref[...] += _colsum(jnp.sum(err * err, axis=1, keepdims=True)) * (0.5 / d)
            dy = err * (1.0 / d)
            dg2_ref[...] += _colsum(dy * xh2)
            db2_ref[...] += _colsum(dy)
            dr2_ref[rows, :] = _ln_bwd(dy * g2, xh2, rstd2)

    row = lambda w: pl.BlockSpec((tm, w), lambda i: (i, 0))
    act = [2 * sz for _, sz in slabs]
    return pl.pallas_call(
        body, name="fwd_mlp", grid=(n,),
        in_specs=[row(d), row(d), ANY, ANY, ANY, _full(vd.shape)],
        out_specs=[row(sz) for sz in act] + [row(d), row(d), _full((8, LANES)), _full((1, d)), _full((1, d))],
        out_shape=[jax.ShapeDtypeStruct((t, sz), BF16) for sz in act]
        + [jax.ShapeDtypeStruct((t, d), BF16), jax.ShapeDtypeStruct((t, d), F32),
           jax.ShapeDtypeStruct((8, LANES), F32), jax.ShapeDtypeStruct((1, d), F32), jax.ShapeDtypeStruct((1, d), F32)],
        scratch_shapes=[pltpu.VMEM(wg.shape, BF16), pltpu.VMEM(wu.shape, BF16), pltpu.VMEM(wd.shape, BF16), pltpu.SemaphoreType.DMA((3,))],
        compiler_params=_cparams(),
    )(saved_f32, tgt, wg, wu, wd, vd)


def _bwd_mlp_slab(j, slab, dr2, prev, x1b, gate_up, wg, wu, wd, alpha, tm):
    t, d = dr2.shape
    off, sz = slab
    n = t // tm
    first = prev is None

    def body(*refs):
        if first:
            dr_ref, x1b_ref, gu_ref, wg_hbm, wu_hbm, wd_hbm = refs[:6]
        else:
            dr_ref, dxp_ref, x1b_ref, gu_ref, wg_hbm, wu_hbm, wd_hbm = refs[:7]
        dx_ref, dwg_hbm, dwu_hbm, dwd_hbm, dwg16_hbm, dwu16_hbm, dwd16_hbm, ag, au, ad, wg_v, wu_v, wd_v, copy_sems = refs[-14:]

        @pl.when(pl.program_id(0) == 0)
        def _():
            loads = _start_copies(copy_sems, [(src.at[pl.ds(off, sz)], dst) for src, dst in ((wg_hbm, wg_v), (wu_hbm, wu_v), (wd_hbm, wd_v))])
            ag[...] = jnp.zeros_like(ag)
            au[...] = jnp.zeros_like(au)
            ad[...] = jnp.zeros_like(ad)
            for cp in loads:
                cp.wait()

        dr = dr_ref[...]
        drb = dr.astype(BF16)
        x1b = x1b_ref[...]
        gpv = gu_ref[:, 0:sz].astype(F32)
        upv = gu_ref[:, sz:2 * sz].astype(F32)
        dh = _dot_nt(drb, wd_v[...])
        sg = _sigmoid(gpv)
        silu = gpv * sg
        ad[...] += _dot_tn((silu * upv).astype(BF16), drb)
        dgp = (dh * upv * (sg * (1.0 + gpv * (1.0 - sg)))).astype(BF16)
        dup = (dh * silu).astype(BF16)
        ag[...] += _dot_tn(dgp, x1b)
        au[...] += _dot_tn(dup, x1b)
        base = alpha * dr if first else dxp_ref[...]
        dx_ref[...] = base + _dot(dgp, wg_v[...]) + _dot(dup, wu_v[...])

        @pl.when(pl.program_id(0) == n - 1)
        def _():
            rows = pl.ds(off, sz)
            stores = _start_copies(copy_sems, [(ag, dwg_hbm.at[rows]), (au, dwu_hbm.at[rows]), (ad, dwd_hbm.at[rows])])
            for acc, stage in ((ag, wg_v), (au, wu_v), (ad, wd_v)):
                stage[...] = acc[...].astype(BF16)
            stores += _start_copies(copy_sems, [(wg_v, dwg16_hbm.at[rows]), (wu_v, dwu16_hbm.at[rows]), (wd_v, dwd16_hbm.at[rows])], first=3)
            for cp in stores:
                cp.wait()


    row = lambda w: pl.BlockSpec((tm, w), lambda i: (i, 0))
    ins = [dr2] + ([] if first else [prev[0]]) + [x1b, gate_up, wg, wu, wd] + ([] if first else list(prev[1:]))
    in_specs = [row(d)] + ([] if first else [row(d)]) + [row(d), row(2 * sz), ANY, ANY, ANY] + ([] if first else [ANY] * 6)
    return pl.pallas_call(
        body, name=f"bwd_mlp_{j}", grid=(n,),
        in_specs=in_specs,
        out_specs=[row(d)] + [ANY] * 6,
        out_shape=[jax.ShapeDtypeStruct((t, d), F32)] + [jax.ShapeDtypeStruct(wg.shape, F32)] * 3 + [jax.ShapeDtypeStruct(wg.shape, BF16)] * 3,
        scratch_shapes=[pltpu.VMEM((sz, d), F32)] * 3 + [pltpu.VMEM((sz, d), BF16)] * 3 + [pltpu.SemaphoreType.DMA((6,))],
        input_output_aliases={} if first else {7 + a: 1 + a for a in range(6)},
        compiler_params=_cparams(),
    )(*ins)


SMALL_VD = CHUNK
SMALL_CW = CHUNK + 8
SMALL_VQ = CHUNK + 8
SMALL_LOSS = CHUNK + 16
SMALL_BS = CHUNK + 24


def _small_rows(kwp):
    return -(-(SMALL_CW + max(kwp, 24 + SUBLANES)) // 16) * 16


def _bwd_mix(dx1, saved, wi, wo, wstt, cwf, tabs, vq, vd, mlp_small, token, alpha, tm):
    saved_f32, saved_bf16, hf_s = saved
    t, d = dx1.shape
    q = wi.shape[2]
    nc, n_pairs = CONV_BLOCK // CHUNK, q // LANES
    n = t // tm
    nb = tm // CONV_BLOCK
    assert tm % CONV_BLOCK == 0

    def body(dx1_ref, f32_ref, bf16_ref, hf_ref,
             wi_hbm, wo_hbm, wstt_ref, cwf_ref, fwd_ref, fwd_halo_ref, shift_ref, taps_ref, inv_ref, inv_taps_ref, vq_ref, vd_ref,
             loss_ref, dg2_ref, db2_ref, token_ref,
             gx_ref, dwi_hbm, dwo_hbm, small_hbm, dwi16_hbm, dwo16_hbm,
             wi_v, wo_v, awi, awo, dyb_ref, later_ref, dbm_ref, gf_ref, dgf_ref, small_ref, copy_sems):
        xh_ref, zu_ref, mg_ref, vhat_ref, gv_ref, yhat_ref, rs_ref, xb_ref, pag_ref, y_ref, vnb_ref = _saved_views(f32_ref, bf16_ref, d, q)
        i = pl.program_id(0)

        @pl.when(i == 0)
        def _():
            loads = _start_copies(copy_sems, [(wi_hbm, wi_v), (wo_hbm, wo_v)])
            for r in (awi, awo, small_ref, dbm_ref, dgf_ref, dyb_ref, later_ref):
                r[...] = jnp.zeros_like(r)
            gf_ref[...] = _dot(taps_ref[...], _split(cwf_ref[...], True))
            for cp in loads:
                cp.wait()

        dr1b_parts, dproj_parts = [None] * nb, [None] * nb

        def sub_tile(b):
            rows = slice(b * CONV_BLOCK, (b + 1) * CONV_BLOCK)
            dx1v = dx1_ref[rows, :]
            xh = xh_ref[rows, :]
            rsv = rs_ref[rows, :]
            small_ref[SMALL_VD + VD_LN1_G:SMALL_VD + VD_LN1_G + 1, :] += _colsum(dx1v * xh)
            small_ref[SMALL_VD + VD_LN1_B:SMALL_VD + VD_LN1_B + 1, :] += _colsum(dx1v)
            dr1 = _ln_bwd(dx1v * vd_ref[VD_LN1_G:VD_LN1_G + 1, :], xh, rsv[:, RS_LN1:RS_LN1 + 1])
            dr1b = dr1.astype(BF16)
            yield
            dy = _dot_nt(dr1b, wo_v[...])
            yield
            vhat = vhat_ref[rows, :]
            sgu_g = vq_ref[VQ_SGU_G:VQ_SGU_G + 1, :]
            doa = dy[:, 0:q]
            dm = doa * zu_ref[rows, :]
            dpu = (doa * mg_ref[rows, :]).astype(BF16)
            acc = dm[0:CHUNK]
            for c in range(1, nc):
                acc = acc + dm[c * CHUNK:(c + 1) * CHUNK]
            dbm_ref[...] += acc
            pa = pag_ref[rows, 0:q].astype(F32)
            sg = _sigmoid(pag_ref[rows, q:2 * q].astype(F32))
            yhat = yhat_ref[rows, :]
            cln_g = vq_ref[VQ_CLN_G:VQ_CLN_G + 1, :]
            yn = yhat * cln_g + vq_ref[VQ_CLN_B:VQ_CLN_B + 1, :]
            sy = _sigmoid(yn)
            dyn = dy[:, q:2 * q] * (sy * (1.0 + yn * (1.0 - sy)))
            small_ref[SMALL_VQ + VQ_CLN_G:SMALL_VQ + VQ_CLN_G + 1, q:2 * q] += _colsum(dyn * yhat)
            small_ref[SMALL_VQ + VQ_CLN_B:SMALL_VQ + VQ_CLN_B + 1, q:2 * q] += _colsum(dyn)
            dyc = _ln_bwd(dyn * cln_g, yhat, rsv[:, RS_CONV:RS_CONV + 1])
            small_ref[SMALL_VQ + VQ_CONV_B:SMALL_VQ + VQ_CONV_B + 1, q:2 * q] += _colsum(dyc)
            dyb_ref[b, 0:CONV_BLOCK, :] = dyc
            yield
            wgrads = _mix_wgrad(dm, vnb_ref[rows, :], nc, n_pairs)
            dvn = _mix(wstt_ref, dm, nc, n_pairs)
            own = _dot(fwd_ref[...], _split(dyb_ref[b]))
            with_later = own + _dot(fwd_halo_ref[...], _split(later_ref[...]))
            later_ref[...] = dyb_ref[b, 0:HALO, :]
            yield
            for p, g in enumerate(wgrads):
                for half in range(2):
                    small_ref[0:CHUNK, (2 * p + half) * CHUNK:(2 * p + half + 1) * CHUNK] += g[half * CHUNK:(half + 1) * CHUNK]
            small_ref[SMALL_VQ + VQ_SGU_G:SMALL_VQ + VQ_SGU_G + 1, q:2 * q] += _colsum(dvn * vhat)
            small_ref[SMALL_VQ + VQ_SGU_B:SMALL_VQ + VQ_SGU_B + 1, q:2 * q] += _colsum(dvn)
            dpv = (_ln_bwd(dvn * sgu_g, vhat, rsv[:, RS_SGU:RS_SGU + 1]) * gv_ref[rows, :]).astype(BF16)
            dgf_ref[...] += _cmul(_cmul(own, shift_ref[...]), hf_ref[b * 2 * DFT_F:(b + 1) * 2 * DFT_F, :], conj_b=True)
            product = _split(_cmul(with_later, gf_ref[...], conj_b=True))
            yield
            dh = _dot(inv_ref[...], product)
            yield
            da = (dh * sg).astype(BF16)
            dg = (dh * pa * (sg * (1.0 - sg))).astype(BF16)
            yield
            gx = alpha * dr1
            for dpj, wj in zip((dpu, dpv, da, dg), range(4)):
                gx = gx + _dot_nt(dpj, wi_v[wj])
            gx_ref[rows, :] = gx
            dr1b_parts[b], dproj_parts[b] = dr1b, (dpu, dpv, da, dg)

        _interleave([sub_tile(b) for b in reversed(range(nb))])

        awo[...] += _dot_tn(y_ref[...], jnp.concatenate(dr1b_parts, axis=0))
        xb = xb_ref[...]
        for j in range(4):
            awi[j] += _dot_tn(xb, jnp.concatenate([part[j] for part in dproj_parts], axis=0))

        @pl.when(i == n - 1)
        def _():
            stores = _start_copies(copy_sems, [(awi, dwi_hbm), (awo, dwo_hbm)])
            wi_v[...] = awi[...].astype(BF16)
            wo_v[...] = awo[...].astype(BF16)
            stores += _start_copies(copy_sems, [(wi_v, dwi16_hbm), (wo_v, dwo16_hbm)], first=3)
            lane = lax.broadcasted_iota(jnp.int32, (CHUNK, LANES), 1)
            low = lane < HEAD_DIM
            dbs = jnp.zeros((CHUNK, LANES), F32)
            for p in range(n_pairs):
                grp = dbm_ref[:, p * LANES:(p + 1) * LANES]
                dbs = jnp.where(lane == 2 * p, jnp.sum(jnp.where(low, grp, 0.0), axis=1, keepdims=True), dbs)
                dbs = jnp.where(lane == 2 * p + 1, jnp.sum(jnp.where(low, 0.0, grp), axis=1, keepdims=True), dbs)
            tril = lax.broadcasted_iota(jnp.int32, (CHUNK, CHUNK), 0) >= lax.broadcasted_iota(jnp.int32, (CHUNK, CHUNK), 1)
            for h in range(2 * n_pairs):
                block = small_ref[0:CHUNK, h * CHUNK:(h + 1) * CHUNK]
                small_ref[0:CHUNK, h * CHUNK:(h + 1) * CHUNK] = jnp.where(tril, block, 0.0)
            small_ref[SMALL_VD + VD_LN2_G:SMALL_VD + VD_LN2_G + 1, :] = dg2_ref[...]
            small_ref[SMALL_VD + VD_LN2_B:SMALL_VD + VD_LN2_B + 1, :] = db2_ref[...]
            small_ref[SMALL_CW:SMALL_CW + kwp, 0:q] = _dot(inv_taps_ref[...], _split(dgf_ref[...], True))
            small_ref[SMALL_LOSS:SMALL_LOSS + SUBLANES, q:q + LANES] = loss_ref[...]
            small_ref[SMALL_BS:SMALL_BS + SUBLANES, q:q + LANES] = jnp.transpose(dbs)[0:SUBLANES]
            stores += _start_copies(copy_sems, [(small_ref, small_hbm)], first=2)
            for cp in stores:
                cp.wait()

    rev = lambda w: pl.BlockSpec((tm, w), lambda i: (n - 1 - i, 0))
    kwp = cwf.shape[0]
    small = jax.ShapeDtypeStruct((_small_rows(kwp), 2 * q), F32)
    small_ins = [wstt, cwf, tabs["fwd"], tabs["fwd_halo"], tabs["shift"], tabs["taps"], tabs["inv_in"], tabs["inv_taps"], vq, vd,
                 *mlp_small]
    return pl.pallas_call(
        body, name="bwd_mix", grid=(n,),
        in_specs=[rev(d), rev(saved_f32.shape[1]), rev(saved_bf16.shape[1]),
                  pl.BlockSpec((nb * 2 * DFT_F, q), lambda i: (n - 1 - i, 0)), ANY, ANY] + [_full(a.shape) for a in small_ins] + [ANY],
        out_specs=[rev(d)] + [ANY] * 5,
        out_shape=[jax.ShapeDtypeStruct((t, d), F32), jax.ShapeDtypeStruct(wi.shape, F32), jax.ShapeDtypeStruct(wo.shape, F32), small,
                   jax.ShapeDtypeStruct(wi.shape, BF16), jax.ShapeDtypeStruct(wo.shape, BF16)],
        scratch_shapes=[pltpu.VMEM(wi.shape, BF16), pltpu.VMEM(wo.shape, BF16), pltpu.VMEM(wi.shape, F32), pltpu.VMEM(wo.shape, F32),
                        pltpu.VMEM((nb, DFT_N, q), F32), pltpu.VMEM((HALO, q), F32),
                        pltpu.VMEM((CHUNK, q), F32), pltpu.VMEM((2 * DFT_F, q), F32), pltpu.VMEM((2 * DFT_F, q), F32),
                        pltpu.VMEM(small.shape, F32), pltpu.SemaphoreType.DMA((5,))],
        compiler_params=_cparams(),
    )(dx1, saved_f32, saved_bf16, hf_s, wi, wo, *small_ins, token)


def _prep(me_arr, w_in, w_out, w_gate_t, w_up_t, w_down, conv_w, kwp):
    kw, cshard = conv_w.shape

    def body(me_ref, wi_ref, wo_ref, wg_ref, wu_ref, wd_ref, cw_ref, oi, oo, og, ou, od, oc):
        for src, dst in ((wi_ref, oi), (wo_ref, oo), (wg_ref, og), (wu_ref, ou), (wd_ref, od)):
            dst[...] = src[...].astype(BF16)
        oc[0:kw, :] = cw_ref[...]
        oc[kw:kwp, :] = jnp.zeros((kwp - kw, cshard), F32)

    ins = (w_in, w_out, w_gate_t, w_up_t, w_down, conv_w)
    outs = [jax.ShapeDtypeStruct((4,) + a.shape, BF16) for a in ins[:5]] + [jax.ShapeDtypeStruct((4, kwp, cshard), F32)]
    grid_spec = pltpu.PrefetchScalarGridSpec(
        num_scalar_prefetch=1, grid=(1,),
        in_specs=[pl.BlockSpec(a.shape, lambda i, me: (0, 0)) for a in ins],
        out_specs=[pl.BlockSpec((None,) + o.shape[1:], lambda i, me: (me[0], 0, 0)) for o in outs])
    return pl.pallas_call(body, name="wprep", grid_spec=grid_spec, out_shape=outs, compiler_params=_cparams())(me_arr, *ins)


def _coords():
    return tuple(lax.axis_index(a) for a in MESH_AXES)


def _other_chips(x, y):
    return [(1 - x, y), (x, 1 - y), (1 - x, 1 - y)]


def _remote(src, dst, send_sem, recv_sem, to):
    return pltpu.make_async_remote_copy(src_ref=src, dst_ref=dst, send_sem=send_sem, recv_sem=recv_sem,
                                        device_id=to, device_id_type=MESH_ID)


def _hbm_call(body, name, ins, out_shape, scratch_shapes, aliases=None):
    return pl.pallas_call(
        body, name=name, in_specs=[ANY] * len(ins), out_specs=[ANY] * len(out_shape), out_shape=out_shape,
        scratch_shapes=scratch_shapes, input_output_aliases=aliases or {},
    )(*ins)


class _Gather:
    def __init__(self, bufs, send_sems, recv_sems):
        self.bufs, self.send_sems, self.recv_sems = bufs, send_sems, recv_sems
        self.x, self.y, self.c = _coords()

    def _copies(self, stage):
        x, y, c = self.x, self.y, self.c
        for a, buf in enumerate(self.bufs):
            hr = buf.shape[1] // 2
            for j, chip in enumerate(_other_chips(x, y)):
                if stage == "ici_out":
                    ref, k, to = buf.at[2 * x + y, pl.ds(c * hr, hr)], j, (*chip, c)
                elif stage == "ici_in":
                    ref, k, to = buf.at[2 * chip[0] + chip[1], pl.ds(c * hr, hr)], j, (*chip, c)
                elif stage == "d2d_out":
                    ref, k, to = buf.at[2 * chip[0] + chip[1], pl.ds(c * hr, hr)], 3 + j, (x, y, 1 - c)
                else:
                    ref, k, to = buf.at[2 * chip[0] + chip[1], pl.ds((1 - c) * hr, hr)], 3 + j, (x, y, 1 - c)
                yield _remote(ref, ref, self.send_sems.at[a, k], self.recv_sems.at[a, k], to)

    def start(self):
        for cp in self._copies("ici_out"):
            cp.start()

    def forward(self):
        for landed, onward in zip(self._copies("ici_in"), self._copies("d2d_out")):
            landed.wait_recv()
            onward.start()

    def finish(self):
        for cp in self._copies("d2d_in"):
            cp.wait_recv()
        for stage in ("ici_out", "d2d_out"):
            for cp in self._copies(stage):
                cp.wait_send()


def _gather_sems(n):
    return [pltpu.SemaphoreType.DMA((n, 6)), pltpu.SemaphoreType.DMA((n, 6))]


def _gather_shards(bufs):
    n = len(bufs)

    def body(*refs):
        g = _Gather(refs[n:2 * n], *refs[2 * n:])
        g.start()
        g.forward()
        g.finish()

    return _hbm_call(body, "gather_shards", bufs, [jax.ShapeDtypeStruct(s.shape, s.dtype) for s in bufs],
                     _gather_sems(n), aliases={a: a for a in range(n)})


def _pair_reduce(name, partials, payloads, c_arr, out_dtypes):
    n = len(partials)
    counts = [g.shape[0] for g in partials]
    first = [sum(counts[:a]) for a in range(n)]
    steps = sum(counts)

    def body(c_ref, *refs):
        own, travelling, out, land = (refs[k * n:(k + 1) * n] for k in range(4))
        send_sems, recv_sems = refs[4 * n:]
        i = pl.program_id(0)
        x, y, c = _coords()

        def copy(a, q):
            return _remote(travelling[a].at[q, 1 - c], land[a].at[q], send_sems.at[first[a] + q], recv_sems.at[first[a] + q],
                           (x, y, 1 - c))

        blocks = [(a, q) for a in range(n) for q in range(counts[a])]

        @pl.when(i == 0)
        def _():
            for a, q in blocks:
                copy(a, q).start()

        for a in range(n):
            @pl.when((i >= first[a]) & (i < first[a] + counts[a]))
            def _(a=a):
                q = i - first[a]
                copy(a, q).wait_recv()
                out[a][...] = (own[a][...] + land[a][q].astype(F32)).astype(out_dtypes[a])

        @pl.when(i == steps - 1)
        def _():
            for a, q in blocks:
                copy(a, q).wait_send()

    at = lambda a, i: jnp.clip(i - first[a], 0, counts[a] - 1)
    in_specs = [pl.BlockSpec((None, None) + g.shape[2:], lambda i, cr, a=a: (at(a, i), cr[0], 0, 0)) for a, g in enumerate(partials)]
    out_specs = [pl.BlockSpec((None,) + g.shape[2:], lambda i, cr, a=a: (at(a, i), 0, 0)) for a, g in enumerate(partials)]
    grid_spec = pltpu.PrefetchScalarGridSpec(
        num_scalar_prefetch=1, grid=(steps,), in_specs=in_specs + [ANY] * n, out_specs=out_specs,
        scratch_shapes=[pltpu.VMEM((g.shape[0],) + g.shape[2:], p.dtype) for g, p in zip(partials, payloads)]
        + [pltpu.SemaphoreType.DMA((steps,)), pltpu.SemaphoreType.DMA((steps,))])
    out_shape = [jax.ShapeDtypeStruct((g.shape[0],) + g.shape[2:], dt) for g, dt in zip(partials, out_dtypes)]
    return list(pl.pallas_call(body, name=name, grid_spec=grid_spec, out_shape=out_shape, compiler_params=_cparams())(
        c_arr, *partials, *payloads))


class _Exchange:
    def __init__(self, src, dst, send_sems, recv_sems):
        self.src, self.dst, self.send_sems, self.recv_sems = src, dst, send_sems, recv_sems
        self.x, self.y, self.c = _coords()

    def _copies(self, incoming):
        x, y, c = self.x, self.y, self.c
        for a, (s, d) in enumerate(zip(self.src, self.dst)):
            for j, chip in enumerate(_other_chips(x, y)):
                slot = 2 * chip[0] + chip[1]
                if incoming:
                    out, into = d.at[slot], d.at[slot]
                else:
                    out, into = (s.at[slot] if len(s.shape) == 3 else s), d.at[2 * x + y]
                yield _remote(out, into, self.send_sems.at[a, j], self.recv_sems.at[a, j], (*chip, c))

    def start(self):
        for cp in self._copies(False):
            cp.start()

    def finish(self):
        for cp in self._copies(True):
            cp.wait_recv()
        for cp in self._copies(False):
            cp.wait_send()


def _exchange_shapes(arrs):
    return [jax.ShapeDtypeStruct((4,) + s.shape[-2:], s.dtype) for s in arrs]


class _FlatSems:
    def __init__(self, ref):
        self.ref = ref

    @property
    def at(self):
        return self

    def __getitem__(self, idx):
        return self.ref.at[3 * idx[0] + idx[1]]


HBM = pl.BlockSpec(memory_space=pltpu.HBM)
SEM = pl.BlockSpec(memory_space=pltpu.SEMAPHORE)
DATAFLOW = pltpu.SideEffectType.DATAFLOW_SIDE_EFFECTING


def _exchange_start(name, arrs):
    n = len(arrs)
    lands = _exchange_shapes(arrs)

    def body(*refs):
        src, land = refs[:n], refs[n:2 * n]
        send_sems, recv_sems = refs[2 * n:2 * n + 2]
        token = refs[-1]
        _Exchange(src, land, _FlatSems(send_sems), _FlatSems(recv_sems)).start()
        token[...] = jnp.zeros_like(token)

    hbm = lambda a: pltpu.with_memory_space_constraint(a, pltpu.HBM)
    outs = pl.pallas_call(
        body, name=name,
        out_shape=(pltpu.SemaphoreType.DMA((3 * n,)), pltpu.SemaphoreType.DMA((3 * n,)),
                   *[pltpu.HBM(a.shape, a.dtype) for a in arrs], *[pltpu.HBM(s.shape, s.dtype) for s in lands],
                   jax.ShapeDtypeStruct((SUBLANES, LANES), F32)),
        in_specs=[HBM] * (2 * n), out_specs=(SEM, SEM, *[HBM] * (2 * n), pl.BlockSpec(memory_space=pltpu.VMEM)),
        input_output_aliases={a: 2 + a for a in range(2 * n)},
        compiler_params=pltpu.CompilerParams(has_side_effects=DATAFLOW),
    )(*[hbm(a) for a in arrs], *[hbm(lax.empty(s.shape, s.dtype)) for s in lands])
    return outs[:-1], outs[-1]


def _exchange_wait(name, started, after):
    send_sems, recv_sems, *bufs = started
    n = len(bufs) // 2

    def body(*refs):
        src, land = refs[:n], refs[n:2 * n]
        send_sems, recv_sems = refs[2 * n:2 * n + 2]
        _Exchange(src, land, _FlatSems(send_sems), _FlatSems(recv_sems)).finish()

    outs = pl.pallas_call(
        body, name=name,
        out_shape=tuple(pltpu.HBM(b.shape, b.dtype) for b in bufs),
        in_specs=[HBM] * (2 * n) + [SEM, SEM] + [ANY] * len(after), out_specs=tuple([HBM] * (2 * n)),
        input_output_aliases={a: a for a in range(2 * n)},
        compiler_params=pltpu.CompilerParams(has_side_effects=DATAFLOW),
    )(*bufs, send_sems, recv_sems, *after)
    return list(outs[:n]), list(outs[n:])


def _pair_gather(name, halves):
    n = len(halves)

    def body(*refs):
        src, dst = refs[:n], refs[n:2 * n]
        send_sems, recv_sems = refs[2 * n:]
        x, y, c = _coords()
        copies = [_remote(src[a], dst[a], send_sems.at[a], recv_sems.at[a], (x, y, 1 - c)) for a in range(n)]
        for cp in copies:
            cp.start()
        for cp in copies:
            cp.wait()

    outs = [jax.ShapeDtypeStruct(s.shape, s.dtype) for s in halves]
    return _hbm_call(body, name, halves, outs, [pltpu.SemaphoreType.DMA((n,)), pltpu.SemaphoreType.DMA((n,))])


def _chip_sum(a, parts, own, me_arr, after):
    _, hr, cc = parts.shape

    def body(me_ref, p_ref, own_ref, after_ref, o_ref):
        for mine in range(4):
            @pl.when(me_ref[0] == mine)
            def _():
                term = lambda j: (own_ref if j == mine else p_ref.at[j])[...].astype(F32)
                o_ref[...] = ((term(0) + term(1)) + term(2)) + term(3)

    own_spec = (pl.BlockSpec((None, hr, cc), lambda i, me: (me[0], 0, 0)) if own.ndim == 3
                else pl.BlockSpec((hr, cc), lambda i, me: (0, 0)))
    grid_spec = pltpu.PrefetchScalarGridSpec(
        num_scalar_prefetch=1, grid=(1,),
        in_specs=[pl.BlockSpec((4, hr, cc), lambda i, me: (0, 0, 0)), own_spec, ANY],
        out_specs=pl.BlockSpec((hr, cc), lambda i, me: (0, 0)))
    return pl.pallas_call(body, name=f"chip_sum_{a}", grid_spec=grid_spec, out_shape=jax.ShapeDtypeStruct((hr, cc), F32),
                          compiler_params=_cparams())(me_arr, parts, own, after)


def _row_block(rows, cols, limit=1 << 20):
    best = 8
    for tr in range(8, rows + 1, 8):
        if rows % tr == 0 and tr * cols * 4 <= limit:
            best = tr
    return best


def _adamw(name, w, g_mine, g_other, m, v, c_arr):
    r, c = w.shape
    hr, cg = g_mine.shape
    tr = hr if r % hr == 0 and hr * cg * 4 <= (3 << 19) else math.gcd(_row_block(hr, cg), r)
    per_half = hr // tr
    bc1 = 1.0 - ADAM_B1 ** ADAM_STEP
    bc2 = 1.0 - ADAM_B2 ** ADAM_STEP

    def body(c_ref, w_ref, gm_ref, go_ref, m_ref, v_ref, go, do, mo, vo):
        gv = jnp.where(pl.program_id(0) // per_half == c_ref[0], gm_ref[:, 0:c], go_ref[:, 0:c])
        mn = ADAM_B1 * m_ref[...] + (1.0 - ADAM_B1) * gv
        vn = ADAM_B2 * v_ref[...] + (1.0 - ADAM_B2) * (gv * gv)
        go[...] = gv
        mo[...] = mn
        vo[...] = vn
        do[...] = -ADAM_LR * ((mn / bc1) / (jnp.sqrt(vn / bc2) + ADAM_EPS) + ADAM_WD * w_ref[...])

    blk = pl.BlockSpec((tr, c), lambda i, cr: (i, 0))
    gblk = pl.BlockSpec((tr, cg), lambda i, cr: (i % per_half, 0))
    grid_spec = pltpu.PrefetchScalarGridSpec(num_scalar_prefetch=1, grid=(r // tr,), in_specs=[blk, gblk, gblk, blk, blk],
                                             out_specs=[blk] * 4)
    return pl.pallas_call(body, name=f"adamw_{name}", grid_spec=grid_spec, out_shape=[jax.ShapeDtypeStruct((r, c), F32)] * 4,
                          compiler_params=_cparams())(c_arr, w, g_mine, g_other, m, v)


SMALL_Q = ("sgu_ln_g", "sgu_ln_b", "conv_b", "conv_ln_g", "conv_ln_b")
SMALL_D = ("ln1_g", "ln1_b", "ln2_g", "ln2_b")


def _adamw_small(g_mine, g_other, c_arr, me_arr, params):
    names = list(SMALL_Q) + list(SMALL_D) + ["w_s", "b_s", "conv_w"]
    hr, width = g_mine.shape
    q = width // 2
    heads = params["w_s"][0].shape[1]
    kw, cshard = params["conv_w"][0].shape[1:]
    bc1 = 1.0 - ADAM_B1 ** ADAM_STEP
    bc2 = 1.0 - ADAM_B2 ** ADAM_STEP

    def update(w, g, m, v):
        mn = ADAM_B1 * m + (1.0 - ADAM_B1) * g
        vn = ADAM_B2 * v + (1.0 - ADAM_B2) * (g * g)
        return g, -ADAM_LR * ((mn / bc1) / (jnp.sqrt(vn / bc2) + ADAM_EPS) + ADAM_WD * w), mn, vn

    def body(c_ref, me_ref, gm_ref, go_ref, *refs):
        ins = {nm: refs[3 * k:3 * k + 3] for k, nm in enumerate(names)}
        outs = {nm: refs[3 * len(names) + 4 * k:3 * len(names) + 4 * k + 4] for k, nm in enumerate(names)}
        loss_ref, cw_ref = refs[-2:]
        first, second = gm_ref[...], go_ref[...]
        low = c_ref[0] == 0
        g_all = jnp.concatenate([jnp.where(low, first, second), jnp.where(low, second, first)], axis=0)

        def apply(nm, g, at):
            w, m, v = (r[at] for r in ins[nm])
            for o, val in zip(outs[nm], update(w, g, m, v)):
                o[at] = val

        for row, nm in enumerate(SMALL_Q):
            apply(nm, g_all[SMALL_VQ + row:SMALL_VQ + row + 1, q:2 * q], ...)
        for row, nm in enumerate(SMALL_D):
            apply(nm, g_all[SMALL_VD + row:SMALL_VD + row + 1, :], ...)
        for h in range(heads):
            apply("w_s", g_all[0:CHUNK, h * CHUNK:(h + 1) * CHUNK], (0, h))
        apply("b_s", g_all[SMALL_BS:SMALL_BS + heads, q:q + LANES], 0)
        cw_ref[...] = jnp.zeros_like(cw_ref)
        for chip in range(4):
            @pl.when(me_ref[0] == chip)
            def _():
                cw_ref[...] = g_all[SMALL_CW:SMALL_CW + cw_ref.shape[0], chip * cshard:(chip + 1) * cshard]
        apply("conv_w", cw_ref[0:kw, :], 0)
        loss_ref[...] = g_all[SMALL_LOSS:SMALL_LOSS + SUBLANES, q:q + LANES]

    arrays = [a for nm in names for a in params[nm]]
    out_shape = [jax.ShapeDtypeStruct(params[nm][0].shape, F32) for nm in names for _ in range(4)] + [jax.ShapeDtypeStruct((SUBLANES, LANES), F32)]
    whole = lambda shape: pl.BlockSpec(shape, lambda i, c, me: (0,) * len(shape))
    grid_spec = pltpu.PrefetchScalarGridSpec(
        num_scalar_prefetch=2, grid=(1,),
        in_specs=[whole(g_mine.shape), whole(g_other.shape)] + [whole(a.shape) for a in arrays],
        out_specs=[whole(s.shape) for s in out_shape],
        scratch_shapes=[pltpu.VMEM((-(-kw // SUBLANES) * SUBLANES, cshard), F32)])
    res = pl.pallas_call(body, name="adamw_small", grid_spec=grid_spec, out_shape=out_shape, compiler_params=_cparams())(
        c_arr, me_arr, g_mine, g_other, *arrays)
    return {nm: list(res[4 * k:4 * k + 4]) for k, nm in enumerate(names)}, res[-1]


def _pad_rows(a, rows):
    return jnp.pad(a, ((0, rows - a.shape[0]), (0, 0)))


def kernel(x, w_in, sgu_ln_g, sgu_ln_b, w_s, b_s, conv_w, conv_b, conv_ln_g, conv_ln_b, w_out, ln1_g, ln1_b, w_gate, w_up, w_down, ln2_g, ln2_b, loss_target, m_w_in, m_sgu_ln_g, m_sgu_ln_b, m_w_s, m_b_s, m_conv_w, m_conv_b, m_conv_ln_g, m_conv_ln_b, m_w_out, m_ln1_g, m_ln1_b, m_w_gate, m_w_up, m_w_down, m_ln2_g, m_ln2_b, v_w_in, v_sgu_ln_g, v_sgu_ln_b, v_w_s, v_b_s, v_conv_w, v_conv_b, v_conv_ln_g, v_conv_ln_b, v_w_out, v_ln1_g, v_ln1_b, v_w_gate, v_w_up, v_w_down, v_ln2_g, v_ln2_b):
    depth, d, q = w_in.shape
    assert depth == 1 and x.shape[0] == 1
    t = x.shape[1]
    heads = w_s.shape[1]
    kw, cshard = conv_w.shape[1], conv_w.shape[2]
    fs = w_gate.shape[2]
    slabs = _hidden_slabs(4 * fs)
    n_pairs = q // LANES
    assert heads * HEAD_DIM == q and q % LANES == 0 and w_s.shape[2] == CHUNK and 4 * cshard == q and kw - 1 <= HALO
    alpha = (2.0 * depth) ** 0.25
    tm = min(512, t)
    assert t % tm == 0 and tm % CHUNK == 0
    x2, tgt = x[0], loss_target[0]
    mx, my, mc = _coords()
    me = 2 * mx + my
    c_arr = jnp.reshape(mc, (1,)).astype(jnp.int32)

    kwp = -(-kw // 16) * 16
    me_arr = jnp.reshape(me, (1,)).astype(jnp.int32)
    wi, wo, wg, wu, wd, cw4 = _prep(me_arr, w_in[0], w_out[0], w_gate[0].T, w_up[0].T, w_down[0], conv_w[0], kwp)
    wi, wo, cw4 = _gather_shards([wi, wo, cw4])
    wo = wo.reshape(d, d)
    cw = jnp.transpose(cw4, (1, 0, 2)).reshape(kwp, q)
    cwf = _pad_rows(cw[:kw][::-1], kwp)
    tabs = {name: jnp.asarray(tab) for name, tab in _dft_tables(kw, kwp, q).items()}

    wm = jnp.where(jnp.tril(jnp.ones((CHUNK, CHUNK), bool)), w_s[0], 0.0)
    wst = wm.reshape(n_pairs, 2 * CHUNK, CHUNK).astype(BF16)
    wstt = jnp.transpose(wm, (0, 2, 1)).reshape(n_pairs, 2 * CHUNK, CHUNK).astype(BF16)
    bmat = jnp.repeat(b_s[0].T, HEAD_DIM, axis=1)
    vq = _pad_rows(jnp.concatenate([sgu_ln_g, sgu_ln_b, conv_b, conv_ln_g, conv_ln_b], axis=0), 8)
    vd = _pad_rows(jnp.concatenate([ln1_g, ln1_b, ln2_g, ln2_b], axis=0), 8)

    *saved, wg, wu, wd = _fwd_mix(x2, wi, wo, wst, bmat, cwf, tabs, vq, vd, [wg, wu, wd], alpha, tm)
    wg, wu, wd = (w.reshape(4 * fs, d) for w in (wg, wu, wd))
    *acts, x1b, dr2, loss_part, dg2, db2 = _fwd_mlp(saved[0], tgt, wg, wu, wd, vd, alpha, slabs, tm)
    mlp_grads = None
    for j, slab in enumerate(slabs):
        mlp_grads = _bwd_mlp_slab(j, slab, dr2, mlp_grads, x1b, acts[j], wg, wu, wd, alpha, tm)
    dx1 = mlp_grads[0]
    mlp_halves = [b.reshape(4, 2, fs // 2, d) for b in mlp_grads[1:]]
    mlp_sums = _pair_reduce("pair_reduce_mlp", mlp_halves[:3], mlp_halves[3:], c_arr, [BF16] * 3)
    mlp_started, token = _exchange_start("exchange_mlp_start", mlp_sums)
    grad_x, dwi, dwo, small, dwi16, dwo16 = _bwd_mix(dx1, saved, wi, wo, wstt, cwf, tabs, vq, vd, (loss_part, dg2, db2), token, alpha, tm)
    mlp_sums, mlp_parts = _exchange_wait("exchange_mlp_wait", mlp_started, [dwo])

    by_halves = lambda b: b.reshape(4, 2, b.shape[1] // 2, b.shape[2])
    halves = [by_halves(dwi), by_halves(dwo.reshape(4, d // 4, d)), small.reshape(1, 2, small.shape[0] // 2, small.shape[1])]
    travelling = [by_halves(dwi16), by_halves(dwo16.reshape(4, d // 4, d)), halves[-1]]
    *sums, small_sum = _pair_reduce("pair_reduce_mix", halves, travelling, c_arr, [BF16, BF16, F32])
    sums.append(small_sum[0])
    mix_started, token = _exchange_start("exchange_mix_start", sums)

    out, raw = {}, {}

    def finish(first, names, parts, sums, after):
        mine = [_chip_sum(first + a, p, s, me_arr, after) for a, (p, s) in enumerate(zip(parts, sums))]
        other = _pair_gather(f"pair_gather_{first}", mine)
        for a, nm in enumerate(names):
            w_, m_, v_ = weights[nm]
            if nm in ("w_gate", "w_up"):
                raw[nm] = _adamw(nm, w_[0].T, mine[a], other[a], m_[0].T, v_[0].T, c_arr)
                out[nm] = [o.T for o in raw[nm]]
            else:
                raw[nm] = out[nm] = _adamw(nm, w_[0], mine[a], other[a], m_[0], v_[0], c_arr)
        return mine[-1], other[-1]

    weights = {"w_in": (w_in, m_w_in, v_w_in), "w_out": (w_out, m_w_out, v_w_out), "w_gate": (w_gate, m_w_gate, v_w_gate),
               "w_up": (w_up, m_w_up, v_w_up), "w_down": (w_down, m_w_down, v_w_down)}
    finish(2, ["w_gate", "w_up", "w_down"], mlp_parts, mlp_sums, token)
    sums, parts = _exchange_wait("exchange_mix_wait", mix_started, [raw[nm][1] for nm in ("w_gate", "w_up", "w_down")])
    small_mine, small_other = finish(5, ["w_in", "w_out"], parts, sums, parts[0])

    small_params = {
        "sgu_ln_g": (sgu_ln_g, m_sgu_ln_g, v_sgu_ln_g), "sgu_ln_b": (sgu_ln_b, m_sgu_ln_b, v_sgu_ln_b),
        "conv_b": (conv_b, m_conv_b, v_conv_b), "conv_ln_g": (conv_ln_g, m_conv_ln_g, v_conv_ln_g),
        "conv_ln_b": (conv_ln_b, m_conv_ln_b, v_conv_ln_b), "ln1_g": (ln1_g, m_ln1_g, v_ln1_g), "ln1_b": (ln1_b, m_ln1_b, v_ln1_b),
        "ln2_g": (ln2_g, m_ln2_g, v_ln2_g), "ln2_b": (ln2_b, m_ln2_b, v_ln2_b), "w_s": (w_s, m_w_s, v_w_s),
        "b_s": (b_s, m_b_s, v_b_s), "conv_w": (conv_w, m_conv_w, v_conv_w)}
    small_out, loss_block = _adamw_small(small_mine, small_other, c_arr, me_arr, small_params)
    loss = loss_block[0, 0]
    names = ["w_in", "sgu_ln_g", "sgu_ln_b", "w_s", "b_s", "conv_w", "conv_b", "conv_ln_g", "conv_ln_b", "w_out",
             "ln1_g", "ln1_b", "w_gate", "w_up", "w_down", "ln2_g", "ln2_b"]
    result = [loss, grad_x[None]]
    for kind in range(4):
        for nm in names:
            result.append(out[nm][kind][None] if nm in out else small_out[nm][kind])
    return tuple(result)
```

```python
import math

import jax
import numpy as np
import jax.numpy as jnp
from jax import lax
from jax.experimental import pallas as pl
from jax.experimental.pallas import tpu as pltpu

F32 = jnp.float32
BF16 = jnp.bfloat16

LN_EPS = 1e-5
HEAD_DIM = 64
CHUNK = 128
HALO = 32
LANES = 128
MXU_N = 256
ADAM_LR, ADAM_B1, ADAM_B2, ADAM_EPS, ADAM_WD, ADAM_STEP = 0.001, 0.9, 0.999, 1e-08, 0.01, 10
VMEM_LIMIT = 63 * 1024 * 1024
MESH_AXES = ("x", "y", "c")
MESH_ID = pl.DeviceIdType.MESH


def _dot(a, b):
    return jnp.dot(a, b, preferred_element_type=F32)


def _dot_nt(a, b):
    return lax.dot_general(a, b, (((1,), (1,)), ((), ())), preferred_element_type=F32)


def _dot_tn(a, b):
    return lax.dot_general(a, b, (((0,), (0,)), ((), ())), preferred_element_type=F32)


def _sigmoid(v):
    return 1.0 / (1.0 + jnp.exp(-v))


def _gelu(v):
    cdf = 0.5 * (1.0 + lax.erf(v * (1.0 / math.sqrt(2.0))))
    pdf = jnp.exp(-0.5 * v * v) * (1.0 / math.sqrt(2.0 * math.pi))
    return v * cdf, cdf + v * pdf


def _ln_stats(v):
    mu = jnp.mean(v, axis=-1, keepdims=True)
    d = v - mu
    rstd = lax.rsqrt(jnp.mean(d * d, axis=-1, keepdims=True) + LN_EPS)
    return d * rstd, rstd


def _ln_bwd(dxhat, xhat, rstd):
    m1 = jnp.mean(dxhat, axis=-1, keepdims=True)
    m2 = jnp.mean(dxhat * xhat, axis=-1, keepdims=True)
    return rstd * (dxhat - m1 - xhat * m2)


def _colsum(v):
    return jnp.sum(v, axis=0, keepdims=True)


def _pair_lanes(v, nc, p):
    return jnp.concatenate([v[c * CHUNK:(c + 1) * CHUNK, p * LANES:(p + 1) * LANES] for c in range(nc)], axis=1)


def _unpair(parts, nc):
    rows = [jnp.concatenate([part[:, c * LANES:(c + 1) * LANES] for part in parts], axis=1) for c in range(nc)]
    return jnp.concatenate(rows, axis=0)


def _low_head(nc):
    lane = lax.broadcasted_iota(jnp.int32, (CHUNK, nc * LANES), 1)
    return (lane & (LANES - 1)) < HEAD_DIM


def _mix(wst_ref, v, nc, n_pairs):
    vb = v.astype(BF16)
    low = _low_head(nc)
    parts = []
    for p in range(n_pairs):
        r = _dot(wst_ref[p], _pair_lanes(vb, nc, p))
        parts.append(jnp.where(low, r[:CHUNK], r[CHUNK:]))
    return _unpair(parts, nc)


def _mix_wgrad(dm, vn, nc, n_pairs):
    low = _low_head(nc)
    vb = vn.astype(BF16)
    out = []
    for p in range(n_pairs):
        a = _pair_lanes(dm, nc, p)
        lhs = jnp.concatenate([jnp.where(low, a, 0.0), jnp.where(low, 0.0, a)], axis=0).astype(BF16)
        out.append(_dot_nt(lhs, _pair_lanes(vb, nc, p)))
    return out


SUBLANES = 8


CONV_BLOCK = 256
DFT_N = CONV_BLOCK + HALO
DFT_F = -(-(DFT_N // 2 + 1) // SUBLANES) * SUBLANES


def _terms(m, exact):
    hi = m.astype(np.float32).astype(BF16)
    lo = (m.astype(np.float32) - hi.astype(np.float32)).astype(BF16)
    return np.concatenate([hi, hi, lo] if exact else [hi], axis=1)


def _split(v, exact=False):
    hi = v.astype(BF16)
    if not exact:
        return hi
    lo = (v - hi.astype(F32)).astype(BF16)
    return jnp.concatenate([hi, lo, hi], axis=0)


def _dft_tables(kw, kwp, q):
    nf = DFT_N // 2 + 1
    ang = 2.0 * np.pi * np.arange(nf)[:, None] * np.arange(DFT_N)[None, :] / DFT_N
    fwd = np.zeros((2 * DFT_F, DFT_N))
    fwd[:nf], fwd[DFT_F:DFT_F + nf] = np.cos(ang), -np.sin(ang)
    weight = np.full((nf, 1), 2.0 / DFT_N)
    weight[0] = weight[-1] = 1.0 / DFT_N
    inv = np.zeros((DFT_N, 2 * DFT_F))
    inv[:, :nf], inv[:, DFT_F:DFT_F + nf] = (np.cos(ang) * weight).T, (-np.sin(ang) * weight).T
    inv_taps = np.zeros((kwp, 2 * DFT_F))
    inv_taps[:kw] = inv[kw - 1::-1][:kw]
    shift = np.zeros((2 * DFT_F, q), np.float32)
    shift[:nf], shift[DFT_F:DFT_F + nf] = np.cos(ang[:, HALO:HALO + 1]), -np.sin(ang[:, HALO:HALO + 1])
    return {"fwd": _terms(fwd, False), "fwd_halo": _terms(fwd[:, CONV_BLOCK:], False), "shift": shift,
            "inv_out": _terms(inv[HALO:HALO + CONV_BLOCK], False), "inv_in": _terms(inv[:CONV_BLOCK], False),
            "taps": _terms(fwd[:, :kwp], True), "inv_taps": _terms(inv_taps, True)}


def _cmul(a, b, conj_b=False):
    ar, ai, br, bi = a[:DFT_F], a[DFT_F:], b[:DFT_F], b[DFT_F:]
    if conj_b:
        return jnp.concatenate([ar * br + ai * bi, ai * br - ar * bi], axis=0)
    return jnp.concatenate([ar * br - ai * bi, ar * bi + ai * br], axis=0)


def _interleave(sub_tiles):
    waiting, live = list(sub_tiles), []
    while waiting or live:
        if waiting:
            live.append(waiting.pop(0))
        for g in list(live):
            try:
                next(g)
            except StopIteration:
                live.remove(g)


def _start_copies(sems, pairs, first=0):
    copies = [pltpu.make_async_copy(src, dst, sems.at[first + k]) for k, (src, dst) in enumerate(pairs)]
    for cp in copies:
        cp.start()
    return copies


def _cparams():
    return pltpu.CompilerParams(dimension_semantics=("arbitrary",), vmem_limit_bytes=VMEM_LIMIT)


def _full(shape):
    return pl.BlockSpec(shape, lambda i: (0,) * len(shape))


ANY = pl.BlockSpec(memory_space=pl.ANY)

VQ_SGU_G, VQ_SGU_B, VQ_CONV_B, VQ_CLN_G, VQ_CLN_B = range(5)
VD_LN1_G, VD_LN1_B, VD_LN2_G, VD_LN2_B = range(4)
RS_LN1, RS_SGU, RS_CONV = range(3)
RS_COLS = LANES


def _saved_widths(d, q):
    f32 = [d, q, q, q, q, q, RS_COLS]
    bf16 = [d, 2 * q, d, q]
    return f32, bf16


def _saved_views(f32_ref, bf16_ref, d, q):
    views = []
    for ref, widths in zip((f32_ref, bf16_ref), _saved_widths(d, q)):
        for k, w in enumerate(widths):
            views.append(ref.at[pl.ds(0, ref.shape[0]), pl.ds(sum(widths[:k]), w)])
    return views
def _fwd_mix(x, wi, wo, wst, bmat, cwf, tabs, vq, vd, mlp_w, alpha, tm):
    t, d = x.shape
    q = wi.shape[2]
    nc, n_pairs = CONV_BLOCK // CHUNK, q // LANES
    n = t // tm
    n_in, n_saved = 11, 3
    assert tm % CONV_BLOCK == 0

    def body(x_ref, wi_hbm, wo_hbm, wst_ref, bmat_ref, cwf_ref, fwd_ref, taps_ref, inv_ref, vq_ref, vd_ref, *rest):
        f32_ref, bf16_ref, hf_ref = rest[3:3 + n_saved]
        xh_ref, zu_ref, mg_ref, vhat_ref, gv_ref, yhat_ref, rs_ref, xb_ref, pag_ref, y_ref, vnb_ref = _saved_views(f32_ref, bf16_ref, d, q)
        gathered = rest[3 + n_saved:6 + n_saved]
        wi_v, wo_v, hb_ref, gf_ref, send_sems, recv_sems, copy_sems = rest[6 + n_saved:]
        step = pl.program_id(0)

        @pl.when(step == 0)
        def _():
            loads = _start_copies(copy_sems, [(wi_hbm, wi_v), (wo_hbm, wo_v)])
            _Gather(gathered, send_sems, recv_sems).start()
            hb_ref[...] = jnp.zeros_like(hb_ref)
            gf_ref[...] = _dot(taps_ref[...], _split(cwf_ref[...], True))
            for cp in loads:
                cp.wait()

        @pl.when(step == (3 * n) // 4)
        def _():
            _Gather(gathered, send_sems, recv_sems).forward()

        def sub_tile(b):
            rows = slice(b * CONV_BLOCK, (b + 1) * CONV_BLOCK)
            xv = x_ref[rows, :]
            xb = xv.astype(BF16)
            xb_ref[rows, :] = xb
            pu, pv, pa, pg = (_dot(xb, wi_v[j]) for j in range(4))
            yield
            pag_ref[rows, 0:q] = pa.astype(BF16)
            pag_ref[rows, q:2 * q] = pg.astype(BF16)
            zu, gu = _gelu(pu)
            zv, gv = _gelu(pv)
            vhat, rstd_v = _ln_stats(zv)
            vnb = (vhat * vq_ref[VQ_SGU_G:VQ_SGU_G + 1, :] + vq_ref[VQ_SGU_B:VQ_SGU_B + 1, :]).astype(BF16)
            hb_ref[HALO + b * CONV_BLOCK:HALO + (b + 1) * CONV_BLOCK, :] = pa * _sigmoid(pg)
            yield
            mixed = _mix(wst_ref, vnb, nc, n_pairs) + jnp.concatenate([bmat_ref[...]] * nc, axis=0)
            spectrum = _dot(fwd_ref[...], _split(hb_ref[b * CONV_BLOCK:b * CONV_BLOCK + DFT_N, :]))
            yield
            y_ref[rows, 0:q] = (zu * mixed).astype(BF16)
            zu_ref[rows, :] = zu
            mg_ref[rows, :] = mixed * gu
            vhat_ref[rows, :] = vhat
            gv_ref[rows, :] = gv
            vnb_ref[rows, :] = vnb
            hf_ref[b * 2 * DFT_F:(b + 1) * 2 * DFT_F, :] = spectrum
            product = _split(_cmul(gf_ref[...], spectrum))
            yield
            yc = _dot(inv_ref[...], product) + vq_ref[VQ_CONV_B:VQ_CONV_B + 1, :]
            yield
            yhat, rstd_c = _ln_stats(yc)
            yhat_ref[rows, :] = yhat
            yn = yhat * vq_ref[VQ_CLN_G:VQ_CLN_G + 1, :] + vq_ref[VQ_CLN_B:VQ_CLN_B + 1, :]
            y_ref[rows, q:2 * q] = (yn * _sigmoid(yn)).astype(BF16)
            yield
            r1 = alpha * xv + _dot(y_ref[rows, :], wo_v[...])
            yield
            xhat, rstd1 = _ln_stats(r1)
            xh_ref[rows, :] = xhat
            col = lax.broadcasted_iota(jnp.int32, (CONV_BLOCK, RS_COLS), 1)
            rs_ref[rows, :] = jnp.where(col == RS_LN1, rstd1, jnp.where(col == RS_SGU, rstd_v, jnp.where(col == RS_CONV, rstd_c, 0.0)))

        _interleave([sub_tile(b) for b in range(tm // CONV_BLOCK)])
        hb_ref[0:HALO, :] = hb_ref[tm:tm + HALO, :]

        @pl.when(step == n - 1)
        def _():
            _Gather(gathered, send_sems, recv_sems).finish()

    row = lambda w: pl.BlockSpec((tm, w), lambda i: (i, 0))
    widths = [(sum(w), dt) for w, dt in zip(_saved_widths(d, q), (F32, BF16))]
    small_ins = [wst, bmat, cwf, tabs["fwd"], tabs["taps"], tabs["inv_out"], vq, vd]
    return pl.pallas_call(
        body, name="fwd_mix", grid=(n,),
        in_specs=[row(d), ANY, ANY] + [_full(a.shape) for a in small_ins] + [ANY] * 3,
        out_specs=[row(w) for w, _ in widths] + [pl.BlockSpec((tm // CONV_BLOCK * 2 * DFT_F, q), lambda i: (i, 0))] + [ANY] * 3,
        out_shape=[jax.ShapeDtypeStruct((t, w), dt) for w, dt in widths] + [jax.ShapeDtypeStruct((t // CONV_BLOCK * 2 * DFT_F, q), F32)]
        + [jax.ShapeDtypeStruct(b.shape, b.dtype) for b in mlp_w],
        scratch_shapes=[pltpu.VMEM(wi.shape, BF16), pltpu.VMEM(wo.shape, BF16), pltpu.VMEM((HALO + tm, q), F32),
                        pltpu.VMEM((2 * DFT_F, q), F32)] + _gather_sems(3) + [pltpu.SemaphoreType.DMA((2,))],
        input_output_aliases={n_in + a: n_saved + a for a in range(3)},
        compiler_params=_cparams(),
    )(x, wi, wo, *small_ins, *mlp_w)


MLP_SLABS = 4


def _hidden_slabs(f):
    assert f % MXU_N == 0
    tiles = f // MXU_N
    sizes = [(tiles // MLP_SLABS + (1 if j < tiles % MLP_SLABS else 0)) * MXU_N for j in range(MLP_SLABS)]
    return [(sum(sizes[:j]), sz) for j, sz in enumerate(sizes) if sz]


def _fwd_mlp(saved_f32, tgt, wg, wu, wd, vd, alpha, slabs, tm):
    t, d = tgt.shape
    n = t // tm
    ns = len(slabs)
    half = tm // 2 if tm % 32 == 0 else tm

    def body(xh_ref, tgt_ref, wg_hbm, wu_hbm, wd_hbm, vd_ref, *rest):
        gp_refs = [r.at[pl.ds(0, tm), pl.ds(0, sz)] for r, (_, sz) in zip(rest[:ns], slabs)]
        up_refs = [r.at[pl.ds(0, tm), pl.ds(sz, sz)] for r, (_, sz) in zip(rest[:ns], slabs)]
        x1b_ref, dr2_ref, loss_ref, dg2_ref, db2_ref, wg_v, wu_v, wd_v, copy_sems = rest[ns:]

        @pl.when(pl.program_id(0) == 0)
        def _():
            loads = _start_copies(copy_sems, [(wg_hbm, wg_v), (wu_hbm, wu_v), (wd_hbm, wd_v)])
            loss_ref[...] = jnp.zeros_like(loss_ref)
            dg2_ref[...] = jnp.zeros_like(dg2_ref)
            db2_ref[...] = jnp.zeros_like(db2_ref)
            for cp in loads:
                cp.wait()

        g2 = vd_ref[VD_LN2_G:VD_LN2_G + 1, :]

        for r0 in range(0, tm, half):
            rows = slice(r0, r0 + half)
            x1 = xh_ref[rows, :] * vd_ref[VD_LN1_G:VD_LN1_G + 1, :] + vd_ref[VD_LN1_B:VD_LN1_B + 1, :]
            x1b = x1.astype(BF16)
            x1b_ref[rows, :] = x1b
            acc = alpha * x1
            for (off, sz), gp_ref, up_ref in zip(slabs, gp_refs, up_refs):
                gp = _dot_nt(x1b, wg_v[off:off + sz, :])
                up = _dot_nt(x1b, wu_v[off:off + sz, :])
                gp_ref[rows, :] = gp.astype(BF16)
                up_ref[rows, :] = up.astype(BF16)
                acc = acc + _dot((gp * _sigmoid(gp) * up).astype(BF16), wd_v[off:off + sz, :])
            xh2, rstd2 = _ln_stats(acc)
            err = xh2 * g2 + vd_ref[VD_LN2_B:VD_LN2_B + 1, :] - tgt_ref[rows, :]
            loss_ref[...] += _colsum(jnp.sum(err * err, axis=1, keepdims=True)) * (0.5 / d)
            dy = err * (1.0 / d)
            dg2_ref[...] += _colsum(dy * xh2)
            db2_ref[...] += _colsum(dy)
            dr2_ref[rows, :] = _ln_bwd(dy * g2, xh2, rstd2)

    row = lambda w: pl.BlockSpec((tm, w), lambda i: (i, 0))
    act = [2 * sz for _, sz in slabs]
    return pl.pallas_call(
        body, name="fwd_mlp", grid=(n,),
        in_specs=[row(d), row(d), ANY, ANY, ANY, _full(vd.shape)],
        out_specs=[row(sz) for sz in act] + [row(d), row(d), _full((8, LANES)), _full((1, d)), _full((1, d))],
        out_shape=[jax.ShapeDtypeStruct((t, sz), BF16) for sz in act]
        + [jax.ShapeDtypeStruct((t, d), BF16), jax.ShapeDtypeStruct((t, d), F32),
           jax.ShapeDtypeStruct((8, LANES), F32), jax.ShapeDtypeStruct((1, d), F32), jax.ShapeDtypeStruct((1, d), F32)],
        scratch_shapes=[pltpu.VMEM(wg.shape, BF16), pltpu.VMEM(wu.shape, BF16), pltpu.VMEM(wd.shape, BF16), pltpu.SemaphoreType.DMA((3,))],
        compiler_params=_cparams(),
    )(saved_f32, tgt, wg, wu, wd, vd)


def _bwd_mlp_slab(j, slab, dr2, prev, x1b, gate_up, wg, wu, wd, alpha, tm):
    t, d = dr2.shape
    off, sz = slab
    n = t // tm
    first = prev is None

    def body(*refs):
        if first:
            dr_ref, x1b_ref, gu_ref, wg_hbm, wu_hbm, wd_hbm = refs[:6]
        else:
            dr_ref, dxp_ref, x1b_ref, gu_ref, wg_hbm, wu_hbm, wd_hbm = refs[:7]
        dx_ref, dwg_hbm, dwu_hbm, dwd_hbm, dwg16_hbm, dwu16_hbm, dwd16_hbm, ag, au, ad, wg_v, wu_v, wd_v, copy_sems = refs[-14:]

        @pl.when(pl.program_id(0) == 0)
        def _():
            loads = _start_copies(copy_sems, [(src.at[pl.ds(off, sz)], dst) for src, dst in ((wg_hbm, wg_v), (wu_hbm, wu_v), (wd_hbm, wd_v))])
            ag[...] = jnp.zeros_like(ag)
            au[...] = jnp.zeros_like(au)
            ad[...] = jnp.zeros_like(ad)
            for cp in loads:
                cp.wait()

        dr = dr_ref[...]
        drb = dr.astype(BF16)
        x1b = x1b_ref[...]
        gpv = gu_ref[:, 0:sz].astype(F32)
        upv = gu_ref[:, sz:2 * sz].astype(F32)
        dh = _dot_nt(drb, wd_v[...])
        sg = _sigmoid(gpv)
        silu = gpv * sg
        ad[...] += _dot_tn((silu * upv).astype(BF16), drb)
        dgp = (dh * upv * (sg * (1.0 + gpv * (1.0 - sg)))).astype(BF16)
        dup = (dh * silu).astype(BF16)
        ag[...] += _dot_tn(dgp, x1b)
        au[...] += _dot_tn(dup, x1b)
        base = alpha * dr if first else dxp_ref[...]
        dx_ref[...] = base + _dot(dgp, wg_v[...]) + _dot(dup, wu_v[...])

        @pl.when(pl.program_id(0) == n - 1)
        def _():
            rows = pl.ds(off, sz)
            stores = _start_copies(copy_sems, [(ag, dwg_hbm.at[rows]), (au, dwu_hbm.at[rows]), (ad, dwd_hbm.at[rows])])
            for acc, stage in ((ag, wg_v), (au, wu_v), (ad, wd_v)):
                stage[...] = acc[...].astype(BF16)
            stores += _start_copies(copy_sems, [(wg_v, dwg16_hbm.at[rows]), (wu_v, dwu16_hbm.at[rows]), (wd_v, dwd16_hbm.at[rows])], first=3)
            for cp in stores:
                cp.wait()


    row = lambda w: pl.BlockSpec((tm, w), lambda i: (i, 0))
    ins = [dr2] + ([] if first else [prev[0]]) + [x1b, gate_up, wg, wu, wd] + ([] if first else list(prev[1:]))
    in_specs = [row(d)] + ([] if first else [row(d)]) + [row(d), row(2 * sz), ANY, ANY, ANY] + ([] if first else [ANY] * 6)
    return pl.pallas_call(
        body, name=f"bwd_mlp_{j}", grid=(n,),
        in_specs=in_specs,
        out_specs=[row(d)] + [ANY] * 6,
        out_shape=[jax.ShapeDtypeStruct((t, d), F32)] + [jax.ShapeDtypeStruct(wg.shape, F32)] * 3 + [jax.ShapeDtypeStruct(wg.shape, BF16)] * 3,
        scratch_shapes=[pltpu.VMEM((sz, d), F32)] * 3 + [pltpu.VMEM((sz, d), BF16)] * 3 + [pltpu.SemaphoreType.DMA((6,))],
        input_output_aliases={} if first else {7 + a: 1 + a for a in range(6)},
        compiler_params=_cparams(),
    )(*ins)


SMALL_VD = CHUNK
SMALL_CW = CHUNK + 8
SMALL_VQ = CHUNK + 8
SMALL_LOSS = CHUNK + 16
SMALL_BS = CHUNK + 24


def _small_rows(kwp):
    return -(-(SMALL_CW + max(kwp, 24 + SUBLANES)) // 16) * 16


def _bwd_mix(dx1, saved, wi, wo, wstt, cwf, tabs, vq, vd, mlp_small, token, alpha, tm):
    saved_f32, saved_bf16, hf_s = saved
    t, d = dx1.shape
    q = wi.shape[2]
    nc, n_pairs = CONV_BLOCK // CHUNK, q // LANES
    n = t // tm
    nb = tm // CONV_BLOCK
    assert tm % CONV_BLOCK == 0

    def body(dx1_ref, f32_ref, bf16_ref, hf_ref,
             wi_hbm, wo_hbm, wstt_ref, cwf_ref, fwd_ref, fwd_halo_ref, shift_ref, taps_ref, inv_ref, inv_taps_ref, vq_ref, vd_ref,
             loss_ref, dg2_ref, db2_ref, token_ref,
             gx_ref, dwi_hbm, dwo_hbm, small_hbm, dwi16_hbm, dwo16_hbm,
             wi_v, wo_v, awi, awo, dyb_ref, later_ref, dbm_ref, gf_ref, dgf_ref, small_ref, copy_sems):
        xh_ref, zu_ref, mg_ref, vhat_ref, gv_ref, yhat_ref, rs_ref, xb_ref, pag_ref, y_ref, vnb_ref = _saved_views(f32_ref, bf16_ref, d, q)
        i = pl.program_id(0)

        @pl.when(i == 0)
        def _():
            loads = _start_copies(copy_sems, [(wi_hbm, wi_v), (wo_hbm, wo_v)])
            for r in (awi, awo, small_ref, dbm_ref, dgf_ref, dyb_ref, later_ref):
                r[...] = jnp.zeros_like(r)
            gf_ref[...] = _dot(taps_ref[...], _split(cwf_ref[...], True))
            for cp in loads:
                cp.wait()

        dr1b_parts, dproj_parts = [None] * nb, [None] * nb

        def sub_tile(b):
            rows = slice(b * CONV_BLOCK, (b + 1) * CONV_BLOCK)
            dx1v = dx1_ref[rows, :]
            xh = xh_ref[rows, :]
            rsv = rs_ref[rows, :]
            small_ref[SMALL_VD + VD_LN1_G:SMALL_VD + VD_LN1_G + 1, :] += _colsum(dx1v * xh)
            small_ref[SMALL_VD + VD_LN1_B:SMALL_VD + VD_LN1_B + 1, :] += _colsum(dx1v)
            dr1 = _ln_bwd(dx1v * vd_ref[VD_LN1_G:VD_LN1_G + 1, :], xh, rsv[:, RS_LN1:RS_LN1 + 1])
            dr1b = dr1.astype(BF16)
            yield
            dy = _dot_nt(dr1b, wo_v[...])
            yield
            vhat = vhat_ref[rows, :]
            sgu_g = vq_ref[VQ_SGU_G:VQ_SGU_G + 1, :]
            doa = dy[:, 0:q]
            dm = doa * zu_ref[rows, :]
            dpu = (doa * mg_ref[rows, :]).astype(BF16)
            acc = dm[0:CHUNK]
            for c in range(1, nc):
                acc = acc + dm[c * CHUNK:(c + 1) * CHUNK]
            dbm_ref[...] += acc
            pa = pag_ref[rows, 0:q].astype(F32)
            sg = _sigmoid(pag_ref[rows, q:2 * q].astype(F32))
            yhat = yhat_ref[rows, :]
            cln_g = vq_ref[VQ_CLN_G:VQ_CLN_G + 1, :]
            yn = yhat * cln_g + vq_ref[VQ_CLN_B:VQ_CLN_B + 1, :]
            sy = _sigmoid(yn)
            dyn = dy[:, q:2 * q] * (sy * (1.0 + yn * (1.0 - sy)))
            small_ref[SMALL_VQ + VQ_CLN_G:SMALL_VQ + VQ_CLN_G + 1, q:2 * q] += _colsum(dyn * yhat)
            small_ref[SMALL_VQ + VQ_CLN_B:SMALL_VQ + VQ_CLN_B + 1, q:2 * q] += _colsum(dyn)
            dyc = _ln_bwd(dyn * cln_g, yhat, rsv[:, RS_CONV:RS_CONV + 1])
            small_ref[SMALL_VQ + VQ_CONV_B:SMALL_VQ + VQ_CONV_B + 1, q:2 * q] += _colsum(dyc)
            dyb_ref[b, 0:CONV_BLOCK, :] = dyc
            yield
            wgrads = _mix_wgrad(dm, vnb_ref[rows, :], nc, n_pairs)
            dvn = _mix(wstt_ref, dm, nc, n_pairs)
            own = _dot(fwd_ref[...], _split(dyb_ref[b]))
            with_later = own + _dot(fwd_halo_ref[...], _split(later_ref[...]))
            later_ref[...] = dyb_ref[b, 0:HALO, :]
            yield
            for p, g in enumerate(wgrads):
                for half in range(2):
                    small_ref[0:CHUNK, (2 * p + half) * CHUNK:(2 * p + half + 1) * CHUNK] += g[half * CHUNK:(half + 1) * CHUNK]
            small_ref[SMALL_VQ + VQ_SGU_G:SMALL_VQ + VQ_SGU_G + 1, q:2 * q] += _colsum(dvn * vhat)
            small_ref[SMALL_VQ + VQ_SGU_B:SMALL_VQ + VQ_SGU_B + 1, q:2 * q] += _colsum(dvn)
            dpv = (_ln_bwd(dvn * sgu_g, vhat, rsv[:, RS_SGU:RS_SGU + 1]) * gv_ref[rows, :]).astype(BF16)
            dgf_ref[...] += _cmul(_cmul(own, shift_ref[...]), hf_ref[b * 2 * DFT_F:(b + 1) * 2 * DFT_F, :], conj_b=True)
            product = _split(_cmul(with_later, gf_ref[...], conj_b=True))
            yield
            dh = _dot(inv_ref[...], product)
            yield
            da = (dh * sg).astype(BF16)
            dg = (dh * pa * (sg * (1.0 - sg))).astype(BF16)
            yield
            gx = alpha * dr1
            for dpj, wj in zip((dpu, dpv, da, dg), range(4)):
                gx = gx + _dot_nt(dpj, wi_v[wj])
            gx_ref[rows, :] = gx
            dr1b_parts[b], dproj_parts[b] = dr1b, (dpu, dpv, da, dg)

        _interleave([sub_tile(b) for b in reversed(range(nb))])

        awo[...] += _dot_tn(y_ref[...], jnp.concatenate(dr1b_parts, axis=0))
        xb = xb_ref[...]
        for j in range(4):
            awi[j] += _dot_tn(xb, jnp.concatenate([part[j] for part in dproj_parts], axis=0))

        @pl.when(i == n - 1)
        def _():
            stores = _start_copies(copy_sems, [(awi, dwi_hbm), (awo, dwo_hbm)])
            wi_v[...] = awi[...].astype(BF16)
            wo_v[...] = awo[...].astype(BF16)
            stores += _start_copies(copy_sems, [(wi_v, dwi16_hbm), (wo_v, dwo16_hbm)], first=3)
            lane = lax.broadcasted_iota(jnp.int32, (CHUNK, LANES), 1)
            low = lane < HEAD_DIM
            dbs = jnp.zeros((CHUNK, LANES), F32)
            for p in range(n_pairs):
                grp = dbm_ref[:, p * LANES:(p + 1) * LANES]
                dbs = jnp.where(lane == 2 * p, jnp.sum(jnp.where(low, grp, 0.0), axis=1, keepdims=True), dbs)
                dbs = jnp.where(lane == 2 * p + 1, jnp.sum(jnp.where(low, 0.0, grp), axis=1, keepdims=True), dbs)
            tril = lax.broadcasted_iota(jnp.int32, (CHUNK, CHUNK), 0) >= lax.broadcasted_iota(jnp.int32, (CHUNK, CHUNK), 1)
            for h in range(2 * n_pairs):
                block = small_ref[0:CHUNK, h * CHUNK:(h + 1) * CHUNK]
                small_ref[0:CHUNK, h * CHUNK:(h + 1) * CHUNK] = jnp.where(tril, block, 0.0)
            small_ref[SMALL_VD + VD_LN2_G:SMALL_VD + VD_LN2_G + 1, :] = dg2_ref[...]
            small_ref[SMALL_VD + VD_LN2_B:SMALL_VD + VD_LN2_B + 1, :] = db2_ref[...]
            small_ref[SMALL_CW:SMALL_CW + kwp, 0:q] = _dot(inv_taps_ref[...], _split(dgf_ref[...], True))
            small_ref[SMALL_LOSS:SMALL_LOSS + SUBLANES, q:q + LANES] = loss_ref[...]
            small_ref[SMALL_BS:SMALL_BS + SUBLANES, q:q + LANES] = jnp.transpose(dbs)[0:SUBLANES]
            stores += _start_copies(copy_sems, [(small_ref, small_hbm)], first=2)
            for cp in stores:
                cp.wait()

    rev = lambda w: pl.BlockSpec((tm, w), lambda i: (n - 1 - i, 0))
    kwp = cwf.shape[0]
    small = jax.ShapeDtypeStruct((_small_rows(kwp), 2 * q), F32)
    small_ins = [wstt, cwf, tabs["fwd"], tabs["fwd_halo"], tabs["shift"], tabs["taps"], tabs["inv_in"], tabs["inv_taps"], vq, vd,
                 *mlp_small]
    return pl.pallas_call(
        body, name="bwd_mix", grid=(n,),
        in_specs=[rev(d), rev(saved_f32.shape[1]), rev(saved_bf16.shape[1]),
                  pl.BlockSpec((nb * 2 * DFT_F, q), lambda i: (n - 1 - i, 0)), ANY, ANY] + [_full(a.shape) for a in small_ins] + [ANY],
        out_specs=[rev(d)] + [ANY] * 5,
        out_shape=[jax.ShapeDtypeStruct((t, d), F32), jax.ShapeDtypeStruct(wi.shape, F32), jax.ShapeDtypeStruct(wo.shape, F32), small,
                   jax.ShapeDtypeStruct(wi.shape, BF16), jax.ShapeDtypeStruct(wo.shape, BF16)],
        scratch_shapes=[pltpu.VMEM(wi.shape, BF16), pltpu.VMEM(wo.shape, BF16), pltpu.VMEM(wi.shape, F32), pltpu.VMEM(wo.shape, F32),
                        pltpu.VMEM((nb, DFT_N, q), F32), pltpu.VMEM((HALO, q), F32),
                        pltpu.VMEM((CHUNK, q), F32), pltpu.VMEM((2 * DFT_F, q), F32), pltpu.VMEM((2 * DFT_F, q), F32),
                        pltpu.VMEM(small.shape, F32), pltpu.SemaphoreType.DMA((5,))],
        compiler_params=_cparams(),
    )(dx1, saved_f32, saved_bf16, hf_s, wi, wo, *small_ins, token)


def _coords():
    return tuple(lax.axis_index(a) for a in MESH_AXES)


def _other_chips(x, y):
    return [(1 - x, y), (x, 1 - y), (1 - x, 1 - y)]


def _remote(src, dst, send_sem, recv_sem, to):
    return pltpu.make_async_remote_copy(src_ref=src, dst_ref=dst, send_sem=send_sem, recv_sem=recv_sem,
                                        device_id=to, device_id_type=MESH_ID)


def _hbm_call(body, name, ins, out_shape, scratch_shapes, aliases=None):
    return pl.pallas_call(
        body, name=name, in_specs=[ANY] * len(ins), out_specs=[ANY] * len(out_shape), out_shape=out_shape,
        scratch_shapes=scratch_shapes, input_output_aliases=aliases or {},
    )(*ins)


class _Gather:
    def __init__(self, bufs, send_sems, recv_sems, own=None):
        self.bufs, self.send_sems, self.recv_sems, self.own = bufs, send_sems, recv_sems, own
        self.x, self.y, self.c = _coords()

    def _copies(self, stage):
        x, y, c = self.x, self.y, self.c
        for a, buf in enumerate(self.bufs):
            hr = buf.shape[1] // 2
            for j, chip in enumerate(_other_chips(x, y)):
                if stage == "ici_out":
                    ref, k, to = buf.at[2 * x + y, pl.ds(c * hr, hr)], j, (*chip, c)
                    if self.own:
                        yield _remote(self.own[a].at[pl.ds(c * hr, hr)], ref, self.send_sems.at[a, k], self.recv_sems.at[a, k], to)
                        continue
                elif stage == "ici_in":
                    ref, k, to = buf.at[2 * chip[0] + chip[1], pl.ds(c * hr, hr)], j, (*chip, c)
                elif stage == "d2d_out":
                    ref, k, to = buf.at[2 * chip[0] + chip[1], pl.ds(c * hr, hr)], 3 + j, (x, y, 1 - c)
                else:
                    ref, k, to = buf.at[2 * chip[0] + chip[1], pl.ds((1 - c) * hr, hr)], 3 + j, (x, y, 1 - c)
                yield _remote(ref, ref, self.send_sems.at[a, k], self.recv_sems.at[a, k], to)

    def start(self):
        for cp in self._copies("ici_out"):
            cp.start()

    def forward(self):
        for landed, onward in zip(self._copies("ici_in"), self._copies("d2d_out")):
            landed.wait_recv()
            onward.start()

    def finish(self):
        for cp in self._copies("d2d_in"):
            cp.wait_recv()
        for stage in ("ici_out", "d2d_out"):
            for cp in self._copies(stage):
                cp.wait_send()


def _gather_sems(n):
    return [pltpu.SemaphoreType.DMA((n, 6)), pltpu.SemaphoreType.DMA((n, 6))]


def _prep_gather(gathered, local, conv_w, kwp):
    kw, cshard = conv_w.shape
    shards = list(gathered) + list(local)
    n, n_g = len(shards), len(gathered)

    def body(*refs):
        src, cw_ref, out = refs[:n], refs[n], refs[n + 1:2 * n + 2]
        wide, narrow, taps = refs[2 * n + 2:3 * n + 2], refs[3 * n + 2:4 * n + 2], refs[4 * n + 2]
        load_sems, store_sems, send_sems, recv_sems = refs[4 * n + 3:]
        x, y, _ = _coords()
        loads = [pltpu.make_async_copy(src[a], wide[a], load_sems.at[a]) for a in range(n)]
        stores = [pltpu.make_async_copy(narrow[a], out[a].at[2 * x + y], store_sems.at[a]) for a in range(n)]
        stores.append(pltpu.make_async_copy(taps, out[n].at[2 * x + y], store_sems.at[n]))
        for cp in loads:
            cp.start()

        def cast(a):
            loads[a].wait()
            narrow[a][...] = wide[a][...].astype(BF16)
            stores[a].start()

        for a in range(n_g):
            cast(a)
        taps[0:kw, :] = cw_ref[...]
        taps[kw:kwp, :] = jnp.zeros((kwp - kw, cshard), F32)
        stores[n].start()
        g = _Gather(list(out[:n_g]) + [out[n]], send_sems, recv_sems, own=list(narrow[:n_g]) + [taps])
        g.start()
        for a in range(n_g, n):
            cast(a)
        g.forward()
        g.finish()
        for cp in stores:
            cp.wait()

    out_shape = [jax.ShapeDtypeStruct((4,) + a.shape, BF16) for a in shards] + [jax.ShapeDtypeStruct((4, kwp, cshard), F32)]
    scratch = ([pltpu.VMEM(a.shape, F32) for a in shards] + [pltpu.VMEM(a.shape, BF16) for a in shards] + [pltpu.VMEM((kwp, cshard), F32)]
               + [pltpu.SemaphoreType.DMA((n,)), pltpu.SemaphoreType.DMA((n + 1,))] + _gather_sems(n_g + 1))
    res = pl.pallas_call(
        body, name="prep_gather", in_specs=[ANY] * n + [pl.BlockSpec(memory_space=pltpu.VMEM)], out_specs=[ANY] * (n + 1),
        out_shape=out_shape, scratch_shapes=scratch, compiler_params=pltpu.CompilerParams(vmem_limit_bytes=VMEM_LIMIT),
    )(*shards, conv_w)
    return list(res[:n_g]), list(res[n_g:n]), res[n]


def _pair_reduce(name, partials, payloads, c_arr, out_dtypes):
    n = len(partials)
    counts = [g.shape[0] for g in partials]
    first = [sum(counts[:a]) for a in range(n)]
    steps = sum(counts)

    def body(c_ref, *refs):
        own, travelling, out, land = (refs[k * n:(k + 1) * n] for k in range(4))
        send_sems, recv_sems = refs[4 * n:]
        i = pl.program_id(0)
        x, y, c = _coords()

        def copy(a, q):
            return _remote(travelling[a].at[q, 1 - c], land[a].at[q], send_sems.at[first[a] + q], recv_sems.at[first[a] + q],
                           (x, y, 1 - c))

        blocks = [(a, q) for a in range(n) for q in range(counts[a])]

        @pl.when(i == 0)
        def _():
            for a, q in blocks:
                copy(a, q).start()

        for a in range(n):
            @pl.when((i >= first[a]) & (i < first[a] + counts[a]))
            def _(a=a):
                q = i - first[a]
                copy(a, q).wait_recv()
                out[a][...] = (own[a][...] + land[a][q].astype(F32)).astype(out_dtypes[a])

        @pl.when(i == steps - 1)
        def _():
            for a, q in blocks:
                copy(a, q).wait_send()

    at = lambda a, i: jnp.clip(i - first[a], 0, counts[a] - 1)
    in_specs = [pl.BlockSpec((None, None) + g.shape[2:], lambda i, cr, a=a: (at(a, i), cr[0], 0, 0)) for a, g in enumerate(partials)]
    out_specs = [pl.BlockSpec((None,) + g.shape[2:], lambda i, cr, a=a: (at(a, i), 0, 0)) for a, g in enumerate(partials)]
    grid_spec = pltpu.PrefetchScalarGridSpec(
        num_scalar_prefetch=1, grid=(steps,), in_specs=in_specs + [ANY] * n, out_specs=out_specs,
        scratch_shapes=[pltpu.VMEM((g.shape[0],) + g.shape[2:], p.dtype) for g, p in zip(partials, payloads)]
        + [pltpu.SemaphoreType.DMA((steps,)), pltpu.SemaphoreType.DMA((steps,))])
    out_shape = [jax.ShapeDtypeStruct((g.shape[0],) + g.shape[2:], dt) for g, dt in zip(partials, out_dtypes)]
    return list(pl.pallas_call(body, name=name, grid_spec=grid_spec, out_shape=out_shape, compiler_params=_cparams())(
        c_arr, *partials, *payloads))


class _Exchange:
    def __init__(self, src, dst, send_sems, recv_sems):
        self.src, self.dst, self.send_sems, self.recv_sems = src, dst, send_sems, recv_sems
        self.x, self.y, self.c = _coords()

    def _copies(self, incoming):
        x, y, c = self.x, self.y, self.c
        for a, (s, d) in enumerate(zip(self.src, self.dst)):
            for j, chip in enumerate(_other_chips(x, y)):
                slot = 2 * chip[0] + chip[1]
                if incoming:
                    out, into = d.at[slot], d.at[slot]
                else:
                    out, into = (s.at[slot] if len(s.shape) == 3 else s), d.at[2 * x + y]
                yield _remote(out, into, self.send_sems.at[a, j], self.recv_sems.at[a, j], (*chip, c))

    def start(self):
        for cp in self._copies(False):
            cp.start()

    def finish(self):
        for cp in self._copies(True):
            cp.wait_recv()
        for cp in self._copies(False):
            cp.wait_send()


def _exchange_shapes(arrs):
    return [jax.ShapeDtypeStruct((4,) + s.shape[-2:], s.dtype) for s in arrs]


class _FlatSems:
    def __init__(self, ref):
        self.ref = ref

    @property
    def at(self):
        return self

    def __getitem__(self, idx):
        return self.ref.at[3 * idx[0] + idx[1]]


HBM = pl.BlockSpec(memory_space=pltpu.HBM)
SEM = pl.BlockSpec(memory_space=pltpu.SEMAPHORE)
DATAFLOW = pltpu.SideEffectType.DATAFLOW_SIDE_EFFECTING


def _exchange_start(name, arrs):
    n = len(arrs)
    lands = _exchange_shapes(arrs)

    def body(*refs):
        src, land = refs[:n], refs[n:2 * n]
        send_sems, recv_sems = refs[2 * n:2 * n + 2]
        token = refs[-1]
        _Exchange(src, land, _FlatSems(send_sems), _FlatSems(recv_sems)).start()
        token[...] = jnp.zeros_like(token)

    hbm = lambda a: pltpu.with_memory_space_constraint(a, pltpu.HBM)
    outs = pl.pallas_call(
        body, name=name,
        out_shape=(pltpu.SemaphoreType.DMA((3 * n,)), pltpu.SemaphoreType.DMA((3 * n,)),
                   *[pltpu.HBM(a.shape, a.dtype) for a in arrs], *[pltpu.HBM(s.shape, s.dtype) for s in lands],
                   jax.ShapeDtypeStruct((SUBLANES, LANES), F32)),
        in_specs=[HBM] * (2 * n), out_specs=(SEM, SEM, *[HBM] * (2 * n), pl.BlockSpec(memory_space=pltpu.VMEM)),
        input_output_aliases={a: 2 + a for a in range(2 * n)},
        compiler_params=pltpu.CompilerParams(has_side_effects=DATAFLOW),
    )(*[hbm(a) for a in arrs], *[hbm(lax.empty(s.shape, s.dtype)) for s in lands])
    return outs[:-1], outs[-1]


def _exchange_wait(name, started, after):
    send_sems, recv_sems, *bufs = started
    n = len(bufs) // 2

    def body(*refs):
        src, land = refs[:n], refs[n:2 * n]
        send_sems, recv_sems = refs[2 * n:2 * n + 2]
        _Exchange(src, land, _FlatSems(send_sems), _FlatSems(recv_sems)).finish()

    outs = pl.pallas_call(
        body, name=name,
        out_shape=tuple(pltpu.HBM(b.shape, b.dtype) for b in bufs),
        in_specs=[HBM] * (2 * n) + [SEM, SEM] + [ANY] * len(after), out_specs=tuple([HBM] * (2 * n)),
        input_output_aliases={a: a for a in range(2 * n)},
        compiler_params=pltpu.CompilerParams(has_side_effects=DATAFLOW),
    )(*bufs, send_sems, recv_sems, *after)
    return list(outs[:n]), list(outs[n:])


def _pair_gather(name, halves):
    n = len(halves)

    def body(*refs):
        src, dst = refs[:n], refs[n:2 * n]
        send_sems, recv_sems = refs[2 * n:]
        x, y, c = _coords()
        copies = [_remote(src[a], dst[a], send_sems.at[a], recv_sems.at[a], (x, y, 1 - c)) for a in range(n)]
        for cp in copies:
            cp.start()
        for cp in copies:
            cp.wait()

    outs = [jax.ShapeDtypeStruct(s.shape, s.dtype) for s in halves]
    return _hbm_call(body, name, halves, outs, [pltpu.SemaphoreType.DMA((n,)), pltpu.SemaphoreType.DMA((n,))])


def _chip_sum(a, parts, own, me_arr, after):
    _, hr, cc = parts.shape

    def body(me_ref, p_ref, own_ref, after_ref, o_ref):
        for mine in range(4):
            @pl.when(me_ref[0] == mine)
            def _():
                term = lambda j: (own_ref if j == mine else p_ref.at[j])[...].astype(F32)
                o_ref[...] = ((term(0) + term(1)) + term(2)) + term(3)

    own_spec = (pl.BlockSpec((None, hr, cc), lambda i, me: (me[0], 0, 0)) if own.ndim == 3
                else pl.BlockSpec((hr, cc), lambda i, me: (0, 0)))
    grid_spec = pltpu.PrefetchScalarGridSpec(
        num_scalar_prefetch=1, grid=(1,),
        in_specs=[pl.BlockSpec((4, hr, cc), lambda i, me: (0, 0, 0)), own_spec, ANY],
        out_specs=pl.BlockSpec((hr, cc), lambda i, me: (0, 0)))
    return pl.pallas_call(body, name=f"chip_sum_{a}", grid_spec=grid_spec, out_shape=jax.ShapeDtypeStruct((hr, cc), F32),
                          compiler_params=_cparams())(me_arr, parts, own, after)


def _row_block(rows, cols, limit=1 << 20):
    best = 8
    for tr in range(8, rows + 1, 8):
        if rows % tr == 0 and tr * cols * 4 <= limit:
            best = tr
    return best


def _adamw(name, w, g_mine, g_other, m, v, c_arr):
    r, c = w.shape
    hr, cg = g_mine.shape
    tr = hr if r % hr == 0 and hr * cg * 4 <= (3 << 19) else math.gcd(_row_block(hr, cg), r)
    per_half = hr // tr
    bc1 = 1.0 - ADAM_B1 ** ADAM_STEP
    bc2 = 1.0 - ADAM_B2 ** ADAM_STEP

    def body(c_ref, w_ref, gm_ref, go_ref, m_ref, v_ref, go, do, mo, vo):
        gv = jnp.where(pl.program_id(0) // per_half == c_ref[0], gm_ref[:, 0:c], go_ref[:, 0:c])
        mn = ADAM_B1 * m_ref[...] + (1.0 - ADAM_B1) * gv
        vn = ADAM_B2 * v_ref[...] + (1.0 - ADAM_B2) * (gv * gv)
        go[...] = gv
        mo[...] = mn
        vo[...] = vn
        do[...] = -ADAM_LR * ((mn / bc1) / (jnp.sqrt(vn / bc2) + ADAM_EPS) + ADAM_WD * w_ref[...])

    blk = pl.BlockSpec((tr, c), lambda i, cr: (i, 0))
    gblk = pl.BlockSpec((tr, cg), lambda i, cr: (i % per_half, 0))
    grid_spec = pltpu.PrefetchScalarGridSpec(num_scalar_prefetch=1, grid=(r // tr,), in_specs=[blk, gblk, gblk, blk, blk],
                                             out_specs=[blk] * 4)
    return pl.pallas_call(body, name=f"adamw_{name}", grid_spec=grid_spec, out_shape=[jax.ShapeDtypeStruct((r, c), F32)] * 4,
                          compiler_params=_cparams())(c_arr, w, g_mine, g_other, m, v)


SMALL_Q = ("sgu_ln_g", "sgu_ln_b", "conv_b", "conv_ln_g", "conv_ln_b")
SMALL_D = ("ln1_g", "ln1_b", "ln2_g", "ln2_b")


def _adamw_small(g_mine, g_other, c_arr, me_arr, params):
    names = list(SMALL_Q) + list(SMALL_D) + ["w_s", "b_s", "conv_w"]
    hr, width = g_mine.shape
    q = width // 2
    heads = params["w_s"][0].shape[1]
    kw, cshard = params["conv_w"][0].shape[1:]
    bc1 = 1.0 - ADAM_B1 ** ADAM_STEP
    bc2 = 1.0 - ADAM_B2 ** ADAM_STEP

    def update(w, g, m, v):
        mn = ADAM_B1 * m + (1.0 - ADAM_B1) * g
        vn = ADAM_B2 * v + (1.0 - ADAM_B2) * (g * g)
        return g, -ADAM_LR * ((mn / bc1) / (jnp.sqrt(vn / bc2) + ADAM_EPS) + ADAM_WD * w), mn, vn

    def body(c_ref, me_ref, gm_ref, go_ref, *refs):
        ins = {nm: refs[3 * k:3 * k + 3] for k, nm in enumerate(names)}
        outs = {nm: refs[3 * len(names) + 4 * k:3 * len(names) + 4 * k + 4] for k, nm in enumerate(names)}
        loss_ref, cw_ref = refs[-2:]
        first, second = gm_ref[...], go_ref[...]
        low = c_ref[0] == 0
        g_all = jnp.concatenate([jnp.where(low, first, second), jnp.where(low, second, first)], axis=0)

        def apply(nm, g, at):
            w, m, v = (r[at] for r in ins[nm])
            for o, val in zip(outs[nm], update(w, g, m, v)):
                o[at] = val

        for row, nm in enumerate(SMALL_Q):
            apply(nm, g_all[SMALL_VQ + row:SMALL_VQ + row + 1, q:2 * q], ...)
        for row, nm in enumerate(SMALL_D):
            apply(nm, g_all[SMALL_VD + row:SMALL_VD + row + 1, :], ...)
        for h in range(heads):
            apply("w_s", g_all[0:CHUNK, h * CHUNK:(h + 1) * CHUNK], (0, h))
        apply("b_s", g_all[SMALL_BS:SMALL_BS + heads, q:q + LANES], 0)
        cw_ref[...] = jnp.zeros_like(cw_ref)
        for chip in range(4):
            @pl.when(me_ref[0] == chip)
            def _():
                cw_ref[...] = g_all[SMALL_CW:SMALL_CW + cw_ref.shape[0], chip * cshard:(chip + 1) * cshard]
        apply("conv_w", cw_ref[0:kw, :], 0)
        loss_ref[...] = g_all[SMALL_LOSS:SMALL_LOSS + SUBLANES, q:q + LANES]

    arrays = [a for nm in names for a in params[nm]]
    out_shape = [jax.ShapeDtypeStruct(params[nm][0].shape, F32) for nm in names for _ in range(4)] + [jax.ShapeDtypeStruct((SUBLANES, LANES), F32)]
    whole = lambda shape: pl.BlockSpec(shape, lambda i, c, me: (0,) * len(shape))
    grid_spec = pltpu.PrefetchScalarGridSpec(
        num_scalar_prefetch=2, grid=(1,),
        in_specs=[whole(g_mine.shape), whole(g_other.shape)] + [whole(a.shape) for a in arrays],
        out_specs=[whole(s.shape) for s in out_shape],
        scratch_shapes=[pltpu.VMEM((-(-kw // SUBLANES) * SUBLANES, cshard), F32)])
    res = pl.pallas_call(body, name="adamw_small", grid_spec=grid_spec, out_shape=out_shape, compiler_params=_cparams())(
        c_arr, me_arr, g_mine, g_other, *arrays)
    return {nm: list(res[4 * k:4 * k + 4]) for k, nm in enumerate(names)}, res[-1]


def _pad_rows(a, rows):
    return jnp.pad(a, ((0, rows - a.shape[0]), (0, 0)))


def kernel(x, w_in, sgu_ln_g, sgu_ln_b, w_s, b_s, conv_w, conv_b, conv_ln_g, conv_ln_b, w_out, ln1_g, ln1_b, w_gate, w_up, w_down, ln2_g, ln2_b, loss_target, m_w_in, m_sgu_ln_g, m_sgu_ln_b, m_w_s, m_b_s, m_conv_w, m_conv_b, m_conv_ln_g, m_conv_ln_b, m_w_out, m_ln1_g, m_ln1_b, m_w_gate, m_w_up, m_w_down, m_ln2_g, m_ln2_b, v_w_in, v_sgu_ln_g, v_sgu_ln_b, v_w_s, v_b_s, v_conv_w, v_conv_b, v_conv_ln_g, v_conv_ln_b, v_w_out, v_ln1_g, v_ln1_b, v_w_gate, v_w_up, v_w_down, v_ln2_g, v_ln2_b):
    depth, d, q = w_in.shape
    assert depth == 1 and x.shape[0] == 1
    t = x.shape[1]
    heads = w_s.shape[1]
    kw, cshard = conv_w.shape[1], conv_w.shape[2]
    fs = w_gate.shape[2]
    slabs = _hidden_slabs(4 * fs)
    n_pairs = q // LANES
    assert heads * HEAD_DIM == q and q % LANES == 0 and w_s.shape[2] == CHUNK and 4 * cshard == q and kw - 1 <= HALO
    alpha = (2.0 * depth) ** 0.25
    tm = min(512, t)
    assert t % tm == 0 and tm % CHUNK == 0
    x2, tgt = x[0], loss_target[0]
    mx, my, mc = _coords()
    me = 2 * mx + my
    c_arr = jnp.reshape(mc, (1,)).astype(jnp.int32)

    kwp = -(-kw // 16) * 16
    me_arr = jnp.reshape(me, (1,)).astype(jnp.int32)
    (wi, wo), (wg, wu, wd), cw4 = _prep_gather([w_in[0], w_out[0]], [w_gate[0].T, w_up[0].T, w_down[0]], conv_w[0], kwp)
    wo = wo.reshape(d, d)
    cw = jnp.transpose(cw4, (1, 0, 2)).reshape(kwp, q)
    cwf = _pad_rows(cw[:kw][::-1], kwp)
    tabs = {name: jnp.asarray(tab) for name, tab in _dft_tables(kw, kwp, q).items()}

    wm = jnp.where(jnp.tril(jnp.ones((CHUNK, CHUNK), bool)), w_s[0], 0.0)
    wst = wm.reshape(n_pairs, 2 * CHUNK, CHUNK).astype(BF16)
    wstt = jnp.transpose(wm, (0, 2, 1)).reshape(n_pairs, 2 * CHUNK, CHUNK).astype(BF16)
    bmat = jnp.repeat(b_s[0].T, HEAD_DIM, axis=1)
    vq = _pad_rows(jnp.concatenate([sgu_ln_g, sgu_ln_b, conv_b, conv_ln_g, conv_ln_b], axis=0), 8)
    vd = _pad_rows(jnp.concatenate([ln1_g, ln1_b, ln2_g, ln2_b], axis=0), 8)

    *saved, wg, wu, wd = _fwd_mix(x2, wi, wo, wst, bmat, cwf, tabs, vq, vd, [wg, wu, wd], alpha, tm)
    wg, wu, wd = (w.reshape(4 * fs, d) for w in (wg, wu, wd))
    *acts, x1b, dr2, loss_part, dg2, db2 = _fwd_mlp(saved[0], tgt, wg, wu, wd, vd, alpha, slabs, tm)
    mlp_grads = None
    for j, slab in enumerate(slabs):
        mlp_grads = _bwd_mlp_slab(j, slab, dr2, mlp_grads, x1b, acts[j], wg, wu, wd, alpha, tm)
    dx1 = mlp_grads[0]
    mlp_halves = [b.reshape(4, 2, fs // 2, d) for b in mlp_grads[1:]]
    mlp_sums = _pair_reduce("pair_reduce_mlp", mlp_halves[:3], mlp_halves[3:], c_arr, [BF16] * 3)
    mlp_started, token = _exchange_start("exchange_mlp_start", mlp_sums)
    grad_x, dwi, dwo, small, dwi16, dwo16 = _bwd_mix(dx1, saved, wi, wo, wstt, cwf, tabs, vq, vd, (loss_part, dg2, db2), token, alpha, tm)
    mlp_sums, mlp_parts = _exchange_wait("exchange_mlp_wait", mlp_started, [dwo])

    by_halves = lambda b: b.reshape(4, 2, b.shape[1] // 2, b.shape[2])
    halves = [by_halves(dwi), by_halves(dwo.reshape(4, d // 4, d)), small.reshape(1, 2, small.shape[0] // 2, small.shape[1])]
    travelling = [by_halves(dwi16), by_halves(dwo16.reshape(4, d // 4, d)), halves[-1]]
    *sums, small_sum = _pair_reduce("pair_reduce_mix", halves, travelling, c_arr, [BF16, BF16, F32])
    sums.append(small_sum[0])
    mix_started, token = _exchange_start("exchange_mix_start", sums)

    out, raw = {}, {}

    def finish(first, names, parts, sums, after):
        mine = [_chip_sum(first + a, p, s, me_arr, after) for a, (p, s) in enumerate(zip(parts, sums))]
        other = _pair_gather(f"pair_gather_{first}", mine)
        for a, nm in enumerate(names):
            w_, m_, v_ = weights[nm]
            if nm in ("w_gate", "w_up"):
                raw[nm] = _adamw(nm, w_[0].T, mine[a], other[a], m_[0].T, v_[0].T, c_arr)
                out[nm] = [o.T for o in raw[nm]]
            else:
                raw[nm] = out[nm] = _adamw(nm, w_[0], mine[a], other[a], m_[0], v_[0], c_arr)
        return mine[-1], other[-1]

    weights = {"w_in": (w_in, m_w_in, v_w_in), "w_out": (w_out, m_w_out, v_w_out), "w_gate": (w_gate, m_w_gate, v_w_gate),
               "w_up": (w_up, m_w_up, v_w_up), "w_down": (w_down, m_w_down, v_w_down)}
    finish(2, ["w_gate", "w_up", "w_down"], mlp_parts, mlp_sums, token)
    sums, parts = _exchange_wait("exchange_mix_wait", mix_started, [raw[nm][1] for nm in ("w_gate", "w_up", "w_down")])
    small_mine, small_other = finish(5, ["w_in", "w_out"], parts, sums, parts[0])

    small_params = {
        "sgu_ln_g": (sgu_ln_g, m_sgu_ln_g, v_sgu_ln_g), "sgu_ln_b": (sgu_ln_b, m_sgu_ln_b, v_sgu_ln_b),
        "conv_b": (conv_b, m_conv_b, v_conv_b), "conv_ln_g": (conv_ln_g, m_conv_ln_g, v_conv_ln_g),
        "conv_ln_b": (conv_ln_b, m_conv_ln_b, v_conv_ln_b), "ln1_g": (ln1_g, m_ln1_g, v_ln1_g), "ln1_b": (ln1_b, m_ln1_b, v_ln1_b),
        "ln2_g": (ln2_g, m_ln2_g, v_ln2_g), "ln2_b": (ln2_b, m_ln2_b, v_ln2_b), "w_s": (w_s, m_w_s, v_w_s),
        "b_s": (b_s, m_b_s, v_b_s), "conv_w": (conv_w, m_conv_w, v_conv_w)}
    small_out, loss_block = _adamw_small(small_mine, small_other, c_arr, me_arr, small_params)
    loss = loss_block[0, 0]
    names = ["w_in", "sgu_ln_g", "sgu_ln_b", "w_s", "b_s", "conv_w", "conv_b", "conv_ln_g", "conv_ln_b", "w_out",
             "ln1_g", "ln1_b", "w_gate", "w_up", "w_down", "ln2_g", "ln2_b"]
    result = [loss, grad_x[None]]
    for kind in range(4):
        for nm in names:
            result.append(out[nm][kind][None] if nm in out else small_out[nm][kind])
    return tuple(result)
```

```python
import math

import jax
import numpy as np
import jax.numpy as jnp
from jax import lax
from jax.experimental import pallas as pl
from jax.experimental.pallas import tpu as pltpu

F32 = jnp.float32
BF16 = jnp.bfloat16

LN_EPS = 1e-5
HEAD_DIM = 64
CHUNK = 128
HALO = 32
LANES = 128
MXU_N = 256
ADAM_LR, ADAM_B1, ADAM_B2, ADAM_EPS, ADAM_WD, ADAM_STEP = 0.001, 0.9, 0.999, 1e-08, 0.01, 10
VMEM_LIMIT = 63 * 1024 * 1024
MESH_AXES = ("x", "y", "c")
MESH_ID = pl.DeviceIdType.MESH


def _dot(a, b):
    return jnp.dot(a, b, preferred_element_type=F32)


def _dot_nt(a, b):
    return lax.dot_general(a, b, (((1,), (1,)), ((), ())), preferred_element_type=F32)


def _dot_tn(a, b):
    return lax.dot_general(a, b, (((0,), (0,)), ((), ())), preferred_element_type=F32)


def _sigmoid(v):
    return 1.0 / (1.0 + jnp.exp(-v))


def _gelu(v):
    cdf = 0.5 * (1.0 + lax.erf(v * (1.0 / math.sqrt(2.0))))
    pdf = jnp.exp(-0.5 * v * v) * (1.0 / math.sqrt(2.0 * math.pi))
    return v * cdf, cdf + v * pdf


def _ln_stats(v):
    mu = jnp.mean(v, axis=-1, keepdims=True)
    d = v - mu
    rstd = lax.rsqrt(jnp.mean(d * d, axis=-1, keepdims=True) + LN_EPS)
    return d * rstd, rstd


def _ln_bwd(dxhat, xhat, rstd):
    m1 = jnp.mean(dxhat, axis=-1, keepdims=True)
    m2 = jnp.mean(dxhat * xhat, axis=-1, keepdims=True)
    return rstd * (dxhat - m1 - xhat * m2)


def _colsum(v):
    return jnp.sum(v, axis=0, keepdims=True)


def _pair_lanes(v, nc, p):
    return jnp.concatenate([v[c * CHUNK:(c + 1) * CHUNK, p * LANES:(p + 1) * LANES] for c in range(nc)], axis=1)


def _unpair(parts, nc):
    rows = [jnp.concatenate([part[:, c * LANES:(c + 1) * LANES] for part in parts], axis=1) for c in range(nc)]
    return jnp.concatenate(rows, axis=0)


def _low_head(nc):
    lane = lax.broadcasted_iota(jnp.int32, (CHUNK, nc * LANES), 1)
    return (lane & (LANES - 1)) < HEAD_DIM


def _mix(wst_ref, v, nc, n_pairs):
    vb = v.astype(BF16)
    low = _low_head(nc)
    parts = []
    for p in range(n_pairs):
        r = _dot(wst_ref[p], _pair_lanes(vb, nc, p))
        parts.append(jnp.where(low, r[:CHUNK], r[CHUNK:]))
    return _unpair(parts, nc)


def _mix_wgrad(dm, vn, nc, n_pairs):
    low = _low_head(nc)
    vb = vn.astype(BF16)
    out = []
    for p in range(n_pairs):
        a = _pair_lanes(dm, nc, p)
        lhs = jnp.concatenate([jnp.where(low, a, 0.0), jnp.where(low, 0.0, a)], axis=0).astype(BF16)
        out.append(_dot_nt(lhs, _pair_lanes(vb, nc, p)))
    return out


SUBLANES = 8


CONV_BLOCK = 256
DFT_N = CONV_BLOCK + HALO
DFT_F = -(-(DFT_N // 2 + 1) // SUBLANES) * SUBLANES


def _terms(m, exact):
    hi = m.astype(np.float32).astype(BF16)
    lo = (m.astype(np.float32) - hi.astype(np.float32)).astype(BF16)
    return np.concatenate([hi, hi, lo] if exact else [hi], axis=1)


def _split(v, exact=False):
    hi = v.astype(BF16)
    if not exact:
        return hi
    lo = (v - hi.astype(F32)).astype(BF16)
    return jnp.concatenate([hi, lo, hi], axis=0)


def _dft_tables(kw, kwp, q):
    nf = DFT_N // 2 + 1
    ang = 2.0 * np.pi * np.arange(nf)[:, None] * np.arange(DFT_N)[None, :] / DFT_N
    fwd = np.zeros((2 * DFT_F, DFT_N))
    fwd[:nf], fwd[DFT_F:DFT_F + nf] = np.cos(ang), -np.sin(ang)
    weight = np.full((nf, 1), 2.0 / DFT_N)
    weight[0] = weight[-1] = 1.0 / DFT_N
    inv = np.zeros((DFT_N, 2 * DFT_F))
    inv[:, :nf], inv[:, DFT_F:DFT_F + nf] = (np.cos(ang) * weight).T, (-np.sin(ang) * weight).T
    inv_taps = np.zeros((kwp, 2 * DFT_F))
    inv_taps[:kw] = inv[kw - 1::-1][:kw]
    shift = np.zeros((2 * DFT_F, q), np.float32)
    shift[:nf], shift[DFT_F:DFT_F + nf] = np.cos(ang[:, HALO:HALO + 1]), -np.sin(ang[:, HALO:HALO + 1])
    return {"fwd": _terms(fwd, False), "fwd_halo": _terms(fwd[:, CONV_BLOCK:], False), "shift": shift,
            "inv_out": _terms(inv[HALO:HALO + CONV_BLOCK], False), "inv_in": _terms(inv[:CONV_BLOCK], False),
            "taps": _terms(fwd[:, :kwp], True), "inv_taps": _terms(inv_taps, True)}


def _cmul(a, b, conj_b=False):
    ar, ai, br, bi = a[:DFT_F], a[DFT_F:], b[:DFT_F], b[DFT_F:]
    if conj_b:
        return jnp.concatenate([ar * br + ai * bi, ai * br - ar * bi], axis=0)
    return jnp.concatenate([ar * br - ai * bi, ar * bi + ai * br], axis=0)


def _interleave(sub_tiles):
    waiting, live = list(sub_tiles), []
    while waiting or live:
        if waiting:
            live.append(waiting.pop(0))
        for g in list(live):
            try:
                next(g)
            except StopIteration:
                live.remove(g)


def _start_copies(sems, pairs, first=0):
    copies = [pltpu.make_async_copy(src, dst, sems.at[first + k]) for k, (src, dst) in enumerate(pairs)]
    for cp in copies:
        cp.start()
    return copies


def _cparams():
    return pltpu.CompilerParams(dimension_semantics=("arbitrary",), vmem_limit_bytes=VMEM_LIMIT)


def _full(shape):
    return pl.BlockSpec(shape, lambda i: (0,) * len(shape))


ANY = pl.BlockSpec(memory_space=pl.ANY)

VQ_SGU_G, VQ_SGU_B, VQ_CONV_B, VQ_CLN_G, VQ_CLN_B = range(5)
VD_LN1_G, VD_LN1_B, VD_LN2_G, VD_LN2_B = range(4)
RS_LN1, RS_SGU, RS_CONV = range(3)
RS_COLS = LANES


def _saved_widths(d, q):
    f32 = [d, q, q, q, q, q, RS_COLS]
    bf16 = [d, 2 * q, d, q]
    return f32, bf16


def _saved_views(f32_ref, bf16_ref, d, q):
    views = []
    for ref, widths in zip((f32_ref, bf16_ref), _saved_widths(d, q)):
        for k, w in enumerate(widths):
            views.append(ref.at[pl.ds(0, ref.shape[0]), pl.ds(sum(widths[:k]), w)])
    return views
def _fwd_mix(x, wi, wo, wst, bmat, cwf, tabs, vq, vd, mlp_w, alpha, tm):
    t, d = x.shape
    q = wi.shape[2]
    nc, n_pairs = CONV_BLOCK // CHUNK, q // LANES
    n = t // tm
    n_in, n_saved = 11, 3
    assert tm % CONV_BLOCK == 0

    def body(x_ref, wi_hbm, wo_hbm, wst_ref, bmat_ref, cwf_ref, fwd_ref, taps_ref, inv_ref, vq_ref, vd_ref, *rest):
        f32_ref, bf16_ref, hf_ref = rest[3:3 + n_saved]
        xh_ref, zu_ref, mg_ref, vhat_ref, gv_ref, yhat_ref, rs_ref, xb_ref, pag_ref, y_ref, vnb_ref = _saved_views(f32_ref, bf16_ref, d, q)
        gathered = rest[3 + n_saved:6 + n_saved]
        wi_v, wo_v, hb_ref, gf_ref, send_sems, recv_sems, copy_sems = rest[6 + n_saved:]
        step = pl.program_id(0)

        @pl.when(step == 0)
        def _():
            loads = _start_copies(copy_sems, [(wi_hbm, wi_v), (wo_hbm, wo_v)])
            _Gather(gathered, send_sems, recv_sems).start()
            hb_ref[...] = jnp.zeros_like(hb_ref)
            gf_ref[...] = _dot(taps_ref[...], _split(cwf_ref[...], True))
            for cp in loads:
                cp.wait()

        @pl.when(step == (3 * n) // 4)
        def _():
            _Gather(gathered, send_sems, recv_sems).forward()

        def sub_tile(b):
            rows = slice(b * CONV_BLOCK, (b + 1) * CONV_BLOCK)
            xv = x_ref[rows, :]
            xb = xv.astype(BF16)
            xb_ref[rows, :] = xb
            pu, pv, pa, pg = (_dot(xb, wi_v[j]) for j in range(4))
            yield
            pag_ref[rows, 0:q] = pa.astype(BF16)
            pag_ref[rows, q:2 * q] = pg.astype(BF16)
            zu, gu = _gelu(pu)
            zv, gv = _gelu(pv)
            vhat, rstd_v = _ln_stats(zv)
            vnb = (vhat * vq_ref[VQ_SGU_G:VQ_SGU_G + 1, :] + vq_ref[VQ_SGU_B:VQ_SGU_B + 1, :]).astype(BF16)
            hb_ref[HALO + b * CONV_BLOCK:HALO + (b + 1) * CONV_BLOCK, :] = pa * _sigmoid(pg)
            yield
            mixed = _mix(wst_ref, vnb, nc, n_pairs) + jnp.concatenate([bmat_ref[...]] * nc, axis=0)
            spectrum = _dot(fwd_ref[...], _split(hb_ref[b * CONV_BLOCK:b * CONV_BLOCK + DFT_N, :]))
            yield
            y_ref[rows, 0:q] = (zu * mixed).astype(BF16)
            zu_ref[rows, :] = zu
            mg_ref[rows, :] = mixed * gu
            vhat_ref[rows, :] = vhat
            gv_ref[rows, :] = gv
            vnb_ref[rows, :] = vnb
            hf_ref[b * 2 * DFT_F:(b + 1) * 2 * DFT_F, :] = spectrum
            product = _split(_cmul(gf_ref[...], spectrum))
            yield
            yc = _dot(inv_ref[...], product) + vq_ref[VQ_CONV_B:VQ_CONV_B + 1, :]
            yield
            yhat, rstd_c = _ln_stats(yc)
            yhat_ref[rows, :] = yhat
            yn = yhat * vq_ref[VQ_CLN_G:VQ_CLN_G + 1, :] + vq_ref[VQ_CLN_B:VQ_CLN_B + 1, :]
            y_ref[rows, q:2 * q] = (yn * _sigmoid(yn)).astype(BF16)
            yield
            r1 = alpha * xv + _dot(y_ref[rows, :], wo_v[...])
            yield
            xhat, rstd1 = _ln_stats(r1)
            xh_ref[rows, :] = xhat
            col = lax.broadcasted_iota(jnp.int32, (CONV_BLOCK, RS_COLS), 1)
            rs_ref[rows, :] = jnp.where(col == RS_LN1, rstd1, jnp.where(col == RS_SGU, rstd_v, jnp.where(col == RS_CONV, rstd_c, 0.0)))

        _interleave([sub_tile(b) for b in range(tm // CONV_BLOCK)])
        hb_ref[0:HALO, :] = hb_ref[tm:tm + HALO, :]

        @pl.when(step == n - 1)
        def _():
            _Gather(gathered, send_sems, recv_sems).finish()

    row = lambda w: pl.BlockSpec((tm, w), lambda i: (i, 0))
    widths = [(sum(w), dt) for w, dt in zip(_saved_widths(d, q), (F32, BF16))]
    small_ins = [wst, bmat, cwf, tabs["fwd"], tabs["taps"], tabs["inv_out"], vq, vd]
    return pl.pallas_call(
        body, name="fwd_mix", grid=(n,),
        in_specs=[row(d), ANY, ANY] + [_full(a.shape) for a in small_ins] + [ANY] * 3,
        out_specs=[row(w) for w, _ in widths] + [pl.BlockSpec((tm // CONV_BLOCK * 2 * DFT_F, q), lambda i: (i, 0))] + [ANY] * 3,
        out_shape=[jax.ShapeDtypeStruct((t, w), dt) for w, dt in widths] + [jax.ShapeDtypeStruct((t // CONV_BLOCK * 2 * DFT_F, q), F32)]
        + [jax.ShapeDtypeStruct(b.shape, b.dtype) for b in mlp_w],
        scratch_shapes=[pltpu.VMEM(wi.shape, BF16), pltpu.VMEM(wo.shape, BF16), pltpu.VMEM((HALO + tm, q), F32),
                        pltpu.VMEM((2 * DFT_F, q), F32)] + _gather_sems(3) + [pltpu.SemaphoreType.DMA((2,))],
        input_output_aliases={n_in + a: n_saved + a for a in range(3)},
        compiler_params=_cparams(),
    )(x, wi, wo, *small_ins, *mlp_w)


MLP_SLABS = 4


def _hidden_slabs(f):
    assert f % MXU_N == 0
    tiles = f // MXU_N
    sizes = [(tiles // MLP_SLABS + (1 if j < tiles % MLP_SLABS else 0)) * MXU_N for j in range(MLP_SLABS)]
    return [(sum(sizes[:j]), sz) for j, sz in enumerate(sizes) if sz]


def _fwd_mlp(saved_f32, tgt, wg, wu, wd, vd, alpha, slabs, tm):
    t, d = tgt.shape
    n = t // tm
    ns = len(slabs)
    half = tm // 2 if tm % 32 == 0 else tm

    def body(xh_ref, tgt_ref, wg_hbm, wu_hbm, wd_hbm, vd_ref, *rest):
        gp_refs = [r.at[pl.ds(0, tm), pl.ds(0, sz)] for r, (_, sz) in zip(rest[:ns], slabs)]
        up_refs = [r.at[pl.ds(0, tm), pl.ds(sz, sz)] for r, (_, sz) in zip(rest[:ns], slabs)]
        x1b_ref, dr2_ref, loss_ref, dg2_ref, db2_ref, wg_v, wu_v, wd_v, copy_sems = rest[ns:]

        @pl.when(pl.program_id(0) == 0)
        def _():
            loads = _start_copies(copy_sems, [(wg_hbm, wg_v), (wu_hbm, wu_v), (wd_hbm, wd_v)])
            loss_ref[...] = jnp.zeros_like(loss_ref)
            dg2_ref[...] = jnp.zeros_like(dg2_ref)
            db2_ref[...] = jnp.zeros_like(db2_ref)
            for cp in loads:
                cp.wait()

        g2 = vd_ref[VD_LN2_G:VD_LN2_G + 1, :]

        for r0 in range(0, tm, half):
            rows = slice(r0, r0 + half)
            x1 = xh_ref[rows, :] * vd_ref[VD_LN1_G:VD_LN1_G + 1, :] + vd_ref[VD_LN1_B:VD_LN1_B + 1, :]
            x1b = x1.astype(BF16)
            x1b_ref[rows, :] = x1b
            acc = alpha * x1
            for (off, sz), gp_ref, up_ref in zip(slabs, gp_refs, up_refs):
                gp = _dot_nt(x1b, wg_v[off:off + sz, :])
                up = _dot_nt(x1b, wu_v[off:off + sz, :])
                gp_ref[rows, :] = gp.astype(BF16)
                up_ref[rows, :] = up.astype(BF16)
                acc = acc + _dot((gp * _sigmoid(gp) * up).astype(BF16), wd_v[off:off + sz, :])
            xh2, rstd2 = _ln_stats(acc)
            err = xh2 * g2 + vd_ref[VD_LN2_B:VD_LN2_B + 1, :] - tgt_ref[rows, :]
            loss_ref[...] += _colsum(jnp.sum(err * err, axis=1, keepdims=True)) * (0.5 / d)
            dy = err * (1.0 / d)
            dg2_ref[...] += _colsum(dy * xh2)
            db2_ref[...] += _colsum(dy)
            dr2_ref[rows, :] = _ln_bwd(dy * g2, xh2, rstd2)

    row = lambda w: pl.BlockSpec((tm, w), lambda i: (i, 0))
    act = [2 * sz for _, sz in slabs]
    return pl.pallas_call(
        body, name="fwd_mlp", grid=(n,),
        in_specs=[row(d), row(d), ANY, ANY, ANY, _full(vd.shape)],
        out_specs=[row(sz) for sz in act] + [row(d), row(d), _full((8, LANES)), _full((1, d)), _full((1, d))],
        out_shape=[jax.ShapeDtypeStruct((t, sz), BF16) for sz in act]
        + [jax.ShapeDtypeStruct((t, d), BF16), jax.ShapeDtypeStruct((t, d), F32),
           jax.ShapeDtypeStruct((8, LANES), F32), jax.ShapeDtypeStruct((1, d), F32), jax.ShapeDtypeStruct((1, d), F32)],
        scratch_shapes=[pltpu.VMEM(wg.shape, BF16), pltpu.VMEM(wu.shape, BF16), pltpu.VMEM(wd.shape, BF16), pltpu.SemaphoreType.DMA((3,))],
        compiler_params=_cparams(),
    )(saved_f32, tgt, wg, wu, wd, vd)


def _bwd_mlp_slab(j, slab, dr2, prev, x1b, gate_up, wg, wu, wd, alpha, tm):
    t, d = dr2.shape
    off, sz = slab
    n = t // tm
    first = prev is None

    def body(*refs):
        if first:
            dr_ref, x1b_ref, gu_ref, wg_hbm, wu_hbm, wd_hbm = refs[:6]
        else:
            dr_ref, dxp_ref, x1b_ref, gu_ref, wg_hbm, wu_hbm, wd_hbm = refs[:7]
        dx_ref, dwg_hbm, dwu_hbm, dwd_hbm, dwg16_hbm, dwu16_hbm, dwd16_hbm, ag, au, ad, wg_v, wu_v, wd_v, copy_sems = refs[-14:]

        @pl.when(pl.program_id(0) == 0)
        def _():
            loads = _start_copies(copy_sems, [(src.at[pl.ds(off, sz)], dst) for src, dst in ((wg_hbm, wg_v), (wu_hbm, wu_v), (wd_hbm, wd_v))])
            ag[...] = jnp.zeros_like(ag)
            au[...] = jnp.zeros_like(au)
            ad[...] = jnp.zeros_like(ad)
            for cp in loads:
                cp.wait()

        dr = dr_ref[...]
        drb = dr.astype(BF16)
        x1b = x1b_ref[...]
        gpv = gu_ref[:, 0:sz].astype(F32)
        upv = gu_ref[:, sz:2 * sz].astype(F32)
        dh = _dot_nt(drb, wd_v[...])
        sg = _sigmoid(gpv)
        silu = gpv * sg
        ad[...] += _dot_tn((silu * upv).astype(BF16), drb)
        dgp = (dh * upv * (sg * (1.0 + gpv * (1.0 - sg)))).astype(BF16)
        dup = (dh * silu).astype(BF16)
        ag[...] += _dot_tn(dgp, x1b)
        au[...] += _dot_tn(dup, x1b)
        base = alpha * dr if first else dxp_ref[...]
        dx_ref[...] = base + _dot(dgp, wg_v[...]) + _dot(dup, wu_v[...])

        @pl.when(pl.program_id(0) == n - 1)
        def _():
            rows = pl.ds(off, sz)
            stores = _start_copies(copy_sems, [(ag, dwg_hbm.at[rows]), (au, dwu_hbm.at[rows]), (ad, dwd_hbm.at[rows])])
            for acc, stage in ((ag, wg_v), (au, wu_v), (ad, wd_v)):
                stage[...] = acc[...].astype(BF16)
            stores += _start_copies(copy_sems, [(wg_v, dwg16_hbm.at[rows]), (wu_v, dwu16_hbm.at[rows]), (wd_v, dwd16_hbm.at[rows])], first=3)
            for cp in stores:
                cp.wait()


    row = lambda w: pl.BlockSpec((tm, w), lambda i: (i, 0))
    ins = [dr2] + ([] if first else [prev[0]]) + [x1b, gate_up, wg, wu, wd] + ([] if first else list(prev[1:]))
    in_specs = [row(d)] + ([] if first else [row(d)]) + [row(d), row(2 * sz), ANY, ANY, ANY] + ([] if first else [ANY] * 6)
    return pl.pallas_call(
        body, name=f"bwd_mlp_{j}", grid=(n,),
        in_specs=in_specs,
        out_specs=[row(d)] + [ANY] * 6,
        out_shape=[jax.ShapeDtypeStruct((t, d), F32)] + [jax.ShapeDtypeStruct(wg.shape, F32)] * 3 + [jax.ShapeDtypeStruct(wg.shape, BF16)] * 3,
        scratch_shapes=[pltpu.VMEM((sz, d), F32)] * 3 + [pltpu.VMEM((sz, d), BF16)] * 3 + [pltpu.SemaphoreType.DMA((6,))],
        input_output_aliases={} if first else {7 + a: 1 + a for a in range(6)},
        compiler_params=_cparams(),
    )(*ins)


SMALL_VD = CHUNK
SMALL_CW = CHUNK + 8
SMALL_VQ = CHUNK + 8
SMALL_LOSS = CHUNK + 16
SMALL_BS = CHUNK + 24


def _small_rows(kwp):
    return -(-(SMALL_CW + max(kwp, 24 + SUBLANES)) // 16) * 16


def _bwd_mix(dx1, saved, wi, wo, wstt, cwf, tabs, vq, vd, mlp_small, token, alpha, tm):
    saved_f32, saved_bf16, hf_s = saved
    t, d = dx1.shape
    q = wi.shape[2]
    nc, n_pairs = CONV_BLOCK // CHUNK, q // LANES
    n = t // tm
    nb = tm // CONV_BLOCK
    assert tm % CONV_BLOCK == 0

    def body(dx1_ref, f32_ref, bf16_ref, hf_ref,
             wi_hbm, wo_hbm, wstt_ref, cwf_ref, fwd_ref, fwd_halo_ref, shift_ref, taps_ref, inv_ref, inv_taps_ref, vq_ref, vd_ref,
             loss_ref, dg2_ref, db2_ref, token_ref,
             gx_ref, dwi_hbm, dwo_hbm, small_hbm, dwi16_hbm, dwo16_hbm,
             wi_v, wo_v, awi, awo, dyb_ref, later_ref, dbm_ref, gf_ref, dgf_ref, small_ref, copy_sems):
        xh_ref, zu_ref, mg_ref, vhat_ref, gv_ref, yhat_ref, rs_ref, xb_ref, pag_ref, y_ref, vnb_ref = _saved_views(f32_ref, bf16_ref, d, q)
        i = pl.program_id(0)

        @pl.when(i == 0)
        def _():
            loads = _start_copies(copy_sems, [(wi_hbm, wi_v), (wo_hbm, wo_v)])
            for r in (awi, awo, small_ref, dbm_ref, dgf_ref, dyb_ref, later_ref):
                r[...] = jnp.zeros_like(r)
            gf_ref[...] = _dot(taps_ref[...], _split(cwf_ref[...], True))
            for cp in loads:
                cp.wait()

        dr1b_parts, dproj_parts = [None] * nb, [None] * nb

        def sub_tile(b):
            rows = slice(b * CONV_BLOCK, (b + 1) * CONV_BLOCK)
            dx1v = dx1_ref[rows, :]
            xh = xh_ref[rows, :]
            rsv = rs_ref[rows, :]
            small_ref[SMALL_VD + VD_LN1_G:SMALL_VD + VD_LN1_G + 1, :] += _colsum(dx1v * xh)
            small_ref[SMALL_VD + VD_LN1_B:SMALL_VD + VD_LN1_B + 1, :] += _colsum(dx1v)
            dr1 = _ln_bwd(dx1v * vd_ref[VD_LN1_G:VD_LN1_G + 1, :], xh, rsv[:, RS_LN1:RS_LN1 + 1])
            dr1b = dr1.astype(BF16)
            yield
            dy = _dot_nt(dr1b, wo_v[...])
            yield
            vhat = vhat_ref[rows, :]
            sgu_g = vq_ref[VQ_SGU_G:VQ_SGU_G + 1, :]
            doa = dy[:, 0:q]
            dm = doa * zu_ref[rows, :]
            dpu = (doa * mg_ref[rows, :]).astype(BF16)
            acc = dm[0:CHUNK]
            for c in range(1, nc):
                acc = acc + dm[c * CHUNK:(c + 1) * CHUNK]
            dbm_ref[...] += acc
            pa = pag_ref[rows, 0:q].astype(F32)
            sg = _sigmoid(pag_ref[rows, q:2 * q].astype(F32))
            yhat = yhat_ref[rows, :]
            cln_g = vq_ref[VQ_CLN_G:VQ_CLN_G + 1, :]
            yn = yhat * cln_g + vq_ref[VQ_CLN_B:VQ_CLN_B + 1, :]
            sy = _sigmoid(yn)
            dyn = dy[:, q:2 * q] * (sy * (1.0 + yn * (1.0 - sy)))
            small_ref[SMALL_VQ + VQ_CLN_G:SMALL_VQ + VQ_CLN_G + 1, q:2 * q] += _colsum(dyn * yhat)
            small_ref[SMALL_VQ + VQ_CLN_B:SMALL_VQ + VQ_CLN_B + 1, q:2 * q] += _colsum(dyn)
            dyc = _ln_bwd(dyn * cln_g, yhat, rsv[:, RS_CONV:RS_CONV + 1])
            small_ref[SMALL_VQ + VQ_CONV_B:SMALL_VQ + VQ_CONV_B + 1, q:2 * q] += _colsum(dyc)
            dyb_ref[b, 0:CONV_BLOCK, :] = dyc
            yield
            wgrads = _mix_wgrad(dm, vnb_ref[rows, :], nc, n_pairs)
            dvn = _mix(wstt_ref, dm, nc, n_pairs)
            own = _dot(fwd_ref[...], _split(dyb_ref[b]))
            with_later = own + _dot(fwd_halo_ref[...], _split(later_ref[...]))
            later_ref[...] = dyb_ref[b, 0:HALO, :]
            yield
            for p, g in enumerate(wgrads):
                for half in range(2):
                    small_ref[0:CHUNK, (2 * p + half) * CHUNK:(2 * p + half + 1) * CHUNK] += g[half * CHUNK:(half + 1) * CHUNK]
            small_ref[SMALL_VQ + VQ_SGU_G:SMALL_VQ + VQ_SGU_G + 1, q:2 * q] += _colsum(dvn * vhat)
            small_ref[SMALL_VQ + VQ_SGU_B:SMALL_VQ + VQ_SGU_B + 1, q:2 * q] += _colsum(dvn)
            dpv = (_ln_bwd(dvn * sgu_g, vhat, rsv[:, RS_SGU:RS_SGU + 1]) * gv_ref[rows, :]).astype(BF16)
            dgf_ref[...] += _cmul(_cmul(own, shift_ref[...]), hf_ref[b * 2 * DFT_F:(b + 1) * 2 * DFT_F, :], conj_b=True)
            product = _split(_cmul(with_later, gf_ref[...], conj_b=True))
            yield
            dh = _dot(inv_ref[...], product)
            yield
            da = (dh * sg).astype(BF16)
            dg = (dh * pa * (sg * (1.0 - sg))).astype(BF16)
            yield
            gx = alpha * dr1
            for dpj, wj in zip((dpu, dpv, da, dg), range(4)):
                gx = gx + _dot_nt(dpj, wi_v[wj])
            gx_ref[rows, :] = gx
            dr1b_parts[b], dproj_parts[b] = dr1b, (dpu, dpv, da, dg)

        _interleave([sub_tile(b) for b in reversed(range(nb))])

        awo[...] += _dot_tn(y_ref[...], jnp.concatenate(dr1b_parts, axis=0))
        xb = xb_ref[...]
        for j in range(4):
            awi[j] += _dot_tn(xb, jnp.concatenate([part[j] for part in dproj_parts], axis=0))

        @pl.when(i == n - 1)
        def _():
            stores = _start_copies(copy_sems, [(awi, dwi_hbm), (awo, dwo_hbm)])
            wi_v[...] = awi[...].astype(BF16)
            wo_v[...] = awo[...].astype(BF16)
            stores += _start_copies(copy_sems, [(wi_v, dwi16_hbm), (wo_v, dwo16_hbm)], first=3)
            lane = lax.broadcasted_iota(jnp.int32, (CHUNK, LANES), 1)
            low = lane < HEAD_DIM
            dbs = jnp.zeros((CHUNK, LANES), F32)
            for p in range(n_pairs):
                grp = dbm_ref[:, p * LANES:(p + 1) * LANES]
                dbs = jnp.where(lane == 2 * p, jnp.sum(jnp.where(low, grp, 0.0), axis=1, keepdims=True), dbs)
                dbs = jnp.where(lane == 2 * p + 1, jnp.sum(jnp.where(low, 0.0, grp), axis=1, keepdims=True), dbs)
            tril = lax.broadcasted_iota(jnp.int32, (CHUNK, CHUNK), 0) >= lax.broadcasted_iota(jnp.int32, (CHUNK, CHUNK), 1)
            for h in range(2 * n_pairs):
                block = small_ref[0:CHUNK, h * CHUNK:(h + 1) * CHUNK]
                small_ref[0:CHUNK, h * CHUNK:(h + 1) * CHUNK] = jnp.where(tril, block, 0.0)
            small_ref[SMALL_VD + VD_LN2_G:SMALL_VD + VD_LN2_G + 1, :] = dg2_ref[...]
            small_ref[SMALL_VD + VD_LN2_B:SMALL_VD + VD_LN2_B + 1, :] = db2_ref[...]
            small_ref[SMALL_CW:SMALL_CW + kwp, 0:q] = _dot(inv_taps_ref[...], _split(dgf_ref[...], True))
            small_ref[SMALL_LOSS:SMALL_LOSS + SUBLANES, q:q + LANES] = loss_ref[...]
            small_ref[SMALL_BS:SMALL_BS + SUBLANES, q:q + LANES] = jnp.transpose(dbs)[0:SUBLANES]
            stores += _start_copies(copy_sems, [(small_ref, small_hbm)], first=2)
            for cp in stores:
                cp.wait()

    rev = lambda w: pl.BlockSpec((tm, w), lambda i: (n - 1 - i, 0))
    kwp = cwf.shape[0]
    small = jax.ShapeDtypeStruct((_small_rows(kwp), 2 * q), F32)
    small_ins = [wstt, cwf, tabs["fwd"], tabs["fwd_halo"], tabs["shift"], tabs["taps"], tabs["inv_in"], tabs["inv_taps"], vq, vd,
                 *mlp_small]
    return pl.pallas_call(
        body, name="bwd_mix", grid=(n,),
        in_specs=[rev(d), rev(saved_f32.shape[1]), rev(saved_bf16.shape[1]),
                  pl.BlockSpec((nb * 2 * DFT_F, q), lambda i: (n - 1 - i, 0)), ANY, ANY] + [_full(a.shape) for a in small_ins] + [ANY],
        out_specs=[rev(d)] + [ANY] * 5,
        out_shape=[jax.ShapeDtypeStruct((t, d), F32), jax.ShapeDtypeStruct(wi.shape, F32), jax.ShapeDtypeStruct(wo.shape, F32), small,
                   jax.ShapeDtypeStruct(wi.shape, BF16), jax.ShapeDtypeStruct(wo.shape, BF16)],
        scratch_shapes=[pltpu.VMEM(wi.shape, BF16), pltpu.VMEM(wo.shape, BF16), pltpu.VMEM(wi.shape, F32), pltpu.VMEM(wo.shape, F32),
                        pltpu.VMEM((nb, DFT_N, q), F32), pltpu.VMEM((HALO, q), F32),
                        pltpu.VMEM((CHUNK, q), F32), pltpu.VMEM((2 * DFT_F, q), F32), pltpu.VMEM((2 * DFT_F, q), F32),
                        pltpu.VMEM(small.shape, F32), pltpu.SemaphoreType.DMA((5,))],
        compiler_params=_cparams(),
    )(dx1, saved_f32, saved_bf16, hf_s, wi, wo, *small_ins, token)


def _coords():
    return tuple(lax.axis_index(a) for a in MESH_AXES)


def _other_chips(x, y):
    return [(1 - x, y), (x, 1 - y), (1 - x, 1 - y)]


def _remote(src, dst, send_sem, recv_sem, to):
    return pltpu.make_async_remote_copy(src_ref=src, dst_ref=dst, send_sem=send_sem, recv_sem=recv_sem,
                                        device_id=to, device_id_type=MESH_ID)


def _hbm_call(body, name, ins, out_shape, scratch_shapes, aliases=None):
    return pl.pallas_call(
        body, name=name, in_specs=[ANY] * len(ins), out_specs=[ANY] * len(out_shape), out_shape=out_shape,
        scratch_shapes=scratch_shapes, input_output_aliases=aliases or {},
    )(*ins)


class _Gather:
    def __init__(self, bufs, send_sems, recv_sems, own=None):
        self.bufs, self.send_sems, self.recv_sems, self.own = bufs, send_sems, recv_sems, own
        self.x, self.y, self.c = _coords()

    def _copies(self, stage):
        x, y, c = self.x, self.y, self.c
        for a, buf in enumerate(self.bufs):
            hr = buf.shape[1] // 2
            for j, chip in enumerate(_other_chips(x, y)):
                if stage == "ici_out":
                    ref, k, to = buf.at[2 * x + y, pl.ds(c * hr, hr)], j, (*chip, c)
                    if self.own:
                        yield _remote(self.own[a].at[pl.ds(c * hr, hr)], ref, self.send_sems.at[a, k], self.recv_sems.at[a, k], to)
                        continue
                elif stage == "ici_in":
                    ref, k, to = buf.at[2 * chip[0] + chip[1], pl.ds(c * hr, hr)], j, (*chip, c)
                elif stage == "d2d_out":
                    ref, k, to = buf.at[2 * chip[0] + chip[1], pl.ds(c * hr, hr)], 3 + j, (x, y, 1 - c)
                else:
                    ref, k, to = buf.at[2 * chip[0] + chip[1], pl.ds((1 - c) * hr, hr)], 3 + j, (x, y, 1 - c)
                yield _remote(ref, ref, self.send_sems.at[a, k], self.recv_sems.at[a, k], to)

    def start(self):
        for cp in self._copies("ici_out"):
            cp.start()

    def forward(self):
        for landed, onward in zip(self._copies("ici_in"), self._copies("d2d_out")):
            landed.wait_recv()
            onward.start()

    def finish(self):
        for cp in self._copies("d2d_in"):
            cp.wait_recv()
        for stage in ("ici_out", "d2d_out"):
            for cp in self._copies(stage):
                cp.wait_send()


def _gather_sems(n):
    return [pltpu.SemaphoreType.DMA((n, 6)), pltpu.SemaphoreType.DMA((n, 6))]


def _prep_gather(gathered, local, conv_w, kwp):
    kw, _, cshard = conv_w.shape
    shards = list(gathered) + list(local)
    n, n_g = len(shards), len(gathered)

    def body(*refs):
        src, cw_ref, out = refs[:n], refs[n], refs[n + 1:2 * n + 2]
        wide, narrow, taps = refs[2 * n + 2:3 * n + 2], refs[3 * n + 2:4 * n + 2], refs[4 * n + 2]
        load_sems, store_sems, send_sems, recv_sems = refs[4 * n + 3:]
        x, y, _ = _coords()
        loads = [pltpu.make_async_copy(src[a], wide[a], load_sems.at[a]) for a in range(n)]
        stores = [pltpu.make_async_copy(narrow[a], out[a].at[2 * x + y], store_sems.at[a]) for a in range(n)]
        stores.append(pltpu.make_async_copy(taps, out[n].at[2 * x + y], store_sems.at[n]))
        for cp in loads:
            cp.start()

        def cast(a):
            loads[a].wait()
            narrow[a][...] = wide[a][...].astype(BF16)
            stores[a].start()

        for a in range(n_g):
            cast(a)
        for tap in range(kw):
            taps[tap:tap + 1, :] = cw_ref[tap]
        taps[kw:kwp, :] = jnp.zeros((kwp - kw, cshard), F32)
        stores[n].start()
        g = _Gather(list(out[:n_g]) + [out[n]], send_sems, recv_sems, own=list(narrow[:n_g]) + [taps])
        g.start()
        for a in range(n_g, n):
            cast(a)
        g.forward()
        g.finish()
        for cp in stores:
            cp.wait()

    out_shape = [jax.ShapeDtypeStruct((4,) + a.shape, BF16) for a in shards] + [jax.ShapeDtypeStruct((4, kwp, cshard), F32)]
    scratch = ([pltpu.VMEM(a.shape, F32) for a in shards] + [pltpu.VMEM(a.shape, BF16) for a in shards] + [pltpu.VMEM((kwp, cshard), F32)]
               + [pltpu.SemaphoreType.DMA((n,)), pltpu.SemaphoreType.DMA((n + 1,))] + _gather_sems(n_g + 1))
    res = pl.pallas_call(
        body, name="prep_gather", in_specs=[ANY] * n + [pl.BlockSpec(memory_space=pltpu.VMEM)], out_specs=[ANY] * (n + 1),
        out_shape=out_shape, scratch_shapes=scratch, compiler_params=pltpu.CompilerParams(vmem_limit_bytes=VMEM_LIMIT),
    )(*shards, conv_w)
    return list(res[:n_g]), list(res[n_g:n]), res[n]


def _pair_reduce(name, partials, payloads, c_arr, out_dtypes):
    n = len(partials)
    counts = [g.shape[0] for g in partials]
    first = [sum(counts[:a]) for a in range(n)]
    steps = sum(counts)

    def body(c_ref, *refs):
        own, travelling, out, land = (refs[k * n:(k + 1) * n] for k in range(4))
        send_sems, recv_sems = refs[4 * n:]
        i = pl.program_id(0)
        x, y, c = _coords()

        def copy(a, q):
            return _remote(travelling[a].at[q, 1 - c], land[a].at[q], send_sems.at[first[a] + q], recv_sems.at[first[a] + q],
                           (x, y, 1 - c))

        blocks = [(a, q) for a in range(n) for q in range(counts[a])]

        @pl.when(i == 0)
        def _():
            for a, q in blocks:
                copy(a, q).start()

        for a in range(n):
            @pl.when((i >= first[a]) & (i < first[a] + counts[a]))
            def _(a=a):
                q = i - first[a]
                copy(a, q).wait_recv()
                out[a][...] = (own[a][...] + land[a][q].astype(F32)).astype(out_dtypes[a])

        @pl.when(i == steps - 1)
        def _():
            for a, q in blocks:
                copy(a, q).wait_send()

    at = lambda a, i: jnp.clip(i - first[a], 0, counts[a] - 1)
    in_specs = [pl.BlockSpec((None, None) + g.shape[2:], lambda i, cr, a=a: (at(a, i), cr[0], 0, 0)) for a, g in enumerate(partials)]
    out_specs = [pl.BlockSpec((None,) + g.shape[2:], lambda i, cr, a=a: (at(a, i), 0, 0)) for a, g in enumerate(partials)]
    grid_spec = pltpu.PrefetchScalarGridSpec(
        num_scalar_prefetch=1, grid=(steps,), in_specs=in_specs + [ANY] * n, out_specs=out_specs,
        scratch_shapes=[pltpu.VMEM((g.shape[0],) + g.shape[2:], p.dtype) for g, p in zip(partials, payloads)]
        + [pltpu.SemaphoreType.DMA((steps,)), pltpu.SemaphoreType.DMA((steps,))])
    out_shape = [jax.ShapeDtypeStruct((g.shape[0],) + g.shape[2:], dt) for g, dt in zip(partials, out_dtypes)]
    return list(pl.pallas_call(body, name=name, grid_spec=grid_spec, out_shape=out_shape, compiler_params=_cparams())(
        c_arr, *partials, *payloads))


class _Exchange:
    def __init__(self, src, dst, send_sems, recv_sems):
        self.src, self.dst, self.send_sems, self.recv_sems = src, dst, send_sems, recv_sems
        self.x, self.y, self.c = _coords()

    def _copies(self, incoming):
        x, y, c = self.x, self.y, self.c
        for a, (s, d) in enumerate(zip(self.src, self.dst)):
            for j, chip in enumerate(_other_chips(x, y)):
                slot = 2 * chip[0] + chip[1]
                if incoming:
                    out, into = d.at[slot], d.at[slot]
                else:
                    out, into = (s.at[slot] if len(s.shape) == 3 else s), d.at[2 * x + y]
                yield _remote(out, into, self.send_sems.at[a, j], self.recv_sems.at[a, j], (*chip, c))

    def start(self):
        for cp in self._copies(False):
            cp.start()

    def finish(self):
        for cp in self._copies(True):
            cp.wait_recv()
        for cp in self._copies(False):
            cp.wait_send()


def _exchange_shapes(arrs):
    return [jax.ShapeDtypeStruct((4,) + s.shape[-2:], s.dtype) for s in arrs]


class _FlatSems:
    def __init__(self, ref):
        self.ref = ref

    @property
    def at(self):
        return self

    def __getitem__(self, idx):
        return self.ref.at[3 * idx[0] + idx[1]]


HBM = pl.BlockSpec(memory_space=pltpu.HBM)
SEM = pl.BlockSpec(memory_space=pltpu.SEMAPHORE)
DATAFLOW = pltpu.SideEffectType.DATAFLOW_SIDE_EFFECTING


def _exchange_start(name, arrs):
    n = len(arrs)
    lands = _exchange_shapes(arrs)

    def body(*refs):
        src, land = refs[:n], refs[n:2 * n]
        send_sems, recv_sems = refs[2 * n:2 * n + 2]
        token = refs[-1]
        _Exchange(src, land, _FlatSems(send_sems), _FlatSems(recv_sems)).start()
        token[...] = jnp.zeros_like(token)

    hbm = lambda a: pltpu.with_memory_space_constraint(a, pltpu.HBM)
    outs = pl.pallas_call(
        body, name=name,
        out_shape=(pltpu.SemaphoreType.DMA((3 * n,)), pltpu.SemaphoreType.DMA((3 * n,)),
                   *[pltpu.HBM(a.shape, a.dtype) for a in arrs], *[pltpu.HBM(s.shape, s.dtype) for s in lands],
                   jax.ShapeDtypeStruct((SUBLANES, LANES), F32)),
        in_specs=[HBM] * (2 * n), out_specs=(SEM, SEM, *[HBM] * (2 * n), pl.BlockSpec(memory_space=pltpu.VMEM)),
        input_output_aliases={a: 2 + a for a in range(2 * n)},
        compiler_params=pltpu.CompilerParams(has_side_effects=DATAFLOW),
    )(*[hbm(a) for a in arrs], *[hbm(lax.empty(s.shape, s.dtype)) for s in lands])
    return outs[:-1], outs[-1]


def _exchange_wait(name, started, after):
    send_sems, recv_sems, *bufs = started
    n = len(bufs) // 2

    def body(*refs):
        src, land = refs[:n], refs[n:2 * n]
        send_sems, recv_sems = refs[2 * n:2 * n + 2]
        _Exchange(src, land, _FlatSems(send_sems), _FlatSems(recv_sems)).finish()

    outs = pl.pallas_call(
        body, name=name,
        out_shape=tuple(pltpu.HBM(b.shape, b.dtype) for b in bufs),
        in_specs=[HBM] * (2 * n) + [SEM, SEM] + [ANY] * len(after), out_specs=tuple([HBM] * (2 * n)),
        input_output_aliases={a: a for a in range(2 * n)},
        compiler_params=pltpu.CompilerParams(has_side_effects=DATAFLOW),
    )(*bufs, send_sems, recv_sems, *after)
    return list(outs[:n]), list(outs[n:])


def _pair_gather(name, halves):
    n = len(halves)

    def body(*refs):
        src, dst = refs[:n], refs[n:2 * n]
        send_sems, recv_sems = refs[2 * n:]
        x, y, c = _coords()
        copies = [_remote(src[a], dst[a], send_sems.at[a], recv_sems.at[a], (x, y, 1 - c)) for a in range(n)]
        for cp in copies:
            cp.start()
        for cp in copies:
            cp.wait()

    outs = [jax.ShapeDtypeStruct(s.shape, s.dtype) for s in halves]
    return _hbm_call(body, name, halves, outs, [pltpu.SemaphoreType.DMA((n,)), pltpu.SemaphoreType.DMA((n,))])


def _chip_sum(a, parts, own, me_arr, after):
    _, hr, cc = parts.shape

    def body(me_ref, p_ref, own_ref, after_ref, o_ref):
        for mine in range(4):
            @pl.when(me_ref[0] == mine)
            def _():
                term = lambda j: (own_ref if j == mine else p_ref.at[j])[...].astype(F32)
                o_ref[...] = ((term(0) + term(1)) + term(2)) + term(3)

    own_spec = (pl.BlockSpec((None, hr, cc), lambda i, me: (me[0], 0, 0)) if own.ndim == 3
                else pl.BlockSpec((hr, cc), lambda i, me: (0, 0)))
    grid_spec = pltpu.PrefetchScalarGridSpec(
        num_scalar_prefetch=1, grid=(1,),
        in_specs=[pl.BlockSpec((4, hr, cc), lambda i, me: (0, 0, 0)), own_spec, ANY],
        out_specs=pl.BlockSpec((hr, cc), lambda i, me: (0, 0)))
    return pl.pallas_call(body, name=f"chip_sum_{a}", grid_spec=grid_spec, out_shape=jax.ShapeDtypeStruct((hr, cc), F32),
                          compiler_params=_cparams())(me_arr, parts, own, after)


def _row_block(rows, cols, limit=1 << 20):
    best = 8
    for tr in range(8, rows + 1, 8):
        if rows % tr == 0 and tr * cols * 4 <= limit:
            best = tr
    return best


def _adamw(name, w, g_mine, g_other, m, v, c_arr):
    r, c = w.shape
    hr, cg = g_mine.shape
    tr = hr if r % hr == 0 and hr * cg * 4 <= (3 << 19) else math.gcd(_row_block(hr, cg), r)
    per_half = hr // tr
    bc1 = 1.0 - ADAM_B1 ** ADAM_STEP
    bc2 = 1.0 - ADAM_B2 ** ADAM_STEP

    def body(c_ref, w_ref, gm_ref, go_ref, m_ref, v_ref, go, do, mo, vo):
        gv = jnp.where(pl.program_id(0) // per_half == c_ref[0], gm_ref[:, 0:c], go_ref[:, 0:c])
        mn = ADAM_B1 * m_ref[...] + (1.0 - ADAM_B1) * gv
        vn = ADAM_B2 * v_ref[...] + (1.0 - ADAM_B2) * (gv * gv)
        go[...] = gv
        mo[...] = mn
        vo[...] = vn
        do[...] = -ADAM_LR * ((mn / bc1) / (jnp.sqrt(vn / bc2) + ADAM_EPS) + ADAM_WD * w_ref[...])

    blk = pl.BlockSpec((tr, c), lambda i, cr: (i, 0))
    gblk = pl.BlockSpec((tr, cg), lambda i, cr: (i % per_half, 0))
    grid_spec = pltpu.PrefetchScalarGridSpec(num_scalar_prefetch=1, grid=(r // tr,), in_specs=[blk, gblk, gblk, blk, blk],
                                             out_specs=[blk] * 4)
    return pl.pallas_call(body, name=f"adamw_{name}", grid_spec=grid_spec, out_shape=[jax.ShapeDtypeStruct((r, c), F32)] * 4,
                          compiler_params=_cparams())(c_arr, w, g_mine, g_other, m, v)


SMALL_Q = ("sgu_ln_g", "sgu_ln_b", "conv_b", "conv_ln_g", "conv_ln_b")
SMALL_D = ("ln1_g", "ln1_b", "ln2_g", "ln2_b")


def _adamw_small(g_mine, g_other, c_arr, me_arr, params):
    names = list(SMALL_Q) + list(SMALL_D) + ["w_s", "b_s", "conv_w"]
    hr, width = g_mine.shape
    q = width // 2
    heads = params["w_s"][0].shape[1]
    kw, _, cshard = params["conv_w"][0].shape
    bc1 = 1.0 - ADAM_B1 ** ADAM_STEP
    bc2 = 1.0 - ADAM_B2 ** ADAM_STEP

    def update(w, g, m, v):
        mn = ADAM_B1 * m + (1.0 - ADAM_B1) * g
        vn = ADAM_B2 * v + (1.0 - ADAM_B2) * (g * g)
        return g, -ADAM_LR * ((mn / bc1) / (jnp.sqrt(vn / bc2) + ADAM_EPS) + ADAM_WD * w), mn, vn

    def body(c_ref, me_ref, gm_ref, go_ref, *refs):
        ins = {nm: refs[3 * k:3 * k + 3] for k, nm in enumerate(names)}
        outs = {nm: refs[3 * len(names) + 4 * k:3 * len(names) + 4 * k + 4] for k, nm in enumerate(names)}
        loss_ref, cw_ref = refs[-2:]
        first, second = gm_ref[...], go_ref[...]
        low = c_ref[0] == 0
        g_all = jnp.concatenate([jnp.where(low, first, second), jnp.where(low, second, first)], axis=0)

        def apply(nm, g, at):
            w, m, v = (r[at] for r in ins[nm])
            for o, val in zip(outs[nm], update(w, g, m, v)):
                o[at] = val

        for row, nm in enumerate(SMALL_Q):
            apply(nm, g_all[SMALL_VQ + row:SMALL_VQ + row + 1, q:2 * q], ...)
        for row, nm in enumerate(SMALL_D):
            apply(nm, g_all[SMALL_VD + row:SMALL_VD + row + 1, :], ...)
        for h in range(heads):
            apply("w_s", g_all[0:CHUNK, h * CHUNK:(h + 1) * CHUNK], (0, h))
        apply("b_s", g_all[SMALL_BS:SMALL_BS + heads, q:q + LANES], 0)
        cw_ref[...] = jnp.zeros_like(cw_ref)
        for chip in range(4):
            @pl.when(me_ref[0] == chip)
            def _():
                cw_ref[...] = g_all[SMALL_CW:SMALL_CW + cw_ref.shape[0], chip * cshard:(chip + 1) * cshard]
        for tap in range(kw):
            apply("conv_w", cw_ref[tap:tap + 1, :], tap)
        loss_ref[...] = g_all[SMALL_LOSS:SMALL_LOSS + 1, q:q + 1]

    arrays = [a for nm in names for a in params[nm]]
    out_shape = [jax.ShapeDtypeStruct(params[nm][0].shape, F32) for nm in names for _ in range(4)] + [jax.ShapeDtypeStruct((1, 1), F32)]
    whole = lambda shape: pl.BlockSpec(shape, lambda i, c, me: (0,) * len(shape))
    grid_spec = pltpu.PrefetchScalarGridSpec(
        num_scalar_prefetch=2, grid=(1,),
        in_specs=[whole(g_mine.shape), whole(g_other.shape)] + [whole(a.shape) for a in arrays],
        out_specs=[whole(s.shape) for s in out_shape],
        scratch_shapes=[pltpu.VMEM((-(-kw // SUBLANES) * SUBLANES, cshard), F32)])
    res = pl.pallas_call(body, name="adamw_small", grid_spec=grid_spec, out_shape=out_shape, compiler_params=_cparams())(
        c_arr, me_arr, g_mine, g_other, *arrays)
    return {nm: list(res[4 * k:4 * k + 4]) for k, nm in enumerate(names)}, res[-1]


def _pad_rows(a, rows):
    return jnp.pad(a, ((0, rows - a.shape[0]), (0, 0)))


def kernel(x, w_in, sgu_ln_g, sgu_ln_b, w_s, b_s, conv_w, conv_b, conv_ln_g, conv_ln_b, w_out, ln1_g, ln1_b, w_gate, w_up, w_down, ln2_g, ln2_b, loss_target, m_w_in, m_sgu_ln_g, m_sgu_ln_b, m_w_s, m_b_s, m_conv_w, m_conv_b, m_conv_ln_g, m_conv_ln_b, m_w_out, m_ln1_g, m_ln1_b, m_w_gate, m_w_up, m_w_down, m_ln2_g, m_ln2_b, v_w_in, v_sgu_ln_g, v_sgu_ln_b, v_w_s, v_b_s, v_conv_w, v_conv_b, v_conv_ln_g, v_conv_ln_b, v_w_out, v_ln1_g, v_ln1_b, v_w_gate, v_w_up, v_w_down, v_ln2_g, v_ln2_b):
    depth, d, q = w_in.shape
    assert depth == 1 and x.shape[0] == 1
    t = x.shape[1]
    heads = w_s.shape[1]
    kw, cshard = conv_w.shape[1], conv_w.shape[2]
    fs = w_gate.shape[2]
    slabs = _hidden_slabs(4 * fs)
    n_pairs = q // LANES
    assert heads * HEAD_DIM == q and q % LANES == 0 and w_s.shape[2] == CHUNK and 4 * cshard == q and kw - 1 <= HALO
    alpha = (2.0 * depth) ** 0.25
    tm = min(512, t)
    assert t % tm == 0 and tm % CHUNK == 0
    x2, tgt = x[0], loss_target[0]
    mx, my, mc = _coords()
    me = 2 * mx + my
    c_arr = jnp.reshape(mc, (1,)).astype(jnp.int32)

    kwp = -(-kw // 16) * 16
    me_arr = jnp.reshape(me, (1,)).astype(jnp.int32)
    (wi, wo), (wg, wu, wd), cw4 = _prep_gather([w_in[0], w_out[0]], [w_gate[0].T, w_up[0].T, w_down[0]],
                                              jnp.transpose(conv_w, (1, 0, 2)), kwp)
    wo = wo.reshape(d, d)
    cw = jnp.transpose(cw4, (1, 0, 2)).reshape(kwp, q)
    cwf = _pad_rows(cw[:kw][::-1], kwp)
    tabs = {name: jnp.asarray(tab) for name, tab in _dft_tables(kw, kwp, q).items()}

    wm = jnp.where(jnp.tril(jnp.ones((CHUNK, CHUNK), bool)), w_s[0], 0.0)
    wst = wm.reshape(n_pairs, 2 * CHUNK, CHUNK).astype(BF16)
    wstt = jnp.transpose(wm, (0, 2, 1)).reshape(n_pairs, 2 * CHUNK, CHUNK).astype(BF16)
    bmat = jnp.repeat(b_s[0].T, HEAD_DIM, axis=1)
    vq = _pad_rows(jnp.concatenate([sgu_ln_g, sgu_ln_b, conv_b, conv_ln_g, conv_ln_b], axis=0), 8)
    vd = _pad_rows(jnp.concatenate([ln1_g, ln1_b, ln2_g, ln2_b], axis=0), 8)

    *saved, wg, wu, wd = _fwd_mix(x2, wi, wo, wst, bmat, cwf, tabs, vq, vd, [wg, wu, wd], alpha, tm)
    wg, wu, wd = (w.reshape(4 * fs, d) for w in (wg, wu, wd))
    *acts, x1b, dr2, loss_part, dg2, db2 = _fwd_mlp(saved[0], tgt, wg, wu, wd, vd, alpha, slabs, tm)
    mlp_grads = None
    for j, slab in enumerate(slabs):
        mlp_grads = _bwd_mlp_slab(j, slab, dr2, mlp_grads, x1b, acts[j], wg, wu, wd, alpha, tm)
    dx1 = mlp_grads[0]
    mlp_halves = [b.reshape(4, 2, fs // 2, d) for b in mlp_grads[1:]]
    mlp_sums = _pair_reduce("pair_reduce_mlp", mlp_halves[:3], mlp_halves[3:], c_arr, [BF16] * 3)
    mlp_started, token = _exchange_start("exchange_mlp_start", mlp_sums)
    grad_x, dwi, dwo, small, dwi16, dwo16 = _bwd_mix(dx1, saved, wi, wo, wstt, cwf, tabs, vq, vd, (loss_part, dg2, db2), token, alpha, tm)
    mlp_sums, mlp_parts = _exchange_wait("exchange_mlp_wait", mlp_started, [dwo])

    by_halves = lambda b: b.reshape(4, 2, b.shape[1] // 2, b.shape[2])
    halves = [by_halves(dwi), by_halves(dwo.reshape(4, d // 4, d)), small.reshape(1, 2, small.shape[0] // 2, small.shape[1])]
    travelling = [by_halves(dwi16), by_halves(dwo16.reshape(4, d // 4, d)), halves[-1]]
    *sums, small_sum = _pair_reduce("pair_reduce_mix", halves, travelling, c_arr, [BF16, BF16, F32])
    sums.append(small_sum[0])
    mix_started, token = _exchange_start("exchange_mix_start", sums)

    out, raw = {}, {}

    def finish(first, names, parts, sums, after):
        mine = [_chip_sum(first + a, p, s, me_arr, after) for a, (p, s) in enumerate(zip(parts, sums))]
        other = _pair_gather(f"pair_gather_{first}", mine)
        for a, nm in enumerate(names):
            w_, m_, v_ = weights[nm]
            if nm in ("w_gate", "w_up"):
                raw[nm] = _adamw(nm, w_[0].T, mine[a], other[a], m_[0].T, v_[0].T, c_arr)
                out[nm] = [o.T for o in raw[nm]]
            else:
                raw[nm] = out[nm] = _adamw(nm, w_[0], mine[a], other[a], m_[0], v_[0], c_arr)
        return mine[-1], other[-1]

    weights = {"w_in": (w_in, m_w_in, v_w_in), "w_out": (w_out, m_w_out, v_w_out), "w_gate": (w_gate, m_w_gate, v_w_gate),
               "w_up": (w_up, m_w_up, v_w_up), "w_down": (w_down, m_w_down, v_w_down)}
    finish(2, ["w_gate", "w_up", "w_down"], mlp_parts, mlp_sums, token)
    sums, parts = _exchange_wait("exchange_mix_wait", mix_started, [raw[nm][1] for nm in ("w_gate", "w_up", "w_down")])
    small_mine, small_other = finish(5, ["w_in", "w_out"], parts, sums, parts[0])

    small_params = {
        "sgu_ln_g": (sgu_ln_g, m_sgu_ln_g, v_sgu_ln_g), "sgu_ln_b": (sgu_ln_b, m_sgu_ln_b, v_sgu_ln_b),
        "conv_b": (conv_b, m_conv_b, v_conv_b), "conv_ln_g": (conv_ln_g, m_conv_ln_g, v_conv_ln_g),
        "conv_ln_b": (conv_ln_b, m_conv_ln_b, v_conv_ln_b), "ln1_g": (ln1_g, m_ln1_g, v_ln1_g), "ln1_b": (ln1_b, m_ln1_b, v_ln1_b),
        "ln2_g": (ln2_g, m_ln2_g, v_ln2_g), "ln2_b": (ln2_b, m_ln2_b, v_ln2_b), "w_s": (w_s, m_w_s, v_w_s),
        "b_s": (b_s, m_b_s, v_b_s), "conv_w": tuple(jnp.transpose(a, (1, 0, 2)) for a in (conv_w, m_conv_w, v_conv_w))}
    small_out, loss_block = _adamw_small(small_mine, small_other, c_arr, me_arr, small_params)
    small_out["conv_w"] = [jnp.transpose(o, (1, 0, 2)) for o in small_out["conv_w"]]
    loss = loss_block.reshape(())
    names = ["w_in", "sgu_ln_g", "sgu_ln_b", "w_s", "b_s", "conv_w", "conv_b", "conv_ln_g", "conv_ln_b", "w_out",
             "ln1_g", "ln1_b", "w_gate", "w_up", "w_down", "ln2_g", "ln2_b"]
    result = [loss, grad_x[None]]
    for kind in range(4):
        for nm in names:
            result.append(out[nm][kind][None] if nm in out else small_out[nm][kind])
    return tuple(result)
```

```python
import math

import jax
import numpy as np
import jax.numpy as jnp
from jax import lax
from jax.experimental import pallas as pl
from jax.experimental.pallas import tpu as pltpu

F32 = jnp.float32
BF16 = jnp.bfloat16

LN_EPS = 1e-5
HEAD_DIM = 64
CHUNK = 128
HALO = 32
LANES = 128
MXU_N = 256
ADAM_LR, ADAM_B1, ADAM_B2, ADAM_EPS, ADAM_WD, ADAM_STEP = 0.001, 0.9, 0.999, 1e-08, 0.01, 10
VMEM_LIMIT = 63 * 1024 * 1024
MESH_AXES = ("x", "y", "c")
MESH_ID = pl.DeviceIdType.MESH


def _dot(a, b):
    return jnp.dot(a, b, preferred_element_type=F32)


def _dot_nt(a, b):
    return lax.dot_general(a, b, (((1,), (1,)), ((), ())), preferred_element_type=F32)


def _dot_tn(a, b):
    return lax.dot_general(a, b, (((0,), (0,)), ((), ())), preferred_element_type=F32)


def _sigmoid(v):
    return 1.0 / (1.0 + jnp.exp(-v))


def _gelu(v):
    cdf = 0.5 * (1.0 + lax.erf(v * (1.0 / math.sqrt(2.0))))
    pdf = jnp.exp(-0.5 * v * v) * (1.0 / math.sqrt(2.0 * math.pi))
    return v * cdf, cdf + v * pdf


def _ln_stats(v):
    mu = jnp.mean(v, axis=-1, keepdims=True)
    d = v - mu
    rstd = lax.rsqrt(jnp.mean(d * d, axis=-1, keepdims=True) + LN_EPS)
    return d * rstd, rstd


def _ln_bwd(dxhat, xhat, rstd):
    m1 = jnp.mean(dxhat, axis=-1, keepdims=True)
    m2 = jnp.mean(dxhat * xhat, axis=-1, keepdims=True)
    return rstd * (dxhat - m1 - xhat * m2)


def _colsum(v):
    return jnp.sum(v, axis=0, keepdims=True)


def _pair_lanes(v, nc, p):
    return jnp.concatenate([v[c * CHUNK:(c + 1) * CHUNK, p * LANES:(p + 1) * LANES] for c in range(nc)], axis=1)


def _unpair(parts, nc):
    rows = [jnp.concatenate([part[:, c * LANES:(c + 1) * LANES] for part in parts], axis=1) for c in range(nc)]
    return jnp.concatenate(rows, axis=0)


def _low_head(nc):
    lane = lax.broadcasted_iota(jnp.int32, (CHUNK, nc * LANES), 1)
    return (lane & (LANES - 1)) < HEAD_DIM


def _mix(wst_ref, v, nc, n_pairs):
    vb = v.astype(BF16)
    low = _low_head(nc)
    parts = []
    for p in range(n_pairs):
        r = _dot(wst_ref[p], _pair_lanes(vb, nc, p))
        parts.append(jnp.where(low, r[:CHUNK], r[CHUNK:]))
    return _unpair(parts, nc)


def _mix_wgrad(dm, vn, nc, n_pairs):
    low = _low_head(nc)
    vb = vn.astype(BF16)
    out = []
    for p in range(n_pairs):
        a = _pair_lanes(dm, nc, p)
        lhs = jnp.concatenate([jnp.where(low, a, 0.0), jnp.where(low, 0.0, a)], axis=0).astype(BF16)
        out.append(_dot_nt(lhs, _pair_lanes(vb, nc, p)))
    return out


SUBLANES = 8


CONV_BLOCK = 256
DFT_N = CONV_BLOCK + HALO
DFT_F = -(-(DFT_N // 2 + 1) // SUBLANES) * SUBLANES


def _terms(m, exact):
    hi = m.astype(np.float32).astype(BF16)
    lo = (m.astype(np.float32) - hi.astype(np.float32)).astype(BF16)
    return np.concatenate([hi, hi, lo] if exact else [hi], axis=1)


def _split(v, exact=False):
    hi = v.astype(BF16)
    if not exact:
        return hi
    lo = (v - hi.astype(F32)).astype(BF16)
    return jnp.concatenate([hi, lo, hi], axis=0)


def _dft_tables(kw, kwp, q):
    nf = DFT_N // 2 + 1
    ang = 2.0 * np.pi * np.arange(nf)[:, None] * np.arange(DFT_N)[None, :] / DFT_N
    fwd = np.zeros((2 * DFT_F, DFT_N))
    fwd[:nf], fwd[DFT_F:DFT_F + nf] = np.cos(ang), -np.sin(ang)
    weight = np.full((nf, 1), 2.0 / DFT_N)
    weight[0] = weight[-1] = 1.0 / DFT_N
    inv = np.zeros((DFT_N, 2 * DFT_F))
    inv[:, :nf], inv[:, DFT_F:DFT_F + nf] = (np.cos(ang) * weight).T, (-np.sin(ang) * weight).T
    inv_taps = np.zeros((kwp, 2 * DFT_F))
    inv_taps[:kw] = inv[kw - 1::-1][:kw]
    shift = np.zeros((2 * DFT_F, q), np.float32)
    shift[:nf], shift[DFT_F:DFT_F + nf] = np.cos(ang[:, HALO:HALO + 1]), -np.sin(ang[:, HALO:HALO + 1])
    return {"fwd": _terms(fwd, False), "fwd_halo": _terms(fwd[:, CONV_BLOCK:], False), "shift": shift,
            "inv_out": _terms(inv[HALO:HALO + CONV_BLOCK], False), "inv_in": _terms(inv[:CONV_BLOCK], False),
            "taps": _terms(fwd[:, :kwp], True), "inv_taps": _terms(inv_taps, True)}


def _cmul(a, b, conj_b=False):
    ar, ai, br, bi = a[:DFT_F], a[DFT_F:], b[:DFT_F], b[DFT_F:]
    if conj_b:
        return jnp.concatenate([ar * br + ai * bi, ai * br - ar * bi], axis=0)
    return jnp.concatenate([ar * br - ai * bi, ar * bi + ai * br], axis=0)


def _interleave(sub_tiles):
    waiting, live = list(sub_tiles), []
    while waiting or live:
        if waiting:
            live.append(waiting.pop(0))
        for g in list(live):
            try:
                next(g)
            except StopIteration:
                live.remove(g)


def _start_copies(sems, pairs, first=0):
    copies = [pltpu.make_async_copy(src, dst, sems.at[first + k]) for k, (src, dst) in enumerate(pairs)]
    for cp in copies:
        cp.start()
    return copies


def _cparams():
    return pltpu.CompilerParams(dimension_semantics=("arbitrary",), vmem_limit_bytes=VMEM_LIMIT)


def _full(shape):
    return pl.BlockSpec(shape, lambda i: (0,) * len(shape))


ANY = pl.BlockSpec(memory_space=pl.ANY)

VQ_SGU_G, VQ_SGU_B, VQ_CONV_B, VQ_CLN_G, VQ_CLN_B = range(5)
VD_LN1_G, VD_LN1_B, VD_LN2_G, VD_LN2_B = range(4)
RS_LN1, RS_SGU, RS_CONV = range(3)
RS_COLS = LANES


def _saved_widths(d, q):
    f32 = [d, q, q, q, q, q, RS_COLS]
    bf16 = [d, 2 * q, d, q]
    return f32, bf16


def _saved_views(f32_ref, bf16_ref, d, q):
    views = []
    for ref, widths in zip((f32_ref, bf16_ref), _saved_widths(d, q)):
        for k, w in enumerate(widths):
            views.append(ref.at[pl.ds(0, ref.shape[0]), pl.ds(sum(widths[:k]), w)])
    return views
def _fwd_mix(x, wi, wo, wst, bmat, cwf, tabs, vq, vd, mlp_w, alpha, tm):
    t, d = x.shape
    q = wi.shape[2]
    nc, n_pairs = CONV_BLOCK // CHUNK, q // LANES
    n = t // tm
    n_in, n_saved = 11, 3
    assert tm % CONV_BLOCK == 0

    def body(x_ref, wi_hbm, wo_hbm, wst_ref, bmat_ref, cwf_ref, fwd_ref, taps_ref, inv_ref, vq_ref, vd_ref, *rest):
        f32_ref, bf16_ref, hf_ref = rest[3:3 + n_saved]
        xh_ref, zu_ref, mg_ref, vhat_ref, gv_ref, yhat_ref, rs_ref, xb_ref, pag_ref, y_ref, vnb_ref = _saved_views(f32_ref, bf16_ref, d, q)
        gathered = rest[3 + n_saved:6 + n_saved]
        wi_v, wo_v, hb_ref, gf_ref, send_sems, recv_sems, copy_sems = rest[6 + n_saved:]
        step = pl.program_id(0)

        @pl.when(step == 0)
        def _():
            loads = _start_copies(copy_sems, [(wi_hbm, wi_v), (wo_hbm, wo_v)])
            _Gather(gathered, send_sems, recv_sems).start()
            hb_ref[...] = jnp.zeros_like(hb_ref)
            gf_ref[...] = _dot(taps_ref[...], _split(cwf_ref[...], True))
            for cp in loads:
                cp.wait()

        @pl.when(step == (3 * n) // 4)
        def _():
            _Gather(gathered, send_sems, recv_sems).forward()

        def sub_tile(b):
            rows = slice(b * CONV_BLOCK, (b + 1) * CONV_BLOCK)
            xv = x_ref[rows, :]
            xb = xv.astype(BF16)
            xb_ref[rows, :] = xb
            pu, pv, pa, pg = (_dot(xb, wi_v[j]) for j in range(4))
            yield
            pag_ref[rows, 0:q] = pa.astype(BF16)
            pag_ref[rows, q:2 * q] = pg.astype(BF16)
            zu, gu = _gelu(pu)
            zv, gv = _gelu(pv)
            vhat, rstd_v = _ln_stats(zv)
            vnb = (vhat * vq_ref[VQ_SGU_G:VQ_SGU_G + 1, :] + vq_ref[VQ_SGU_B:VQ_SGU_B + 1, :]).astype(BF16)
            hb_ref[HALO + b * CONV_BLOCK:HALO + (b + 1) * CONV_BLOCK, :] = pa * _sigmoid(pg)
            yield
            mixed = _mix(wst_ref, vnb, nc, n_pairs) + jnp.concatenate([bmat_ref[...]] * nc, axis=0)
            spectrum = _dot(fwd_ref[...], _split(hb_ref[b * CONV_BLOCK:b * CONV_BLOCK + DFT_N, :]))
            yield
            y_ref[rows, 0:q] = (zu * mixed).astype(BF16)
            zu_ref[rows, :] = zu
            mg_ref[rows, :] = mixed * gu
            vhat_ref[rows, :] = vhat
            gv_ref[rows, :] = gv
            vnb_ref[rows, :] = vnb
            hf_ref[b * 2 * DFT_F:(b + 1) * 2 * DFT_F, :] = spectrum
            product = _split(_cmul(gf_ref[...], spectrum))
            yield
            yc = _dot(inv_ref[...], product) + vq_ref[VQ_CONV_B:VQ_CONV_B + 1, :]
            yield
            yhat, rstd_c = _ln_stats(yc)
            yhat_ref[rows, :] = yhat
            yn = yhat * vq_ref[VQ_CLN_G:VQ_CLN_G + 1, :] + vq_ref[VQ_CLN_B:VQ_CLN_B + 1, :]
            y_ref[rows, q:2 * q] = (yn * _sigmoid(yn)).astype(BF16)
            yield
            r1 = alpha * xv + _dot(y_ref[rows, :], wo_v[...])
            yield
            xhat, rstd1 = _ln_stats(r1)
            xh_ref[rows, :] = xhat
            col = lax.broadcasted_iota(jnp.int32, (CONV_BLOCK, RS_COLS), 1)
            rs_ref[rows, :] = jnp.where(col == RS_LN1, rstd1, jnp.where(col == RS_SGU, rstd_v, jnp.where(col == RS_CONV, rstd_c, 0.0)))

        _interleave([sub_tile(b) for b in range(tm // CONV_BLOCK)])
        hb_ref[0:HALO, :] = hb_ref[tm:tm + HALO, :]

        @pl.when(step == n - 1)
        def _():
            _Gather(gathered, send_sems, recv_sems).finish()

    row = lambda w: pl.BlockSpec((tm, w), lambda i: (i, 0))
    widths = [(sum(w), dt) for w, dt in zip(_saved_widths(d, q), (F32, BF16))]
    small_ins = [wst, bmat, cwf, tabs["fwd"], tabs["taps"], tabs["inv_out"], vq, vd]
    return pl.pallas_call(
        body, name="fwd_mix", grid=(n,),
        in_specs=[row(d), ANY, ANY] + [_full(a.shape) for a in small_ins] + [ANY] * 3,
        out_specs=[row(w) for w, _ in widths] + [pl.BlockSpec((tm // CONV_BLOCK * 2 * DFT_F, q), lambda i: (i, 0))] + [ANY] * 3,
        out_shape=[jax.ShapeDtypeStruct((t, w), dt) for w, dt in widths] + [jax.ShapeDtypeStruct((t // CONV_BLOCK * 2 * DFT_F, q), F32)]
        + [jax.ShapeDtypeStruct(b.shape, b.dtype) for b in mlp_w],
        scratch_shapes=[pltpu.VMEM(wi.shape, BF16), pltpu.VMEM(wo.shape, BF16), pltpu.VMEM((HALO + tm, q), F32),
                        pltpu.VMEM((2 * DFT_F, q), F32)] + _gather_sems(3) + [pltpu.SemaphoreType.DMA((2,))],
        input_output_aliases={n_in + a: n_saved + a for a in range(3)},
        compiler_params=_cparams(),
    )(x, wi, wo, *small_ins, *mlp_w)


MLP_SLABS = 4


def _hidden_slabs(f):
    assert f % MXU_N == 0
    tiles = f // MXU_N
    sizes = [(tiles // MLP_SLABS + (1 if j < tiles % MLP_SLABS else 0)) * MXU_N for j in range(MLP_SLABS)]
    return [(sum(sizes[:j]), sz) for j, sz in enumerate(sizes) if sz]


def _fwd_mlp(saved_f32, tgt, wg, wu, wd, vd, alpha, slabs, tm):
    t, d = tgt.shape
    n = t // tm
    ns = len(slabs)
    half = tm // 2 if tm % 32 == 0 else tm

    def body(xh_ref, tgt_ref, wg_hbm, wu_hbm, wd_hbm, vd_ref, *rest):
        gp_refs = [r.at[pl.ds(0, tm), pl.ds(0, sz)] for r, (_, sz) in zip(rest[:ns], slabs)]
        up_refs = [r.at[pl.ds(0, tm), pl.ds(sz, sz)] for r, (_, sz) in zip(rest[:ns], slabs)]
        x1b_ref, dr2_ref, loss_ref, dg2_ref, db2_ref, wg_v, wu_v, wd_v, copy_sems = rest[ns:]

        @pl.when(pl.program_id(0) == 0)
        def _():
            loads = _start_copies(copy_sems, [(wg_hbm, wg_v), (wu_hbm, wu_v), (wd_hbm, wd_v)])
            loss_ref[...] = jnp.zeros_like(loss_ref)
            dg2_ref[...] = jnp.zeros_like(dg2_ref)
            db2_ref[...] = jnp.zeros_like(db2_ref)
            for cp in loads:
                cp.wait()

        g2 = vd_ref[VD_LN2_G:VD_LN2_G + 1, :]

        for r0 in range(0, tm, half):
            rows = slice(r0, r0 + half)
            x1 = xh_ref[rows, :] * vd_ref[VD_LN1_G:VD_LN1_G + 1, :] + vd_ref[VD_LN1_B:VD_LN1_B + 1, :]
            x1b = x1.astype(BF16)
            x1b_ref[rows, :] = x1b
            acc = alpha * x1
            for (off, sz), gp_ref, up_ref in zip(slabs, gp_refs, up_refs):
                gp = _dot_nt(x1b, wg_v[off:off + sz, :])
                up = _dot_nt(x1b, wu_v[off:off + sz, :])
                gp_ref[rows, :] = gp.astype(BF16)
                up_ref[rows, :] = up.astype(BF16)
                acc = acc + _dot((gp * _sigmoid(gp) * up).astype(BF16), wd_v[off:off + sz, :])
            xh2, rstd2 = _ln_stats(acc)
            err = xh2 * g2 + vd_ref[VD_LN2_B:VD_LN2_B + 1, :] - tgt_ref[rows, :]
            loss_ref[...] += _colsum(jnp.sum(err * err, axis=1, keepdims=True)) * (0.5 / d)
            dy = err * (1.0 / d)
            dg2_ref[...] += _colsum(dy * xh2)
            db2_ref[...] += _colsum(dy)
            dr2_ref[rows, :] = _ln_bwd(dy * g2, xh2, rstd2)

    row = lambda w: pl.BlockSpec((tm, w), lambda i: (i, 0))
    act = [2 * sz for _, sz in slabs]
    return pl.pallas_call(
        body, name="fwd_mlp", grid=(n,),
        in_specs=[row(d), row(d), ANY, ANY, ANY, _full(vd.shape)],
        out_specs=[row(sz) for sz in act] + [row(d), row(d), _full((8, LANES)), _full((1, d)), _full((1, d))],
        out_shape=[jax.ShapeDtypeStruct((t, sz), BF16) for sz in act]
        + [jax.ShapeDtypeStruct((t, d), BF16), jax.ShapeDtypeStruct((t, d), F32),
           jax.ShapeDtypeStruct((8, LANES), F32), jax.ShapeDtypeStruct((1, d), F32), jax.ShapeDtypeStruct((1, d), F32)],
        scratch_shapes=[pltpu.VMEM(wg.shape, BF16), pltpu.VMEM(wu.shape, BF16), pltpu.VMEM(wd.shape, BF16), pltpu.SemaphoreType.DMA((3,))],
        compiler_params=_cparams(),
    )(saved_f32, tgt, wg, wu, wd, vd)


def _bwd_mlp_slab(j, slab, dr2, prev, x1b, gate_up, wg, wu, wd, alpha, tm):
    t, d = dr2.shape
    off, sz = slab
    n = t // tm
    first = prev is None

    def body(*refs):
        if first:
            dr_ref, x1b_ref, gu_ref, wg_hbm, wu_hbm, wd_hbm = refs[:6]
        else:
            dr_ref, dxp_ref, x1b_ref, gu_ref, wg_hbm, wu_hbm, wd_hbm = refs[:7]
        dx_ref, dwg_hbm, dwu_hbm, dwd_hbm, dwg16_hbm, dwu16_hbm, dwd16_hbm, ag, au, ad, wg_v, wu_v, wd_v, copy_sems = refs[-14:]

        @pl.when(pl.program_id(0) == 0)
        def _():
            loads = _start_copies(copy_sems, [(src.at[pl.ds(off, sz)], dst) for src, dst in ((wg_hbm, wg_v), (wu_hbm, wu_v), (wd_hbm, wd_v))])
            ag[...] = jnp.zeros_like(ag)
            au[...] = jnp.zeros_like(au)
            ad[...] = jnp.zeros_like(ad)
            for cp in loads:
                cp.wait()

        dr = dr_ref[...]
        drb = dr.astype(BF16)
        x1b = x1b_ref[...]
        gpv = gu_ref[:, 0:sz].astype(F32)
        upv = gu_ref[:, sz:2 * sz].astype(F32)
        dh = _dot_nt(drb, wd_v[...])
        sg = _sigmoid(gpv)
        silu = gpv * sg
        ad[...] += _dot_tn((silu * upv).astype(BF16), drb)
        dgp = (dh * upv * (sg * (1.0 + gpv * (1.0 - sg)))).astype(BF16)
        dup = (dh * silu).astype(BF16)
        ag[...] += _dot_tn(dgp, x1b)
        au[...] += _dot_tn(dup, x1b)
        base = alpha * dr if first else dxp_ref[...]
        dx_ref[...] = base + _dot(dgp, wg_v[...]) + _dot(dup, wu_v[...])

        @pl.when(pl.program_id(0) == n - 1)
        def _():
            rows = pl.ds(off, sz)
            stores = _start_copies(copy_sems, [(ag, dwg_hbm.at[rows]), (au, dwu_hbm.at[rows]), (ad, dwd_hbm.at[rows])])
            for acc, stage in ((ag, wg_v), (au, wu_v), (ad, wd_v)):
                stage[...] = acc[...].astype(BF16)
            stores += _start_copies(copy_sems, [(wg_v, dwg16_hbm.at[rows]), (wu_v, dwu16_hbm.at[rows]), (wd_v, dwd16_hbm.at[rows])], first=3)
            for cp in stores:
                cp.wait()


    row = lambda w: pl.BlockSpec((tm, w), lambda i: (i, 0))
    ins = [dr2] + ([] if first else [prev[0]]) + [x1b, gate_up, wg, wu, wd] + ([] if first else list(prev[1:]))
    in_specs = [row(d)] + ([] if first else [row(d)]) + [row(d), row(2 * sz), ANY, ANY, ANY] + ([] if first else [ANY] * 6)
    return pl.pallas_call(
        body, name=f"bwd_mlp_{j}", grid=(n,),
        in_specs=in_specs,
        out_specs=[row(d)] + [ANY] * 6,
        out_shape=[jax.ShapeDtypeStruct((t, d), F32)] + [jax.ShapeDtypeStruct(wg.shape, F32)] * 3 + [jax.ShapeDtypeStruct(wg.shape, BF16)] * 3,
        scratch_shapes=[pltpu.VMEM((sz, d), F32)] * 3 + [pltpu.VMEM((sz, d), BF16)] * 3 + [pltpu.SemaphoreType.DMA((6,))],
        input_output_aliases={} if first else {7 + a: 1 + a for a in range(6)},
        compiler_params=_cparams(),
    )(*ins)


SMALL_VD = CHUNK
SMALL_CW = CHUNK + 8
SMALL_VQ = CHUNK + 8
SMALL_LOSS = CHUNK + 16
SMALL_BS = CHUNK + 24


def _small_rows(kwp):
    return -(-(SMALL_CW + max(kwp, 24 + SUBLANES)) // 16) * 16


def _bwd_mix(dx1, saved, wi, wo, wstt, cwf, tabs, vq, vd, mlp_small, token, alpha, tm):
    saved_f32, saved_bf16, hf_s = saved
    t, d = dx1.shape
    q = wi.shape[2]
    nc, n_pairs = CONV_BLOCK // CHUNK, q // LANES
    n = t // tm
    nb = tm // CONV_BLOCK
    assert tm % CONV_BLOCK == 0

    def body(dx1_ref, f32_ref, bf16_ref, hf_ref,
             wi_hbm, wo_hbm, wstt_ref, cwf_ref, fwd_ref, fwd_halo_ref, shift_ref, taps_ref, inv_ref, inv_taps_ref, vq_ref, vd_ref,
             loss_ref, dg2_ref, db2_ref, token_ref,
             gx_ref, dwi_hbm, dwo_hbm, small_hbm, dwi16_hbm, dwo16_hbm,
             wi_v, wo_v, awi, awo, dyb_ref, later_ref, dbm_ref, gf_ref, dgf_ref, small_ref, copy_sems):
        xh_ref, zu_ref, mg_ref, vhat_ref, gv_ref, yhat_ref, rs_ref, xb_ref, pag_ref, y_ref, vnb_ref = _saved_views(f32_ref, bf16_ref, d, q)
        i = pl.program_id(0)

        @pl.when(i == 0)
        def _():
            loads = _start_copies(copy_sems, [(wi_hbm, wi_v), (wo_hbm, wo_v)])
            for r in (awi, awo, small_ref, dbm_ref, dgf_ref, dyb_ref, later_ref):
                r[...] = jnp.zeros_like(r)
            gf_ref[...] = _dot(taps_ref[...], _split(cwf_ref[...], True))
            for cp in loads:
                cp.wait()

        dr1b_parts, dproj_parts = [None] * nb, [None] * nb

        def sub_tile(b):
            rows = slice(b * CONV_BLOCK, (b + 1) * CONV_BLOCK)
            dx1v = dx1_ref[rows, :]
            xh = xh_ref[rows, :]
            rsv = rs_ref[rows, :]
            small_ref[SMALL_VD + VD_LN1_G:SMALL_VD + VD_LN1_G + 1, :] += _colsum(dx1v * xh)
            small_ref[SMALL_VD + VD_LN1_B:SMALL_VD + VD_LN1_B + 1, :] += _colsum(dx1v)
            dr1 = _ln_bwd(dx1v * vd_ref[VD_LN1_G:VD_LN1_G + 1, :], xh, rsv[:, RS_LN1:RS_LN1 + 1])
            dr1b = dr1.astype(BF16)
            yield
            dy = _dot_nt(dr1b, wo_v[...])
            yield
            vhat = vhat_ref[rows, :]
            sgu_g = vq_ref[VQ_SGU_G:VQ_SGU_G + 1, :]
            doa = dy[:, 0:q]
            dm = doa * zu_ref[rows, :]
            dpu = (doa * mg_ref[rows, :]).astype(BF16)
            acc = dm[0:CHUNK]
            for c in range(1, nc):
                acc = acc + dm[c * CHUNK:(c + 1) * CHUNK]
            dbm_ref[...] += acc
            pa = pag_ref[rows, 0:q].astype(F32)
            sg = _sigmoid(pag_ref[rows, q:2 * q].astype(F32))
            yhat = yhat_ref[rows, :]
            cln_g = vq_ref[VQ_CLN_G:VQ_CLN_G + 1, :]
            yn = yhat * cln_g + vq_ref[VQ_CLN_B:VQ_CLN_B + 1, :]
            sy = _sigmoid(yn)
            dyn = dy[:, q:2 * q] * (sy * (1.0 + yn * (1.0 - sy)))
            small_ref[SMALL_VQ + VQ_CLN_G:SMALL_VQ + VQ_CLN_G + 1, q:2 * q] += _colsum(dyn * yhat)
            small_ref[SMALL_VQ + VQ_CLN_B:SMALL_VQ + VQ_CLN_B + 1, q:2 * q] += _colsum(dyn)
            dyc = _ln_bwd(dyn * cln_g, yhat, rsv[:, RS_CONV:RS_CONV + 1])
            small_ref[SMALL_VQ + VQ_CONV_B:SMALL_VQ + VQ_CONV_B + 1, q:2 * q] += _colsum(dyc)
            dyb_ref[b, 0:CONV_BLOCK, :] = dyc
            yield
            wgrads = _mix_wgrad(dm, vnb_ref[rows, :], nc, n_pairs)
            dvn = _mix(wstt_ref, dm, nc, n_pairs)
            own = _dot(fwd_ref[...], _split(dyb_ref[b]))
            with_later = own + _dot(fwd_halo_ref[...], _split(later_ref[...]))
            later_ref[...] = dyb_ref[b, 0:HALO, :]
            yield
            for p, g in enumerate(wgrads):
                for half in range(2):
                    small_ref[0:CHUNK, (2 * p + half) * CHUNK:(2 * p + half + 1) * CHUNK] += g[half * CHUNK:(half + 1) * CHUNK]
            small_ref[SMALL_VQ + VQ_SGU_G:SMALL_VQ + VQ_SGU_G + 1, q:2 * q] += _colsum(dvn * vhat)
            small_ref[SMALL_VQ + VQ_SGU_B:SMALL_VQ + VQ_SGU_B + 1, q:2 * q] += _colsum(dvn)
            dpv = (_ln_bwd(dvn * sgu_g, vhat, rsv[:, RS_SGU:RS_SGU + 1]) * gv_ref[rows, :]).astype(BF16)
            dgf_ref[...] += _cmul(_cmul(own, shift_ref[...]), hf_ref[b * 2 * DFT_F:(b + 1) * 2 * DFT_F, :], conj_b=True)
            product = _split(_cmul(with_later, gf_ref[...], conj_b=True))
            yield
            dh = _dot(inv_ref[...], product)
            yield
            da = (dh * sg).astype(BF16)
            dg = (dh * pa * (sg * (1.0 - sg))).astype(BF16)
            yield
            gx = alpha * dr1
            for dpj, wj in zip((dpu, dpv, da, dg), range(4)):
                gx = gx + _dot_nt(dpj, wi_v[wj])
            gx_ref[rows, :] = gx
            dr1b_parts[b], dproj_parts[b] = dr1b, (dpu, dpv, da, dg)

        _interleave([sub_tile(b) for b in reversed(range(nb))])

        awo[...] += _dot_tn(y_ref[...], jnp.concatenate(dr1b_parts, axis=0))
        xb = xb_ref[...]
        for j in range(4):
            awi[j] += _dot_tn(xb, jnp.concatenate([part[j] for part in dproj_parts], axis=0))

        @pl.when(i == n - 1)
        def _():
            stores = _start_copies(copy_sems, [(awi, dwi_hbm), (awo, dwo_hbm)])
            wi_v[...] = awi[...].astype(BF16)
            wo_v[...] = awo[...].astype(BF16)
            stores += _start_copies(copy_sems, [(wi_v, dwi16_hbm), (wo_v, dwo16_hbm)], first=3)
            lane = lax.broadcasted_iota(jnp.int32, (CHUNK, LANES), 1)
            low = lane < HEAD_DIM
            dbs = jnp.zeros((CHUNK, LANES), F32)
            for p in range(n_pairs):
                grp = dbm_ref[:, p * LANES:(p + 1) * LANES]
                dbs = jnp.where(lane == 2 * p, jnp.sum(jnp.where(low, grp, 0.0), axis=1, keepdims=True), dbs)
                dbs = jnp.where(lane == 2 * p + 1, jnp.sum(jnp.where(low, 0.0, grp), axis=1, keepdims=True), dbs)
            tril = lax.broadcasted_iota(jnp.int32, (CHUNK, CHUNK), 0) >= lax.broadcasted_iota(jnp.int32, (CHUNK, CHUNK), 1)
            for h in range(2 * n_pairs):
                block = small_ref[0:CHUNK, h * CHUNK:(h + 1) * CHUNK]
                small_ref[0:CHUNK, h * CHUNK:(h + 1) * CHUNK] = jnp.where(tril, block, 0.0)
            small_ref[SMALL_VD + VD_LN2_G:SMALL_VD + VD_LN2_G + 1, :] = dg2_ref[...]
            small_ref[SMALL_VD + VD_LN2_B:SMALL_VD + VD_LN2_B + 1, :] = db2_ref[...]
            small_ref[SMALL_CW:SMALL_CW + kwp, 0:q] = _dot(inv_taps_ref[...], _split(dgf_ref[...], True))
            small_ref[SMALL_LOSS:SMALL_LOSS + SUBLANES, q:q + LANES] = loss_ref[...]
            small_ref[SMALL_BS:SMALL_BS + SUBLANES, q:q + LANES] = jnp.transpose(dbs)[0:SUBLANES]
            stores += _start_copies(copy_sems, [(small_ref, small_hbm)], first=2)
            for cp in stores:
                cp.wait()

    rev = lambda w: pl.BlockSpec((tm, w), lambda i: (n - 1 - i, 0))
    kwp = cwf.shape[0]
    small = jax.ShapeDtypeStruct((_small_rows(kwp), 2 * q), F32)
    small_ins = [wstt, cwf, tabs["fwd"], tabs["fwd_halo"], tabs["shift"], tabs["taps"], tabs["inv_in"], tabs["inv_taps"], vq, vd,
                 *mlp_small]
    return pl.pallas_call(
        body, name="bwd_mix", grid=(n,),
        in_specs=[rev(d), rev(saved_f32.shape[1]), rev(saved_bf16.shape[1]),
                  pl.BlockSpec((nb * 2 * DFT_F, q), lambda i: (n - 1 - i, 0)), ANY, ANY] + [_full(a.shape) for a in small_ins] + [ANY],
        out_specs=[rev(d)] + [ANY] * 5,
        out_shape=[jax.ShapeDtypeStruct((t, d), F32), jax.ShapeDtypeStruct(wi.shape, F32), jax.ShapeDtypeStruct(wo.shape, F32), small,
                   jax.ShapeDtypeStruct(wi.shape, BF16), jax.ShapeDtypeStruct(wo.shape, BF16)],
        scratch_shapes=[pltpu.VMEM(wi.shape, BF16), pltpu.VMEM(wo.shape, BF16), pltpu.VMEM(wi.shape, F32), pltpu.VMEM(wo.shape, F32),
                        pltpu.VMEM((nb, DFT_N, q), F32), pltpu.VMEM((HALO, q), F32),
                        pltpu.VMEM((CHUNK, q), F32), pltpu.VMEM((2 * DFT_F, q), F32), pltpu.VMEM((2 * DFT_F, q), F32),
                        pltpu.VMEM(small.shape, F32), pltpu.SemaphoreType.DMA((5,))],
        compiler_params=_cparams(),
    )(dx1, saved_f32, saved_bf16, hf_s, wi, wo, *small_ins, token)


def _coords():
    return tuple(lax.axis_index(a) for a in MESH_AXES)


def _other_chips(x, y):
    return [(1 - x, y), (x, 1 - y), (1 - x, 1 - y)]


def _remote(src, dst, send_sem, recv_sem, to):
    return pltpu.make_async_remote_copy(src_ref=src, dst_ref=dst, send_sem=send_sem, recv_sem=recv_sem,
                                        device_id=to, device_id_type=MESH_ID)


class _Gather:
    def __init__(self, bufs, send_sems, recv_sems, own=None):
        self.bufs, self.send_sems, self.recv_sems, self.own = bufs, send_sems, recv_sems, own
        self.x, self.y, self.c = _coords()

    def _copies(self, stage):
        x, y, c = self.x, self.y, self.c
        for a, buf in enumerate(self.bufs):
            hr = buf.shape[1] // 2
            for j, chip in enumerate(_other_chips(x, y)):
                if stage == "ici_out":
                    ref, k, to = buf.at[2 * x + y, pl.ds(c * hr, hr)], j, (*chip, c)
                    if self.own:
                        yield _remote(self.own[a].at[pl.ds(c * hr, hr)], ref, self.send_sems.at[a, k], self.recv_sems.at[a, k], to)
                        continue
                elif stage == "ici_in":
                    ref, k, to = buf.at[2 * chip[0] + chip[1], pl.ds(c * hr, hr)], j, (*chip, c)
                elif stage == "d2d_out":
                    ref, k, to = buf.at[2 * chip[0] + chip[1], pl.ds(c * hr, hr)], 3 + j, (x, y, 1 - c)
                else:
                    ref, k, to = buf.at[2 * chip[0] + chip[1], pl.ds((1 - c) * hr, hr)], 3 + j, (x, y, 1 - c)
                yield _remote(ref, ref, self.send_sems.at[a, k], self.recv_sems.at[a, k], to)

    def start(self):
        for cp in self._copies("ici_out"):
            cp.start()

    def forward(self):
        for landed, onward in zip(self._copies("ici_in"), self._copies("d2d_out")):
            landed.wait_recv()
            onward.start()

    def finish(self):
        for cp in self._copies("d2d_in"):
            cp.wait_recv()
        for stage in ("ici_out", "d2d_out"):
            for cp in self._copies(stage):
                cp.wait_send()


def _gather_sems(n):
    return [pltpu.SemaphoreType.DMA((n, 6)), pltpu.SemaphoreType.DMA((n, 6))]


def _prep_gather(gathered, local, conv_w, kwp):
    kw, _, cshard = conv_w.shape
    shards = list(gathered) + list(local)
    n, n_g = len(shards), len(gathered)

    def body(*refs):
        src, cw_ref, out = refs[:n], refs[n], refs[n + 1:2 * n + 2]
        wide, narrow, taps = refs[2 * n + 2:3 * n + 2], refs[3 * n + 2:4 * n + 2], refs[4 * n + 2]
        load_sems, store_sems, send_sems, recv_sems = refs[4 * n + 3:]
        x, y, _ = _coords()
        loads = [pltpu.make_async_copy(src[a], wide[a], load_sems.at[a]) for a in range(n)]
        stores = [pltpu.make_async_copy(narrow[a], out[a].at[2 * x + y], store_sems.at[a]) for a in range(n)]
        stores.append(pltpu.make_async_copy(taps, out[n].at[2 * x + y], store_sems.at[n]))
        for cp in loads:
            cp.start()

        def cast(a):
            loads[a].wait()
            narrow[a][...] = wide[a][...].astype(BF16)
            stores[a].start()

        for a in range(n_g):
            cast(a)
        for tap in range(kw):
            taps[tap:tap + 1, :] = cw_ref[tap]
        taps[kw:kwp, :] = jnp.zeros((kwp - kw, cshard), F32)
        stores[n].start()
        g = _Gather(list(out[:n_g]) + [out[n]], send_sems, recv_sems, own=list(narrow[:n_g]) + [taps])
        g.start()
        for a in range(n_g, n):
            cast(a)
        g.forward()
        g.finish()
        for cp in stores:
            cp.wait()

    out_shape = [jax.ShapeDtypeStruct((4,) + a.shape, BF16) for a in shards] + [jax.ShapeDtypeStruct((4, kwp, cshard), F32)]
    scratch = ([pltpu.VMEM(a.shape, F32) for a in shards] + [pltpu.VMEM(a.shape, BF16) for a in shards] + [pltpu.VMEM((kwp, cshard), F32)]
               + [pltpu.SemaphoreType.DMA((n,)), pltpu.SemaphoreType.DMA((n + 1,))] + _gather_sems(n_g + 1))
    res = pl.pallas_call(
        body, name="prep_gather", in_specs=[ANY] * n + [pl.BlockSpec(memory_space=pltpu.VMEM)], out_specs=[ANY] * (n + 1),
        out_shape=out_shape, scratch_shapes=scratch, compiler_params=pltpu.CompilerParams(vmem_limit_bytes=VMEM_LIMIT),
    )(*shards, conv_w)
    return list(res[:n_g]), list(res[n_g:n]), res[n]


def _pair_reduce(name, partials, payloads, c_arr, out_dtypes):
    n = len(partials)
    counts = [g.shape[0] for g in partials]
    first = [sum(counts[:a]) for a in range(n)]
    steps = sum(counts)

    def body(c_ref, *refs):
        own, travelling, out, land = (refs[k * n:(k + 1) * n] for k in range(4))
        send_sems, recv_sems = refs[4 * n:]
        i = pl.program_id(0)
        x, y, c = _coords()

        def copy(a, q):
            return _remote(travelling[a].at[q, 1 - c], land[a].at[q], send_sems.at[first[a] + q], recv_sems.at[first[a] + q],
                           (x, y, 1 - c))

        blocks = [(a, q) for a in range(n) for q in range(counts[a])]

        @pl.when(i == 0)
        def _():
            for a, q in blocks:
                copy(a, q).start()

        for a in range(n):
            @pl.when((i >= first[a]) & (i < first[a] + counts[a]))
            def _(a=a):
                q = i - first[a]
                copy(a, q).wait_recv()
                out[a][...] = (own[a][...] + land[a][q].astype(F32)).astype(out_dtypes[a])

        @pl.when(i == steps - 1)
        def _():
            for a, q in blocks:
                copy(a, q).wait_send()

    at = lambda a, i: jnp.clip(i - first[a], 0, counts[a] - 1)
    in_specs = [pl.BlockSpec((None, None) + g.shape[2:], lambda i, cr, a=a: (at(a, i), cr[0], 0, 0)) for a, g in enumerate(partials)]
    out_specs = [pl.BlockSpec((None,) + g.shape[2:], lambda i, cr, a=a: (at(a, i), 0, 0)) for a, g in enumerate(partials)]
    grid_spec = pltpu.PrefetchScalarGridSpec(
        num_scalar_prefetch=1, grid=(steps,), in_specs=in_specs + [ANY] * n, out_specs=out_specs,
        scratch_shapes=[pltpu.VMEM((g.shape[0],) + g.shape[2:], p.dtype) for g, p in zip(partials, payloads)]
        + [pltpu.SemaphoreType.DMA((steps,)), pltpu.SemaphoreType.DMA((steps,))])
    out_shape = [jax.ShapeDtypeStruct((g.shape[0],) + g.shape[2:], dt) for g, dt in zip(partials, out_dtypes)]
    return list(pl.pallas_call(body, name=name, grid_spec=grid_spec, out_shape=out_shape, compiler_params=_cparams())(
        c_arr, *partials, *payloads))


class _Exchange:
    def __init__(self, src, dst, send_sems, recv_sems):
        self.src, self.dst, self.send_sems, self.recv_sems = src, dst, send_sems, recv_sems
        self.x, self.y, self.c = _coords()

    def _copies(self, incoming):
        x, y, c = self.x, self.y, self.c
        for a, (s, d) in enumerate(zip(self.src, self.dst)):
            for j, chip in enumerate(_other_chips(x, y)):
                slot = 2 * chip[0] + chip[1]
                if incoming:
                    out, into = d.at[slot], d.at[slot]
                else:
                    out, into = (s.at[slot] if len(s.shape) == 3 else s), d.at[2 * x + y]
                yield _remote(out, into, self.send_sems.at[a, j], self.recv_sems.at[a, j], (*chip, c))

    def start(self):
        for cp in self._copies(False):
            cp.start()

    def finish(self):
        for cp in self._copies(True):
            cp.wait_recv()
        for cp in self._copies(False):
            cp.wait_send()


def _exchange_shapes(arrs):
    return [jax.ShapeDtypeStruct((4,) + s.shape[-2:], s.dtype) for s in arrs]


class _FlatSems:
    def __init__(self, ref):
        self.ref = ref

    @property
    def at(self):
        return self

    def __getitem__(self, idx):
        return self.ref.at[3 * idx[0] + idx[1]]


HBM = pl.BlockSpec(memory_space=pltpu.HBM)
SEM = pl.BlockSpec(memory_space=pltpu.SEMAPHORE)
DATAFLOW = pltpu.SideEffectType.DATAFLOW_SIDE_EFFECTING


def _exchange_start(name, arrs):
    n = len(arrs)
    lands = _exchange_shapes(arrs)

    def body(*refs):
        src, land = refs[:n], refs[n:2 * n]
        send_sems, recv_sems = refs[2 * n:2 * n + 2]
        token = refs[-1]
        _Exchange(src, land, _FlatSems(send_sems), _FlatSems(recv_sems)).start()
        token[...] = jnp.zeros_like(token)

    hbm = lambda a: pltpu.with_memory_space_constraint(a, pltpu.HBM)
    outs = pl.pallas_call(
        body, name=name,
        out_shape=(pltpu.SemaphoreType.DMA((3 * n,)), pltpu.SemaphoreType.DMA((3 * n,)),
                   *[pltpu.HBM(a.shape, a.dtype) for a in arrs], *[pltpu.HBM(s.shape, s.dtype) for s in lands],
                   jax.ShapeDtypeStruct((SUBLANES, LANES), F32)),
        in_specs=[HBM] * (2 * n), out_specs=(SEM, SEM, *[HBM] * (2 * n), pl.BlockSpec(memory_space=pltpu.VMEM)),
        input_output_aliases={a: 2 + a for a in range(2 * n)},
        compiler_params=pltpu.CompilerParams(has_side_effects=DATAFLOW),
    )(*[hbm(a) for a in arrs], *[hbm(lax.empty(s.shape, s.dtype)) for s in lands])
    return outs[:-1], outs[-1]


def _exchange_wait(name, started, after):
    send_sems, recv_sems, *bufs = started
    n = len(bufs) // 2

    def body(*refs):
        src, land = refs[:n], refs[n:2 * n]
        send_sems, recv_sems = refs[2 * n:2 * n + 2]
        _Exchange(src, land, _FlatSems(send_sems), _FlatSems(recv_sems)).finish()

    outs = pl.pallas_call(
        body, name=name,
        out_shape=tuple(pltpu.HBM(b.shape, b.dtype) for b in bufs),
        in_specs=[HBM] * (2 * n) + [SEM, SEM] + [ANY] * len(after), out_specs=tuple([HBM] * (2 * n)),
        input_output_aliases={a: a for a in range(2 * n)},
        compiler_params=pltpu.CompilerParams(has_side_effects=DATAFLOW),
    )(*bufs, send_sems, recv_sems, *after)
    return list(outs[:n]), list(outs[n:])


def _chip_reduce(name, parts, owns, after):
    n = len(parts)

    def body(*refs):
        part, own = refs[:n], refs[n:2 * n]
        mine, other = refs[2 * n + len(after):3 * n + len(after)], refs[3 * n + len(after):4 * n + len(after)]
        scratch = refs[4 * n + len(after):]
        landed, own_part, total = scratch[:n], scratch[n:2 * n], scratch[2 * n:3 * n]
        load_sems, store_sems, send_sems, recv_sems = scratch[3 * n:]
        x, y, c = _coords()
        me = 2 * x + y
        loads, stores, sends = [], [], []
        for a in range(n):
            loads.append((pltpu.make_async_copy(part[a], landed[a], load_sems.at[a, 0]),
                          pltpu.make_async_copy(own[a].at[me] if len(own[a].shape) == 3 else own[a], own_part[a], load_sems.at[a, 1])))
            stores.append(pltpu.make_async_copy(total[a], mine[a], store_sems.at[a]))
            sends.append(_remote(total[a], other[a], send_sems.at[a], recv_sems.at[a], (x, y, 1 - c)))
        for both in loads:
            for cp in both:
                cp.start()
        for a in range(n):
            for cp in loads[a]:
                cp.wait()
            for chip in range(4):
                @pl.when(me == chip)
                def _(a=a, chip=chip):
                    term = lambda j: (own_part[a] if j == chip else landed[a].at[j])[...].astype(F32)
                    total[a][...] = ((term(0) + term(1)) + term(2)) + term(3)
            stores[a].start()
            sends[a].start()
        for a in range(n):
            stores[a].wait()
            sends[a].wait()

    halves = [jax.ShapeDtypeStruct(p.shape[1:], F32) for p in parts]
    scratch = ([pltpu.VMEM(p.shape, p.dtype) for p in parts] + [pltpu.VMEM(p.shape[1:], o.dtype) for p, o in zip(parts, owns)]
               + [pltpu.VMEM(p.shape[1:], F32) for p in parts]
               + [pltpu.SemaphoreType.DMA((n, 2))] + [pltpu.SemaphoreType.DMA((n,))] * 3)
    res = pl.pallas_call(
        body, name=name, in_specs=[ANY] * (2 * n + len(after)), out_specs=[ANY] * (2 * n), out_shape=halves + halves,
        scratch_shapes=scratch, compiler_params=pltpu.CompilerParams(vmem_limit_bytes=VMEM_LIMIT),
    )(*parts, *owns, *after)
    return list(res[:n]), list(res[n:])


def _row_block(rows, cols, limit=1 << 20):
    best = 8
    for tr in range(8, rows + 1, 8):
        if rows % tr == 0 and tr * cols * 4 <= limit:
            best = tr
    return best


def _adamw(name, w, g_mine, g_other, m, v, c_arr):
    r, c = w.shape
    hr, cg = g_mine.shape
    tr = hr if r % hr == 0 and hr * cg * 4 <= (3 << 19) else math.gcd(_row_block(hr, cg), r)
    per_half = hr // tr
    bc1 = 1.0 - ADAM_B1 ** ADAM_STEP
    bc2 = 1.0 - ADAM_B2 ** ADAM_STEP

    def body(c_ref, w_ref, gm_ref, go_ref, m_ref, v_ref, go, do, mo, vo):
        gv = jnp.where(pl.program_id(0) // per_half == c_ref[0], gm_ref[:, 0:c], go_ref[:, 0:c])
        mn = ADAM_B1 * m_ref[...] + (1.0 - ADAM_B1) * gv
        vn = ADAM_B2 * v_ref[...] + (1.0 - ADAM_B2) * (gv * gv)
        go[...] = gv
        mo[...] = mn
        vo[...] = vn
        do[...] = -ADAM_LR * ((mn / bc1) / (jnp.sqrt(vn / bc2) + ADAM_EPS) + ADAM_WD * w_ref[...])

    blk = pl.BlockSpec((tr, c), lambda i, cr: (i, 0))
    gblk = pl.BlockSpec((tr, cg), lambda i, cr: (i % per_half, 0))
    grid_spec = pltpu.PrefetchScalarGridSpec(num_scalar_prefetch=1, grid=(r // tr,), in_specs=[blk, gblk, gblk, blk, blk],
                                             out_specs=[blk] * 4)
    return pl.pallas_call(body, name=f"adamw_{name}", grid_spec=grid_spec, out_shape=[jax.ShapeDtypeStruct((r, c), F32)] * 4,
                          compiler_params=_cparams())(c_arr, w, g_mine, g_other, m, v)


SMALL_Q = ("sgu_ln_g", "sgu_ln_b", "conv_b", "conv_ln_g", "conv_ln_b")
SMALL_D = ("ln1_g", "ln1_b", "ln2_g", "ln2_b")


def _adamw_small(g_mine, g_other, c_arr, me_arr, params):
    names = list(SMALL_Q) + list(SMALL_D) + ["w_s", "b_s", "conv_w"]
    hr, width = g_mine.shape
    q = width // 2
    heads = params["w_s"][0].shape[1]
    kw, _, cshard = params["conv_w"][0].shape
    bc1 = 1.0 - ADAM_B1 ** ADAM_STEP
    bc2 = 1.0 - ADAM_B2 ** ADAM_STEP

    def update(w, g, m, v):
        mn = ADAM_B1 * m + (1.0 - ADAM_B1) * g
        vn = ADAM_B2 * v + (1.0 - ADAM_B2) * (g * g)
        return g, -ADAM_LR * ((mn / bc1) / (jnp.sqrt(vn / bc2) + ADAM_EPS) + ADAM_WD * w), mn, vn

    def body(c_ref, me_ref, gm_ref, go_ref, *refs):
        ins = {nm: refs[3 * k:3 * k + 3] for k, nm in enumerate(names)}
        outs = {nm: refs[3 * len(names) + 4 * k:3 * len(names) + 4 * k + 4] for k, nm in enumerate(names)}
        loss_ref, cw_ref = refs[-2:]
        first, second = gm_ref[...], go_ref[...]
        low = c_ref[0] == 0
        g_all = jnp.concatenate([jnp.where(low, first, second), jnp.where(low, second, first)], axis=0)

        def apply(nm, g, at):
            w, m, v = (r[at] for r in ins[nm])
            for o, val in zip(outs[nm], update(w, g, m, v)):
                o[at] = val

        for row, nm in enumerate(SMALL_Q):
            apply(nm, g_all[SMALL_VQ + row:SMALL_VQ + row + 1, q:2 * q], ...)
        for row, nm in enumerate(SMALL_D):
            apply(nm, g_all[SMALL_VD + row:SMALL_VD + row + 1, :], ...)
        for h in range(heads):
            apply("w_s", g_all[0:CHUNK, h * CHUNK:(h + 1) * CHUNK], (0, h))
        apply("b_s", g_all[SMALL_BS:SMALL_BS + heads, q:q + LANES], 0)
        cw_ref[...] = jnp.zeros_like(cw_ref)
        for chip in range(4):
            @pl.when(me_ref[0] == chip)
            def _():
                cw_ref[...] = g_all[SMALL_CW:SMALL_CW + cw_ref.shape[0], chip * cshard:(chip + 1) * cshard]
        for tap in range(kw):
            apply("conv_w", cw_ref[tap:tap + 1, :], tap)
        loss_ref[...] = g_all[SMALL_LOSS:SMALL_LOSS + 1, q:q + 1]

    arrays = [a for nm in names for a in params[nm]]
    out_shape = [jax.ShapeDtypeStruct(params[nm][0].shape, F32) for nm in names for _ in range(4)] + [jax.ShapeDtypeStruct((1, 1), F32)]
    whole = lambda shape: pl.BlockSpec(shape, lambda i, c, me: (0,) * len(shape))
    grid_spec = pltpu.PrefetchScalarGridSpec(
        num_scalar_prefetch=2, grid=(1,),
        in_specs=[whole(g_mine.shape), whole(g_other.shape)] + [whole(a.shape) for a in arrays],
        out_specs=[whole(s.shape) for s in out_shape],
        scratch_shapes=[pltpu.VMEM((-(-kw // SUBLANES) * SUBLANES, cshard), F32)])
    res = pl.pallas_call(body, name="adamw_small", grid_spec=grid_spec, out_shape=out_shape, compiler_params=_cparams())(
        c_arr, me_arr, g_mine, g_other, *arrays)
    return {nm: list(res[4 * k:4 * k + 4]) for k, nm in enumerate(names)}, res[-1]


def _pad_rows(a, rows):
    return jnp.pad(a, ((0, rows - a.shape[0]), (0, 0)))


def kernel(x, w_in, sgu_ln_g, sgu_ln_b, w_s, b_s, conv_w, conv_b, conv_ln_g, conv_ln_b, w_out, ln1_g, ln1_b, w_gate, w_up, w_down, ln2_g, ln2_b, loss_target, m_w_in, m_sgu_ln_g, m_sgu_ln_b, m_w_s, m_b_s, m_conv_w, m_conv_b, m_conv_ln_g, m_conv_ln_b, m_w_out, m_ln1_g, m_ln1_b, m_w_gate, m_w_up, m_w_down, m_ln2_g, m_ln2_b, v_w_in, v_sgu_ln_g, v_sgu_ln_b, v_w_s, v_b_s, v_conv_w, v_conv_b, v_conv_ln_g, v_conv_ln_b, v_w_out, v_ln1_g, v_ln1_b, v_w_gate, v_w_up, v_w_down, v_ln2_g, v_ln2_b):
    depth, d, q = w_in.shape
    assert depth == 1 and x.shape[0] == 1
    t = x.shape[1]
    heads = w_s.shape[1]
    kw, cshard = conv_w.shape[1], conv_w.shape[2]
    fs = w_gate.shape[2]
    slabs = _hidden_slabs(4 * fs)
    n_pairs = q // LANES
    assert heads * HEAD_DIM == q and q % LANES == 0 and w_s.shape[2] == CHUNK and 4 * cshard == q and kw - 1 <= HALO
    alpha = (2.0 * depth) ** 0.25
    tm = min(512, t)
    assert t % tm == 0 and tm % CHUNK == 0
    x2, tgt = x[0], loss_target[0]
    mx, my, mc = _coords()
    me = 2 * mx + my
    c_arr = jnp.reshape(mc, (1,)).astype(jnp.int32)

    kwp = -(-kw // 16) * 16
    me_arr = jnp.reshape(me, (1,)).astype(jnp.int32)
    (wi, wo), (wg, wu, wd), cw4 = _prep_gather([w_in[0], w_out[0]], [w_gate[0].T, w_up[0].T, w_down[0]],
                                              jnp.transpose(conv_w, (1, 0, 2)), kwp)
    wo = wo.reshape(d, d)
    cw = jnp.transpose(cw4, (1, 0, 2)).reshape(kwp, q)
    cwf = _pad_rows(cw[:kw][::-1], kwp)
    tabs = {name: jnp.asarray(tab) for name, tab in _dft_tables(kw, kwp, q).items()}

    wm = jnp.where(jnp.tril(jnp.ones((CHUNK, CHUNK), bool)), w_s[0], 0.0)
    wst = wm.reshape(n_pairs, 2 * CHUNK, CHUNK).astype(BF16)
    wstt = jnp.transpose(wm, (0, 2, 1)).reshape(n_pairs, 2 * CHUNK, CHUNK).astype(BF16)
    bmat = jnp.repeat(b_s[0].T, HEAD_DIM, axis=1)
    vq = _pad_rows(jnp.concatenate([sgu_ln_g, sgu_ln_b, conv_b, conv_ln_g, conv_ln_b], axis=0), 8)
    vd = _pad_rows(jnp.concatenate([ln1_g, ln1_b, ln2_g, ln2_b], axis=0), 8)

    *saved, wg, wu, wd = _fwd_mix(x2, wi, wo, wst, bmat, cwf, tabs, vq, vd, [wg, wu, wd], alpha, tm)
    wg, wu, wd = (w.reshape(4 * fs, d) for w in (wg, wu, wd))
    *acts, x1b, dr2, loss_part, dg2, db2 = _fwd_mlp(saved[0], tgt, wg, wu, wd, vd, alpha, slabs, tm)
    mlp_grads = None
    for j, slab in enumerate(slabs):
        mlp_grads = _bwd_mlp_slab(j, slab, dr2, mlp_grads, x1b, acts[j], wg, wu, wd, alpha, tm)
    dx1 = mlp_grads[0]
    mlp_halves = [b.reshape(4, 2, fs // 2, d) for b in mlp_grads[1:]]
    mlp_sums = _pair_reduce("pair_reduce_mlp", mlp_halves[:3], mlp_halves[3:], c_arr, [BF16] * 3)
    mlp_started, token = _exchange_start("exchange_mlp_start", mlp_sums)
    grad_x, dwi, dwo, small, dwi16, dwo16 = _bwd_mix(dx1, saved, wi, wo, wstt, cwf, tabs, vq, vd, (loss_part, dg2, db2), token, alpha, tm)
    mlp_sums, mlp_parts = _exchange_wait("exchange_mlp_wait", mlp_started, [dwo])

    by_halves = lambda b: b.reshape(4, 2, b.shape[1] // 2, b.shape[2])
    halves = [by_halves(dwi), by_halves(dwo.reshape(4, d // 4, d)), small.reshape(1, 2, small.shape[0] // 2, small.shape[1])]
    travelling = [by_halves(dwi16), by_halves(dwo16.reshape(4, d // 4, d)), halves[-1]]
    *sums, small_sum = _pair_reduce("pair_reduce_mix", halves, travelling, c_arr, [BF16, BF16, F32])
    sums.append(small_sum[0])
    mix_started, token = _exchange_start("exchange_mix_start", sums)

    out, raw = {}, {}

    def finish(first, names, parts, sums, after):
        mine, other = _chip_reduce(f"chip_reduce_{first}", parts, sums, [after])
        for a, nm in enumerate(names):
            w_, m_, v_ = weights[nm]
            if nm in ("w_gate", "w_up"):
                raw[nm] = _adamw(nm, w_[0].T, mine[a], other[a], m_[0].T, v_[0].T, c_arr)
                out[nm] = [o.T for o in raw[nm]]
            else:
                raw[nm] = out[nm] = _adamw(nm, w_[0], mine[a], other[a], m_[0], v_[0], c_arr)
        return mine[-1], other[-1]

    weights = {"w_in": (w_in, m_w_in, v_w_in), "w_out": (w_out, m_w_out, v_w_out), "w_gate": (w_gate, m_w_gate, v_w_gate),
               "w_up": (w_up, m_w_up, v_w_up), "w_down": (w_down, m_w_down, v_w_down)}
    finish(2, ["w_gate", "w_up", "w_down"], mlp_parts, mlp_sums, token)
    sums, parts = _exchange_wait("exchange_mix_wait", mix_started, [raw[nm][1] for nm in ("w_gate", "w_up", "w_down")])
    small_mine, small_other = finish(5, ["w_in", "w_out"], parts, sums, parts[0])

    small_params = {
        "sgu_ln_g": (sgu_ln_g, m_sgu_ln_g, v_sgu_ln_g), "sgu_ln_b": (sgu_ln_b, m_sgu_ln_b, v_sgu_ln_b),
        "conv_b": (conv_b, m_conv_b, v_conv_b), "conv_ln_g": (conv_ln_g, m_conv_ln_g, v_conv_ln_g),
        "conv_ln_b": (conv_ln_b, m_conv_ln_b, v_conv_ln_b), "ln1_g": (ln1_g, m_ln1_g, v_ln1_g), "ln1_b": (ln1_b, m_ln1_b, v_ln1_b),
        "ln2_g": (ln2_g, m_ln2_g, v_ln2_g), "ln2_b": (ln2_b, m_ln2_b, v_ln2_b), "w_s": (w_s, m_w_s, v_w_s),
        "b_s": (b_s, m_b_s, v_b_s), "conv_w": tuple(jnp.transpose(a, (1, 0, 2)) for a in (conv_w, m_conv_w, v_conv_w))}
    small_out, loss_block = _adamw_small(small_mine, small_other, c_arr, me_arr, small_params)
    small_out["conv_w"] = [jnp.transpose(o, (1, 0, 2)) for o in small_out["conv_w"]]
    loss = loss_block.reshape(())
    names = ["w_in", "sgu_ln_g", "sgu_ln_b", "w_s", "b_s", "conv_w", "conv_b", "conv_ln_g", "conv_ln_b", "w_out",
             "ln1_g", "ln1_b", "w_gate", "w_up", "w_down", "ln2_g", "ln2_b"]
    result = [loss, grad_x[None]]
    for kind in range(4):
        for nm in names:
            result.append(out[nm][kind][None] if nm in out else small_out[nm][kind])
    return tuple(result)
```

```python
import math

import jax
import numpy as np
import jax.numpy as jnp
from jax import lax
from jax.experimental import pallas as pl
from jax.experimental.pallas import tpu as pltpu

F32 = jnp.float32
BF16 = jnp.bfloat16

LN_EPS = 1e-5
HEAD_DIM = 64
CHUNK = 128
HALO = 32
LANES = 128
MXU_N = 256
ADAM_LR, ADAM_B1, ADAM_B2, ADAM_EPS, ADAM_WD, ADAM_STEP = 0.001, 0.9, 0.999, 1e-08, 0.01, 10
VMEM_LIMIT = 63 * 1024 * 1024
MESH_AXES = ("x", "y", "c")
MESH_ID = pl.DeviceIdType.MESH


def _dot(a, b):
    return jnp.dot(a, b, preferred_element_type=F32)


def _dot_nt(a, b):
    return lax.dot_general(a, b, (((1,), (1,)), ((), ())), preferred_element_type=F32)


def _dot_tn(a, b):
    return lax.dot_general(a, b, (((0,), (0,)), ((), ())), preferred_element_type=F32)


def _sigmoid(v):
    return 1.0 / (1.0 + jnp.exp(-v))


def _gelu(v):
    cdf = 0.5 * (1.0 + lax.erf(v * (1.0 / math.sqrt(2.0))))
    pdf = jnp.exp(-0.5 * v * v) * (1.0 / math.sqrt(2.0 * math.pi))
    return v * cdf, cdf + v * pdf


def _ln_stats(v):
    mu = jnp.mean(v, axis=-1, keepdims=True)
    d = v - mu
    rstd = lax.rsqrt(jnp.mean(d * d, axis=-1, keepdims=True) + LN_EPS)
    return d * rstd, rstd


def _ln_bwd(dxhat, xhat, rstd):
    m1 = jnp.mean(dxhat, axis=-1, keepdims=True)
    m2 = jnp.mean(dxhat * xhat, axis=-1, keepdims=True)
    return rstd * (dxhat - m1 - xhat * m2)


def _colsum(v):
    return jnp.sum(v, axis=0, keepdims=True)


def _pair_lanes(v, nc, p):
    return jnp.concatenate([v[c * CHUNK:(c + 1) * CHUNK, p * LANES:(p + 1) * LANES] for c in range(nc)], axis=1)


def _unpair(parts, nc):
    rows = [jnp.concatenate([part[:, c * LANES:(c + 1) * LANES] for part in parts], axis=1) for c in range(nc)]
    return jnp.concatenate(rows, axis=0)


def _low_head(nc):
    lane = lax.broadcasted_iota(jnp.int32, (CHUNK, nc * LANES), 1)
    return (lane & (LANES - 1)) < HEAD_DIM


def _mix(wst_ref, v, nc, n_pairs):
    vb = v.astype(BF16)
    low = _low_head(nc)
    parts = []
    for p in range(n_pairs):
        r = _dot(wst_ref[p], _pair_lanes(vb, nc, p))
        parts.append(jnp.where(low, r[:CHUNK], r[CHUNK:]))
    return _unpair(parts, nc)


def _mix_wgrad(dm, vn, nc, n_pairs):
    low = _low_head(nc)
    vb = vn.astype(BF16)
    out = []
    for p in range(n_pairs):
        a = _pair_lanes(dm, nc, p)
        lhs = jnp.concatenate([jnp.where(low, a, 0.0), jnp.where(low, 0.0, a)], axis=0).astype(BF16)
        out.append(_dot_nt(lhs, _pair_lanes(vb, nc, p)))
    return out


SUBLANES = 8


CONV_BLOCK = 256
DFT_N = CONV_BLOCK + HALO
DFT_F = -(-(DFT_N // 2 + 1) // SUBLANES) * SUBLANES


def _terms(m, exact):
    hi = m.astype(np.float32).astype(BF16)
    lo = (m.astype(np.float32) - hi.astype(np.float32)).astype(BF16)
    return np.concatenate([hi, hi, lo] if exact else [hi], axis=1)


def _split(v, exact=False):
    hi = v.astype(BF16)
    if not exact:
        return hi
    lo = (v - hi.astype(F32)).astype(BF16)
    return jnp.concatenate([hi, lo, hi], axis=0)


def _dft_tables(kw, kwp, q):
    nf = DFT_N // 2 + 1
    ang = 2.0 * np.pi * np.arange(nf)[:, None] * np.arange(DFT_N)[None, :] / DFT_N
    fwd = np.zeros((2 * DFT_F, DFT_N))
    fwd[:nf], fwd[DFT_F:DFT_F + nf] = np.cos(ang), -np.sin(ang)
    weight = np.full((nf, 1), 2.0 / DFT_N)
    weight[0] = weight[-1] = 1.0 / DFT_N
    inv = np.zeros((DFT_N, 2 * DFT_F))
    inv[:, :nf], inv[:, DFT_F:DFT_F + nf] = (np.cos(ang) * weight).T, (-np.sin(ang) * weight).T
    inv_taps = np.zeros((kwp, 2 * DFT_F))
    inv_taps[:kw] = inv[kw - 1::-1][:kw]
    shift = np.zeros((2 * DFT_F, q), np.float32)
    shift[:nf], shift[DFT_F:DFT_F + nf] = np.cos(ang[:, HALO:HALO + 1]), -np.sin(ang[:, HALO:HALO + 1])
    return {"fwd": _terms(fwd, False), "fwd_halo": _terms(fwd[:, CONV_BLOCK:], False), "shift": shift,
            "inv_out": _terms(inv[HALO:HALO + CONV_BLOCK], False), "inv_in": _terms(inv[:CONV_BLOCK], False),
            "taps": _terms(fwd[:, :kwp], True), "inv_taps": _terms(inv_taps, True)}


def _cmul(a, b, conj_b=False):
    ar, ai, br, bi = a[:DFT_F], a[DFT_F:], b[:DFT_F], b[DFT_F:]
    if conj_b:
        return jnp.concatenate([ar * br + ai * bi, ai * br - ar * bi], axis=0)
    return jnp.concatenate([ar * br - ai * bi, ar * bi + ai * br], axis=0)


def _interleave(sub_tiles):
    waiting, live = list(sub_tiles), []
    while waiting or live:
        if waiting:
            live.append(waiting.pop(0))
        for g in list(live):
            try:
                next(g)
            except StopIteration:
                live.remove(g)


def _start_copies(sems, pairs, first=0):
    copies = [pltpu.make_async_copy(src, dst, sems.at[first + k]) for k, (src, dst) in enumerate(pairs)]
    for cp in copies:
        cp.start()
    return copies


def _cparams():
    return pltpu.CompilerParams(dimension_semantics=("arbitrary",), vmem_limit_bytes=VMEM_LIMIT)


def _full(shape):
    return pl.BlockSpec(shape, lambda i: (0,) * len(shape))


ANY = pl.BlockSpec(memory_space=pl.ANY)

VQ_SGU_G, VQ_SGU_B, VQ_CONV_B, VQ_CLN_G, VQ_CLN_B = range(5)
VD_LN1_G, VD_LN1_B, VD_LN2_G, VD_LN2_B = range(4)
RS_LN1, RS_SGU, RS_CONV = range(3)
RS_COLS = LANES


def _saved_widths(d, q):
    f32 = [d, q, q, q, q, q, RS_COLS]
    bf16 = [d, 2 * q, d, q]
    return f32, bf16


def _saved_views(f32_ref, bf16_ref, d, q):
    views = []
    for ref, widths in zip((f32_ref, bf16_ref), _saved_widths(d, q)):
        for k, w in enumerate(widths):
            views.append(ref.at[pl.ds(0, ref.shape[0]), pl.ds(sum(widths[:k]), w)])
    return views
def _fwd_mix(x, wi, wo, wst, bmat, cwf, tabs, vq, vd, mlp_w, alpha, tm):
    t, d = x.shape
    q = wi.shape[2]
    nc, n_pairs = CONV_BLOCK // CHUNK, q // LANES
    n = t // tm
    n_in, n_saved = 11, 3
    assert tm % CONV_BLOCK == 0

    def body(x_ref, wi_hbm, wo_hbm, wst_ref, bmat_ref, cwf_ref, fwd_ref, taps_ref, inv_ref, vq_ref, vd_ref, *rest):
        f32_ref, bf16_ref, hf_ref = rest[3:3 + n_saved]
        xh_ref, zu_ref, mg_ref, vhat_ref, gv_ref, yhat_ref, rs_ref, xb_ref, pag_ref, y_ref, vnb_ref = _saved_views(f32_ref, bf16_ref, d, q)
        gathered = rest[3 + n_saved:6 + n_saved]
        wi_v, wo_v, hb_ref, gf_ref, send_sems, recv_sems, copy_sems = rest[6 + n_saved:]
        step = pl.program_id(0)

        @pl.when(step == 0)
        def _():
            loads = _start_copies(copy_sems, [(wi_hbm, wi_v), (wo_hbm, wo_v)])
            _Gather(gathered, send_sems, recv_sems).start()
            hb_ref[...] = jnp.zeros_like(hb_ref)
            gf_ref[...] = _dot(taps_ref[...], _split(cwf_ref[...], True))
            for cp in loads:
                cp.wait()

        @pl.when(step == (3 * n) // 4)
        def _():
            _Gather(gathered, send_sems, recv_sems).forward()

        def sub_tile(b):
            rows = slice(b * CONV_BLOCK, (b + 1) * CONV_BLOCK)
            xv = x_ref[rows, :]
            xb = xv.astype(BF16)
            xb_ref[rows, :] = xb
            pu, pv, pa, pg = (_dot(xb, wi_v[j]) for j in range(4))
            yield
            pag_ref[rows, 0:q] = pa.astype(BF16)
            pag_ref[rows, q:2 * q] = pg.astype(BF16)
            zu, gu = _gelu(pu)
            zv, gv = _gelu(pv)
            vhat, rstd_v = _ln_stats(zv)
            vnb = (vhat * vq_ref[VQ_SGU_G:VQ_SGU_G + 1, :] + vq_ref[VQ_SGU_B:VQ_SGU_B + 1, :]).astype(BF16)
            hb_ref[HALO + b * CONV_BLOCK:HALO + (b + 1) * CONV_BLOCK, :] = pa * _sigmoid(pg)
            yield
            mixed = _mix(wst_ref, vnb, nc, n_pairs) + jnp.concatenate([bmat_ref[...]] * nc, axis=0)
            spectrum = _dot(fwd_ref[...], _split(hb_ref[b * CONV_BLOCK:b * CONV_BLOCK + DFT_N, :]))
            yield
            y_ref[rows, 0:q] = (zu * mixed).astype(BF16)
            zu_ref[rows, :] = zu
            mg_ref[rows, :] = mixed * gu
            vhat_ref[rows, :] = vhat
            gv_ref[rows, :] = gv
            vnb_ref[rows, :] = vnb
            hf_ref[b * 2 * DFT_F:(b + 1) * 2 * DFT_F, :] = spectrum
            product = _split(_cmul(gf_ref[...], spectrum))
            yield
            yc = _dot(inv_ref[...], product) + vq_ref[VQ_CONV_B:VQ_CONV_B + 1, :]
            yield
            yhat, rstd_c = _ln_stats(yc)
            yhat_ref[rows, :] = yhat
            yn = yhat * vq_ref[VQ_CLN_G:VQ_CLN_G + 1, :] + vq_ref[VQ_CLN_B:VQ_CLN_B + 1, :]
            y_ref[rows, q:2 * q] = (yn * _sigmoid(yn)).astype(BF16)
            yield
            r1 = alpha * xv + _dot(y_ref[rows, :], wo_v[...])
            yield
            xhat, rstd1 = _ln_stats(r1)
            xh_ref[rows, :] = xhat
            col = lax.broadcasted_iota(jnp.int32, (CONV_BLOCK, RS_COLS), 1)
            rs_ref[rows, :] = jnp.where(col == RS_LN1, rstd1, jnp.where(col == RS_SGU, rstd_v, jnp.where(col == RS_CONV, rstd_c, 0.0)))

        _interleave([sub_tile(b) for b in range(tm // CONV_BLOCK)])
        hb_ref[0:HALO, :] = hb_ref[tm:tm + HALO, :]

        @pl.when(step == n - 1)
        def _():
            _Gather(gathered, send_sems, recv_sems).finish()

    row = lambda w: pl.BlockSpec((tm, w), lambda i: (i, 0))
    widths = [(sum(w), dt) for w, dt in zip(_saved_widths(d, q), (F32, BF16))]
    small_ins = [wst, bmat, cwf, tabs["fwd"], tabs["taps"], tabs["inv_out"], vq, vd]
    return pl.pallas_call(
        body, name="fwd_mix", grid=(n,),
        in_specs=[row(d), ANY, ANY] + [_full(a.shape) for a in small_ins] + [ANY] * 3,
        out_specs=[row(w) for w, _ in widths] + [pl.BlockSpec((tm // CONV_BLOCK * 2 * DFT_F, q), lambda i: (i, 0))] + [ANY] * 3,
        out_shape=[jax.ShapeDtypeStruct((t, w), dt) for w, dt in widths] + [jax.ShapeDtypeStruct((t // CONV_BLOCK * 2 * DFT_F, q), F32)]
        + [jax.ShapeDtypeStruct(b.shape, b.dtype) for b in mlp_w],
        scratch_shapes=[pltpu.VMEM(wi.shape, BF16), pltpu.VMEM(wo.shape, BF16), pltpu.VMEM((HALO + tm, q), F32),
                        pltpu.VMEM((2 * DFT_F, q), F32)] + _gather_sems(3) + [pltpu.SemaphoreType.DMA((2,))],
        input_output_aliases={n_in + a: n_saved + a for a in range(3)},
        compiler_params=_cparams(),
    )(x, wi, wo, *small_ins, *mlp_w)


MLP_SLABS = 4


def _hidden_slabs(f):
    assert f % MXU_N == 0
    tiles = f // MXU_N
    sizes = [(tiles // MLP_SLABS + (1 if j < tiles % MLP_SLABS else 0)) * MXU_N for j in range(MLP_SLABS)]
    return [(sum(sizes[:j]), sz) for j, sz in enumerate(sizes) if sz]


def _fwd_mlp(saved_f32, tgt, wg, wu, wd, vd, alpha, slabs, tm):
    t, d = tgt.shape
    n = t // tm
    ns = len(slabs)
    half = tm // 2 if tm % 32 == 0 else tm

    def body(xh_ref, tgt_ref, wg_hbm, wu_hbm, wd_hbm, vd_ref, *rest):
        gp_refs = [r.at[pl.ds(0, tm), pl.ds(0, sz)] for r, (_, sz) in zip(rest[:ns], slabs)]
        up_refs = [r.at[pl.ds(0, tm), pl.ds(sz, sz)] for r, (_, sz) in zip(rest[:ns], slabs)]
        x1b_ref, dr2_ref, loss_ref, dg2_ref, db2_ref, wg_v, wu_v, wd_v, copy_sems = rest[ns:]

        @pl.when(pl.program_id(0) == 0)
        def _():
            loads = _start_copies(copy_sems, [(wg_hbm, wg_v), (wu_hbm, wu_v), (wd_hbm, wd_v)])
            loss_ref[...] = jnp.zeros_like(loss_ref)
            dg2_ref[...] = jnp.zeros_like(dg2_ref)
            db2_ref[...] = jnp.zeros_like(db2_ref)
            for cp in loads:
                cp.wait()

        g2 = vd_ref[VD_LN2_G:VD_LN2_G + 1, :]

        for r0 in range(0, tm, half):
            rows = slice(r0, r0 + half)
            x1 = xh_ref[rows, :] * vd_ref[VD_LN1_G:VD_LN1_G + 1, :] + vd_ref[VD_LN1_B:VD_LN1_B + 1, :]
            x1b = x1.astype(BF16)
            x1b_ref[rows, :] = x1b
            acc = alpha * x1
            for (off, sz), gp_ref, up_ref in zip(slabs, gp_refs, up_refs):
                gp = _dot_nt(x1b, wg_v[off:off + sz, :])
                up = _dot_nt(x1b, wu_v[off:off + sz, :])
                gp_ref[rows, :] = gp.astype(BF16)
                up_ref[rows, :] = up.astype(BF16)
                acc = acc + _dot((gp * _sigmoid(gp) * up).astype(BF16), wd_v[off:off + sz, :])
            xh2, rstd2 = _ln_stats(acc)
            err = xh2 * g2 + vd_ref[VD_LN2_B:VD_LN2_B + 1, :] - tgt_ref[rows, :]
            loss_ref[...] += _colsum(jnp.sum(err * err, axis=1, keepdims=True)) * (0.5 / d)
            dy = err * (1.0 / d)
            dg2_ref[...] += _colsum(dy * xh2)
            db2_ref[...] += _colsum(dy)
            dr2_ref[rows, :] = _ln_bwd(dy * g2, xh2, rstd2)

    row = lambda w: pl.BlockSpec((tm, w), lambda i: (i, 0))
    act = [2 * sz for _, sz in slabs]
    return pl.pallas_call(
        body, name="fwd_mlp", grid=(n,),
        in_specs=[row(d), row(d), ANY, ANY, ANY, _full(vd.shape)],
        out_specs=[row(sz) for sz in act] + [row(d), row(d), _full((8, LANES)), _full((1, d)), _full((1, d))],
        out_shape=[jax.ShapeDtypeStruct((t, sz), BF16) for sz in act]
        + [jax.ShapeDtypeStruct((t, d), BF16), jax.ShapeDtypeStruct((t, d), F32),
           jax.ShapeDtypeStruct((8, LANES), F32), jax.ShapeDtypeStruct((1, d), F32), jax.ShapeDtypeStruct((1, d), F32)],
        scratch_shapes=[pltpu.VMEM(wg.shape, BF16), pltpu.VMEM(wu.shape, BF16), pltpu.VMEM(wd.shape, BF16), pltpu.SemaphoreType.DMA((3,))],
        compiler_params=_cparams(),
    )(saved_f32, tgt, wg, wu, wd, vd)


def _bwd_mlp_slab(j, slab, dr2, prev, x1b, gate_up, wg, wu, wd, alpha, tm):
    t, d = dr2.shape
    off, sz = slab
    n = t // tm
    first = prev is None

    def body(*refs):
        if first:
            dr_ref, x1b_ref, gu_ref, wg_hbm, wu_hbm, wd_hbm = refs[:6]
        else:
            dr_ref, dxp_ref, x1b_ref, gu_ref, wg_hbm, wu_hbm, wd_hbm = refs[:7]
        dx_ref, dwg_hbm, dwu_hbm, dwd_hbm, dwg16_hbm, dwu16_hbm, dwd16_hbm, ag, au, ad, wg_v, wu_v, wd_v, copy_sems = refs[-14:]

        @pl.when(pl.program_id(0) == 0)
        def _():
            loads = _start_copies(copy_sems, [(src.at[pl.ds(off, sz)], dst) for src, dst in ((wg_hbm, wg_v), (wu_hbm, wu_v), (wd_hbm, wd_v))])
            ag[...] = jnp.zeros_like(ag)
            au[...] = jnp.zeros_like(au)
            ad[...] = jnp.zeros_like(ad)
            for cp in loads:
                cp.wait()

        dr = dr_ref[...]
        drb = dr.astype(BF16)
        x1b = x1b_ref[...]
        gpv = gu_ref[:, 0:sz].astype(F32)
        upv = gu_ref[:, sz:2 * sz].astype(F32)
        dh = _dot_nt(drb, wd_v[...])
        sg = _sigmoid(gpv)
        silu = gpv * sg
        ad[...] += _dot_tn((silu * upv).astype(BF16), drb)
        dgp = (dh * upv * (sg * (1.0 + gpv * (1.0 - sg)))).astype(BF16)
        dup = (dh * silu).astype(BF16)
        ag[...] += _dot_tn(dgp, x1b)
        au[...] += _dot_tn(dup, x1b)
        base = alpha * dr if first else dxp_ref[...]
        dx_ref[...] = base + _dot(dgp, wg_v[...]) + _dot(dup, wu_v[...])

        @pl.when(pl.program_id(0) == n - 1)
        def _():
            rows = pl.ds(off, sz)
            stores = _start_copies(copy_sems, [(ag, dwg_hbm.at[rows]), (au, dwu_hbm.at[rows]), (ad, dwd_hbm.at[rows])])
            for acc, stage in ((ag, wg_v), (au, wu_v), (ad, wd_v)):
                stage[...] = acc[...].astype(BF16)
            stores += _start_copies(copy_sems, [(wg_v, dwg16_hbm.at[rows]), (wu_v, dwu16_hbm.at[rows]), (wd_v, dwd16_hbm.at[rows])], first=3)
            for cp in stores:
                cp.wait()


    row = lambda w: pl.BlockSpec((tm, w), lambda i: (i, 0))
    ins = [dr2] + ([] if first else [prev[0]]) + [x1b, gate_up, wg, wu, wd] + ([] if first else list(prev[1:]))
    in_specs = [row(d)] + ([] if first else [row(d)]) + [row(d), row(2 * sz), ANY, ANY, ANY] + ([] if first else [ANY] * 6)
    return pl.pallas_call(
        body, name=f"bwd_mlp_{j}", grid=(n,),
        in_specs=in_specs,
        out_specs=[row(d)] + [ANY] * 6,
        out_shape=[jax.ShapeDtypeStruct((t, d), F32)] + [jax.ShapeDtypeStruct(wg.shape, F32)] * 3 + [jax.ShapeDtypeStruct(wg.shape, BF16)] * 3,
        scratch_shapes=[pltpu.VMEM((sz, d), F32)] * 3 + [pltpu.VMEM((sz, d), BF16)] * 3 + [pltpu.SemaphoreType.DMA((6,))],
        input_output_aliases={} if first else {7 + a: 1 + a for a in range(6)},
        compiler_params=_cparams(),
    )(*ins)


SMALL_VD = CHUNK
SMALL_CW = CHUNK + 8
SMALL_VQ = CHUNK + 8
SMALL_LOSS = CHUNK + 16
SMALL_BS = CHUNK + 24


def _small_rows(kwp):
    return -(-(SMALL_CW + max(kwp, 24 + SUBLANES)) // 16) * 16


def _bwd_mix(dx1, saved, wi, wo, wstt, cwf, tabs, vq, vd, mlp_small, token, alpha, tm):
    saved_f32, saved_bf16, hf_s = saved
    t, d = dx1.shape
    q = wi.shape[2]
    nc, n_pairs = CONV_BLOCK // CHUNK, q // LANES
    n = t // tm
    nb = tm // CONV_BLOCK
    assert tm % CONV_BLOCK == 0

    def body(dx1_ref, f32_ref, bf16_ref, hf_ref,
             wi_hbm, wo_hbm, wstt_ref, cwf_ref, fwd_ref, fwd_halo_ref, shift_ref, taps_ref, inv_ref, inv_taps_ref, vq_ref, vd_ref,
             loss_ref, dg2_ref, db2_ref, token_ref,
             gx_ref, dwi_hbm, dwo_hbm, small_ws_hbm, small_rest_hbm, dwi16_hbm, dwo16_hbm,
             wi_v, wo_v, awi, awo, dyb_ref, later_ref, dbm_ref, gf_ref, dgf_ref, small_ref, copy_sems):
        xh_ref, zu_ref, mg_ref, vhat_ref, gv_ref, yhat_ref, rs_ref, xb_ref, pag_ref, y_ref, vnb_ref = _saved_views(f32_ref, bf16_ref, d, q)
        i = pl.program_id(0)

        @pl.when(i == 0)
        def _():
            loads = _start_copies(copy_sems, [(wi_hbm, wi_v), (wo_hbm, wo_v)])
            for r in (awi, awo, small_ref, dbm_ref, dgf_ref, dyb_ref, later_ref):
                r[...] = jnp.zeros_like(r)
            gf_ref[...] = _dot(taps_ref[...], _split(cwf_ref[...], True))
            for cp in loads:
                cp.wait()

        dr1b_parts, dproj_parts = [None] * nb, [None] * nb

        def sub_tile(b):
            rows = slice(b * CONV_BLOCK, (b + 1) * CONV_BLOCK)
            dx1v = dx1_ref[rows, :]
            xh = xh_ref[rows, :]
            rsv = rs_ref[rows, :]
            small_ref[SMALL_VD + VD_LN1_G:SMALL_VD + VD_LN1_G + 1, :] += _colsum(dx1v * xh)
            small_ref[SMALL_VD + VD_LN1_B:SMALL_VD + VD_LN1_B + 1, :] += _colsum(dx1v)
            dr1 = _ln_bwd(dx1v * vd_ref[VD_LN1_G:VD_LN1_G + 1, :], xh, rsv[:, RS_LN1:RS_LN1 + 1])
            dr1b = dr1.astype(BF16)
            yield
            dy = _dot_nt(dr1b, wo_v[...])
            yield
            vhat = vhat_ref[rows, :]
            sgu_g = vq_ref[VQ_SGU_G:VQ_SGU_G + 1, :]
            doa = dy[:, 0:q]
            dm = doa * zu_ref[rows, :]
            dpu = (doa * mg_ref[rows, :]).astype(BF16)
            acc = dm[0:CHUNK]
            for c in range(1, nc):
                acc = acc + dm[c * CHUNK:(c + 1) * CHUNK]
            dbm_ref[...] += acc
            pa = pag_ref[rows, 0:q].astype(F32)
            sg = _sigmoid(pag_ref[rows, q:2 * q].astype(F32))
            yhat = yhat_ref[rows, :]
            cln_g = vq_ref[VQ_CLN_G:VQ_CLN_G + 1, :]
            yn = yhat * cln_g + vq_ref[VQ_CLN_B:VQ_CLN_B + 1, :]
            sy = _sigmoid(yn)
            dyn = dy[:, q:2 * q] * (sy * (1.0 + yn * (1.0 - sy)))
            small_ref[SMALL_VQ + VQ_CLN_G:SMALL_VQ + VQ_CLN_G + 1, q:2 * q] += _colsum(dyn * yhat)
            small_ref[SMALL_VQ + VQ_CLN_B:SMALL_VQ + VQ_CLN_B + 1, q:2 * q] += _colsum(dyn)
            dyc = _ln_bwd(dyn * cln_g, yhat, rsv[:, RS_CONV:RS_CONV + 1])
            small_ref[SMALL_VQ + VQ_CONV_B:SMALL_VQ + VQ_CONV_B + 1, q:2 * q] += _colsum(dyc)
            dyb_ref[b, 0:CONV_BLOCK, :] = dyc
            yield
            wgrads = _mix_wgrad(dm, vnb_ref[rows, :], nc, n_pairs)
            dvn = _mix(wstt_ref, dm, nc, n_pairs)
            own = _dot(fwd_ref[...], _split(dyb_ref[b]))
            with_later = own + _dot(fwd_halo_ref[...], _split(later_ref[...]))
            later_ref[...] = dyb_ref[b, 0:HALO, :]
            yield
            for p, g in enumerate(wgrads):
                for half in range(2):
                    small_ref[0:CHUNK, (2 * p + half) * CHUNK:(2 * p + half + 1) * CHUNK] += g[half * CHUNK:(half + 1) * CHUNK]
            small_ref[SMALL_VQ + VQ_SGU_G:SMALL_VQ + VQ_SGU_G + 1, q:2 * q] += _colsum(dvn * vhat)
            small_ref[SMALL_VQ + VQ_SGU_B:SMALL_VQ + VQ_SGU_B + 1, q:2 * q] += _colsum(dvn)
            dpv = (_ln_bwd(dvn * sgu_g, vhat, rsv[:, RS_SGU:RS_SGU + 1]) * gv_ref[rows, :]).astype(BF16)
            dgf_ref[...] += _cmul(_cmul(own, shift_ref[...]), hf_ref[b * 2 * DFT_F:(b + 1) * 2 * DFT_F, :], conj_b=True)
            product = _split(_cmul(with_later, gf_ref[...], conj_b=True))
            yield
            dh = _dot(inv_ref[...], product)
            yield
            da = (dh * sg).astype(BF16)
            dg = (dh * pa * (sg * (1.0 - sg))).astype(BF16)
            yield
            gx = alpha * dr1
            for dpj, wj in zip((dpu, dpv, da, dg), range(4)):
                gx = gx + _dot_nt(dpj, wi_v[wj])
            gx_ref[rows, :] = gx
            dr1b_parts[b], dproj_parts[b] = dr1b, (dpu, dpv, da, dg)

        _interleave([sub_tile(b) for b in reversed(range(nb))])

        awo[...] += _dot_tn(y_ref[...], jnp.concatenate(dr1b_parts, axis=0))
        xb = xb_ref[...]
        for j in range(4):
            awi[j] += _dot_tn(xb, jnp.concatenate([part[j] for part in dproj_parts], axis=0))

        @pl.when(i == n - 1)
        def _():
            stores = _start_copies(copy_sems, [(awi, dwi_hbm), (awo, dwo_hbm)])
            wi_v[...] = awi[...].astype(BF16)
            wo_v[...] = awo[...].astype(BF16)
            stores += _start_copies(copy_sems, [(wi_v, dwi16_hbm), (wo_v, dwo16_hbm)], first=3)
            lane = lax.broadcasted_iota(jnp.int32, (CHUNK, LANES), 1)
            low = lane < HEAD_DIM
            dbs = jnp.zeros((CHUNK, LANES), F32)
            for p in range(n_pairs):
                grp = dbm_ref[:, p * LANES:(p + 1) * LANES]
                dbs = jnp.where(lane == 2 * p, jnp.sum(jnp.where(low, grp, 0.0), axis=1, keepdims=True), dbs)
                dbs = jnp.where(lane == 2 * p + 1, jnp.sum(jnp.where(low, 0.0, grp), axis=1, keepdims=True), dbs)
            tril = lax.broadcasted_iota(jnp.int32, (CHUNK, CHUNK), 0) >= lax.broadcasted_iota(jnp.int32, (CHUNK, CHUNK), 1)
            for h in range(2 * n_pairs):
                block = small_ref[0:CHUNK, h * CHUNK:(h + 1) * CHUNK]
                small_ref[0:CHUNK, h * CHUNK:(h + 1) * CHUNK] = jnp.where(tril, block, 0.0)
            small_ref[SMALL_VD + VD_LN2_G:SMALL_VD + VD_LN2_G + 1, :] = dg2_ref[...]
            small_ref[SMALL_VD + VD_LN2_B:SMALL_VD + VD_LN2_B + 1, :] = db2_ref[...]
            small_ref[SMALL_CW:SMALL_CW + kwp, 0:q] = _dot(inv_taps_ref[...], _split(dgf_ref[...], True))
            small_ref[SMALL_LOSS:SMALL_LOSS + SUBLANES, q:q + LANES] = loss_ref[...]
            small_ref[SMALL_BS:SMALL_BS + SUBLANES, q:q + LANES] = jnp.transpose(dbs)[0:SUBLANES]
            stores += _start_copies(copy_sems, [(small_ref.at[pl.ds(0, CHUNK)], small_ws_hbm)], first=2)
            stores += _start_copies(copy_sems, [(small_ref.at[pl.ds(CHUNK, small.shape[0] - CHUNK)], small_rest_hbm)], first=5)
            for cp in stores:
                cp.wait()

    rev = lambda w: pl.BlockSpec((tm, w), lambda i: (n - 1 - i, 0))
    kwp = cwf.shape[0]
    small = jax.ShapeDtypeStruct((_small_rows(kwp), 2 * q), F32)
    small_ins = [wstt, cwf, tabs["fwd"], tabs["fwd_halo"], tabs["shift"], tabs["taps"], tabs["inv_in"], tabs["inv_taps"], vq, vd,
                 *mlp_small]
    return pl.pallas_call(
        body, name="bwd_mix", grid=(n,),
        in_specs=[rev(d), rev(saved_f32.shape[1]), rev(saved_bf16.shape[1]),
                  pl.BlockSpec((nb * 2 * DFT_F, q), lambda i: (n - 1 - i, 0)), ANY, ANY] + [_full(a.shape) for a in small_ins] + [ANY],
        out_specs=[rev(d)] + [ANY] * 6,
        out_shape=[jax.ShapeDtypeStruct((t, d), F32), jax.ShapeDtypeStruct(wi.shape, F32), jax.ShapeDtypeStruct(wo.shape, F32),
                   jax.ShapeDtypeStruct((CHUNK, 2 * q), F32), jax.ShapeDtypeStruct((small.shape[0] - CHUNK, 2 * q), F32),
                   jax.ShapeDtypeStruct(wi.shape, BF16), jax.ShapeDtypeStruct(wo.shape, BF16)],
        scratch_shapes=[pltpu.VMEM(wi.shape, BF16), pltpu.VMEM(wo.shape, BF16), pltpu.VMEM(wi.shape, F32), pltpu.VMEM(wo.shape, F32),
                        pltpu.VMEM((nb, DFT_N, q), F32), pltpu.VMEM((HALO, q), F32),
                        pltpu.VMEM((CHUNK, q), F32), pltpu.VMEM((2 * DFT_F, q), F32), pltpu.VMEM((2 * DFT_F, q), F32),
                        pltpu.VMEM(small.shape, F32), pltpu.SemaphoreType.DMA((6,))],
        compiler_params=_cparams(),
    )(dx1, saved_f32, saved_bf16, hf_s, wi, wo, *small_ins, token)


def _coords():
    return tuple(lax.axis_index(a) for a in MESH_AXES)


def _other_chips(x, y):
    return [(1 - x, y), (x, 1 - y), (1 - x, 1 - y)]


def _remote(src, dst, send_sem, recv_sem, to):
    return pltpu.make_async_remote_copy(src_ref=src, dst_ref=dst, send_sem=send_sem, recv_sem=recv_sem,
                                        device_id=to, device_id_type=MESH_ID)


class _Gather:
    def __init__(self, bufs, send_sems, recv_sems, own=None):
        self.bufs, self.send_sems, self.recv_sems, self.own = bufs, send_sems, recv_sems, own
        self.x, self.y, self.c = _coords()

    def _copies(self, stage):
        x, y, c = self.x, self.y, self.c
        for a, buf in enumerate(self.bufs):
            hr = buf.shape[1] // 2
            for j, chip in enumerate(_other_chips(x, y)):
                if stage == "ici_out":
                    ref, k, to = buf.at[2 * x + y, pl.ds(c * hr, hr)], j, (*chip, c)
                    if self.own:
                        yield _remote(self.own[a].at[pl.ds(c * hr, hr)], ref, self.send_sems.at[a, k], self.recv_sems.at[a, k], to)
                        continue
                elif stage == "ici_in":
                    ref, k, to = buf.at[2 * chip[0] + chip[1], pl.ds(c * hr, hr)], j, (*chip, c)
                elif stage == "d2d_out":
                    ref, k, to = buf.at[2 * chip[0] + chip[1], pl.ds(c * hr, hr)], 3 + j, (x, y, 1 - c)
                else:
                    ref, k, to = buf.at[2 * chip[0] + chip[1], pl.ds((1 - c) * hr, hr)], 3 + j, (x, y, 1 - c)
                yield _remote(ref, ref, self.send_sems.at[a, k], self.recv_sems.at[a, k], to)

    def start(self):
        for cp in self._copies("ici_out"):
            cp.start()

    def forward(self):
        for landed, onward in zip(self._copies("ici_in"), self._copies("d2d_out")):
            landed.wait_recv()
            onward.start()

    def finish(self):
        for cp in self._copies("d2d_in"):
            cp.wait_recv()
        for stage in ("ici_out", "d2d_out"):
            for cp in self._copies(stage):
                cp.wait_send()


def _gather_sems(n):
    return [pltpu.SemaphoreType.DMA((n, 6)), pltpu.SemaphoreType.DMA((n, 6))]


def _prep_gather(gathered, local, conv_w, kwp):
    kw, _, cshard = conv_w.shape
    shards = list(gathered) + list(local)
    n, n_g = len(shards), len(gathered)

    def body(*refs):
        src, cw_ref, out = refs[:n], refs[n], refs[n + 1:2 * n + 2]
        wide, narrow, taps = refs[2 * n + 2:3 * n + 2], refs[3 * n + 2:4 * n + 2], refs[4 * n + 2]
        load_sems, store_sems, send_sems, recv_sems = refs[4 * n + 3:]
        x, y, _ = _coords()
        loads = [pltpu.make_async_copy(src[a], wide[a], load_sems.at[a]) for a in range(n)]
        stores = [pltpu.make_async_copy(narrow[a], out[a].at[2 * x + y], store_sems.at[a]) for a in range(n)]
        stores.append(pltpu.make_async_copy(taps, out[n].at[2 * x + y], store_sems.at[n]))
        for cp in loads:
            cp.start()

        def cast(a):
            loads[a].wait()
            narrow[a][...] = wide[a][...].astype(BF16)
            stores[a].start()

        for a in range(n_g):
            cast(a)
        for tap in range(kw):
            taps[tap:tap + 1, :] = cw_ref[tap]
        taps[kw:kwp, :] = jnp.zeros((kwp - kw, cshard), F32)
        stores[n].start()
        g = _Gather(list(out[:n_g]) + [out[n]], send_sems, recv_sems, own=list(narrow[:n_g]) + [taps])
        g.start()
        for a in range(n_g, n):
            cast(a)
        g.forward()
        g.finish()
        for cp in stores:
            cp.wait()

    out_shape = [jax.ShapeDtypeStruct((4,) + a.shape, BF16) for a in shards] + [jax.ShapeDtypeStruct((4, kwp, cshard), F32)]
    scratch = ([pltpu.VMEM(a.shape, F32) for a in shards] + [pltpu.VMEM(a.shape, BF16) for a in shards] + [pltpu.VMEM((kwp, cshard), F32)]
               + [pltpu.SemaphoreType.DMA((n,)), pltpu.SemaphoreType.DMA((n + 1,))] + _gather_sems(n_g + 1))
    res = pl.pallas_call(
        body, name="prep_gather", in_specs=[ANY] * n + [pl.BlockSpec(memory_space=pltpu.VMEM)], out_specs=[ANY] * (n + 1),
        out_shape=out_shape, scratch_shapes=scratch, compiler_params=pltpu.CompilerParams(vmem_limit_bytes=VMEM_LIMIT),
    )(*shards, conv_w)
    return list(res[:n_g]), list(res[n_g:n]), res[n]


def _pair_reduce(name, partials, payloads, c_arr, out_dtypes):
    n = len(partials)
    counts = [g.shape[0] for g in partials]
    first = [sum(counts[:a]) for a in range(n)]
    steps = sum(counts)

    def body(c_ref, *refs):
        own, travelling, out, land = (refs[k * n:(k + 1) * n] for k in range(4))
        send_sems, recv_sems = refs[4 * n:]
        i = pl.program_id(0)
        x, y, c = _coords()

        def copy(a, q):
            return _remote(travelling[a].at[q, 1 - c], land[a].at[q], send_sems.at[first[a] + q], recv_sems.at[first[a] + q],
                           (x, y, 1 - c))

        blocks = [(a, q) for a in range(n) for q in range(counts[a])]

        @pl.when(i == 0)
        def _():
            for a, q in blocks:
                copy(a, q).start()

        for a in range(n):
            @pl.when((i >= first[a]) & (i < first[a] + counts[a]))
            def _(a=a):
                q = i - first[a]
                copy(a, q).wait_recv()
                out[a][...] = (own[a][...] + land[a][q].astype(F32)).astype(out_dtypes[a])

        @pl.when(i == steps - 1)
        def _():
            for a, q in blocks:
                copy(a, q).wait_send()

    at = lambda a, i: jnp.clip(i - first[a], 0, counts[a] - 1)
    in_specs = [pl.BlockSpec((None, None) + g.shape[2:], lambda i, cr, a=a: (at(a, i), cr[0], 0, 0)) for a, g in enumerate(partials)]
    out_specs = [pl.BlockSpec((None,) + g.shape[2:], lambda i, cr, a=a: (at(a, i), 0, 0)) for a, g in enumerate(partials)]
    grid_spec = pltpu.PrefetchScalarGridSpec(
        num_scalar_prefetch=1, grid=(steps,), in_specs=in_specs + [ANY] * n, out_specs=out_specs,
        scratch_shapes=[pltpu.VMEM((g.shape[0],) + g.shape[2:], p.dtype) for g, p in zip(partials, payloads)]
        + [pltpu.SemaphoreType.DMA((steps,)), pltpu.SemaphoreType.DMA((steps,))])
    out_shape = [jax.ShapeDtypeStruct((g.shape[0],) + g.shape[2:], dt) for g, dt in zip(partials, out_dtypes)]
    return list(pl.pallas_call(body, name=name, grid_spec=grid_spec, out_shape=out_shape, compiler_params=_cparams())(
        c_arr, *partials, *payloads))


class _Exchange:
    def __init__(self, src, dst, send_sems, recv_sems):
        self.src, self.dst, self.send_sems, self.recv_sems = src, dst, send_sems, recv_sems
        self.x, self.y, self.c = _coords()

    def _copies(self, incoming):
        x, y, c = self.x, self.y, self.c
        for a, (s, d) in enumerate(zip(self.src, self.dst)):
            for j, chip in enumerate(_other_chips(x, y)):
                slot = 2 * chip[0] + chip[1]
                if incoming:
                    out, into = d.at[slot], d.at[slot]
                else:
                    out, into = (s.at[slot] if len(s.shape) == 3 else s), d.at[2 * x + y]
                yield _remote(out, into, self.send_sems.at[a, j], self.recv_sems.at[a, j], (*chip, c))

    def start(self):
        for cp in self._copies(False):
            cp.start()

    def finish(self):
        for cp in self._copies(True):
            cp.wait_recv()
        for cp in self._copies(False):
            cp.wait_send()


def _exchange_shapes(arrs):
    return [jax.ShapeDtypeStruct((4,) + s.shape[-2:], s.dtype) for s in arrs]


class _FlatSems:
    def __init__(self, ref):
        self.ref = ref

    @property
    def at(self):
        return self

    def __getitem__(self, idx):
        return self.ref.at[3 * idx[0] + idx[1]]


HBM = pl.BlockSpec(memory_space=pltpu.HBM)
SEM = pl.BlockSpec(memory_space=pltpu.SEMAPHORE)
DATAFLOW = pltpu.SideEffectType.DATAFLOW_SIDE_EFFECTING


def _exchange_start(name, arrs):
    n = len(arrs)
    lands = _exchange_shapes(arrs)

    def body(*refs):
        src, land = refs[:n], refs[n:2 * n]
        send_sems, recv_sems = refs[2 * n:2 * n + 2]
        token = refs[-1]
        _Exchange(src, land, _FlatSems(send_sems), _FlatSems(recv_sems)).start()
        token[...] = jnp.zeros_like(token)

    hbm = lambda a: pltpu.with_memory_space_constraint(a, pltpu.HBM)
    outs = pl.pallas_call(
        body, name=name,
        out_shape=(pltpu.SemaphoreType.DMA((3 * n,)), pltpu.SemaphoreType.DMA((3 * n,)),
                   *[pltpu.HBM(a.shape, a.dtype) for a in arrs], *[pltpu.HBM(s.shape, s.dtype) for s in lands],
                   jax.ShapeDtypeStruct((SUBLANES, LANES), F32)),
        in_specs=[HBM] * (2 * n), out_specs=(SEM, SEM, *[HBM] * (2 * n), pl.BlockSpec(memory_space=pltpu.VMEM)),
        input_output_aliases={a: 2 + a for a in range(2 * n)},
        compiler_params=pltpu.CompilerParams(has_side_effects=DATAFLOW),
    )(*[hbm(a) for a in arrs], *[hbm(lax.empty(s.shape, s.dtype)) for s in lands])
    return outs[:-1], outs[-1]


def _exchange_wait(name, started, after):
    send_sems, recv_sems, *bufs = started
    n = len(bufs) // 2

    def body(*refs):
        src, land = refs[:n], refs[n:2 * n]
        send_sems, recv_sems = refs[2 * n:2 * n + 2]
        _Exchange(src, land, _FlatSems(send_sems), _FlatSems(recv_sems)).finish()

    outs = pl.pallas_call(
        body, name=name,
        out_shape=tuple(pltpu.HBM(b.shape, b.dtype) for b in bufs),
        in_specs=[HBM] * (2 * n) + [SEM, SEM] + [ANY] * len(after), out_specs=tuple([HBM] * (2 * n)),
        input_output_aliases={a: a for a in range(2 * n)},
        compiler_params=pltpu.CompilerParams(has_side_effects=DATAFLOW),
    )(*bufs, send_sems, recv_sems, *after)
    return list(outs[:n]), list(outs[n:])


def _chip_reduce(name, parts, owns, after):
    n = len(parts)

    def body(*refs):
        part, own = refs[:n], refs[n:2 * n]
        mine, other = refs[2 * n + len(after):3 * n + len(after)], refs[3 * n + len(after):4 * n + len(after)]
        scratch = refs[4 * n + len(after):]
        landed, own_part, total = scratch[:n], scratch[n:2 * n], scratch[2 * n:3 * n]
        load_sems, store_sems, send_sems, recv_sems = scratch[3 * n:]
        x, y, c = _coords()
        me = 2 * x + y
        loads, stores, sends = [], [], []
        for a in range(n):
            loads.append((pltpu.make_async_copy(part[a], landed[a], load_sems.at[a, 0]),
                          pltpu.make_async_copy(own[a].at[me] if len(own[a].shape) == 3 else own[a], own_part[a], load_sems.at[a, 1])))
            stores.append(pltpu.make_async_copy(total[a], mine[a], store_sems.at[a]))
            sends.append(_remote(total[a], other[a], send_sems.at[a], recv_sems.at[a], (x, y, 1 - c)))
        for both in loads:
            for cp in both:
                cp.start()
        for a in range(n):
            for cp in loads[a]:
                cp.wait()
            for chip in range(4):
                @pl.when(me == chip)
                def _(a=a, chip=chip):
                    term = lambda j: (own_part[a] if j == chip else landed[a].at[j])[...].astype(F32)
                    total[a][...] = ((term(0) + term(1)) + term(2)) + term(3)
            stores[a].start()
            sends[a].start()
        for a in range(n):
            stores[a].wait()
            sends[a].wait()

    halves = [jax.ShapeDtypeStruct(p.shape[1:], F32) for p in parts]
    scratch = ([pltpu.VMEM(p.shape, p.dtype) for p in parts] + [pltpu.VMEM(p.shape[1:], o.dtype) for p, o in zip(parts, owns)]
               + [pltpu.VMEM(p.shape[1:], F32) for p in parts]
               + [pltpu.SemaphoreType.DMA((n, 2))] + [pltpu.SemaphoreType.DMA((n,))] * 3)
    res = pl.pallas_call(
        body, name=name, in_specs=[ANY] * (2 * n + len(after)), out_specs=[ANY] * (2 * n), out_shape=halves + halves,
        scratch_shapes=scratch, compiler_params=pltpu.CompilerParams(vmem_limit_bytes=VMEM_LIMIT),
    )(*parts, *owns, *after)
    return list(res[:n]), list(res[n:])


def _row_block(rows, cols, limit=1 << 20):
    best = 8
    for tr in range(8, rows + 1, 8):
        if rows % tr == 0 and tr * cols * 4 <= limit:
            best = tr
    return best


def _adamw(name, w, g_mine, g_other, m, v, c_arr):
    r, c = w.shape
    hr, cg = g_mine.shape
    tr = hr if r % hr == 0 and hr * cg * 4 <= (3 << 19) else math.gcd(_row_block(hr, cg), r)
    per_half = hr // tr
    bc1 = 1.0 - ADAM_B1 ** ADAM_STEP
    bc2 = 1.0 - ADAM_B2 ** ADAM_STEP

    def body(c_ref, w_ref, gm_ref, go_ref, m_ref, v_ref, go, do, mo, vo):
        gv = jnp.where(pl.program_id(0) // per_half == c_ref[0], gm_ref[:, 0:c], go_ref[:, 0:c])
        mn = ADAM_B1 * m_ref[...] + (1.0 - ADAM_B1) * gv
        vn = ADAM_B2 * v_ref[...] + (1.0 - ADAM_B2) * (gv * gv)
        go[...] = gv
        mo[...] = mn
        vo[...] = vn
        do[...] = -ADAM_LR * ((mn / bc1) / (jnp.sqrt(vn / bc2) + ADAM_EPS) + ADAM_WD * w_ref[...])

    blk = pl.BlockSpec((tr, c), lambda i, cr: (i, 0))
    gblk = pl.BlockSpec((tr, cg), lambda i, cr: (i % per_half, 0))
    grid_spec = pltpu.PrefetchScalarGridSpec(num_scalar_prefetch=1, grid=(r // tr,), in_specs=[blk, gblk, gblk, blk, blk],
                                             out_specs=[blk] * 4)
    return pl.pallas_call(body, name=f"adamw_{name}", grid_spec=grid_spec, out_shape=[jax.ShapeDtypeStruct((r, c), F32)] * 4,
                          compiler_params=_cparams())(c_arr, w, g_mine, g_other, m, v)


SMALL_Q = ("sgu_ln_g", "sgu_ln_b", "conv_b", "conv_ln_g", "conv_ln_b")
SMALL_D = ("ln1_g", "ln1_b", "ln2_g", "ln2_b")


def _adamw_small(g_mine, g_other, c_arr, me_arr, params):
    names = list(SMALL_Q) + list(SMALL_D) + ["w_s", "b_s", "conv_w"]
    pieces = len(g_mine)
    q = g_mine[0].shape[1] // 2
    heads = params["w_s"][0].shape[1]
    kw, _, cshard = params["conv_w"][0].shape
    bc1 = 1.0 - ADAM_B1 ** ADAM_STEP
    bc2 = 1.0 - ADAM_B2 ** ADAM_STEP

    def update(w, g, m, v):
        mn = ADAM_B1 * m + (1.0 - ADAM_B1) * g
        vn = ADAM_B2 * v + (1.0 - ADAM_B2) * (g * g)
        return g, -ADAM_LR * ((mn / bc1) / (jnp.sqrt(vn / bc2) + ADAM_EPS) + ADAM_WD * w), mn, vn

    def body(c_ref, me_ref, *refs):
        mine, other, refs = refs[:pieces], refs[pieces:2 * pieces], refs[2 * pieces:]
        ins = {nm: refs[3 * k:3 * k + 3] for k, nm in enumerate(names)}
        outs = {nm: refs[3 * len(names) + 4 * k:3 * len(names) + 4 * k + 4] for k, nm in enumerate(names)}
        loss_ref, cw_ref = refs[-2:]
        low = c_ref[0] == 0
        in_order = lambda first, second: [jnp.where(low, first, second), jnp.where(low, second, first)]
        g_all = jnp.concatenate([half for a in range(pieces) for half in in_order(mine[a][...], other[a][...])], axis=0)

        def apply(nm, g, at):
            w, m, v = (r[at] for r in ins[nm])
            for o, val in zip(outs[nm], update(w, g, m, v)):
                o[at] = val

        for row, nm in enumerate(SMALL_Q):
            apply(nm, g_all[SMALL_VQ + row:SMALL_VQ + row + 1, q:2 * q], ...)
        for row, nm in enumerate(SMALL_D):
            apply(nm, g_all[SMALL_VD + row:SMALL_VD + row + 1, :], ...)
        for h in range(heads):
            apply("w_s", g_all[0:CHUNK, h * CHUNK:(h + 1) * CHUNK], (0, h))
        apply("b_s", g_all[SMALL_BS:SMALL_BS + heads, q:q + LANES], 0)
        cw_ref[...] = jnp.zeros_like(cw_ref)
        for chip in range(4):
            @pl.when(me_ref[0] == chip)
            def _():
                cw_ref[...] = g_all[SMALL_CW:SMALL_CW + cw_ref.shape[0], chip * cshard:(chip + 1) * cshard]
        for tap in range(kw):
            apply("conv_w", cw_ref[tap:tap + 1, :], tap)
        loss_ref[...] = g_all[SMALL_LOSS:SMALL_LOSS + 1, q:q + 1]

    arrays = [a for nm in names for a in params[nm]]
    out_shape = [jax.ShapeDtypeStruct(params[nm][0].shape, F32) for nm in names for _ in range(4)] + [jax.ShapeDtypeStruct((1, 1), F32)]
    whole = lambda shape: pl.BlockSpec(shape, lambda i, c, me: (0,) * len(shape))
    grid_spec = pltpu.PrefetchScalarGridSpec(
        num_scalar_prefetch=2, grid=(1,),
        in_specs=[whole(a.shape) for a in [*g_mine, *g_other, *arrays]],
        out_specs=[whole(s.shape) for s in out_shape],
        scratch_shapes=[pltpu.VMEM((-(-kw // SUBLANES) * SUBLANES, cshard), F32)])
    res = pl.pallas_call(body, name="adamw_small", grid_spec=grid_spec, out_shape=out_shape, compiler_params=_cparams())(
        c_arr, me_arr, *g_mine, *g_other, *arrays)
    return {nm: list(res[4 * k:4 * k + 4]) for k, nm in enumerate(names)}, res[-1]


def _pad_rows(a, rows):
    return jnp.pad(a, ((0, rows - a.shape[0]), (0, 0)))


def kernel(x, w_in, sgu_ln_g, sgu_ln_b, w_s, b_s, conv_w, conv_b, conv_ln_g, conv_ln_b, w_out, ln1_g, ln1_b, w_gate, w_up, w_down, ln2_g, ln2_b, loss_target, m_w_in, m_sgu_ln_g, m_sgu_ln_b, m_w_s, m_b_s, m_conv_w, m_conv_b, m_conv_ln_g, m_conv_ln_b, m_w_out, m_ln1_g, m_ln1_b, m_w_gate, m_w_up, m_w_down, m_ln2_g, m_ln2_b, v_w_in, v_sgu_ln_g, v_sgu_ln_b, v_w_s, v_b_s, v_conv_w, v_conv_b, v_conv_ln_g, v_conv_ln_b, v_w_out, v_ln1_g, v_ln1_b, v_w_gate, v_w_up, v_w_down, v_ln2_g, v_ln2_b):
    depth, d, q = w_in.shape
    assert depth == 1 and x.shape[0] == 1
    t = x.shape[1]
    heads = w_s.shape[1]
    kw, cshard = conv_w.shape[1], conv_w.shape[2]
    fs = w_gate.shape[2]
    slabs = _hidden_slabs(4 * fs)
    n_pairs = q // LANES
    assert heads * HEAD_DIM == q and q % LANES == 0 and w_s.shape[2] == CHUNK and 4 * cshard == q and kw - 1 <= HALO
    alpha = (2.0 * depth) ** 0.25
    tm = min(512, t)
    assert t % tm == 0 and tm % CHUNK == 0
    x2, tgt = x[0], loss_target[0]
    mx, my, mc = _coords()
    me = 2 * mx + my
    c_arr = jnp.reshape(mc, (1,)).astype(jnp.int32)

    kwp = -(-kw // 16) * 16
    me_arr = jnp.reshape(me, (1,)).astype(jnp.int32)
    (wi, wo), (wg, wu, wd), cw4 = _prep_gather([w_in[0], w_out[0]], [w_gate[0].T, w_up[0].T, w_down[0]],
                                              jnp.transpose(conv_w, (1, 0, 2)), kwp)
    wo = wo.reshape(d, d)
    cw = jnp.transpose(cw4, (1, 0, 2)).reshape(kwp, q)
    cwf = _pad_rows(cw[:kw][::-1], kwp)
    tabs = {name: jnp.asarray(tab) for name, tab in _dft_tables(kw, kwp, q).items()}

    wm = jnp.where(jnp.tril(jnp.ones((CHUNK, CHUNK), bool)), w_s[0], 0.0)
    wst = wm.reshape(n_pairs, 2 * CHUNK, CHUNK).astype(BF16)
    wstt = jnp.transpose(wm, (0, 2, 1)).reshape(n_pairs, 2 * CHUNK, CHUNK).astype(BF16)
    bmat = jnp.repeat(b_s[0].T, HEAD_DIM, axis=1)
    vq = _pad_rows(jnp.concatenate([sgu_ln_g, sgu_ln_b, conv_b, conv_ln_g, conv_ln_b], axis=0), 8)
    vd = _pad_rows(jnp.concatenate([ln1_g, ln1_b, ln2_g, ln2_b], axis=0), 8)

    *saved, wg, wu, wd = _fwd_mix(x2, wi, wo, wst, bmat, cwf, tabs, vq, vd, [wg, wu, wd], alpha, tm)
    wg, wu, wd = (w.reshape(4 * fs, d) for w in (wg, wu, wd))
    *acts, x1b, dr2, loss_part, dg2, db2 = _fwd_mlp(saved[0], tgt, wg, wu, wd, vd, alpha, slabs, tm)
    mlp_grads = None
    for j, slab in enumerate(slabs):
        mlp_grads = _bwd_mlp_slab(j, slab, dr2, mlp_grads, x1b, acts[j], wg, wu, wd, alpha, tm)
    dx1 = mlp_grads[0]
    mlp_halves = [b.reshape(4, 2, fs // 2, d) for b in mlp_grads[1:]]
    mlp_sums = _pair_reduce("pair_reduce_mlp", mlp_halves[:3], mlp_halves[3:], c_arr, [BF16] * 3)
    mlp_started, token = _exchange_start("exchange_mlp_start", mlp_sums)
    grad_x, dwi, dwo, *small, dwi16, dwo16 = _bwd_mix(dx1, saved, wi, wo, wstt, cwf, tabs, vq, vd, (loss_part, dg2, db2), token, alpha, tm)
    mlp_sums, mlp_parts = _exchange_wait("exchange_mlp_wait", mlp_started, [dwo])

    by_halves = lambda b: b.reshape(4, 2, b.shape[1] // 2, b.shape[2])
    halves = [by_halves(dwi), by_halves(dwo.reshape(4, d // 4, d))] + [s.reshape(1, 2, s.shape[0] // 2, s.shape[1]) for s in small]
    travelling = [by_halves(dwi16), by_halves(dwo16.reshape(4, d // 4, d))] + halves[2:]
    *sums, ws_sum, rest_sum = _pair_reduce("pair_reduce_mix", halves, travelling, c_arr, [BF16, BF16, BF16, F32])
    sums += [ws_sum[0], rest_sum[0]]
    mix_started, token = _exchange_start("exchange_mix_start", sums)

    out, raw = {}, {}

    def finish(first, names, parts, sums, after):
        mine, other = _chip_reduce(f"chip_reduce_{first}", parts, sums, [after])
        for a, nm in enumerate(names):
            w_, m_, v_ = weights[nm]
            if nm in ("w_gate", "w_up"):
                raw[nm] = _adamw(nm, w_[0].T, mine[a], other[a], m_[0].T, v_[0].T, c_arr)
                out[nm] = [o.T for o in raw[nm]]
            else:
                raw[nm] = out[nm] = _adamw(nm, w_[0], mine[a], other[a], m_[0], v_[0], c_arr)
        return mine[len(names):], other[len(names):]

    weights = {"w_in": (w_in, m_w_in, v_w_in), "w_out": (w_out, m_w_out, v_w_out), "w_gate": (w_gate, m_w_gate, v_w_gate),
               "w_up": (w_up, m_w_up, v_w_up), "w_down": (w_down, m_w_down, v_w_down)}
    finish(2, ["w_gate", "w_up", "w_down"], mlp_parts, mlp_sums, token)
    sums, parts = _exchange_wait("exchange_mix_wait", mix_started, [raw[nm][1] for nm in ("w_gate", "w_up", "w_down")])
    small_mine, small_other = finish(5, ["w_in", "w_out"], parts, sums, parts[0])

    small_params = {
        "sgu_ln_g": (sgu_ln_g, m_sgu_ln_g, v_sgu_ln_g), "sgu_ln_b": (sgu_ln_b, m_sgu_ln_b, v_sgu_ln_b),
        "conv_b": (conv_b, m_conv_b, v_conv_b), "conv_ln_g": (conv_ln_g, m_conv_ln_g, v_conv_ln_g),
        "conv_ln_b": (conv_ln_b, m_conv_ln_b, v_conv_ln_b), "ln1_g": (ln1_g, m_ln1_g, v_ln1_g), "ln1_b": (ln1_b, m_ln1_b, v_ln1_b),
        "ln2_g": (ln2_g, m_ln2_g, v_ln2_g), "ln2_b": (ln2_b, m_ln2_b, v_ln2_b), "w_s": (w_s, m_w_s, v_w_s),
        "b_s": (b_s, m_b_s, v_b_s), "conv_w": tuple(jnp.transpose(a, (1, 0, 2)) for a in (conv_w, m_conv_w, v_conv_w))}
    small_out, loss_block = _adamw_small(small_mine, small_other, c_arr, me_arr, small_params)
    small_out["conv_w"] = [jnp.transpose(o, (1, 0, 2)) for o in small_out["conv_w"]]
    loss = loss_block.reshape(())
    names = ["w_in", "sgu_ln_g", "sgu_ln_b", "w_s", "b_s", "conv_w", "conv_b", "conv_ln_g", "conv_ln_b", "w_out",
             "ln1_g", "ln1_b", "w_gate", "w_up", "w_down", "ln2_g", "ln2_b"]
    result = [loss, grad_x[None]]
    for kind in range(4):
        for nm in names:
            result.append(out[nm][kind][None] if nm in out else small_out[nm][kind])
    return tuple(result)
```

```python
import math

import jax
import numpy as np
import jax.numpy as jnp
from jax import lax
from jax.experimental import pallas as pl
from jax.experimental.pallas import tpu as pltpu

F32 = jnp.float32
BF16 = jnp.bfloat16

LN_EPS = 1e-5
HEAD_DIM = 64
CHUNK = 128
HALO = 32
LANES = 128
MXU_N = 256
ADAM_LR, ADAM_B1, ADAM_B2, ADAM_EPS, ADAM_WD, ADAM_STEP = 0.001, 0.9, 0.999, 1e-08, 0.01, 10
VMEM_LIMIT = 63 * 1024 * 1024
MESH_AXES = ("x", "y", "c")
MESH_ID = pl.DeviceIdType.MESH


def _dot(a, b):
    return jnp.dot(a, b, preferred_element_type=F32)


def _dot_nt(a, b):
    return lax.dot_general(a, b, (((1,), (1,)), ((), ())), preferred_element_type=F32)


def _dot_tn(a, b):
    return lax.dot_general(a, b, (((0,), (0,)), ((), ())), preferred_element_type=F32)


def _sigmoid(v):
    return 1.0 / (1.0 + jnp.exp(-v))


def _gelu(v):
    cdf = 0.5 * (1.0 + lax.erf(v * (1.0 / math.sqrt(2.0))))
    pdf = jnp.exp(-0.5 * v * v) * (1.0 / math.sqrt(2.0 * math.pi))
    return v * cdf, cdf + v * pdf


def _ln_stats(v):
    mu = jnp.mean(v, axis=-1, keepdims=True)
    d = v - mu
    rstd = lax.rsqrt(jnp.mean(d * d, axis=-1, keepdims=True) + LN_EPS)
    return d * rstd, rstd


def _ln_bwd(dxhat, xhat, rstd):
    m1 = jnp.mean(dxhat, axis=-1, keepdims=True)
    m2 = jnp.mean(dxhat * xhat, axis=-1, keepdims=True)
    return rstd * (dxhat - m1 - xhat * m2)


def _colsum(v):
    return jnp.sum(v, axis=0, keepdims=True)


def _pair_lanes(v, nc, p):
    return jnp.concatenate([v[c * CHUNK:(c + 1) * CHUNK, p * LANES:(p + 1) * LANES] for c in range(nc)], axis=1)


def _unpair(parts, nc):
    rows = [jnp.concatenate([part[:, c * LANES:(c + 1) * LANES] for part in parts], axis=1) for c in range(nc)]
    return jnp.concatenate(rows, axis=0)


def _low_head(nc):
    lane = lax.broadcasted_iota(jnp.int32, (CHUNK, nc * LANES), 1)
    return (lane & (LANES - 1)) < HEAD_DIM


def _mix(wst_ref, v, nc, n_pairs):
    vb = v.astype(BF16)
    low = _low_head(nc)
    parts = []
    for p in range(n_pairs):
        r = _dot(wst_ref[p], _pair_lanes(vb, nc, p))
        parts.append(jnp.where(low, r[:CHUNK], r[CHUNK:]))
    return _unpair(parts, nc)


def _mix_wgrad(dm, vn, nc, n_pairs):
    low = _low_head(nc)
    vb = vn.astype(BF16)
    out = []
    for p in range(n_pairs):
        a = _pair_lanes(dm, nc, p)
        lhs = jnp.concatenate([jnp.where(low, a, 0.0), jnp.where(low, 0.0, a)], axis=0).astype(BF16)
        out.append(_dot_nt(lhs, _pair_lanes(vb, nc, p)))
    return out


SUBLANES = 8


CONV_BLOCK = 256
DFT_N = CONV_BLOCK + HALO
DFT_F = -(-(DFT_N // 2 + 1) // SUBLANES) * SUBLANES


def _terms(m, exact):
    hi = m.astype(np.float32).astype(BF16)
    lo = (m.astype(np.float32) - hi.astype(np.float32)).astype(BF16)
    return np.concatenate([hi, hi, lo] if exact else [hi], axis=1)


def _split(v, exact=False):
    hi = v.astype(BF16)
    if not exact:
        return hi
    lo = (v - hi.astype(F32)).astype(BF16)
    return jnp.concatenate([hi, lo, hi], axis=0)


def _dft_tables(kw, kwp, q):
    nf = DFT_N // 2 + 1
    ang = 2.0 * np.pi * np.arange(nf)[:, None] * np.arange(DFT_N)[None, :] / DFT_N
    fwd = np.zeros((2 * DFT_F, DFT_N))
    fwd[:nf], fwd[DFT_F:DFT_F + nf] = np.cos(ang), -np.sin(ang)
    weight = np.full((nf, 1), 2.0 / DFT_N)
    weight[0] = weight[-1] = 1.0 / DFT_N
    inv = np.zeros((DFT_N, 2 * DFT_F))
    inv[:, :nf], inv[:, DFT_F:DFT_F + nf] = (np.cos(ang) * weight).T, (-np.sin(ang) * weight).T
    inv_taps = np.zeros((kwp, 2 * DFT_F))
    inv_taps[:kw] = inv[kw - 1::-1][:kw]
    shift = np.zeros((2 * DFT_F, q), np.float32)
    shift[:nf], shift[DFT_F:DFT_F + nf] = np.cos(ang[:, HALO:HALO + 1]), -np.sin(ang[:, HALO:HALO + 1])
    return {"fwd": _terms(fwd, False), "fwd_halo": _terms(fwd[:, CONV_BLOCK:], False), "shift": shift,
            "inv_out": _terms(inv[HALO:HALO + CONV_BLOCK], False), "inv_in": _terms(inv[:CONV_BLOCK], False),
            "taps": _terms(fwd[:, :kwp], True), "inv_taps": _terms(inv_taps, True)}


def _cmul(a, b, conj_b=False):
    ar, ai, br, bi = a[:DFT_F], a[DFT_F:], b[:DFT_F], b[DFT_F:]
    if conj_b:
        return jnp.concatenate([ar * br + ai * bi, ai * br - ar * bi], axis=0)
    return jnp.concatenate([ar * br - ai * bi, ar * bi + ai * br], axis=0)


def _interleave(sub_tiles):
    waiting, live = list(sub_tiles), []
    while waiting or live:
        if waiting:
            live.append(waiting.pop(0))
        for g in list(live):
            try:
                next(g)
            except StopIteration:
                live.remove(g)


def _start_copies(sems, pairs, first=0):
    copies = [pltpu.make_async_copy(src, dst, sems.at[first + k]) for k, (src, dst) in enumerate(pairs)]
    for cp in copies:
        cp.start()
    return copies


def _cparams():
    return pltpu.CompilerParams(dimension_semantics=("arbitrary",), vmem_limit_bytes=VMEM_LIMIT)


def _full(shape):
    return pl.BlockSpec(shape, lambda i: (0,) * len(shape))


ANY = pl.BlockSpec(memory_space=pl.ANY)

VQ_SGU_G, VQ_SGU_B, VQ_CONV_B, VQ_CLN_G, VQ_CLN_B = range(5)
VD_LN1_G, VD_LN1_B, VD_LN2_G, VD_LN2_B = range(4)
RS_LN1, RS_SGU, RS_CONV = range(3)
RS_COLS = LANES


def _saved_widths(d, q):
    f32 = [d, q, q, q, q, q, RS_COLS]
    bf16 = [d, 2 * q, d, q]
    return f32, bf16


def _saved_views(f32_ref, bf16_ref, d, q):
    views = []
    for ref, widths in zip((f32_ref, bf16_ref), _saved_widths(d, q)):
        for k, w in enumerate(widths):
            views.append(ref.at[pl.ds(0, ref.shape[0]), pl.ds(sum(widths[:k]), w)])
    return views
def _fwd_mix(x, wi, wo, wst, bmat, cwf, tabs, vq, vd, mlp_w, alpha, tm):
    t, d = x.shape
    q = wi.shape[2]
    nc, n_pairs = CONV_BLOCK // CHUNK, q // LANES
    n = t // tm
    n_in, n_saved = 11, 3
    assert tm % CONV_BLOCK == 0

    def body(x_ref, wi_hbm, wo_hbm, wst_ref, bmat_ref, cwf_ref, fwd_ref, taps_ref, inv_ref, vq_ref, vd_ref, *rest):
        f32_ref, bf16_ref, hf_ref = rest[3:3 + n_saved]
        xh_ref, zu_ref, mg_ref, vhat_ref, gv_ref, yhat_ref, rs_ref, xb_ref, pag_ref, y_ref, vnb_ref = _saved_views(f32_ref, bf16_ref, d, q)
        gathered = rest[3 + n_saved:6 + n_saved]
        wi_v, wo_v, hb_ref, gf_ref, send_sems, recv_sems, copy_sems = rest[6 + n_saved:]
        step = pl.program_id(0)

        @pl.when(step == 0)
        def _():
            loads = _start_copies(copy_sems, [(wi_hbm, wi_v), (wo_hbm, wo_v)])
            _Gather(gathered, send_sems, recv_sems).start()
            hb_ref[...] = jnp.zeros_like(hb_ref)
            gf_ref[...] = _dot(taps_ref[...], _split(cwf_ref[...], True))
            for cp in loads:
                cp.wait()

        @pl.when(step == (3 * n) // 4)
        def _():
            _Gather(gathered, send_sems, recv_sems).forward()

        def sub_tile(b):
            rows = slice(b * CONV_BLOCK, (b + 1) * CONV_BLOCK)
            xv = x_ref[rows, :]
            xb = xv.astype(BF16)
            xb_ref[rows, :] = xb
            pu, pv, pa, pg = (_dot(xb, wi_v[j]) for j in range(4))
            yield
            pag_ref[rows, 0:q] = pa.astype(BF16)
            pag_ref[rows, q:2 * q] = pg.astype(BF16)
            zu, gu = _gelu(pu)
            zv, gv = _gelu(pv)
            vhat, rstd_v = _ln_stats(zv)
            vnb = (vhat * vq_ref[VQ_SGU_G:VQ_SGU_G + 1, :] + vq_ref[VQ_SGU_B:VQ_SGU_B + 1, :]).astype(BF16)
            hb_ref[HALO + b * CONV_BLOCK:HALO + (b + 1) * CONV_BLOCK, :] = pa * _sigmoid(pg)
            yield
            mixed = _mix(wst_ref, vnb, nc, n_pairs) + jnp.concatenate([bmat_ref[...]] * nc, axis=0)
            spectrum = _dot(fwd_ref[...], _split(hb_ref[b * CONV_BLOCK:b * CONV_BLOCK + DFT_N, :]))
            yield
            y_ref[rows, 0:q] = (zu * mixed).astype(BF16)
            zu_ref[rows, :] = zu
            mg_ref[rows, :] = mixed * gu
            vhat_ref[rows, :] = vhat
            gv_ref[rows, :] = gv
            vnb_ref[rows, :] = vnb
            hf_ref[b * 2 * DFT_F:(b + 1) * 2 * DFT_F, :] = spectrum
            product = _split(_cmul(gf_ref[...], spectrum))
            yield
            yc = _dot(inv_ref[...], product) + vq_ref[VQ_CONV_B:VQ_CONV_B + 1, :]
            yield
            yhat, rstd_c = _ln_stats(yc)
            yhat_ref[rows, :] = yhat
            yn = yhat * vq_ref[VQ_CLN_G:VQ_CLN_G + 1, :] + vq_ref[VQ_CLN_B:VQ_CLN_B + 1, :]
            y_ref[rows, q:2 * q] = (yn * _sigmoid(yn)).astype(BF16)
            yield
            r1 = alpha * xv + _dot(y_ref[rows, :], wo_v[...])
            yield
            xhat, rstd1 = _ln_stats(r1)
            xh_ref[rows, :] = xhat
            col = lax.broadcasted_iota(jnp.int32, (CONV_BLOCK, RS_COLS), 1)
            rs_ref[rows, :] = jnp.where(col == RS_LN1, rstd1, jnp.where(col == RS_SGU, rstd_v, jnp.where(col == RS_CONV, rstd_c, 0.0)))

        _interleave([sub_tile(b) for b in range(tm // CONV_BLOCK)])
        hb_ref[0:HALO, :] = hb_ref[tm:tm + HALO, :]

        @pl.when(step == n - 1)
        def _():
            _Gather(gathered, send_sems, recv_sems).finish()

    row = lambda w: pl.BlockSpec((tm, w), lambda i: (i, 0))
    widths = [(sum(w), dt) for w, dt in zip(_saved_widths(d, q), (F32, BF16))]
    small_ins = [wst, bmat, cwf, tabs["fwd"], tabs["taps"], tabs["inv_out"], vq, vd]
    return pl.pallas_call(
        body, name="fwd_mix", grid=(n,),
        in_specs=[row(d), ANY, ANY] + [_full(a.shape) for a in small_ins] + [ANY] * 3,
        out_specs=[row(w) for w, _ in widths] + [pl.BlockSpec((tm // CONV_BLOCK * 2 * DFT_F, q), lambda i: (i, 0))] + [ANY] * 3,
        out_shape=[jax.ShapeDtypeStruct((t, w), dt) for w, dt in widths] + [jax.ShapeDtypeStruct((t // CONV_BLOCK * 2 * DFT_F, q), F32)]
        + [jax.ShapeDtypeStruct(b.shape, b.dtype) for b in mlp_w],
        scratch_shapes=[pltpu.VMEM(wi.shape, BF16), pltpu.VMEM(wo.shape, BF16), pltpu.VMEM((HALO + tm, q), F32),
                        pltpu.VMEM((2 * DFT_F, q), F32)] + _gather_sems(3) + [pltpu.SemaphoreType.DMA((2,))],
        input_output_aliases={n_in + a: n_saved + a for a in range(3)},
        compiler_params=_cparams(),
    )(x, wi, wo, *small_ins, *mlp_w)


MLP_SLABS = 4


def _hidden_slabs(f):
    assert f % MXU_N == 0
    tiles = f // MXU_N
    sizes = [(tiles // MLP_SLABS + (1 if j < tiles % MLP_SLABS else 0)) * MXU_N for j in range(MLP_SLABS)]
    return [(sum(sizes[:j]), sz) for j, sz in enumerate(sizes) if sz]


def _fwd_mlp(saved_f32, tgt, wg, wu, wd, vd, alpha, slabs, tm):
    t, d = tgt.shape
    n = t // tm
    ns = len(slabs)
    half = tm // 2 if tm % 32 == 0 else tm

    def body(xh_ref, tgt_ref, wg_hbm, wu_hbm, wd_hbm, vd_ref, *rest):
        gp_refs = [r.at[pl.ds(0, tm), pl.ds(0, sz)] for r, (_, sz) in zip(rest[:ns], slabs)]
        up_refs = [r.at[pl.ds(0, tm), pl.ds(sz, sz)] for r, (_, sz) in zip(rest[:ns], slabs)]
        x1b_ref, dr2_ref, loss_ref, dg2_ref, db2_ref, wg_v, wu_v, wd_v, copy_sems = rest[ns:]

        @pl.when(pl.program_id(0) == 0)
        def _():
            loads = _start_copies(copy_sems, [(wg_hbm, wg_v), (wu_hbm, wu_v), (wd_hbm, wd_v)])
            loss_ref[...] = jnp.zeros_like(loss_ref)
            dg2_ref[...] = jnp.zeros_like(dg2_ref)
            db2_ref[...] = jnp.zeros_like(db2_ref)
            for cp in loads:
                cp.wait()

        g2 = vd_ref[VD_LN2_G:VD_LN2_G + 1, :]

        for r0 in range(0, tm, half):
            rows = slice(r0, r0 + half)
            x1 = xh_ref[rows, :] * vd_ref[VD_LN1_G:VD_LN1_G + 1, :] + vd_ref[VD_LN1_B:VD_LN1_B + 1, :]
            x1b = x1.astype(BF16)
            x1b_ref[rows, :] = x1b
            acc = alpha * x1
            for (off, sz), gp_ref, up_ref in zip(slabs, gp_refs, up_refs):
                gp = _dot_nt(x1b, wg_v[off:off + sz, :])
                up = _dot_nt(x1b, wu_v[off:off + sz, :])
                gp_ref[rows, :] = gp.astype(BF16)
                up_ref[rows, :] = up.astype(BF16)
                acc = acc + _dot((gp * _sigmoid(gp) * up).astype(BF16), wd_v[off:off + sz, :])
            xh2, rstd2 = _ln_stats(acc)
            err = xh2 * g2 + vd_ref[VD_LN2_B:VD_LN2_B + 1, :] - tgt_ref[rows, :]
            loss_ref[...] += _colsum(jnp.sum(err * err, axis=1, keepdims=True)) * (0.5 / d)
            dy = err * (1.0 / d)
            dg2_ref[...] += _colsum(dy * xh2)
            db2_ref[...] += _colsum(dy)
            dr2_ref[rows, :] = _ln_bwd(dy * g2, xh2, rstd2)

    row = lambda w: pl.BlockSpec((tm, w), lambda i: (i, 0))
    act = [2 * sz for _, sz in slabs]
    return pl.pallas_call(
        body, name="fwd_mlp", grid=(n,),
        in_specs=[row(d), row(d), ANY, ANY, ANY, _full(vd.shape)],
        out_specs=[row(sz) for sz in act] + [row(d), row(d), _full((8, LANES)), _full((1, d)), _full((1, d))],
        out_shape=[jax.ShapeDtypeStruct((t, sz), BF16) for sz in act]
        + [jax.ShapeDtypeStruct((t, d), BF16), jax.ShapeDtypeStruct((t, d), F32),
           jax.ShapeDtypeStruct((8, LANES), F32), jax.ShapeDtypeStruct((1, d), F32), jax.ShapeDtypeStruct((1, d), F32)],
        scratch_shapes=[pltpu.VMEM(wg.shape, BF16), pltpu.VMEM(wu.shape, BF16), pltpu.VMEM(wd.shape, BF16), pltpu.SemaphoreType.DMA((3,))],
        compiler_params=_cparams(),
    )(saved_f32, tgt, wg, wu, wd, vd)


def _bwd_mlp_slab(j, slab, dr2, prev, x1b, gate_up, wg, wu, wd, alpha, tm):
    t, d = dr2.shape
    off, sz = slab
    n = t // tm
    first = prev is None

    def body(*refs):
        if first:
            dr_ref, x1b_ref, gu_ref, wg_hbm, wu_hbm, wd_hbm = refs[:6]
        else:
            dr_ref, dxp_ref, x1b_ref, gu_ref, wg_hbm, wu_hbm, wd_hbm = refs[:7]
        dx_ref, dwg_hbm, dwu_hbm, dwd_hbm, dwg16_hbm, dwu16_hbm, dwd16_hbm, ag, au, ad, wg_v, wu_v, wd_v, copy_sems = refs[-14:]

        @pl.when(pl.program_id(0) == 0)
        def _():
            loads = _start_copies(copy_sems, [(src.at[pl.ds(off, sz)], dst) for src, dst in ((wg_hbm, wg_v), (wu_hbm, wu_v), (wd_hbm, wd_v))])
            ag[...] = jnp.zeros_like(ag)
            au[...] = jnp.zeros_like(au)
            ad[...] = jnp.zeros_like(ad)
            for cp in loads:
                cp.wait()

        dr = dr_ref[...]
        drb = dr.astype(BF16)
        x1b = x1b_ref[...]
        gpv = gu_ref[:, 0:sz].astype(F32)
        upv = gu_ref[:, sz:2 * sz].astype(F32)
        dh = _dot_nt(drb, wd_v[...])
        sg = _sigmoid(gpv)
        silu = gpv * sg
        ad[...] += _dot_tn((silu * upv).astype(BF16), drb)
        dgp = (dh * upv * (sg * (1.0 + gpv * (1.0 - sg)))).astype(BF16)
        dup = (dh * silu).astype(BF16)
        ag[...] += _dot_tn(dgp, x1b)
        au[...] += _dot_tn(dup, x1b)
        base = alpha * dr if first else dxp_ref[...]
        dx_ref[...] = base + _dot(dgp, wg_v[...]) + _dot(dup, wu_v[...])

        @pl.when(pl.program_id(0) == n - 1)
        def _():
            rows = pl.ds(off, sz)
            stores = _start_copies(copy_sems, [(ag, dwg_hbm.at[rows]), (au, dwu_hbm.at[rows]), (ad, dwd_hbm.at[rows])])
            for acc, stage in ((ag, wg_v), (au, wu_v), (ad, wd_v)):
                stage[...] = acc[...].astype(BF16)
            stores += _start_copies(copy_sems, [(wg_v, dwg16_hbm.at[rows]), (wu_v, dwu16_hbm.at[rows]), (wd_v, dwd16_hbm.at[rows])], first=3)
            for cp in stores:
                cp.wait()


    row = lambda w: pl.BlockSpec((tm, w), lambda i: (i, 0))
    ins = [dr2] + ([] if first else [prev[0]]) + [x1b, gate_up, wg, wu, wd] + ([] if first else list(prev[1:]))
    in_specs = [row(d)] + ([] if first else [row(d)]) + [row(d), row(2 * sz), ANY, ANY, ANY] + ([] if first else [ANY] * 6)
    return pl.pallas_call(
        body, name=f"bwd_mlp_{j}", grid=(n,),
        in_specs=in_specs,
        out_specs=[row(d)] + [ANY] * 6,
        out_shape=[jax.ShapeDtypeStruct((t, d), F32)] + [jax.ShapeDtypeStruct(wg.shape, F32)] * 3 + [jax.ShapeDtypeStruct(wg.shape, BF16)] * 3,
        scratch_shapes=[pltpu.VMEM((sz, d), F32)] * 3 + [pltpu.VMEM((sz, d), BF16)] * 3 + [pltpu.SemaphoreType.DMA((6,))],
        input_output_aliases={} if first else {7 + a: 1 + a for a in range(6)},
        compiler_params=_cparams(),
    )(*ins)


SMALL_VD = CHUNK
SMALL_CW = CHUNK + 8
SMALL_VQ = CHUNK + 8
SMALL_LOSS = CHUNK + 16
SMALL_BS = CHUNK + 24


def _small_rows(kwp):
    return -(-(SMALL_CW + max(kwp, 24 + SUBLANES)) // 16) * 16


def _bwd_mix(dx1, saved, wi, wo, wstt, cwf, tabs, vq, vd, mlp_small, token, alpha, tm):
    saved_f32, saved_bf16, hf_s = saved
    t, d = dx1.shape
    q = wi.shape[2]
    nc, n_pairs = CONV_BLOCK // CHUNK, q // LANES
    n = t // tm
    nb = tm // CONV_BLOCK
    assert tm % CONV_BLOCK == 0

    def body(dx1_ref, f32_ref, bf16_ref, hf_ref,
             wi_hbm, wo_hbm, wstt_ref, cwf_ref, fwd_ref, fwd_halo_ref, shift_ref, taps_ref, inv_ref, inv_taps_ref, vq_ref, vd_ref,
             loss_ref, dg2_ref, db2_ref, token_ref,
             gx_ref, dwi_hbm, dwo_hbm, small_ws_hbm, small_rest_hbm, dwi16_hbm, dwo16_hbm,
             wi_v, wo_v, awi, awo, dyb_ref, later_ref, dbm_ref, gf_ref, dgf_ref, small_ref, copy_sems):
        xh_ref, zu_ref, mg_ref, vhat_ref, gv_ref, yhat_ref, rs_ref, xb_ref, pag_ref, y_ref, vnb_ref = _saved_views(f32_ref, bf16_ref, d, q)
        i = pl.program_id(0)

        @pl.when(i == 0)
        def _():
            loads = _start_copies(copy_sems, [(wi_hbm, wi_v), (wo_hbm, wo_v)])
            for r in (awi, awo, small_ref, dbm_ref, dgf_ref, dyb_ref, later_ref):
                r[...] = jnp.zeros_like(r)
            gf_ref[...] = _dot(taps_ref[...], _split(cwf_ref[...], True))
            for cp in loads:
                cp.wait()

        dr1b_parts, dproj_parts = [None] * nb, [None] * nb

        def sub_tile(b):
            rows = slice(b * CONV_BLOCK, (b + 1) * CONV_BLOCK)
            dx1v = dx1_ref[rows, :]
            xh = xh_ref[rows, :]
            rsv = rs_ref[rows, :]
            small_ref[SMALL_VD + VD_LN1_G:SMALL_VD + VD_LN1_G + 1, :] += _colsum(dx1v * xh)
            small_ref[SMALL_VD + VD_LN1_B:SMALL_VD + VD_LN1_B + 1, :] += _colsum(dx1v)
            dr1 = _ln_bwd(dx1v * vd_ref[VD_LN1_G:VD_LN1_G + 1, :], xh, rsv[:, RS_LN1:RS_LN1 + 1])
            dr1b = dr1.astype(BF16)
            yield
            dy = _dot_nt(dr1b, wo_v[...])
            yield
            vhat = vhat_ref[rows, :]
            sgu_g = vq_ref[VQ_SGU_G:VQ_SGU_G + 1, :]
            doa = dy[:, 0:q]
            dm = doa * zu_ref[rows, :]
            dpu = (doa * mg_ref[rows, :]).astype(BF16)
            acc = dm[0:CHUNK]
            for c in range(1, nc):
                acc = acc + dm[c * CHUNK:(c + 1) * CHUNK]
            dbm_ref[...] += acc
            pa = pag_ref[rows, 0:q].astype(F32)
            sg = _sigmoid(pag_ref[rows, q:2 * q].astype(F32))
            yhat = yhat_ref[rows, :]
            cln_g = vq_ref[VQ_CLN_G:VQ_CLN_G + 1, :]
            yn = yhat * cln_g + vq_ref[VQ_CLN_B:VQ_CLN_B + 1, :]
            sy = _sigmoid(yn)
            dyn = dy[:, q:2 * q] * (sy * (1.0 + yn * (1.0 - sy)))
            small_ref[SMALL_VQ + VQ_CLN_G:SMALL_VQ + VQ_CLN_G + 1, q:2 * q] += _colsum(dyn * yhat)
            small_ref[SMALL_VQ + VQ_CLN_B:SMALL_VQ + VQ_CLN_B + 1, q:2 * q] += _colsum(dyn)
            dyc = _ln_bwd(dyn * cln_g, yhat, rsv[:, RS_CONV:RS_CONV + 1])
            small_ref[SMALL_VQ + VQ_CONV_B:SMALL_VQ + VQ_CONV_B + 1, q:2 * q] += _colsum(dyc)
            dyb_ref[b, 0:CONV_BLOCK, :] = dyc
            yield
            wgrads = _mix_wgrad(dm, vnb_ref[rows, :], nc, n_pairs)
            dvn = _mix(wstt_ref, dm, nc, n_pairs)
            own = _dot(fwd_ref[...], _split(dyb_ref[b]))
            with_later = own + _dot(fwd_halo_ref[...], _split(later_ref[...]))
            later_ref[...] = dyb_ref[b, 0:HALO, :]
            yield
            for p, g in enumerate(wgrads):
                for half in range(2):
                    small_ref[0:CHUNK, (2 * p + half) * CHUNK:(2 * p + half + 1) * CHUNK] += g[half * CHUNK:(half + 1) * CHUNK]
            small_ref[SMALL_VQ + VQ_SGU_G:SMALL_VQ + VQ_SGU_G + 1, q:2 * q] += _colsum(dvn * vhat)
            small_ref[SMALL_VQ + VQ_SGU_B:SMALL_VQ + VQ_SGU_B + 1, q:2 * q] += _colsum(dvn)
            dpv = (_ln_bwd(dvn * sgu_g, vhat, rsv[:, RS_SGU:RS_SGU + 1]) * gv_ref[rows, :]).astype(BF16)
            dgf_ref[...] += _cmul(_cmul(own, shift_ref[...]), hf_ref[b * 2 * DFT_F:(b + 1) * 2 * DFT_F, :], conj_b=True)
            product = _split(_cmul(with_later, gf_ref[...], conj_b=True))
            yield
            dh = _dot(inv_ref[...], product)
            yield
            da = (dh * sg).astype(BF16)
            dg = (dh * pa * (sg * (1.0 - sg))).astype(BF16)
            yield
            gx = alpha * dr1
            for dpj, wj in zip((dpu, dpv, da, dg), range(4)):
                gx = gx + _dot_nt(dpj, wi_v[wj])
            gx_ref[rows, :] = gx
            dr1b_parts[b], dproj_parts[b] = dr1b, (dpu, dpv, da, dg)

        _interleave([sub_tile(b) for b in reversed(range(nb))])

        awo[...] += _dot_tn(y_ref[...], jnp.concatenate(dr1b_parts, axis=0))
        xb = xb_ref[...]
        for j in range(4):
            awi[j] += _dot_tn(xb, jnp.concatenate([part[j] for part in dproj_parts], axis=0))

        @pl.when(i == n - 1)
        def _():
            stores = _start_copies(copy_sems, [(awi, dwi_hbm), (awo, dwo_hbm)])
            wi_v[...] = awi[...].astype(BF16)
            wo_v[...] = awo[...].astype(BF16)
            stores += _start_copies(copy_sems, [(wi_v, dwi16_hbm), (wo_v, dwo16_hbm)], first=3)
            lane = lax.broadcasted_iota(jnp.int32, (CHUNK, LANES), 1)
            low = lane < HEAD_DIM
            dbs = jnp.zeros((CHUNK, LANES), F32)
            for p in range(n_pairs):
                grp = dbm_ref[:, p * LANES:(p + 1) * LANES]
                dbs = jnp.where(lane == 2 * p, jnp.sum(jnp.where(low, grp, 0.0), axis=1, keepdims=True), dbs)
                dbs = jnp.where(lane == 2 * p + 1, jnp.sum(jnp.where(low, 0.0, grp), axis=1, keepdims=True), dbs)
            tril = lax.broadcasted_iota(jnp.int32, (CHUNK, CHUNK), 0) >= lax.broadcasted_iota(jnp.int32, (CHUNK, CHUNK), 1)
            for h in range(2 * n_pairs):
                block = small_ref[0:CHUNK, h * CHUNK:(h + 1) * CHUNK]
                small_ref[0:CHUNK, h * CHUNK:(h + 1) * CHUNK] = jnp.where(tril, block, 0.0)
            small_ref[SMALL_VD + VD_LN2_G:SMALL_VD + VD_LN2_G + 1, :] = dg2_ref[...]
            small_ref[SMALL_VD + VD_LN2_B:SMALL_VD + VD_LN2_B + 1, :] = db2_ref[...]
            small_ref[SMALL_CW:SMALL_CW + kwp, 0:q] = _dot(inv_taps_ref[...], _split(dgf_ref[...], True))
            small_ref[SMALL_LOSS:SMALL_LOSS + SUBLANES, q:q + LANES] = loss_ref[...]
            small_ref[SMALL_BS:SMALL_BS + SUBLANES, q:q + LANES] = jnp.transpose(dbs)[0:SUBLANES]
            stores += _start_copies(copy_sems, [(small_ref.at[pl.ds(0, CHUNK)], small_ws_hbm)], first=2)
            stores += _start_copies(copy_sems, [(small_ref.at[pl.ds(CHUNK, small.shape[0] - CHUNK)], small_rest_hbm)], first=5)
            for cp in stores:
                cp.wait()

    rev = lambda w: pl.BlockSpec((tm, w), lambda i: (n - 1 - i, 0))
    kwp = cwf.shape[0]
    small = jax.ShapeDtypeStruct((_small_rows(kwp), 2 * q), F32)
    small_ins = [wstt, cwf, tabs["fwd"], tabs["fwd_halo"], tabs["shift"], tabs["taps"], tabs["inv_in"], tabs["inv_taps"], vq, vd,
                 *mlp_small]
    return pl.pallas_call(
        body, name="bwd_mix", grid=(n,),
        in_specs=[rev(d), rev(saved_f32.shape[1]), rev(saved_bf16.shape[1]),
                  pl.BlockSpec((nb * 2 * DFT_F, q), lambda i: (n - 1 - i, 0)), ANY, ANY] + [_full(a.shape) for a in small_ins] + [ANY],
        out_specs=[rev(d)] + [ANY] * 6,
        out_shape=[jax.ShapeDtypeStruct((t, d), F32), jax.ShapeDtypeStruct(wi.shape, F32), jax.ShapeDtypeStruct(wo.shape, F32),
                   jax.ShapeDtypeStruct((CHUNK, 2 * q), F32), jax.ShapeDtypeStruct((small.shape[0] - CHUNK, 2 * q), F32),
                   jax.ShapeDtypeStruct(wi.shape, BF16), jax.ShapeDtypeStruct(wo.shape, BF16)],
        scratch_shapes=[pltpu.VMEM(wi.shape, BF16), pltpu.VMEM(wo.shape, BF16), pltpu.VMEM(wi.shape, F32), pltpu.VMEM(wo.shape, F32),
                        pltpu.VMEM((nb, DFT_N, q), F32), pltpu.VMEM((HALO, q), F32),
                        pltpu.VMEM((CHUNK, q), F32), pltpu.VMEM((2 * DFT_F, q), F32), pltpu.VMEM((2 * DFT_F, q), F32),
                        pltpu.VMEM(small.shape, F32), pltpu.SemaphoreType.DMA((6,))],
        compiler_params=_cparams(),
    )(dx1, saved_f32, saved_bf16, hf_s, wi, wo, *small_ins, token)


def _coords():
    return tuple(lax.axis_index(a) for a in MESH_AXES)


def _other_chips(x, y):
    return [(1 - x, y), (x, 1 - y), (1 - x, 1 - y)]


def _remote(src, dst, send_sem, recv_sem, to):
    return pltpu.make_async_remote_copy(src_ref=src, dst_ref=dst, send_sem=send_sem, recv_sem=recv_sem,
                                        device_id=to, device_id_type=MESH_ID)


class _Gather:
    def __init__(self, bufs, send_sems, recv_sems, own=None):
        self.bufs, self.send_sems, self.recv_sems, self.own = bufs, send_sems, recv_sems, own
        self.x, self.y, self.c = _coords()

    def _copies(self, stage):
        x, y, c = self.x, self.y, self.c
        for a, buf in enumerate(self.bufs):
            hr = buf.shape[1] // 2
            for j, chip in enumerate(_other_chips(x, y)):
                if stage == "ici_out":
                    ref, k, to = buf.at[2 * x + y, pl.ds(c * hr, hr)], j, (*chip, c)
                    if self.own:
                        yield _remote(self.own[a].at[pl.ds(c * hr, hr)], ref, self.send_sems.at[a, k], self.recv_sems.at[a, k], to)
                        continue
                elif stage == "ici_in":
                    ref, k, to = buf.at[2 * chip[0] + chip[1], pl.ds(c * hr, hr)], j, (*chip, c)
                elif stage == "d2d_out":
                    ref, k, to = buf.at[2 * chip[0] + chip[1], pl.ds(c * hr, hr)], 3 + j, (x, y, 1 - c)
                else:
                    ref, k, to = buf.at[2 * chip[0] + chip[1], pl.ds((1 - c) * hr, hr)], 3 + j, (x, y, 1 - c)
                yield _remote(ref, ref, self.send_sems.at[a, k], self.recv_sems.at[a, k], to)

    def start(self):
        for cp in self._copies("ici_out"):
            cp.start()

    def forward(self):
        for landed, onward in zip(self._copies("ici_in"), self._copies("d2d_out")):
            landed.wait_recv()
            onward.start()

    def finish(self):
        for cp in self._copies("d2d_in"):
            cp.wait_recv()
        for stage in ("ici_out", "d2d_out"):
            for cp in self._copies(stage):
                cp.wait_send()


def _gather_sems(n):
    return [pltpu.SemaphoreType.DMA((n, 6)), pltpu.SemaphoreType.DMA((n, 6))]


def _prep_gather(gathered, local, conv_w, kwp):
    kw, _, cshard = conv_w.shape
    shards = list(gathered) + list(local)
    n, n_g = len(shards), len(gathered)

    def body(*refs):
        src, cw_ref, out = refs[:n], refs[n], refs[n + 1:2 * n + 2]
        wide, narrow, taps = refs[2 * n + 2:3 * n + 2], refs[3 * n + 2:4 * n + 2], refs[4 * n + 2]
        load_sems, store_sems, send_sems, recv_sems = refs[4 * n + 3:]
        x, y, _ = _coords()
        loads = [pltpu.make_async_copy(src[a], wide[a], load_sems.at[a]) for a in range(n)]
        stores = [pltpu.make_async_copy(narrow[a], out[a].at[2 * x + y], store_sems.at[a]) for a in range(n)]
        stores.append(pltpu.make_async_copy(taps, out[n].at[2 * x + y], store_sems.at[n]))
        for cp in loads[:n_g]:
            cp.start()

        def cast(a):
            loads[a].wait()
            narrow[a][...] = wide[a][...].astype(BF16)
            stores[a].start()

        for a in range(n_g):
            cast(a)
        for tap in range(kw):
            taps[tap:tap + 1, :] = cw_ref[tap]
        taps[kw:kwp, :] = jnp.zeros((kwp - kw, cshard), F32)
        stores[n].start()
        g = _Gather(list(out[:n_g]) + [out[n]], send_sems, recv_sems, own=list(narrow[:n_g]) + [taps])
        g.start()
        for cp in loads[n_g:]:
            cp.start()
        for a in range(n_g, n):
            cast(a)
        g.forward()
        g.finish()
        for cp in stores:
            cp.wait()

    out_shape = [jax.ShapeDtypeStruct((4,) + a.shape, BF16) for a in shards] + [jax.ShapeDtypeStruct((4, kwp, cshard), F32)]
    scratch = ([pltpu.VMEM(a.shape, F32) for a in shards] + [pltpu.VMEM(a.shape, BF16) for a in shards] + [pltpu.VMEM((kwp, cshard), F32)]
               + [pltpu.SemaphoreType.DMA((n,)), pltpu.SemaphoreType.DMA((n + 1,))] + _gather_sems(n_g + 1))
    res = pl.pallas_call(
        body, name="prep_gather", in_specs=[ANY] * n + [pl.BlockSpec(memory_space=pltpu.VMEM)], out_specs=[ANY] * (n + 1),
        out_shape=out_shape, scratch_shapes=scratch, compiler_params=pltpu.CompilerParams(vmem_limit_bytes=VMEM_LIMIT),
    )(*shards, conv_w)
    return list(res[:n_g]), list(res[n_g:n]), res[n]


def _pair_reduce(name, partials, payloads, c_arr, out_dtypes):
    n = len(partials)
    counts = [g.shape[0] for g in partials]
    first = [sum(counts[:a]) for a in range(n)]
    steps = sum(counts)

    def body(c_ref, *refs):
        own, travelling, out, land = (refs[k * n:(k + 1) * n] for k in range(4))
        send_sems, recv_sems = refs[4 * n:]
        i = pl.program_id(0)
        x, y, c = _coords()

        def copy(a, q):
            return _remote(travelling[a].at[q, 1 - c], land[a].at[q], send_sems.at[first[a] + q], recv_sems.at[first[a] + q],
                           (x, y, 1 - c))

        blocks = [(a, q) for a in range(n) for q in range(counts[a])]

        @pl.when(i == 0)
        def _():
            for a, q in blocks:
                copy(a, q).start()

        for a in range(n):
            @pl.when((i >= first[a]) & (i < first[a] + counts[a]))
            def _(a=a):
                q = i - first[a]
                copy(a, q).wait_recv()
                out[a][...] = (own[a][...] + land[a][q].astype(F32)).astype(out_dtypes[a])

        @pl.when(i == steps - 1)
        def _():
            for a, q in blocks:
                copy(a, q).wait_send()

    at = lambda a, i: jnp.clip(i - first[a], 0, counts[a] - 1)
    in_specs = [pl.BlockSpec((None, None) + g.shape[2:], lambda i, cr, a=a: (at(a, i), cr[0], 0, 0)) for a, g in enumerate(partials)]
    out_specs = [pl.BlockSpec((None,) + g.shape[2:], lambda i, cr, a=a: (at(a, i), 0, 0)) for a, g in enumerate(partials)]
    grid_spec = pltpu.PrefetchScalarGridSpec(
        num_scalar_prefetch=1, grid=(steps,), in_specs=in_specs + [ANY] * n, out_specs=out_specs,
        scratch_shapes=[pltpu.VMEM((g.shape[0],) + g.shape[2:], p.dtype) for g, p in zip(partials, payloads)]
        + [pltpu.SemaphoreType.DMA((steps,)), pltpu.SemaphoreType.DMA((steps,))])
    out_shape = [jax.ShapeDtypeStruct((g.shape[0],) + g.shape[2:], dt) for g, dt in zip(partials, out_dtypes)]
    return list(pl.pallas_call(body, name=name, grid_spec=grid_spec, out_shape=out_shape, compiler_params=_cparams())(
        c_arr, *partials, *payloads))


class _Exchange:
    def __init__(self, src, dst, send_sems, recv_sems):
        self.src, self.dst, self.send_sems, self.recv_sems = src, dst, send_sems, recv_sems
        self.x, self.y, self.c = _coords()

    def _copies(self, incoming):
        x, y, c = self.x, self.y, self.c
        for a, (s, d) in enumerate(zip(self.src, self.dst)):
            for j, chip in enumerate(_other_chips(x, y)):
                slot = 2 * chip[0] + chip[1]
                if incoming:
                    out, into = d.at[slot], d.at[slot]
                else:
                    out, into = (s.at[slot] if len(s.shape) == 3 else s), d.at[2 * x + y]
                yield _remote(out, into, self.send_sems.at[a, j], self.recv_sems.at[a, j], (*chip, c))

    def start(self):
        for cp in self._copies(False):
            cp.start()

    def finish(self):
        for cp in self._copies(True):
            cp.wait_recv()
        for cp in self._copies(False):
            cp.wait_send()


def _exchange_shapes(arrs):
    return [jax.ShapeDtypeStruct((4,) + s.shape[-2:], s.dtype) for s in arrs]


class _FlatSems:
    def __init__(self, ref):
        self.ref = ref

    @property
    def at(self):
        return self

    def __getitem__(self, idx):
        return self.ref.at[3 * idx[0] + idx[1]]


HBM = pl.BlockSpec(memory_space=pltpu.HBM)
SEM = pl.BlockSpec(memory_space=pltpu.SEMAPHORE)
DATAFLOW = pltpu.SideEffectType.DATAFLOW_SIDE_EFFECTING


def _exchange_start(name, arrs):
    n = len(arrs)
    lands = _exchange_shapes(arrs)

    def body(*refs):
        src, land = refs[:n], refs[n:2 * n]
        send_sems, recv_sems = refs[2 * n:2 * n + 2]
        token = refs[-1]
        _Exchange(src, land, _FlatSems(send_sems), _FlatSems(recv_sems)).start()
        token[...] = jnp.zeros_like(token)

    hbm = lambda a: pltpu.with_memory_space_constraint(a, pltpu.HBM)
    outs = pl.pallas_call(
        body, name=name,
        out_shape=(pltpu.SemaphoreType.DMA((3 * n,)), pltpu.SemaphoreType.DMA((3 * n,)),
                   *[pltpu.HBM(a.shape, a.dtype) for a in arrs], *[pltpu.HBM(s.shape, s.dtype) for s in lands],
                   jax.ShapeDtypeStruct((SUBLANES, LANES), F32)),
        in_specs=[HBM] * (2 * n), out_specs=(SEM, SEM, *[HBM] * (2 * n), pl.BlockSpec(memory_space=pltpu.VMEM)),
        input_output_aliases={a: 2 + a for a in range(2 * n)},
        compiler_params=pltpu.CompilerParams(has_side_effects=DATAFLOW),
    )(*[hbm(a) for a in arrs], *[hbm(lax.empty(s.shape, s.dtype)) for s in lands])
    return outs[:-1], outs[-1]


def _exchange_wait(name, started, after):
    send_sems, recv_sems, *bufs = started
    n = len(bufs) // 2

    def body(*refs):
        src, land = refs[:n], refs[n:2 * n]
        send_sems, recv_sems = refs[2 * n:2 * n + 2]
        _Exchange(src, land, _FlatSems(send_sems), _FlatSems(recv_sems)).finish()

    outs = pl.pallas_call(
        body, name=name,
        out_shape=tuple(pltpu.HBM(b.shape, b.dtype) for b in bufs),
        in_specs=[HBM] * (2 * n) + [SEM, SEM] + [ANY] * len(after), out_specs=tuple([HBM] * (2 * n)),
        input_output_aliases={a: a for a in range(2 * n)},
        compiler_params=pltpu.CompilerParams(has_side_effects=DATAFLOW),
    )(*bufs, send_sems, recv_sems, *after)
    return list(outs[:n]), list(outs[n:])


def _chip_reduce(name, parts, owns, after):
    n = len(parts)

    def body(*refs):
        part, own = refs[:n], refs[n:2 * n]
        mine, other = refs[2 * n + len(after):3 * n + len(after)], refs[3 * n + len(after):4 * n + len(after)]
        scratch = refs[4 * n + len(after):]
        landed, own_part, total = scratch[:n], scratch[n:2 * n], scratch[2 * n:3 * n]
        load_sems, store_sems, send_sems, recv_sems = scratch[3 * n:]
        x, y, c = _coords()
        me = 2 * x + y
        loads, stores, sends = [], [], []
        for a in range(n):
            loads.append((pltpu.make_async_copy(part[a], landed[a], load_sems.at[a, 0]),
                          pltpu.make_async_copy(own[a].at[me] if len(own[a].shape) == 3 else own[a], own_part[a], load_sems.at[a, 1])))
            stores.append(pltpu.make_async_copy(total[a], mine[a], store_sems.at[a]))
            sends.append(_remote(total[a], other[a], send_sems.at[a], recv_sems.at[a], (x, y, 1 - c)))
        for both in loads:
            for cp in both:
                cp.start()
        for a in range(n):
            for cp in loads[a]:
                cp.wait()
            for chip in range(4):
                @pl.when(me == chip)
                def _(a=a, chip=chip):
                    term = lambda j: (own_part[a] if j == chip else landed[a].at[j])[...].astype(F32)
                    total[a][...] = ((term(0) + term(1)) + term(2)) + term(3)
            stores[a].start()
            sends[a].start()
        for a in range(n):
            stores[a].wait()
            sends[a].wait()

    halves = [jax.ShapeDtypeStruct(p.shape[1:], F32) for p in parts]
    scratch = ([pltpu.VMEM(p.shape, p.dtype) for p in parts] + [pltpu.VMEM(p.shape[1:], o.dtype) for p, o in zip(parts, owns)]
               + [pltpu.VMEM(p.shape[1:], F32) for p in parts]
               + [pltpu.SemaphoreType.DMA((n, 2))] + [pltpu.SemaphoreType.DMA((n,))] * 3)
    res = pl.pallas_call(
        body, name=name, in_specs=[ANY] * (2 * n + len(after)), out_specs=[ANY] * (2 * n), out_shape=halves + halves,
        scratch_shapes=scratch, compiler_params=pltpu.CompilerParams(vmem_limit_bytes=VMEM_LIMIT),
    )(*parts, *owns, *after)
    return list(res[:n]), list(res[n:])


def _row_block(rows, cols, limit=1 << 20):
    best = 8
    for tr in range(8, rows + 1, 8):
        if rows % tr == 0 and tr * cols * 4 <= limit:
            best = tr
    return best


def _adamw(name, w, g_mine, g_other, m, v, c_arr):
    r, c = w.shape
    hr, cg = g_mine.shape
    tr = hr if r % hr == 0 and hr * cg * 4 <= (3 << 18) else math.gcd(_row_block(hr, cg, 3 << 18), r)
    per_half = hr // tr
    bc1 = 1.0 - ADAM_B1 ** ADAM_STEP
    bc2 = 1.0 - ADAM_B2 ** ADAM_STEP

    def body(c_ref, w_ref, gm_ref, go_ref, m_ref, v_ref, go, do, mo, vo):
        gv = jnp.where(pl.program_id(0) // per_half == c_ref[0], gm_ref[:, 0:c], go_ref[:, 0:c])
        mn = ADAM_B1 * m_ref[...] + (1.0 - ADAM_B1) * gv
        vn = ADAM_B2 * v_ref[...] + (1.0 - ADAM_B2) * (gv * gv)
        go[...] = gv
        mo[...] = mn
        vo[...] = vn
        do[...] = -ADAM_LR * ((mn / bc1) / (jnp.sqrt(vn / bc2) + ADAM_EPS) + ADAM_WD * w_ref[...])

    blk = pl.BlockSpec((tr, c), lambda i, cr: (i, 0))
    gblk = pl.BlockSpec((tr, cg), lambda i, cr: (i % per_half, 0))
    grid_spec = pltpu.PrefetchScalarGridSpec(num_scalar_prefetch=1, grid=(r // tr,), in_specs=[blk, gblk, gblk, blk, blk],
                                             out_specs=[blk] * 4)
    return pl.pallas_call(body, name=f"adamw_{name}", grid_spec=grid_spec, out_shape=[jax.ShapeDtypeStruct((r, c), F32)] * 4,
                          compiler_params=_cparams())(c_arr, w, g_mine, g_other, m, v)


SMALL_Q = ("sgu_ln_g", "sgu_ln_b", "conv_b", "conv_ln_g", "conv_ln_b")
SMALL_D = ("ln1_g", "ln1_b", "ln2_g", "ln2_b")


def _adamw_small(g_mine, g_other, c_arr, me_arr, params):
    names = list(SMALL_Q) + list(SMALL_D) + ["w_s", "b_s", "conv_w"]
    pieces = len(g_mine)
    q = g_mine[0].shape[1] // 2
    heads = params["w_s"][0].shape[1]
    kw, _, cshard = params["conv_w"][0].shape
    bc1 = 1.0 - ADAM_B1 ** ADAM_STEP
    bc2 = 1.0 - ADAM_B2 ** ADAM_STEP

    def update(w, g, m, v):
        mn = ADAM_B1 * m + (1.0 - ADAM_B1) * g
        vn = ADAM_B2 * v + (1.0 - ADAM_B2) * (g * g)
        return g, -ADAM_LR * ((mn / bc1) / (jnp.sqrt(vn / bc2) + ADAM_EPS) + ADAM_WD * w), mn, vn

    def body(c_ref, me_ref, *refs):
        mine, other, refs = refs[:pieces], refs[pieces:2 * pieces], refs[2 * pieces:]
        ins = {nm: refs[3 * k:3 * k + 3] for k, nm in enumerate(names)}
        outs = {nm: refs[3 * len(names) + 4 * k:3 * len(names) + 4 * k + 4] for k, nm in enumerate(names)}
        loss_ref, cw_ref = refs[-2:]
        low = c_ref[0] == 0
        in_order = lambda first, second: [jnp.where(low, first, second), jnp.where(low, second, first)]
        g_all = jnp.concatenate([half for a in range(pieces) for half in in_order(mine[a][...], other[a][...])], axis=0)

        def apply(nm, g, at):
            w, m, v = (r[at] for r in ins[nm])
            for o, val in zip(outs[nm], update(w, g, m, v)):
                o[at] = val

        for row, nm in enumerate(SMALL_Q):
            apply(nm, g_all[SMALL_VQ + row:SMALL_VQ + row + 1, q:2 * q], ...)
        for row, nm in enumerate(SMALL_D):
            apply(nm, g_all[SMALL_VD + row:SMALL_VD + row + 1, :], ...)
        for h in range(heads):
            apply("w_s", g_all[0:CHUNK, h * CHUNK:(h + 1) * CHUNK], (0, h))
        apply("b_s", g_all[SMALL_BS:SMALL_BS + heads, q:q + LANES], 0)
        cw_ref[...] = jnp.zeros_like(cw_ref)
        for chip in range(4):
            @pl.when(me_ref[0] == chip)
            def _():
                cw_ref[...] = g_all[SMALL_CW:SMALL_CW + cw_ref.shape[0], chip * cshard:(chip + 1) * cshard]
        for tap in range(kw):
            apply("conv_w", cw_ref[tap:tap + 1, :], tap)
        loss_ref[...] = g_all[SMALL_LOSS:SMALL_LOSS + 1, q:q + 1]

    arrays = [a for nm in names for a in params[nm]]
    out_shape = [jax.ShapeDtypeStruct(params[nm][0].shape, F32) for nm in names for _ in range(4)] + [jax.ShapeDtypeStruct((1, 1), F32)]
    whole = lambda shape: pl.BlockSpec(shape, lambda i, c, me: (0,) * len(shape))
    grid_spec = pltpu.PrefetchScalarGridSpec(
        num_scalar_prefetch=2, grid=(1,),
        in_specs=[whole(a.shape) for a in [*g_mine, *g_other, *arrays]],
        out_specs=[whole(s.shape) for s in out_shape],
        scratch_shapes=[pltpu.VMEM((-(-kw // SUBLANES) * SUBLANES, cshard), F32)])
    res = pl.pallas_call(body, name="adamw_small", grid_spec=grid_spec, out_shape=out_shape, compiler_params=_cparams())(
        c_arr, me_arr, *g_mine, *g_other, *arrays)
    return {nm: list(res[4 * k:4 * k + 4]) for k, nm in enumerate(names)}, res[-1]


def _pad_rows(a, rows):
    return jnp.pad(a, ((0, rows - a.shape[0]), (0, 0)))


def kernel(x, w_in, sgu_ln_g, sgu_ln_b, w_s, b_s, conv_w, conv_b, conv_ln_g, conv_ln_b, w_out, ln1_g, ln1_b, w_gate, w_up, w_down, ln2_g, ln2_b, loss_target, m_w_in, m_sgu_ln_g, m_sgu_ln_b, m_w_s, m_b_s, m_conv_w, m_conv_b, m_conv_ln_g, m_conv_ln_b, m_w_out, m_ln1_g, m_ln1_b, m_w_gate, m_w_up, m_w_down, m_ln2_g, m_ln2_b, v_w_in, v_sgu_ln_g, v_sgu_ln_b, v_w_s, v_b_s, v_conv_w, v_conv_b, v_conv_ln_g, v_conv_ln_b, v_w_out, v_ln1_g, v_ln1_b, v_w_gate, v_w_up, v_w_down, v_ln2_g, v_ln2_b):
    depth, d, q = w_in.shape
    assert depth == 1 and x.shape[0] == 1
    t = x.shape[1]
    heads = w_s.shape[1]
    kw, cshard = conv_w.shape[1], conv_w.shape[2]
    fs = w_gate.shape[2]
    slabs = _hidden_slabs(4 * fs)
    n_pairs = q // LANES
    assert heads * HEAD_DIM == q and q % LANES == 0 and w_s.shape[2] == CHUNK and 4 * cshard == q and kw - 1 <= HALO
    alpha = (2.0 * depth) ** 0.25
    tm = min(512, t)
    assert t % tm == 0 and tm % CHUNK == 0
    x2, tgt = x[0], loss_target[0]
    mx, my, mc = _coords()
    me = 2 * mx + my
    c_arr = jnp.reshape(mc, (1,)).astype(jnp.int32)

    kwp = -(-kw // 16) * 16
    me_arr = jnp.reshape(me, (1,)).astype(jnp.int32)
    (wi, wo), (wg, wu, wd), cw4 = _prep_gather([w_in[0], w_out[0]], [w_gate[0].T, w_up[0].T, w_down[0]],
                                              jnp.transpose(conv_w, (1, 0, 2)), kwp)
    wo = wo.reshape(d, d)
    cw = jnp.transpose(cw4, (1, 0, 2)).reshape(kwp, q)
    cwf = _pad_rows(cw[:kw][::-1], kwp)
    tabs = {name: jnp.asarray(tab) for name, tab in _dft_tables(kw, kwp, q).items()}

    wm = jnp.where(jnp.tril(jnp.ones((CHUNK, CHUNK), bool)), w_s[0], 0.0)
    wst = wm.reshape(n_pairs, 2 * CHUNK, CHUNK).astype(BF16)
    wstt = jnp.transpose(wm, (0, 2, 1)).reshape(n_pairs, 2 * CHUNK, CHUNK).astype(BF16)
    bmat = jnp.repeat(b_s[0].T, HEAD_DIM, axis=1)
    vq = _pad_rows(jnp.concatenate([sgu_ln_g, sgu_ln_b, conv_b, conv_ln_g, conv_ln_b], axis=0), 8)
    vd = _pad_rows(jnp.concatenate([ln1_g, ln1_b, ln2_g, ln2_b], axis=0), 8)

    *saved, wg, wu, wd = _fwd_mix(x2, wi, wo, wst, bmat, cwf, tabs, vq, vd, [wg, wu, wd], alpha, tm)
    wg, wu, wd = (w.reshape(4 * fs, d) for w in (wg, wu, wd))
    *acts, x1b, dr2, loss_part, dg2, db2 = _fwd_mlp(saved[0], tgt, wg, wu, wd, vd, alpha, slabs, tm)
    mlp_grads = None
    for j, slab in enumerate(slabs):
        mlp_grads = _bwd_mlp_slab(j, slab, dr2, mlp_grads, x1b, acts[j], wg, wu, wd, alpha, tm)
    dx1 = mlp_grads[0]
    mlp_halves = [b.reshape(4, 2, fs // 2, d) for b in mlp_grads[1:]]
    mlp_sums = _pair_reduce("pair_reduce_mlp", mlp_halves[:3], mlp_halves[3:], c_arr, [BF16] * 3)
    mlp_started, token = _exchange_start("exchange_mlp_start", mlp_sums)
    grad_x, dwi, dwo, *small, dwi16, dwo16 = _bwd_mix(dx1, saved, wi, wo, wstt, cwf, tabs, vq, vd, (loss_part, dg2, db2), token, alpha, tm)
    mlp_sums, mlp_parts = _exchange_wait("exchange_mlp_wait", mlp_started, [dwo])

    by_halves = lambda b: b.reshape(4, 2, b.shape[1] // 2, b.shape[2])
    halves = [by_halves(dwi), by_halves(dwo.reshape(4, d // 4, d))] + [s.reshape(1, 2, s.shape[0] // 2, s.shape[1]) for s in small]
    travelling = [by_halves(dwi16), by_halves(dwo16.reshape(4, d // 4, d))] + halves[2:]
    *sums, ws_sum, rest_sum = _pair_reduce("pair_reduce_mix", halves, travelling, c_arr, [BF16, BF16, BF16, F32])
    sums += [ws_sum[0], rest_sum[0]]
    mix_started, token = _exchange_start("exchange_mix_start", sums)

    out, raw = {}, {}

    def finish(first, names, parts, sums, after):
        mine, other = _chip_reduce(f"chip_reduce_{first}", parts, sums, [after])
        for a, nm in enumerate(names):
            w_, m_, v_ = weights[nm]
            if nm in ("w_gate", "w_up"):
                raw[nm] = _adamw(nm, w_[0].T, mine[a], other[a], m_[0].T, v_[0].T, c_arr)
                out[nm] = [o.T for o in raw[nm]]
            else:
                raw[nm] = out[nm] = _adamw(nm, w_[0], mine[a], other[a], m_[0], v_[0], c_arr)
        return mine[len(names):], other[len(names):]

    weights = {"w_in": (w_in, m_w_in, v_w_in), "w_out": (w_out, m_w_out, v_w_out), "w_gate": (w_gate, m_w_gate, v_w_gate),
               "w_up": (w_up, m_w_up, v_w_up), "w_down": (w_down, m_w_down, v_w_down)}
    finish(2, ["w_gate", "w_up", "w_down"], mlp_parts, mlp_sums, token)
    sums, parts = _exchange_wait("exchange_mix_wait", mix_started, [raw[nm][1] for nm in ("w_gate", "w_up", "w_down")])
    small_mine, small_other = finish(5, ["w_in", "w_out"], parts, sums, parts[0])

    small_params = {
        "sgu_ln_g": (sgu_ln_g, m_sgu_ln_g, v_sgu_ln_g), "sgu_ln_b": (sgu_ln_b, m_sgu_ln_b, v_sgu_ln_b),
        "conv_b": (conv_b, m_conv_b, v_conv_b), "conv_ln_g": (conv_ln_g, m_conv_ln_g, v_conv_ln_g),
        "conv_ln_b": (conv_ln_b, m_conv_ln_b, v_conv_ln_b), "ln1_g": (ln1_g, m_ln1_g, v_ln1_g), "ln1_b": (ln1_b, m_ln1_b, v_ln1_b),
        "ln2_g": (ln2_g, m_ln2_g, v_ln2_g), "ln2_b": (ln2_b, m_ln2_b, v_ln2_b), "w_s": (w_s, m_w_s, v_w_s),
        "b_s": (b_s, m_b_s, v_b_s), "conv_w": tuple(jnp.transpose(a, (1, 0, 2)) for a in (conv_w, m_conv_w, v_conv_w))}
    small_out, loss_block = _adamw_small(small_mine, small_other, c_arr, me_arr, small_params)
    small_out["conv_w"] = [jnp.transpose(o, (1, 0, 2)) for o in small_out["conv_w"]]
    loss = loss_block.reshape(())
    names = ["w_in", "sgu_ln_g", "sgu_ln_b", "w_s", "b_s", "conv_w", "conv_b", "conv_ln_g", "conv_ln_b", "w_out",
             "ln1_g", "ln1_b", "w_gate", "w_up", "w_down", "ln2_g", "ln2_b"]
    result = [loss, grad_x[None]]
    for kind in range(4):
        for nm in names:
            result.append(out[nm][kind][None] if nm in out else small_out[nm][kind])
    return tuple(result)
```

```python
import math

import jax
import numpy as np
import jax.numpy as jnp
from jax import lax
from jax.experimental import pallas as pl
from jax.experimental.pallas import tpu as pltpu

F32 = jnp.float32
BF16 = jnp.bfloat16

LN_EPS = 1e-5
HEAD_DIM = 64
CHUNK = 128
HALO = 32
LANES = 128
MXU_N = 256
ADAM_LR, ADAM_B1, ADAM_B2, ADAM_EPS, ADAM_WD, ADAM_STEP = 0.001, 0.9, 0.999, 1e-08, 0.01, 10
VMEM_LIMIT = 63 * 1024 * 1024
MESH_AXES = ("x", "y", "c")
MESH_ID = pl.DeviceIdType.MESH


def _dot(a, b):
    return jnp.dot(a, b, preferred_element_type=F32)


def _dot_nt(a, b):
    return lax.dot_general(a, b, (((1,), (1,)), ((), ())), preferred_element_type=F32)


def _dot_tn(a, b):
    return lax.dot_general(a, b, (((0,), (0,)), ((), ())), preferred_element_type=F32)


def _sigmoid(v):
    return 1.0 / (1.0 + jnp.exp(-v))


def _gelu(v):
    cdf = 0.5 * (1.0 + lax.erf(v * (1.0 / math.sqrt(2.0))))
    pdf = jnp.exp(-0.5 * v * v) * (1.0 / math.sqrt(2.0 * math.pi))
    return v * cdf, cdf + v * pdf


def _ln_stats(v):
    mu = jnp.mean(v, axis=-1, keepdims=True)
    d = v - mu
    rstd = lax.rsqrt(jnp.mean(d * d, axis=-1, keepdims=True) + LN_EPS)
    return d * rstd, rstd


def _ln_bwd(dxhat, xhat, rstd):
    m1 = jnp.mean(dxhat, axis=-1, keepdims=True)
    m2 = jnp.mean(dxhat * xhat, axis=-1, keepdims=True)
    return rstd * (dxhat - m1 - xhat * m2)


def _colsum(v):
    return jnp.sum(v, axis=0, keepdims=True)


def _pair_lanes(v, nc, p):
    return jnp.concatenate([v[c * CHUNK:(c + 1) * CHUNK, p * LANES:(p + 1) * LANES] for c in range(nc)], axis=1)


def _unpair(parts, nc):
    rows = [jnp.concatenate([part[:, c * LANES:(c + 1) * LANES] for part in parts], axis=1) for c in range(nc)]
    return jnp.concatenate(rows, axis=0)


def _low_head(nc):
    lane = lax.broadcasted_iota(jnp.int32, (CHUNK, nc * LANES), 1)
    return (lane & (LANES - 1)) < HEAD_DIM


def _mix(wst_ref, v, nc, n_pairs):
    vb = v.astype(BF16)
    low = _low_head(nc)
    parts = []
    for p in range(n_pairs):
        r = _dot(wst_ref[p], _pair_lanes(vb, nc, p))
        parts.append(jnp.where(low, r[:CHUNK], r[CHUNK:]))
    return _unpair(parts, nc)


def _mix_wgrad(dm, vn, nc, n_pairs):
    low = _low_head(nc)
    vb = vn.astype(BF16)
    out = []
    for p in range(n_pairs):
        a = _pair_lanes(dm, nc, p)
        lhs = jnp.concatenate([jnp.where(low, a, 0.0), jnp.where(low, 0.0, a)], axis=0).astype(BF16)
        out.append(_dot_nt(lhs, _pair_lanes(vb, nc, p)))
    return out


SUBLANES = 8


CONV_BLOCK = 256
DFT_N = CONV_BLOCK + HALO
DFT_F = -(-(DFT_N // 2 + 1) // SUBLANES) * SUBLANES


def _terms(m, exact):
    hi = m.astype(np.float32).astype(BF16)
    lo = (m.astype(np.float32) - hi.astype(np.float32)).astype(BF16)
    return np.concatenate([hi, hi, lo] if exact else [hi], axis=1)


def _split(v, exact=False):
    hi = v.astype(BF16)
    if not exact:
        return hi
    lo = (v - hi.astype(F32)).astype(BF16)
    return jnp.concatenate([hi, lo, hi], axis=0)


def _dft_tables(kw, kwp, q):
    nf = DFT_N // 2 + 1
    ang = 2.0 * np.pi * np.arange(nf)[:, None] * np.arange(DFT_N)[None, :] / DFT_N
    fwd = np.zeros((2 * DFT_F, DFT_N))
    fwd[:nf], fwd[DFT_F:DFT_F + nf] = np.cos(ang), -np.sin(ang)
    weight = np.full((nf, 1), 2.0 / DFT_N)
    weight[0] = weight[-1] = 1.0 / DFT_N
    inv = np.zeros((DFT_N, 2 * DFT_F))
    inv[:, :nf], inv[:, DFT_F:DFT_F + nf] = (np.cos(ang) * weight).T, (-np.sin(ang) * weight).T
    inv_taps = np.zeros((kwp, 2 * DFT_F))
    inv_taps[:kw] = inv[kw - 1::-1][:kw]
    shift = np.zeros((2 * DFT_F, q), np.float32)
    shift[:nf], shift[DFT_F:DFT_F + nf] = np.cos(ang[:, HALO:HALO + 1]), -np.sin(ang[:, HALO:HALO + 1])
    return {"fwd": _terms(fwd, False), "fwd_halo": _terms(fwd[:, CONV_BLOCK:], False), "shift": shift,
            "inv_out": _terms(inv[HALO:HALO + CONV_BLOCK], False), "inv_in": _terms(inv[:CONV_BLOCK], False),
            "taps": _terms(fwd[:, :kwp], True), "inv_taps": _terms(inv_taps, True)}


def _cmul(a, b, conj_b=False):
    ar, ai, br, bi = a[:DFT_F], a[DFT_F:], b[:DFT_F], b[DFT_F:]
    if conj_b:
        return jnp.concatenate([ar * br + ai * bi, ai * br - ar * bi], axis=0)
    return jnp.concatenate([ar * br - ai * bi, ar * bi + ai * br], axis=0)


def _interleave(sub_tiles):
    waiting, live = list(sub_tiles), []
    while waiting or live:
        if waiting:
            live.append(waiting.pop(0))
        for g in list(live):
            try:
                next(g)
            except StopIteration:
                live.remove(g)


def _start_copies(sems, pairs, first=0):
    copies = [pltpu.make_async_copy(src, dst, sems.at[first + k]) for k, (src, dst) in enumerate(pairs)]
    for cp in copies:
        cp.start()
    return copies


def _cparams():
    return pltpu.CompilerParams(dimension_semantics=("arbitrary",), vmem_limit_bytes=VMEM_LIMIT)


def _full(shape):
    return pl.BlockSpec(shape, lambda i: (0,) * len(shape))


ANY = pl.BlockSpec(memory_space=pl.ANY)

VQ_SGU_G, VQ_SGU_B, VQ_CONV_B, VQ_CLN_G, VQ_CLN_B = range(5)
VD_LN1_G, VD_LN1_B, VD_LN2_G, VD_LN2_B = range(4)
RS_LN1, RS_SGU, RS_CONV = range(3)
RS_COLS = LANES


def _saved_widths(d, q):
    f32 = [d, q, q, q, q, q, RS_COLS]
    bf16 = [d, 2 * q, d, q]
    return f32, bf16


def _saved_views(f32_ref, bf16_ref, d, q):
    views = []
    for ref, widths in zip((f32_ref, bf16_ref), _saved_widths(d, q)):
        for k, w in enumerate(widths):
            views.append(ref.at[pl.ds(0, ref.shape[0]), pl.ds(sum(widths[:k]), w)])
    return views
def _fwd_mix(x, wi, wo, wst, bmat, cwf, tabs, vq, vd, mlp_w, alpha, tm):
    t, d = x.shape
    q = wi.shape[2]
    nc, n_pairs = CONV_BLOCK // CHUNK, q // LANES
    n = t // tm
    n_in, n_saved = 11, 3
    assert tm % CONV_BLOCK == 0

    def body(x_ref, wi_hbm, wo_hbm, wst_ref, bmat_ref, cwf_ref, fwd_ref, taps_ref, inv_ref, vq_ref, vd_ref, *rest):
        f32_ref, bf16_ref, hf_ref = rest[3:3 + n_saved]
        xh_ref, zu_ref, mg_ref, vhat_ref, gv_ref, yhat_ref, rs_ref, xb_ref, pag_ref, y_ref, vnb_ref = _saved_views(f32_ref, bf16_ref, d, q)
        gathered = rest[3 + n_saved:6 + n_saved]
        wi_v, wo_v, hb_ref, gf_ref, send_sems, recv_sems, copy_sems = rest[6 + n_saved:]
        step = pl.program_id(0)

        @pl.when(step == 0)
        def _():
            loads = _start_copies(copy_sems, [(wi_hbm, wi_v), (wo_hbm, wo_v)])
            _Gather(gathered, send_sems, recv_sems).start()
            hb_ref[...] = jnp.zeros_like(hb_ref)
            gf_ref[...] = _dot(taps_ref[...], _split(cwf_ref[...], True))
            for cp in loads:
                cp.wait()

        @pl.when(step == (3 * n) // 4)
        def _():
            _Gather(gathered, send_sems, recv_sems).forward()

        def sub_tile(b):
            rows = slice(b * CONV_BLOCK, (b + 1) * CONV_BLOCK)
            xv = x_ref[rows, :]
            xb = xv.astype(BF16)
            xb_ref[rows, :] = xb
            pu, pv, pa, pg = (_dot(xb, wi_v[j]) for j in range(4))
            yield
            pag_ref[rows, 0:q] = pa.astype(BF16)
            pag_ref[rows, q:2 * q] = pg.astype(BF16)
            zu, gu = _gelu(pu)
            zv, gv = _gelu(pv)
            vhat, rstd_v = _ln_stats(zv)
            vnb = (vhat * vq_ref[VQ_SGU_G:VQ_SGU_G + 1, :] + vq_ref[VQ_SGU_B:VQ_SGU_B + 1, :]).astype(BF16)
            hb_ref[HALO + b * CONV_BLOCK:HALO + (b + 1) * CONV_BLOCK, :] = pa * _sigmoid(pg)
            yield
            mixed = _mix(wst_ref, vnb, nc, n_pairs) + jnp.concatenate([bmat_ref[...]] * nc, axis=0)
            spectrum = _dot(fwd_ref[...], _split(hb_ref[b * CONV_BLOCK:b * CONV_BLOCK + DFT_N, :]))
            yield
            y_ref[rows, 0:q] = (zu * mixed).astype(BF16)
            zu_ref[rows, :] = zu
            mg_ref[rows, :] = mixed * gu
            vhat_ref[rows, :] = vhat
            gv_ref[rows, :] = gv
            vnb_ref[rows, :] = vnb
            hf_ref[b * 2 * DFT_F:(b + 1) * 2 * DFT_F, :] = spectrum
            product = _split(_cmul(gf_ref[...], spectrum))
            yield
            yc = _dot(inv_ref[...], product) + vq_ref[VQ_CONV_B:VQ_CONV_B + 1, :]
            yield
            yhat, rstd_c = _ln_stats(yc)
            yhat_ref[rows, :] = yhat
            yn = yhat * vq_ref[VQ_CLN_G:VQ_CLN_G + 1, :] + vq_ref[VQ_CLN_B:VQ_CLN_B + 1, :]
            y_ref[rows, q:2 * q] = (yn * _sigmoid(yn)).astype(BF16)
            yield
            r1 = alpha * xv + _dot(y_ref[rows, :], wo_v[...])
            yield
            xhat, rstd1 = _ln_stats(r1)
            xh_ref[rows, :] = xhat
            col = lax.broadcasted_iota(jnp.int32, (CONV_BLOCK, RS_COLS), 1)
            rs_ref[rows, :] = jnp.where(col == RS_LN1, rstd1, jnp.where(col == RS_SGU, rstd_v, jnp.where(col == RS_CONV, rstd_c, 0.0)))

        _interleave([sub_tile(b) for b in range(tm // CONV_BLOCK)])
        hb_ref[0:HALO, :] = hb_ref[tm:tm + HALO, :]

        @pl.when(step == n - 1)
        def _():
            _Gather(gathered, send_sems, recv_sems).finish()

    row = lambda w: pl.BlockSpec((tm, w), lambda i: (i, 0))
    widths = [(sum(w), dt) for w, dt in zip(_saved_widths(d, q), (F32, BF16))]
    small_ins = [wst, bmat, cwf, tabs["fwd"], tabs["taps"], tabs["inv_out"], vq, vd]
    return pl.pallas_call(
        body, name="fwd_mix", grid=(n,),
        in_specs=[row(d), ANY, ANY] + [_full(a.shape) for a in small_ins] + [ANY] * 3,
        out_specs=[row(w) for w, _ in widths] + [pl.BlockSpec((tm // CONV_BLOCK * 2 * DFT_F, q), lambda i: (i, 0))] + [ANY] * 3,
        out_shape=[jax.ShapeDtypeStruct((t, w), dt) for w, dt in widths] + [jax.ShapeDtypeStruct((t // CONV_BLOCK * 2 * DFT_F, q), F32)]
        + [jax.ShapeDtypeStruct(b.shape, b.dtype) for b in mlp_w],
        scratch_shapes=[pltpu.VMEM(wi.shape, BF16), pltpu.VMEM(wo.shape, BF16), pltpu.VMEM((HALO + tm, q), F32),
                        pltpu.VMEM((2 * DFT_F, q), F32)] + _gather_sems(3) + [pltpu.SemaphoreType.DMA((2,))],
        input_output_aliases={n_in + a: n_saved + a for a in range(3)},
        compiler_params=_cparams(),
    )(x, wi, wo, *small_ins, *mlp_w)


MLP_SLABS = 4


def _hidden_slabs(f):
    assert f % MXU_N == 0
    tiles = f // MXU_N
    sizes = [(tiles // MLP_SLABS + (1 if j < tiles % MLP_SLABS else 0)) * MXU_N for j in range(MLP_SLABS)]
    return [(sum(sizes[:j]), sz) for j, sz in enumerate(sizes) if sz]


def _fwd_mlp(saved_f32, tgt, wg, wu, wd, vd, alpha, slabs, tm):
    t, d = tgt.shape
    n = t // tm
    ns = len(slabs)
    half = tm // 2 if tm % 32 == 0 else tm

    def body(xh_ref, tgt_ref, wg_hbm, wu_hbm, wd_hbm, vd_ref, *rest):
        gp_refs = [r.at[pl.ds(0, tm), pl.ds(0, sz)] for r, (_, sz) in zip(rest[:ns], slabs)]
        up_refs = [r.at[pl.ds(0, tm), pl.ds(sz, sz)] for r, (_, sz) in zip(rest[:ns], slabs)]
        x1b_ref, dr2_ref, loss_ref, dg2_ref, db2_ref, wg_v, wu_v, wd_v, copy_sems = rest[ns:]

        @pl.when(pl.program_id(0) == 0)
        def _():
            loads = _start_copies(copy_sems, [(wg_hbm, wg_v), (wu_hbm, wu_v), (wd_hbm, wd_v)])
            loss_ref[...] = jnp.zeros_like(loss_ref)
            dg2_ref[...] = jnp.zeros_like(dg2_ref)
            db2_ref[...] = jnp.zeros_like(db2_ref)
            for cp in loads:
                cp.wait()

        g2 = vd_ref[VD_LN2_G:VD_LN2_G + 1, :]

        for r0 in range(0, tm, half):
            rows = slice(r0, r0 + half)
            x1 = xh_ref[rows, :] * vd_ref[VD_LN1_G:VD_LN1_G + 1, :] + vd_ref[VD_LN1_B:VD_LN1_B + 1, :]
            x1b = x1.astype(BF16)
            x1b_ref[rows, :] = x1b
            acc = alpha * x1
            for (off, sz), gp_ref, up_ref in zip(slabs, gp_refs, up_refs):
                gp = _dot_nt(x1b, wg_v[off:off + sz, :])
                up = _dot_nt(x1b, wu_v[off:off + sz, :])
                gp_ref[rows, :] = gp.astype(BF16)
                up_ref[rows, :] = up.astype(BF16)
                acc = acc + _dot((gp * _sigmoid(gp) * up).astype(BF16), wd_v[off:off + sz, :])
            xh2, rstd2 = _ln_stats(acc)
            err = xh2 * g2 + vd_ref[VD_LN2_B:VD_LN2_B + 1, :] - tgt_ref[rows, :]
            loss_ref[...] += _colsum(jnp.sum(err * err, axis=1, keepdims=True)) * (0.5 / d)
            dy = err * (1.0 / d)
            dg2_ref[...] += _colsum(dy * xh2)
            db2_ref[...] += _colsum(dy)
            dr2_ref[rows, :] = _ln_bwd(dy * g2, xh2, rstd2)

    row = lambda w: pl.BlockSpec((tm, w), lambda i: (i, 0))
    act = [2 * sz for _, sz in slabs]
    return pl.pallas_call(
        body, name="fwd_mlp", grid=(n,),
        in_specs=[row(d), row(d), ANY, ANY, ANY, _full(vd.shape)],
        out_specs=[row(sz) for sz in act] + [row(d), row(d), _full((8, LANES)), _full((1, d)), _full((1, d))],
        out_shape=[jax.ShapeDtypeStruct((t, sz), BF16) for sz in act]
        + [jax.ShapeDtypeStruct((t, d), BF16), jax.ShapeDtypeStruct((t, d), F32),
           jax.ShapeDtypeStruct((8, LANES), F32), jax.ShapeDtypeStruct((1, d), F32), jax.ShapeDtypeStruct((1, d), F32)],
        scratch_shapes=[pltpu.VMEM(wg.shape, BF16), pltpu.VMEM(wu.shape, BF16), pltpu.VMEM(wd.shape, BF16), pltpu.SemaphoreType.DMA((3,))],
        compiler_params=_cparams(),
    )(saved_f32, tgt, wg, wu, wd, vd)


def _bwd_mlp_slab(j, slab, dr2, prev, x1b, gate_up, wg, wu, wd, alpha, tm, early=None):
    t, d = dr2.shape
    off, sz = slab
    n = t // tm
    first = prev is None

    def body(*refs):
        if first:
            dr_ref, x1b_ref, gu_ref, wg_hbm, wu_hbm, wd_hbm = refs[:6]
        else:
            dr_ref, dxp_ref, x1b_ref, gu_ref, wg_hbm, wu_hbm, wd_hbm = refs[:7]
        if early:
            lands, (send_sems, recv_sems) = refs[-12:-9], refs[-2:]
            refs = refs[:-12] + refs[-9:-2]
            x, y, c = _coords()
            ne, hr = early
            trips = [_remote(refs[10 + a].at[pl.ds(2 * hr * s + hr * (1 - c), hr)], lands[a].at[s], send_sems.at[ne * a + s],
                             recv_sems.at[ne * a + s], (x, y, 1 - c)) for a in range(3) for s in range(ne)]
        else:
            trips = []
        dx_ref, dwg_hbm, dwu_hbm, dwd_hbm, dwg16_hbm, dwu16_hbm, dwd16_hbm, ag, au, ad, wg_v, wu_v, wd_v, copy_sems = refs[-14:]

        @pl.when(pl.program_id(0) == 0)
        def _():
            loads = _start_copies(copy_sems, [(src.at[pl.ds(off, sz)], dst) for src, dst in ((wg_hbm, wg_v), (wu_hbm, wu_v), (wd_hbm, wd_v))])
            ag[...] = jnp.zeros_like(ag)
            au[...] = jnp.zeros_like(au)
            ad[...] = jnp.zeros_like(ad)
            for cp in trips:
                cp.start()
            for cp in loads:
                cp.wait()

        dr = dr_ref[...]
        drb = dr.astype(BF16)
        x1b = x1b_ref[...]
        gpv = gu_ref[:, 0:sz].astype(F32)
        upv = gu_ref[:, sz:2 * sz].astype(F32)
        dh = _dot_nt(drb, wd_v[...])
        sg = _sigmoid(gpv)
        silu = gpv * sg
        ad[...] += _dot_tn((silu * upv).astype(BF16), drb)
        dgp = (dh * upv * (sg * (1.0 + gpv * (1.0 - sg)))).astype(BF16)
        dup = (dh * silu).astype(BF16)
        ag[...] += _dot_tn(dgp, x1b)
        au[...] += _dot_tn(dup, x1b)
        base = alpha * dr if first else dxp_ref[...]
        dx_ref[...] = base + _dot(dgp, wg_v[...]) + _dot(dup, wu_v[...])

        @pl.when(pl.program_id(0) == n - 1)
        def _():
            rows = pl.ds(off, sz)
            stores = _start_copies(copy_sems, [(ag, dwg_hbm.at[rows]), (au, dwu_hbm.at[rows]), (ad, dwd_hbm.at[rows])])
            for acc, stage in ((ag, wg_v), (au, wu_v), (ad, wd_v)):
                stage[...] = acc[...].astype(BF16)
            stores += _start_copies(copy_sems, [(wg_v, dwg16_hbm.at[rows]), (wu_v, dwu16_hbm.at[rows]), (wd_v, dwd16_hbm.at[rows])], first=3)
            for cp in stores + trips:
                cp.wait()

    row = lambda w: pl.BlockSpec((tm, w), lambda i: (i, 0))
    extra_out = [jax.ShapeDtypeStruct((early[0], early[1], d), BF16)] * 3 if early else []
    extra_sems = [pltpu.SemaphoreType.DMA((3 * early[0],))] * 2 if early else []
    ins = [dr2] + ([] if first else [prev[0]]) + [x1b, gate_up, wg, wu, wd] + ([] if first else list(prev[1:]))
    in_specs = [row(d)] + ([] if first else [row(d)]) + [row(d), row(2 * sz), ANY, ANY, ANY] + ([] if first else [ANY] * 6)
    return pl.pallas_call(
        body, name=f"bwd_mlp_{j}", grid=(n,),
        in_specs=in_specs,
        out_specs=[row(d)] + [ANY] * (6 + len(extra_out)),
        out_shape=[jax.ShapeDtypeStruct((t, d), F32)] + [jax.ShapeDtypeStruct(wg.shape, F32)] * 3 + [jax.ShapeDtypeStruct(wg.shape, BF16)] * 3
        + extra_out,
        scratch_shapes=[pltpu.VMEM((sz, d), F32)] * 3 + [pltpu.VMEM((sz, d), BF16)] * 3 + [pltpu.SemaphoreType.DMA((6,))] + extra_sems,
        input_output_aliases={} if first else {7 + a: 1 + a for a in range(6)},
        compiler_params=_cparams(),
    )(*ins)


SMALL_VD = CHUNK
SMALL_CW = CHUNK + 8
SMALL_VQ = CHUNK + 8
SMALL_LOSS = CHUNK + 16
SMALL_BS = CHUNK + 24


def _small_rows(kwp):
    return -(-(SMALL_CW + max(kwp, 24 + SUBLANES)) // 16) * 16


def _bwd_mix(dx1, saved, wi, wo, wstt, cwf, tabs, vq, vd, mlp_small, token, alpha, tm):
    saved_f32, saved_bf16, hf_s = saved
    t, d = dx1.shape
    q = wi.shape[2]
    nc, n_pairs = CONV_BLOCK // CHUNK, q // LANES
    n = t // tm
    nb = tm // CONV_BLOCK
    assert tm % CONV_BLOCK == 0

    def body(dx1_ref, f32_ref, bf16_ref, hf_ref,
             wi_hbm, wo_hbm, wstt_ref, cwf_ref, fwd_ref, fwd_halo_ref, shift_ref, taps_ref, inv_ref, inv_taps_ref, vq_ref, vd_ref,
             loss_ref, dg2_ref, db2_ref, token_ref,
             gx_ref, dwi_hbm, dwo_hbm, small_ws_hbm, small_rest_hbm, dwi16_hbm, dwo16_hbm,
             wi_v, wo_v, awi, awo, dyb_ref, later_ref, dbm_ref, gf_ref, dgf_ref, small_ref, copy_sems):
        xh_ref, zu_ref, mg_ref, vhat_ref, gv_ref, yhat_ref, rs_ref, xb_ref, pag_ref, y_ref, vnb_ref = _saved_views(f32_ref, bf16_ref, d, q)
        i = pl.program_id(0)

        @pl.when(i == 0)
        def _():
            loads = _start_copies(copy_sems, [(wi_hbm, wi_v), (wo_hbm, wo_v)])
            for r in (awi, awo, small_ref, dbm_ref, dgf_ref, dyb_ref, later_ref):
                r[...] = jnp.zeros_like(r)
            gf_ref[...] = _dot(taps_ref[...], _split(cwf_ref[...], True))
            for cp in loads:
                cp.wait()

        dr1b_parts, dproj_parts = [None] * nb, [None] * nb

        def sub_tile(b):
            rows = slice(b * CONV_BLOCK, (b + 1) * CONV_BLOCK)
            dx1v = dx1_ref[rows, :]
            xh = xh_ref[rows, :]
            rsv = rs_ref[rows, :]
            small_ref[SMALL_VD + VD_LN1_G:SMALL_VD + VD_LN1_G + 1, :] += _colsum(dx1v * xh)
            small_ref[SMALL_VD + VD_LN1_B:SMALL_VD + VD_LN1_B + 1, :] += _colsum(dx1v)
            dr1 = _ln_bwd(dx1v * vd_ref[VD_LN1_G:VD_LN1_G + 1, :], xh, rsv[:, RS_LN1:RS_LN1 + 1])
            dr1b = dr1.astype(BF16)
            yield
            dy = _dot_nt(dr1b, wo_v[...])
            yield
            vhat = vhat_ref[rows, :]
            sgu_g = vq_ref[VQ_SGU_G:VQ_SGU_G + 1, :]
            doa = dy[:, 0:q]
            dm = doa * zu_ref[rows, :]
            dpu = (doa * mg_ref[rows, :]).astype(BF16)
            acc = dm[0:CHUNK]
            for c in range(1, nc):
                acc = acc + dm[c * CHUNK:(c + 1) * CHUNK]
            dbm_ref[...] += acc
            pa = pag_ref[rows, 0:q].astype(F32)
            sg = _sigmoid(pag_ref[rows, q:2 * q].astype(F32))
            yhat = yhat_ref[rows, :]
            cln_g = vq_ref[VQ_CLN_G:VQ_CLN_G + 1, :]
            yn = yhat * cln_g + vq_ref[VQ_CLN_B:VQ_CLN_B + 1, :]
            sy = _sigmoid(yn)
            dyn = dy[:, q:2 * q] * (sy * (1.0 + yn * (1.0 - sy)))
            small_ref[SMALL_VQ + VQ_CLN_G:SMALL_VQ + VQ_CLN_G + 1, q:2 * q] += _colsum(dyn * yhat)
            small_ref[SMALL_VQ + VQ_CLN_B:SMALL_VQ + VQ_CLN_B + 1, q:2 * q] += _colsum(dyn)
            dyc = _ln_bwd(dyn * cln_g, yhat, rsv[:, RS_CONV:RS_CONV + 1])
            small_ref[SMALL_VQ + VQ_CONV_B:SMALL_VQ + VQ_CONV_B + 1, q:2 * q] += _colsum(dyc)
            dyb_ref[b, 0:CONV_BLOCK, :] = dyc
            yield
            wgrads = _mix_wgrad(dm, vnb_ref[rows, :], nc, n_pairs)
            dvn = _mix(wstt_ref, dm, nc, n_pairs)
            own = _dot(fwd_ref[...], _split(dyb_ref[b]))
            with_later = own + _dot(fwd_halo_ref[...], _split(later_ref[...]))
            later_ref[...] = dyb_ref[b, 0:HALO, :]
            yield
            for p, g in enumerate(wgrads):
                for half in range(2):
                    small_ref[0:CHUNK, (2 * p + half) * CHUNK:(2 * p + half + 1) * CHUNK] += g[half * CHUNK:(half + 1) * CHUNK]
            small_ref[SMALL_VQ + VQ_SGU_G:SMALL_VQ + VQ_SGU_G + 1, q:2 * q] += _colsum(dvn * vhat)
            small_ref[SMALL_VQ + VQ_SGU_B:SMALL_VQ + VQ_SGU_B + 1, q:2 * q] += _colsum(dvn)
            dpv = (_ln_bwd(dvn * sgu_g, vhat, rsv[:, RS_SGU:RS_SGU + 1]) * gv_ref[rows, :]).astype(BF16)
            dgf_ref[...] += _cmul(_cmul(own, shift_ref[...]), hf_ref[b * 2 * DFT_F:(b + 1) * 2 * DFT_F, :], conj_b=True)
            product = _split(_cmul(with_later, gf_ref[...], conj_b=True))
            yield
            dh = _dot(inv_ref[...], product)
            yield
            da = (dh * sg).astype(BF16)
            dg = (dh * pa * (sg * (1.0 - sg))).astype(BF16)
            yield
            gx = alpha * dr1
            for dpj, wj in zip((dpu, dpv, da, dg), range(4)):
                gx = gx + _dot_nt(dpj, wi_v[wj])
            gx_ref[rows, :] = gx
            dr1b_parts[b], dproj_parts[b] = dr1b, (dpu, dpv, da, dg)

        _interleave([sub_tile(b) for b in reversed(range(nb))])

        awo[...] += _dot_tn(y_ref[...], jnp.concatenate(dr1b_parts, axis=0))
        xb = xb_ref[...]
        for j in range(4):
            awi[j] += _dot_tn(xb, jnp.concatenate([part[j] for part in dproj_parts], axis=0))

        @pl.when(i == n - 1)
        def _():
            stores = _start_copies(copy_sems, [(awi, dwi_hbm), (awo, dwo_hbm)])
            wi_v[...] = awi[...].astype(BF16)
            wo_v[...] = awo[...].astype(BF16)
            stores += _start_copies(copy_sems, [(wi_v, dwi16_hbm), (wo_v, dwo16_hbm)], first=3)
            lane = lax.broadcasted_iota(jnp.int32, (CHUNK, LANES), 1)
            low = lane < HEAD_DIM
            dbs = jnp.zeros((CHUNK, LANES), F32)
            for p in range(n_pairs):
                grp = dbm_ref[:, p * LANES:(p + 1) * LANES]
                dbs = jnp.where(lane == 2 * p, jnp.sum(jnp.where(low, grp, 0.0), axis=1, keepdims=True), dbs)
                dbs = jnp.where(lane == 2 * p + 1, jnp.sum(jnp.where(low, 0.0, grp), axis=1, keepdims=True), dbs)
            tril = lax.broadcasted_iota(jnp.int32, (CHUNK, CHUNK), 0) >= lax.broadcasted_iota(jnp.int32, (CHUNK, CHUNK), 1)
            for h in range(2 * n_pairs):
                block = small_ref[0:CHUNK, h * CHUNK:(h + 1) * CHUNK]
                small_ref[0:CHUNK, h * CHUNK:(h + 1) * CHUNK] = jnp.where(tril, block, 0.0)
            small_ref[SMALL_VD + VD_LN2_G:SMALL_VD + VD_LN2_G + 1, :] = dg2_ref[...]
            small_ref[SMALL_VD + VD_LN2_B:SMALL_VD + VD_LN2_B + 1, :] = db2_ref[...]
            small_ref[SMALL_CW:SMALL_CW + kwp, 0:q] = _dot(inv_taps_ref[...], _split(dgf_ref[...], True))
            small_ref[SMALL_LOSS:SMALL_LOSS + SUBLANES, q:q + LANES] = loss_ref[...]
            small_ref[SMALL_BS:SMALL_BS + SUBLANES, q:q + LANES] = jnp.transpose(dbs)[0:SUBLANES]
            stores += _start_copies(copy_sems, [(small_ref.at[pl.ds(0, CHUNK)], small_ws_hbm)], first=2)
            stores += _start_copies(copy_sems, [(small_ref.at[pl.ds(CHUNK, small.shape[0] - CHUNK)], small_rest_hbm)], first=5)
            for cp in stores:
                cp.wait()

    rev = lambda w: pl.BlockSpec((tm, w), lambda i: (n - 1 - i, 0))
    kwp = cwf.shape[0]
    small = jax.ShapeDtypeStruct((_small_rows(kwp), 2 * q), F32)
    small_ins = [wstt, cwf, tabs["fwd"], tabs["fwd_halo"], tabs["shift"], tabs["taps"], tabs["inv_in"], tabs["inv_taps"], vq, vd,
                 *mlp_small]
    return pl.pallas_call(
        body, name="bwd_mix", grid=(n,),
        in_specs=[rev(d), rev(saved_f32.shape[1]), rev(saved_bf16.shape[1]),
                  pl.BlockSpec((nb * 2 * DFT_F, q), lambda i: (n - 1 - i, 0)), ANY, ANY] + [_full(a.shape) for a in small_ins] + [ANY],
        out_specs=[rev(d)] + [ANY] * 6,
        out_shape=[jax.ShapeDtypeStruct((t, d), F32), jax.ShapeDtypeStruct(wi.shape, F32), jax.ShapeDtypeStruct(wo.shape, F32),
                   jax.ShapeDtypeStruct((CHUNK, 2 * q), F32), jax.ShapeDtypeStruct((small.shape[0] - CHUNK, 2 * q), F32),
                   jax.ShapeDtypeStruct(wi.shape, BF16), jax.ShapeDtypeStruct(wo.shape, BF16)],
        scratch_shapes=[pltpu.VMEM(wi.shape, BF16), pltpu.VMEM(wo.shape, BF16), pltpu.VMEM(wi.shape, F32), pltpu.VMEM(wo.shape, F32),
                        pltpu.VMEM((nb, DFT_N, q), F32), pltpu.VMEM((HALO, q), F32),
                        pltpu.VMEM((CHUNK, q), F32), pltpu.VMEM((2 * DFT_F, q), F32), pltpu.VMEM((2 * DFT_F, q), F32),
                        pltpu.VMEM(small.shape, F32), pltpu.SemaphoreType.DMA((6,))],
        compiler_params=_cparams(),
    )(dx1, saved_f32, saved_bf16, hf_s, wi, wo, *small_ins, token)


def _coords():
    return tuple(lax.axis_index(a) for a in MESH_AXES)


def _other_chips(x, y):
    return [(1 - x, y), (x, 1 - y), (1 - x, 1 - y)]


def _remote(src, dst, send_sem, recv_sem, to):
    return pltpu.make_async_remote_copy(src_ref=src, dst_ref=dst, send_sem=send_sem, recv_sem=recv_sem,
                                        device_id=to, device_id_type=MESH_ID)


class _Gather:
    def __init__(self, bufs, send_sems, recv_sems, own=None):
        self.bufs, self.send_sems, self.recv_sems, self.own = bufs, send_sems, recv_sems, own
        self.x, self.y, self.c = _coords()

    def _copies(self, stage):
        x, y, c = self.x, self.y, self.c
        for a, buf in enumerate(self.bufs):
            hr = buf.shape[1] // 2
            for j, chip in enumerate(_other_chips(x, y)):
                if stage == "ici_out":
                    ref, k, to = buf.at[2 * x + y, pl.ds(c * hr, hr)], j, (*chip, c)
                    if self.own:
                        yield _remote(self.own[a].at[pl.ds(c * hr, hr)], ref, self.send_sems.at[a, k], self.recv_sems.at[a, k], to)
                        continue
                elif stage == "ici_in":
                    ref, k, to = buf.at[2 * chip[0] + chip[1], pl.ds(c * hr, hr)], j, (*chip, c)
                elif stage == "d2d_out":
                    ref, k, to = buf.at[2 * chip[0] + chip[1], pl.ds(c * hr, hr)], 3 + j, (x, y, 1 - c)
                else:
                    ref, k, to = buf.at[2 * chip[0] + chip[1], pl.ds((1 - c) * hr, hr)], 3 + j, (x, y, 1 - c)
                yield _remote(ref, ref, self.send_sems.at[a, k], self.recv_sems.at[a, k], to)

    def start(self):
        for cp in self._copies("ici_out"):
            cp.start()

    def forward(self):
        for landed, onward in zip(self._copies("ici_in"), self._copies("d2d_out")):
            landed.wait_recv()
            onward.start()

    def finish(self):
        for cp in self._copies("d2d_in"):
            cp.wait_recv()
        for stage in ("ici_out", "d2d_out"):
            for cp in self._copies(stage):
                cp.wait_send()


def _gather_sems(n):
    return [pltpu.SemaphoreType.DMA((n, 6)), pltpu.SemaphoreType.DMA((n, 6))]


def _prep_gather(gathered, local, conv_w, kwp):
    kw, _, cshard = conv_w.shape
    shards = list(gathered) + list(local)
    n, n_g = len(shards), len(gathered)

    def body(*refs):
        src, cw_ref, out = refs[:n], refs[n], refs[n + 1:2 * n + 2]
        wide, narrow, taps = refs[2 * n + 2:3 * n + 2], refs[3 * n + 2:4 * n + 2], refs[4 * n + 2]
        load_sems, store_sems, send_sems, recv_sems = refs[4 * n + 3:]
        x, y, _ = _coords()
        loads = [pltpu.make_async_copy(src[a], wide[a], load_sems.at[a]) for a in range(n)]
        stores = [pltpu.make_async_copy(narrow[a], out[a].at[2 * x + y], store_sems.at[a]) for a in range(n)]
        stores.append(pltpu.make_async_copy(taps, out[n].at[2 * x + y], store_sems.at[n]))
        for cp in loads:
            cp.start()

        def cast(a):
            loads[a].wait()
            narrow[a][...] = wide[a][...].astype(BF16)
            stores[a].start()

        for a in range(n_g):
            cast(a)
        for tap in range(kw):
            taps[tap:tap + 1, :] = cw_ref[tap]
        taps[kw:kwp, :] = jnp.zeros((kwp - kw, cshard), F32)
        stores[n].start()
        g = _Gather(list(out[:n_g]) + [out[n]], send_sems, recv_sems, own=list(narrow[:n_g]) + [taps])
        g.start()
        for a in range(n_g, n):
            cast(a)
        g.forward()
        g.finish()
        for cp in stores:
            cp.wait()

    out_shape = [jax.ShapeDtypeStruct((4,) + a.shape, BF16) for a in shards] + [jax.ShapeDtypeStruct((4, kwp, cshard), F32)]
    scratch = ([pltpu.VMEM(a.shape, F32) for a in shards] + [pltpu.VMEM(a.shape, BF16) for a in shards] + [pltpu.VMEM((kwp, cshard), F32)]
               + [pltpu.SemaphoreType.DMA((n,)), pltpu.SemaphoreType.DMA((n + 1,))] + _gather_sems(n_g + 1))
    res = pl.pallas_call(
        body, name="prep_gather", in_specs=[ANY] * n + [pl.BlockSpec(memory_space=pltpu.VMEM)], out_specs=[ANY] * (n + 1),
        out_shape=out_shape, scratch_shapes=scratch, compiler_params=pltpu.CompilerParams(vmem_limit_bytes=VMEM_LIMIT),
    )(*shards, conv_w)
    return list(res[:n_g]), list(res[n_g:n]), res[n]


def _pair_reduce(name, partials, payloads, c_arr, out_dtypes, landed=()):
    n = len(partials)
    counts = [g.shape[0] for g in partials]
    first = [sum(counts[:a]) for a in range(n)]
    steps = sum(counts)
    early = [landed[a].shape[0] if a < len(landed) else 0 for a in range(n)]

    def body(c_ref, *refs):
        own, travelling, here = refs[:n], refs[n:2 * n], refs[2 * n:2 * n + len(landed)]
        out, land = (refs[2 * n + len(landed) + k * n:2 * n + len(landed) + (k + 1) * n] for k in range(2))
        send_sems, recv_sems = refs[4 * n + len(landed):]
        i = pl.program_id(0)
        x, y, c = _coords()

        def copy(a, q):
            return _remote(travelling[a].at[q, 1 - c], land[a].at[q], send_sems.at[first[a] + q], recv_sems.at[first[a] + q],
                           (x, y, 1 - c))

        blocks = [(a, q) for a in range(n) for q in range(early[a], counts[a])]

        @pl.when(i == 0)
        def _():
            for a, q in blocks:
                copy(a, q).start()

        for a in range(n):
            if early[a]:
                @pl.when((i >= first[a]) & (i < first[a] + early[a]))
                def _(a=a):
                    out[a][...] = (own[a][...] + here[a][...].astype(F32)).astype(out_dtypes[a])

            @pl.when((i >= first[a] + early[a]) & (i < first[a] + counts[a]))
            def _(a=a):
                q = i - first[a]
                copy(a, q).wait_recv()
                out[a][...] = (own[a][...] + land[a][q].astype(F32)).astype(out_dtypes[a])

        @pl.when(i == steps - 1)
        def _():
            for a, q in blocks:
                copy(a, q).wait_send()

    at = lambda a, i: jnp.clip(i - first[a], 0, counts[a] - 1)
    in_specs = [pl.BlockSpec((None, None) + g.shape[2:], lambda i, cr, a=a: (at(a, i), cr[0], 0, 0)) for a, g in enumerate(partials)]
    out_specs = [pl.BlockSpec((None,) + g.shape[2:], lambda i, cr, a=a: (at(a, i), 0, 0)) for a, g in enumerate(partials)]
    here_specs = [pl.BlockSpec((None,) + l.shape[1:], lambda i, cr, a=a: (jnp.clip(i - first[a], 0, early[a] - 1), 0, 0))
                  for a, l in enumerate(landed)]
    grid_spec = pltpu.PrefetchScalarGridSpec(
        num_scalar_prefetch=1, grid=(steps,), in_specs=in_specs + [ANY] * n + here_specs, out_specs=out_specs,
        scratch_shapes=[pltpu.VMEM((g.shape[0],) + g.shape[2:], p.dtype) for g, p in zip(partials, payloads)]
        + [pltpu.SemaphoreType.DMA((steps,)), pltpu.SemaphoreType.DMA((steps,))])
    out_shape = [jax.ShapeDtypeStruct((g.shape[0],) + g.shape[2:], dt) for g, dt in zip(partials, out_dtypes)]
    return list(pl.pallas_call(body, name=name, grid_spec=grid_spec, out_shape=out_shape, compiler_params=_cparams())(
        c_arr, *partials, *payloads, *landed))


class _Exchange:
    def __init__(self, src, dst, send_sems, recv_sems):
        self.src, self.dst, self.send_sems, self.recv_sems = src, dst, send_sems, recv_sems
        self.x, self.y, self.c = _coords()

    def _copies(self, incoming):
        x, y, c = self.x, self.y, self.c
        for a, (s, d) in enumerate(zip(self.src, self.dst)):
            for j, chip in enumerate(_other_chips(x, y)):
                slot = 2 * chip[0] + chip[1]
                if incoming:
                    out, into = d.at[slot], d.at[slot]
                else:
                    out, into = (s.at[slot] if len(s.shape) == 3 else s), d.at[2 * x + y]
                yield _remote(out, into, self.send_sems.at[a, j], self.recv_sems.at[a, j], (*chip, c))

    def start(self):
        for cp in self._copies(False):
            cp.start()

    def finish(self):
        for cp in self._copies(True):
            cp.wait_recv()
        for cp in self._copies(False):
            cp.wait_send()


def _exchange_shapes(arrs):
    return [jax.ShapeDtypeStruct((4,) + s.shape[-2:], s.dtype) for s in arrs]


class _FlatSems:
    def __init__(self, ref):
        self.ref = ref

    @property
    def at(self):
        return self

    def __getitem__(self, idx):
        return self.ref.at[3 * idx[0] + idx[1]]


HBM = pl.BlockSpec(memory_space=pltpu.HBM)
SEM = pl.BlockSpec(memory_space=pltpu.SEMAPHORE)
DATAFLOW = pltpu.SideEffectType.DATAFLOW_SIDE_EFFECTING


def _exchange_start(name, arrs):
    n = len(arrs)
    lands = _exchange_shapes(arrs)

    def body(*refs):
        src, land = refs[:n], refs[n:2 * n]
        send_sems, recv_sems = refs[2 * n:2 * n + 2]
        token = refs[-1]
        _Exchange(src, land, _FlatSems(send_sems), _FlatSems(recv_sems)).start()
        token[...] = jnp.zeros_like(token)

    hbm = lambda a: pltpu.with_memory_space_constraint(a, pltpu.HBM)
    outs = pl.pallas_call(
        body, name=name,
        out_shape=(pltpu.SemaphoreType.DMA((3 * n,)), pltpu.SemaphoreType.DMA((3 * n,)),
                   *[pltpu.HBM(a.shape, a.dtype) for a in arrs], *[pltpu.HBM(s.shape, s.dtype) for s in lands],
                   jax.ShapeDtypeStruct((SUBLANES, LANES), F32)),
        in_specs=[HBM] * (2 * n), out_specs=(SEM, SEM, *[HBM] * (2 * n), pl.BlockSpec(memory_space=pltpu.VMEM)),
        input_output_aliases={a: 2 + a for a in range(2 * n)},
        compiler_params=pltpu.CompilerParams(has_side_effects=DATAFLOW),
    )(*[hbm(a) for a in arrs], *[hbm(lax.empty(s.shape, s.dtype)) for s in lands])
    return outs[:-1], outs[-1]


def _exchange_wait(name, started, after):
    send_sems, recv_sems, *bufs = started
    n = len(bufs) // 2

    def body(*refs):
        src, land = refs[:n], refs[n:2 * n]
        send_sems, recv_sems = refs[2 * n:2 * n + 2]
        _Exchange(src, land, _FlatSems(send_sems), _FlatSems(recv_sems)).finish()

    outs = pl.pallas_call(
        body, name=name,
        out_shape=tuple(pltpu.HBM(b.shape, b.dtype) for b in bufs),
        in_specs=[HBM] * (2 * n) + [SEM, SEM] + [ANY] * len(after), out_specs=tuple([HBM] * (2 * n)),
        input_output_aliases={a: a for a in range(2 * n)},
        compiler_params=pltpu.CompilerParams(has_side_effects=DATAFLOW),
    )(*bufs, send_sems, recv_sems, *after)
    return list(outs[:n]), list(outs[n:])


def _chip_reduce(name, parts, owns, after):
    n = len(parts)

    def body(*refs):
        part, own = refs[:n], refs[n:2 * n]
        mine, other = refs[2 * n + len(after):3 * n + len(after)], refs[3 * n + len(after):4 * n + len(after)]
        scratch = refs[4 * n + len(after):]
        landed, own_part, total = scratch[:n], scratch[n:2 * n], scratch[2 * n:3 * n]
        load_sems, store_sems, send_sems, recv_sems = scratch[3 * n:]
        x, y, c = _coords()
        me = 2 * x + y
        loads, stores, sends = [], [], []
        for a in range(n):
            loads.append((pltpu.make_async_copy(part[a], landed[a], load_sems.at[a, 0]),
                          pltpu.make_async_copy(own[a].at[me] if len(own[a].shape) == 3 else own[a], own_part[a], load_sems.at[a, 1])))
            stores.append(pltpu.make_async_copy(total[a], mine[a], store_sems.at[a]))
            sends.append(_remote(total[a], other[a], send_sems.at[a], recv_sems.at[a], (x, y, 1 - c)))
        for both in loads:
            for cp in both:
                cp.start()
        for a in range(n):
            for cp in loads[a]:
                cp.wait()
            for chip in range(4):
                @pl.when(me == chip)
                def _(a=a, chip=chip):
                    term = lambda j: (own_part[a] if j == chip else landed[a].at[j])[...].astype(F32)
                    total[a][...] = ((term(0) + term(1)) + term(2)) + term(3)
            stores[a].start()
            sends[a].start()
        for a in range(n):
            stores[a].wait()
            sends[a].wait()

    halves = [jax.ShapeDtypeStruct(p.shape[1:], F32) for p in parts]
    scratch = ([pltpu.VMEM(p.shape, p.dtype) for p in parts] + [pltpu.VMEM(p.shape[1:], o.dtype) for p, o in zip(parts, owns)]
               + [pltpu.VMEM(p.shape[1:], F32) for p in parts]
               + [pltpu.SemaphoreType.DMA((n, 2))] + [pltpu.SemaphoreType.DMA((n,))] * 3)
    res = pl.pallas_call(
        body, name=name, in_specs=[ANY] * (2 * n + len(after)), out_specs=[ANY] * (2 * n), out_shape=halves + halves,
        scratch_shapes=scratch, compiler_params=pltpu.CompilerParams(vmem_limit_bytes=VMEM_LIMIT),
    )(*parts, *owns, *after)
    return list(res[:n]), list(res[n:])


def _row_block(rows, cols, limit=1 << 20):
    best = 8
    for tr in range(8, rows + 1, 8):
        if rows % tr == 0 and tr * cols * 4 <= limit:
            best = tr
    return best


def _adamw(name, w, g_mine, g_other, m, v, c_arr):
    r, c = w.shape
    hr, cg = g_mine.shape
    tr = hr if r % hr == 0 and hr * cg * 4 <= (3 << 19) else math.gcd(_row_block(hr, cg), r)
    per_half = hr // tr
    bc1 = 1.0 - ADAM_B1 ** ADAM_STEP
    bc2 = 1.0 - ADAM_B2 ** ADAM_STEP

    def body(c_ref, w_ref, gm_ref, go_ref, m_ref, v_ref, go, do, mo, vo):
        gv = jnp.where(pl.program_id(0) // per_half == c_ref[0], gm_ref[:, 0:c], go_ref[:, 0:c])
        mn = ADAM_B1 * m_ref[...] + (1.0 - ADAM_B1) * gv
        vn = ADAM_B2 * v_ref[...] + (1.0 - ADAM_B2) * (gv * gv)
        go[...] = gv
        mo[...] = mn
        vo[...] = vn
        do[...] = -ADAM_LR * ((mn / bc1) / (jnp.sqrt(vn / bc2) + ADAM_EPS) + ADAM_WD * w_ref[...])

    blk = pl.BlockSpec((tr, c), lambda i, cr: (i, 0))
    gblk = pl.BlockSpec((tr, cg), lambda i, cr: (i % per_half, 0))
    grid_spec = pltpu.PrefetchScalarGridSpec(num_scalar_prefetch=1, grid=(r // tr,), in_specs=[blk, gblk, gblk, blk, blk],
                                             out_specs=[blk] * 4)
    return pl.pallas_call(body, name=f"adamw_{name}", grid_spec=grid_spec, out_shape=[jax.ShapeDtypeStruct((r, c), F32)] * 4,
                          compiler_params=_cparams())(c_arr, w, g_mine, g_other, m, v)


SMALL_Q = ("sgu_ln_g", "sgu_ln_b", "conv_b", "conv_ln_g", "conv_ln_b")
SMALL_D = ("ln1_g", "ln1_b", "ln2_g", "ln2_b")


def _adamw_small(g_mine, g_other, c_arr, me_arr, params):
    names = list(SMALL_Q) + list(SMALL_D) + ["w_s", "b_s", "conv_w"]
    pieces = len(g_mine)
    q = g_mine[0].shape[1] // 2
    heads = params["w_s"][0].shape[1]
    kw, _, cshard = params["conv_w"][0].shape
    bc1 = 1.0 - ADAM_B1 ** ADAM_STEP
    bc2 = 1.0 - ADAM_B2 ** ADAM_STEP

    def update(w, g, m, v):
        mn = ADAM_B1 * m + (1.0 - ADAM_B1) * g
        vn = ADAM_B2 * v + (1.0 - ADAM_B2) * (g * g)
        return g, -ADAM_LR * ((mn / bc1) / (jnp.sqrt(vn / bc2) + ADAM_EPS) + ADAM_WD * w), mn, vn

    def body(c_ref, me_ref, *refs):
        mine, other, refs = refs[:pieces], refs[pieces:2 * pieces], refs[2 * pieces:]
        ins = {nm: refs[3 * k:3 * k + 3] for k, nm in enumerate(names)}
        outs = {nm: refs[3 * len(names) + 4 * k:3 * len(names) + 4 * k + 4] for k, nm in enumerate(names)}
        loss_ref, cw_ref = refs[-2:]
        low = c_ref[0] == 0
        in_order = lambda first, second: [jnp.where(low, first, second), jnp.where(low, second, first)]
        g_all = jnp.concatenate([half for a in range(pieces) for half in in_order(mine[a][...], other[a][...])], axis=0)

        def apply(nm, g, at):
            w, m, v = (r[at] for r in ins[nm])
            for o, val in zip(outs[nm], update(w, g, m, v)):
                o[at] = val

        for row, nm in enumerate(SMALL_Q):
            apply(nm, g_all[SMALL_VQ + row:SMALL_VQ + row + 1, q:2 * q], ...)
        for row, nm in enumerate(SMALL_D):
            apply(nm, g_all[SMALL_VD + row:SMALL_VD + row + 1, :], ...)
        for h in range(heads):
            apply("w_s", g_all[0:CHUNK, h * CHUNK:(h + 1) * CHUNK], (0, h))
        apply("b_s", g_all[SMALL_BS:SMALL_BS + heads, q:q + LANES], 0)
        cw_ref[...] = jnp.zeros_like(cw_ref)
        for chip in range(4):
            @pl.when(me_ref[0] == chip)
            def _():
                cw_ref[...] = g_all[SMALL_CW:SMALL_CW + cw_ref.shape[0], chip * cshard:(chip + 1) * cshard]
        for tap in range(kw):
            apply("conv_w", cw_ref[tap:tap + 1, :], tap)
        loss_ref[...] = g_all[SMALL_LOSS:SMALL_LOSS + 1, q:q + 1]

    arrays = [a for nm in names for a in params[nm]]
    out_shape = [jax.ShapeDtypeStruct(params[nm][0].shape, F32) for nm in names for _ in range(4)] + [jax.ShapeDtypeStruct((1, 1), F32)]
    whole = lambda shape: pl.BlockSpec(shape, lambda i, c, me: (0,) * len(shape))
    grid_spec = pltpu.PrefetchScalarGridSpec(
        num_scalar_prefetch=2, grid=(1,),
        in_specs=[whole(a.shape) for a in [*g_mine, *g_other, *arrays]],
        out_specs=[whole(s.shape) for s in out_shape],
        scratch_shapes=[pltpu.VMEM((-(-kw // SUBLANES) * SUBLANES, cshard), F32)])
    res = pl.pallas_call(body, name="adamw_small", grid_spec=grid_spec, out_shape=out_shape, compiler_params=_cparams())(
        c_arr, me_arr, *g_mine, *g_other, *arrays)
    return {nm: list(res[4 * k:4 * k + 4]) for k, nm in enumerate(names)}, res[-1]


def _pad_rows(a, rows):
    return jnp.pad(a, ((0, rows - a.shape[0]), (0, 0)))


def kernel(x, w_in, sgu_ln_g, sgu_ln_b, w_s, b_s, conv_w, conv_b, conv_ln_g, conv_ln_b, w_out, ln1_g, ln1_b, w_gate, w_up, w_down, ln2_g, ln2_b, loss_target, m_w_in, m_sgu_ln_g, m_sgu_ln_b, m_w_s, m_b_s, m_conv_w, m_conv_b, m_conv_ln_g, m_conv_ln_b, m_w_out, m_ln1_g, m_ln1_b, m_w_gate, m_w_up, m_w_down, m_ln2_g, m_ln2_b, v_w_in, v_sgu_ln_g, v_sgu_ln_b, v_w_s, v_b_s, v_conv_w, v_conv_b, v_conv_ln_g, v_conv_ln_b, v_w_out, v_ln1_g, v_ln1_b, v_w_gate, v_w_up, v_w_down, v_ln2_g, v_ln2_b):
    depth, d, q = w_in.shape
    assert depth == 1 and x.shape[0] == 1
    t = x.shape[1]
    heads = w_s.shape[1]
    kw, cshard = conv_w.shape[1], conv_w.shape[2]
    fs = w_gate.shape[2]
    slabs = _hidden_slabs(4 * fs)
    n_pairs = q // LANES
    assert heads * HEAD_DIM == q and q % LANES == 0 and w_s.shape[2] == CHUNK and 4 * cshard == q and kw - 1 <= HALO
    alpha = (2.0 * depth) ** 0.25
    tm = min(512, t)
    assert t % tm == 0 and tm % CHUNK == 0
    x2, tgt = x[0], loss_target[0]
    mx, my, mc = _coords()
    me = 2 * mx + my
    c_arr = jnp.reshape(mc, (1,)).astype(jnp.int32)

    kwp = -(-kw // 16) * 16
    me_arr = jnp.reshape(me, (1,)).astype(jnp.int32)
    (wi, wo), (wg, wu, wd), cw4 = _prep_gather([w_in[0], w_out[0]], [w_gate[0].T, w_up[0].T, w_down[0]],
                                              jnp.transpose(conv_w, (1, 0, 2)), kwp)
    wo = wo.reshape(d, d)
    cw = jnp.transpose(cw4, (1, 0, 2)).reshape(kwp, q)
    cwf = _pad_rows(cw[:kw][::-1], kwp)
    tabs = {name: jnp.asarray(tab) for name, tab in _dft_tables(kw, kwp, q).items()}

    wm = jnp.where(jnp.tril(jnp.ones((CHUNK, CHUNK), bool)), w_s[0], 0.0)
    wst = wm.reshape(n_pairs, 2 * CHUNK, CHUNK).astype(BF16)
    wstt = jnp.transpose(wm, (0, 2, 1)).reshape(n_pairs, 2 * CHUNK, CHUNK).astype(BF16)
    bmat = jnp.repeat(b_s[0].T, HEAD_DIM, axis=1)
    vq = _pad_rows(jnp.concatenate([sgu_ln_g, sgu_ln_b, conv_b, conv_ln_g, conv_ln_b], axis=0), 8)
    vd = _pad_rows(jnp.concatenate([ln1_g, ln1_b, ln2_g, ln2_b], axis=0), 8)

    *saved, wg, wu, wd = _fwd_mix(x2, wi, wo, wst, bmat, cwf, tabs, vq, vd, [wg, wu, wd], alpha, tm)
    wg, wu, wd = (w.reshape(4 * fs, d) for w in (wg, wu, wd))
    *acts, x1b, dr2, loss_part, dg2, db2 = _fwd_mlp(saved[0], tgt, wg, wu, wd, vd, alpha, slabs, tm)
    mlp_grads = None
    for j, slab in enumerate(slabs[:-1]):
        mlp_grads = _bwd_mlp_slab(j, slab, dr2, mlp_grads, x1b, acts[j], wg, wu, wd, alpha, tm)
    early = (slabs[-1][0] // fs, fs // 2)
    *mlp_grads, l_gate, l_up, l_down = _bwd_mlp_slab(len(slabs) - 1, slabs[-1], dr2, mlp_grads, x1b, acts[-1], wg, wu, wd, alpha, tm, early)
    dx1 = mlp_grads[0]
    mlp_halves = [b.reshape(4, 2, fs // 2, d) for b in mlp_grads[1:]]
    mlp_sums = _pair_reduce("pair_reduce_mlp", mlp_halves[:3], mlp_halves[3:], c_arr, [BF16] * 3, [l_gate, l_up, l_down])
    mlp_started, token = _exchange_start("exchange_mlp_start", mlp_sums)
    grad_x, dwi, dwo, *small, dwi16, dwo16 = _bwd_mix(dx1, saved, wi, wo, wstt, cwf, tabs, vq, vd, (loss_part, dg2, db2), token, alpha, tm)
    mlp_sums, mlp_parts = _exchange_wait("exchange_mlp_wait", mlp_started, [dwo])

    by_halves = lambda b: b.reshape(4, 2, b.shape[1] // 2, b.shape[2])
    halves = [by_halves(dwi), by_halves(dwo.reshape(4, d // 4, d))] + [s.reshape(1, 2, s.shape[0] // 2, s.shape[1]) for s in small]
    travelling = [by_halves(dwi16), by_halves(dwo16.reshape(4, d // 4, d))] + halves[2:]
    *sums, ws_sum, rest_sum = _pair_reduce("pair_reduce_mix", halves, travelling, c_arr, [BF16, BF16, BF16, F32])
    sums += [ws_sum[0], rest_sum[0]]
    mix_started, token = _exchange_start("exchange_mix_start", sums)

    out, raw = {}, {}

    def finish(first, names, parts, sums, after):
        mine, other = _chip_reduce(f"chip_reduce_{first}", parts, sums, [after])
        for a, nm in enumerate(names):
            w_, m_, v_ = weights[nm]
            if nm in ("w_gate", "w_up"):
                raw[nm] = _adamw(nm, w_[0].T, mine[a], other[a], m_[0].T, v_[0].T, c_arr)
                out[nm] = [o.T for o in raw[nm]]
            else:
                raw[nm] = out[nm] = _adamw(nm, w_[0], mine[a], other[a], m_[0], v_[0], c_arr)
        return mine[len(names):], other[len(names):]

    weights = {"w_in": (w_in, m_w_in, v_w_in), "w_out": (w_out, m_w_out, v_w_out), "w_gate": (w_gate, m_w_gate, v_w_gate),
               "w_up": (w_up, m_w_up, v_w_up), "w_down": (w_down, m_w_down, v_w_down)}
    finish(2, ["w_gate", "w_up", "w_down"], mlp_parts, mlp_sums, token)
    sums, parts = _exchange_wait("exchange_mix_wait", mix_started, [raw[nm][1] for nm in ("w_gate", "w_up", "w_down")])
    small_mine, small_other = finish(5, ["w_in", "w_out"], parts, sums, parts[0])

    small_params = {
        "sgu_ln_g": (sgu_ln_g, m_sgu_ln_g, v_sgu_ln_g), "sgu_ln_b": (sgu_ln_b, m_sgu_ln_b, v_sgu_ln_b),
        "conv_b": (conv_b, m_conv_b, v_conv_b), "conv_ln_g": (conv_ln_g, m_conv_ln_g, v_conv_ln_g),
        "conv_ln_b": (conv_ln_b, m_conv_ln_b, v_conv_ln_b), "ln1_g": (ln1_g, m_ln1_g, v_ln1_g), "ln1_b": (ln1_b, m_ln1_b, v_ln1_b),
        "ln2_g": (ln2_g, m_ln2_g, v_ln2_g), "ln2_b": (ln2_b, m_ln2_b, v_ln2_b), "w_s": (w_s, m_w_s, v_w_s),
        "b_s": (b_s, m_b_s, v_b_s), "conv_w": tuple(jnp.transpose(a, (1, 0, 2)) for a in (conv_w, m_conv_w, v_conv_w))}
    small_out, loss_block = _adamw_small(small_mine, small_other, c_arr, me_arr, small_params)
    small_out["conv_w"] = [jnp.transpose(o, (1, 0, 2)) for o in small_out["conv_w"]]
    loss = loss_block.reshape(())
    names = ["w_in", "sgu_ln_g", "sgu_ln_b", "w_s", "b_s", "conv_w", "conv_b", "conv_ln_g", "conv_ln_b", "w_out",
             "ln1_g", "ln1_b", "w_gate", "w_up", "w_down", "ln2_g", "ln2_b"]
    result = [loss, grad_x[None]]
    for kind in range(4):
        for nm in names:
            result.append(out[nm][kind][None] if nm in out else small_out[nm][kind])
    return tuple(result)
```

```python
import math

import jax
import numpy as np
import jax.numpy as jnp
from jax import lax
from jax.experimental import pallas as pl
from jax.experimental.pallas import tpu as pltpu

F32 = jnp.float32
BF16 = jnp.bfloat16

LN_EPS = 1e-5
HEAD_DIM = 64
CHUNK = 128
HALO = 32
LANES = 128
MXU_N = 256
ADAM_LR, ADAM_B1, ADAM_B2, ADAM_EPS, ADAM_WD, ADAM_STEP = 0.001, 0.9, 0.999, 1e-08, 0.01, 10
VMEM_LIMIT = 63 * 1024 * 1024
MESH_AXES = ("x", "y", "c")
MESH_ID = pl.DeviceIdType.MESH


def _dot(a, b):
    return jnp.dot(a, b, preferred_element_type=F32)


def _dot_nt(a, b):
    return lax.dot_general(a, b, (((1,), (1,)), ((), ())), preferred_element_type=F32)


def _dot_tn(a, b):
    return lax.dot_general(a, b, (((0,), (0,)), ((), ())), preferred_element_type=F32)


def _sigmoid(v):
    return 1.0 / (1.0 + jnp.exp(-v))


def _gelu(v):
    cdf = 0.5 * (1.0 + lax.erf(v * (1.0 / math.sqrt(2.0))))
    pdf = jnp.exp(-0.5 * v * v) * (1.0 / math.sqrt(2.0 * math.pi))
    return v * cdf, cdf + v * pdf


def _ln_stats(v):
    mu = jnp.mean(v, axis=-1, keepdims=True)
    d = v - mu
    rstd = lax.rsqrt(jnp.mean(d * d, axis=-1, keepdims=True) + LN_EPS)
    return d * rstd, rstd


def _ln_bwd(dxhat, xhat, rstd):
    m1 = jnp.mean(dxhat, axis=-1, keepdims=True)
    m2 = jnp.mean(dxhat * xhat, axis=-1, keepdims=True)
    return rstd * (dxhat - m1 - xhat * m2)


def _colsum(v):
    return jnp.sum(v, axis=0, keepdims=True)


def _pair_lanes(v, nc, p):
    return jnp.concatenate([v[c * CHUNK:(c + 1) * CHUNK, p * LANES:(p + 1) * LANES] for c in range(nc)], axis=1)


def _unpair(parts, nc):
    rows = [jnp.concatenate([part[:, c * LANES:(c + 1) * LANES] for part in parts], axis=1) for c in range(nc)]
    return jnp.concatenate(rows, axis=0)


def _low_head(nc):
    lane = lax.broadcasted_iota(jnp.int32, (CHUNK, nc * LANES), 1)
    return (lane & (LANES - 1)) < HEAD_DIM


def _mix(wst_ref, v, nc, n_pairs):
    vb = v.astype(BF16)
    low = _low_head(nc)
    parts = []
    for p in range(n_pairs):
        r = _dot(wst_ref[p], _pair_lanes(vb, nc, p))
        parts.append(jnp.where(low, r[:CHUNK], r[CHUNK:]))
    return _unpair(parts, nc)


def _mix_wgrad(dm, vn, nc, n_pairs):
    low = _low_head(nc)
    vb = vn.astype(BF16)
    out = []
    for p in range(n_pairs):
        a = _pair_lanes(dm, nc, p)
        lhs = jnp.concatenate([jnp.where(low, a, 0.0), jnp.where(low, 0.0, a)], axis=0).astype(BF16)
        out.append(_dot_nt(lhs, _pair_lanes(vb, nc, p)))
    return out


SUBLANES = 8


CONV_BLOCK = 256
DFT_N = CONV_BLOCK + HALO
DFT_F = -(-(DFT_N // 2 + 1) // SUBLANES) * SUBLANES


def _terms(m, exact):
    hi = m.astype(np.float32).astype(BF16)
    lo = (m.astype(np.float32) - hi.astype(np.float32)).astype(BF16)
    return np.concatenate([hi, hi, lo] if exact else [hi], axis=1)


def _split(v, exact=False):
    hi = v.astype(BF16)
    if not exact:
        return hi
    lo = (v - hi.astype(F32)).astype(BF16)
    return jnp.concatenate([hi, lo, hi], axis=0)


def _dft_tables(kw, kwp, q):
    nf = DFT_N // 2 + 1
    ang = 2.0 * np.pi * np.arange(nf)[:, None] * np.arange(DFT_N)[None, :] / DFT_N
    fwd = np.zeros((2 * DFT_F, DFT_N))
    fwd[:nf], fwd[DFT_F:DFT_F + nf] = np.cos(ang), -np.sin(ang)
    weight = np.full((nf, 1), 2.0 / DFT_N)
    weight[0] = weight[-1] = 1.0 / DFT_N
    inv = np.zeros((DFT_N, 2 * DFT_F))
    inv[:, :nf], inv[:, DFT_F:DFT_F + nf] = (np.cos(ang) * weight).T, (-np.sin(ang) * weight).T
    inv_taps = np.zeros((kwp, 2 * DFT_F))
    inv_taps[:kw] = inv[kw - 1::-1][:kw]
    shift = np.zeros((2 * DFT_F, q), np.float32)
    shift[:nf], shift[DFT_F:DFT_F + nf] = np.cos(ang[:, HALO:HALO + 1]), -np.sin(ang[:, HALO:HALO + 1])
    return {"fwd": _terms(fwd, False), "fwd_halo": _terms(fwd[:, CONV_BLOCK:], False), "shift": shift,
            "inv_out": _terms(inv[HALO:HALO + CONV_BLOCK], False), "inv_in": _terms(inv[:CONV_BLOCK], False),
            "taps": _terms(fwd[:, :kwp], True), "inv_taps": _terms(inv_taps, True)}


def _cmul(a, b, conj_b=False):
    ar, ai, br, bi = a[:DFT_F], a[DFT_F:], b[:DFT_F], b[DFT_F:]
    if conj_b:
        return jnp.concatenate([ar * br + ai * bi, ai * br - ar * bi], axis=0)
    return jnp.concatenate([ar * br - ai * bi, ar * bi + ai * br], axis=0)


def _interleave(sub_tiles):
    waiting, live = list(sub_tiles), []
    while waiting or live:
        if waiting:
            live.append(waiting.pop(0))
        for g in list(live):
            try:
                next(g)
            except StopIteration:
                live.remove(g)


def _start_copies(sems, pairs, first=0):
    copies = [pltpu.make_async_copy(src, dst, sems.at[first + k]) for k, (src, dst) in enumerate(pairs)]
    for cp in copies:
        cp.start()
    return copies


def _cparams():
    return pltpu.CompilerParams(dimension_semantics=("arbitrary",), vmem_limit_bytes=VMEM_LIMIT)


def _full(shape):
    return pl.BlockSpec(shape, lambda i: (0,) * len(shape))


ANY = pl.BlockSpec(memory_space=pl.ANY)

VQ_SGU_G, VQ_SGU_B, VQ_CONV_B, VQ_CLN_G, VQ_CLN_B = range(5)
VD_LN1_G, VD_LN1_B, VD_LN2_G, VD_LN2_B = range(4)
RS_LN1, RS_SGU, RS_CONV = range(3)
RS_COLS = LANES


def _saved_widths(d, q):
    f32 = [d, q, q, q, q, q, RS_COLS]
    bf16 = [d, 2 * q, d, q]
    return f32, bf16


def _saved_views(f32_ref, bf16_ref, d, q):
    views = []
    for ref, widths in zip((f32_ref, bf16_ref), _saved_widths(d, q)):
        for k, w in enumerate(widths):
            views.append(ref.at[pl.ds(0, ref.shape[0]), pl.ds(sum(widths[:k]), w)])
    return views
def _fwd_mix(x, wi, wo, wst, bmat, cwf, tabs, vq, vd, mlp_w, alpha, tm):
    t, d = x.shape
    q = wi.shape[2]
    nc, n_pairs = CONV_BLOCK // CHUNK, q // LANES
    n = t // tm
    n_in, n_saved = 11, 3
    assert tm % CONV_BLOCK == 0

    def body(x_ref, wi_hbm, wo_hbm, wst_ref, bmat_ref, cwf_ref, fwd_ref, taps_ref, inv_ref, vq_ref, vd_ref, *rest):
        f32_ref, bf16_ref, hf_ref = rest[3:3 + n_saved]
        xh_ref, zu_ref, mg_ref, vhat_ref, gv_ref, yhat_ref, rs_ref, xb_ref, pag_ref, y_ref, vnb_ref = _saved_views(f32_ref, bf16_ref, d, q)
        gathered = rest[3 + n_saved:6 + n_saved]
        wi_v, wo_v, hb_ref, gf_ref, send_sems, recv_sems, copy_sems = rest[6 + n_saved:]
        step = pl.program_id(0)

        @pl.when(step == 0)
        def _():
            loads = _start_copies(copy_sems, [(wi_hbm, wi_v), (wo_hbm, wo_v)])
            _Gather(gathered, send_sems, recv_sems).start()
            hb_ref[...] = jnp.zeros_like(hb_ref)
            gf_ref[...] = _dot(taps_ref[...], _split(cwf_ref[...], True))
            for cp in loads:
                cp.wait()

        @pl.when(step == (3 * n) // 4)
        def _():
            _Gather(gathered, send_sems, recv_sems).forward()

        def sub_tile(b):
            rows = slice(b * CONV_BLOCK, (b + 1) * CONV_BLOCK)
            xv = x_ref[rows, :]
            xb = xv.astype(BF16)
            xb_ref[rows, :] = xb
            pu, pv, pa, pg = (_dot(xb, wi_v[j]) for j in range(4))
            yield
            pag_ref[rows, 0:q] = pa.astype(BF16)
            pag_ref[rows, q:2 * q] = pg.astype(BF16)
            zu, gu = _gelu(pu)
            zv, gv = _gelu(pv)
            vhat, rstd_v = _ln_stats(zv)
            vnb = (vhat * vq_ref[VQ_SGU_G:VQ_SGU_G + 1, :] + vq_ref[VQ_SGU_B:VQ_SGU_B + 1, :]).astype(BF16)
            hb_ref[HALO + b * CONV_BLOCK:HALO + (b + 1) * CONV_BLOCK, :] = pa * _sigmoid(pg)
            yield
            mixed = _mix(wst_ref, vnb, nc, n_pairs) + jnp.concatenate([bmat_ref[...]] * nc, axis=0)
            spectrum = _dot(fwd_ref[...], _split(hb_ref[b * CONV_BLOCK:b * CONV_BLOCK + DFT_N, :]))
            yield
            y_ref[rows, 0:q] = (zu * mixed).astype(BF16)
            zu_ref[rows, :] = zu
            mg_ref[rows, :] = mixed * gu
            vhat_ref[rows, :] = vhat
            gv_ref[rows, :] = gv
            vnb_ref[rows, :] = vnb
            hf_ref[b * 2 * DFT_F:(b + 1) * 2 * DFT_F, :] = spectrum
            product = _split(_cmul(gf_ref[...], spectrum))
            yield
            yc = _dot(inv_ref[...], product) + vq_ref[VQ_CONV_B:VQ_CONV_B + 1, :]
            yield
            yhat, rstd_c = _ln_stats(yc)
            yhat_ref[rows, :] = yhat
            yn = yhat * vq_ref[VQ_CLN_G:VQ_CLN_G + 1, :] + vq_ref[VQ_CLN_B:VQ_CLN_B + 1, :]
            y_ref[rows, q:2 * q] = (yn * _sigmoid(yn)).astype(BF16)
            yield
            r1 = alpha * xv + _dot(y_ref[rows, :], wo_v[...])
            yield
            xhat, rstd1 = _ln_stats(r1)
            xh_ref[rows, :] = xhat
            col = lax.broadcasted_iota(jnp.int32, (CONV_BLOCK, RS_COLS), 1)
            rs_ref[rows, :] = jnp.where(col == RS_LN1, rstd1, jnp.where(col == RS_SGU, rstd_v, jnp.where(col == RS_CONV, rstd_c, 0.0)))

        _interleave([sub_tile(b) for b in range(tm // CONV_BLOCK)])
        hb_ref[0:HALO, :] = hb_ref[tm:tm + HALO, :]

        @pl.when(step == n - 1)
        def _():
            _Gather(gathered, send_sems, recv_sems).finish()

    row = lambda w: pl.BlockSpec((tm, w), lambda i: (i, 0))
    widths = [(sum(w), dt) for w, dt in zip(_saved_widths(d, q), (F32, BF16))]
    small_ins = [wst, bmat, cwf, tabs["fwd"], tabs["taps"], tabs["inv_out"], vq, vd]
    return pl.pallas_call(
        body, name="fwd_mix", grid=(n,),
        in_specs=[row(d), ANY, ANY] + [_full(a.shape) for a in small_ins] + [ANY] * 3,
        out_specs=[row(w) for w, _ in widths] + [pl.BlockSpec((tm // CONV_BLOCK * 2 * DFT_F, q), lambda i: (i, 0))] + [ANY] * 3,
        out_shape=[jax.ShapeDtypeStruct((t, w), dt) for w, dt in widths] + [jax.ShapeDtypeStruct((t // CONV_BLOCK * 2 * DFT_F, q), F32)]
        + [jax.ShapeDtypeStruct(b.shape, b.dtype) for b in mlp_w],
        scratch_shapes=[pltpu.VMEM(wi.shape, BF16), pltpu.VMEM(wo.shape, BF16), pltpu.VMEM((HALO + tm, q), F32),
                        pltpu.VMEM((2 * DFT_F, q), F32)] + _gather_sems(3) + [pltpu.SemaphoreType.DMA((2,))],
        input_output_aliases={n_in + a: n_saved + a for a in range(3)},
        compiler_params=_cparams(),
    )(x, wi, wo, *small_ins, *mlp_w)


MLP_SLABS = 4


def _hidden_slabs(f):
    assert f % MXU_N == 0
    tiles = f // MXU_N
    sizes = [(tiles // MLP_SLABS + (1 if j < tiles % MLP_SLABS else 0)) * MXU_N for j in range(MLP_SLABS)]
    return [(sum(sizes[:j]), sz) for j, sz in enumerate(sizes) if sz]


def _fwd_mlp(saved_f32, tgt, wg, wu, wd, vd, alpha, slabs, tm):
    t, d = tgt.shape
    n = t // tm
    ns = len(slabs)
    half = tm // 2 if tm % 32 == 0 else tm

    def body(xh_ref, tgt_ref, wg_hbm, wu_hbm, wd_hbm, vd_ref, *rest):
        gp_refs = [r.at[pl.ds(0, tm), pl.ds(0, sz)] for r, (_, sz) in zip(rest[:ns], slabs)]
        up_refs = [r.at[pl.ds(0, tm), pl.ds(sz, sz)] for r, (_, sz) in zip(rest[:ns], slabs)]
        x1b_ref, dr2_ref, loss_ref, dg2_ref, db2_ref, wg_v, wu_v, wd_v, copy_sems = rest[ns:]

        @pl.when(pl.program_id(0) == 0)
        def _():
            loads = _start_copies(copy_sems, [(wg_hbm, wg_v), (wu_hbm, wu_v), (wd_hbm, wd_v)])
            loss_ref[...] = jnp.zeros_like(loss_ref)
            dg2_ref[...] = jnp.zeros_like(dg2_ref)
            db2_ref[...] = jnp.zeros_like(db2_ref)
            for cp in loads:
                cp.wait()

        g2 = vd_ref[VD_LN2_G:VD_LN2_G + 1, :]

        for r0 in range(0, tm, half):
            rows = slice(r0, r0 + half)
            x1 = xh_ref[rows, :] * vd_ref[VD_LN1_G:VD_LN1_G + 1, :] + vd_ref[VD_LN1_B:VD_LN1_B + 1, :]
            x1b = x1.astype(BF16)
            x1b_ref[rows, :] = x1b
            acc = alpha * x1
            for (off, sz), gp_ref, up_ref in zip(slabs, gp_refs, up_refs):
                gp = _dot_nt(x1b, wg_v[off:off + sz, :])
                up = _dot_nt(x1b, wu_v[off:off + sz, :])
                gp_ref[rows, :] = gp.astype(BF16)
                up_ref[rows, :] = up.astype(BF16)
                acc = acc + _dot((gp * _sigmoid(gp) * up).astype(BF16), wd_v[off:off + sz, :])
            xh2, rstd2 = _ln_stats(acc)
            err = xh2 * g2 + vd_ref[VD_LN2_B:VD_LN2_B + 1, :] - tgt_ref[rows, :]
            loss_ref[...] += _colsum(jnp.sum(err * err, axis=1, keepdims=True)) * (0.5 / d)
            dy = err * (1.0 / d)
            dg2_ref[...] += _colsum(dy * xh2)
            db2_ref[...] += _colsum(dy)
            dr2_ref[rows, :] = _ln_bwd(dy * g2, xh2, rstd2)

    row = lambda w: pl.BlockSpec((tm, w), lambda i: (i, 0))
    act = [2 * sz for _, sz in slabs]
    return pl.pallas_call(
        body, name="fwd_mlp", grid=(n,),
        in_specs=[row(d), row(d), ANY, ANY, ANY, _full(vd.shape)],
        out_specs=[row(sz) for sz in act] + [row(d), row(d), _full((8, LANES)), _full((1, d)), _full((1, d))],
        out_shape=[jax.ShapeDtypeStruct((t, sz), BF16) for sz in act]
        + [jax.ShapeDtypeStruct((t, d), BF16), jax.ShapeDtypeStruct((t, d), F32),
           jax.ShapeDtypeStruct((8, LANES), F32), jax.ShapeDtypeStruct((1, d), F32), jax.ShapeDtypeStruct((1, d), F32)],
        scratch_shapes=[pltpu.VMEM(wg.shape, BF16), pltpu.VMEM(wu.shape, BF16), pltpu.VMEM(wd.shape, BF16), pltpu.SemaphoreType.DMA((3,))],
        compiler_params=_cparams(),
    )(saved_f32, tgt, wg, wu, wd, vd)


def _bwd_mlp_slab(j, slab, dr2, prev, x1b, gate_up, wg, wu, wd, alpha, tm):
    t, d = dr2.shape
    off, sz = slab
    n = t // tm
    first = prev is None

    def body(*refs):
        if first:
            dr_ref, x1b_ref, gu_ref, wg_hbm, wu_hbm, wd_hbm = refs[:6]
        else:
            dr_ref, dxp_ref, x1b_ref, gu_ref, wg_hbm, wu_hbm, wd_hbm = refs[:7]
        dx_ref, dwg_hbm, dwu_hbm, dwd_hbm, dwg16_hbm, dwu16_hbm, dwd16_hbm, ag, au, ad, wg_v, wu_v, wd_v, copy_sems = refs[-14:]

        @pl.when(pl.program_id(0) == 0)
        def _():
            loads = _start_copies(copy_sems, [(src.at[pl.ds(off, sz)], dst) for src, dst in ((wg_hbm, wg_v), (wu_hbm, wu_v), (wd_hbm, wd_v))])
            ag[...] = jnp.zeros_like(ag)
            au[...] = jnp.zeros_like(au)
            ad[...] = jnp.zeros_like(ad)
            for cp in loads:
                cp.wait()

        dr = dr_ref[...]
        drb = dr.astype(BF16)
        x1b = x1b_ref[...]
        gpv = gu_ref[:, 0:sz].astype(F32)
        upv = gu_ref[:, sz:2 * sz].astype(F32)
        dh = _dot_nt(drb, wd_v[...])
        sg = _sigmoid(gpv)
        silu = gpv * sg
        ad[...] += _dot_tn((silu * upv).astype(BF16), drb)
        dgp = (dh * upv * (sg * (1.0 + gpv * (1.0 - sg)))).astype(BF16)
        dup = (dh * silu).astype(BF16)
        ag[...] += _dot_tn(dgp, x1b)
        au[...] += _dot_tn(dup, x1b)
        base = alpha * dr if first else dxp_ref[...]
        dx_ref[...] = base + _dot(dgp, wg_v[...]) + _dot(dup, wu_v[...])

        @pl.when(pl.program_id(0) == n - 1)
        def _():
            rows = pl.ds(off, sz)
            stores = _start_copies(copy_sems, [(ag, dwg_hbm.at[rows]), (au, dwu_hbm.at[rows]), (ad, dwd_hbm.at[rows])])
            for acc, stage in ((ag, wg_v), (au, wu_v), (ad, wd_v)):
                stage[...] = acc[...].astype(BF16)
            stores += _start_copies(copy_sems, [(wg_v, dwg16_hbm.at[rows]), (wu_v, dwu16_hbm.at[rows]), (wd_v, dwd16_hbm.at[rows])], first=3)
            for cp in stores:
                cp.wait()


    row = lambda w: pl.BlockSpec((tm, w), lambda i: (i, 0))
    ins = [dr2] + ([] if first else [prev[0]]) + [x1b, gate_up, wg, wu, wd] + ([] if first else list(prev[1:]))
    in_specs = [row(d)] + ([] if first else [row(d)]) + [row(d), row(2 * sz), ANY, ANY, ANY] + ([] if first else [ANY] * 6)
    return pl.pallas_call(
        body, name=f"bwd_mlp_{j}", grid=(n,),
        in_specs=in_specs,
        out_specs=[row(d)] + [ANY] * 6,
        out_shape=[jax.ShapeDtypeStruct((t, d), F32)] + [jax.ShapeDtypeStruct(wg.shape, F32)] * 3 + [jax.ShapeDtypeStruct(wg.shape, BF16)] * 3,
        scratch_shapes=[pltpu.VMEM((sz, d), F32)] * 3 + [pltpu.VMEM((sz, d), BF16)] * 3 + [pltpu.SemaphoreType.DMA((6,))],
        input_output_aliases={} if first else {7 + a: 1 + a for a in range(6)},
        compiler_params=_cparams(),
    )(*ins)


SMALL_VD = CHUNK
SMALL_CW = CHUNK + 8
SMALL_VQ = CHUNK + 8
SMALL_LOSS = CHUNK + 16
SMALL_BS = CHUNK + 24


def _small_rows(kwp):
    return -(-(SMALL_CW + max(kwp, 24 + SUBLANES)) // 16) * 16


def _bwd_mix(dx1, saved, wi, wo, wstt, cwf, tabs, vq, vd, mlp_small, token, alpha, tm):
    saved_f32, saved_bf16, hf_s = saved
    t, d = dx1.shape
    q = wi.shape[2]
    nc, n_pairs = CONV_BLOCK // CHUNK, q // LANES
    n = t // tm
    nb = tm // CONV_BLOCK
    assert tm % CONV_BLOCK == 0

    def body(dx1_ref, f32_ref, bf16_ref, hf_ref,
             wi_hbm, wo_hbm, wstt_ref, cwf_ref, fwd_ref, fwd_halo_ref, shift_ref, taps_ref, inv_ref, inv_taps_ref, vq_ref, vd_ref,
             loss_ref, dg2_ref, db2_ref, token_ref,
             gx_ref, dwi_hbm, dwo_hbm, small_ws_hbm, small_rest_hbm, dwi16_hbm, dwo16_hbm,
             wi_v, wo_v, awi, awo, dyb_ref, later_ref, dbm_ref, gf_ref, dgf_ref, small_ref, copy_sems):
        xh_ref, zu_ref, mg_ref, vhat_ref, gv_ref, yhat_ref, rs_ref, xb_ref, pag_ref, y_ref, vnb_ref = _saved_views(f32_ref, bf16_ref, d, q)
        i = pl.program_id(0)

        @pl.when(i == 0)
        def _():
            loads = _start_copies(copy_sems, [(wi_hbm, wi_v), (wo_hbm, wo_v)])
            for r in (awi, awo, small_ref, dbm_ref, dgf_ref, dyb_ref, later_ref):
                r[...] = jnp.zeros_like(r)
            gf_ref[...] = _dot(taps_ref[...], _split(cwf_ref[...], True))
            for cp in loads:
                cp.wait()

        dr1b_parts, dproj_parts = [None] * nb, [None] * nb

        def sub_tile(b):
            rows = slice(b * CONV_BLOCK, (b + 1) * CONV_BLOCK)
            dx1v = dx1_ref[rows, :]
            xh = xh_ref[rows, :]
            rsv = rs_ref[rows, :]
            small_ref[SMALL_VD + VD_LN1_G:SMALL_VD + VD_LN1_G + 1, :] += _colsum(dx1v * xh)
            small_ref[SMALL_VD + VD_LN1_B:SMALL_VD + VD_LN1_B + 1, :] += _colsum(dx1v)
            dr1 = _ln_bwd(dx1v * vd_ref[VD_LN1_G:VD_LN1_G + 1, :], xh, rsv[:, RS_LN1:RS_LN1 + 1])
            dr1b = dr1.astype(BF16)
            yield
            dy = _dot_nt(dr1b, wo_v[...])
            yield
            vhat = vhat_ref[rows, :]
            sgu_g = vq_ref[VQ_SGU_G:VQ_SGU_G + 1, :]
            doa = dy[:, 0:q]
            dm = doa * zu_ref[rows, :]
            dpu = (doa * mg_ref[rows, :]).astype(BF16)
            acc = dm[0:CHUNK]
            for c in range(1, nc):
                acc = acc + dm[c * CHUNK:(c + 1) * CHUNK]
            dbm_ref[...] += acc
            pa = pag_ref[rows, 0:q].astype(F32)
            sg = _sigmoid(pag_ref[rows, q:2 * q].astype(F32))
            yhat = yhat_ref[rows, :]
            cln_g = vq_ref[VQ_CLN_G:VQ_CLN_G + 1, :]
            yn = yhat * cln_g + vq_ref[VQ_CLN_B:VQ_CLN_B + 1, :]
            sy = _sigmoid(yn)
            dyn = dy[:, q:2 * q] * (sy * (1.0 + yn * (1.0 - sy)))
            small_ref[SMALL_VQ + VQ_CLN_G:SMALL_VQ + VQ_CLN_G + 1, q:2 * q] += _colsum(dyn * yhat)
            small_ref[SMALL_VQ + VQ_CLN_B:SMALL_VQ + VQ_CLN_B + 1, q:2 * q] += _colsum(dyn)
            dyc = _ln_bwd(dyn * cln_g, yhat, rsv[:, RS_CONV:RS_CONV + 1])
            small_ref[SMALL_VQ + VQ_CONV_B:SMALL_VQ + VQ_CONV_B + 1, q:2 * q] += _colsum(dyc)
            dyb_ref[b, 0:CONV_BLOCK, :] = dyc
            yield
            wgrads = _mix_wgrad(dm, vnb_ref[rows, :], nc, n_pairs)
            dvn = _mix(wstt_ref, dm, nc, n_pairs)
            own = _dot(fwd_ref[...], _split(dyb_ref[b]))
            with_later = own + _dot(fwd_halo_ref[...], _split(later_ref[...]))
            later_ref[...] = dyb_ref[b, 0:HALO, :]
            yield
            for p, g in enumerate(wgrads):
                for half in range(2):
                    small_ref[0:CHUNK, (2 * p + half) * CHUNK:(2 * p + half + 1) * CHUNK] += g[half * CHUNK:(half + 1) * CHUNK]
            small_ref[SMALL_VQ + VQ_SGU_G:SMALL_VQ + VQ_SGU_G + 1, q:2 * q] += _colsum(dvn * vhat)
            small_ref[SMALL_VQ + VQ_SGU_B:SMALL_VQ + VQ_SGU_B + 1, q:2 * q] += _colsum(dvn)
            dpv = (_ln_bwd(dvn * sgu_g, vhat, rsv[:, RS_SGU:RS_SGU + 1]) * gv_ref[rows, :]).astype(BF16)
            dgf_ref[...] += _cmul(_cmul(own, shift_ref[...]), hf_ref[b * 2 * DFT_F:(b + 1) * 2 * DFT_F, :], conj_b=True)
            product = _split(_cmul(with_later, gf_ref[...], conj_b=True))
            yield
            dh = _dot(inv_ref[...], product)
            yield
            da = (dh * sg).astype(BF16)
            dg = (dh * pa * (sg * (1.0 - sg))).astype(BF16)
            yield
            gx = alpha * dr1
            for dpj, wj in zip((dpu, dpv, da, dg), range(4)):
                gx = gx + _dot_nt(dpj, wi_v[wj])
            gx_ref[rows, :] = gx
            dr1b_parts[b], dproj_parts[b] = dr1b, (dpu, dpv, da, dg)

        _interleave([sub_tile(b) for b in reversed(range(nb))])

        awo[...] += _dot_tn(y_ref[...], jnp.concatenate(dr1b_parts, axis=0))
        xb = xb_ref[...]
        for j in range(4):
            awi[j] += _dot_tn(xb, jnp.concatenate([part[j] for part in dproj_parts], axis=0))

        @pl.when(i == n - 1)
        def _():
            stores = _start_copies(copy_sems, [(awi, dwi_hbm), (awo, dwo_hbm)])
            wi_v[...] = awi[...].astype(BF16)
            wo_v[...] = awo[...].astype(BF16)
            stores += _start_copies(copy_sems, [(wi_v, dwi16_hbm), (wo_v, dwo16_hbm)], first=3)
            lane = lax.broadcasted_iota(jnp.int32, (CHUNK, LANES), 1)
            low = lane < HEAD_DIM
            dbs = jnp.zeros((CHUNK, LANES), F32)
            for p in range(n_pairs):
                grp = dbm_ref[:, p * LANES:(p + 1) * LANES]
                dbs = jnp.where(lane == 2 * p, jnp.sum(jnp.where(low, grp, 0.0), axis=1, keepdims=True), dbs)
                dbs = jnp.where(lane == 2 * p + 1, jnp.sum(jnp.where(low, 0.0, grp), axis=1, keepdims=True), dbs)
            tril = lax.broadcasted_iota(jnp.int32, (CHUNK, CHUNK), 0) >= lax.broadcasted_iota(jnp.int32, (CHUNK, CHUNK), 1)
            for h in range(2 * n_pairs):
                block = small_ref[0:CHUNK, h * CHUNK:(h + 1) * CHUNK]
                small_ref[0:CHUNK, h * CHUNK:(h + 1) * CHUNK] = jnp.where(tril, block, 0.0)
            small_ref[SMALL_VD + VD_LN2_G:SMALL_VD + VD_LN2_G + 1, :] = dg2_ref[...]
            small_ref[SMALL_VD + VD_LN2_B:SMALL_VD + VD_LN2_B + 1, :] = db2_ref[...]
            small_ref[SMALL_CW:SMALL_CW + kwp, 0:q] = _dot(inv_taps_ref[...], _split(dgf_ref[...], True))
            small_ref[SMALL_LOSS:SMALL_LOSS + SUBLANES, q:q + LANES] = loss_ref[...]
            small_ref[SMALL_BS:SMALL_BS + SUBLANES, q:q + LANES] = jnp.transpose(dbs)[0:SUBLANES]
            stores += _start_copies(copy_sems, [(small_ref.at[pl.ds(0, CHUNK)], small_ws_hbm)], first=2)
            stores += _start_copies(copy_sems, [(small_ref.at[pl.ds(CHUNK, small.shape[0] - CHUNK)], small_rest_hbm)], first=5)
            for cp in stores:
                cp.wait()

    rev = lambda w: pl.BlockSpec((tm, w), lambda i: (n - 1 - i, 0))
    kwp = cwf.shape[0]
    small = jax.ShapeDtypeStruct((_small_rows(kwp), 2 * q), F32)
    small_ins = [wstt, cwf, tabs["fwd"], tabs["fwd_halo"], tabs["shift"], tabs["taps"], tabs["inv_in"], tabs["inv_taps"], vq, vd,
                 *mlp_small]
    return pl.pallas_call(
        body, name="bwd_mix", grid=(n,),
        in_specs=[rev(d), rev(saved_f32.shape[1]), rev(saved_bf16.shape[1]),
                  pl.BlockSpec((nb * 2 * DFT_F, q), lambda i: (n - 1 - i, 0)), ANY, ANY] + [_full(a.shape) for a in small_ins] + [ANY],
        out_specs=[rev(d)] + [ANY] * 6,
        out_shape=[jax.ShapeDtypeStruct((t, d), F32), jax.ShapeDtypeStruct(wi.shape, F32), jax.ShapeDtypeStruct(wo.shape, F32),
                   jax.ShapeDtypeStruct((CHUNK, 2 * q), F32), jax.ShapeDtypeStruct((small.shape[0] - CHUNK, 2 * q), F32),
                   jax.ShapeDtypeStruct(wi.shape, BF16), jax.ShapeDtypeStruct(wo.shape, BF16)],
        scratch_shapes=[pltpu.VMEM(wi.shape, BF16), pltpu.VMEM(wo.shape, BF16), pltpu.VMEM(wi.shape, F32), pltpu.VMEM(wo.shape, F32),
                        pltpu.VMEM((nb, DFT_N, q), F32), pltpu.VMEM((HALO, q), F32),
                        pltpu.VMEM((CHUNK, q), F32), pltpu.VMEM((2 * DFT_F, q), F32), pltpu.VMEM((2 * DFT_F, q), F32),
                        pltpu.VMEM(small.shape, F32), pltpu.SemaphoreType.DMA((6,))],
        compiler_params=_cparams(),
    )(dx1, saved_f32, saved_bf16, hf_s, wi, wo, *small_ins, token)


def _coords():
    return tuple(lax.axis_index(a) for a in MESH_AXES)


def _other_chips(x, y):
    return [(1 - x, y), (x, 1 - y), (1 - x, 1 - y)]


def _remote(src, dst, send_sem, recv_sem, to):
    return pltpu.make_async_remote_copy(src_ref=src, dst_ref=dst, send_sem=send_sem, recv_sem=recv_sem,
                                        device_id=to, device_id_type=MESH_ID)


class _Gather:
    def __init__(self, bufs, send_sems, recv_sems, own=None):
        self.bufs, self.send_sems, self.recv_sems, self.own = bufs, send_sems, recv_sems, own
        self.x, self.y, self.c = _coords()

    def _copies(self, stage):
        x, y, c = self.x, self.y, self.c
        for a, buf in enumerate(self.bufs):
            hr = buf.shape[1] // 2
            for j, chip in enumerate(_other_chips(x, y)):
                if stage == "ici_out":
                    ref, k, to = buf.at[2 * x + y, pl.ds(c * hr, hr)], j, (*chip, c)
                    if self.own:
                        yield _remote(self.own[a].at[pl.ds(c * hr, hr)], ref, self.send_sems.at[a, k], self.recv_sems.at[a, k], to)
                        continue
                elif stage == "ici_in":
                    ref, k, to = buf.at[2 * chip[0] + chip[1], pl.ds(c * hr, hr)], j, (*chip, c)
                elif stage == "d2d_out":
                    ref, k, to = buf.at[2 * chip[0] + chip[1], pl.ds(c * hr, hr)], 3 + j, (x, y, 1 - c)
                else:
                    ref, k, to = buf.at[2 * chip[0] + chip[1], pl.ds((1 - c) * hr, hr)], 3 + j, (x, y, 1 - c)
                yield _remote(ref, ref, self.send_sems.at[a, k], self.recv_sems.at[a, k], to)

    def start(self):
        for cp in self._copies("ici_out"):
            cp.start()

    def forward(self):
        for landed, onward in zip(self._copies("ici_in"), self._copies("d2d_out")):
            landed.wait_recv()
            onward.start()

    def finish(self):
        for cp in self._copies("d2d_in"):
            cp.wait_recv()
        for stage in ("ici_out", "d2d_out"):
            for cp in self._copies(stage):
                cp.wait_send()


def _gather_sems(n):
    return [pltpu.SemaphoreType.DMA((n, 6)), pltpu.SemaphoreType.DMA((n, 6))]


def _prep_gather(gathered, local, conv_w, kwp):
    kw, _, cshard = conv_w.shape
    shards = list(gathered) + list(local)
    n, n_g = len(shards), len(gathered)

    def body(*refs):
        src, cw_ref, out = refs[:n], refs[n], refs[n + 1:2 * n + 2]
        wide, narrow, taps = refs[2 * n + 2:3 * n + 2], refs[3 * n + 2:4 * n + 2], refs[4 * n + 2]
        load_sems, store_sems, send_sems, recv_sems = refs[4 * n + 3:]
        x, y, _ = _coords()
        loads = [pltpu.make_async_copy(src[a], wide[a], load_sems.at[a]) for a in range(n)]
        stores = [pltpu.make_async_copy(narrow[a], out[a].at[2 * x + y], store_sems.at[a]) for a in range(n)]
        stores.append(pltpu.make_async_copy(taps, out[n].at[2 * x + y], store_sems.at[n]))
        for cp in loads:
            cp.start()

        def cast(a):
            loads[a].wait()
            narrow[a][...] = wide[a][...].astype(BF16)
            stores[a].start()

        for a in range(n_g):
            cast(a)
        for tap in range(kw):
            taps[tap:tap + 1, :] = cw_ref[tap]
        taps[kw:kwp, :] = jnp.zeros((kwp - kw, cshard), F32)
        stores[n].start()
        g = _Gather(list(out[:n_g]) + [out[n]], send_sems, recv_sems, own=list(narrow[:n_g]) + [taps])
        g.start()
        for a in range(n_g, n):
            cast(a)
        g.forward()
        g.finish()
        for cp in stores:
            cp.wait()

    out_shape = [jax.ShapeDtypeStruct((4,) + a.shape, BF16) for a in shards] + [jax.ShapeDtypeStruct((4, kwp, cshard), F32)]
    scratch = ([pltpu.VMEM(a.shape, F32) for a in shards] + [pltpu.VMEM(a.shape, BF16) for a in shards] + [pltpu.VMEM((kwp, cshard), F32)]
               + [pltpu.SemaphoreType.DMA((n,)), pltpu.SemaphoreType.DMA((n + 1,))] + _gather_sems(n_g + 1))
    res = pl.pallas_call(
        body, name="prep_gather", in_specs=[ANY] * n + [pl.BlockSpec(memory_space=pltpu.VMEM)], out_specs=[ANY] * (n + 1),
        out_shape=out_shape, scratch_shapes=scratch, compiler_params=pltpu.CompilerParams(vmem_limit_bytes=VMEM_LIMIT),
    )(*shards, conv_w)
    return list(res[:n_g]), list(res[n_g:n]), res[n]


def _pair_reduce(name, partials, payloads, c_arr, out_dtypes):
    n = len(partials)
    counts = [g.shape[0] for g in partials]
    first = [sum(counts[:a]) for a in range(n)]
    steps = sum(counts)

    def body(c_ref, *refs):
        own, travelling, out, land = (refs[k * n:(k + 1) * n] for k in range(4))
        send_sems, recv_sems = refs[4 * n:]
        i = pl.program_id(0)
        x, y, c = _coords()

        def copy(a, q):
            return _remote(travelling[a].at[q, 1 - c], land[a].at[q], send_sems.at[first[a] + q], recv_sems.at[first[a] + q],
                           (x, y, 1 - c))

        blocks = [(a, q) for a in range(n) for q in range(counts[a])]

        @pl.when(i == 0)
        def _():
            for a, q in blocks:
                copy(a, q).start()

        for a in range(n):
            @pl.when((i >= first[a]) & (i < first[a] + counts[a]))
            def _(a=a):
                q = i - first[a]
                copy(a, q).wait_recv()
                out[a][...] = (own[a][...] + land[a][q].astype(F32)).astype(out_dtypes[a])

        @pl.when(i == steps - 1)
        def _():
            for a, q in blocks:
                copy(a, q).wait_send()

    at = lambda a, i: jnp.clip(i - first[a], 0, counts[a] - 1)
    in_specs = [pl.BlockSpec((None, None) + g.shape[2:], lambda i, cr, a=a: (at(a, i), cr[0], 0, 0)) for a, g in enumerate(partials)]
    out_specs = [pl.BlockSpec((None,) + g.shape[2:], lambda i, cr, a=a: (at(a, i), 0, 0)) for a, g in enumerate(partials)]
    grid_spec = pltpu.PrefetchScalarGridSpec(
        num_scalar_prefetch=1, grid=(steps,), in_specs=in_specs + [ANY] * n, out_specs=out_specs,
        scratch_shapes=[pltpu.VMEM((g.shape[0],) + g.shape[2:], p.dtype) for g, p in zip(partials, payloads)]
        + [pltpu.SemaphoreType.DMA((steps,)), pltpu.SemaphoreType.DMA((steps,))])
    out_shape = [jax.ShapeDtypeStruct((g.shape[0],) + g.shape[2:], dt) for g, dt in zip(partials, out_dtypes)]
    return list(pl.pallas_call(body, name=name, grid_spec=grid_spec, out_shape=out_shape, compiler_params=_cparams())(
        c_arr, *partials, *payloads))


class _Exchange:
    def __init__(self, src, dst, send_sems, recv_sems):
        self.src, self.dst, self.send_sems, self.recv_sems = src, dst, send_sems, recv_sems
        self.x, self.y, self.c = _coords()

    def _copies(self, incoming):
        x, y, c = self.x, self.y, self.c
        for a, (s, d) in enumerate(zip(self.src, self.dst)):
            for j, chip in enumerate(_other_chips(x, y)):
                slot = 2 * chip[0] + chip[1]
                if incoming:
                    out, into = d.at[slot], d.at[slot]
                else:
                    out, into = (s.at[slot] if len(s.shape) == 3 else s), d.at[2 * x + y]
                yield _remote(out, into, self.send_sems.at[a, j], self.recv_sems.at[a, j], (*chip, c))

    def start(self):
        for cp in self._copies(False):
            cp.start()

    def finish(self):
        for cp in self._copies(True):
            cp.wait_recv()
        for cp in self._copies(False):
            cp.wait_send()


def _exchange_shapes(arrs):
    return [jax.ShapeDtypeStruct((4,) + s.shape[-2:], s.dtype) for s in arrs]


class _FlatSems:
    def __init__(self, ref):
        self.ref = ref

    @property
    def at(self):
        return self

    def __getitem__(self, idx):
        return self.ref.at[3 * idx[0] + idx[1]]


HBM = pl.BlockSpec(memory_space=pltpu.HBM)
SEM = pl.BlockSpec(memory_space=pltpu.SEMAPHORE)
DATAFLOW = pltpu.SideEffectType.DATAFLOW_SIDE_EFFECTING


def _exchange_start(name, arrs):
    n = len(arrs)
    lands = _exchange_shapes(arrs)

    def body(*refs):
        src, land = refs[:n], refs[n:2 * n]
        send_sems, recv_sems = refs[2 * n:2 * n + 2]
        token = refs[-1]
        _Exchange(src, land, _FlatSems(send_sems), _FlatSems(recv_sems)).start()
        token[...] = jnp.zeros_like(token)

    hbm = lambda a: pltpu.with_memory_space_constraint(a, pltpu.HBM)
    outs = pl.pallas_call(
        body, name=name,
        out_shape=(pltpu.SemaphoreType.DMA((3 * n,)), pltpu.SemaphoreType.DMA((3 * n,)),
                   *[pltpu.HBM(a.shape, a.dtype) for a in arrs], *[pltpu.HBM(s.shape, s.dtype) for s in lands],
                   jax.ShapeDtypeStruct((SUBLANES, LANES), F32)),
        in_specs=[HBM] * (2 * n), out_specs=(SEM, SEM, *[HBM] * (2 * n), pl.BlockSpec(memory_space=pltpu.VMEM)),
        input_output_aliases={a: 2 + a for a in range(2 * n)},
        compiler_params=pltpu.CompilerParams(has_side_effects=DATAFLOW),
    )(*[hbm(a) for a in arrs], *[hbm(lax.empty(s.shape, s.dtype)) for s in lands])
    return outs[:-1], outs[-1]


def _exchange_wait(name, started, after):
    send_sems, recv_sems, *bufs = started
    n = len(bufs) // 2

    def body(*refs):
        src, land = refs[:n], refs[n:2 * n]
        send_sems, recv_sems = refs[2 * n:2 * n + 2]
        _Exchange(src, land, _FlatSems(send_sems), _FlatSems(recv_sems)).finish()

    outs = pl.pallas_call(
        body, name=name,
        out_shape=tuple(pltpu.HBM(b.shape, b.dtype) for b in bufs),
        in_specs=[HBM] * (2 * n) + [SEM, SEM] + [ANY] * len(after), out_specs=tuple([HBM] * (2 * n)),
        input_output_aliases={a: a for a in range(2 * n)},
        compiler_params=pltpu.CompilerParams(has_side_effects=DATAFLOW),
    )(*bufs, send_sems, recv_sems, *after)
    return list(outs[:n]), list(outs[n:])


def _chip_reduce(name, parts, owns, after):
    n = len(parts)

    def body(*refs):
        part, own = refs[:n], refs[n:2 * n]
        mine, other = refs[2 * n + len(after):3 * n + len(after)], refs[3 * n + len(after):4 * n + len(after)]
        scratch = refs[4 * n + len(after):]
        landed, own_part, total = scratch[:n], scratch[n:2 * n], scratch[2 * n:3 * n]
        load_sems, store_sems, send_sems, recv_sems = scratch[3 * n:]
        x, y, c = _coords()
        me = 2 * x + y
        loads, stores, sends = [], [], []
        for a in range(n):
            loads.append((pltpu.make_async_copy(part[a], landed[a], load_sems.at[a, 0]),
                          pltpu.make_async_copy(own[a].at[me] if len(own[a].shape) == 3 else own[a], own_part[a], load_sems.at[a, 1])))
            stores.append(pltpu.make_async_copy(total[a], mine[a], store_sems.at[a]))
            sends.append(_remote(total[a], other[a], send_sems.at[a], recv_sems.at[a], (x, y, 1 - c)))
        for both in loads:
            for cp in both:
                cp.start()
        for a in range(n):
            for cp in loads[a]:
                cp.wait()
            for chip in range(4):
                @pl.when(me == chip)
                def _(a=a, chip=chip):
                    term = lambda j: (own_part[a] if j == chip else landed[a].at[j])[...].astype(F32)
                    total[a][...] = ((term(0) + term(1)) + term(2)) + term(3)
            stores[a].start()
            sends[a].start()
        for a in range(n):
            stores[a].wait()
            sends[a].wait()

    halves = [jax.ShapeDtypeStruct(p.shape[1:], F32) for p in parts]
    scratch = ([pltpu.VMEM(p.shape, p.dtype) for p in parts] + [pltpu.VMEM(p.shape[1:], o.dtype) for p, o in zip(parts, owns)]
               + [pltpu.VMEM(p.shape[1:], F32) for p in parts]
               + [pltpu.SemaphoreType.DMA((n, 2))] + [pltpu.SemaphoreType.DMA((n,))] * 3)
    res = pl.pallas_call(
        body, name=name, in_specs=[ANY] * (2 * n + len(after)), out_specs=[ANY] * (2 * n), out_shape=halves + halves,
        scratch_shapes=scratch, compiler_params=pltpu.CompilerParams(vmem_limit_bytes=VMEM_LIMIT),
    )(*parts, *owns, *after)
    return list(res[:n]), list(res[n:])


def _row_block(rows, cols, limit=1 << 20):
    best = 8
    for tr in range(8, rows + 1, 8):
        if rows % tr == 0 and tr * cols * 4 <= limit:
            best = tr
    return best


def _adamw(name, w, g_mine, g_other, m, v, c_arr):
    r, c = w.shape
    hr, cg = g_mine.shape
    tr = hr if r % hr == 0 and hr * cg * 4 <= (3 << 19) else math.gcd(_row_block(hr, cg), r)
    per_half = hr // tr
    bc1 = 1.0 - ADAM_B1 ** ADAM_STEP
    bc2 = 1.0 - ADAM_B2 ** ADAM_STEP

    def body(c_ref, w_ref, gm_ref, go_ref, m_ref, v_ref, go, do, mo, vo):
        gv = jnp.where(pl.program_id(0) // per_half == c_ref[0], gm_ref[:, 0:c], go_ref[:, 0:c])
        mn = ADAM_B1 * m_ref[...] + (1.0 - ADAM_B1) * gv
        vn = ADAM_B2 * v_ref[...] + (1.0 - ADAM_B2) * (gv * gv)
        go[...] = gv
        mo[...] = mn
        vo[...] = vn
        do[...] = -ADAM_LR * ((mn / bc1) / (jnp.sqrt(vn / bc2) + ADAM_EPS) + ADAM_WD * w_ref[...])

    blk = pl.BlockSpec((tr, c), lambda i, cr: (i, 0))
    gblk = pl.BlockSpec((tr, cg), lambda i, cr: (i % per_half, 0))
    grid_spec = pltpu.PrefetchScalarGridSpec(num_scalar_prefetch=1, grid=(r // tr,), in_specs=[blk, gblk, gblk, blk, blk],
                                             out_specs=[blk] * 4)
    return pl.pallas_call(body, name=f"adamw_{name}", grid_spec=grid_spec, out_shape=[jax.ShapeDtypeStruct((r, c), F32)] * 4,
                          compiler_params=_cparams())(c_arr, w, g_mine, g_other, m, v)


SMALL_Q = ("sgu_ln_g", "sgu_ln_b", "conv_b", "conv_ln_g", "conv_ln_b")
SMALL_D = ("ln1_g", "ln1_b", "ln2_g", "ln2_b")


def _adamw_small(g_mine, g_other, c_arr, me_arr, params):
    names = list(SMALL_Q) + list(SMALL_D) + ["w_s", "b_s", "conv_w"]
    pieces = len(g_mine)
    q = g_mine[0].shape[1] // 2
    heads = params["w_s"][0].shape[1]
    kw, _, cshard = params["conv_w"][0].shape
    bc1 = 1.0 - ADAM_B1 ** ADAM_STEP
    bc2 = 1.0 - ADAM_B2 ** ADAM_STEP

    def update(w, g, m, v):
        mn = ADAM_B1 * m + (1.0 - ADAM_B1) * g
        vn = ADAM_B2 * v + (1.0 - ADAM_B2) * (g * g)
        return g, -ADAM_LR * ((mn / bc1) / (jnp.sqrt(vn / bc2) + ADAM_EPS) + ADAM_WD * w), mn, vn

    def body(c_ref, me_ref, *refs):
        mine, other, refs = refs[:pieces], refs[pieces:2 * pieces], refs[2 * pieces:]
        ins = {nm: refs[3 * k:3 * k + 3] for k, nm in enumerate(names)}
        outs = {nm: refs[3 * len(names) + 4 * k:3 * len(names) + 4 * k + 4] for k, nm in enumerate(names)}
        loss_ref, cw_ref = refs[-2:]
        low = c_ref[0] == 0
        in_order = lambda first, second: [jnp.where(low, first, second), jnp.where(low, second, first)]
        g_all = jnp.concatenate([half for a in range(pieces) for half in in_order(mine[a][...], other[a][...])], axis=0)

        def apply(nm, g, at):
            w, m, v = (r[at] for r in ins[nm])
            for o, val in zip(outs[nm], update(w, g, m, v)):
                o[at] = val

        for row, nm in enumerate(SMALL_Q):
            apply(nm, g_all[SMALL_VQ + row:SMALL_VQ + row + 1, q:2 * q], ...)
        for row, nm in enumerate(SMALL_D):
            apply(nm, g_all[SMALL_VD + row:SMALL_VD + row + 1, :], ...)
        for h in range(heads):
            apply("w_s", g_all[0:CHUNK, h * CHUNK:(h + 1) * CHUNK], (0, h))
        apply("b_s", g_all[SMALL_BS:SMALL_BS + heads, q:q + LANES], 0)
        cw_ref[...] = jnp.zeros_like(cw_ref)
        for chip in range(4):
            @pl.when(me_ref[0] == chip)
            def _():
                cw_ref[...] = g_all[SMALL_CW:SMALL_CW + cw_ref.shape[0], chip * cshard:(chip + 1) * cshard]
        for tap in range(kw):
            apply("conv_w", cw_ref[tap:tap + 1, :], tap)
        loss_ref[...] = g_all[SMALL_LOSS:SMALL_LOSS + 1, q:q + 1]

    arrays = [a for nm in names for a in params[nm]]
    out_shape = [jax.ShapeDtypeStruct(params[nm][0].shape, F32) for nm in names for _ in range(4)] + [jax.ShapeDtypeStruct((1, 1), F32)]
    whole = lambda shape: pl.BlockSpec(shape, lambda i, c, me: (0,) * len(shape))
    grid_spec = pltpu.PrefetchScalarGridSpec(
        num_scalar_prefetch=2, grid=(1,),
        in_specs=[whole(a.shape) for a in [*g_mine, *g_other, *arrays]],
        out_specs=[whole(s.shape) for s in out_shape],
        scratch_shapes=[pltpu.VMEM((-(-kw // SUBLANES) * SUBLANES, cshard), F32)])
    res = pl.pallas_call(body, name="adamw_small", grid_spec=grid_spec, out_shape=out_shape, compiler_params=_cparams())(
        c_arr, me_arr, *g_mine, *g_other, *arrays)
    return {nm: list(res[4 * k:4 * k + 4]) for k, nm in enumerate(names)}, res[-1]


def _pad_rows(a, rows):
    return jnp.pad(a, ((0, rows - a.shape[0]), (0, 0)))


def kernel(x, w_in, sgu_ln_g, sgu_ln_b, w_s, b_s, conv_w, conv_b, conv_ln_g, conv_ln_b, w_out, ln1_g, ln1_b, w_gate, w_up, w_down, ln2_g, ln2_b, loss_target, m_w_in, m_sgu_ln_g, m_sgu_ln_b, m_w_s, m_b_s, m_conv_w, m_conv_b, m_conv_ln_g, m_conv_ln_b, m_w_out, m_ln1_g, m_ln1_b, m_w_gate, m_w_up, m_w_down, m_ln2_g, m_ln2_b, v_w_in, v_sgu_ln_g, v_sgu_ln_b, v_w_s, v_b_s, v_conv_w, v_conv_b, v_conv_ln_g, v_conv_ln_b, v_w_out, v_ln1_g, v_ln1_b, v_w_gate, v_w_up, v_w_down, v_ln2_g, v_ln2_b):
    depth, d, q = w_in.shape
    assert depth == 1 and x.shape[0] == 1
    t = x.shape[1]
    heads = w_s.shape[1]
    kw, cshard = conv_w.shape[1], conv_w.shape[2]
    fs = w_gate.shape[2]
    slabs = _hidden_slabs(4 * fs)
    n_pairs = q // LANES
    assert heads * HEAD_DIM == q and q % LANES == 0 and w_s.shape[2] == CHUNK and 4 * cshard == q and kw - 1 <= HALO
    alpha = (2.0 * depth) ** 0.25
    tm = min(512, t)
    assert t % tm == 0 and tm % CHUNK == 0
    x2, tgt = x[0], loss_target[0]
    mx, my, mc = _coords()
    me = 2 * mx + my
    c_arr = jnp.reshape(mc, (1,)).astype(jnp.int32)

    kwp = -(-kw // 16) * 16
    me_arr = jnp.reshape(me, (1,)).astype(jnp.int32)
    (wi, wo), (wg, wu, wd), cw4 = _prep_gather([w_in[0], w_out[0]], [w_gate[0].T, w_up[0].T, w_down[0]],
                                              jnp.transpose(conv_w, (1, 0, 2)), kwp)
    wo = wo.reshape(d, d)
    cw = jnp.transpose(cw4, (1, 0, 2)).reshape(kwp, q)
    cwf = _pad_rows(cw[:kw][::-1], kwp)
    tabs = {name: jnp.asarray(tab) for name, tab in _dft_tables(kw, kwp, q).items()}

    wm = jnp.where(jnp.tril(jnp.ones((CHUNK, CHUNK), bool)), w_s[0], 0.0)
    wst = wm.reshape(n_pairs, 2 * CHUNK, CHUNK).astype(BF16)
    wstt = jnp.transpose(wm, (0, 2, 1)).reshape(n_pairs, 2 * CHUNK, CHUNK).astype(BF16)
    bmat = jnp.repeat(b_s[0].T, HEAD_DIM, axis=1)
    vq = _pad_rows(jnp.concatenate([sgu_ln_g, sgu_ln_b, conv_b, conv_ln_g, conv_ln_b], axis=0), 8)
    vd = _pad_rows(jnp.concatenate([ln1_g, ln1_b, ln2_g, ln2_b], axis=0), 8)

    *saved, wg, wu, wd = _fwd_mix(x2, wi, wo, wst, bmat, cwf, tabs, vq, vd, [wg, wu, wd], alpha, tm)
    wg, wu, wd = (w.reshape(4 * fs, d) for w in (wg, wu, wd))
    *acts, x1b, dr2, loss_part, dg2, db2 = _fwd_mlp(saved[0], tgt, wg, wu, wd, vd, alpha, slabs, tm)
    mlp_grads = None
    for j, slab in enumerate(slabs):
        mlp_grads = _bwd_mlp_slab(j, slab, dr2, mlp_grads, x1b, acts[j], wg, wu, wd, alpha, tm)
    dx1 = mlp_grads[0]
    mlp_halves = [b.reshape(4, 2, fs // 2, d) for b in mlp_grads[1:]]
    mlp_sums = _pair_reduce("pair_reduce_mlp", mlp_halves[3:], mlp_halves[3:], c_arr, [BF16] * 3)
    mlp_started, token = _exchange_start("exchange_mlp_start", mlp_sums)
    grad_x, dwi, dwo, *small, dwi16, dwo16 = _bwd_mix(dx1, saved, wi, wo, wstt, cwf, tabs, vq, vd, (loss_part, dg2, db2), token, alpha, tm)
    mlp_sums, mlp_parts = _exchange_wait("exchange_mlp_wait", mlp_started, [dwo])

    by_halves = lambda b: b.reshape(4, 2, b.shape[1] // 2, b.shape[2])
    halves = [by_halves(dwi16), by_halves(dwo16.reshape(4, d // 4, d))] + [s.reshape(1, 2, s.shape[0] // 2, s.shape[1]) for s in small]
    travelling = halves
    *sums, ws_sum, rest_sum = _pair_reduce("pair_reduce_mix", halves, travelling, c_arr, [BF16, BF16, BF16, F32])
    sums += [ws_sum[0], rest_sum[0]]
    mix_started, token = _exchange_start("exchange_mix_start", sums)

    out, raw = {}, {}

    def finish(first, names, parts, sums, after):
        mine, other = _chip_reduce(f"chip_reduce_{first}", parts, sums, [after])
        for a, nm in enumerate(names):
            w_, m_, v_ = weights[nm]
            if nm in ("w_gate", "w_up"):
                raw[nm] = _adamw(nm, w_[0].T, mine[a], other[a], m_[0].T, v_[0].T, c_arr)
                out[nm] = [o.T for o in raw[nm]]
            else:
                raw[nm] = out[nm] = _adamw(nm, w_[0], mine[a], other[a], m_[0], v_[0], c_arr)
        return mine[len(names):], other[len(names):]

    weights = {"w_in": (w_in, m_w_in, v_w_in), "w_out": (w_out, m_w_out, v_w_out), "w_gate": (w_gate, m_w_gate, v_w_gate),
               "w_up": (w_up, m_w_up, v_w_up), "w_down": (w_down, m_w_down, v_w_down)}
    finish(2, ["w_gate", "w_up", "w_down"], mlp_parts, mlp_sums, token)
    sums, parts = _exchange_wait("exchange_mix_wait", mix_started, [raw[nm][1] for nm in ("w_gate", "w_up", "w_down")])
    small_mine, small_other = finish(5, ["w_in", "w_out"], parts, sums, parts[0])

    small_params = {
        "sgu_ln_g": (sgu_ln_g, m_sgu_ln_g, v_sgu_ln_g), "sgu_ln_b": (sgu_ln_b, m_sgu_ln_b, v_sgu_ln_b),
        "conv_b": (conv_b, m_conv_b, v_conv_b), "conv_ln_g": (conv_ln_g, m_conv_ln_g, v_conv_ln_g),
        "conv_ln_b": (conv_ln_b, m_conv_ln_b, v_conv_ln_b), "ln1_g": (ln1_g, m_ln1_g, v_ln1_g), "ln1_b": (ln1_b, m_ln1_b, v_ln1_b),
        "ln2_g": (ln2_g, m_ln2_g, v_ln2_g), "ln2_b": (ln2_b, m_ln2_b, v_ln2_b), "w_s": (w_s, m_w_s, v_w_s),
        "b_s": (b_s, m_b_s, v_b_s), "conv_w": tuple(jnp.transpose(a, (1, 0, 2)) for a in (conv_w, m_conv_w, v_conv_w))}
    small_out, loss_block = _adamw_small(small_mine, small_other, c_arr, me_arr, small_params)
    small_out["conv_w"] = [jnp.transpose(o, (1, 0, 2)) for o in small_out["conv_w"]]
    loss = loss_block.reshape(())
    names = ["w_in", "sgu_ln_g", "sgu_ln_b", "w_s", "b_s", "conv_w", "conv_b", "conv_ln_g", "conv_ln_b", "w_out",
             "ln1_g", "ln1_b", "w_gate", "w_up", "w_down", "ln2_g", "ln2_b"]
    result = [loss, grad_x[None]]
    for kind in range(4):
        for nm in names:
            result.append(out[nm][kind][None] if nm in out else small_out[nm][kind])
    return tuple(result)
```

```python
import math

import jax
import numpy as np
import jax.numpy as jnp
from jax import lax
from jax.experimental import pallas as pl
from jax.experimental.pallas import tpu as pltpu

F32 = jnp.float32
BF16 = jnp.bfloat16

LN_EPS = 1e-5
HEAD_DIM = 64
CHUNK = 128
HALO = 32
LANES = 128
MXU_N = 256
ADAM_LR, ADAM_B1, ADAM_B2, ADAM_EPS, ADAM_WD, ADAM_STEP = 0.001, 0.9, 0.999, 1e-08, 0.01, 10
VMEM_LIMIT = 63 * 1024 * 1024
MESH_AXES = ("x", "y", "c")
MESH_ID = pl.DeviceIdType.MESH


def _dot(a, b):
    return jnp.dot(a, b, preferred_element_type=F32)


def _dot_nt(a, b):
    return lax.dot_general(a, b, (((1,), (1,)), ((), ())), preferred_element_type=F32)


def _dot_tn(a, b):
    return lax.dot_general(a, b, (((0,), (0,)), ((), ())), preferred_element_type=F32)


def _sigmoid(v):
    return 1.0 / (1.0 + jnp.exp(-v))


def _gelu(v):
    cdf = 0.5 * (1.0 + lax.erf(v * (1.0 / math.sqrt(2.0))))
    pdf = jnp.exp(-0.5 * v * v) * (1.0 / math.sqrt(2.0 * math.pi))
    return v * cdf, cdf + v * pdf


def _ln_stats(v):
    mu = jnp.mean(v, axis=-1, keepdims=True)
    d = v - mu
    rstd = lax.rsqrt(jnp.mean(d * d, axis=-1, keepdims=True) + LN_EPS)
    return d * rstd, rstd


def _ln_bwd(dxhat, xhat, rstd):
    m1 = jnp.mean(dxhat, axis=-1, keepdims=True)
    m2 = jnp.mean(dxhat * xhat, axis=-1, keepdims=True)
    return rstd * (dxhat - m1 - xhat * m2)


def _colsum(v):
    return jnp.sum(v, axis=0, keepdims=True)


def _pair_lanes(v, nc, p):
    return jnp.concatenate([v[c * CHUNK:(c + 1) * CHUNK, p * LANES:(p + 1) * LANES] for c in range(nc)], axis=1)


def _unpair(parts, nc):
    rows = [jnp.concatenate([part[:, c * LANES:(c + 1) * LANES] for part in parts], axis=1) for c in range(nc)]
    return jnp.concatenate(rows, axis=0)


def _low_head(nc):
    lane = lax.broadcasted_iota(jnp.int32, (CHUNK, nc * LANES), 1)
    return (lane & (LANES - 1)) < HEAD_DIM


def _mix(wst_ref, v, nc, n_pairs):
    vb = v.astype(BF16)
    low = _low_head(nc)
    parts = []
    for p in range(n_pairs):
        r = _dot(wst_ref[p], _pair_lanes(vb, nc, p))
        parts.append(jnp.where(low, r[:CHUNK], r[CHUNK:]))
    return _unpair(parts, nc)


def _mix_wgrad(dm, vn, nc, n_pairs):
    low = _low_head(nc)
    vb = vn.astype(BF16)
    out = []
    for p in range(n_pairs):
        a = _pair_lanes(dm, nc, p)
        lhs = jnp.concatenate([jnp.where(low, a, 0.0), jnp.where(low, 0.0, a)], axis=0).astype(BF16)
        out.append(_dot_nt(lhs, _pair_lanes(vb, nc, p)))
    return out


SUBLANES = 8


CONV_BLOCK = 256
DFT_N = CONV_BLOCK + HALO
DFT_F = -(-(DFT_N // 2 + 1) // SUBLANES) * SUBLANES


def _terms(m, exact):
    hi = m.astype(np.float32).astype(BF16)
    lo = (m.astype(np.float32) - hi.astype(np.float32)).astype(BF16)
    return np.concatenate([hi, hi, lo] if exact else [hi], axis=1)


def _split(v, exact=False):
    hi = v.astype(BF16)
    if not exact:
        return hi
    lo = (v - hi.astype(F32)).astype(BF16)
    return jnp.concatenate([hi, lo, hi], axis=0)


def _dft_tables(kw, kwp, q):
    nf = DFT_N // 2 + 1
    ang = 2.0 * np.pi * np.arange(nf)[:, None] * np.arange(DFT_N)[None, :] / DFT_N
    fwd = np.zeros((2 * DFT_F, DFT_N))
    fwd[:nf], fwd[DFT_F:DFT_F + nf] = np.cos(ang), -np.sin(ang)
    weight = np.full((nf, 1), 2.0 / DFT_N)
    weight[0] = weight[-1] = 1.0 / DFT_N
    inv = np.zeros((DFT_N, 2 * DFT_F))
    inv[:, :nf], inv[:, DFT_F:DFT_F + nf] = (np.cos(ang) * weight).T, (-np.sin(ang) * weight).T
    inv_taps = np.zeros((kwp, 2 * DFT_F))
    inv_taps[:kw] = inv[kw - 1::-1][:kw]
    shift = np.zeros((2 * DFT_F, q), np.float32)
    shift[:nf], shift[DFT_F:DFT_F + nf] = np.cos(ang[:, HALO:HALO + 1]), -np.sin(ang[:, HALO:HALO + 1])
    return {"fwd": _terms(fwd, False), "fwd_halo": _terms(fwd[:, CONV_BLOCK:], False), "shift": shift,
            "inv_out": _terms(inv[HALO:HALO + CONV_BLOCK], False), "inv_in": _terms(inv[:CONV_BLOCK], False),
            "taps": _terms(fwd[:, :kwp], True), "inv_taps": _terms(inv_taps, True)}


def _cmul(a, b, conj_b=False):
    ar, ai, br, bi = a[:DFT_F], a[DFT_F:], b[:DFT_F], b[DFT_F:]
    if conj_b:
        return jnp.concatenate([ar * br + ai * bi, ai * br - ar * bi], axis=0)
    return jnp.concatenate([ar * br - ai * bi, ar * bi + ai * br], axis=0)


def _interleave(sub_tiles):
    waiting, live = list(sub_tiles), []
    while waiting or live:
        if waiting:
            live.append(waiting.pop(0))
        for g in list(live):
            try:
                next(g)
            except StopIteration:
                live.remove(g)


def _start_copies(sems, pairs, first=0):
    copies = [pltpu.make_async_copy(src, dst, sems.at[first + k]) for k, (src, dst) in enumerate(pairs)]
    for cp in copies:
        cp.start()
    return copies


def _cparams():
    return pltpu.CompilerParams(dimension_semantics=("arbitrary",), vmem_limit_bytes=VMEM_LIMIT)


def _full(shape):
    return pl.BlockSpec(shape, lambda i: (0,) * len(shape))


ANY = pl.BlockSpec(memory_space=pl.ANY)

VQ_SGU_G, VQ_SGU_B, VQ_CONV_B, VQ_CLN_G, VQ_CLN_B = range(5)
VD_LN1_G, VD_LN1_B, VD_LN2_G, VD_LN2_B = range(4)
RS_LN1, RS_SGU, RS_CONV = range(3)
RS_COLS = LANES


def _saved_widths(d, q):
    f32 = [d, q, q, q, q, q, RS_COLS]
    bf16 = [d, 2 * q, d, q]
    return f32, bf16


def _saved_views(f32_ref, bf16_ref, d, q):
    views = []
    for ref, widths in zip((f32_ref, bf16_ref), _saved_widths(d, q)):
        for k, w in enumerate(widths):
            views.append(ref.at[pl.ds(0, ref.shape[0]), pl.ds(sum(widths[:k]), w)])
    return views
def _fwd_mix(x, wi, wo, wst, bmat, cwf, tabs, vq, vd, mlp_w, alpha, tm):
    t, d = x.shape
    q = wi.shape[2]
    nc, n_pairs = CONV_BLOCK // CHUNK, q // LANES
    n = t // tm
    n_in, n_saved = 11, 3
    assert tm % CONV_BLOCK == 0

    def body(x_ref, wi_hbm, wo_hbm, wst_ref, bmat_ref, cwf_ref, fwd_ref, taps_ref, inv_ref, vq_ref, vd_ref, *rest):
        f32_ref, bf16_ref, hf_ref = rest[3:3 + n_saved]
        xh_ref, zu_ref, mg_ref, vhat_ref, gv_ref, yhat_ref, rs_ref, xb_ref, pag_ref, y_ref, vnb_ref = _saved_views(f32_ref, bf16_ref, d, q)
        gathered = rest[3 + n_saved:6 + n_saved]
        wi_v, wo_v, hb_ref, gf_ref, send_sems, recv_sems, copy_sems = rest[6 + n_saved:]
        step = pl.program_id(0)

        @pl.when(step == 0)
        def _():
            loads = _start_copies(copy_sems, [(wi_hbm, wi_v), (wo_hbm, wo_v)])
            _Gather(gathered, send_sems, recv_sems).start()
            hb_ref[...] = jnp.zeros_like(hb_ref)
            gf_ref[...] = _dot(taps_ref[...], _split(cwf_ref[...], True))
            for cp in loads:
                cp.wait()

        @pl.when(step == (3 * n) // 4)
        def _():
            _Gather(gathered, send_sems, recv_sems).forward()

        def sub_tile(b):
            rows = slice(b * CONV_BLOCK, (b + 1) * CONV_BLOCK)
            xv = x_ref[rows, :]
            xb = xv.astype(BF16)
            xb_ref[rows, :] = xb
            pu, pv, pa, pg = (_dot(xb, wi_v[j]) for j in range(4))
            yield
            pag_ref[rows, 0:q] = pa.astype(BF16)
            pag_ref[rows, q:2 * q] = pg.astype(BF16)
            zu, gu = _gelu(pu)
            zv, gv = _gelu(pv)
            vhat, rstd_v = _ln_stats(zv)
            vnb = (vhat * vq_ref[VQ_SGU_G:VQ_SGU_G + 1, :] + vq_ref[VQ_SGU_B:VQ_SGU_B + 1, :]).astype(BF16)
            hb_ref[HALO + b * CONV_BLOCK:HALO + (b + 1) * CONV_BLOCK, :] = pa * _sigmoid(pg)
            yield
            mixed = _mix(wst_ref, vnb, nc, n_pairs) + jnp.concatenate([bmat_ref[...]] * nc, axis=0)
            spectrum = _dot(fwd_ref[...], _split(hb_ref[b * CONV_BLOCK:b * CONV_BLOCK + DFT_N, :]))
            yield
            y_ref[rows, 0:q] = (zu * mixed).astype(BF16)
            zu_ref[rows, :] = zu
            mg_ref[rows, :] = mixed * gu
            vhat_ref[rows, :] = vhat
            gv_ref[rows, :] = gv
            vnb_ref[rows, :] = vnb
            hf_ref[b * 2 * DFT_F:(b + 1) * 2 * DFT_F, :] = spectrum
            product = _split(_cmul(gf_ref[...], spectrum))
            yield
            yc = _dot(inv_ref[...], product) + vq_ref[VQ_CONV_B:VQ_CONV_B + 1, :]
            yield
            yhat, rstd_c = _ln_stats(yc)
            yhat_ref[rows, :] = yhat
            yn = yhat * vq_ref[VQ_CLN_G:VQ_CLN_G + 1, :] + vq_ref[VQ_CLN_B:VQ_CLN_B + 1, :]
            y_ref[rows, q:2 * q] = (yn * _sigmoid(yn)).astype(BF16)
            yield
            r1 = alpha * xv + _dot(y_ref[rows, :], wo_v[...])
            yield
            xhat, rstd1 = _ln_stats(r1)
            xh_ref[rows, :] = xhat
            col = lax.broadcasted_iota(jnp.int32, (CONV_BLOCK, RS_COLS), 1)
            rs_ref[rows, :] = jnp.where(col == RS_LN1, rstd1, jnp.where(col == RS_SGU, rstd_v, jnp.where(col == RS_CONV, rstd_c, 0.0)))

        _interleave([sub_tile(b) for b in range(tm // CONV_BLOCK)])
        hb_ref[0:HALO, :] = hb_ref[tm:tm + HALO, :]

        @pl.when(step == n - 1)
        def _():
            _Gather(gathered, send_sems, recv_sems).finish()

    row = lambda w: pl.BlockSpec((tm, w), lambda i: (i, 0))
    widths = [(sum(w), dt) for w, dt in zip(_saved_widths(d, q), (F32, BF16))]
    small_ins = [wst, bmat, cwf, tabs["fwd"], tabs["taps"], tabs["inv_out"], vq, vd]
    return pl.pallas_call(
        body, name="fwd_mix", grid=(n,),
        in_specs=[row(d), ANY, ANY] + [_full(a.shape) for a in small_ins] + [ANY] * 3,
        out_specs=[row(w) for w, _ in widths] + [pl.BlockSpec((tm // CONV_BLOCK * 2 * DFT_F, q), lambda i: (i, 0))] + [ANY] * 3,
        out_shape=[jax.ShapeDtypeStruct((t, w), dt) for w, dt in widths] + [jax.ShapeDtypeStruct((t // CONV_BLOCK * 2 * DFT_F, q), F32)]
        + [jax.ShapeDtypeStruct(b.shape, b.dtype) for b in mlp_w],
        scratch_shapes=[pltpu.VMEM(wi.shape, BF16), pltpu.VMEM(wo.shape, BF16), pltpu.VMEM((HALO + tm, q), F32),
                        pltpu.VMEM((2 * DFT_F, q), F32)] + _gather_sems(3) + [pltpu.SemaphoreType.DMA((2,))],
        input_output_aliases={n_in + a: n_saved + a for a in range(3)},
        compiler_params=_cparams(),
    )(x, wi, wo, *small_ins, *mlp_w)


MLP_SLABS = 4


def _hidden_slabs(f):
    assert f % MXU_N == 0
    tiles = f // MXU_N
    sizes = [(tiles // MLP_SLABS + (1 if j < tiles % MLP_SLABS else 0)) * MXU_N for j in range(MLP_SLABS)]
    return [(sum(sizes[:j]), sz) for j, sz in enumerate(sizes) if sz]


def _fwd_mlp(saved_f32, tgt, wg, wu, wd, vd, alpha, slabs, tm):
    t, d = tgt.shape
    n = t // tm
    ns = len(slabs)
    half = tm // 2 if tm % 32 == 0 else tm

    def body(xh_ref, tgt_ref, wg_hbm, wu_hbm, wd_hbm, vd_ref, *rest):
        gp_refs = [r.at[pl.ds(0, tm), pl.ds(0, sz)] for r, (_, sz) in zip(rest[:ns], slabs)]
        up_refs = [r.at[pl.ds(0, tm), pl.ds(sz, sz)] for r, (_, sz) in zip(rest[:ns], slabs)]
        x1b_ref, dr2_ref, loss_ref, dg2_ref, db2_ref, wg_v, wu_v, wd_v, copy_sems = rest[ns:]

        @pl.when(pl.program_id(0) == 0)
        def _():
            loads = _start_copies(copy_sems, [(wg_hbm, wg_v), (wu_hbm, wu_v), (wd_hbm, wd_v)])
            loss_ref[...] = jnp.zeros_like(loss_ref)
            dg2_ref[...] = jnp.zeros_like(dg2_ref)
            db2_ref[...] = jnp.zeros_like(db2_ref)
            for cp in loads:
                cp.wait()

        g2 = vd_ref[VD_LN2_G:VD_LN2_G + 1, :]

        for r0 in range(0, tm, half):
            rows = slice(r0, r0 + half)
            x1 = xh_ref[rows, :] * vd_ref[VD_LN1_G:VD_LN1_G + 1, :] + vd_ref[VD_LN1_B:VD_LN1_B + 1, :]
            x1b = x1.astype(BF16)
            x1b_ref[rows, :] = x1b
            acc = alpha * x1
            for (off, sz), gp_ref, up_ref in zip(slabs, gp_refs, up_refs):
                gp = _dot_nt(x1b, wg_v[off:off + sz, :])
                up = _dot_nt(x1b, wu_v[off:off + sz, :])
                gp_ref[rows, :] = gp.astype(BF16)
                up_ref[rows, :] = up.astype(BF16)
                acc = acc + _dot((gp * _sigmoid(gp) * up).astype(BF16), wd_v[off:off + sz, :])
            xh2, rstd2 = _ln_stats(acc)
            err = xh2 * g2 + vd_ref[VD_LN2_B:VD_LN2_B + 1, :] - tgt_ref[rows, :]
            loss_ref[...] += _colsum(jnp.sum(err * err, axis=1, keepdims=True)) * (0.5 / d)
            dy = err * (1.0 / d)
            dg2_ref[...] += _colsum(dy * xh2)
            db2_ref[...] += _colsum(dy)
            dr2_ref[rows, :] = _ln_bwd(dy * g2, xh2, rstd2)

    row = lambda w: pl.BlockSpec((tm, w), lambda i: (i, 0))
    act = [2 * sz for _, sz in slabs]
    return pl.pallas_call(
        body, name="fwd_mlp", grid=(n,),
        in_specs=[row(d), row(d), ANY, ANY, ANY, _full(vd.shape)],
        out_specs=[row(sz) for sz in act] + [row(d), row(d), _full((8, LANES)), _full((1, d)), _full((1, d))],
        out_shape=[jax.ShapeDtypeStruct((t, sz), BF16) for sz in act]
        + [jax.ShapeDtypeStruct((t, d), BF16), jax.ShapeDtypeStruct((t, d), F32),
           jax.ShapeDtypeStruct((8, LANES), F32), jax.ShapeDtypeStruct((1, d), F32), jax.ShapeDtypeStruct((1, d), F32)],
        scratch_shapes=[pltpu.VMEM(wg.shape, BF16), pltpu.VMEM(wu.shape, BF16), pltpu.VMEM(wd.shape, BF16), pltpu.SemaphoreType.DMA((3,))],
        compiler_params=_cparams(),
    )(saved_f32, tgt, wg, wu, wd, vd)


def _bwd_mlp_slab(j, slab, dr2, prev, x1b, gate_up, wg, wu, wd, alpha, tm):
    t, d = dr2.shape
    off, sz = slab
    n = t // tm
    first = prev is None

    def body(*refs):
        if first:
            dr_ref, x1b_ref, gu_ref, wg_hbm, wu_hbm, wd_hbm = refs[:6]
        else:
            dr_ref, dxp_ref, x1b_ref, gu_ref, wg_hbm, wu_hbm, wd_hbm = refs[:7]
        dx_ref, dwg_hbm, dwu_hbm, dwd_hbm, dwg16_hbm, dwu16_hbm, dwd16_hbm, ag, au, ad, wg_v, wu_v, wd_v, copy_sems = refs[-14:]

        @pl.when(pl.program_id(0) == 0)
        def _():
            loads = _start_copies(copy_sems, [(src.at[pl.ds(off, sz)], dst) for src, dst in ((wg_hbm, wg_v), (wu_hbm, wu_v), (wd_hbm, wd_v))])
            ag[...] = jnp.zeros_like(ag)
            au[...] = jnp.zeros_like(au)
            ad[...] = jnp.zeros_like(ad)
            for cp in loads:
                cp.wait()

        dr = dr_ref[...]
        drb = dr.astype(BF16)
        x1b = x1b_ref[...]
        gpv = gu_ref[:, 0:sz].astype(F32)
        upv = gu_ref[:, sz:2 * sz].astype(F32)
        dh = _dot_nt(drb, wd_v[...])
        sg = _sigmoid(gpv)
        silu = gpv * sg
        ad[...] += _dot_tn((silu * upv).astype(BF16), drb)
        dgp = (dh * upv * (sg * (1.0 + gpv * (1.0 - sg)))).astype(BF16)
        dup = (dh * silu).astype(BF16)
        ag[...] += _dot_tn(dgp, x1b)
        au[...] += _dot_tn(dup, x1b)
        base = alpha * dr if first else dxp_ref[...]
        dx_ref[...] = base + _dot(dgp, wg_v[...]) + _dot(dup, wu_v[...])

        @pl.when(pl.program_id(0) == n - 1)
        def _():
            rows = pl.ds(off, sz)
            stores = []
            for acc, stage in ((ag, wg_v), (au, wu_v), (ad, wd_v)):
                stage[...] = acc[...].astype(BF16)
            stores += _start_copies(copy_sems, [(wg_v, dwg16_hbm.at[rows]), (wu_v, dwu16_hbm.at[rows]), (wd_v, dwd16_hbm.at[rows])], first=3)
            for cp in stores:
                cp.wait()


    row = lambda w: pl.BlockSpec((tm, w), lambda i: (i, 0))
    ins = [dr2] + ([] if first else [prev[0]]) + [x1b, gate_up, wg, wu, wd] + ([] if first else list(prev[1:]))
    in_specs = [row(d)] + ([] if first else [row(d)]) + [row(d), row(2 * sz), ANY, ANY, ANY] + ([] if first else [ANY] * 6)
    return pl.pallas_call(
        body, name=f"bwd_mlp_{j}", grid=(n,),
        in_specs=in_specs,
        out_specs=[row(d)] + [ANY] * 6,
        out_shape=[jax.ShapeDtypeStruct((t, d), F32)] + [jax.ShapeDtypeStruct(wg.shape, F32)] * 3 + [jax.ShapeDtypeStruct(wg.shape, BF16)] * 3,
        scratch_shapes=[pltpu.VMEM((sz, d), F32)] * 3 + [pltpu.VMEM((sz, d), BF16)] * 3 + [pltpu.SemaphoreType.DMA((6,))],
        input_output_aliases={} if first else {7 + a: 1 + a for a in range(6)},
        compiler_params=_cparams(),
    )(*ins)


SMALL_VD = CHUNK
SMALL_CW = CHUNK + 8
SMALL_VQ = CHUNK + 8
SMALL_LOSS = CHUNK + 16
SMALL_BS = CHUNK + 24


def _small_rows(kwp):
    return -(-(SMALL_CW + max(kwp, 24 + SUBLANES)) // 16) * 16


def _bwd_mix(dx1, saved, wi, wo, wstt, cwf, tabs, vq, vd, mlp_small, token, alpha, tm):
    saved_f32, saved_bf16, hf_s = saved
    t, d = dx1.shape
    q = wi.shape[2]
    nc, n_pairs = CONV_BLOCK // CHUNK, q // LANES
    n = t // tm
    nb = tm // CONV_BLOCK
    assert tm % CONV_BLOCK == 0

    def body(dx1_ref, f32_ref, bf16_ref, hf_ref,
             wi_hbm, wo_hbm, wstt_ref, cwf_ref, fwd_ref, fwd_halo_ref, shift_ref, taps_ref, inv_ref, inv_taps_ref, vq_ref, vd_ref,
             loss_ref, dg2_ref, db2_ref, token_ref,
             gx_ref, dwi_hbm, dwo_hbm, small_ws_hbm, small_rest_hbm, dwi16_hbm, dwo16_hbm,
             wi_v, wo_v, awi, awo, dyb_ref, later_ref, dbm_ref, gf_ref, dgf_ref, small_ref, copy_sems):
        xh_ref, zu_ref, mg_ref, vhat_ref, gv_ref, yhat_ref, rs_ref, xb_ref, pag_ref, y_ref, vnb_ref = _saved_views(f32_ref, bf16_ref, d, q)
        i = pl.program_id(0)

        @pl.when(i == 0)
        def _():
            loads = _start_copies(copy_sems, [(wi_hbm, wi_v), (wo_hbm, wo_v)])
            for r in (awi, awo, small_ref, dbm_ref, dgf_ref, dyb_ref, later_ref):
                r[...] = jnp.zeros_like(r)
            gf_ref[...] = _dot(taps_ref[...], _split(cwf_ref[...], True))
            for cp in loads:
                cp.wait()

        dr1b_parts, dproj_parts = [None] * nb, [None] * nb

        def sub_tile(b):
            rows = slice(b * CONV_BLOCK, (b + 1) * CONV_BLOCK)
            dx1v = dx1_ref[rows, :]
            xh = xh_ref[rows, :]
            rsv = rs_ref[rows, :]
            small_ref[SMALL_VD + VD_LN1_G:SMALL_VD + VD_LN1_G + 1, :] += _colsum(dx1v * xh)
            small_ref[SMALL_VD + VD_LN1_B:SMALL_VD + VD_LN1_B + 1, :] += _colsum(dx1v)
            dr1 = _ln_bwd(dx1v * vd_ref[VD_LN1_G:VD_LN1_G + 1, :], xh, rsv[:, RS_LN1:RS_LN1 + 1])
            dr1b = dr1.astype(BF16)
            yield
            dy = _dot_nt(dr1b, wo_v[...])
            yield
            vhat = vhat_ref[rows, :]
            sgu_g = vq_ref[VQ_SGU_G:VQ_SGU_G + 1, :]
            doa = dy[:, 0:q]
            dm = doa * zu_ref[rows, :]
            dpu = (doa * mg_ref[rows, :]).astype(BF16)
            acc = dm[0:CHUNK]
            for c in range(1, nc):
                acc = acc + dm[c * CHUNK:(c + 1) * CHUNK]
            dbm_ref[...] += acc
            pa = pag_ref[rows, 0:q].astype(F32)
            sg = _sigmoid(pag_ref[rows, q:2 * q].astype(F32))
            yhat = yhat_ref[rows, :]
            cln_g = vq_ref[VQ_CLN_G:VQ_CLN_G + 1, :]
            yn = yhat * cln_g + vq_ref[VQ_CLN_B:VQ_CLN_B + 1, :]
            sy = _sigmoid(yn)
            dyn = dy[:, q:2 * q] * (sy * (1.0 + yn * (1.0 - sy)))
            small_ref[SMALL_VQ + VQ_CLN_G:SMALL_VQ + VQ_CLN_G + 1, q:2 * q] += _colsum(dyn * yhat)
            small_ref[SMALL_VQ + VQ_CLN_B:SMALL_VQ + VQ_CLN_B + 1, q:2 * q] += _colsum(dyn)
            dyc = _ln_bwd(dyn * cln_g, yhat, rsv[:, RS_CONV:RS_CONV + 1])
            small_ref[SMALL_VQ + VQ_CONV_B:SMALL_VQ + VQ_CONV_B + 1, q:2 * q] += _colsum(dyc)
            dyb_ref[b, 0:CONV_BLOCK, :] = dyc
            yield
            wgrads = _mix_wgrad(dm, vnb_ref[rows, :], nc, n_pairs)
            dvn = _mix(wstt_ref, dm, nc, n_pairs)
            own = _dot(fwd_ref[...], _split(dyb_ref[b]))
            with_later = own + _dot(fwd_halo_ref[...], _split(later_ref[...]))
            later_ref[...] = dyb_ref[b, 0:HALO, :]
            yield
            for p, g in enumerate(wgrads):
                for half in range(2):
                    small_ref[0:CHUNK, (2 * p + half) * CHUNK:(2 * p + half + 1) * CHUNK] += g[half * CHUNK:(half + 1) * CHUNK]
            small_ref[SMALL_VQ + VQ_SGU_G:SMALL_VQ + VQ_SGU_G + 1, q:2 * q] += _colsum(dvn * vhat)
            small_ref[SMALL_VQ + VQ_SGU_B:SMALL_VQ + VQ_SGU_B + 1, q:2 * q] += _colsum(dvn)
            dpv = (_ln_bwd(dvn * sgu_g, vhat, rsv[:, RS_SGU:RS_SGU + 1]) * gv_ref[rows, :]).astype(BF16)
            dgf_ref[...] += _cmul(_cmul(own, shift_ref[...]), hf_ref[b * 2 * DFT_F:(b + 1) * 2 * DFT_F, :], conj_b=True)
            product = _split(_cmul(with_later, gf_ref[...], conj_b=True))
            yield
            dh = _dot(inv_ref[...], product)
            yield
            da = (dh * sg).astype(BF16)
            dg = (dh * pa * (sg * (1.0 - sg))).astype(BF16)
            yield
            gx = alpha * dr1
            for dpj, wj in zip((dpu, dpv, da, dg), range(4)):
                gx = gx + _dot_nt(dpj, wi_v[wj])
            gx_ref[rows, :] = gx
            dr1b_parts[b], dproj_parts[b] = dr1b, (dpu, dpv, da, dg)

        _interleave([sub_tile(b) for b in reversed(range(nb))])

        awo[...] += _dot_tn(y_ref[...], jnp.concatenate(dr1b_parts, axis=0))
        xb = xb_ref[...]
        for j in range(4):
            awi[j] += _dot_tn(xb, jnp.concatenate([part[j] for part in dproj_parts], axis=0))

        @pl.when(i == n - 1)
        def _():
            stores = []
            wi_v[...] = awi[...].astype(BF16)
            wo_v[...] = awo[...].astype(BF16)
            stores += _start_copies(copy_sems, [(wi_v, dwi16_hbm), (wo_v, dwo16_hbm)], first=3)
            lane = lax.broadcasted_iota(jnp.int32, (CHUNK, LANES), 1)
            low = lane < HEAD_DIM
            dbs = jnp.zeros((CHUNK, LANES), F32)
            for p in range(n_pairs):
                grp = dbm_ref[:, p * LANES:(p + 1) * LANES]
                dbs = jnp.where(lane == 2 * p, jnp.sum(jnp.where(low, grp, 0.0), axis=1, keepdims=True), dbs)
                dbs = jnp.where(lane == 2 * p + 1, jnp.sum(jnp.where(low, 0.0, grp), axis=1, keepdims=True), dbs)
            tril = lax.broadcasted_iota(jnp.int32, (CHUNK, CHUNK), 0) >= lax.broadcasted_iota(jnp.int32, (CHUNK, CHUNK), 1)
            for h in range(2 * n_pairs):
                block = small_ref[0:CHUNK, h * CHUNK:(h + 1) * CHUNK]
                small_ref[0:CHUNK, h * CHUNK:(h + 1) * CHUNK] = jnp.where(tril, block, 0.0)
            small_ref[SMALL_VD + VD_LN2_G:SMALL_VD + VD_LN2_G + 1, :] = dg2_ref[...]
            small_ref[SMALL_VD + VD_LN2_B:SMALL_VD + VD_LN2_B + 1, :] = db2_ref[...]
            small_ref[SMALL_CW:SMALL_CW + kwp, 0:q] = _dot(inv_taps_ref[...], _split(dgf_ref[...], True))
            small_ref[SMALL_LOSS:SMALL_LOSS + SUBLANES, q:q + LANES] = loss_ref[...]
            small_ref[SMALL_BS:SMALL_BS + SUBLANES, q:q + LANES] = jnp.transpose(dbs)[0:SUBLANES]
            stores += _start_copies(copy_sems, [(small_ref.at[pl.ds(0, CHUNK)], small_ws_hbm)], first=2)
            stores += _start_copies(copy_sems, [(small_ref.at[pl.ds(CHUNK, small.shape[0] - CHUNK)], small_rest_hbm)], first=5)
            for cp in stores:
                cp.wait()

    rev = lambda w: pl.BlockSpec((tm, w), lambda i: (n - 1 - i, 0))
    kwp = cwf.shape[0]
    small = jax.ShapeDtypeStruct((_small_rows(kwp), 2 * q), F32)
    small_ins = [wstt, cwf, tabs["fwd"], tabs["fwd_halo"], tabs["shift"], tabs["taps"], tabs["inv_in"], tabs["inv_taps"], vq, vd,
                 *mlp_small]
    return pl.pallas_call(
        body, name="bwd_mix", grid=(n,),
        in_specs=[rev(d), rev(saved_f32.shape[1]), rev(saved_bf16.shape[1]),
                  pl.BlockSpec((nb * 2 * DFT_F, q), lambda i: (n - 1 - i, 0)), ANY, ANY] + [_full(a.shape) for a in small_ins] + [ANY],
        out_specs=[rev(d)] + [ANY] * 6,
        out_shape=[jax.ShapeDtypeStruct((t, d), F32), jax.ShapeDtypeStruct(wi.shape, F32), jax.ShapeDtypeStruct(wo.shape, F32),
                   jax.ShapeDtypeStruct((CHUNK, 2 * q), F32), jax.ShapeDtypeStruct((small.shape[0] - CHUNK, 2 * q), F32),
                   jax.ShapeDtypeStruct(wi.shape, BF16), jax.ShapeDtypeStruct(wo.shape, BF16)],
        scratch_shapes=[pltpu.VMEM(wi.shape, BF16), pltpu.VMEM(wo.shape, BF16), pltpu.VMEM(wi.shape, F32), pltpu.VMEM(wo.shape, F32),
                        pltpu.VMEM((nb, DFT_N, q), F32), pltpu.VMEM((HALO, q), F32),
                        pltpu.VMEM((CHUNK, q), F32), pltpu.VMEM((2 * DFT_F, q), F32), pltpu.VMEM((2 * DFT_F, q), F32),
                        pltpu.VMEM(small.shape, F32), pltpu.SemaphoreType.DMA((6,))],
        compiler_params=_cparams(),
    )(dx1, saved_f32, saved_bf16, hf_s, wi, wo, *small_ins, token)


def _coords():
    return tuple(lax.axis_index(a) for a in MESH_AXES)


def _other_chips(x, y):
    return [(1 - x, y), (x, 1 - y), (1 - x, 1 - y)]


def _remote(src, dst, send_sem, recv_sem, to):
    return pltpu.make_async_remote_copy(src_ref=src, dst_ref=dst, send_sem=send_sem, recv_sem=recv_sem,
                                        device_id=to, device_id_type=MESH_ID)


class _Gather:
    def __init__(self, bufs, send_sems, recv_sems, own=None):
        self.bufs, self.send_sems, self.recv_sems, self.own = bufs, send_sems, recv_sems, own
        self.x, self.y, self.c = _coords()

    def _copies(self, stage):
        x, y, c = self.x, self.y, self.c
        for a, buf in enumerate(self.bufs):
            hr = buf.shape[1] // 2
            for j, chip in enumerate(_other_chips(x, y)):
                if stage == "ici_out":
                    ref, k, to = buf.at[2 * x + y, pl.ds(c * hr, hr)], j, (*chip, c)
                    if self.own:
                        yield _remote(self.own[a].at[pl.ds(c * hr, hr)], ref, self.send_sems.at[a, k], self.recv_sems.at[a, k], to)
                        continue
                elif stage == "ici_in":
                    ref, k, to = buf.at[2 * chip[0] + chip[1], pl.ds(c * hr, hr)], j, (*chip, c)
                elif stage == "d2d_out":
                    ref, k, to = buf.at[2 * chip[0] + chip[1], pl.ds(c * hr, hr)], 3 + j, (x, y, 1 - c)
                else:
                    ref, k, to = buf.at[2 * chip[0] + chip[1], pl.ds((1 - c) * hr, hr)], 3 + j, (x, y, 1 - c)
                yield _remote(ref, ref, self.send_sems.at[a, k], self.recv_sems.at[a, k], to)

    def start(self):
        for cp in self._copies("ici_out"):
            cp.start()

    def forward(self):
        for landed, onward in zip(self._copies("ici_in"), self._copies("d2d_out")):
            landed.wait_recv()
            onward.start()

    def finish(self):
        for cp in self._copies("d2d_in"):
            cp.wait_recv()
        for stage in ("ici_out", "d2d_out"):
            for cp in self._copies(stage):
                cp.wait_send()


def _gather_sems(n):
    return [pltpu.SemaphoreType.DMA((n, 6)), pltpu.SemaphoreType.DMA((n, 6))]


def _prep_gather(gathered, local, conv_w, kwp):
    kw, _, cshard = conv_w.shape
    shards = list(gathered) + list(local)
    n, n_g = len(shards), len(gathered)

    def body(*refs):
        src, cw_ref, out = refs[:n], refs[n], refs[n + 1:2 * n + 2]
        wide, narrow, taps = refs[2 * n + 2:3 * n + 2], refs[3 * n + 2:4 * n + 2], refs[4 * n + 2]
        load_sems, store_sems, send_sems, recv_sems = refs[4 * n + 3:]
        x, y, _ = _coords()
        loads = [pltpu.make_async_copy(src[a], wide[a], load_sems.at[a]) for a in range(n)]
        stores = [pltpu.make_async_copy(narrow[a], out[a].at[2 * x + y], store_sems.at[a]) for a in range(n)]
        stores.append(pltpu.make_async_copy(taps, out[n].at[2 * x + y], store_sems.at[n]))
        for cp in loads:
            cp.start()

        def cast(a):
            loads[a].wait()
            narrow[a][...] = wide[a][...].astype(BF16)
            stores[a].start()

        for a in range(n_g):
            cast(a)
        for tap in range(kw):
            taps[tap:tap + 1, :] = cw_ref[tap]
        taps[kw:kwp, :] = jnp.zeros((kwp - kw, cshard), F32)
        stores[n].start()
        g = _Gather(list(out[:n_g]) + [out[n]], send_sems, recv_sems, own=list(narrow[:n_g]) + [taps])
        g.start()
        for a in range(n_g, n):
            cast(a)
        g.forward()
        g.finish()
        for cp in stores:
            cp.wait()

    out_shape = [jax.ShapeDtypeStruct((4,) + a.shape, BF16) for a in shards] + [jax.ShapeDtypeStruct((4, kwp, cshard), F32)]
    scratch = ([pltpu.VMEM(a.shape, F32) for a in shards] + [pltpu.VMEM(a.shape, BF16) for a in shards] + [pltpu.VMEM((kwp, cshard), F32)]
               + [pltpu.SemaphoreType.DMA((n,)), pltpu.SemaphoreType.DMA((n + 1,))] + _gather_sems(n_g + 1))
    res = pl.pallas_call(
        body, name="prep_gather", in_specs=[ANY] * n + [pl.BlockSpec(memory_space=pltpu.VMEM)], out_specs=[ANY] * (n + 1),
        out_shape=out_shape, scratch_shapes=scratch, compiler_params=pltpu.CompilerParams(vmem_limit_bytes=VMEM_LIMIT),
    )(*shards, conv_w)
    return list(res[:n_g]), list(res[n_g:n]), res[n]


def _pair_reduce(name, partials, payloads, c_arr, out_dtypes):
    n = len(partials)
    counts = [g.shape[0] for g in partials]
    first = [sum(counts[:a]) for a in range(n)]
    steps = sum(counts)

    def body(c_ref, *refs):
        own, travelling, out, land = (refs[k * n:(k + 1) * n] for k in range(4))
        send_sems, recv_sems = refs[4 * n:]
        i = pl.program_id(0)
        x, y, c = _coords()

        def copy(a, q):
            return _remote(travelling[a].at[q, 1 - c], land[a].at[q], send_sems.at[first[a] + q], recv_sems.at[first[a] + q],
                           (x, y, 1 - c))

        blocks = [(a, q) for a in range(n) for q in range(counts[a])]

        @pl.when(i == 0)
        def _():
            for a, q in blocks:
                copy(a, q).start()

        for a in range(n):
            @pl.when((i >= first[a]) & (i < first[a] + counts[a]))
            def _(a=a):
                q = i - first[a]
                copy(a, q).wait_recv()
                out[a][...] = (own[a][...] + land[a][q].astype(F32)).astype(out_dtypes[a])

        @pl.when(i == steps - 1)
        def _():
            for a, q in blocks:
                copy(a, q).wait_send()

    at = lambda a, i: jnp.clip(i - first[a], 0, counts[a] - 1)
    in_specs = [pl.BlockSpec((None, None) + g.shape[2:], lambda i, cr, a=a: (at(a, i), cr[0], 0, 0)) for a, g in enumerate(partials)]
    out_specs = [pl.BlockSpec((None,) + g.shape[2:], lambda i, cr, a=a: (at(a, i), 0, 0)) for a, g in enumerate(partials)]
    grid_spec = pltpu.PrefetchScalarGridSpec(
        num_scalar_prefetch=1, grid=(steps,), in_specs=in_specs + [ANY] * n, out_specs=out_specs,
        scratch_shapes=[pltpu.VMEM((g.shape[0],) + g.shape[2:], p.dtype) for g, p in zip(partials, payloads)]
        + [pltpu.SemaphoreType.DMA((steps,)), pltpu.SemaphoreType.DMA((steps,))])
    out_shape = [jax.ShapeDtypeStruct((g.shape[0],) + g.shape[2:], dt) for g, dt in zip(partials, out_dtypes)]
    return list(pl.pallas_call(body, name=name, grid_spec=grid_spec, out_shape=out_shape, compiler_params=_cparams())(
        c_arr, *partials, *payloads))


class _Exchange:
    def __init__(self, src, dst, send_sems, recv_sems):
        self.src, self.dst, self.send_sems, self.recv_sems = src, dst, send_sems, recv_sems
        self.x, self.y, self.c = _coords()

    def _copies(self, incoming):
        x, y, c = self.x, self.y, self.c
        for a, (s, d) in enumerate(zip(self.src, self.dst)):
            for j, chip in enumerate(_other_chips(x, y)):
                slot = 2 * chip[0] + chip[1]
                if incoming:
                    out, into = d.at[slot], d.at[slot]
                else:
                    out, into = (s.at[slot] if len(s.shape) == 3 else s), d.at[2 * x + y]
                yield _remote(out, into, self.send_sems.at[a, j], self.recv_sems.at[a, j], (*chip, c))

    def start(self):
        for cp in self._copies(False):
            cp.start()

    def finish(self):
        for cp in self._copies(True):
            cp.wait_recv()
        for cp in self._copies(False):
            cp.wait_send()


def _exchange_shapes(arrs):
    return [jax.ShapeDtypeStruct((4,) + s.shape[-2:], s.dtype) for s in arrs]


class _FlatSems:
    def __init__(self, ref):
        self.ref = ref

    @property
    def at(self):
        return self

    def __getitem__(self, idx):
        return self.ref.at[3 * idx[0] + idx[1]]


HBM = pl.BlockSpec(memory_space=pltpu.HBM)
SEM = pl.BlockSpec(memory_space=pltpu.SEMAPHORE)
DATAFLOW = pltpu.SideEffectType.DATAFLOW_SIDE_EFFECTING


def _exchange_start(name, arrs):
    n = len(arrs)
    lands = _exchange_shapes(arrs)

    def body(*refs):
        src, land = refs[:n], refs[n:2 * n]
        send_sems, recv_sems = refs[2 * n:2 * n + 2]
        token = refs[-1]
        _Exchange(src, land, _FlatSems(send_sems), _FlatSems(recv_sems)).start()
        token[...] = jnp.zeros_like(token)

    hbm = lambda a: pltpu.with_memory_space_constraint(a, pltpu.HBM)
    outs = pl.pallas_call(
        body, name=name,
        out_shape=(pltpu.SemaphoreType.DMA((3 * n,)), pltpu.SemaphoreType.DMA((3 * n,)),
                   *[pltpu.HBM(a.shape, a.dtype) for a in arrs], *[pltpu.HBM(s.shape, s.dtype) for s in lands],
                   jax.ShapeDtypeStruct((SUBLANES, LANES), F32)),
        in_specs=[HBM] * (2 * n), out_specs=(SEM, SEM, *[HBM] * (2 * n), pl.BlockSpec(memory_space=pltpu.VMEM)),
        input_output_aliases={a: 2 + a for a in range(2 * n)},
        compiler_params=pltpu.CompilerParams(has_side_effects=DATAFLOW),
    )(*[hbm(a) for a in arrs], *[hbm(lax.empty(s.shape, s.dtype)) for s in lands])
    return outs[:-1], outs[-1]


def _exchange_wait(name, started, after):
    send_sems, recv_sems, *bufs = started
    n = len(bufs) // 2

    def body(*refs):
        src, land = refs[:n], refs[n:2 * n]
        send_sems, recv_sems = refs[2 * n:2 * n + 2]
        _Exchange(src, land, _FlatSems(send_sems), _FlatSems(recv_sems)).finish()

    outs = pl.pallas_call(
        body, name=name,
        out_shape=tuple(pltpu.HBM(b.shape, b.dtype) for b in bufs),
        in_specs=[HBM] * (2 * n) + [SEM, SEM] + [ANY] * len(after), out_specs=tuple([HBM] * (2 * n)),
        input_output_aliases={a: a for a in range(2 * n)},
        compiler_params=pltpu.CompilerParams(has_side_effects=DATAFLOW),
    )(*bufs, send_sems, recv_sems, *after)
    return list(outs[:n]), list(outs[n:])


def _chip_reduce(name, parts, owns, after):
    n = len(parts)

    def body(*refs):
        part, own = refs[:n], refs[n:2 * n]
        mine, other = refs[2 * n + len(after):3 * n + len(after)], refs[3 * n + len(after):4 * n + len(after)]
        scratch = refs[4 * n + len(after):]
        landed, own_part, total = scratch[:n], scratch[n:2 * n], scratch[2 * n:3 * n]
        load_sems, store_sems, send_sems, recv_sems = scratch[3 * n:]
        x, y, c = _coords()
        me = 2 * x + y
        loads, stores, sends = [], [], []
        for a in range(n):
            loads.append((pltpu.make_async_copy(part[a], landed[a], load_sems.at[a, 0]),
                          pltpu.make_async_copy(own[a].at[me] if len(own[a].shape) == 3 else own[a], own_part[a], load_sems.at[a, 1])))
            stores.append(pltpu.make_async_copy(total[a], mine[a], store_sems.at[a]))
            sends.append(_remote(total[a], other[a], send_sems.at[a], recv_sems.at[a], (x, y, 1 - c)))
        for both in loads:
            for cp in both:
                cp.start()
        for a in range(n):
            for cp in loads[a]:
                cp.wait()
            for chip in range(4):
                @pl.when(me == chip)
                def _(a=a, chip=chip):
                    term = lambda j: (own_part[a] if j == chip else landed[a].at[j])[...].astype(F32)
                    total[a][...] = ((term(0) + term(1)) + term(2)) + term(3)
            stores[a].start()
            sends[a].start()
        for a in range(n):
            stores[a].wait()
            sends[a].wait()

    halves = [jax.ShapeDtypeStruct(p.shape[1:], F32) for p in parts]
    scratch = ([pltpu.VMEM(p.shape, p.dtype) for p in parts] + [pltpu.VMEM(p.shape[1:], o.dtype) for p, o in zip(parts, owns)]
               + [pltpu.VMEM(p.shape[1:], F32) for p in parts]
               + [pltpu.SemaphoreType.DMA((n, 2))] + [pltpu.SemaphoreType.DMA((n,))] * 3)
    res = pl.pallas_call(
        body, name=name, in_specs=[ANY] * (2 * n + len(after)), out_specs=[ANY] * (2 * n), out_shape=halves + halves,
        scratch_shapes=scratch, compiler_params=pltpu.CompilerParams(vmem_limit_bytes=VMEM_LIMIT),
    )(*parts, *owns, *after)
    return list(res[:n]), list(res[n:])


def _row_block(rows, cols, limit=1 << 20):
    best = 8
    for tr in range(8, rows + 1, 8):
        if rows % tr == 0 and tr * cols * 4 <= limit:
            best = tr
    return best


def _adamw(name, w, g_mine, g_other, m, v, c_arr):
    r, c = w.shape
    hr, cg = g_mine.shape
    tr = hr if r % hr == 0 and hr * cg * 4 <= (3 << 19) else math.gcd(_row_block(hr, cg), r)
    per_half = hr // tr
    bc1 = 1.0 - ADAM_B1 ** ADAM_STEP
    bc2 = 1.0 - ADAM_B2 ** ADAM_STEP

    def body(c_ref, w_ref, gm_ref, go_ref, m_ref, v_ref, go, do, mo, vo):
        gv = jnp.where(pl.program_id(0) // per_half == c_ref[0], gm_ref[:, 0:c], go_ref[:, 0:c])
        mn = ADAM_B1 * m_ref[...] + (1.0 - ADAM_B1) * gv
        vn = ADAM_B2 * v_ref[...] + (1.0 - ADAM_B2) * (gv * gv)
        go[...] = gv
        mo[...] = mn
        vo[...] = vn
        do[...] = -ADAM_LR * ((mn / bc1) / (jnp.sqrt(vn / bc2) + ADAM_EPS) + ADAM_WD * w_ref[...])

    blk = pl.BlockSpec((tr, c), lambda i, cr: (i, 0))
    gblk = pl.BlockSpec((tr, cg), lambda i, cr: (i % per_half, 0))
    grid_spec = pltpu.PrefetchScalarGridSpec(num_scalar_prefetch=1, grid=(r // tr,), in_specs=[blk, gblk, gblk, blk, blk],
                                             out_specs=[blk] * 4)
    return pl.pallas_call(body, name=f"adamw_{name}", grid_spec=grid_spec, out_shape=[jax.ShapeDtypeStruct((r, c), F32)] * 4,
                          compiler_params=_cparams())(c_arr, w, g_mine, g_other, m, v)


SMALL_Q = ("sgu_ln_g", "sgu_ln_b", "conv_b", "conv_ln_g", "conv_ln_b")
SMALL_D = ("ln1_g", "ln1_b", "ln2_g", "ln2_b")


def _adamw_small(g_mine, g_other, c_arr, me_arr, params):
    names = list(SMALL_Q) + list(SMALL_D) + ["w_s", "b_s", "conv_w"]
    pieces = len(g_mine)
    q = g_mine[0].shape[1] // 2
    heads = params["w_s"][0].shape[1]
    kw, _, cshard = params["conv_w"][0].shape
    bc1 = 1.0 - ADAM_B1 ** ADAM_STEP
    bc2 = 1.0 - ADAM_B2 ** ADAM_STEP

    def update(w, g, m, v):
        mn = ADAM_B1 * m + (1.0 - ADAM_B1) * g
        vn = ADAM_B2 * v + (1.0 - ADAM_B2) * (g * g)
        return g, -ADAM_LR * ((mn / bc1) / (jnp.sqrt(vn / bc2) + ADAM_EPS) + ADAM_WD * w), mn, vn

    def body(c_ref, me_ref, *refs):
        mine, other, refs = refs[:pieces], refs[pieces:2 * pieces], refs[2 * pieces:]
        ins = {nm: refs[3 * k:3 * k + 3] for k, nm in enumerate(names)}
        outs = {nm: refs[3 * len(names) + 4 * k:3 * len(names) + 4 * k + 4] for k, nm in enumerate(names)}
        loss_ref, cw_ref = refs[-2:]
        low = c_ref[0] == 0
        in_order = lambda first, second: [jnp.where(low, first, second), jnp.where(low, second, first)]
        g_all = jnp.concatenate([half for a in range(pieces) for half in in_order(mine[a][...], other[a][...])], axis=0)

        def apply(nm, g, at):
            w, m, v = (r[at] for r in ins[nm])
            for o, val in zip(outs[nm], update(w, g, m, v)):
                o[at] = val

        for row, nm in enumerate(SMALL_Q):
            apply(nm, g_all[SMALL_VQ + row:SMALL_VQ + row + 1, q:2 * q], ...)
        for row, nm in enumerate(SMALL_D):
            apply(nm, g_all[SMALL_VD + row:SMALL_VD + row + 1, :], ...)
        for h in range(heads):
            apply("w_s", g_all[0:CHUNK, h * CHUNK:(h + 1) * CHUNK], (0, h))
        apply("b_s", g_all[SMALL_BS:SMALL_BS + heads, q:q + LANES], 0)
        cw_ref[...] = jnp.zeros_like(cw_ref)
        for chip in range(4):
            @pl.when(me_ref[0] == chip)
            def _():
                cw_ref[...] = g_all[SMALL_CW:SMALL_CW + cw_ref.shape[0], chip * cshard:(chip + 1) * cshard]
        for tap in range(kw):
            apply("conv_w", cw_ref[tap:tap + 1, :], tap)
        loss_ref[...] = g_all[SMALL_LOSS:SMALL_LOSS + 1, q:q + 1]

    arrays = [a for nm in names for a in params[nm]]
    out_shape = [jax.ShapeDtypeStruct(params[nm][0].shape, F32) for nm in names for _ in range(4)] + [jax.ShapeDtypeStruct((1, 1), F32)]
    whole = lambda shape: pl.BlockSpec(shape, lambda i, c, me: (0,) * len(shape))
    grid_spec = pltpu.PrefetchScalarGridSpec(
        num_scalar_prefetch=2, grid=(1,),
        in_specs=[whole(a.shape) for a in [*g_mine, *g_other, *arrays]],
        out_specs=[whole(s.shape) for s in out_shape],
        scratch_shapes=[pltpu.VMEM((-(-kw // SUBLANES) * SUBLANES, cshard), F32)])
    res = pl.pallas_call(body, name="adamw_small", grid_spec=grid_spec, out_shape=out_shape, compiler_params=_cparams())(
        c_arr, me_arr, *g_mine, *g_other, *arrays)
    return {nm: list(res[4 * k:4 * k + 4]) for k, nm in enumerate(names)}, res[-1]


def _pad_rows(a, rows):
    return jnp.pad(a, ((0, rows - a.shape[0]), (0, 0)))


def kernel(x, w_in, sgu_ln_g, sgu_ln_b, w_s, b_s, conv_w, conv_b, conv_ln_g, conv_ln_b, w_out, ln1_g, ln1_b, w_gate, w_up, w_down, ln2_g, ln2_b, loss_target, m_w_in, m_sgu_ln_g, m_sgu_ln_b, m_w_s, m_b_s, m_conv_w, m_conv_b, m_conv_ln_g, m_conv_ln_b, m_w_out, m_ln1_g, m_ln1_b, m_w_gate, m_w_up, m_w_down, m_ln2_g, m_ln2_b, v_w_in, v_sgu_ln_g, v_sgu_ln_b, v_w_s, v_b_s, v_conv_w, v_conv_b, v_conv_ln_g, v_conv_ln_b, v_w_out, v_ln1_g, v_ln1_b, v_w_gate, v_w_up, v_w_down, v_ln2_g, v_ln2_b):
    depth, d, q = w_in.shape
    assert depth == 1 and x.shape[0] == 1
    t = x.shape[1]
    heads = w_s.shape[1]
    kw, cshard = conv_w.shape[1], conv_w.shape[2]
    fs = w_gate.shape[2]
    slabs = _hidden_slabs(4 * fs)
    n_pairs = q // LANES
    assert heads * HEAD_DIM == q and q % LANES == 0 and w_s.shape[2] == CHUNK and 4 * cshard == q and kw - 1 <= HALO
    alpha = (2.0 * depth) ** 0.25
    tm = min(512, t)
    assert t % tm == 0 and tm % CHUNK == 0
    x2, tgt = x[0], loss_target[0]
    mx, my, mc = _coords()
    me = 2 * mx + my
    c_arr = jnp.reshape(mc, (1,)).astype(jnp.int32)

    kwp = -(-kw // 16) * 16
    me_arr = jnp.reshape(me, (1,)).astype(jnp.int32)
    (wi, wo), (wg, wu, wd), cw4 = _prep_gather([w_in[0], w_out[0]], [w_gate[0].T, w_up[0].T, w_down[0]],
                                              jnp.transpose(conv_w, (1, 0, 2)), kwp)
    wo = wo.reshape(d, d)
    cw = jnp.transpose(cw4, (1, 0, 2)).reshape(kwp, q)
    cwf = _pad_rows(cw[:kw][::-1], kwp)
    tabs = {name: jnp.asarray(tab) for name, tab in _dft_tables(kw, kwp, q).items()}

    wm = jnp.where(jnp.tril(jnp.ones((CHUNK, CHUNK), bool)), w_s[0], 0.0)
    wst = wm.reshape(n_pairs, 2 * CHUNK, CHUNK).astype(BF16)
    wstt = jnp.transpose(wm, (0, 2, 1)).reshape(n_pairs, 2 * CHUNK, CHUNK).astype(BF16)
    bmat = jnp.repeat(b_s[0].T, HEAD_DIM, axis=1)
    vq = _pad_rows(jnp.concatenate([sgu_ln_g, sgu_ln_b, conv_b, conv_ln_g, conv_ln_b], axis=0), 8)
    vd = _pad_rows(jnp.concatenate([ln1_g, ln1_b, ln2_g, ln2_b], axis=0), 8)

    *saved, wg, wu, wd = _fwd_mix(x2, wi, wo, wst, bmat, cwf, tabs, vq, vd, [wg, wu, wd], alpha, tm)
    wg, wu, wd = (w.reshape(4 * fs, d) for w in (wg, wu, wd))
    *acts, x1b, dr2, loss_part, dg2, db2 = _fwd_mlp(saved[0], tgt, wg, wu, wd, vd, alpha, slabs, tm)
    mlp_grads = None
    for j, slab in enumerate(slabs):
        mlp_grads = _bwd_mlp_slab(j, slab, dr2, mlp_grads, x1b, acts[j], wg, wu, wd, alpha, tm)
    dx1 = mlp_grads[0]
    mlp_halves = [b.reshape(4, 2, fs // 2, d) for b in mlp_grads[1:]]
    mlp_sums = _pair_reduce("pair_reduce_mlp", mlp_halves[3:], mlp_halves[3:], c_arr, [BF16] * 3)
    mlp_started, token = _exchange_start("exchange_mlp_start", mlp_sums)
    grad_x, dwi, dwo, *small, dwi16, dwo16 = _bwd_mix(dx1, saved, wi, wo, wstt, cwf, tabs, vq, vd, (loss_part, dg2, db2), token, alpha, tm)
    mlp_sums, mlp_parts = _exchange_wait("exchange_mlp_wait", mlp_started, [dwo])

    by_halves = lambda b: b.reshape(4, 2, b.shape[1] // 2, b.shape[2])
    halves = [by_halves(dwi16), by_halves(dwo16.reshape(4, d // 4, d))] + [s.reshape(1, 2, s.shape[0] // 2, s.shape[1]) for s in small]
    travelling = halves
    *sums, ws_sum, rest_sum = _pair_reduce("pair_reduce_mix", halves, travelling, c_arr, [BF16, BF16, BF16, F32])
    sums += [ws_sum[0], rest_sum[0]]
    mix_started, token = _exchange_start("exchange_mix_start", sums)

    out, raw = {}, {}

    def finish(first, names, parts, sums, after):
        mine, other = _chip_reduce(f"chip_reduce_{first}", parts, sums, [after])
        for a, nm in enumerate(names):
            w_, m_, v_ = weights[nm]
            if nm in ("w_gate", "w_up"):
                raw[nm] = _adamw(nm, w_[0].T, mine[a], other[a], m_[0].T, v_[0].T, c_arr)
                out[nm] = [o.T for o in raw[nm]]
            else:
                raw[nm] = out[nm] = _adamw(nm, w_[0], mine[a], other[a], m_[0], v_[0], c_arr)
        return mine[len(names):], other[len(names):]

    weights = {"w_in": (w_in, m_w_in, v_w_in), "w_out": (w_out, m_w_out, v_w_out), "w_gate": (w_gate, m_w_gate, v_w_gate),
               "w_up": (w_up, m_w_up, v_w_up), "w_down": (w_down, m_w_down, v_w_down)}
    finish(2, ["w_gate", "w_up", "w_down"], mlp_parts, mlp_sums, token)
    sums, parts = _exchange_wait("exchange_mix_wait", mix_started, [raw[nm][1] for nm in ("w_gate", "w_up", "w_down")])
    small_mine, small_other = finish(5, ["w_in", "w_out"], parts, sums, parts[0])

    small_params = {
        "sgu_ln_g": (sgu_ln_g, m_sgu_ln_g, v_sgu_ln_g), "sgu_ln_b": (sgu_ln_b, m_sgu_ln_b, v_sgu_ln_b),
        "conv_b": (conv_b, m_conv_b, v_conv_b), "conv_ln_g": (conv_ln_g, m_conv_ln_g, v_conv_ln_g),
        "conv_ln_b": (conv_ln_b, m_conv_ln_b, v_conv_ln_b), "ln1_g": (ln1_g, m_ln1_g, v_ln1_g), "ln1_b": (ln1_b, m_ln1_b, v_ln1_b),
        "ln2_g": (ln2_g, m_ln2_g, v_ln2_g), "ln2_b": (ln2_b, m_ln2_b, v_ln2_b), "w_s": (w_s, m_w_s, v_w_s),
        "b_s": (b_s, m_b_s, v_b_s), "conv_w": tuple(jnp.transpose(a, (1, 0, 2)) for a in (conv_w, m_conv_w, v_conv_w))}
    small_out, loss_block = _adamw_small(small_mine, small_other, c_arr, me_arr, small_params)
    small_out["conv_w"] = [jnp.transpose(o, (1, 0, 2)) for o in small_out["conv_w"]]
    loss = loss_block.reshape(())
    names = ["w_in", "sgu_ln_g", "sgu_ln_b", "w_s", "b_s", "conv_w", "conv_b", "conv_ln_g", "conv_ln_b", "w_out",
             "ln1_g", "ln1_b", "w_gate", "w_up", "w_down", "ln2_g", "ln2_b"]
    result = [loss, grad_x[None]]
    for kind in range(4):
        for nm in names:
            result.append(out[nm][kind][None] if nm in out else small_out[nm][kind])
    return tuple(result)
```
